```python
import jax, jax.numpy as jnp
from jax import lax
import numpy as np

D_MODEL = 1024
BATCH = 32
SEQ = 2048
DEPTH = 1

D_RNN = 1280
N_RNN_BLOCKS = 10
RNN_BLOCK = D_RNN // N_RNN_BLOCKS
RNN_CONV_WIDTH = 4
LRU_C = 8.0
N_HEADS = 8
HEAD_DIM = 128
D_ATTN = N_HEADS * HEAD_DIM
Q_BLOCK = 128
D_FF = 3 * D_MODEL
FFN_CONV_WIDTH = 3
RMS_EPS = 1e-6
SPLIT_SIZES = (D_RNN, D_RNN, D_ATTN, D_ATTN, D_ATTN, N_HEADS, 2 * D_MODEL)
D_IN = sum(SPLIT_SIZES)
N_MOD = 6

kernel_name = "hybrid_rglru_fox_convffn_adaln"


def rmsnorm(x, g):
    xf = x.astype(jnp.float32)
    y = xf * lax.rsqrt(jnp.mean(xf * xf, axis=-1, keepdims=True) + RMS_EPS)
    return (y * g.astype(jnp.float32)).astype(x.dtype)


def modulate(h, shift, scale):
    return h * (1 + scale[:, None, :]) + shift[:, None, :]


def causal_dwconv(x, w, b):
    K = w.shape[0]
    S = x.shape[1]
    xp = jnp.pad(x, ((0, 0), (K - 1, 0), (0, 0)))
    y = b
    for k in range(K):
        y = y + xp[:, k:k + S, :] * w[k]
    return y


def block_diag_linear(x, w, b):
    B, S, _ = x.shape
    xb = x.reshape(B, S, N_RNN_BLOCKS, RNN_BLOCK)
    return jnp.einsum('bsnc,ncd->bsnd', xb, w).reshape(B, S, D_RNN) + b


def rg_lru(x, w_a, b_a, w_i, b_i, lam):
    r = jax.nn.sigmoid(block_diag_linear(x, w_a, b_a)).astype(jnp.float32)
    i = jax.nn.sigmoid(block_diag_linear(x, w_i, b_i))
    log_a = -LRU_C * r * jax.nn.softplus(-lam.astype(jnp.float32))
    a = jnp.exp(log_a)
    mult = jnp.sqrt(-jnp.expm1(2.0 * log_a))
    u = mult * (i * x).astype(jnp.float32)

    def combine(left, right):
        a_l, b_l = left
        a_r, b_r = right
        return a_l * a_r, a_r * b_l + b_r

    _, h = lax.associative_scan(combine, (a, u), axis=1)
    return h.astype(x.dtype)


def forgetting_attention(q, k, v, log_f):
    B, S, H, Dh = q.shape
    F = jnp.cumsum(log_f, axis=1).transpose(0, 2, 1)
    qh = (q * (Dh ** -0.5)).transpose(0, 2, 1, 3)
    kh = k.transpose(0, 2, 1, 3)
    vh = v.transpose(0, 2, 1, 3)
    outs = []
    for blk in range(S // Q_BLOCK):
        q0 = blk * Q_BLOCK
        q1 = q0 + Q_BLOCK
        s = jnp.einsum('bhqd,bhkd->bhqk', qh[:, :, q0:q1], kh[:, :, :q1],
                       preferred_element_type=jnp.float32)
        s = s + F[:, :, q0:q1, None] - F[:, :, None, :q1]
        mask = (q0 + jnp.arange(Q_BLOCK))[:, None] >= jnp.arange(q1)[None, :]
        s = jnp.where(mask, s, -jnp.inf)
        p = jax.nn.softmax(s, axis=-1)
        outs.append(jnp.einsum('bhqk,bhkd->bhqd', p.astype(vh.dtype), vh[:, :, :q1]))
    o = jnp.concatenate(outs, axis=2)
    return o.transpose(0, 2, 1, 3).reshape(B, S, H * Dh)


def _fwd_setup_inputs(seed: int = 0) -> dict:
    key = jax.random.key(seed)
    ks = iter(jax.random.split(key, 40))
    nrm = lambda shape, s: jax.random.normal(next(ks), shape, jnp.float32) * s
    L = DEPTH
    a0 = jax.random.uniform(next(ks), (L, D_RNN), jnp.float32, 0.9, 0.999)
    return {
        "x": nrm((BATCH, SEQ, D_MODEL), 1.0),
        "c": nrm((BATCH, D_MODEL), 1.0),
        "w_ada": nrm((L, D_MODEL, N_MOD * D_MODEL), D_MODEL ** -0.5),
        "b_ada": nrm((L, N_MOD * D_MODEL), 0.01),
        "g_norm1": 1.0 + nrm((L, D_MODEL), 0.02),
        "w_in": nrm((L, D_MODEL, D_IN), D_MODEL ** -0.5),
        "w_rnn_conv": nrm((L, RNN_CONV_WIDTH, D_RNN), RNN_CONV_WIDTH ** -0.5),
        "b_rnn_conv": nrm((L, D_RNN), 0.01),
        "w_lru_a": nrm((L, N_RNN_BLOCKS, RNN_BLOCK, RNN_BLOCK), RNN_BLOCK ** -0.5),
        "b_lru_a": nrm((L, D_RNN), 0.01),
        "w_lru_i": nrm((L, N_RNN_BLOCKS, RNN_BLOCK, RNN_BLOCK), RNN_BLOCK ** -0.5),
        "b_lru_i": nrm((L, D_RNN), 0.01),
        "lru_lambda": jnp.log(a0) - jnp.log1p(-a0),
        "b_fgate": 3.0 + nrm((L, N_HEADS), 0.1),
        "w_proj_rnn": nrm((L, D_RNN, D_MODEL), D_RNN ** -0.5),
        "w_proj_attn": nrm((L, D_ATTN, D_MODEL), D_ATTN ** -0.5),
        "w_out": nrm((L, D_MODEL, D_MODEL), D_MODEL ** -0.5),
        "g_norm2": 1.0 + nrm((L, D_MODEL), 0.02),
        "w_ffn_up": nrm((L, D_MODEL, 2 * D_FF), D_MODEL ** -0.5),
        "w_ffn_conv": nrm((L, FFN_CONV_WIDTH, D_FF), FFN_CONV_WIDTH ** -0.5),
        "b_ffn_conv": nrm((L, D_FF), 0.01),
        "w_ffn_down": nrm((L, D_FF, D_MODEL), D_FF ** -0.5),
        "w_ada_final": nrm((D_MODEL, 2 * D_MODEL), D_MODEL ** -0.5),
        "b_ada_final": nrm((2 * D_MODEL,), 0.01),
        "g_final": 1.0 + nrm((D_MODEL,), 0.02),
    }


def _fwd_reference(x, c, w_ada, b_ada, g_norm1, w_in, w_rnn_conv, b_rnn_conv, w_lru_a, b_lru_a,
              w_lru_i, b_lru_i, lru_lambda, b_fgate, w_proj_rnn, w_proj_attn, w_out, g_norm2,
              w_ffn_up, w_ffn_conv, b_ffn_conv, w_ffn_down, w_ada_final, b_ada_final, g_final):
    B, S, _ = x.shape
    c_act = jax.nn.silu(c)
    split_idx = [int(v) for v in np.cumsum(SPLIT_SIZES)[:-1]]
    for l in range(DEPTH):
        mod = c_act @ w_ada[l] + b_ada[l]
        shift1, scale1, gate1, shift2, scale2, gate2 = jnp.split(mod, N_MOD, axis=-1)

        h = modulate(rmsnorm(x, g_norm1[l]), shift1, scale1)
        z = h @ w_in[l]
        xr, gr, q, k, v, fl, mg = jnp.split(z, split_idx, axis=-1)

        xr = causal_dwconv(xr, w_rnn_conv[l], b_rnn_conv[l])
        hr = rg_lru(xr, w_lru_a[l], b_lru_a[l], w_lru_i[l], b_lru_i[l], lru_lambda[l])
        y_rnn = jax.nn.gelu(gr, approximate=True) * hr

        log_f = jax.nn.log_sigmoid(fl.astype(jnp.float32) + b_fgate[l].astype(jnp.float32))
        y_attn = forgetting_attention(q.reshape(B, S, N_HEADS, HEAD_DIM),
                                      k.reshape(B, S, N_HEADS, HEAD_DIM),
                                      v.reshape(B, S, N_HEADS, HEAD_DIM), log_f)

        g_r, g_a = jnp.split(jax.nn.sigmoid(mg), 2, axis=-1)
        merged = g_r * (y_rnn @ w_proj_rnn[l]) + g_a * (y_attn @ w_proj_attn[l])
        x = x + gate1[:, None, :] * (merged @ w_out[l])

        h = modulate(rmsnorm(x, g_norm2[l]), shift2, scale2)
        gf, uf = jnp.split(h @ w_ffn_up[l], 2, axis=-1)
        gf = causal_dwconv(gf, w_ffn_conv[l], b_ffn_conv[l])
        y_ffn = (jax.nn.gelu(gf, approximate=True) * uf) @ w_ffn_down[l]
        x = x + gate2[:, None, :] * y_ffn

    shift_f, scale_f = jnp.split(c_act @ w_ada_final + b_ada_final, 2, axis=-1)
    return modulate(rmsnorm(x, g_final), shift_f, scale_f)


import jax as _jax
import jax.numpy as _jnp

TWIN_FORMAT = 'train_step'
FWD_PARAMS = ['x', 'c', 'w_ada', 'b_ada', 'g_norm1', 'w_in', 'w_rnn_conv', 'b_rnn_conv', 'w_lru_a', 'b_lru_a', 'w_lru_i', 'b_lru_i', 'lru_lambda', 'b_fgate', 'w_proj_rnn', 'w_proj_attn', 'w_out', 'g_norm2', 'w_ffn_up', 'w_ffn_conv', 'b_ffn_conv', 'w_ffn_down', 'w_ada_final', 'b_ada_final', 'g_final']
TWIN_WEIGHTS = ['w_ada', 'b_ada', 'g_norm1', 'w_in', 'w_rnn_conv', 'b_rnn_conv', 'w_lru_a', 'b_lru_a', 'w_lru_i', 'b_lru_i', 'lru_lambda', 'b_fgate', 'w_proj_rnn', 'w_proj_attn', 'w_out', 'g_norm2', 'w_ffn_up', 'w_ffn_conv', 'b_ffn_conv', 'w_ffn_down', 'w_ada_final', 'b_ada_final', 'g_final']
TWIN_DIFF_INPUT = 'x'
TWIN_INPUTS = ['x', 'c', 'w_ada', 'b_ada', 'g_norm1', 'w_in', 'w_rnn_conv', 'b_rnn_conv', 'w_lru_a', 'b_lru_a', 'w_lru_i', 'b_lru_i', 'lru_lambda', 'b_fgate', 'w_proj_rnn', 'w_proj_attn', 'w_out', 'g_norm2', 'w_ffn_up', 'w_ffn_conv', 'b_ffn_conv', 'w_ffn_down', 'w_ada_final', 'b_ada_final', 'g_final', 'loss_target', 'm_w_ada', 'm_b_ada', 'm_g_norm1', 'm_w_in', 'm_w_rnn_conv', 'm_b_rnn_conv', 'm_w_lru_a', 'm_b_lru_a', 'm_w_lru_i', 'm_b_lru_i', 'm_lru_lambda', 'm_b_fgate', 'm_w_proj_rnn', 'm_w_proj_attn', 'm_w_out', 'm_g_norm2', 'm_w_ffn_up', 'm_w_ffn_conv', 'm_b_ffn_conv', 'm_w_ffn_down', 'm_w_ada_final', 'm_b_ada_final', 'm_g_final', 'v_w_ada', 'v_b_ada', 'v_g_norm1', 'v_w_in', 'v_w_rnn_conv', 'v_b_rnn_conv', 'v_w_lru_a', 'v_b_lru_a', 'v_w_lru_i', 'v_b_lru_i', 'v_lru_lambda', 'v_b_fgate', 'v_w_proj_rnn', 'v_w_proj_attn', 'v_w_out', 'v_g_norm2', 'v_w_ffn_up', 'v_w_ffn_conv', 'v_b_ffn_conv', 'v_w_ffn_down', 'v_w_ada_final', 'v_b_ada_final', 'v_g_final']
TWIN_OUTPUTS = ['loss', 'grad_x', 'grad_w_ada', 'grad_b_ada', 'grad_g_norm1', 'grad_w_in', 'grad_w_rnn_conv', 'grad_b_rnn_conv', 'grad_w_lru_a', 'grad_b_lru_a', 'grad_w_lru_i', 'grad_b_lru_i', 'grad_lru_lambda', 'grad_b_fgate', 'grad_w_proj_rnn', 'grad_w_proj_attn', 'grad_w_out', 'grad_g_norm2', 'grad_w_ffn_up', 'grad_w_ffn_conv', 'grad_b_ffn_conv', 'grad_w_ffn_down', 'grad_w_ada_final', 'grad_b_ada_final', 'grad_g_final', 'delta_w_ada', 'delta_b_ada', 'delta_g_norm1', 'delta_w_in', 'delta_w_rnn_conv', 'delta_b_rnn_conv', 'delta_w_lru_a', 'delta_b_lru_a', 'delta_w_lru_i', 'delta_b_lru_i', 'delta_lru_lambda', 'delta_b_fgate', 'delta_w_proj_rnn', 'delta_w_proj_attn', 'delta_w_out', 'delta_g_norm2', 'delta_w_ffn_up', 'delta_w_ffn_conv', 'delta_b_ffn_conv', 'delta_w_ffn_down', 'delta_w_ada_final', 'delta_b_ada_final', 'delta_g_final', 'new_m_w_ada', 'new_m_b_ada', 'new_m_g_norm1', 'new_m_w_in', 'new_m_w_rnn_conv', 'new_m_b_rnn_conv', 'new_m_w_lru_a', 'new_m_b_lru_a', 'new_m_w_lru_i', 'new_m_b_lru_i', 'new_m_lru_lambda', 'new_m_b_fgate', 'new_m_w_proj_rnn', 'new_m_w_proj_attn', 'new_m_w_out', 'new_m_g_norm2', 'new_m_w_ffn_up', 'new_m_w_ffn_conv', 'new_m_b_ffn_conv', 'new_m_w_ffn_down', 'new_m_w_ada_final', 'new_m_b_ada_final', 'new_m_g_final', 'new_v_w_ada', 'new_v_b_ada', 'new_v_g_norm1', 'new_v_w_in', 'new_v_w_rnn_conv', 'new_v_b_rnn_conv', 'new_v_w_lru_a', 'new_v_b_lru_a', 'new_v_w_lru_i', 'new_v_b_lru_i', 'new_v_lru_lambda', 'new_v_b_fgate', 'new_v_w_proj_rnn', 'new_v_w_proj_attn', 'new_v_w_out', 'new_v_g_norm2', 'new_v_w_ffn_up', 'new_v_w_ffn_conv', 'new_v_b_ffn_conv', 'new_v_w_ffn_down', 'new_v_w_ada_final', 'new_v_b_ada_final', 'new_v_g_final']
TWIN_LEAF_KINDS = {'loss': 'loss', 'grad_x': 'grad_x', 'grad_w_ada': 'grad_w', 'grad_b_ada': 'grad_w', 'grad_g_norm1': 'grad_w', 'grad_w_in': 'grad_w', 'grad_w_rnn_conv': 'grad_w', 'grad_b_rnn_conv': 'grad_w', 'grad_w_lru_a': 'grad_w', 'grad_b_lru_a': 'grad_w', 'grad_w_lru_i': 'grad_w', 'grad_b_lru_i': 'grad_w', 'grad_lru_lambda': 'grad_w', 'grad_b_fgate': 'grad_w', 'grad_w_proj_rnn': 'grad_w', 'grad_w_proj_attn': 'grad_w', 'grad_w_out': 'grad_w', 'grad_g_norm2': 'grad_w', 'grad_w_ffn_up': 'grad_w', 'grad_w_ffn_conv': 'grad_w', 'grad_b_ffn_conv': 'grad_w', 'grad_w_ffn_down': 'grad_w', 'grad_w_ada_final': 'grad_w', 'grad_b_ada_final': 'grad_w', 'grad_g_final': 'grad_w', 'delta_w_ada': 'delta_w', 'delta_b_ada': 'delta_w', 'delta_g_norm1': 'delta_w', 'delta_w_in': 'delta_w', 'delta_w_rnn_conv': 'delta_w', 'delta_b_rnn_conv': 'delta_w', 'delta_w_lru_a': 'delta_w', 'delta_b_lru_a': 'delta_w', 'delta_w_lru_i': 'delta_w', 'delta_b_lru_i': 'delta_w', 'delta_lru_lambda': 'delta_w', 'delta_b_fgate': 'delta_w', 'delta_w_proj_rnn': 'delta_w', 'delta_w_proj_attn': 'delta_w', 'delta_w_out': 'delta_w', 'delta_g_norm2': 'delta_w', 'delta_w_ffn_up': 'delta_w', 'delta_w_ffn_conv': 'delta_w', 'delta_b_ffn_conv': 'delta_w', 'delta_w_ffn_down': 'delta_w', 'delta_w_ada_final': 'delta_w', 'delta_b_ada_final': 'delta_w', 'delta_g_final': 'delta_w', 'new_m_w_ada': 'new_m', 'new_m_b_ada': 'new_m', 'new_m_g_norm1': 'new_m', 'new_m_w_in': 'new_m', 'new_m_w_rnn_conv': 'new_m', 'new_m_b_rnn_conv': 'new_m', 'new_m_w_lru_a': 'new_m', 'new_m_b_lru_a': 'new_m', 'new_m_w_lru_i': 'new_m', 'new_m_b_lru_i': 'new_m', 'new_m_lru_lambda': 'new_m', 'new_m_b_fgate': 'new_m', 'new_m_w_proj_rnn': 'new_m', 'new_m_w_proj_attn': 'new_m', 'new_m_w_out': 'new_m', 'new_m_g_norm2': 'new_m', 'new_m_w_ffn_up': 'new_m', 'new_m_w_ffn_conv': 'new_m', 'new_m_b_ffn_conv': 'new_m', 'new_m_w_ffn_down': 'new_m', 'new_m_w_ada_final': 'new_m', 'new_m_b_ada_final': 'new_m', 'new_m_g_final': 'new_m', 'new_v_w_ada': 'new_v', 'new_v_b_ada': 'new_v', 'new_v_g_norm1': 'new_v', 'new_v_w_in': 'new_v', 'new_v_w_rnn_conv': 'new_v', 'new_v_b_rnn_conv': 'new_v', 'new_v_w_lru_a': 'new_v', 'new_v_b_lru_a': 'new_v', 'new_v_w_lru_i': 'new_v', 'new_v_b_lru_i': 'new_v', 'new_v_lru_lambda': 'new_v', 'new_v_b_fgate': 'new_v', 'new_v_w_proj_rnn': 'new_v', 'new_v_w_proj_attn': 'new_v', 'new_v_w_out': 'new_v', 'new_v_g_norm2': 'new_v', 'new_v_w_ffn_up': 'new_v', 'new_v_w_ffn_conv': 'new_v', 'new_v_b_ffn_conv': 'new_v', 'new_v_w_ffn_down': 'new_v', 'new_v_w_ada_final': 'new_v', 'new_v_b_ada_final': 'new_v', 'new_v_g_final': 'new_v'}


def _forward(args):
    return _fwd_reference(*[args[k] for k in FWD_PARAMS])


def _output_shape():
    out = _jax.eval_shape(lambda: _forward(_fwd_setup_inputs(0)))
    return out.shape, out.dtype

N_MICROBATCH = 1
ADAM_LR = 0.001
ADAM_B1 = 0.9
ADAM_B2 = 0.999
ADAM_EPS = 1e-08
ADAM_WD = 0.01
ADAM_STEP = 10
PER_EXAMPLE_BATCH_AXIS = {'x': 0, 'c': 0, 'loss_target': 0}
SHARED_INPUTS = []
_WEIGHT_DTYPES = {'w_ada': _jnp.float32, 'b_ada': _jnp.float32, 'g_norm1': _jnp.float32, 'w_in': _jnp.float32, 'w_rnn_conv': _jnp.float32, 'b_rnn_conv': _jnp.float32, 'w_lru_a': _jnp.float32, 'b_lru_a': _jnp.float32, 'w_lru_i': _jnp.float32, 'b_lru_i': _jnp.float32, 'lru_lambda': _jnp.float32, 'b_fgate': _jnp.float32, 'w_proj_rnn': _jnp.float32, 'w_proj_attn': _jnp.float32, 'w_out': _jnp.float32, 'g_norm2': _jnp.float32, 'w_ffn_up': _jnp.float32, 'w_ffn_conv': _jnp.float32, 'b_ffn_conv': _jnp.float32, 'w_ffn_down': _jnp.float32, 'w_ada_final': _jnp.float32, 'b_ada_final': _jnp.float32, 'g_final': _jnp.float32}
MOMENT_SCALE = {'w_ada': 9.033893e+00, 'b_ada': 1.547247e+01, 'g_norm1': 7.554071e-01, 'w_in': 3.717643e+00, 'w_rnn_conv': 6.078752e+00, 'b_rnn_conv': 1.027433e+01, 'w_lru_a': 7.327864e-01, 'b_lru_a': 8.930368e-01, 'w_lru_i': 1.639779e+00, 'b_lru_i': 2.032161e+00, 'lru_lambda': 2.299758e+00, 'b_fgate': 8.736790e-01, 'w_proj_rnn': 6.262369e+00, 'w_proj_attn': 5.584265e+00, 'w_out': 8.407691e+00, 'g_norm2': 2.548593e+00, 'w_ffn_up': 2.687340e+00, 'w_ffn_conv': 2.636007e+00, 'b_ffn_conv': 3.077024e+00, 'w_ffn_down': 5.494376e+00, 'w_ada_final': 2.311618e+01, 'b_ada_final': 5.454143e+01, 'g_final': 1.122503e+02}


def _to_microbatches(a, axis):
    t = _jnp.moveaxis(a, axis, 0)
    t = t.reshape((N_MICROBATCH, t.shape[0] // N_MICROBATCH) + t.shape[1:])
    return _jnp.moveaxis(t, 1, axis + 1)


def setup_inputs(seed: int = 0) -> dict:
    inp = _fwd_setup_inputs(seed)
    key = _jax.random.fold_in(_jax.random.key(seed), 7919)
    shape, _ = _output_shape()
    out = dict(inp)
    out["loss_target"] = _jax.random.normal(_jax.random.fold_in(key, 0), shape, _jnp.float32)
    for i, name in enumerate(TWIN_WEIGHTS):
        w = inp[name].astype(_jnp.float32)
        if MOMENT_SCALE is None:
            s = _jnp.sqrt(_jnp.mean(_jnp.square(w)) + 1e-30)
        else:
            s = MOMENT_SCALE[name]
        km, kv = _jax.random.split(_jax.random.fold_in(key, i + 1))
        out[name] = w
        out["m_" + name] = s * _jax.random.normal(km, w.shape, _jnp.float32)
        out["v_" + name] = (s * s) * _jax.random.uniform(kv, w.shape, _jnp.float32, 0.5, 1.5)
    if N_MICROBATCH > 1:
        for name, axis in PER_EXAMPLE_BATCH_AXIS.items():
            out[name] = _to_microbatches(out[name], axis)
    return {'x': out['x'], 'c': out['c'], 'w_ada': out['w_ada'], 'b_ada': out['b_ada'], 'g_norm1': out['g_norm1'], 'w_in': out['w_in'], 'w_rnn_conv': out['w_rnn_conv'], 'b_rnn_conv': out['b_rnn_conv'], 'w_lru_a': out['w_lru_a'], 'b_lru_a': out['b_lru_a'], 'w_lru_i': out['w_lru_i'], 'b_lru_i': out['b_lru_i'], 'lru_lambda': out['lru_lambda'], 'b_fgate': out['b_fgate'], 'w_proj_rnn': out['w_proj_rnn'], 'w_proj_attn': out['w_proj_attn'], 'w_out': out['w_out'], 'g_norm2': out['g_norm2'], 'w_ffn_up': out['w_ffn_up'], 'w_ffn_conv': out['w_ffn_conv'], 'b_ffn_conv': out['b_ffn_conv'], 'w_ffn_down': out['w_ffn_down'], 'w_ada_final': out['w_ada_final'], 'b_ada_final': out['b_ada_final'], 'g_final': out['g_final'], 'loss_target': out['loss_target'], 'm_w_ada': out['m_w_ada'], 'm_b_ada': out['m_b_ada'], 'm_g_norm1': out['m_g_norm1'], 'm_w_in': out['m_w_in'], 'm_w_rnn_conv': out['m_w_rnn_conv'], 'm_b_rnn_conv': out['m_b_rnn_conv'], 'm_w_lru_a': out['m_w_lru_a'], 'm_b_lru_a': out['m_b_lru_a'], 'm_w_lru_i': out['m_w_lru_i'], 'm_b_lru_i': out['m_b_lru_i'], 'm_lru_lambda': out['m_lru_lambda'], 'm_b_fgate': out['m_b_fgate'], 'm_w_proj_rnn': out['m_w_proj_rnn'], 'm_w_proj_attn': out['m_w_proj_attn'], 'm_w_out': out['m_w_out'], 'm_g_norm2': out['m_g_norm2'], 'm_w_ffn_up': out['m_w_ffn_up'], 'm_w_ffn_conv': out['m_w_ffn_conv'], 'm_b_ffn_conv': out['m_b_ffn_conv'], 'm_w_ffn_down': out['m_w_ffn_down'], 'm_w_ada_final': out['m_w_ada_final'], 'm_b_ada_final': out['m_b_ada_final'], 'm_g_final': out['m_g_final'], 'v_w_ada': out['v_w_ada'], 'v_b_ada': out['v_b_ada'], 'v_g_norm1': out['v_g_norm1'], 'v_w_in': out['v_w_in'], 'v_w_rnn_conv': out['v_w_rnn_conv'], 'v_b_rnn_conv': out['v_b_rnn_conv'], 'v_w_lru_a': out['v_w_lru_a'], 'v_b_lru_a': out['v_b_lru_a'], 'v_w_lru_i': out['v_w_lru_i'], 'v_b_lru_i': out['v_b_lru_i'], 'v_lru_lambda': out['v_lru_lambda'], 'v_b_fgate': out['v_b_fgate'], 'v_w_proj_rnn': out['v_w_proj_rnn'], 'v_w_proj_attn': out['v_w_proj_attn'], 'v_w_out': out['v_w_out'], 'v_g_norm2': out['v_g_norm2'], 'v_w_ffn_up': out['v_w_ffn_up'], 'v_w_ffn_conv': out['v_w_ffn_conv'], 'v_b_ffn_conv': out['v_b_ffn_conv'], 'v_w_ffn_down': out['v_w_ffn_down'], 'v_w_ada_final': out['v_w_ada_final'], 'v_b_ada_final': out['v_b_ada_final'], 'v_g_final': out['v_g_final']}


def _loss(weights, diff, rest, loss_target):
    with _jax.named_scope("forward"):
        args = {**rest, TWIN_DIFF_INPUT: diff, **{k: w.astype(_WEIGHT_DTYPES[k]) for k, w in weights.items()}}
        y = _forward(args)
    with _jax.named_scope("loss_head"):
        err = _jnp.square(y.astype(_jnp.float32) - loss_target)
        return 0.5 * _jnp.sum(_jnp.mean(err, axis=-1)) if err.ndim else 0.5 * err


def _adamw(w, g, m, v):
    m = ADAM_B1 * m + (1.0 - ADAM_B1) * g
    v = ADAM_B2 * v + (1.0 - ADAM_B2) * _jnp.square(g)
    m_hat = m / (1.0 - ADAM_B1 ** ADAM_STEP)
    v_hat = v / (1.0 - ADAM_B2 ** ADAM_STEP)
    delta = -ADAM_LR * (m_hat / (_jnp.sqrt(v_hat) + ADAM_EPS) + ADAM_WD * w)
    return delta, m, v


def reference(x, c, w_ada, b_ada, g_norm1, w_in, w_rnn_conv, b_rnn_conv, w_lru_a, b_lru_a, w_lru_i, b_lru_i, lru_lambda, b_fgate, w_proj_rnn, w_proj_attn, w_out, g_norm2, w_ffn_up, w_ffn_conv, b_ffn_conv, w_ffn_down, w_ada_final, b_ada_final, g_final, loss_target, m_w_ada, m_b_ada, m_g_norm1, m_w_in, m_w_rnn_conv, m_b_rnn_conv, m_w_lru_a, m_b_lru_a, m_w_lru_i, m_b_lru_i, m_lru_lambda, m_b_fgate, m_w_proj_rnn, m_w_proj_attn, m_w_out, m_g_norm2, m_w_ffn_up, m_w_ffn_conv, m_b_ffn_conv, m_w_ffn_down, m_w_ada_final, m_b_ada_final, m_g_final, v_w_ada, v_b_ada, v_g_norm1, v_w_in, v_w_rnn_conv, v_b_rnn_conv, v_w_lru_a, v_b_lru_a, v_w_lru_i, v_b_lru_i, v_lru_lambda, v_b_fgate, v_w_proj_rnn, v_w_proj_attn, v_w_out, v_g_norm2, v_w_ffn_up, v_w_ffn_conv, v_b_ffn_conv, v_w_ffn_down, v_w_ada_final, v_b_ada_final, v_g_final):
    given = dict(x=x, c=c, w_ada=w_ada, b_ada=b_ada, g_norm1=g_norm1, w_in=w_in, w_rnn_conv=w_rnn_conv, b_rnn_conv=b_rnn_conv, w_lru_a=w_lru_a, b_lru_a=b_lru_a, w_lru_i=w_lru_i, b_lru_i=b_lru_i, lru_lambda=lru_lambda, b_fgate=b_fgate, w_proj_rnn=w_proj_rnn, w_proj_attn=w_proj_attn, w_out=w_out, g_norm2=g_norm2, w_ffn_up=w_ffn_up, w_ffn_conv=w_ffn_conv, b_ffn_conv=b_ffn_conv, w_ffn_down=w_ffn_down, w_ada_final=w_ada_final, b_ada_final=b_ada_final, g_final=g_final, loss_target=loss_target, m_w_ada=m_w_ada, m_b_ada=m_b_ada, m_g_norm1=m_g_norm1, m_w_in=m_w_in, m_w_rnn_conv=m_w_rnn_conv, m_b_rnn_conv=m_b_rnn_conv, m_w_lru_a=m_w_lru_a, m_b_lru_a=m_b_lru_a, m_w_lru_i=m_w_lru_i, m_b_lru_i=m_b_lru_i, m_lru_lambda=m_lru_lambda, m_b_fgate=m_b_fgate, m_w_proj_rnn=m_w_proj_rnn, m_w_proj_attn=m_w_proj_attn, m_w_out=m_w_out, m_g_norm2=m_g_norm2, m_w_ffn_up=m_w_ffn_up, m_w_ffn_conv=m_w_ffn_conv, m_b_ffn_conv=m_b_ffn_conv, m_w_ffn_down=m_w_ffn_down, m_w_ada_final=m_w_ada_final, m_b_ada_final=m_b_ada_final, m_g_final=m_g_final, v_w_ada=v_w_ada, v_b_ada=v_b_ada, v_g_norm1=v_g_norm1, v_w_in=v_w_in, v_w_rnn_conv=v_w_rnn_conv, v_b_rnn_conv=v_b_rnn_conv, v_w_lru_a=v_w_lru_a, v_b_lru_a=v_b_lru_a, v_w_lru_i=v_w_lru_i, v_b_lru_i=v_b_lru_i, v_lru_lambda=v_lru_lambda, v_b_fgate=v_b_fgate, v_w_proj_rnn=v_w_proj_rnn, v_w_proj_attn=v_w_proj_attn, v_w_out=v_w_out, v_g_norm2=v_g_norm2, v_w_ffn_up=v_w_ffn_up, v_w_ffn_conv=v_w_ffn_conv, v_b_ffn_conv=v_b_ffn_conv, v_w_ffn_down=v_w_ffn_down, v_w_ada_final=v_w_ada_final, v_b_ada_final=v_b_ada_final, v_g_final=v_g_final)
    weights = {n: given[n] for n in TWIN_WEIGHTS}
    shared = {n: given[n] for n in SHARED_INPUTS}
    per_example = {n: given[n] for n in ['x', 'c']}
    grad_fn = _jax.value_and_grad(_loss, argnums=(0, 1))

    def one_microbatch(ex, loss_target):
        ex = dict(ex)
        diff = ex.pop(TWIN_DIFF_INPUT)
        return grad_fn(weights, diff, {**shared, **ex}, loss_target)

    if N_MICROBATCH == 1:
        loss, (grad_w, grad_x) = one_microbatch(per_example, given["loss_target"])
    else:
        def body(carry, xs):
            loss_sum, grad_sum = carry
            l_k, (gw_k, gx_k) = one_microbatch(xs[0], xs[1])
            with _jax.named_scope("update"):
                return (loss_sum + l_k, _jax.tree.map(_jnp.add, grad_sum, gw_k)), gx_k

        init = (_jnp.zeros((), _jnp.float32), _jax.tree.map(_jnp.zeros_like, weights))
        (loss, grad_w), grad_x = _jax.lax.scan(body, init, (per_example, given["loss_target"]))
    with _jax.named_scope("update"):
        delta_w, new_m, new_v = {}, {}, {}
        for n in TWIN_WEIGHTS:
            delta_w[n], new_m[n], new_v[n] = _adamw(weights[n], grad_w[n], given["m_" + n], given["v_" + n])
    return (loss, grad_x, *[grad_w[n] for n in TWIN_WEIGHTS], *[delta_w[n] for n in TWIN_WEIGHTS],
            *[new_m[n] for n in TWIN_WEIGHTS], *[new_v[n] for n in TWIN_WEIGHTS])
```

```python
import functools
import math

import jax
import jax.numpy as jnp
from jax import lax
from jax.experimental import pallas as pl
from jax.experimental.pallas import tpu as pltpu

F32 = jnp.float32
BF16 = jnp.bfloat16
MESH = pl.DeviceIdType.MESH

RMS_EPS = 1e-6
LRU_C = 8.0
ADAM_LR = 0.001
ADAM_B1 = 0.9
ADAM_B2 = 0.999
ADAM_EPS = 1e-08
ADAM_WD = 0.01
ADAM_STEP = 10

LANES = 128
SUBLANES = 8
N_CHIPS = 4
FLAT_COLS = 1024
SCAN_SEGMENTS = SUBLANES
VMEM_LIMIT = 48 * 1024 * 1024
NEG_BIG = -1e30


def _cp(n_axes):
    return pltpu.CompilerParams(dimension_semantics=("arbitrary",) * n_axes, vmem_limit_bytes=VMEM_LIMIT)


def _tile(n, target, align):
    if n <= target:
        return n
    t = (target // align) * align
    while t >= align:
        if n % t == 0:
            return t
        t -= align
    return n


def _nice_rows(n, align):
    r = -(-n // align) * align
    while True:
        if r <= 640:
            return r, r
        t = _tile(r, 640, align)
        if 128 <= t <= 640:
            return r, t
        r += align


def _sigmoid(x):
    return jax.nn.sigmoid(x)


def _softplus(x):
    return jnp.maximum(x, 0.0) + jnp.log1p(jnp.exp(-jnp.abs(x)))


def _expm1(x):
    small = x * (1.0 + 0.5 * x * (1.0 + (1.0 / 3.0) * x * (1.0 + 0.25 * x)))
    return jnp.where(jnp.abs(x) < 0.05, small, jnp.exp(x) - 1.0)


_GELU_K = math.sqrt(2.0 / math.pi)
_GELU_C = 0.044715


def _gelu(x):
    t = jnp.tanh(_GELU_K * (x + _GELU_C * x * x * x))
    return 0.5 * x * (1.0 + t)


def _gelu_and_grad(x):
    t = jnp.tanh(_GELU_K * (x + _GELU_C * x * x * x))
    g = 0.5 * x * (1.0 + t)
    dg = 0.5 * (1.0 + t) + 0.5 * x * (1.0 - t * t) * _GELU_K * (1.0 + 3.0 * _GELU_C * x * x)
    return g, dg


def _shift_down(x, k):
    if k == 0:
        return x
    rows = lax.broadcasted_iota(jnp.int32, x.shape, 0)
    return jnp.where(rows >= k, pltpu.roll(x, k, 0), 0.0)


def _shift_up(x, k):
    if k == 0:
        return x
    s = x.shape[0]
    rows = lax.broadcasted_iota(jnp.int32, x.shape, 0)
    return jnp.where(rows < s - k, pltpu.roll(x, s - k, 0), 0.0)


def _dot(a, b, dims):
    return lax.dot_general(a.astype(BF16), b.astype(BF16), (dims, ((), ())), preferred_element_type=F32)


_NN = ((1,), (0,))
_NT = ((1,), (1,))
_TN = ((0,), (0,))


def _mm(a, b, mode, *, name, out_dtype=F32, scale=None, bias=None, add=None, tm=1024, tn=1024, tk=512):
    if mode == "nn":
        (m, k), (k2, n) = a.shape, b.shape
    elif mode == "nt":
        (m, k), (n, k2) = a.shape, b.shape
    else:
        (k, m), (k2, n) = a.shape, b.shape
    assert k == k2, (a.shape, b.shape, mode)
    tm = _tile(m, tm, LANES)
    tn = _tile(n, tn, LANES)
    tk = _tile(k, tk, LANES)
    nk = k // tk
    dims = {"nn": _NN, "nt": _NT, "tn": _TN}[mode]

    if mode == "tn":
        a_spec = pl.BlockSpec((tk, tm), lambda i, j, kk: (kk, i))
    else:
        a_spec = pl.BlockSpec((tm, tk), lambda i, j, kk: (i, kk))
    if mode == "nt":
        b_spec = pl.BlockSpec((tn, tk), lambda i, j, kk: (j, kk))
    else:
        b_spec = pl.BlockSpec((tk, tn), lambda i, j, kk: (kk, j))
    in_specs = [a_spec, b_spec]
    args = [a, b]
    if bias is not None:
        in_specs.append(pl.BlockSpec((1, tn), lambda i, j, kk: (0, j)))
        args.append(bias)
    if add is not None:
        in_specs.append(pl.BlockSpec((tm, tn), lambda i, j, kk: (i, j)))
        args.append(add)

    def kern(*refs):
        a_ref, b_ref = refs[0], refs[1]
        o_ref, acc = refs[-2], refs[-1]
        kk = pl.program_id(2)

        @pl.when(kk == 0)
        def _():
            acc[...] = jnp.zeros_like(acc)

        acc[...] += _dot(a_ref[...], b_ref[...], dims)

        @pl.when(kk == nk - 1)
        def _():
            r = acc[...]
            if scale is not None:
                r = r * scale
            pos = 2
            if bias is not None:
                r = r + refs[pos][...]
                pos += 1
            if add is not None:
                r = r + refs[pos][...]
            o_ref[...] = r.astype(out_dtype)

    return pl.pallas_call(
        kern, name=name,
        grid=(m // tm, n // tn, nk),
        in_specs=in_specs,
        out_specs=pl.BlockSpec((tm, tn), lambda i, j, kk: (i, j)),
        out_shape=jax.ShapeDtypeStruct((m, n), out_dtype),
        scratch_shapes=[pltpu.VMEM((tm, tn), F32)],
        compiler_params=_cp(3),
    )(*args)


def _silu_pad(c, rows):
    bl, d = c.shape

    def kern(c_ref, o_ref):
        o_ref[...] = jnp.zeros_like(o_ref)
        v = c_ref[...]
        o_ref[0:bl, :] = v * _sigmoid(v)

    return pl.pallas_call(kern, name="silu_pad", out_shape=jax.ShapeDtypeStruct((rows, d), F32))(c)


def _rowsum(x, name):
    r, n = x.shape

    def kern(x_ref, o_ref):
        o_ref[...] = jnp.sum(x_ref[...], axis=0, keepdims=True)

    return pl.pallas_call(kern, name=name, out_shape=jax.ShapeDtypeStruct((1, n), F32))(x)


def _norm_parts(x, g):
    r = lax.rsqrt(jnp.mean(x * x, axis=-1, keepdims=True) + RMS_EPS)
    xh = x * r
    return r, xh, xh * g


def _norm_bwd_parts(dh, xh, r, g, sc):
    n = xh * g
    dn = dh * (1.0 + sc)
    dxh = dn * g
    dx = r * (dxh - xh * jnp.mean(dxh * xh, axis=-1, keepdims=True))
    return dx, dh, dh * n, dn * xh


def _act_specs(ts, d, n):
    return [pl.BlockSpec((1, ts, d), lambda b, t: (b, t, 0)) for _ in range(n)]


def _vec_spec(d):
    return pl.BlockSpec((1, 1, d), lambda b, t: (b, 0, 0))


def _par_spec(d):
    return pl.BlockSpec((1, d), lambda b, t: (0, 0))


def _norm_mod_fwd(x3, g, sh, sc):
    bl, s, d = x3.shape
    ts = _tile(s, 512, SUBLANES)

    def kern(x_ref, g_ref, sh_ref, sc_ref, h_ref):
        _, _, n = _norm_parts(x_ref[0], g_ref[...])
        h_ref[0] = (n * (1.0 + sc_ref[0]) + sh_ref[0]).astype(BF16)

    return pl.pallas_call(
        kern, name="norm_mod_fwd", grid=(bl, s // ts),
        in_specs=_act_specs(ts, d, 1) + [_par_spec(d), _vec_spec(d), _vec_spec(d)],
        out_specs=_act_specs(ts, d, 1)[0],
        out_shape=jax.ShapeDtypeStruct((bl, s, d), BF16),
        compiler_params=_cp(2),
    )(x3, g, sh, sc)


def _resid_norm_fwd(x3, y3, gate, g, sh, sc):
    bl, s, d = x3.shape
    ts = _tile(s, 512, SUBLANES)

    def kern(x_ref, y_ref, gate_ref, g_ref, sh_ref, sc_ref, x1_ref, h_ref):
        x1 = x_ref[0] + gate_ref[0] * y_ref[0]
        x1_ref[0] = x1
        _, _, n = _norm_parts(x1, g_ref[...])
        h_ref[0] = (n * (1.0 + sc_ref[0]) + sh_ref[0]).astype(BF16)

    return pl.pallas_call(
        kern, name="resid_norm_fwd", grid=(bl, s // ts),
        in_specs=_act_specs(ts, d, 2) + [_vec_spec(d), _par_spec(d), _vec_spec(d), _vec_spec(d)],
        out_specs=_act_specs(ts, d, 2),
        out_shape=[jax.ShapeDtypeStruct((bl, s, d), F32), jax.ShapeDtypeStruct((bl, s, d), BF16)],
        compiler_params=_cp(2),
    )(x3, y3, gate, g, sh, sc)


def _norm_mod_bwd(dh3, x3, dres3, g, sc, name):
    bl, s, d = x3.shape
    ts = _tile(s, 512, SUBLANES)

    def kern(dh_ref, x_ref, dres_ref, g_ref, sc_ref, dx_ref, dsh_ref, dsc_ref, dg_ref):
        b, t = pl.program_id(0), pl.program_id(1)
        gv = g_ref[...]
        r, xh, _ = _norm_parts(x_ref[0], gv)
        dx, a, bb, cc = _norm_bwd_parts(dh_ref[0], xh, r, gv, sc_ref[0])
        dx_ref[0] = dres_ref[0] + dx

        @pl.when(t == 0)
        def _():
            dsh_ref[...] = jnp.zeros_like(dsh_ref)
            dsc_ref[...] = jnp.zeros_like(dsc_ref)

        @pl.when((t == 0) & (b == 0))
        def _():
            dg_ref[...] = jnp.zeros_like(dg_ref)

        dsh_ref[0] += jnp.sum(a, axis=0, keepdims=True)
        dsc_ref[0] += jnp.sum(bb, axis=0, keepdims=True)
        dg_ref[...] += jnp.sum(cc, axis=0, keepdims=True)

    return pl.pallas_call(
        kern, name=name, grid=(bl, s // ts),
        in_specs=_act_specs(ts, d, 3) + [_par_spec(d), _vec_spec(d)],
        out_specs=[_act_specs(ts, d, 1)[0], _vec_spec(d), _vec_spec(d), _par_spec(d)],
        out_shape=[jax.ShapeDtypeStruct((bl, s, d), F32), jax.ShapeDtypeStruct((bl, 1, d), F32),
                   jax.ShapeDtypeStruct((bl, 1, d), F32), jax.ShapeDtypeStruct((1, d), F32)],
        compiler_params=_cp(2),
    )(dh3, x3, dres3, g, sc)


def _gate_bwd(dx3, y3, gate, name):
    bl, s, d = dx3.shape
    ts = _tile(s, 512, SUBLANES)

    def kern(dx_ref, y_ref, gate_ref, dy_ref, dgate_ref):
        t = pl.program_id(1)
        dx = dx_ref[0]
        dy_ref[0] = (gate_ref[0] * dx).astype(BF16)

        @pl.when(t == 0)
        def _():
            dgate_ref[...] = jnp.zeros_like(dgate_ref)

        dgate_ref[0] += jnp.sum(dx * y_ref[0], axis=0, keepdims=True)

    return pl.pallas_call(
        kern, name=name, grid=(bl, s // ts),
        in_specs=_act_specs(ts, d, 2) + [_vec_spec(d)],
        out_specs=[_act_specs(ts, d, 1)[0], _vec_spec(d)],
        out_shape=[jax.ShapeDtypeStruct((bl, s, d), BF16), jax.ShapeDtypeStruct((bl, 1, d), F32)],
        compiler_params=_cp(2),
    )(dx3, y3, gate)


def _final_fwd_bwd(x1, yf, gate2, g, shf, scf, tgt):
    bl, s, d = x1.shape
    ts = _tile(s, 512, SUBLANES)

    def kern(x1_ref, yf_ref, gate_ref, g_ref, sh_ref, sc_ref, tgt_ref, dx_ref, dsh_ref, dsc_ref, dg_ref, loss_ref):
        b, t = pl.program_id(0), pl.program_id(1)
        gv, sc = g_ref[...], sc_ref[0]
        x2 = x1_ref[0] + gate_ref[0] * yf_ref[0]
        r, xh, n = _norm_parts(x2, gv)
        err = n * (1.0 + sc) + sh_ref[0] - tgt_ref[0]
        dx, a, bb, cc = _norm_bwd_parts(err * (1.0 / d), xh, r, gv, sc)
        dx_ref[0] = dx

        @pl.when(t == 0)
        def _():
            dsh_ref[...] = jnp.zeros_like(dsh_ref)
            dsc_ref[...] = jnp.zeros_like(dsc_ref)

        @pl.when((t == 0) & (b == 0))
        def _():
            dg_ref[...] = jnp.zeros_like(dg_ref)
            loss_ref[...] = jnp.zeros_like(loss_ref)

        dsh_ref[0] += jnp.sum(a, axis=0, keepdims=True)
        dsc_ref[0] += jnp.sum(bb, axis=0, keepdims=True)
        dg_ref[...] += jnp.sum(cc, axis=0, keepdims=True)
        tok = jnp.mean(err * err, axis=-1, keepdims=True)
        loss_ref[...] += 0.5 * jnp.sum(tok, axis=0, keepdims=True)

    return pl.pallas_call(
        kern, name="final_fwd_bwd", grid=(bl, s // ts),
        in_specs=_act_specs(ts, d, 2) + [_vec_spec(d), _par_spec(d), _vec_spec(d), _vec_spec(d)] + _act_specs(ts, d, 1),
        out_specs=[_act_specs(ts, d, 1)[0], _vec_spec(d), _vec_spec(d), _par_spec(d),
                   pl.BlockSpec((1, 1), lambda b, t: (0, 0))],
        out_shape=[jax.ShapeDtypeStruct((bl, s, d), F32), jax.ShapeDtypeStruct((bl, 1, d), F32),
                   jax.ShapeDtypeStruct((bl, 1, d), F32), jax.ShapeDtypeStruct((1, d), F32),
                   jax.ShapeDtypeStruct((1, 1), F32)],
        compiler_params=_cp(2),
    )(x1, yf, gate2, g, shf, scf, tgt)


def _rnn_gates(xr, cw, cb, wa, ba, wi, bi, lam):
    kw = cw.shape[0]
    xc = cb
    for k in range(kw):
        xc = xc + _shift_down(xr, kw - 1 - k) * cw[k:k + 1, :]
    r = _sigmoid(_dot(xc, wa, _NN) + ba)
    i = _sigmoid(_dot(xc, wi, _NN) + bi)
    sp = _softplus(-lam)
    log_a = -LRU_C * r * sp
    a = jnp.exp(log_a)
    mult = jnp.sqrt(-_expm1(2.0 * log_a))
    return xc, r, i, sp, a, mult


def _segment_scan(a_s, u_s, h_s, p_s, reverse):
    s, c = a_s.shape
    seg = s // SCAN_SEGMENTS

    def step(n, carry):
        t = (seg - 1 - n) if reverse else n
        h, p = carry
        av = a_s[pl.ds(t, SCAN_SEGMENTS, stride=seg), :]
        uv = u_s[pl.ds(t, SCAN_SEGMENTS, stride=seg), :]
        h = av * h + uv
        p = p * av
        h_s[pl.ds(t, SCAN_SEGMENTS, stride=seg), :] = h
        p_s[pl.ds(t, SCAN_SEGMENTS, stride=seg), :] = p
        return h, p

    lax.fori_loop(0, seg, step, (jnp.zeros((SCAN_SEGMENTS, c), F32), jnp.ones((SCAN_SEGMENTS, c), F32)))
    carry = jnp.zeros((1, c), F32)
    order = range(SCAN_SEGMENTS - 1, -1, -1) if reverse else range(SCAN_SEGMENTS)
    for j in order:
        rows = pl.ds(j * seg, seg)
        fixed = h_s[rows, :] + p_s[rows, :] * carry
        h_s[rows, :] = fixed
        carry = fixed[0:1, :] if reverse else fixed[seg - 1:seg, :]


def _rnn_specs(s, rb, nb):
    act = lambda off: pl.BlockSpec((1, s, rb), lambda b, n, off=off: (b, 0, off + n))
    par = pl.BlockSpec((1, rb), lambda b, n: (0, n))
    wsp = pl.BlockSpec((1, rb, rb), lambda b, n: (n, 0, 0))
    return act, par, wsp


def _rnn_fwd(zr3, cw, cb, wa, ba, wi, bi, lam):
    bl, s, two = zr3.shape
    nb, rb, _ = wa.shape
    dr = nb * rb
    kw = cw.shape[0]
    act, par, wsp = _rnn_specs(s, rb, nb)

    def kern(xr_ref, gr_ref, cw_ref, cb_ref, wa_ref, ba_ref, wi_ref, bi_ref, lam_ref, h_ref, y_ref, a_s, u_s, h_s, p_s):
        xc, r, i, sp, a, mult = _rnn_gates(xr_ref[0], cw_ref[...], cb_ref[...], wa_ref[0], ba_ref[...],
                                           wi_ref[0], bi_ref[...], lam_ref[...])
        a_s[...] = a
        u_s[...] = mult * (i * xc)
        _segment_scan(a_s, u_s, h_s, p_s, reverse=False)
        h = h_s[...]
        h_ref[0] = h
        y_ref[0] = (_gelu(gr_ref[0]) * h).astype(BF16)

    return pl.pallas_call(
        kern, name="rnn_fwd", grid=(bl, nb),
        in_specs=[act(0), act(nb), pl.BlockSpec((kw, rb), lambda b, n: (0, n)), par, wsp, par, wsp, par, par],
        out_specs=[act(0), act(0)],
        out_shape=[jax.ShapeDtypeStruct((bl, s, dr), F32), jax.ShapeDtypeStruct((bl, s, dr), BF16)],
        scratch_shapes=[pltpu.VMEM((s, rb), F32)] * 4,
        compiler_params=_cp(2),
    )(zr3, zr3, cw, cb, wa, ba, wi, bi, lam)


def _rnn_bwd(zr3, h3, dy3, cw, cb, wa, ba, wi, bi, lam):
    bl, s, _ = zr3.shape
    nb, rb, _ = wa.shape
    dr = nb * rb
    kw = cw.shape[0]
    act = lambda off: pl.BlockSpec((1, s, rb), lambda n, b, off=off: (b, 0, off + n))
    par = pl.BlockSpec((1, rb), lambda n, b: (0, n))
    wsp = pl.BlockSpec((1, rb, rb), lambda n, b: (n, 0, 0))
    cws = pl.BlockSpec((kw, rb), lambda n, b: (0, n))

    def kern(xr_ref, gr_ref, h_ref, dy_ref, cw_ref, cb_ref, wa_ref, ba_ref, wi_ref, bi_ref, lam_ref,
             dxr_ref, dgr_ref, dcw_ref, dcb_ref, dwa_ref, dba_ref, dwi_ref, dbi_ref, dlam_ref, a_s, u_s, h_s, p_s):
        b = pl.program_id(1)
        xr, cwv, lamv = xr_ref[0], cw_ref[...], lam_ref[...]
        wav, wiv = wa_ref[0], wi_ref[0]
        xc, r, i, sp, a, mult = _rnn_gates(xr, cwv, cb_ref[...], wav, ba_ref[...], wiv, bi_ref[...], lamv)
        h, dy = h_ref[0], dy_ref[0]
        ge, dge = _gelu_and_grad(gr_ref[0])
        dgr_ref[0] = (dy * h * dge).astype(BF16)
        a_s[...] = _shift_up(a, 1)
        u_s[...] = dy * ge
        _segment_scan(a_s, u_s, h_s, p_s, reverse=True)
        g = h_s[...]
        da = g * _shift_down(h, 1)
        ix = i * xc
        dlog_a = da * a + (g * ix) * (-(a * a) / mult)
        di = g * mult * xc
        dpa = (dlog_a * (-LRU_C * sp)) * r * (1.0 - r)
        dpi = di * i * (1.0 - i)
        dxc = g * mult * i + _dot(dpa, wav, _NT) + _dot(dpi, wiv, _NT)
        dxr = jnp.zeros_like(dxc)
        dcw_rows = []
        for k in range(kw):
            dxr = dxr + _shift_up(dxc, kw - 1 - k) * cwv[k:k + 1, :]
            dcw_rows.append(jnp.sum(dxc * _shift_down(xr, kw - 1 - k), axis=0, keepdims=True))
        dxr_ref[0] = dxr.astype(BF16)

        @pl.when(b == 0)
        def _():
            for ref in (dcw_ref, dcb_ref, dwa_ref, dba_ref, dwi_ref, dbi_ref, dlam_ref):
                ref[...] = jnp.zeros_like(ref)

        for k in range(kw):
            dcw_ref[k:k + 1, :] += dcw_rows[k]
        dcb_ref[...] += jnp.sum(dxc, axis=0, keepdims=True)
        dwa_ref[0] += _dot(xc, dpa, _TN)
        dwi_ref[0] += _dot(xc, dpi, _TN)
        dba_ref[...] += jnp.sum(dpa, axis=0, keepdims=True)
        dbi_ref[...] += jnp.sum(dpi, axis=0, keepdims=True)
        dsp = jnp.sum(dlog_a * (-LRU_C * r), axis=0, keepdims=True)
        dlam_ref[...] += dsp * (-_sigmoid(-lamv))

    vec = jax.ShapeDtypeStruct((1, dr), F32)
    wsh = jax.ShapeDtypeStruct((nb, rb, rb), F32)
    return pl.pallas_call(
        kern, name="rnn_bwd", grid=(nb, bl),
        in_specs=[act(0), act(nb), act(0), act(0), cws, par, wsp, par, wsp, par, par],
        out_specs=[act(0), act(0), cws, par, wsp, par, wsp, par, par],
        out_shape=[jax.ShapeDtypeStruct((bl, s, dr), BF16), jax.ShapeDtypeStruct((bl, s, dr), BF16),
                   jax.ShapeDtypeStruct((kw, dr), F32), vec, wsh, vec, wsh, vec, vec],
        scratch_shapes=[pltpu.VMEM((s, rb), F32)] * 4,
        compiler_params=_cp(2),
    )(zr3, zr3, h3, dy3, cw, cb, wa, ba, wi, bi, lam)


def _tri(n, upper):
    r = lax.broadcasted_iota(jnp.int32, (n, n), 0)
    c = lax.broadcasted_iota(jnp.int32, (n, n), 1)
    return jnp.where((c >= r) if upper else (c <= r), 1.0, 0.0).astype(F32)


def _fgate_fwd(zf3, bf):
    bl, s, w = zf3.shape
    ch = _tile(s, 256, SUBLANES)

    def kern(z_ref, b_ref, f_ref):
        tri = _tri(ch, upper=False)
        carry = jnp.zeros((1, w), F32)
        for j in range(s // ch):
            rows = pl.ds(j * ch, ch)
            lf = -_softplus(-(z_ref[0, rows, :] + b_ref[...]))
            out = jnp.dot(tri, lf, precision=lax.Precision.HIGHEST, preferred_element_type=F32) + carry
            f_ref[0, rows, :] = out
            carry = out[ch - 1:ch, :]

    return pl.pallas_call(
        kern, name="fgate_fwd", grid=(bl,),
        in_specs=[pl.BlockSpec((1, s, w), lambda b: (b, 0, 0)), pl.BlockSpec((1, w), lambda b: (0, 0))],
        out_specs=pl.BlockSpec((1, s, w), lambda b: (b, 0, 0)),
        out_shape=jax.ShapeDtypeStruct((bl, s, w), F32),
        compiler_params=_cp(1),
    )(zf3, bf)


def _fgate_bwd(df3, zf3, bf):
    bl, s, w = zf3.shape
    ch = _tile(s, 256, SUBLANES)

    def kern(df_ref, z_ref, b_ref, dz_ref, db_ref):
        b = pl.program_id(0)
        tri = _tri(ch, upper=True)
        carry = jnp.zeros((1, w), F32)
        dbsum = jnp.zeros((1, w), F32)
        for j in range(s // ch - 1, -1, -1):
            rows = pl.ds(j * ch, ch)
            dlf = jnp.dot(tri, df_ref[0, rows, :], precision=lax.Precision.HIGHEST, preferred_element_type=F32) + carry
            carry = dlf[0:1, :]
            dz = dlf * _sigmoid(-(z_ref[0, rows, :] + b_ref[...]))
            dz_ref[0, rows, :] = dz.astype(BF16)
            dbsum = dbsum + jnp.sum(dz, axis=0, keepdims=True)

        @pl.when(b == 0)
        def _():
            db_ref[...] = jnp.zeros_like(db_ref)

        db_ref[...] += dbsum

    return pl.pallas_call(
        kern, name="fgate_bwd", grid=(bl,),
        in_specs=[pl.BlockSpec((1, s, w), lambda b: (b, 0, 0)), pl.BlockSpec((1, s, w), lambda b: (b, 0, 0)),
                  pl.BlockSpec((1, w), lambda b: (0, 0))],
        out_specs=[pl.BlockSpec((1, s, w), lambda b: (b, 0, 0)), pl.BlockSpec((1, w), lambda b: (0, 0))],
        out_shape=[jax.ShapeDtypeStruct((bl, s, w), BF16), jax.ShapeDtypeStruct((1, w), F32)],
        compiler_params=_cp(1),
    )(df3, zf3, bf)


def _causal(sc, row0, col0, transposed):
    r = lax.broadcasted_iota(jnp.int32, sc.shape, 0) + row0
    c = lax.broadcasted_iota(jnp.int32, sc.shape, 1) + col0
    return jnp.where((c >= r) if transposed else (r >= c), sc, NEG_BIG)


def _attn_fwd(q3, kv3, fcol, frow, nh):
    bl, s, da = q3.shape
    dh = da // nh
    tq = _tile(s, 512, LANES)
    nq = s // tq

    def kern(q_ref, k_ref, v_ref, fq_ref, fk_ref, o_ref, lse_ref, m_s, l_s, acc):
        iq, ik = pl.program_id(2), pl.program_id(3)

        @pl.when(ik == 0)
        def _():
            m_s[...] = jnp.full_like(m_s, NEG_BIG)
            l_s[...] = jnp.zeros_like(l_s)
            acc[...] = jnp.zeros_like(acc)

        @pl.when(ik <= iq)
        def _():
            sc = _dot(q_ref[0], k_ref[0], _NT) + fq_ref[0][:, 0:1] - fk_ref[0]
            sc = _causal(sc, iq * tq, ik * tq, False)
            m_old = m_s[...]
            m_new = jnp.maximum(m_old, jnp.max(sc, axis=-1, keepdims=True))
            alpha = jnp.exp(m_old - m_new)
            p = jnp.exp(sc - m_new)
            l_s[...] = alpha * l_s[...] + jnp.sum(p, axis=-1, keepdims=True)
            acc[...] = alpha * acc[...] + _dot(p, v_ref[0], _NN)
            m_s[...] = m_new

        @pl.when(ik == nq - 1)
        def _():
            l = l_s[...]
            o_ref[0] = acc[...] / l
            lse_ref[0] = jnp.broadcast_to(m_s[...] + jnp.log(l), (tq, LANES))

    kmap = lambda off: (lambda b, h, iq, ik: (b, jnp.minimum(ik, iq), off + h))
    return pl.pallas_call(
        kern, name="attn_fwd", grid=(bl, nh, nq, nq),
        in_specs=[pl.BlockSpec((1, tq, dh), lambda b, h, iq, ik: (b, iq, h)),
                  pl.BlockSpec((1, tq, dh), kmap(0)), pl.BlockSpec((1, tq, dh), kmap(nh)),
                  pl.BlockSpec((1, tq, LANES), lambda b, h, iq, ik: (b * nh + h, iq, 0)),
                  pl.BlockSpec((1, 1, tq), lambda b, h, iq, ik: (b * nh + h, 0, jnp.minimum(ik, iq)))],
        out_specs=[pl.BlockSpec((1, tq, dh), lambda b, h, iq, ik: (b, iq, h)),
                   pl.BlockSpec((1, tq, LANES), lambda b, h, iq, ik: (b * nh + h, iq, 0))],
        out_shape=[jax.ShapeDtypeStruct((bl, s, da), F32), jax.ShapeDtypeStruct((bl * nh, s, LANES), F32)],
        scratch_shapes=[pltpu.VMEM((tq, 1), F32), pltpu.VMEM((tq, 1), F32), pltpu.VMEM((tq, dh), F32)],
        compiler_params=_cp(4),
    )(q3, kv3, kv3, fcol, frow)


def _attn_delta(q3, kv3, do3, lse, fcol, frow, nh):
    bl, s, da = q3.shape
    dh = da // nh
    tq = _tile(s, 512, LANES)
    nq = s // tq

    def kern(q_ref, k_ref, v_ref, do_ref, lse_ref, fq_ref, fk_ref, delta_ref, d_s):
        iq, ik = pl.program_id(2), pl.program_id(3)

        @pl.when(ik == 0)
        def _():
            d_s[...] = jnp.zeros_like(d_s)

        @pl.when(ik <= iq)
        def _():
            sc = _dot(q_ref[0], k_ref[0], _NT) + fq_ref[0][:, 0:1] - fk_ref[0]
            sc = _causal(sc, iq * tq, ik * tq, False)
            p = jnp.exp(sc - lse_ref[0][:, 0:1])
            d_s[...] += jnp.sum(p * _dot(do_ref[0], v_ref[0], _NT), axis=-1, keepdims=True)

        @pl.when(ik == nq - 1)
        def _():
            delta_ref[0] = jnp.broadcast_to(d_s[...], (tq, LANES))

    qmap = lambda b, h, iq, ik: (b, iq, h)
    bmap = lambda b, h, iq, ik: (b * nh + h, iq, 0)
    kmap = lambda off: (lambda b, h, iq, ik: (b, jnp.minimum(ik, iq), off + h))
    return pl.pallas_call(
        kern, name="attn_delta", grid=(bl, nh, nq, nq),
        in_specs=[pl.BlockSpec((1, tq, dh), qmap), pl.BlockSpec((1, tq, dh), kmap(0)), pl.BlockSpec((1, tq, dh), kmap(nh)),
                  pl.BlockSpec((1, tq, dh), qmap), pl.BlockSpec((1, tq, LANES), bmap), pl.BlockSpec((1, tq, LANES), bmap),
                  pl.BlockSpec((1, 1, tq), lambda b, h, iq, ik: (b * nh + h, 0, jnp.minimum(ik, iq)))],
        out_specs=pl.BlockSpec((1, tq, LANES), bmap),
        out_shape=jax.ShapeDtypeStruct((bl * nh, s, LANES), F32),
        scratch_shapes=[pltpu.VMEM((tq, 1), F32)],
        compiler_params=_cp(4),
    )(q3, kv3, kv3, do3, lse, fcol, frow)


def _attn_bwd_dq(q3, kv3, do3, lse, delta, fcol, frow, nh, scale):
    bl, s, da = q3.shape
    dh = da // nh
    tq = _tile(s, 512, LANES)
    nq = s // tq

    def kern(q_ref, k_ref, v_ref, do_ref, lse_ref, dl_ref, fq_ref, fk_ref, dq_ref, acc):
        iq, ik = pl.program_id(2), pl.program_id(3)

        @pl.when(ik == 0)
        def _():
            acc[...] = jnp.zeros_like(acc)

        @pl.when(ik <= iq)
        def _():
            sc = _dot(q_ref[0], k_ref[0], _NT) + fq_ref[0][:, 0:1] - fk_ref[0]
            sc = _causal(sc, iq * tq, ik * tq, False)
            p = jnp.exp(sc - lse_ref[0][:, 0:1])
            dp = _dot(do_ref[0], v_ref[0], _NT)
            ds = p * (dp - dl_ref[0][:, 0:1])
            acc[...] += _dot(ds, k_ref[0], _NN)

        @pl.when(ik == nq - 1)
        def _():
            dq_ref[0] = (acc[...] * scale).astype(BF16)

    qmap = lambda b, h, iq, ik: (b, iq, h)
    bmap = lambda b, h, iq, ik: (b * nh + h, iq, 0)
    kmap = lambda off: (lambda b, h, iq, ik: (b, jnp.minimum(ik, iq), off + h))
    return pl.pallas_call(
        kern, name="attn_bwd_dq", grid=(bl, nh, nq, nq),
        in_specs=[pl.BlockSpec((1, tq, dh), qmap), pl.BlockSpec((1, tq, dh), kmap(0)), pl.BlockSpec((1, tq, dh), kmap(nh)),
                  pl.BlockSpec((1, tq, dh), qmap), pl.BlockSpec((1, tq, LANES), bmap), pl.BlockSpec((1, tq, LANES), bmap),
                  pl.BlockSpec((1, tq, LANES), bmap),
                  pl.BlockSpec((1, 1, tq), lambda b, h, iq, ik: (b * nh + h, 0, jnp.minimum(ik, iq)))],
        out_specs=pl.BlockSpec((1, tq, dh), qmap),
        out_shape=jax.ShapeDtypeStruct((bl, s, da), BF16),
        scratch_shapes=[pltpu.VMEM((tq, dh), F32)],
        compiler_params=_cp(4),
    )(q3, kv3, kv3, do3, lse, delta, fcol, frow)


def _attn_bwd_dkv(q3, kv3, do3, lse_row, delta_row, fcol, frow, nh):
    bl, s, da = q3.shape
    dh = da // nh
    tk = _tile(s, 512, LANES)
    nk = s // tk

    def kern(q_ref, k_ref, v_ref, do_ref, lse_ref, dl_ref, fk_ref, fq_ref, dk_ref, dv_ref, df_ref, dk_acc, dv_acc, df_acc):
        ik, iq = pl.program_id(2), pl.program_id(3)

        @pl.when(iq == 0)
        def _():
            dk_acc[...] = jnp.zeros_like(dk_acc)
            dv_acc[...] = jnp.zeros_like(dv_acc)
            df_acc[...] = jnp.zeros_like(df_acc)

        @pl.when(iq >= ik)
        def _():
            st = _dot(k_ref[0], q_ref[0], _NT) - fk_ref[0][:, 0:1] + fq_ref[0]
            st = _causal(st, ik * tk, iq * tk, True)
            pt = jnp.exp(st - lse_ref[0])
            dv_acc[...] += _dot(pt, do_ref[0], _NN)
            dpt = _dot(v_ref[0], do_ref[0], _NT)
            dst = pt * (dpt - dl_ref[0])
            dk_acc[...] += _dot(dst, q_ref[0], _NN)
            df_acc[...] += jnp.sum(dst, axis=-1, keepdims=True)

        @pl.when(iq == nk - 1)
        def _():
            dk_ref[0] = dk_acc[...].astype(BF16)
            dv_ref[0] = dv_acc[...].astype(BF16)
            df_ref[0] = jnp.broadcast_to(-df_acc[...], (tk, LANES))

    qmap = lambda b, h, ik, iq: (b, jnp.maximum(iq, ik), h)
    rmap = lambda b, h, ik, iq: (b * nh + h, 0, jnp.maximum(iq, ik))
    kmap = lambda off: (lambda b, h, ik, iq: (b, ik, off + h))
    bmap = lambda b, h, ik, iq: (b * nh + h, ik, 0)
    return pl.pallas_call(
        kern, name="attn_bwd_dkv", grid=(bl, nh, nk, nk),
        in_specs=[pl.BlockSpec((1, tk, dh), qmap), pl.BlockSpec((1, tk, dh), kmap(0)), pl.BlockSpec((1, tk, dh), kmap(nh)),
                  pl.BlockSpec((1, tk, dh), qmap), pl.BlockSpec((1, 1, tk), rmap), pl.BlockSpec((1, 1, tk), rmap),
                  pl.BlockSpec((1, tk, LANES), bmap), pl.BlockSpec((1, 1, tk), rmap)],
        out_specs=[pl.BlockSpec((1, tk, dh), kmap(0)), pl.BlockSpec((1, tk, dh), kmap(0)), pl.BlockSpec((1, tk, LANES), bmap)],
        out_shape=[jax.ShapeDtypeStruct((bl, s, da), BF16), jax.ShapeDtypeStruct((bl, s, da), BF16),
                   jax.ShapeDtypeStruct((bl * nh, s, LANES), F32)],
        scratch_shapes=[pltpu.VMEM((tk, dh), F32), pltpu.VMEM((tk, dh), F32), pltpu.VMEM((tk, 1), F32)],
        compiler_params=_cp(4),
    )(q3, kv3, kv3, do3, lse_row, delta_row, fcol, frow)


def _merge_fwd(mg3, pr3, pa3):
    bl, s, d = pr3.shape
    ts = _tile(s, 256, SUBLANES)
    half = lambda j: pl.BlockSpec((1, ts, d), lambda b, t, j=j: (b, t, j))

    def kern(mr_ref, ma_ref, pr_ref, pa_ref, o_ref):
        o_ref[0] = (_sigmoid(mr_ref[0]) * pr_ref[0] + _sigmoid(ma_ref[0]) * pa_ref[0]).astype(BF16)

    return pl.pallas_call(
        kern, name="merge_fwd", grid=(bl, s // ts),
        in_specs=[half(0), half(1)] + _act_specs(ts, d, 2), out_specs=_act_specs(ts, d, 1)[0],
        out_shape=jax.ShapeDtypeStruct((bl, s, d), BF16), compiler_params=_cp(2),
    )(mg3, mg3, pr3, pa3)


def _merge_bwd(dm3, mg3, pr3, pa3):
    bl, s, d = pr3.shape
    ts = _tile(s, 256, SUBLANES)
    half = lambda j: pl.BlockSpec((1, ts, d), lambda b, t, j=j: (b, t, j))

    def kern(dm_ref, mr_ref, ma_ref, pr_ref, pa_ref, dpr_ref, dpa_ref, dmr_ref, dma_ref):
        dm = dm_ref[0]
        gr, ga = _sigmoid(mr_ref[0]), _sigmoid(ma_ref[0])
        dpr_ref[0] = (gr * dm).astype(BF16)
        dpa_ref[0] = (ga * dm).astype(BF16)
        dmr_ref[0] = (dm * pr_ref[0] * gr * (1.0 - gr)).astype(BF16)
        dma_ref[0] = (dm * pa_ref[0] * ga * (1.0 - ga)).astype(BF16)

    return pl.pallas_call(
        kern, name="merge_bwd", grid=(bl, s // ts),
        in_specs=_act_specs(ts, d, 1) + [half(0), half(1)] + _act_specs(ts, d, 2), out_specs=_act_specs(ts, d, 4),
        out_shape=[jax.ShapeDtypeStruct((bl, s, d), BF16)] * 4, compiler_params=_cp(2),
    )(dm3, mg3, mg3, pr3, pa3)


def _ffn_conv(gf, cw, cb):
    kw = cw.shape[0]
    y = cb
    for k in range(kw):
        y = y + _shift_down(gf, kw - 1 - k) * cw[k:k + 1, :]
    return y


def _ffn_act_fwd(up3, cw, cb):
    bl, s, two = up3.shape
    dff = two // 2
    kw = cw.shape[0]
    tc = _tile(dff, 256, LANES)
    nc = dff // tc

    def kern(gf_ref, uf_ref, cw_ref, cb_ref, o_ref):
        o_ref[0] = (_gelu(_ffn_conv(gf_ref[0], cw_ref[...], cb_ref[...])) * uf_ref[0]).astype(BF16)

    act = lambda off: pl.BlockSpec((1, s, tc), lambda b, j, off=off: (b, 0, off + j))
    return pl.pallas_call(
        kern, name="ffn_act_fwd", grid=(bl, nc),
        in_specs=[act(0), act(nc), pl.BlockSpec((kw, tc), lambda b, j: (0, j)), pl.BlockSpec((1, tc), lambda b, j: (0, j))],
        out_specs=act(0), out_shape=jax.ShapeDtypeStruct((bl, s, dff), BF16), compiler_params=_cp(2),
    )(up3, up3, cw, cb)


def _ffn_act_bwd(up3, dact3, cw, cb):
    bl, s, two = up3.shape
    dff = two // 2
    kw = cw.shape[0]
    tc = _tile(dff, 256, LANES)
    nc = dff // tc

    def kern(gf_ref, uf_ref, da_ref, cw_ref, cb_ref, dgf_ref, duf_ref, dcw_ref, dcb_ref):
        b = pl.program_id(1)
        gf, cwv, da = gf_ref[0], cw_ref[...], da_ref[0]
        ge, dge = _gelu_and_grad(_ffn_conv(gf, cwv, cb_ref[...]))
        duf_ref[0] = (da * ge).astype(BF16)
        dgc = da * uf_ref[0] * dge
        dgf = jnp.zeros_like(dgc)
        rows = []
        for k in range(kw):
            dgf = dgf + _shift_up(dgc, kw - 1 - k) * cwv[k:k + 1, :]
            rows.append(jnp.sum(dgc * _shift_down(gf, kw - 1 - k), axis=0, keepdims=True))
        dgf_ref[0] = dgf.astype(BF16)

        @pl.when(b == 0)
        def _():
            dcw_ref[...] = jnp.zeros_like(dcw_ref)
            dcb_ref[...] = jnp.zeros_like(dcb_ref)

        for k in range(kw):
            dcw_ref[k:k + 1, :] += rows[k]
        dcb_ref[...] += jnp.sum(dgc, axis=0, keepdims=True)

    act = lambda off: pl.BlockSpec((1, s, tc), lambda j, b, off=off: (b, 0, off + j))
    cws = pl.BlockSpec((kw, tc), lambda j, b: (0, j))
    cbs = pl.BlockSpec((1, tc), lambda j, b: (0, j))
    return pl.pallas_call(
        kern, name="ffn_act_bwd", grid=(nc, bl),
        in_specs=[act(0), act(nc), act(0), cws, cbs], out_specs=[act(0), act(0), cws, cbs],
        out_shape=[jax.ShapeDtypeStruct((bl, s, dff), BF16), jax.ShapeDtypeStruct((bl, s, dff), BF16),
                   jax.ShapeDtypeStruct((kw, dff), F32), jax.ShapeDtypeStruct((1, dff), F32)],
        compiler_params=_cp(2),
    )(up3, up3, dact3, cw, cb)


_HBM = pl.BlockSpec(memory_space=pltpu.HBM)


def _place():
    x, y, c = lax.axis_index("x"), lax.axis_index("y"), lax.axis_index("c")
    chips = dict(me=2 * x + y, nx=2 * (1 - x) + y, ny=2 * x + (1 - y), diag=2 * (1 - x) + (1 - y))
    peers = dict(nx=(1 - x, y, c), ny=(x, 1 - y, c), sib=(x, y, 1 - c))
    return c, chips, peers


def _remote(src, dst, sems, k, to):
    return pltpu.make_async_remote_copy(src_ref=src, dst_ref=dst, send_sem=sems[0].at[k], recv_sem=sems[1].at[k],
                                        device_id=to, device_id_type=MESH)


def _all_gather_chips(x4, name):
    def body(x_ref, out_ref, send_sems, recv_sems, local_sem):
        c, chip, peer = _place()
        sems = (send_sems, recv_sems)
        me, nx, ny, dg = chip["me"], chip["nx"], chip["ny"], chip["diag"]
        mine = pltpu.make_async_copy(x_ref, out_ref.at[me], local_sem)
        mine.start()
        own = x_ref.at[c]
        sends = [_remote(own, out_ref.at[me, c], sems, 0, peer["nx"]), _remote(own, out_ref.at[me, c], sems, 1, peer["ny"])]
        for cp in sends:
            cp.start()

        def arrive(k, dst, frm):
            _remote(dst, dst, sems, k, frm).wait_recv()

        def pass_on(k, blk, to):
            cp = _remote(blk, blk, sems, k, to)
            cp.start()
            sends.append(cp)

        arrive(0, out_ref.at[nx, c], peer["nx"])
        pass_on(2, out_ref.at[nx, c, 0], peer["ny"])
        pass_on(4, out_ref.at[nx, c], peer["sib"])
        arrive(1, out_ref.at[ny, c], peer["ny"])
        pass_on(3, out_ref.at[ny, c, 1], peer["nx"])
        pass_on(5, out_ref.at[ny, c], peer["sib"])
        arrive(2, out_ref.at[dg, c, 0], peer["ny"])
        pass_on(6, out_ref.at[dg, c, 0], peer["sib"])
        arrive(3, out_ref.at[dg, c, 1], peer["nx"])
        pass_on(7, out_ref.at[dg, c, 1], peer["sib"])
        arrive(4, out_ref.at[nx, 1 - c], peer["sib"])
        arrive(5, out_ref.at[ny, 1 - c], peer["sib"])
        arrive(6, out_ref.at[dg, 1 - c, 0], peer["sib"])
        arrive(7, out_ref.at[dg, 1 - c, 1], peer["sib"])
        for cp in sends:
            cp.wait_send()
        mine.wait()

    return pl.pallas_call(
        body, name=name, in_specs=[_HBM], out_specs=_HBM,
        out_shape=jax.ShapeDtypeStruct((N_CHIPS,) + x4.shape, x4.dtype),
        scratch_shapes=[pltpu.SemaphoreType.DMA((8,)), pltpu.SemaphoreType.DMA((8,)), pltpu.SemaphoreType.DMA],
    )(x4)


def _rs_sibling(g5):
    n, _, _, r, cc = g5.shape

    def body(g_ref, out_ref, send_sems, recv_sems):
        c, _, peer = _place()
        sems = (send_sems, recv_sems)
        cps = [_remote(g_ref.at[j, 1 - c], out_ref.at[j], sems, j, peer["sib"]) for j in range(n)]
        for cp in cps:
            cp.start()
        for cp in cps:
            cp.wait()

    return pl.pallas_call(
        body, name="rs_sibling", in_specs=[_HBM], out_specs=_HBM,
        out_shape=jax.ShapeDtypeStruct((n, 2, r, cc), g5.dtype),
        scratch_shapes=[pltpu.SemaphoreType.DMA((n,)), pltpu.SemaphoreType.DMA((n,))],
    )(g5)


def _rs_first(p0):
    n, _, r, cc = p0.shape

    def body(p_ref, out_ref, send_sems, recv_sems):
        _, chip, peer = _place()
        sems = (send_sems, recv_sems)
        cps = [_remote(p_ref.at[chip["nx"], 0], out_ref.at[0], sems, 0, peer["nx"]),
               _remote(p_ref.at[chip["diag"], 0], out_ref.at[1], sems, 1, peer["nx"]),
               _remote(p_ref.at[chip["ny"], 1], out_ref.at[2], sems, 2, peer["ny"]),
               _remote(p_ref.at[chip["diag"], 1], out_ref.at[3], sems, 3, peer["ny"])]
        for cp in cps:
            cp.start()
        for cp in cps:
            cp.wait()

    return pl.pallas_call(
        body, name="rs_first", in_specs=[_HBM], out_specs=_HBM,
        out_shape=jax.ShapeDtypeStruct((4, r, cc), p0.dtype),
        scratch_shapes=[pltpu.SemaphoreType.DMA((4,)), pltpu.SemaphoreType.DMA((4,))],
    )(p0)


def _rs_second(p1):
    _, r, cc = p1.shape

    def body(p_ref, out_ref, send_sems, recv_sems):
        _, _, peer = _place()
        sems = (send_sems, recv_sems)
        cps = [_remote(p_ref.at[1], out_ref.at[0], sems, 0, peer["ny"]),
               _remote(p_ref.at[3], out_ref.at[1], sems, 1, peer["nx"])]
        for cp in cps:
            cp.start()
        for cp in cps:
            cp.wait()

    return pl.pallas_call(
        body, name="rs_second", in_specs=[_HBM], out_specs=_HBM,
        out_shape=jax.ShapeDtypeStruct((2, r, cc), p1.dtype),
        scratch_shapes=[pltpu.SemaphoreType.DMA((2,)), pltpu.SemaphoreType.DMA((2,))],
    )(p1)


def _rs_last(p2):
    _, r, cc = p2.shape

    def body(p_ref, out_ref, send_sems, recv_sems, local_sem):
        c, _, peer = _place()
        mine = pltpu.make_async_copy(p_ref, out_ref.at[c], local_sem)
        mine.start()
        cp = _remote(p_ref, out_ref.at[c], (send_sems, recv_sems), 0, peer["sib"])
        cp.start()
        cp.wait_send()
        _remote(p_ref, out_ref.at[1 - c], (send_sems, recv_sems), 0, peer["sib"]).wait_recv()
        mine.wait()

    return pl.pallas_call(
        body, name="rs_last", in_specs=[_HBM], out_specs=_HBM,
        out_shape=jax.ShapeDtypeStruct((2, 2, r, cc), p2.dtype),
        scratch_shapes=[pltpu.SemaphoreType.DMA((1,)), pltpu.SemaphoreType.DMA((1,)), pltpu.SemaphoreType.DMA],
    )(p2)


def _add_sibling(g5, got, c_idx, tr):
    n, _, _, r, cc = g5.shape

    def kern(c_ref, a_ref, b_ref, o_ref):
        o_ref[0, 0] = a_ref[0, 0, 0] + b_ref[0, 0]

    return pl.pallas_call(
        kern, name="rs_add_sibling",
        grid_spec=pltpu.PrefetchScalarGridSpec(
            num_scalar_prefetch=1, grid=(n, 2, r // tr),
            in_specs=[pl.BlockSpec((1, 1, 1, tr, cc), lambda j, h, t, c_ref: (j, c_ref[0], h, t, 0)),
                      pl.BlockSpec((1, 1, tr, cc), lambda j, h, t, c_ref: (j, h, t, 0))],
            out_specs=pl.BlockSpec((1, 1, tr, cc), lambda j, h, t, c_ref: (j, h, t, 0))),
        out_shape=jax.ShapeDtypeStruct((n, 2, r, cc), F32),
        compiler_params=_cp(3),
    )(c_idx, g5, got)


def _add_first(p0, got, chip_idx, tr):
    _, _, r, cc = p0.shape

    def kern(i_ref, a_ref, b_ref, o_ref):
        o_ref[0] = a_ref[0, 0] + b_ref[0]

    return pl.pallas_call(
        kern, name="rs_add_first",
        grid_spec=pltpu.PrefetchScalarGridSpec(
            num_scalar_prefetch=1, grid=(4, r // tr),
            in_specs=[pl.BlockSpec((1, 1, tr, cc), lambda s, t, i_ref: (i_ref[s], s // 2, t, 0)),
                      pl.BlockSpec((1, tr, cc), lambda s, t, i_ref: (s, t, 0))],
            out_specs=pl.BlockSpec((1, tr, cc), lambda s, t, i_ref: (s, t, 0))),
        out_shape=jax.ShapeDtypeStruct((4, r, cc), F32),
        compiler_params=_cp(2),
    )(chip_idx, p0, got)


def _add_second(p1, got, tr):
    _, r, cc = p1.shape

    def kern(a_ref, b_ref, o_ref):
        o_ref[...] = a_ref[...] + b_ref[...]

    return pl.pallas_call(
        kern, name="rs_add_second", grid=(2, r // tr),
        in_specs=[pl.BlockSpec((1, tr, cc), lambda h, t: (2 * h, t, 0)), pl.BlockSpec((1, tr, cc), lambda h, t: (h, t, 0))],
        out_specs=pl.BlockSpec((1, tr, cc), lambda h, t: (h, t, 0)),
        out_shape=jax.ShapeDtypeStruct((2, r, cc), F32),
        compiler_params=_cp(2),
    )(p1, got)


def _reduce_scatter_chips(g5, tr):
    x, y, c = lax.axis_index("x"), lax.axis_index("y"), lax.axis_index("c")
    me, nx, ny = 2 * x + y, 2 * (1 - x) + y, 2 * x + (1 - y)
    p0 = _add_sibling(g5, _rs_sibling(g5), jnp.reshape(c, (1,)).astype(jnp.int32), tr)
    p1 = _add_first(p0, _rs_first(p0), jnp.stack([me, ny, me, nx]).astype(jnp.int32), tr)
    p2 = _add_second(p1, _rs_second(p1), tr)
    return _rs_last(p2)


def _adamw(g, w, m, v, tr, name):
    rows, cc = g.shape
    k1 = 1.0 - ADAM_B1 ** ADAM_STEP
    k2 = 1.0 - ADAM_B2 ** ADAM_STEP

    def kern(g_ref, w_ref, m_ref, v_ref, d_ref, nm_ref, nv_ref):
        gv = g_ref[...]
        nm = ADAM_B1 * m_ref[...] + (1.0 - ADAM_B1) * gv
        nv = ADAM_B2 * v_ref[...] + (1.0 - ADAM_B2) * (gv * gv)
        nm_ref[...] = nm
        nv_ref[...] = nv
        d_ref[...] = -ADAM_LR * ((nm / k1) / (jnp.sqrt(nv / k2) + ADAM_EPS) + ADAM_WD * w_ref[...])

    spec = pl.BlockSpec((tr, cc), lambda t: (t, 0))
    return pl.pallas_call(
        kern, name=name, grid=(rows // tr,), in_specs=[spec] * 4, out_specs=[spec] * 3,
        out_shape=[jax.ShapeDtypeStruct((rows, cc), F32)] * 3, compiler_params=_cp(1),
    )(g, w, m, v)


def _flat_pad(parts, total):
    flat = jnp.concatenate([p.reshape(-1) for p in parts])
    return jnp.pad(flat, (0, total - flat.shape[0]))


def _chunks_of(full, col_sharded):
    r, n = full.shape
    if col_sharded:
        return full.reshape(r, N_CHIPS, n // N_CHIPS).transpose(1, 0, 2).reshape(N_CHIPS, -1)
    return full.reshape(N_CHIPS, -1)


def _full_of(chunks, shard_shape, col_sharded):
    r, n = shard_shape
    if col_sharded:
        return chunks.reshape(N_CHIPS, r, n).transpose(1, 0, 2).reshape(r, N_CHIPS * n)
    return chunks.reshape(N_CHIPS * r, n)


def _split_flat(flat, shapes):
    out, pos = [], 0
    for shp in shapes:
        size = math.prod(shp)
        out.append(flat[pos:pos + size].reshape(shp))
        pos += size
    return out


_WEIGHTS = ['w_ada', 'b_ada', 'g_norm1', 'w_in', 'w_rnn_conv', 'b_rnn_conv', 'w_lru_a', 'b_lru_a', 'w_lru_i', 'b_lru_i',
            'lru_lambda', 'b_fgate', 'w_proj_rnn', 'w_proj_attn', 'w_out', 'g_norm2', 'w_ffn_up', 'w_ffn_conv',
            'b_ffn_conv', 'w_ffn_down', 'w_ada_final', 'b_ada_final', 'g_final']
_SHARDED = [('w_ada', True), ('w_in', True), ('w_rnn_conv', True), ('w_proj_rnn', False), ('w_proj_attn', False),
            ('w_out', False), ('w_ffn_up', True), ('w_ffn_conv', True), ('w_ffn_down', False), ('w_ada_final', True)]
_MATMUL = ['w_ada', 'w_in', 'w_proj_rnn', 'w_proj_attn', 'w_out', 'w_ffn_up', 'w_ffn_down', 'w_ada_final']
_CONV = ['w_rnn_conv', 'w_ffn_conv']
_REPLICATED = [n for n in _WEIGHTS if n not in dict(_SHARDED)]


def kernel(x, c, w_ada, b_ada, g_norm1, w_in, w_rnn_conv, b_rnn_conv, w_lru_a, b_lru_a, w_lru_i, b_lru_i, lru_lambda, b_fgate, w_proj_rnn, w_proj_attn, w_out, g_norm2, w_ffn_up, w_ffn_conv, b_ffn_conv, w_ffn_down, w_ada_final, b_ada_final, g_final, loss_target, m_w_ada, m_b_ada, m_g_norm1, m_w_in, m_w_rnn_conv, m_b_rnn_conv, m_w_lru_a, m_b_lru_a, m_w_lru_i, m_b_lru_i, m_lru_lambda, m_b_fgate, m_w_proj_rnn, m_w_proj_attn, m_w_out, m_g_norm2, m_w_ffn_up, m_w_ffn_conv, m_b_ffn_conv, m_w_ffn_down, m_w_ada_final, m_b_ada_final, m_g_final, v_w_ada, v_b_ada, v_g_norm1, v_w_in, v_w_rnn_conv, v_b_rnn_conv, v_w_lru_a, v_b_lru_a, v_w_lru_i, v_b_lru_i, v_lru_lambda, v_b_fgate, v_w_proj_rnn, v_w_proj_attn, v_w_out, v_g_norm2, v_w_ffn_up, v_w_ffn_conv, v_b_ffn_conv, v_w_ffn_down, v_w_ada_final, v_b_ada_final, v_g_final):
    args = locals()
    is_col = dict(_SHARDED)
    shape_of = {n: args[n].shape for n in _WEIGHTS}
    w2 = {}
    for n in _WEIGHTS:
        a = args[n]
        if n in ('w_lru_a', 'w_lru_i'):
            w2[n] = a[0]
        elif a.ndim == 3:
            w2[n] = a[0]
        elif a.ndim == 1:
            w2[n] = a[None, :]
        elif n in ('w_ada_final',):
            w2[n] = a
        else:
            w2[n] = a
    view = lambda prefix: {n: args[prefix + n].reshape(w2[n].shape) for n in _WEIGHTS}
    m2, v2 = view('m_'), view('v_')

    bl, s, d = x.shape
    t = bl * s
    nh = b_fgate.shape[-1]
    nb, rb = w_lru_a.shape[1], w_lru_a.shape[2]
    dr = nb * rb
    da = w2['w_proj_attn'].shape[0] * N_CHIPS
    dh = da // nh
    dff = w2['w_ffn_conv'].shape[1] * N_CHIPS
    scale = dh ** -0.5

    n_mat = sum(w2[n].size for n in _MATMUL)
    r_mat, _ = _nice_rows(-(-n_mat // (4 * FLAT_COLS)), 16)
    mat_local = _flat_pad([w2[n] for n in _MATMUL], 4 * r_mat * FLAT_COLS).astype(BF16)
    mat_all = _all_gather_chips(mat_local.reshape(2, 2, r_mat, FLAT_COLS), "ag_weights").reshape(N_CHIPS, -1)
    n_conv = sum(w2[n].size for n in _CONV)
    r_conv, _ = _nice_rows(-(-n_conv // (4 * FLAT_COLS)), 8)
    conv_local = _flat_pad([w2[n] for n in _CONV], 4 * r_conv * FLAT_COLS)
    conv_all = _all_gather_chips(conv_local.reshape(2, 2, r_conv, FLAT_COLS), "ag_conv").reshape(N_CHIPS, -1)

    def gathered(all_chunks, names):
        out, pos = {}, 0
        for n in names:
            size = w2[n].size
            out[n] = _full_of(all_chunks[:, pos:pos + size], w2[n].shape, is_col[n])
            pos += size
        return out

    wf = gathered(mat_all, _MATMUL)
    wf.update(gathered(conv_all, _CONV))

    o_gr, o_q, o_k, o_v, o_fl = dr, 2 * dr, 2 * dr + da, 2 * dr + 2 * da, 2 * dr + 3 * da
    o_mg = o_fl + nh
    w_in_f = wf['w_in']
    w_rnn, w_q, w_kv = w_in_f[:, :o_q], w_in_f[:, o_q:o_k], w_in_f[:, o_k:o_fl]
    w_fl = jnp.pad(w_in_f[:, o_fl:o_mg], ((0, 0), (0, LANES - nh)))
    w_mg = w_in_f[:, o_mg:]
    bf_pad = jnp.pad(w2['b_fgate'], ((0, 0), (0, LANES - nh)))

    c_act = _silu_pad(c, 16)
    mod = _mm(c_act, wf['w_ada'], "nn", name="ada_fwd", bias=w2['b_ada'])[:bl]
    sh1, sc1, gt1, sh2, sc2, gt2 = [mod[:, i * d:(i + 1) * d].reshape(bl, 1, d) for i in range(6)]
    modf = _mm(c_act, wf['w_ada_final'], "nn", name="ada_final_fwd", bias=w2['b_ada_final'])[:bl]
    shf, scf = modf[:, :d].reshape(bl, 1, d), modf[:, d:].reshape(bl, 1, d)

    h1 = _norm_mod_fwd(x, w2['g_norm1'], sh1, sc1)
    h1f = h1.reshape(t, d)
    zr = _mm(h1f, w_rnn, "nn", name="in_rnn").reshape(bl, s, 2 * dr)
    q3 = _mm(h1f, w_q, "nn", name="in_q", out_dtype=BF16, scale=scale).reshape(bl, s, da)
    kv3 = _mm(h1f, w_kv, "nn", name="in_kv", out_dtype=BF16).reshape(bl, s, 2 * da)
    mg3 = _mm(h1f, w_mg, "nn", name="in_mg").reshape(bl, s, 2 * d)
    zf3 = _mm(h1f, w_fl, "nn", name="in_fl").reshape(bl, s, LANES)

    lru = (wf['w_rnn_conv'], w2['b_rnn_conv'], w2['w_lru_a'], w2['b_lru_a'], w2['w_lru_i'], w2['b_lru_i'], w2['lru_lambda'])
    hseq, y_rnn = _rnn_fwd(zr, *lru)

    f3 = _fgate_fwd(zf3, bf_pad)
    f_heads = f3[:, :, :nh].transpose(0, 2, 1).reshape(bl * nh, s)
    fcol = jnp.broadcast_to(f_heads[:, :, None], (bl * nh, s, LANES))
    frow = f_heads.reshape(bl * nh, 1, s)
    o3, lse = _attn_fwd(q3, kv3, fcol, frow, nh)

    pr3 = _mm(y_rnn.reshape(t, dr), wf['w_proj_rnn'], "nn", name="proj_rnn").reshape(bl, s, d)
    pa3 = _mm(o3.reshape(t, da), wf['w_proj_attn'], "nn", name="proj_attn").reshape(bl, s, d)
    merged = _merge_fwd(mg3, pr3, pa3)
    mo3 = _mm(merged.reshape(t, d), wf['w_out'], "nn", name="mix_out").reshape(bl, s, d)
    x1, h2 = _resid_norm_fwd(x, mo3, gt1, w2['g_norm2'], sh2, sc2)
    up3 = _mm(h2.reshape(t, d), wf['w_ffn_up'], "nn", name="ffn_up").reshape(bl, s, 2 * dff)
    act3 = _ffn_act_fwd(up3, wf['w_ffn_conv'], w2['b_ffn_conv'])
    yf3 = _mm(act3.reshape(t, dff), wf['w_ffn_down'], "nn", name="ffn_down").reshape(bl, s, d)

    dx2, dshf, dscf, dg_final, loss_part = _final_fwd_bwd(x1, yf3, gt2, w2['g_final'], shf, scf, loss_target)
    loss = lax.psum(loss_part[0, 0], ("x", "y", "c"))

    dyf, dgt2 = _gate_bwd(dx2, yf3, gt2, "ffn_gate_bwd")
    dyf_f = dyf.reshape(t, d)
    g_ffn_down = _mm(act3.reshape(t, dff), dyf_f, "tn", name="dw_ffn_down")
    dact3 = _mm(dyf_f, wf['w_ffn_down'], "nt", name="d_ffn_act").reshape(bl, s, dff)
    dgf, duf, g_ffn_conv, g_b_ffn_conv = _ffn_act_bwd(up3, dact3, wf['w_ffn_conv'], w2['b_ffn_conv'])
    dup = jnp.concatenate([dgf, duf], axis=-1).reshape(t, 2 * dff)
    g_ffn_up = _mm(h2.reshape(t, d), dup, "tn", name="dw_ffn_up")
    dh2 = _mm(dup, wf['w_ffn_up'], "nt", name="d_h2").reshape(bl, s, d)
    dx1, dsh2, dsc2, dg_norm2 = _norm_mod_bwd(dh2, x1, dx2, w2['g_norm2'], sc2, "norm2_bwd")

    dmo, dgt1 = _gate_bwd(dx1, mo3, gt1, "mix_gate_bwd")
    dmo_f = dmo.reshape(t, d)
    g_out = _mm(merged.reshape(t, d), dmo_f, "tn", name="dw_out")
    dm3 = _mm(dmo_f, wf['w_out'], "nt", name="d_merged").reshape(bl, s, d)
    dpr, dpa, dmr, dma = _merge_bwd(dm3, mg3, pr3, pa3)
    g_proj_rnn = _mm(y_rnn.reshape(t, dr), dpr.reshape(t, d), "tn", name="dw_proj_rnn")
    g_proj_attn = _mm(o3.reshape(t, da), dpa.reshape(t, d), "tn", name="dw_proj_attn")
    dyr3 = _mm(dpr.reshape(t, d), wf['w_proj_rnn'], "nt", name="d_y_rnn").reshape(bl, s, dr)
    do3 = _mm(dpa.reshape(t, d), wf['w_proj_attn'], "nt", name="d_y_attn").reshape(bl, s, da)

    delta = _attn_delta(q3, kv3, do3, lse, fcol, frow, nh)
    dq3 = _attn_bwd_dq(q3, kv3, do3, lse, delta, fcol, frow, nh, scale)
    lse_row = lse[:, :, 0].reshape(bl * nh, 1, s)
    delta_row = delta[:, :, 0].reshape(bl * nh, 1, s)
    dk3, dv3, dfk = _attn_bwd_dkv(q3, kv3, do3, lse_row, delta_row, fcol, frow, nh)
    df3 = jnp.pad(dfk[:, :, 0].reshape(bl, nh, s).transpose(0, 2, 1), ((0, 0), (0, 0), (0, LANES - nh)))
    dzf3, g_bf = _fgate_bwd(df3, zf3, bf_pad)

    dxr, dgr, g_rnn_conv, g_b_rnn_conv, g_lru_a, g_b_lru_a, g_lru_i, g_b_lru_i, g_lam = _rnn_bwd(zr, hseq, dyr3, *lru)

    dz_main = jnp.concatenate([dxr, dgr, dq3, dk3, dv3, dmr, dma], axis=-1).reshape(t, o_fl + 2 * d)
    w_main = jnp.concatenate([w_in_f[:, :o_fl], w_mg], axis=1)
    dzf_f = dzf3.reshape(t, LANES)
    g_in_main = _mm(h1f, dz_main, "tn", name="dw_in_main")
    g_in_fl = _mm(h1f, dzf_f, "tn", name="dw_in_fl")
    dh1 = _mm(dzf_f, w_fl, "nt", name="d_h1_fl")
    dh1 = _mm(dz_main, w_main, "nt", name="d_h1", add=dh1).reshape(bl, s, d)
    grad_x, dsh1, dsc1, dg_norm1 = _norm_mod_bwd(dh1, x, dx1, w2['g_norm1'], sc1, "norm1_bwd")
    g_in = jnp.concatenate([g_in_main[:, :o_fl], g_in_fl[:, :nh], g_in_main[:, o_fl:]], axis=1)

    pad_rows = lambda a: jnp.pad(a.reshape(bl, -1), ((0, 16 - bl), (0, 0)))
    dmod = pad_rows(jnp.concatenate([dsh1, dsc1, dgt1, dsh2, dsc2, dgt2], axis=-1))
    dmodf = pad_rows(jnp.concatenate([dshf, dscf], axis=-1))
    g_ada = _mm(c_act, dmod, "tn", name="dw_ada")
    g_ada_final = _mm(c_act, dmodf, "tn", name="dw_ada_final")
    g_b_ada = _rowsum(dmod, "db_ada")
    g_b_ada_final = _rowsum(dmodf, "db_ada_final")

    grads = dict(w_ada=g_ada, b_ada=g_b_ada, g_norm1=dg_norm1, w_in=g_in, w_rnn_conv=g_rnn_conv, b_rnn_conv=g_b_rnn_conv,
                 w_lru_a=g_lru_a, b_lru_a=g_b_lru_a, w_lru_i=g_lru_i, b_lru_i=g_b_lru_i, lru_lambda=g_lam,
                 b_fgate=g_bf[:, :nh], w_proj_rnn=g_proj_rnn, w_proj_attn=g_proj_attn, w_out=g_out, g_norm2=dg_norm2,
                 w_ffn_up=g_ffn_up, w_ffn_conv=g_ffn_conv, b_ffn_conv=g_b_ffn_conv, w_ffn_down=g_ffn_down,
                 w_ada_final=g_ada_final, b_ada_final=g_b_ada_final, g_final=dg_final)

    sharded = [n for n, _ in _SHARDED]
    n_big = sum(w2[n].size for n in sharded)
    rows_big, tr_big = _nice_rows(-(-n_big // FLAT_COLS), 8)
    n_small = sum(w2[n].size for n in _REPLICATED)
    rows_q = -(-n_small // (N_CHIPS * FLAT_COLS * 32)) * 32
    r_rs, tr_rs = _nice_rows(-(-(rows_big + rows_q) // 4), 8)
    chunk = 4 * r_rs * FLAT_COLS
    big = jnp.concatenate([_chunks_of(grads[n], is_col[n]) for n in sharded], axis=1)
    big = jnp.pad(big, ((0, 0), (0, rows_big * FLAT_COLS - n_big)))
    small = _flat_pad([grads[n] for n in _REPLICATED], N_CHIPS * rows_q * FLAT_COLS).reshape(N_CHIPS, -1)
    g_all = jnp.concatenate([big, small], axis=1)
    g_all = jnp.pad(g_all, ((0, 0), (0, chunk - g_all.shape[1])))
    mine = _reduce_scatter_chips(g_all.reshape(N_CHIPS, 2, 2, r_rs, FLAT_COLS), tr_rs).reshape(-1, FLAT_COLS)
    g_big = mine[:rows_big]
    g_small_q = mine[rows_big:rows_big + rows_q].reshape(2, 2, rows_q // 4, FLAT_COLS)
    g_small = _all_gather_chips(g_small_q, "ag_small_grads").reshape(-1, FLAT_COLS)
    rows_small = N_CHIPS * rows_q
    tr_small = _tile(rows_small, 640, 8)

    flat_big = lambda src: _flat_pad([src[n] for n in sharded], rows_big * FLAT_COLS).reshape(rows_big, FLAT_COLS)
    flat_small = lambda src: _flat_pad([src[n] for n in _REPLICATED], rows_small * FLAT_COLS).reshape(rows_small, FLAT_COLS)
    big_out = _adamw(g_big, flat_big(w2), flat_big(m2), flat_big(v2), tr_big, "adamw_sharded")
    small_out = _adamw(g_small, flat_small(w2), flat_small(m2), flat_small(v2), tr_small, "adamw_replicated")

    results = []
    for big_a, small_a in zip((g_big,) + tuple(big_out), (g_small,) + tuple(small_out)):
        parts = dict(zip(sharded, _split_flat(big_a.reshape(-1), [w2[n].shape for n in sharded])))
        parts.update(zip(_REPLICATED, _split_flat(small_a.reshape(-1), [w2[n].shape for n in _REPLICATED])))
        results.append([parts[n].reshape(shape_of[n]) for n in _WEIGHTS])
    return (loss, grad_x, *results[0], *results[1], *results[2], *results[3])
```

```python
import functools
import math

import jax
import jax.numpy as jnp
from jax import lax
from jax.experimental import pallas as pl
from jax.experimental.pallas import tpu as pltpu

F32 = jnp.float32
BF16 = jnp.bfloat16
MESH = pl.DeviceIdType.MESH

RMS_EPS = 1e-6
LRU_C = 8.0
ADAM_LR = 0.001
ADAM_B1 = 0.9
ADAM_B2 = 0.999
ADAM_EPS = 1e-08
ADAM_WD = 0.01
ADAM_STEP = 10

LANES = 128
SUBLANES = 8
N_CHIPS = 4
FLAT_COLS = 1024
SCAN_SEGMENTS = SUBLANES
VMEM_LIMIT = 48 * 1024 * 1024
NEG_BIG = -1e30


def _cp(n_axes):
    return pltpu.CompilerParams(dimension_semantics=("arbitrary",) * n_axes, vmem_limit_bytes=VMEM_LIMIT)


def _tile(n, target, align):
    if n <= target:
        return n
    t = (target // align) * align
    while t >= align:
        if n % t == 0:
            return t
        t -= align
    return n


def _nice_rows(n, align):
    r = -(-n // align) * align
    while True:
        if r <= 640:
            return r, r
        t = _tile(r, 640, align)
        if 128 <= t <= 640:
            return r, t
        r += align


def _sigmoid(x):
    return jax.nn.sigmoid(x)


def _softplus(x):
    return jnp.maximum(x, 0.0) + jnp.log1p(jnp.exp(-jnp.abs(x)))


def _expm1(x):
    small = x * (1.0 + 0.5 * x * (1.0 + (1.0 / 3.0) * x * (1.0 + 0.25 * x)))
    return jnp.where(jnp.abs(x) < 0.05, small, jnp.exp(x) - 1.0)


_GELU_K = math.sqrt(2.0 / math.pi)
_GELU_C = 0.044715


def _gelu(x):
    t = jnp.tanh(_GELU_K * (x + _GELU_C * x * x * x))
    return 0.5 * x * (1.0 + t)


def _gelu_and_grad(x):
    t = jnp.tanh(_GELU_K * (x + _GELU_C * x * x * x))
    g = 0.5 * x * (1.0 + t)
    dg = 0.5 * (1.0 + t) + 0.5 * x * (1.0 - t * t) * _GELU_K * (1.0 + 3.0 * _GELU_C * x * x)
    return g, dg


def _shift_down(x, k):
    if k == 0:
        return x
    rows = lax.broadcasted_iota(jnp.int32, x.shape, 0)
    return jnp.where(rows >= k, pltpu.roll(x, k, 0), 0.0)


def _shift_up(x, k):
    if k == 0:
        return x
    s = x.shape[0]
    rows = lax.broadcasted_iota(jnp.int32, x.shape, 0)
    return jnp.where(rows < s - k, pltpu.roll(x, s - k, 0), 0.0)


def _dot(a, b, dims):
    return lax.dot_general(a.astype(BF16), b.astype(BF16), (dims, ((), ())), preferred_element_type=F32)


_NN = ((1,), (0,))
_NT = ((1,), (1,))
_TN = ((0,), (0,))


def _mm(a, b, mode, *, name, out_dtype=F32, scale=None, bias=None, add=None, tm=1024, tn=1024, tk=512,
        b_chunk=None, out_chunk=None):
    pieces = list(a) if isinstance(a, (list, tuple)) else [a]
    ksize = lambda p: p.shape[0] if mode == "tn" else p.shape[1]
    if b_chunk is None:
        brows, bcols = b.shape
    else:
        brows, bcols = b.shape[1], b.shape[0] * b_chunk
    k = sum(ksize(p) for p in pieces)
    if mode == "nt":
        m, n = pieces[0].shape[0], brows
        assert bcols == k, (bcols, k)
    else:
        m, n = (pieces[0].shape[1] if mode == "tn" else pieces[0].shape[0]), bcols
        assert brows == k, (brows, k)
    tm = _tile(m, tm, LANES)
    ncut = n
    if b_chunk is not None and mode != "nt":
        ncut = b_chunk
    if out_chunk is not None:
        ncut = math.gcd(ncut, out_chunk)
    tn = _tile(ncut, tn, LANES)
    kcut = b_chunk if (b_chunk is not None and mode == "nt") else k
    for p in pieces:
        kcut = math.gcd(kcut, ksize(p))
    tk = _tile(kcut, tk, LANES)
    nk = k // tk
    dims = {"nn": _NN, "nt": _NT, "tn": _TN}[mode]
    counts = [ksize(p) // tk for p in pieces]
    starts = [sum(counts[:i]) for i in range(len(pieces))]
    n_pieces = len(pieces)

    def a_spec(s0, cnt):
        kmap = (lambda kk: kk) if n_pieces == 1 else (lambda kk: jnp.clip(kk - s0, 0, cnt - 1))
        if mode == "tn":
            return pl.BlockSpec((tk, tm), lambda i, j, kk: (kmap(kk), i))
        return pl.BlockSpec((tm, tk), lambda i, j, kk: (i, kmap(kk)))

    if b_chunk is None:
        if mode == "nt":
            b_spec = pl.BlockSpec((tn, tk), lambda i, j, kk: (j, kk))
        else:
            b_spec = pl.BlockSpec((tk, tn), lambda i, j, kk: (kk, j))
    elif mode == "nt":
        per_b = b_chunk // tk
        b_spec = pl.BlockSpec((None, tn, tk), lambda i, j, kk: (kk // per_b, j, kk % per_b))
    else:
        per_b = b_chunk // tn
        b_spec = pl.BlockSpec((None, tk, tn), lambda i, j, kk: (j // per_b, kk, j % per_b))
    if out_chunk is None:
        out_spec = pl.BlockSpec((tm, tn), lambda i, j, kk: (i, j))
        out_shape = jax.ShapeDtypeStruct((m, n), out_dtype)
    else:
        per_o = out_chunk // tn
        out_spec = pl.BlockSpec((None, tm, tn), lambda i, j, kk: (j // per_o, i, j % per_o))
        out_shape = jax.ShapeDtypeStruct((n // out_chunk, m, out_chunk), out_dtype)
    in_specs = [a_spec(s0, cnt) for s0, cnt in zip(starts, counts)] + [b_spec]
    args = pieces + [b]
    if bias is not None:
        in_specs.append(pl.BlockSpec((1, tn), lambda i, j, kk: (0, j)))
        args.append(bias)
    if add is not None:
        in_specs.append(pl.BlockSpec((tm, tn), lambda i, j, kk: (i, j)))
        args.append(add)

    def kern(*refs):
        b_ref = refs[n_pieces]
        o_ref, acc = refs[-2], refs[-1]
        kk = pl.program_id(2)

        @pl.when(kk == 0)
        def _():
            acc[...] = jnp.zeros_like(acc)

        if n_pieces == 1:
            acc[...] += _dot(refs[0][...], b_ref[...], dims)
        else:
            for idx in range(n_pieces):
                @pl.when((kk >= starts[idx]) & (kk < starts[idx] + counts[idx]))
                def _(idx=idx):
                    acc[...] += _dot(refs[idx][...], b_ref[...], dims)

        @pl.when(kk == nk - 1)
        def _():
            r = acc[...]
            if scale is not None:
                r = r * scale
            pos = n_pieces + 1
            if bias is not None:
                r = r + refs[pos][...]
                pos += 1
            if add is not None:
                r = r + refs[pos][...]
            o_ref[...] = r.astype(out_dtype)

    return pl.pallas_call(
        kern, name=name,
        grid=(m // tm, n // tn, nk),
        in_specs=in_specs, out_specs=out_spec, out_shape=out_shape,
        scratch_shapes=[pltpu.VMEM((tm, tn), F32)],
        compiler_params=_cp(3),
    )(*args)


def _silu_pad(c, rows):
    bl, d = c.shape

    def kern(c_ref, o_ref):
        o_ref[...] = jnp.zeros_like(o_ref)
        v = c_ref[...]
        o_ref[0:bl, :] = v * _sigmoid(v)

    return pl.pallas_call(kern, name="silu_pad", out_shape=jax.ShapeDtypeStruct((rows, d), F32))(c)


def _rowsum(x, name):
    r, n = x.shape

    def kern(x_ref, o_ref):
        o_ref[...] = jnp.sum(x_ref[...], axis=0, keepdims=True)

    return pl.pallas_call(kern, name=name, out_shape=jax.ShapeDtypeStruct((1, n), F32))(x)


def _norm_parts(x, g):
    r = lax.rsqrt(jnp.mean(x * x, axis=-1, keepdims=True) + RMS_EPS)
    xh = x * r
    return r, xh, xh * g


def _norm_bwd_parts(dh, xh, r, g, sc):
    n = xh * g
    dn = dh * (1.0 + sc)
    dxh = dn * g
    dx = r * (dxh - xh * jnp.mean(dxh * xh, axis=-1, keepdims=True))
    return dx, dh, dh * n, dn * xh


def _act_specs(ts, d, n):
    return [pl.BlockSpec((1, ts, d), lambda b, t: (b, t, 0)) for _ in range(n)]


def _vec_spec(d):
    return pl.BlockSpec((1, 1, d), lambda b, t: (b, 0, 0))


def _par_spec(d):
    return pl.BlockSpec((1, d), lambda b, t: (0, 0))


def _norm_mod_fwd(x3, g, sh, sc):
    bl, s, d = x3.shape
    ts = _tile(s, 512, SUBLANES)

    def kern(x_ref, g_ref, sh_ref, sc_ref, h_ref):
        _, _, n = _norm_parts(x_ref[0], g_ref[...])
        h_ref[0] = (n * (1.0 + sc_ref[0]) + sh_ref[0]).astype(BF16)

    return pl.pallas_call(
        kern, name="norm_mod_fwd", grid=(bl, s // ts),
        in_specs=_act_specs(ts, d, 1) + [_par_spec(d), _vec_spec(d), _vec_spec(d)],
        out_specs=_act_specs(ts, d, 1)[0],
        out_shape=jax.ShapeDtypeStruct((bl, s, d), BF16),
        compiler_params=_cp(2),
    )(x3, g, sh, sc)


def _resid_norm_fwd(x3, y3, gate, g, sh, sc):
    bl, s, d = x3.shape
    ts = _tile(s, 512, SUBLANES)

    def kern(x_ref, y_ref, gate_ref, g_ref, sh_ref, sc_ref, x1_ref, h_ref):
        x1 = x_ref[0] + gate_ref[0] * y_ref[0]
        x1_ref[0] = x1
        _, _, n = _norm_parts(x1, g_ref[...])
        h_ref[0] = (n * (1.0 + sc_ref[0]) + sh_ref[0]).astype(BF16)

    return pl.pallas_call(
        kern, name="resid_norm_fwd", grid=(bl, s // ts),
        in_specs=_act_specs(ts, d, 2) + [_vec_spec(d), _par_spec(d), _vec_spec(d), _vec_spec(d)],
        out_specs=_act_specs(ts, d, 2),
        out_shape=[jax.ShapeDtypeStruct((bl, s, d), F32), jax.ShapeDtypeStruct((bl, s, d), BF16)],
        compiler_params=_cp(2),
    )(x3, y3, gate, g, sh, sc)


def _norm_mod_bwd(dh3, x3, dres3, g, sc, name):
    bl, s, d = x3.shape
    ts = _tile(s, 512, SUBLANES)

    def kern(dh_ref, x_ref, dres_ref, g_ref, sc_ref, dx_ref, dsh_ref, dsc_ref, dg_ref):
        b, t = pl.program_id(0), pl.program_id(1)
        gv = g_ref[...]
        r, xh, _ = _norm_parts(x_ref[0], gv)
        dx, a, bb, cc = _norm_bwd_parts(dh_ref[0], xh, r, gv, sc_ref[0])
        dx_ref[0] = dres_ref[0] + dx

        @pl.when(t == 0)
        def _():
            dsh_ref[...] = jnp.zeros_like(dsh_ref)
            dsc_ref[...] = jnp.zeros_like(dsc_ref)

        @pl.when((t == 0) & (b == 0))
        def _():
            dg_ref[...] = jnp.zeros_like(dg_ref)

        dsh_ref[0] += jnp.sum(a, axis=0, keepdims=True)
        dsc_ref[0] += jnp.sum(bb, axis=0, keepdims=True)
        dg_ref[...] += jnp.sum(cc, axis=0, keepdims=True)

    return pl.pallas_call(
        kern, name=name, grid=(bl, s // ts),
        in_specs=_act_specs(ts, d, 3) + [_par_spec(d), _vec_spec(d)],
        out_specs=[_act_specs(ts, d, 1)[0], _vec_spec(d), _vec_spec(d), _par_spec(d)],
        out_shape=[jax.ShapeDtypeStruct((bl, s, d), F32), jax.ShapeDtypeStruct((bl, 1, d), F32),
                   jax.ShapeDtypeStruct((bl, 1, d), F32), jax.ShapeDtypeStruct((1, d), F32)],
        compiler_params=_cp(2),
    )(dh3, x3, dres3, g, sc)


def _gate_bwd(dx3, y3, gate, name):
    bl, s, d = dx3.shape
    ts = _tile(s, 512, SUBLANES)

    def kern(dx_ref, y_ref, gate_ref, dy_ref, dgate_ref):
        t = pl.program_id(1)
        dx = dx_ref[0]
        dy_ref[0] = (gate_ref[0] * dx).astype(BF16)

        @pl.when(t == 0)
        def _():
            dgate_ref[...] = jnp.zeros_like(dgate_ref)

        dgate_ref[0] += jnp.sum(dx * y_ref[0], axis=0, keepdims=True)

    return pl.pallas_call(
        kern, name=name, grid=(bl, s // ts),
        in_specs=_act_specs(ts, d, 2) + [_vec_spec(d)],
        out_specs=[_act_specs(ts, d, 1)[0], _vec_spec(d)],
        out_shape=[jax.ShapeDtypeStruct((bl, s, d), BF16), jax.ShapeDtypeStruct((bl, 1, d), F32)],
        compiler_params=_cp(2),
    )(dx3, y3, gate)


def _final_fwd_bwd(x1, yf, gate2, g, shf, scf, tgt):
    bl, s, d = x1.shape
    ts = _tile(s, 512, SUBLANES)

    def kern(x1_ref, yf_ref, gate_ref, g_ref, sh_ref, sc_ref, tgt_ref, dx_ref, dsh_ref, dsc_ref, dg_ref, loss_ref):
        b, t = pl.program_id(0), pl.program_id(1)
        gv, sc = g_ref[...], sc_ref[0]
        x2 = x1_ref[0] + gate_ref[0] * yf_ref[0]
        r, xh, n = _norm_parts(x2, gv)
        err = n * (1.0 + sc) + sh_ref[0] - tgt_ref[0]
        dx, a, bb, cc = _norm_bwd_parts(err * (1.0 / d), xh, r, gv, sc)
        dx_ref[0] = dx

        @pl.when(t == 0)
        def _():
            dsh_ref[...] = jnp.zeros_like(dsh_ref)
            dsc_ref[...] = jnp.zeros_like(dsc_ref)

        @pl.when((t == 0) & (b == 0))
        def _():
            dg_ref[...] = jnp.zeros_like(dg_ref)
            loss_ref[...] = jnp.zeros_like(loss_ref)

        dsh_ref[0] += jnp.sum(a, axis=0, keepdims=True)
        dsc_ref[0] += jnp.sum(bb, axis=0, keepdims=True)
        dg_ref[...] += jnp.sum(cc, axis=0, keepdims=True)
        tok = jnp.mean(err * err, axis=-1, keepdims=True)
        loss_ref[...] += 0.5 * jnp.sum(tok, axis=0, keepdims=True)

    return pl.pallas_call(
        kern, name="final_fwd_bwd", grid=(bl, s // ts),
        in_specs=_act_specs(ts, d, 2) + [_vec_spec(d), _par_spec(d), _vec_spec(d), _vec_spec(d)] + _act_specs(ts, d, 1),
        out_specs=[_act_specs(ts, d, 1)[0], _vec_spec(d), _vec_spec(d), _par_spec(d),
                   pl.BlockSpec((1, 1), lambda b, t: (0, 0))],
        out_shape=[jax.ShapeDtypeStruct((bl, s, d), F32), jax.ShapeDtypeStruct((bl, 1, d), F32),
                   jax.ShapeDtypeStruct((bl, 1, d), F32), jax.ShapeDtypeStruct((1, d), F32),
                   jax.ShapeDtypeStruct((1, 1), F32)],
        compiler_params=_cp(2),
    )(x1, yf, gate2, g, shf, scf, tgt)


def _rnn_gates(xr, cw, cb, wa, ba, wi, bi, lam):
    kw = cw.shape[0]
    xc = cb
    for k in range(kw):
        xc = xc + _shift_down(xr, kw - 1 - k) * cw[k:k + 1, :]
    r = _sigmoid(_dot(xc, wa, _NN) + ba)
    i = _sigmoid(_dot(xc, wi, _NN) + bi)
    sp = _softplus(-lam)
    log_a = -LRU_C * r * sp
    a = jnp.exp(log_a)
    mult = jnp.sqrt(-_expm1(2.0 * log_a))
    return xc, r, i, sp, a, mult


def _segment_scan(a_s, u_s, h_s, p_s, reverse):
    s, c = a_s.shape
    seg = s // SCAN_SEGMENTS

    def step(n, carry):
        t = (seg - 1 - n) if reverse else n
        h, p = carry
        av = a_s[pl.ds(t, SCAN_SEGMENTS, stride=seg), :]
        uv = u_s[pl.ds(t, SCAN_SEGMENTS, stride=seg), :]
        h = av * h + uv
        p = p * av
        h_s[pl.ds(t, SCAN_SEGMENTS, stride=seg), :] = h
        p_s[pl.ds(t, SCAN_SEGMENTS, stride=seg), :] = p
        return h, p

    lax.fori_loop(0, seg, step, (jnp.zeros((SCAN_SEGMENTS, c), F32), jnp.ones((SCAN_SEGMENTS, c), F32)))
    carry = jnp.zeros((1, c), F32)
    order = range(SCAN_SEGMENTS - 1, -1, -1) if reverse else range(SCAN_SEGMENTS)
    for j in order:
        rows = pl.ds(j * seg, seg)
        fixed = h_s[rows, :] + p_s[rows, :] * carry
        h_s[rows, :] = fixed
        carry = fixed[0:1, :] if reverse else fixed[seg - 1:seg, :]


def _rnn_specs(s, rb, nb):
    act = lambda off: pl.BlockSpec((1, s, rb), lambda b, n, off=off: (b, 0, off + n))
    par = pl.BlockSpec((1, rb), lambda b, n: (0, n))
    wsp = pl.BlockSpec((1, rb, rb), lambda b, n: (n, 0, 0))
    return act, par, wsp


def _rnn_fwd(zr3, cw, cb, wa, ba, wi, bi, lam):
    bl, s, two = zr3.shape
    nb, rb, _ = wa.shape
    dr = nb * rb
    kw = cw.shape[0]
    act, par, wsp = _rnn_specs(s, rb, nb)

    def kern(xr_ref, gr_ref, cw_ref, cb_ref, wa_ref, ba_ref, wi_ref, bi_ref, lam_ref, h_ref, y_ref, a_s, u_s, h_s, p_s):
        xc, r, i, sp, a, mult = _rnn_gates(xr_ref[0], cw_ref[...], cb_ref[...], wa_ref[0], ba_ref[...],
                                           wi_ref[0], bi_ref[...], lam_ref[...])
        a_s[...] = a
        u_s[...] = mult * (i * xc)
        _segment_scan(a_s, u_s, h_s, p_s, reverse=False)
        h = h_s[...]
        h_ref[0] = h
        y_ref[0] = (_gelu(gr_ref[0]) * h).astype(BF16)

    return pl.pallas_call(
        kern, name="rnn_fwd", grid=(bl, nb),
        in_specs=[act(0), act(nb), pl.BlockSpec((kw, rb), lambda b, n: (0, n)), par, wsp, par, wsp, par, par],
        out_specs=[act(0), act(0)],
        out_shape=[jax.ShapeDtypeStruct((bl, s, dr), F32), jax.ShapeDtypeStruct((bl, s, dr), BF16)],
        scratch_shapes=[pltpu.VMEM((s, rb), F32)] * 4,
        compiler_params=_cp(2),
    )(zr3, zr3, cw, cb, wa, ba, wi, bi, lam)


def _rnn_bwd(zr3, h3, dy3, cw, cb, wa, ba, wi, bi, lam):
    bl, s, _ = zr3.shape
    nb, rb, _ = wa.shape
    dr = nb * rb
    kw = cw.shape[0]
    act = lambda off: pl.BlockSpec((1, s, rb), lambda n, b, off=off: (b, 0, off + n))
    par = pl.BlockSpec((1, rb), lambda n, b: (0, n))
    wsp = pl.BlockSpec((1, rb, rb), lambda n, b: (n, 0, 0))
    cws = pl.BlockSpec((kw, rb), lambda n, b: (0, n))

    def kern(xr_ref, gr_ref, h_ref, dy_ref, cw_ref, cb_ref, wa_ref, ba_ref, wi_ref, bi_ref, lam_ref,
             dxr_ref, dgr_ref, dcw_ref, dcb_ref, dwa_ref, dba_ref, dwi_ref, dbi_ref, dlam_ref, a_s, u_s, h_s, p_s):
        b = pl.program_id(1)
        xr, cwv, lamv = xr_ref[0], cw_ref[...], lam_ref[...]
        wav, wiv = wa_ref[0], wi_ref[0]
        xc, r, i, sp, a, mult = _rnn_gates(xr, cwv, cb_ref[...], wav, ba_ref[...], wiv, bi_ref[...], lamv)
        h, dy = h_ref[0], dy_ref[0]
        ge, dge = _gelu_and_grad(gr_ref[0])
        dgr_ref[0] = (dy * h * dge).astype(BF16)
        a_s[...] = _shift_up(a, 1)
        u_s[...] = dy * ge
        _segment_scan(a_s, u_s, h_s, p_s, reverse=True)
        g = h_s[...]
        da = g * _shift_down(h, 1)
        ix = i * xc
        dlog_a = da * a + (g * ix) * (-(a * a) / mult)
        di = g * mult * xc
        dpa = (dlog_a * (-LRU_C * sp)) * r * (1.0 - r)
        dpi = di * i * (1.0 - i)
        dxc = g * mult * i + _dot(dpa, wav, _NT) + _dot(dpi, wiv, _NT)
        dxr = jnp.zeros_like(dxc)
        dcw_rows = []
        for k in range(kw):
            dxr = dxr + _shift_up(dxc, kw - 1 - k) * cwv[k:k + 1, :]
            dcw_rows.append(jnp.sum(dxc * _shift_down(xr, kw - 1 - k), axis=0, keepdims=True))
        dxr_ref[0] = dxr.astype(BF16)

        @pl.when(b == 0)
        def _():
            for ref in (dcw_ref, dcb_ref, dwa_ref, dba_ref, dwi_ref, dbi_ref, dlam_ref):
                ref[...] = jnp.zeros_like(ref)

        for k in range(kw):
            dcw_ref[k:k + 1, :] += dcw_rows[k]
        dcb_ref[...] += jnp.sum(dxc, axis=0, keepdims=True)
        dwa_ref[0] += _dot(xc, dpa, _TN)
        dwi_ref[0] += _dot(xc, dpi, _TN)
        dba_ref[...] += jnp.sum(dpa, axis=0, keepdims=True)
        dbi_ref[...] += jnp.sum(dpi, axis=0, keepdims=True)
        dsp = jnp.sum(dlog_a * (-LRU_C * r), axis=0, keepdims=True)
        dlam_ref[...] += dsp * (-_sigmoid(-lamv))

    vec = jax.ShapeDtypeStruct((1, dr), F32)
    wsh = jax.ShapeDtypeStruct((nb, rb, rb), F32)
    return pl.pallas_call(
        kern, name="rnn_bwd", grid=(nb, bl),
        in_specs=[act(0), act(nb), act(0), act(0), cws, par, wsp, par, wsp, par, par],
        out_specs=[act(0), act(0), cws, par, wsp, par, wsp, par, par],
        out_shape=[jax.ShapeDtypeStruct((bl, s, dr), BF16), jax.ShapeDtypeStruct((bl, s, dr), BF16),
                   jax.ShapeDtypeStruct((kw, dr), F32), vec, wsh, vec, wsh, vec, vec],
        scratch_shapes=[pltpu.VMEM((s, rb), F32)] * 4,
        compiler_params=_cp(2),
    )(zr3, zr3, h3, dy3, cw, cb, wa, ba, wi, bi, lam)


def _tri(n, upper):
    r = lax.broadcasted_iota(jnp.int32, (n, n), 0)
    c = lax.broadcasted_iota(jnp.int32, (n, n), 1)
    return jnp.where((c >= r) if upper else (c <= r), 1.0, 0.0).astype(F32)


def _fgate_fwd(zf3, bf):
    bl, s, w = zf3.shape
    ch = _tile(s, 256, SUBLANES)

    def kern(z_ref, b_ref, f_ref):
        tri = _tri(ch, upper=False)
        carry = jnp.zeros((1, w), F32)
        for j in range(s // ch):
            rows = pl.ds(j * ch, ch)
            lf = -_softplus(-(z_ref[0, rows, :] + b_ref[...]))
            out = jnp.dot(tri, lf, precision=lax.Precision.HIGHEST, preferred_element_type=F32) + carry
            f_ref[0, rows, :] = out
            carry = out[ch - 1:ch, :]

    return pl.pallas_call(
        kern, name="fgate_fwd", grid=(bl,),
        in_specs=[pl.BlockSpec((1, s, w), lambda b: (b, 0, 0)), pl.BlockSpec((1, w), lambda b: (0, 0))],
        out_specs=pl.BlockSpec((1, s, w), lambda b: (b, 0, 0)),
        out_shape=jax.ShapeDtypeStruct((bl, s, w), F32),
        compiler_params=_cp(1),
    )(zf3, bf)


def _fgate_bwd(df3, zf3, bf):
    bl, s, w = zf3.shape
    ch = _tile(s, 256, SUBLANES)

    def kern(df_ref, z_ref, b_ref, dz_ref, db_ref):
        b = pl.program_id(0)
        tri = _tri(ch, upper=True)
        carry = jnp.zeros((1, w), F32)
        dbsum = jnp.zeros((1, w), F32)
        for j in range(s // ch - 1, -1, -1):
            rows = pl.ds(j * ch, ch)
            dlf = jnp.dot(tri, df_ref[0, rows, :], precision=lax.Precision.HIGHEST, preferred_element_type=F32) + carry
            carry = dlf[0:1, :]
            dz = dlf * _sigmoid(-(z_ref[0, rows, :] + b_ref[...]))
            dz_ref[0, rows, :] = dz.astype(BF16)
            dbsum = dbsum + jnp.sum(dz, axis=0, keepdims=True)

        @pl.when(b == 0)
        def _():
            db_ref[...] = jnp.zeros_like(db_ref)

        db_ref[...] += dbsum

    return pl.pallas_call(
        kern, name="fgate_bwd", grid=(bl,),
        in_specs=[pl.BlockSpec((1, s, w), lambda b: (b, 0, 0)), pl.BlockSpec((1, s, w), lambda b: (b, 0, 0)),
                  pl.BlockSpec((1, w), lambda b: (0, 0))],
        out_specs=[pl.BlockSpec((1, s, w), lambda b: (b, 0, 0)), pl.BlockSpec((1, w), lambda b: (0, 0))],
        out_shape=[jax.ShapeDtypeStruct((bl, s, w), BF16), jax.ShapeDtypeStruct((1, w), F32)],
        compiler_params=_cp(1),
    )(df3, zf3, bf)


def _causal(sc, row0, col0, transposed):
    r = lax.broadcasted_iota(jnp.int32, sc.shape, 0) + row0
    c = lax.broadcasted_iota(jnp.int32, sc.shape, 1) + col0
    return jnp.where((c >= r) if transposed else (r >= c), sc, NEG_BIG)


def _attn_fwd(q3, kv3, fcol, frow, nh):
    bl, s, da = q3.shape
    dh = da // nh
    tq = _tile(s, 512, LANES)
    nq = s // tq

    def kern(q_ref, k_ref, v_ref, fq_ref, fk_ref, o_ref, lse_ref, m_s, l_s, acc):
        iq, ik = pl.program_id(2), pl.program_id(3)

        @pl.when(ik == 0)
        def _():
            m_s[...] = jnp.full_like(m_s, NEG_BIG)
            l_s[...] = jnp.zeros_like(l_s)
            acc[...] = jnp.zeros_like(acc)

        @pl.when(ik <= iq)
        def _():
            sc = _dot(q_ref[0], k_ref[0], _NT) + fq_ref[0][:, 0:1] - fk_ref[0]
            sc = _causal(sc, iq * tq, ik * tq, False)
            m_old = m_s[...]
            m_new = jnp.maximum(m_old, jnp.max(sc, axis=-1, keepdims=True))
            alpha = jnp.exp(m_old - m_new)
            p = jnp.exp(sc - m_new)
            l_s[...] = alpha * l_s[...] + jnp.sum(p, axis=-1, keepdims=True)
            acc[...] = alpha * acc[...] + _dot(p, v_ref[0], _NN)
            m_s[...] = m_new

        @pl.when(ik == nq - 1)
        def _():
            l = l_s[...]
            o_ref[0] = acc[...] / l
            lse_ref[0] = jnp.broadcast_to(m_s[...] + jnp.log(l), (tq, LANES))

    kmap = lambda off: (lambda b, h, iq, ik: (b, jnp.minimum(ik, iq), off + h))
    return pl.pallas_call(
        kern, name="attn_fwd", grid=(bl, nh, nq, nq),
        in_specs=[pl.BlockSpec((1, tq, dh), lambda b, h, iq, ik: (b, iq, h)),
                  pl.BlockSpec((1, tq, dh), kmap(0)), pl.BlockSpec((1, tq, dh), kmap(nh)),
                  pl.BlockSpec((1, tq, LANES), lambda b, h, iq, ik: (b * nh + h, iq, 0)),
                  pl.BlockSpec((1, 1, tq), lambda b, h, iq, ik: (b * nh + h, 0, jnp.minimum(ik, iq)))],
        out_specs=[pl.BlockSpec((1, tq, dh), lambda b, h, iq, ik: (b, iq, h)),
                   pl.BlockSpec((1, tq, LANES), lambda b, h, iq, ik: (b * nh + h, iq, 0))],
        out_shape=[jax.ShapeDtypeStruct((bl, s, da), F32), jax.ShapeDtypeStruct((bl * nh, s, LANES), F32)],
        scratch_shapes=[pltpu.VMEM((tq, 1), F32), pltpu.VMEM((tq, 1), F32), pltpu.VMEM((tq, dh), F32)],
        compiler_params=_cp(4),
    )(q3, kv3, kv3, fcol, frow)


def _attn_delta(q3, kv3, do3, lse, fcol, frow, nh):
    bl, s, da = q3.shape
    dh = da // nh
    tq = _tile(s, 512, LANES)
    nq = s // tq

    def kern(q_ref, k_ref, v_ref, do_ref, lse_ref, fq_ref, fk_ref, delta_ref, d_s):
        iq, ik = pl.program_id(2), pl.program_id(3)

        @pl.when(ik == 0)
        def _():
            d_s[...] = jnp.zeros_like(d_s)

        @pl.when(ik <= iq)
        def _():
            sc = _dot(q_ref[0], k_ref[0], _NT) + fq_ref[0][:, 0:1] - fk_ref[0]
            sc = _causal(sc, iq * tq, ik * tq, False)
            p = jnp.exp(sc - lse_ref[0][:, 0:1])
            d_s[...] += jnp.sum(p * _dot(do_ref[0], v_ref[0], _NT), axis=-1, keepdims=True)

        @pl.when(ik == nq - 1)
        def _():
            delta_ref[0] = jnp.broadcast_to(d_s[...], (tq, LANES))

    qmap = lambda b, h, iq, ik: (b, iq, h)
    bmap = lambda b, h, iq, ik: (b * nh + h, iq, 0)
    kmap = lambda off: (lambda b, h, iq, ik: (b, jnp.minimum(ik, iq), off + h))
    return pl.pallas_call(
        kern, name="attn_delta", grid=(bl, nh, nq, nq),
        in_specs=[pl.BlockSpec((1, tq, dh), qmap), pl.BlockSpec((1, tq, dh), kmap(0)), pl.BlockSpec((1, tq, dh), kmap(nh)),
                  pl.BlockSpec((1, tq, dh), qmap), pl.BlockSpec((1, tq, LANES), bmap), pl.BlockSpec((1, tq, LANES), bmap),
                  pl.BlockSpec((1, 1, tq), lambda b, h, iq, ik: (b * nh + h, 0, jnp.minimum(ik, iq)))],
        out_specs=pl.BlockSpec((1, tq, LANES), bmap),
        out_shape=jax.ShapeDtypeStruct((bl * nh, s, LANES), F32),
        scratch_shapes=[pltpu.VMEM((tq, 1), F32)],
        compiler_params=_cp(4),
    )(q3, kv3, kv3, do3, lse, fcol, frow)


def _attn_bwd_dq(q3, kv3, do3, lse, delta, fcol, frow, nh, scale):
    bl, s, da = q3.shape
    dh = da // nh
    tq = _tile(s, 512, LANES)
    nq = s // tq

    def kern(q_ref, k_ref, v_ref, do_ref, lse_ref, dl_ref, fq_ref, fk_ref, dq_ref, acc):
        iq, ik = pl.program_id(2), pl.program_id(3)

        @pl.when(ik == 0)
        def _():
            acc[...] = jnp.zeros_like(acc)

        @pl.when(ik <= iq)
        def _():
            sc = _dot(q_ref[0], k_ref[0], _NT) + fq_ref[0][:, 0:1] - fk_ref[0]
            sc = _causal(sc, iq * tq, ik * tq, False)
            p = jnp.exp(sc - lse_ref[0][:, 0:1])
            dp = _dot(do_ref[0], v_ref[0], _NT)
            ds = p * (dp - dl_ref[0][:, 0:1])
            acc[...] += _dot(ds, k_ref[0], _NN)

        @pl.when(ik == nq - 1)
        def _():
            dq_ref[0] = (acc[...] * scale).astype(BF16)

    qmap = lambda b, h, iq, ik: (b, iq, h)
    bmap = lambda b, h, iq, ik: (b * nh + h, iq, 0)
    kmap = lambda off: (lambda b, h, iq, ik: (b, jnp.minimum(ik, iq), off + h))
    return pl.pallas_call(
        kern, name="attn_bwd_dq", grid=(bl, nh, nq, nq),
        in_specs=[pl.BlockSpec((1, tq, dh), qmap), pl.BlockSpec((1, tq, dh), kmap(0)), pl.BlockSpec((1, tq, dh), kmap(nh)),
                  pl.BlockSpec((1, tq, dh), qmap), pl.BlockSpec((1, tq, LANES), bmap), pl.BlockSpec((1, tq, LANES), bmap),
                  pl.BlockSpec((1, tq, LANES), bmap),
                  pl.BlockSpec((1, 1, tq), lambda b, h, iq, ik: (b * nh + h, 0, jnp.minimum(ik, iq)))],
        out_specs=pl.BlockSpec((1, tq, dh), qmap),
        out_shape=jax.ShapeDtypeStruct((bl, s, da), BF16),
        scratch_shapes=[pltpu.VMEM((tq, dh), F32)],
        compiler_params=_cp(4),
    )(q3, kv3, kv3, do3, lse, delta, fcol, frow)


def _attn_bwd_dkv(q3, kv3, do3, lse_row, delta_row, fcol, frow, nh):
    bl, s, da = q3.shape
    dh = da // nh
    tk = _tile(s, 512, LANES)
    nk = s // tk

    def kern(q_ref, k_ref, v_ref, do_ref, lse_ref, dl_ref, fk_ref, fq_ref, dk_ref, dv_ref, df_ref, dk_acc, dv_acc, df_acc):
        ik, iq = pl.program_id(2), pl.program_id(3)

        @pl.when(iq == 0)
        def _():
            dk_acc[...] = jnp.zeros_like(dk_acc)
            dv_acc[...] = jnp.zeros_like(dv_acc)
            df_acc[...] = jnp.zeros_like(df_acc)

        @pl.when(iq >= ik)
        def _():
            st = _dot(k_ref[0], q_ref[0], _NT) - fk_ref[0][:, 0:1] + fq_ref[0]
            st = _causal(st, ik * tk, iq * tk, True)
            pt = jnp.exp(st - lse_ref[0])
            dv_acc[...] += _dot(pt, do_ref[0], _NN)
            dpt = _dot(v_ref[0], do_ref[0], _NT)
            dst = pt * (dpt - dl_ref[0])
            dk_acc[...] += _dot(dst, q_ref[0], _NN)
            df_acc[...] += jnp.sum(dst, axis=-1, keepdims=True)

        @pl.when(iq == nk - 1)
        def _():
            dk_ref[0] = dk_acc[...].astype(BF16)
            dv_ref[0] = dv_acc[...].astype(BF16)
            df_ref[0] = jnp.broadcast_to(-df_acc[...], (tk, LANES))

    qmap = lambda b, h, ik, iq: (b, jnp.maximum(iq, ik), h)
    rmap = lambda b, h, ik, iq: (b * nh + h, 0, jnp.maximum(iq, ik))
    kmap = lambda off: (lambda b, h, ik, iq: (b, ik, off + h))
    bmap = lambda b, h, ik, iq: (b * nh + h, ik, 0)
    return pl.pallas_call(
        kern, name="attn_bwd_dkv", grid=(bl, nh, nk, nk),
        in_specs=[pl.BlockSpec((1, tk, dh), qmap), pl.BlockSpec((1, tk, dh), kmap(0)), pl.BlockSpec((1, tk, dh), kmap(nh)),
                  pl.BlockSpec((1, tk, dh), qmap), pl.BlockSpec((1, 1, tk), rmap), pl.BlockSpec((1, 1, tk), rmap),
                  pl.BlockSpec((1, tk, LANES), bmap), pl.BlockSpec((1, 1, tk), rmap)],
        out_specs=[pl.BlockSpec((1, tk, dh), kmap(0)), pl.BlockSpec((1, tk, dh), kmap(0)), pl.BlockSpec((1, tk, LANES), bmap)],
        out_shape=[jax.ShapeDtypeStruct((bl, s, da), BF16), jax.ShapeDtypeStruct((bl, s, da), BF16),
                   jax.ShapeDtypeStruct((bl * nh, s, LANES), F32)],
        scratch_shapes=[pltpu.VMEM((tk, dh), F32), pltpu.VMEM((tk, dh), F32), pltpu.VMEM((tk, 1), F32)],
        compiler_params=_cp(4),
    )(q3, kv3, kv3, do3, lse_row, delta_row, fcol, frow)


def _merge_fwd(mg3, pr3, pa3):
    bl, s, d = pr3.shape
    ts = _tile(s, 256, SUBLANES)
    half = lambda j: pl.BlockSpec((1, ts, d), lambda b, t, j=j: (b, t, j))

    def kern(mr_ref, ma_ref, pr_ref, pa_ref, o_ref):
        o_ref[0] = (_sigmoid(mr_ref[0]) * pr_ref[0] + _sigmoid(ma_ref[0]) * pa_ref[0]).astype(BF16)

    return pl.pallas_call(
        kern, name="merge_fwd", grid=(bl, s // ts),
        in_specs=[half(0), half(1)] + _act_specs(ts, d, 2), out_specs=_act_specs(ts, d, 1)[0],
        out_shape=jax.ShapeDtypeStruct((bl, s, d), BF16), compiler_params=_cp(2),
    )(mg3, mg3, pr3, pa3)


def _merge_bwd(dm3, mg3, pr3, pa3):
    bl, s, d = pr3.shape
    ts = _tile(s, 256, SUBLANES)
    half = lambda j: pl.BlockSpec((1, ts, d), lambda b, t, j=j: (b, t, j))

    def kern(dm_ref, mr_ref, ma_ref, pr_ref, pa_ref, dpr_ref, dpa_ref, dmr_ref, dma_ref):
        dm = dm_ref[0]
        gr, ga = _sigmoid(mr_ref[0]), _sigmoid(ma_ref[0])
        dpr_ref[0] = (gr * dm).astype(BF16)
        dpa_ref[0] = (ga * dm).astype(BF16)
        dmr_ref[0] = (dm * pr_ref[0] * gr * (1.0 - gr)).astype(BF16)
        dma_ref[0] = (dm * pa_ref[0] * ga * (1.0 - ga)).astype(BF16)

    return pl.pallas_call(
        kern, name="merge_bwd", grid=(bl, s // ts),
        in_specs=_act_specs(ts, d, 1) + [half(0), half(1)] + _act_specs(ts, d, 2), out_specs=_act_specs(ts, d, 4),
        out_shape=[jax.ShapeDtypeStruct((bl, s, d), BF16)] * 4, compiler_params=_cp(2),
    )(dm3, mg3, mg3, pr3, pa3)


def _ffn_conv(gf, cw, cb):
    kw = cw.shape[0]
    y = cb
    for k in range(kw):
        y = y + _shift_down(gf, kw - 1 - k) * cw[k:k + 1, :]
    return y


def _ffn_act_fwd(up3, cw, cb):
    bl, s, two = up3.shape
    dff = two // 2
    kw = cw.shape[0]
    tc = _tile(dff, 256, LANES)
    nc = dff // tc

    def kern(gf_ref, uf_ref, cw_ref, cb_ref, o_ref):
        o_ref[0] = (_gelu(_ffn_conv(gf_ref[0], cw_ref[...], cb_ref[...])) * uf_ref[0]).astype(BF16)

    act = lambda off: pl.BlockSpec((1, s, tc), lambda b, j, off=off: (b, 0, off + j))
    return pl.pallas_call(
        kern, name="ffn_act_fwd", grid=(bl, nc),
        in_specs=[act(0), act(nc), pl.BlockSpec((kw, tc), lambda b, j: (0, j)), pl.BlockSpec((1, tc), lambda b, j: (0, j))],
        out_specs=act(0), out_shape=jax.ShapeDtypeStruct((bl, s, dff), BF16), compiler_params=_cp(2),
    )(up3, up3, cw, cb)


def _ffn_act_bwd(up3, dact3, cw, cb):
    bl, s, two = up3.shape
    dff = two // 2
    kw = cw.shape[0]
    tc = _tile(dff, 256, LANES)
    nc = dff // tc

    def kern(gf_ref, uf_ref, da_ref, cw_ref, cb_ref, dgf_ref, duf_ref, dcw_ref, dcb_ref):
        b = pl.program_id(1)
        gf, cwv, da = gf_ref[0], cw_ref[...], da_ref[0]
        ge, dge = _gelu_and_grad(_ffn_conv(gf, cwv, cb_ref[...]))
        duf_ref[0] = (da * ge).astype(BF16)
        dgc = da * uf_ref[0] * dge
        dgf = jnp.zeros_like(dgc)
        rows = []
        for k in range(kw):
            dgf = dgf + _shift_up(dgc, kw - 1 - k) * cwv[k:k + 1, :]
            rows.append(jnp.sum(dgc * _shift_down(gf, kw - 1 - k), axis=0, keepdims=True))
        dgf_ref[0] = dgf.astype(BF16)

        @pl.when(b == 0)
        def _():
            dcw_ref[...] = jnp.zeros_like(dcw_ref)
            dcb_ref[...] = jnp.zeros_like(dcb_ref)

        for k in range(kw):
            dcw_ref[k:k + 1, :] += rows[k]
        dcb_ref[...] += jnp.sum(dgc, axis=0, keepdims=True)

    act = lambda off: pl.BlockSpec((1, s, tc), lambda j, b, off=off: (b, 0, off + j))
    cws = pl.BlockSpec((kw, tc), lambda j, b: (0, j))
    cbs = pl.BlockSpec((1, tc), lambda j, b: (0, j))
    return pl.pallas_call(
        kern, name="ffn_act_bwd", grid=(nc, bl),
        in_specs=[act(0), act(nc), act(0), cws, cbs], out_specs=[act(0), act(0), cws, cbs],
        out_shape=[jax.ShapeDtypeStruct((bl, s, dff), BF16), jax.ShapeDtypeStruct((bl, s, dff), BF16),
                   jax.ShapeDtypeStruct((kw, dff), F32), jax.ShapeDtypeStruct((1, dff), F32)],
        compiler_params=_cp(2),
    )(up3, up3, dact3, cw, cb)


_HBM = pl.BlockSpec(memory_space=pltpu.HBM)


def _place():
    x, y, c = lax.axis_index("x"), lax.axis_index("y"), lax.axis_index("c")
    chips = dict(me=2 * x + y, nx=2 * (1 - x) + y, ny=2 * x + (1 - y), diag=2 * (1 - x) + (1 - y))
    peers = dict(nx=(1 - x, y, c), ny=(x, 1 - y, c), sib=(x, y, 1 - c))
    return c, chips, peers


def _remote(src, dst, sems, k, to):
    return pltpu.make_async_remote_copy(src_ref=src, dst_ref=dst, send_sem=sems[0].at[k], recv_sem=sems[1].at[k],
                                        device_id=to, device_id_type=MESH)


RS_STEPS = 2


def _piece(q, idx, n=1):
    start = idx * q
    if not isinstance(start, int):
        start = pl.multiple_of(start, SUBLANES)
    return pl.ds(start, n * q)


def _all_gather_chips(xs, name):
    nt = len(xs)

    def body(*refs):
        x_refs, o_refs = refs[:nt], refs[nt:2 * nt]
        send_sems, recv_sems, local_sems = refs[2 * nt:]
        c, chip, peer = _place()
        sems = (send_sems, recv_sems)
        me, nx, ny, dg = chip["me"], chip["nx"], chip["ny"], chip["diag"]
        sends, copies = [], []

        def arrive(k, dst):
            _remote(dst, dst, sems, k, peer["sib"]).wait_recv()

        def pass_on(k, blk, to):
            cp = _remote(blk, blk, sems, k, peer[to])
            cp.start()
            sends.append(cp)

        for t in range(nt):
            q = xs[t].shape[0] // 4
            cp = pltpu.make_async_copy(x_refs[t], o_refs[t].at[me], local_sems.at[t])
            cp.start()
            copies.append(cp)
            half = _piece(q, 2 * c, 2)
            for k, to in ((0, "nx"), (1, "ny")):
                cp = _remote(x_refs[t].at[half], o_refs[t].at[me, half], sems, 8 * t + k, peer[to])
                cp.start()
                sends.append(cp)
        for t in range(nt):
            q, o, k0 = xs[t].shape[0] // 4, o_refs[t], 8 * t
            half, sub0, sub1 = _piece(q, 2 * c, 2), _piece(q, 2 * c), _piece(q, 2 * c + 1)
            arrive(k0 + 0, o.at[nx, half])
            pass_on(k0 + 2, o.at[nx, sub0], "ny")
            pass_on(k0 + 4, o.at[nx, half], "sib")
            arrive(k0 + 1, o.at[ny, half])
            pass_on(k0 + 3, o.at[ny, sub1], "nx")
            pass_on(k0 + 5, o.at[ny, half], "sib")
            arrive(k0 + 2, o.at[dg, sub0])
            pass_on(k0 + 6, o.at[dg, sub0], "sib")
            arrive(k0 + 3, o.at[dg, sub1])
            pass_on(k0 + 7, o.at[dg, sub1], "sib")
        for t in range(nt):
            q, o, k0 = xs[t].shape[0] // 4, o_refs[t], 8 * t
            arrive(k0 + 4, o.at[nx, _piece(q, 2 * (1 - c), 2)])
            arrive(k0 + 5, o.at[ny, _piece(q, 2 * (1 - c), 2)])
            arrive(k0 + 6, o.at[dg, _piece(q, 2 * (1 - c))])
            arrive(k0 + 7, o.at[dg, _piece(q, 2 * (1 - c) + 1)])
        for cp in sends:
            cp.wait_send()
        for cp in copies:
            cp.wait()

    return pl.pallas_call(
        body, name=name, in_specs=[_HBM] * nt, out_specs=[_HBM] * nt,
        out_shape=[jax.ShapeDtypeStruct((N_CHIPS,) + x.shape, x.dtype) for x in xs],
        scratch_shapes=[pltpu.SemaphoreType.DMA((8 * nt,)), pltpu.SemaphoreType.DMA((8 * nt,)),
                        pltpu.SemaphoreType.DMA((nt,))],
    )(*xs)


def _exchange(name, xs, out_shapes, plan):
    nt = len(xs)

    def body(*refs):
        x_refs, o_refs = refs[:nt], refs[nt:2 * nt]
        send_sems, recv_sems = refs[2 * nt:]
        c, chip, peer = _place()
        cps = []
        for t in range(nt):
            for src, dst, to in plan(c, chip, x_refs[t], o_refs[t], xs[t].shape):
                cps.append(_remote(src, dst, (send_sems, recv_sems), len(cps), peer[to]))
        for cp in cps:
            cp.start()
        for cp in cps:
            cp.wait()

    n_copies = nt * len(plan(0, dict(me=0, nx=2, ny=1, diag=3), None, None, xs[0].shape, count_only=True))
    return pl.pallas_call(
        body, name=name, in_specs=[_HBM] * nt, out_specs=[_HBM] * nt,
        out_shape=[jax.ShapeDtypeStruct(s, F32) for s in out_shapes],
        scratch_shapes=[pltpu.SemaphoreType.DMA((n_copies,)), pltpu.SemaphoreType.DMA((n_copies,))],
    )(*xs)


def _plan_sibling(c, chip, g, out, shape, count_only=False):
    if count_only:
        return [None] * N_CHIPS
    q = shape[1] // 4
    return [(g.at[j, _piece(q, 2 * (1 - c), 2)], out.at[j], "sib") for j in range(N_CHIPS)]


def _plan_first(c, chip, p, out, shape, count_only=False):
    if count_only:
        return [None] * 4
    q = shape[1] // 2
    return [(p.at[chip["nx"], _piece(q, 0)], out.at[0], "nx"), (p.at[chip["diag"], _piece(q, 0)], out.at[1], "nx"),
            (p.at[chip["ny"], _piece(q, 1)], out.at[2], "ny"), (p.at[chip["diag"], _piece(q, 1)], out.at[3], "ny")]


def _plan_second(c, chip, p, out, shape, count_only=False):
    if count_only:
        return [None] * 2
    return [(p.at[1], out.at[0], "ny"), (p.at[3], out.at[1], "nx")]


def _rs_last(ps):
    nt = len(ps)

    def body(*refs):
        p_refs, o_refs = refs[:nt], refs[nt:2 * nt]
        send_sems, recv_sems, local_sems = refs[2 * nt:]
        c, _, peer = _place()
        sems = (send_sems, recv_sems)
        started = []
        for t in range(nt):
            q = ps[t].shape[0] // 2
            mine = pltpu.make_async_copy(p_refs[t], o_refs[t].at[_piece(q, 2 * c, 2)], local_sems.at[t])
            mine.start()
            cp = _remote(p_refs[t], o_refs[t].at[_piece(q, 2 * c, 2)], sems, t, peer["sib"])
            cp.start()
            started.append((mine, cp))
        for t in range(nt):
            q = ps[t].shape[0] // 2
            mine, cp = started[t]
            cp.wait_send()
            _remote(p_refs[t], o_refs[t].at[_piece(q, 2 * (1 - c), 2)], sems, t, peer["sib"]).wait_recv()
            mine.wait()

    return pl.pallas_call(
        body, name="rs_last", in_specs=[_HBM] * nt, out_specs=[_HBM] * nt,
        out_shape=[jax.ShapeDtypeStruct((2 * p.shape[0], p.shape[1]), F32) for p in ps],
        scratch_shapes=[pltpu.SemaphoreType.DMA((nt,)), pltpu.SemaphoreType.DMA((nt,)), pltpu.SemaphoreType.DMA((nt,))],
    )(*ps)


def _add_stage(name, grid, a_list, b_list, a_map, b_map, tbs, out_shapes, out_map, prefetch=None):
    nt = len(a_list)
    lead = lambda shape: (None,) * (len(shape) - 2)

    def kern(*refs):
        refs = refs[(1 if prefetch is not None else 0):]
        for t in range(nt):
            refs[2 * nt + t][...] = refs[t][...] + refs[nt + t][...]

    in_specs = [pl.BlockSpec(lead(a.shape) + (tb, a.shape[-1]), a_map) for a, tb in zip(a_list, tbs)]
    in_specs += [pl.BlockSpec(lead(b.shape) + (tb, b.shape[-1]), b_map) for b, tb in zip(b_list, tbs)]
    out_specs = [pl.BlockSpec(lead(s) + (tb, s[-1]), out_map) for s, tb in zip(out_shapes, tbs)]
    out_shape = [jax.ShapeDtypeStruct(s, F32) for s in out_shapes]
    if prefetch is None:
        return pl.pallas_call(kern, name=name, grid=grid, in_specs=in_specs, out_specs=out_specs, out_shape=out_shape,
                              compiler_params=_cp(len(grid)))(*a_list, *b_list)
    return pl.pallas_call(
        kern, name=name,
        grid_spec=pltpu.PrefetchScalarGridSpec(num_scalar_prefetch=1, grid=grid, in_specs=in_specs, out_specs=out_specs),
        out_shape=out_shape, compiler_params=_cp(len(grid)))(prefetch, *a_list, *b_list)


def _reduce_scatter_chips(gs):
    x, y, c = lax.axis_index("x"), lax.axis_index("y"), lax.axis_index("c")
    me, nx, ny = 2 * x + y, 2 * (1 - x) + y, 2 * x + (1 - y)
    st = RS_STEPS
    qs = [g.shape[1] // 4 for g in gs]
    tbs = [q // st for q in qs]
    for g, tb in zip(gs, tbs):
        assert g.shape[1] == 4 * st * tb and tb % SUBLANES == 0, g.shape
    cols = [g.shape[2] for g in gs]

    got = _exchange("rs_sibling", gs, [(N_CHIPS, 2 * q, cc) for q, cc in zip(qs, cols)], _plan_sibling)
    p0 = _add_stage("rs_add_sibling", (N_CHIPS, 2, st), gs, got,
                    lambda j, h, s, c_ref: (j, (2 * c_ref[0] + h) * st + s, 0), lambda j, h, s, c_ref: (j, h * st + s, 0),
                    tbs, [(N_CHIPS, 2 * q, cc) for q, cc in zip(qs, cols)], lambda j, h, s, c_ref: (j, h * st + s, 0),
                    prefetch=jnp.reshape(c, (1,)).astype(jnp.int32))
    got = _exchange("rs_first", p0, [(4, q, cc) for q, cc in zip(qs, cols)], _plan_first)
    p1 = _add_stage("rs_add_first", (4, st), p0, got,
                    lambda k, s, i_ref: (i_ref[k], (k // 2) * st + s, 0), lambda k, s, i_ref: (k, s, 0),
                    tbs, [(4, q, cc) for q, cc in zip(qs, cols)], lambda k, s, i_ref: (k, s, 0),
                    prefetch=jnp.stack([me, ny, me, nx]).astype(jnp.int32))
    got = _exchange("rs_second", p1, [(2, q, cc) for q, cc in zip(qs, cols)], _plan_second)
    p2 = _add_stage("rs_add_second", (2, st), p1, got, lambda h, s: (2 * h, s, 0), lambda h, s: (h, s, 0),
                    tbs, [(2 * q, cc) for q, cc in zip(qs, cols)], lambda h, s: (h * st + s, 0))
    return _rs_last(p2)


def _adamw(g, w, m, v, name):
    rows, cc = g.shape
    tr = _tile(rows, max(SUBLANES, (1 << 18) // cc), SUBLANES)
    k1 = 1.0 - ADAM_B1 ** ADAM_STEP
    k2 = 1.0 - ADAM_B2 ** ADAM_STEP

    def kern(g_ref, w_ref, m_ref, v_ref, d_ref, nm_ref, nv_ref):
        gv = g_ref[...]
        nm = ADAM_B1 * m_ref[...] + (1.0 - ADAM_B1) * gv
        nv = ADAM_B2 * v_ref[...] + (1.0 - ADAM_B2) * (gv * gv)
        nm_ref[...] = nm
        nv_ref[...] = nv
        d_ref[...] = -ADAM_LR * ((nm / k1) / (jnp.sqrt(nv / k2) + ADAM_EPS) + ADAM_WD * w_ref[...])

    spec = pl.BlockSpec((tr, cc), lambda t: (t, 0))
    return pl.pallas_call(
        kern, name=name, grid=(rows // tr,), in_specs=[spec] * 4, out_specs=[spec] * 3,
        out_shape=[jax.ShapeDtypeStruct((rows, cc), F32)] * 3, compiler_params=_cp(1),
    )(g, w, m, v)


def _flat_pad(parts, total):
    flat = jnp.concatenate([p.reshape(-1) for p in parts])
    return jnp.pad(flat, (0, total - flat.shape[0]))


def _split_flat(flat, shapes):
    out, pos = [], 0
    for shp in shapes:
        size = math.prod(shp)
        out.append(flat[pos:pos + size].reshape(shp))
        pos += size
    return out


def _cols_of_chunks(chunks, lo, hi):
    width = chunks.shape[2]
    parts = []
    for j in range(chunks.shape[0]):
        a, b = max(lo, j * width), min(hi, (j + 1) * width)
        if a < b:
            parts.append(chunks[j, :, a - j * width:b - j * width])
    return parts[0] if len(parts) == 1 else jnp.concatenate(parts, axis=1)


def _chunks_of_cols(segments, n_chunks):
    total = sum(s.shape[1] for s in segments)
    width = total // n_chunks
    chunks = []
    for j in range(n_chunks):
        lo, hi, pos, parts = j * width, (j + 1) * width, 0, []
        for s in segments:
            a, b = max(lo, pos), min(hi, pos + s.shape[1])
            if a < b:
                parts.append(s[:, a - pos:b - pos])
            pos += s.shape[1]
        chunks.append(parts[0] if len(parts) == 1 else jnp.concatenate(parts, axis=1))
    return jnp.stack(chunks)


_WEIGHTS = ['w_ada', 'b_ada', 'g_norm1', 'w_in', 'w_rnn_conv', 'b_rnn_conv', 'w_lru_a', 'b_lru_a', 'w_lru_i', 'b_lru_i',
            'lru_lambda', 'b_fgate', 'w_proj_rnn', 'w_proj_attn', 'w_out', 'g_norm2', 'w_ffn_up', 'w_ffn_conv',
            'b_ffn_conv', 'w_ffn_down', 'w_ada_final', 'b_ada_final', 'g_final']
_MATMUL = dict(w_ada=True, w_in=True, w_proj_rnn=False, w_proj_attn=False, w_out=False, w_ffn_up=True,
               w_ffn_down=False, w_ada_final=True)
_CONV = ['w_rnn_conv', 'w_ffn_conv']
_REPLICATED = [n for n in _WEIGHTS if n not in _MATMUL and n not in _CONV]


def kernel(x, c, w_ada, b_ada, g_norm1, w_in, w_rnn_conv, b_rnn_conv, w_lru_a, b_lru_a, w_lru_i, b_lru_i, lru_lambda, b_fgate, w_proj_rnn, w_proj_attn, w_out, g_norm2, w_ffn_up, w_ffn_conv, b_ffn_conv, w_ffn_down, w_ada_final, b_ada_final, g_final, loss_target, m_w_ada, m_b_ada, m_g_norm1, m_w_in, m_w_rnn_conv, m_b_rnn_conv, m_w_lru_a, m_b_lru_a, m_w_lru_i, m_b_lru_i, m_lru_lambda, m_b_fgate, m_w_proj_rnn, m_w_proj_attn, m_w_out, m_g_norm2, m_w_ffn_up, m_w_ffn_conv, m_b_ffn_conv, m_w_ffn_down, m_w_ada_final, m_b_ada_final, m_g_final, v_w_ada, v_b_ada, v_g_norm1, v_w_in, v_w_rnn_conv, v_b_rnn_conv, v_w_lru_a, v_b_lru_a, v_w_lru_i, v_b_lru_i, v_lru_lambda, v_b_fgate, v_w_proj_rnn, v_w_proj_attn, v_w_out, v_g_norm2, v_w_ffn_up, v_w_ffn_conv, v_b_ffn_conv, v_w_ffn_down, v_w_ada_final, v_b_ada_final, v_g_final):
    args = locals()
    shape_of = {n: args[n].shape for n in _WEIGHTS}

    def view(a):
        if a.ndim >= 3:
            return a[0]
        return a[None, :] if a.ndim == 1 else a

    w2 = {n: view(args[n]) for n in _WEIGHTS}
    m2 = {n: args['m_' + n].reshape(w2[n].shape) for n in _WEIGHTS}
    v2 = {n: args['v_' + n].reshape(w2[n].shape) for n in _WEIGHTS}

    bl, s, d = x.shape
    t = bl * s
    nh = b_fgate.shape[-1]
    nb, rb = w_lru_a.shape[1], w_lru_a.shape[2]
    dr = nb * rb
    da = w2['w_proj_attn'].shape[0] * N_CHIPS
    dh = da // nh
    dff = w2['w_ffn_conv'].shape[1] * N_CHIPS
    scale = dh ** -0.5
    chip = 2 * lax.axis_index("x") + lax.axis_index("y")

    names = list(_MATMUL)
    gathered = dict(zip(names, _all_gather_chips([w2[n].astype(BF16) for n in names], "ag_weights")))
    n_conv = sum(w2[n].size for n in _CONV)
    rows_conv = -(-n_conv // (FLAT_COLS * 32)) * 32
    conv_local = _flat_pad([w2[n] for n in _CONV], rows_conv * FLAT_COLS).reshape(rows_conv, FLAT_COLS)
    conv_all = _all_gather_chips([conv_local], "ag_conv")[0].reshape(N_CHIPS, -1)
    conv_full, pos = {}, 0
    for n in _CONV:
        r, n4 = w2[n].shape
        blocks = conv_all[:, pos:pos + r * n4].reshape(N_CHIPS, r, n4)
        conv_full[n] = jnp.concatenate([blocks[j] for j in range(N_CHIPS)], axis=1)
        pos += r * n4
    rowmajor = lambda n: gathered[n].reshape(-1, gathered[n].shape[2])
    w_proj_rnn_f, w_proj_attn_f, w_out_f, w_ffn_down_f = (rowmajor(n) for n in ('w_proj_rnn', 'w_proj_attn', 'w_out', 'w_ffn_down'))
    ada_chunk, adaf_chunk, up_chunk = (w2[n].shape[1] for n in ('w_ada', 'w_ada_final', 'w_ffn_up'))

    o_q, o_k, o_fl = 2 * dr, 2 * dr + da, 2 * dr + 3 * da
    o_mg = o_fl + nh
    g_in_w = gathered['w_in']
    w_rnn, w_q = _cols_of_chunks(g_in_w, 0, o_q), _cols_of_chunks(g_in_w, o_q, o_k)
    w_kv, w_mg = _cols_of_chunks(g_in_w, o_k, o_fl), _cols_of_chunks(g_in_w, o_mg, o_mg + 2 * d)
    w_fl = jnp.pad(_cols_of_chunks(g_in_w, o_fl, o_mg), ((0, 0), (0, LANES - nh)))
    w_main = jnp.concatenate([w_rnn, w_q, w_kv, w_mg], axis=1)
    bf_pad = jnp.pad(w2['b_fgate'], ((0, 0), (0, LANES - nh)))

    c_act = _silu_pad(c, 16)
    mod = _mm(c_act, gathered['w_ada'], "nn", name="ada_fwd", bias=w2['b_ada'], b_chunk=ada_chunk)[:bl]
    sh1, sc1, gt1, sh2, sc2, gt2 = [mod[:, i * d:(i + 1) * d].reshape(bl, 1, d) for i in range(6)]
    modf = _mm(c_act, gathered['w_ada_final'], "nn", name="ada_final_fwd", bias=w2['b_ada_final'], b_chunk=adaf_chunk)[:bl]
    shf, scf = modf[:, :d].reshape(bl, 1, d), modf[:, d:].reshape(bl, 1, d)

    h1 = _norm_mod_fwd(x, w2['g_norm1'], sh1, sc1)
    h1f = h1.reshape(t, d)
    zr = _mm(h1f, w_rnn, "nn", name="in_rnn").reshape(bl, s, 2 * dr)
    q3 = _mm(h1f, w_q, "nn", name="in_q", out_dtype=BF16, scale=scale).reshape(bl, s, da)
    kv3 = _mm(h1f, w_kv, "nn", name="in_kv", out_dtype=BF16).reshape(bl, s, 2 * da)
    mg3 = _mm(h1f, w_mg, "nn", name="in_mg").reshape(bl, s, 2 * d)
    zf3 = _mm(h1f, w_fl, "nn", name="in_fl").reshape(bl, s, LANES)

    lru = (conv_full['w_rnn_conv'], w2['b_rnn_conv'], w2['w_lru_a'], w2['b_lru_a'], w2['w_lru_i'], w2['b_lru_i'], w2['lru_lambda'])
    hseq, y_rnn = _rnn_fwd(zr, *lru)

    f3 = _fgate_fwd(zf3, bf_pad)
    f_heads = f3[:, :, :nh].transpose(0, 2, 1).reshape(bl * nh, s)
    fcol = jnp.broadcast_to(f_heads[:, :, None], (bl * nh, s, LANES))
    frow = f_heads.reshape(bl * nh, 1, s)
    o3, lse = _attn_fwd(q3, kv3, fcol, frow, nh)

    pr3 = _mm(y_rnn.reshape(t, dr), w_proj_rnn_f, "nn", name="proj_rnn").reshape(bl, s, d)
    pa3 = _mm(o3.reshape(t, da), w_proj_attn_f, "nn", name="proj_attn").reshape(bl, s, d)
    merged = _merge_fwd(mg3, pr3, pa3)
    mo3 = _mm(merged.reshape(t, d), w_out_f, "nn", name="mix_out").reshape(bl, s, d)
    x1, h2 = _resid_norm_fwd(x, mo3, gt1, w2['g_norm2'], sh2, sc2)
    h2f = h2.reshape(t, d)
    up3 = _mm(h2f, gathered['w_ffn_up'], "nn", name="ffn_up", b_chunk=up_chunk).reshape(bl, s, 2 * dff)
    act3 = _ffn_act_fwd(up3, conv_full['w_ffn_conv'], w2['b_ffn_conv'])
    yf3 = _mm(act3.reshape(t, dff), w_ffn_down_f, "nn", name="ffn_down").reshape(bl, s, d)

    dx2, dshf, dscf, dg_final, loss_part = _final_fwd_bwd(x1, yf3, gt2, w2['g_final'], shf, scf, loss_target)
    loss = lax.psum(loss_part[0, 0], ("x", "y", "c"))

    dyf, dgt2 = _gate_bwd(dx2, yf3, gt2, "ffn_gate_bwd")
    dyf_f = dyf.reshape(t, d)
    g_ffn_down = _mm(act3.reshape(t, dff), dyf_f, "tn", name="dw_ffn_down")
    dact3 = _mm(dyf_f, w_ffn_down_f, "nt", name="d_ffn_act").reshape(bl, s, dff)
    dgf, duf, g_ffn_conv, g_b_ffn_conv = _ffn_act_bwd(up3, dact3, conv_full['w_ffn_conv'], w2['b_ffn_conv'])
    dgf_f, duf_f = dgf.reshape(t, dff), duf.reshape(t, dff)
    g_ffn_up = jnp.concatenate([_mm(h2f, dgf_f, "tn", name="dw_ffn_up_gate", out_chunk=up_chunk),
                                _mm(h2f, duf_f, "tn", name="dw_ffn_up_value", out_chunk=up_chunk)], axis=0)
    dh2 = _mm([dgf_f, duf_f], gathered['w_ffn_up'], "nt", name="d_h2", b_chunk=up_chunk).reshape(bl, s, d)
    dx1, dsh2, dsc2, dg_norm2 = _norm_mod_bwd(dh2, x1, dx2, w2['g_norm2'], sc2, "norm2_bwd")

    dmo, dgt1 = _gate_bwd(dx1, mo3, gt1, "mix_gate_bwd")
    dmo_f = dmo.reshape(t, d)
    g_out = _mm(merged.reshape(t, d), dmo_f, "tn", name="dw_out")
    dm3 = _mm(dmo_f, w_out_f, "nt", name="d_merged").reshape(bl, s, d)
    dpr, dpa, dmr, dma = _merge_bwd(dm3, mg3, pr3, pa3)
    g_proj_rnn = _mm(y_rnn.reshape(t, dr), dpr.reshape(t, d), "tn", name="dw_proj_rnn")
    g_proj_attn = _mm(o3.reshape(t, da), dpa.reshape(t, d), "tn", name="dw_proj_attn")
    dyr3 = _mm(dpr.reshape(t, d), w_proj_rnn_f, "nt", name="d_y_rnn").reshape(bl, s, dr)
    do3 = _mm(dpa.reshape(t, d), w_proj_attn_f, "nt", name="d_y_attn").reshape(bl, s, da)

    delta = _attn_delta(q3, kv3, do3, lse, fcol, frow, nh)
    dq3 = _attn_bwd_dq(q3, kv3, do3, lse, delta, fcol, frow, nh, scale)
    lse_row = lse[:, :, 0].reshape(bl * nh, 1, s)
    delta_row = delta[:, :, 0].reshape(bl * nh, 1, s)
    dk3, dv3, dfk = _attn_bwd_dkv(q3, kv3, do3, lse_row, delta_row, fcol, frow, nh)
    df3 = jnp.pad(dfk[:, :, 0].reshape(bl, nh, s).transpose(0, 2, 1), ((0, 0), (0, 0), (0, LANES - nh)))
    dzf3, g_bf = _fgate_bwd(df3, zf3, bf_pad)

    dxr, dgr, g_rnn_conv, g_b_rnn_conv, g_lru_a, g_b_lru_a, g_lru_i, g_b_lru_i, g_lam = _rnn_bwd(zr, hseq, dyr3, *lru)

    dz = [a.reshape(t, -1) for a in (dxr, dgr, dq3, dk3, dv3, dmr, dma)]
    dzf_f = dzf3.reshape(t, LANES)
    seg_names = ("xr", "gr", "q", "k", "v", "mr", "ma")
    g_seg = [_mm(h1f, a, "tn", name="dw_in_" + n) for n, a in zip(seg_names, dz)]
    g_in_fl = _mm(h1f, dzf_f, "tn", name="dw_in_fl")[:, :nh]
    g_in = _chunks_of_cols(g_seg[:5] + [g_in_fl] + g_seg[5:], N_CHIPS)
    dh1 = _mm(dzf_f, w_fl, "nt", name="d_h1_fl")
    dh1 = _mm(dz, w_main, "nt", name="d_h1", add=dh1, tk=256).reshape(bl, s, d)
    grad_x, dsh1, dsc1, dg_norm1 = _norm_mod_bwd(dh1, x, dx1, w2['g_norm1'], sc1, "norm1_bwd")

    pad_rows = lambda a: jnp.pad(a.reshape(bl, -1), ((0, 16 - bl), (0, 0)))
    dmod = pad_rows(jnp.concatenate([dsh1, dsc1, dgt1, dsh2, dsc2, dgt2], axis=-1))
    dmodf = pad_rows(jnp.concatenate([dshf, dscf], axis=-1))
    g_ada = _mm(c_act, dmod, "tn", name="dw_ada", out_chunk=ada_chunk)
    g_ada_final = _mm(c_act, dmodf, "tn", name="dw_ada_final", out_chunk=adaf_chunk)
    g_b_ada = _rowsum(dmod, "db_ada")
    g_b_ada_final = _rowsum(dmodf, "db_ada_final")

    rowchunks = lambda g: g.reshape(N_CHIPS, g.shape[0] // N_CHIPS, g.shape[1])
    full = dict(w_ada=g_ada, w_in=g_in, w_proj_rnn=rowchunks(g_proj_rnn), w_proj_attn=rowchunks(g_proj_attn),
                w_out=rowchunks(g_out), w_ffn_up=g_ffn_up, w_ffn_down=rowchunks(g_ffn_down), w_ada_final=g_ada_final)
    small = dict(b_ada=g_b_ada, g_norm1=dg_norm1, w_rnn_conv=g_rnn_conv, b_rnn_conv=g_b_rnn_conv, w_lru_a=g_lru_a,
                 b_lru_a=g_b_lru_a, w_lru_i=g_lru_i, b_lru_i=g_b_lru_i, lru_lambda=g_lam, b_fgate=g_bf[:, :nh],
                 g_norm2=dg_norm2, w_ffn_conv=g_ffn_conv, b_ffn_conv=g_b_ffn_conv, b_ada_final=g_b_ada_final,
                 g_final=dg_final)

    small_names = _REPLICATED + _CONV
    n_small = sum(small[n].size for n in small_names)
    rows_q = -(-n_small // (N_CHIPS * FLAT_COLS * 32 * RS_STEPS)) * 32 * RS_STEPS
    small_flat = _flat_pad([small[n] for n in small_names], N_CHIPS * rows_q * FLAT_COLS).reshape(N_CHIPS, rows_q, FLAT_COLS)
    reduced = _reduce_scatter_chips([full[n] for n in names] + [small_flat])
    grad = dict(zip(names, reduced[:-1]))
    small_all = _all_gather_chips([reduced[-1]], "ag_small_grads")[0].reshape(-1)
    grad.update(zip(small_names, _split_flat(small_all, [small[n].shape for n in small_names])))
    for n in _CONV:
        n4 = w2[n].shape[1]
        grad[n] = lax.dynamic_slice_in_dim(grad[n], chip * n4, n4, axis=1)

    delta_w, new_m, new_v = {}, {}, {}
    for n in names:
        delta_w[n], new_m[n], new_v[n] = _adamw(grad[n], w2[n], m2[n], v2[n], "adamw_" + n)
    rows_small = -(-sum(w2[n].size for n in small_names) // (FLAT_COLS * SUBLANES)) * SUBLANES
    flat_small = lambda src: _flat_pad([src[n] for n in small_names], rows_small * FLAT_COLS).reshape(rows_small, FLAT_COLS)
    small_out = _adamw(flat_small(grad), flat_small(w2), flat_small(m2), flat_small(v2), "adamw_small")
    for dst, flat in zip((delta_w, new_m, new_v), small_out):
        dst.update(zip(small_names, _split_flat(flat.reshape(-1), [w2[n].shape for n in small_names])))

    out = [loss, grad_x]
    for src in (grad, delta_w, new_m, new_v):
        out += [src[n].reshape(shape_of[n]) for n in _WEIGHTS]
    return tuple(out)
```

```python
import functools
import math

import jax
import jax.numpy as jnp
from jax import lax
from jax.experimental import pallas as pl
from jax.experimental.pallas import tpu as pltpu

F32 = jnp.float32
BF16 = jnp.bfloat16
MESH = pl.DeviceIdType.MESH

RMS_EPS = 1e-6
LRU_C = 8.0
ADAM_LR = 0.001
ADAM_B1 = 0.9
ADAM_B2 = 0.999
ADAM_EPS = 1e-08
ADAM_WD = 0.01
ADAM_STEP = 10

LANES = 128
SUBLANES = 8
N_CHIPS = 4
FLAT_COLS = 1024
SCAN_SEGMENTS = SUBLANES
VMEM_LIMIT = 48 * 1024 * 1024
NEG_BIG = -1e30


def _cp(n_axes):
    return pltpu.CompilerParams(dimension_semantics=("arbitrary",) * n_axes, vmem_limit_bytes=VMEM_LIMIT)


def _tile(n, target, align):
    if n <= target:
        return n
    t = (target // align) * align
    while t >= align:
        if n % t == 0:
            return t
        t -= align
    return n


def _nice_rows(n, align):
    r = -(-n // align) * align
    while True:
        if r <= 640:
            return r, r
        t = _tile(r, 640, align)
        if 128 <= t <= 640:
            return r, t
        r += align


def _sigmoid(x):
    return jax.nn.sigmoid(x)


def _softplus(x):
    return jnp.maximum(x, 0.0) + jnp.log1p(jnp.exp(-jnp.abs(x)))


def _expm1(x, exp_x):
    small = x * (1.0 + 0.5 * x * (1.0 + (1.0 / 3.0) * x * (1.0 + 0.25 * x)))
    return jnp.where(jnp.abs(x) < 0.05, small, exp_x - 1.0)


_GELU_K = math.sqrt(2.0 / math.pi)
_GELU_C = 0.044715


def _gelu(x):
    t = jnp.tanh(_GELU_K * (x + _GELU_C * x * x * x))
    return 0.5 * x * (1.0 + t)


def _gelu_and_grad(x):
    t = jnp.tanh(_GELU_K * (x + _GELU_C * x * x * x))
    g = 0.5 * x * (1.0 + t)
    dg = 0.5 * (1.0 + t) + 0.5 * x * (1.0 - t * t) * _GELU_K * (1.0 + 3.0 * _GELU_C * x * x)
    return g, dg


def _shift_down(x, k):
    if k == 0:
        return x
    rows = lax.broadcasted_iota(jnp.int32, x.shape, 0)
    return jnp.where(rows >= k, pltpu.roll(x, k, 0), 0.0)


def _shift_up(x, k):
    if k == 0:
        return x
    s = x.shape[0]
    rows = lax.broadcasted_iota(jnp.int32, x.shape, 0)
    return jnp.where(rows < s - k, pltpu.roll(x, s - k, 0), 0.0)


def _dot(a, b, dims):
    return lax.dot_general(a.astype(BF16), b.astype(BF16), (dims, ((), ())), preferred_element_type=F32)


_NN = ((1,), (0,))
_NT = ((1,), (1,))
_TN = ((0,), (0,))


def _mm(a, b, mode, *, name, out_dtype=F32, scale=None, bias=None, add=None, tm=1024, tn=1024, tk=1024,
        b_chunk=None, out_chunk=None):
    pieces = list(a) if isinstance(a, (list, tuple)) else [a]
    ksize = lambda p: p.shape[0] if mode == "tn" else p.shape[1]
    if b_chunk is None:
        brows, bcols = b.shape
    else:
        brows, bcols = b.shape[1], b.shape[0] * b_chunk
    k = sum(ksize(p) for p in pieces)
    if mode == "nt":
        m, n = pieces[0].shape[0], brows
        assert bcols == k, (bcols, k)
    else:
        m, n = (pieces[0].shape[1] if mode == "tn" else pieces[0].shape[0]), bcols
        assert brows == k, (brows, k)
    tm = _tile(m, tm, LANES)
    ncut = n
    if b_chunk is not None and mode != "nt":
        ncut = b_chunk
    if out_chunk is not None:
        ncut = math.gcd(ncut, out_chunk)
    tn = _tile(ncut, tn, LANES)
    kcut = b_chunk if (b_chunk is not None and mode == "nt") else k
    for p in pieces:
        kcut = math.gcd(kcut, ksize(p))
    tk = _tile(kcut, tk, LANES)
    nk = k // tk
    dims = {"nn": _NN, "nt": _NT, "tn": _TN}[mode]
    counts = [ksize(p) // tk for p in pieces]
    starts = [sum(counts[:i]) for i in range(len(pieces))]
    n_pieces = len(pieces)

    def a_spec(s0, cnt):
        kmap = (lambda kk: kk) if n_pieces == 1 else (lambda kk: jnp.clip(kk - s0, 0, cnt - 1))
        if mode == "tn":
            return pl.BlockSpec((tk, tm), lambda i, j, kk: (kmap(kk), i))
        return pl.BlockSpec((tm, tk), lambda i, j, kk: (i, kmap(kk)))

    if b_chunk is None:
        if mode == "nt":
            b_spec = pl.BlockSpec((tn, tk), lambda i, j, kk: (j, kk))
        else:
            b_spec = pl.BlockSpec((tk, tn), lambda i, j, kk: (kk, j))
    elif mode == "nt":
        per_b = b_chunk // tk
        b_spec = pl.BlockSpec((None, tn, tk), lambda i, j, kk: (kk // per_b, j, kk % per_b))
    else:
        per_b = b_chunk // tn
        b_spec = pl.BlockSpec((None, tk, tn), lambda i, j, kk: (j // per_b, kk, j % per_b))
    if out_chunk is None:
        out_spec = pl.BlockSpec((tm, tn), lambda i, j, kk: (i, j))
        out_shape = jax.ShapeDtypeStruct((m, n), out_dtype)
    else:
        per_o = out_chunk // tn
        out_spec = pl.BlockSpec((None, tm, tn), lambda i, j, kk: (j // per_o, i, j % per_o))
        out_shape = jax.ShapeDtypeStruct((n // out_chunk, m, out_chunk), out_dtype)
    in_specs = [a_spec(s0, cnt) for s0, cnt in zip(starts, counts)] + [b_spec]
    args = pieces + [b]
    if bias is not None:
        in_specs.append(pl.BlockSpec((1, tn), lambda i, j, kk: (0, j)))
        args.append(bias)
    if add is not None:
        in_specs.append(pl.BlockSpec((tm, tn), lambda i, j, kk: (i, j)))
        args.append(add)

    def kern(*refs):
        b_ref = refs[n_pieces]
        o_ref = refs[n_pieces + 1 + (bias is not None) + (add is not None)]

        def finish(r):
            if scale is not None:
                r = r * scale
            pos = n_pieces + 1
            if bias is not None:
                r = r + refs[pos][...]
                pos += 1
            if add is not None:
                r = r + refs[pos][...]
            o_ref[...] = r.astype(out_dtype)

        if nk == 1:
            finish(_dot(refs[0][...], b_ref[...], dims))
            return
        acc = refs[-1]
        kk = pl.program_id(2)

        @pl.when(kk == 0)
        def _():
            acc[...] = jnp.zeros_like(acc)

        if n_pieces == 1:
            acc[...] += _dot(refs[0][...], b_ref[...], dims)
        else:
            for idx in range(n_pieces):
                @pl.when((kk >= starts[idx]) & (kk < starts[idx] + counts[idx]))
                def _(idx=idx):
                    acc[...] += _dot(refs[idx][...], b_ref[...], dims)

        @pl.when(kk == nk - 1)
        def _():
            finish(acc[...])

    return pl.pallas_call(
        kern, name=name,
        grid=(m // tm, n // tn, nk),
        in_specs=in_specs, out_specs=out_spec, out_shape=out_shape,
        scratch_shapes=[pltpu.VMEM((tm, tn), F32)] if nk > 1 else [],
        compiler_params=_cp(3),
    )(*args)


def _silu_pad(c, rows):
    bl, d = c.shape

    def kern(c_ref, o_ref):
        o_ref[...] = jnp.zeros_like(o_ref)
        v = c_ref[...]
        o_ref[0:bl, :] = v * _sigmoid(v)

    return pl.pallas_call(kern, name="silu_pad", out_shape=jax.ShapeDtypeStruct((rows, d), F32))(c)


def _rowsum(x, name):
    r, n = x.shape

    def kern(x_ref, o_ref):
        o_ref[...] = jnp.sum(x_ref[...], axis=0, keepdims=True)

    return pl.pallas_call(kern, name=name, out_shape=jax.ShapeDtypeStruct((1, n), F32))(x)


def _norm_parts(x, g):
    r = lax.rsqrt(jnp.mean(x * x, axis=-1, keepdims=True) + RMS_EPS)
    xh = x * r
    return r, xh, xh * g


def _norm_bwd_parts(dh, xh, r, g, sc):
    n = xh * g
    dn = dh * (1.0 + sc)
    dxh = dn * g
    dx = r * (dxh - xh * jnp.mean(dxh * xh, axis=-1, keepdims=True))
    return dx, dh, dh * n, dn * xh


def _act_specs(ts, d, n):
    return [pl.BlockSpec((1, ts, d), lambda b, t: (b, t, 0)) for _ in range(n)]


def _vec_spec(d):
    return pl.BlockSpec((1, 1, d), lambda b, t: (b, 0, 0))


def _par_spec(d):
    return pl.BlockSpec((1, d), lambda b, t: (0, 0))


def _norm_mod_fwd(x3, g, sh, sc):
    bl, s, d = x3.shape
    ts = _tile(s, 512, SUBLANES)

    def kern(x_ref, g_ref, sh_ref, sc_ref, h_ref):
        _, _, n = _norm_parts(x_ref[0], g_ref[...])
        h_ref[0] = (n * (1.0 + sc_ref[0]) + sh_ref[0]).astype(BF16)

    return pl.pallas_call(
        kern, name="norm_mod_fwd", grid=(bl, s // ts),
        in_specs=_act_specs(ts, d, 1) + [_par_spec(d), _vec_spec(d), _vec_spec(d)],
        out_specs=_act_specs(ts, d, 1)[0],
        out_shape=jax.ShapeDtypeStruct((bl, s, d), BF16),
        compiler_params=_cp(2),
    )(x3, g, sh, sc)


def _resid_norm_fwd(x3, y3, gate, g, sh, sc):
    bl, s, d = x3.shape
    ts = _tile(s, 512, SUBLANES)

    def kern(x_ref, y_ref, gate_ref, g_ref, sh_ref, sc_ref, x1_ref, h_ref):
        x1 = x_ref[0] + gate_ref[0] * y_ref[0]
        x1_ref[0] = x1
        _, _, n = _norm_parts(x1, g_ref[...])
        h_ref[0] = (n * (1.0 + sc_ref[0]) + sh_ref[0]).astype(BF16)

    return pl.pallas_call(
        kern, name="resid_norm_fwd", grid=(bl, s // ts),
        in_specs=_act_specs(ts, d, 2) + [_vec_spec(d), _par_spec(d), _vec_spec(d), _vec_spec(d)],
        out_specs=_act_specs(ts, d, 2),
        out_shape=[jax.ShapeDtypeStruct((bl, s, d), F32), jax.ShapeDtypeStruct((bl, s, d), BF16)],
        compiler_params=_cp(2),
    )(x3, y3, gate, g, sh, sc)


def _norm_mod_bwd(dh3, x3, dres3, g, sc, name):
    bl, s, d = x3.shape
    ts = _tile(s, 512, SUBLANES)

    def kern(dh_ref, x_ref, dres_ref, g_ref, sc_ref, dx_ref, dsh_ref, dsc_ref, dg_ref):
        b, t = pl.program_id(0), pl.program_id(1)
        gv = g_ref[...]
        r, xh, _ = _norm_parts(x_ref[0], gv)
        dx, a, bb, cc = _norm_bwd_parts(dh_ref[0], xh, r, gv, sc_ref[0])
        dx_ref[0] = dres_ref[0] + dx

        @pl.when(t == 0)
        def _():
            dsh_ref[...] = jnp.zeros_like(dsh_ref)
            dsc_ref[...] = jnp.zeros_like(dsc_ref)

        @pl.when((t == 0) & (b == 0))
        def _():
            dg_ref[...] = jnp.zeros_like(dg_ref)

        dsh_ref[0] += jnp.sum(a, axis=0, keepdims=True)
        dsc_ref[0] += jnp.sum(bb, axis=0, keepdims=True)
        dg_ref[...] += jnp.sum(cc, axis=0, keepdims=True)

    return pl.pallas_call(
        kern, name=name, grid=(bl, s // ts),
        in_specs=_act_specs(ts, d, 3) + [_par_spec(d), _vec_spec(d)],
        out_specs=[_act_specs(ts, d, 1)[0], _vec_spec(d), _vec_spec(d), _par_spec(d)],
        out_shape=[jax.ShapeDtypeStruct((bl, s, d), F32), jax.ShapeDtypeStruct((bl, 1, d), F32),
                   jax.ShapeDtypeStruct((bl, 1, d), F32), jax.ShapeDtypeStruct((1, d), F32)],
        compiler_params=_cp(2),
    )(dh3, x3, dres3, g, sc)


def _gate_bwd(dx3, y3, gate, name):
    bl, s, d = dx3.shape
    ts = _tile(s, 512, SUBLANES)

    def kern(dx_ref, y_ref, gate_ref, dy_ref, dgate_ref):
        t = pl.program_id(1)
        dx = dx_ref[0]
        dy_ref[0] = (gate_ref[0] * dx).astype(BF16)

        @pl.when(t == 0)
        def _():
            dgate_ref[...] = jnp.zeros_like(dgate_ref)

        dgate_ref[0] += jnp.sum(dx * y_ref[0], axis=0, keepdims=True)

    return pl.pallas_call(
        kern, name=name, grid=(bl, s // ts),
        in_specs=_act_specs(ts, d, 2) + [_vec_spec(d)],
        out_specs=[_act_specs(ts, d, 1)[0], _vec_spec(d)],
        out_shape=[jax.ShapeDtypeStruct((bl, s, d), BF16), jax.ShapeDtypeStruct((bl, 1, d), F32)],
        compiler_params=_cp(2),
    )(dx3, y3, gate)


def _final_fwd_bwd(x1, yf, gate2, g, shf, scf, tgt):
    bl, s, d = x1.shape
    ts = _tile(s, 512, SUBLANES)

    def kern(x1_ref, yf_ref, gate_ref, g_ref, sh_ref, sc_ref, tgt_ref, dx_ref, dsh_ref, dsc_ref, dg_ref, loss_ref):
        b, t = pl.program_id(0), pl.program_id(1)
        gv, sc = g_ref[...], sc_ref[0]
        x2 = x1_ref[0] + gate_ref[0] * yf_ref[0]
        r, xh, n = _norm_parts(x2, gv)
        err = n * (1.0 + sc) + sh_ref[0] - tgt_ref[0]
        dx, a, bb, cc = _norm_bwd_parts(err * (1.0 / d), xh, r, gv, sc)
        dx_ref[0] = dx

        @pl.when(t == 0)
        def _():
            dsh_ref[...] = jnp.zeros_like(dsh_ref)
            dsc_ref[...] = jnp.zeros_like(dsc_ref)

        @pl.when((t == 0) & (b == 0))
        def _():
            dg_ref[...] = jnp.zeros_like(dg_ref)
            loss_ref[...] = jnp.zeros_like(loss_ref)

        dsh_ref[0] += jnp.sum(a, axis=0, keepdims=True)
        dsc_ref[0] += jnp.sum(bb, axis=0, keepdims=True)
        dg_ref[...] += jnp.sum(cc, axis=0, keepdims=True)
        tok = jnp.mean(err * err, axis=-1, keepdims=True)
        loss_ref[...] += 0.5 * jnp.sum(tok, axis=0, keepdims=True)

    return pl.pallas_call(
        kern, name="final_fwd_bwd", grid=(bl, s // ts),
        in_specs=_act_specs(ts, d, 2) + [_vec_spec(d), _par_spec(d), _vec_spec(d), _vec_spec(d)] + _act_specs(ts, d, 1),
        out_specs=[_act_specs(ts, d, 1)[0], _vec_spec(d), _vec_spec(d), _par_spec(d),
                   pl.BlockSpec((1, 1), lambda b, t: (0, 0))],
        out_shape=[jax.ShapeDtypeStruct((bl, s, d), F32), jax.ShapeDtypeStruct((bl, 1, d), F32),
                   jax.ShapeDtypeStruct((bl, 1, d), F32), jax.ShapeDtypeStruct((1, d), F32),
                   jax.ShapeDtypeStruct((1, 1), F32)],
        compiler_params=_cp(2),
    )(x1, yf, gate2, g, shf, scf, tgt)


def _rnn_gates(xr, cw, cb, wa, ba, wi, bi, lam):
    kw = cw.shape[0]
    xc = cb
    for k in range(kw):
        xc = xc + _shift_down(xr, kw - 1 - k) * cw[k:k + 1, :]
    r = _sigmoid(_dot(xc, wa, _NN) + ba)
    i = _sigmoid(_dot(xc, wi, _NN) + bi)
    sp = _softplus(-lam)
    log_a = -LRU_C * r * sp
    a = jnp.exp(log_a)
    mult = jnp.sqrt(-_expm1(2.0 * log_a, a * a))
    return xc, r, i, sp, a, mult


def _segment_scan(a_s, u_s, h_s, p_s, reverse):
    s, c = a_s.shape
    seg = s // SCAN_SEGMENTS

    def step(n, carry):
        t = (seg - 1 - n) if reverse else n
        h, p = carry
        av = a_s[pl.ds(t, SCAN_SEGMENTS, stride=seg), :]
        uv = u_s[pl.ds(t, SCAN_SEGMENTS, stride=seg), :]
        h = av * h + uv
        p = p * av
        h_s[pl.ds(t, SCAN_SEGMENTS, stride=seg), :] = h
        p_s[pl.ds(t, SCAN_SEGMENTS, stride=seg), :] = p
        return h, p

    lax.fori_loop(0, seg, step, (jnp.zeros((SCAN_SEGMENTS, c), F32), jnp.ones((SCAN_SEGMENTS, c), F32)), unroll=8)
    carry = jnp.zeros((1, c), F32)
    order = range(SCAN_SEGMENTS - 1, -1, -1) if reverse else range(SCAN_SEGMENTS)
    for j in order:
        rows = pl.ds(j * seg, seg)
        fixed = h_s[rows, :] + p_s[rows, :] * carry
        h_s[rows, :] = fixed
        carry = fixed[0:1, :] if reverse else fixed[seg - 1:seg, :]


def _rnn_specs(s, rb, nb):
    act = lambda off: pl.BlockSpec((1, s, rb), lambda b, n, off=off: (b, 0, off + n))
    par = pl.BlockSpec((1, rb), lambda b, n: (0, n))
    wsp = pl.BlockSpec((1, rb, rb), lambda b, n: (n, 0, 0))
    return act, par, wsp


def _rnn_fwd(zr3, cw, cb, wa, ba, wi, bi, lam):
    bl, s, two = zr3.shape
    nb, rb, _ = wa.shape
    dr = nb * rb
    kw = cw.shape[0]
    act, par, wsp = _rnn_specs(s, rb, nb)

    def kern(xr_ref, gr_ref, cw_ref, cb_ref, wa_ref, ba_ref, wi_ref, bi_ref, lam_ref, h_ref, y_ref, a_s, u_s, h_s, p_s):
        xc, r, i, sp, a, mult = _rnn_gates(xr_ref[0], cw_ref[...], cb_ref[...], wa_ref[0], ba_ref[...],
                                           wi_ref[0], bi_ref[...], lam_ref[...])
        a_s[...] = a
        u_s[...] = mult * (i * xc)
        _segment_scan(a_s, u_s, h_s, p_s, reverse=False)
        h = h_s[...]
        h_ref[0] = h
        y_ref[0] = (_gelu(gr_ref[0]) * h).astype(BF16)

    return pl.pallas_call(
        kern, name="rnn_fwd", grid=(bl, nb),
        in_specs=[act(0), act(nb), pl.BlockSpec((kw, rb), lambda b, n: (0, n)), par, wsp, par, wsp, par, par],
        out_specs=[act(0), act(0)],
        out_shape=[jax.ShapeDtypeStruct((bl, s, dr), F32), jax.ShapeDtypeStruct((bl, s, dr), BF16)],
        scratch_shapes=[pltpu.VMEM((s, rb), F32)] * 4,
        compiler_params=_cp(2),
    )(zr3, zr3, cw, cb, wa, ba, wi, bi, lam)


def _rnn_bwd(zr3, h3, dy3, cw, cb, wa, ba, wi, bi, lam):
    bl, s, _ = zr3.shape
    nb, rb, _ = wa.shape
    dr = nb * rb
    kw = cw.shape[0]
    act = lambda off: pl.BlockSpec((1, s, rb), lambda n, b, off=off: (b, 0, off + n))
    par = pl.BlockSpec((1, rb), lambda n, b: (0, n))
    wsp = pl.BlockSpec((1, rb, rb), lambda n, b: (n, 0, 0))
    cws = pl.BlockSpec((kw, rb), lambda n, b: (0, n))

    def kern(xr_ref, gr_ref, h_ref, dy_ref, cw_ref, cb_ref, wa_ref, ba_ref, wi_ref, bi_ref, lam_ref,
             dxr_ref, dgr_ref, dcw_ref, dcb_ref, dwa_ref, dba_ref, dwi_ref, dbi_ref, dlam_ref, a_s, u_s, h_s, p_s):
        b = pl.program_id(1)
        xr, cwv, lamv = xr_ref[0], cw_ref[...], lam_ref[...]
        wav, wiv = wa_ref[0], wi_ref[0]
        xc, r, i, sp, a, mult = _rnn_gates(xr, cwv, cb_ref[...], wav, ba_ref[...], wiv, bi_ref[...], lamv)
        h, dy = h_ref[0], dy_ref[0]
        ge, dge = _gelu_and_grad(gr_ref[0])
        dgr_ref[0] = (dy * h * dge).astype(BF16)
        a_s[...] = _shift_up(a, 1)
        u_s[...] = dy * ge
        _segment_scan(a_s, u_s, h_s, p_s, reverse=True)
        g = h_s[...]
        da = g * _shift_down(h, 1)
        ix = i * xc
        dlog_a = da * a + (g * ix) * (-(a * a) / mult)
        di = g * mult * xc
        dpa = (dlog_a * (-LRU_C * sp)) * r * (1.0 - r)
        dpi = di * i * (1.0 - i)
        dxc = g * mult * i + _dot(dpa, wav, _NT) + _dot(dpi, wiv, _NT)
        dxr = jnp.zeros_like(dxc)
        dcw_rows = []
        for k in range(kw):
            dxr = dxr + _shift_up(dxc, kw - 1 - k) * cwv[k:k + 1, :]
            dcw_rows.append(jnp.sum(dxc * _shift_down(xr, kw - 1 - k), axis=0, keepdims=True))
        dxr_ref[0] = dxr.astype(BF16)

        @pl.when(b == 0)
        def _():
            for ref in (dcw_ref, dcb_ref, dwa_ref, dba_ref, dwi_ref, dbi_ref, dlam_ref):
                ref[...] = jnp.zeros_like(ref)

        for k in range(kw):
            dcw_ref[k:k + 1, :] += dcw_rows[k]
        dcb_ref[...] += jnp.sum(dxc, axis=0, keepdims=True)
        dwa_ref[0] += _dot(xc, dpa, _TN)
        dwi_ref[0] += _dot(xc, dpi, _TN)
        dba_ref[...] += jnp.sum(dpa, axis=0, keepdims=True)
        dbi_ref[...] += jnp.sum(dpi, axis=0, keepdims=True)
        dsp = jnp.sum(dlog_a * (-LRU_C * r), axis=0, keepdims=True)
        dlam_ref[...] += dsp * (-_sigmoid(-lamv))

    vec = jax.ShapeDtypeStruct((1, dr), F32)
    wsh = jax.ShapeDtypeStruct((nb, rb, rb), F32)
    return pl.pallas_call(
        kern, name="rnn_bwd", grid=(nb, bl),
        in_specs=[act(0), act(nb), act(0), act(0), cws, par, wsp, par, wsp, par, par],
        out_specs=[act(0), act(0), cws, par, wsp, par, wsp, par, par],
        out_shape=[jax.ShapeDtypeStruct((bl, s, dr), BF16), jax.ShapeDtypeStruct((bl, s, dr), BF16),
                   jax.ShapeDtypeStruct((kw, dr), F32), vec, wsh, vec, wsh, vec, vec],
        scratch_shapes=[pltpu.VMEM((s, rb), F32)] * 4,
        compiler_params=_cp(2),
    )(zr3, zr3, h3, dy3, cw, cb, wa, ba, wi, bi, lam)


def _tri(n, upper):
    r = lax.broadcasted_iota(jnp.int32, (n, n), 0)
    c = lax.broadcasted_iota(jnp.int32, (n, n), 1)
    return jnp.where((c >= r) if upper else (c <= r), 1.0, 0.0).astype(F32)


def _fgate_fwd(zf3, bf):
    bl, s, w = zf3.shape
    ch = _tile(s, 256, SUBLANES)

    def kern(z_ref, b_ref, f_ref):
        tri = _tri(ch, upper=False)
        carry = jnp.zeros((1, w), F32)
        for j in range(s // ch):
            rows = pl.ds(j * ch, ch)
            lf = -_softplus(-(z_ref[0, rows, :] + b_ref[...]))
            out = jnp.dot(tri, lf, precision=lax.Precision.HIGHEST, preferred_element_type=F32) + carry
            f_ref[0, rows, :] = out
            carry = out[ch - 1:ch, :]

    return pl.pallas_call(
        kern, name="fgate_fwd", grid=(bl,),
        in_specs=[pl.BlockSpec((1, s, w), lambda b: (b, 0, 0)), pl.BlockSpec((1, w), lambda b: (0, 0))],
        out_specs=pl.BlockSpec((1, s, w), lambda b: (b, 0, 0)),
        out_shape=jax.ShapeDtypeStruct((bl, s, w), F32),
        compiler_params=_cp(1),
    )(zf3, bf)


def _fgate_bwd(dfk3, dfq3, zf3, bf):
    bl, s, w = zf3.shape
    ch = _tile(s, 256, SUBLANES)

    def kern(dfk_ref, dfq_ref, z_ref, b_ref, dz_ref, db_ref):
        b = pl.program_id(0)
        tri = _tri(ch, upper=True)
        carry = jnp.zeros((1, w), F32)
        dbsum = jnp.zeros((1, w), F32)
        for j in range(s // ch - 1, -1, -1):
            rows = pl.ds(j * ch, ch)
            df = dfk_ref[0, rows, :] + dfq_ref[0, rows, :]
            dlf = jnp.dot(tri, df, precision=lax.Precision.HIGHEST, preferred_element_type=F32) + carry
            carry = dlf[0:1, :]
            dz = dlf * _sigmoid(-(z_ref[0, rows, :] + b_ref[...]))
            dz_ref[0, rows, :] = dz.astype(BF16)
            dbsum = dbsum + jnp.sum(dz, axis=0, keepdims=True)

        @pl.when(b == 0)
        def _():
            db_ref[...] = jnp.zeros_like(db_ref)

        db_ref[...] += dbsum

    return pl.pallas_call(
        kern, name="fgate_bwd", grid=(bl,),
        in_specs=[pl.BlockSpec((1, s, w), lambda b: (b, 0, 0))] * 3 + [pl.BlockSpec((1, w), lambda b: (0, 0))],
        out_specs=[pl.BlockSpec((1, s, w), lambda b: (b, 0, 0)), pl.BlockSpec((1, w), lambda b: (0, 0))],
        out_shape=[jax.ShapeDtypeStruct((bl, s, w), BF16), jax.ShapeDtypeStruct((1, w), F32)],
        compiler_params=_cp(1),
    )(dfk3, dfq3, zf3, bf)


def _lanes(col, width):
    return col if width == LANES else jnp.concatenate([col] * (width // LANES), axis=1)


def _causal(sc, row0, col0, transposed):
    r = lax.broadcasted_iota(jnp.int32, sc.shape, 0) + row0
    c = lax.broadcasted_iota(jnp.int32, sc.shape, 1) + col0
    return jnp.where((c >= r) if transposed else (r >= c), sc, NEG_BIG)


def _attn_fwd(q3, kv3, fcol, frow, nh):
    bl, s, da = q3.shape
    dh = da // nh
    tq = _tile(s, 512, LANES)
    nq = s // tq

    def kern(q_ref, k_ref, v_ref, fq_ref, fk_ref, o_ref, lse_ref, m_s, l_s, acc):
        iq, ik = pl.program_id(2), pl.program_id(3)

        @pl.when(ik == 0)
        def _():
            m_s[...] = jnp.full_like(m_s, NEG_BIG)
            l_s[...] = jnp.zeros_like(l_s)
            acc[...] = jnp.zeros_like(acc)

        def block(masked):
            sc = _dot(q_ref[0], k_ref[0], _NT) + _lanes(fq_ref[0], tq) - fk_ref[0]
            if masked:
                sc = _causal(sc, iq * tq, ik * tq, False)
            m_old = m_s[...]
            m_new = jnp.maximum(m_old, jnp.max(sc, axis=-1, keepdims=True))
            alpha = jnp.exp(m_old - m_new)
            p = jnp.exp(sc - m_new)
            l_s[...] = alpha * l_s[...] + jnp.sum(p, axis=-1, keepdims=True)
            acc[...] = alpha * acc[...] + _dot(p, v_ref[0], _NN)
            m_s[...] = m_new

        pl.when(ik < iq)(functools.partial(block, False))
        pl.when(ik == iq)(functools.partial(block, True))

        @pl.when(ik == nq - 1)
        def _():
            l = l_s[...]
            o_ref[0] = acc[...] / l
            lse_ref[0] = jnp.broadcast_to(m_s[...] + jnp.log(l), (tq, LANES))

    kmap = lambda off: (lambda b, h, iq, ik: (b, jnp.minimum(ik, iq), off + h))
    return pl.pallas_call(
        kern, name="attn_fwd", grid=(bl, nh, nq, nq),
        in_specs=[pl.BlockSpec((1, tq, dh), lambda b, h, iq, ik: (b, iq, h)),
                  pl.BlockSpec((1, tq, dh), kmap(0)), pl.BlockSpec((1, tq, dh), kmap(nh)),
                  pl.BlockSpec((1, tq, LANES), lambda b, h, iq, ik: (b * nh + h, iq, 0)),
                  pl.BlockSpec((1, 1, tq), lambda b, h, iq, ik: (b * nh + h, 0, jnp.minimum(ik, iq)))],
        out_specs=[pl.BlockSpec((1, tq, dh), lambda b, h, iq, ik: (b, iq, h)),
                   pl.BlockSpec((1, tq, LANES), lambda b, h, iq, ik: (b * nh + h, iq, 0))],
        out_shape=[jax.ShapeDtypeStruct((bl, s, da), F32), jax.ShapeDtypeStruct((bl * nh, s, LANES), F32)],
        scratch_shapes=[pltpu.VMEM((tq, 1), F32), pltpu.VMEM((tq, 1), F32), pltpu.VMEM((tq, dh), F32)],
        compiler_params=_cp(4),
    )(q3, kv3, kv3, fcol, frow)


def _attn_bwd_dq(q3, kv3, do3, o3, lse, fcol, frow, nh, scale):
    bl, s, da = q3.shape
    dh = da // nh
    tq = _tile(s, 512, LANES)
    nq = s // tq

    def kern(q_ref, k_ref, v_ref, do_ref, o_ref, lse_ref, fq_ref, fk_ref, dq_ref, delta_ref, dfq_ref, acc, d_s, r_s):
        iq, ik = pl.program_id(2), pl.program_id(3)

        @pl.when(ik == 0)
        def _():
            acc[...] = jnp.zeros_like(acc)
            r_s[...] = jnp.zeros_like(r_s)
            dl = jnp.sum(do_ref[0] * o_ref[0], axis=-1, keepdims=True)
            d_s[...] = dl
            delta_ref[0] = jnp.broadcast_to(dl, (tq, LANES))

        def block(masked):
            sc = _dot(q_ref[0], k_ref[0], _NT) + _lanes(fq_ref[0], tq) - fk_ref[0]
            if masked:
                sc = _causal(sc, iq * tq, ik * tq, False)
            p = jnp.exp(sc - _lanes(lse_ref[0], tq))
            dp = _dot(do_ref[0], v_ref[0], _NT)
            ds = p * (dp - d_s[...])
            acc[...] += _dot(ds, k_ref[0], _NN)
            r_s[...] += jnp.sum(ds, axis=-1, keepdims=True)

        pl.when(ik < iq)(functools.partial(block, False))
        pl.when(ik == iq)(functools.partial(block, True))

        @pl.when(ik == nq - 1)
        def _():
            dq_ref[0] = (acc[...] * scale).astype(BF16)
            dfq_ref[0] = jnp.broadcast_to(r_s[...], (tq, LANES))

    qmap = lambda b, h, iq, ik: (b, iq, h)
    bmap = lambda b, h, iq, ik: (b * nh + h, iq, 0)
    kmap = lambda off: (lambda b, h, iq, ik: (b, jnp.minimum(ik, iq), off + h))
    col = jax.ShapeDtypeStruct((bl * nh, s, LANES), F32)
    return pl.pallas_call(
        kern, name="attn_bwd_dq", grid=(bl, nh, nq, nq),
        in_specs=[pl.BlockSpec((1, tq, dh), qmap), pl.BlockSpec((1, tq, dh), kmap(0)), pl.BlockSpec((1, tq, dh), kmap(nh)),
                  pl.BlockSpec((1, tq, dh), qmap), pl.BlockSpec((1, tq, dh), qmap),
                  pl.BlockSpec((1, tq, LANES), bmap), pl.BlockSpec((1, tq, LANES), bmap),
                  pl.BlockSpec((1, 1, tq), lambda b, h, iq, ik: (b * nh + h, 0, jnp.minimum(ik, iq)))],
        out_specs=[pl.BlockSpec((1, tq, dh), qmap), pl.BlockSpec((1, tq, LANES), bmap), pl.BlockSpec((1, tq, LANES), bmap)],
        out_shape=[jax.ShapeDtypeStruct((bl, s, da), BF16), col, col],
        scratch_shapes=[pltpu.VMEM((tq, dh), F32), pltpu.VMEM((tq, 1), F32), pltpu.VMEM((tq, 1), F32)],
        compiler_params=_cp(4),
    )(q3, kv3, kv3, do3, o3, lse, fcol, frow)


def _attn_bwd_dkv(q3, kv3, do3, lse_row, delta_row, fcol, frow, nh):
    bl, s, da = q3.shape
    dh = da // nh
    tk = _tile(s, 512, LANES)
    nk = s // tk

    def kern(q_ref, k_ref, v_ref, do_ref, lse_ref, dl_ref, fk_ref, fq_ref, dk_ref, dv_ref, df_ref, dk_acc, dv_acc, df_acc):
        ik, iq = pl.program_id(2), pl.program_id(3)

        @pl.when(iq == 0)
        def _():
            dk_acc[...] = jnp.zeros_like(dk_acc)
            dv_acc[...] = jnp.zeros_like(dv_acc)
            df_acc[...] = jnp.zeros_like(df_acc)

        def block(masked):
            st = _dot(k_ref[0], q_ref[0], _NT) - _lanes(fk_ref[0], tk) + fq_ref[0]
            if masked:
                st = _causal(st, ik * tk, iq * tk, True)
            pt = jnp.exp(st - lse_ref[0])
            dv_acc[...] += _dot(pt, do_ref[0], _NN)
            dpt = _dot(v_ref[0], do_ref[0], _NT)
            dst = pt * (dpt - dl_ref[0])
            dk_acc[...] += _dot(dst, q_ref[0], _NN)
            df_acc[...] += jnp.sum(dst, axis=-1, keepdims=True)

        pl.when(iq > ik)(functools.partial(block, False))
        pl.when(iq == ik)(functools.partial(block, True))

        @pl.when(iq == nk - 1)
        def _():
            dk_ref[0] = dk_acc[...].astype(BF16)
            dv_ref[0] = dv_acc[...].astype(BF16)
            df_ref[0] = jnp.broadcast_to(-df_acc[...], (tk, LANES))

    qmap = lambda b, h, ik, iq: (b, jnp.maximum(iq, ik), h)
    rmap = lambda b, h, ik, iq: (b * nh + h, 0, jnp.maximum(iq, ik))
    kmap = lambda off: (lambda b, h, ik, iq: (b, ik, off + h))
    bmap = lambda b, h, ik, iq: (b * nh + h, ik, 0)
    return pl.pallas_call(
        kern, name="attn_bwd_dkv", grid=(bl, nh, nk, nk),
        in_specs=[pl.BlockSpec((1, tk, dh), qmap), pl.BlockSpec((1, tk, dh), kmap(0)), pl.BlockSpec((1, tk, dh), kmap(nh)),
                  pl.BlockSpec((1, tk, dh), qmap), pl.BlockSpec((1, 1, tk), rmap), pl.BlockSpec((1, 1, tk), rmap),
                  pl.BlockSpec((1, tk, LANES), bmap), pl.BlockSpec((1, 1, tk), rmap)],
        out_specs=[pl.BlockSpec((1, tk, dh), kmap(0)), pl.BlockSpec((1, tk, dh), kmap(0)), pl.BlockSpec((1, tk, LANES), bmap)],
        out_shape=[jax.ShapeDtypeStruct((bl, s, da), BF16), jax.ShapeDtypeStruct((bl, s, da), BF16),
                   jax.ShapeDtypeStruct((bl * nh, s, LANES), F32)],
        scratch_shapes=[pltpu.VMEM((tk, dh), F32), pltpu.VMEM((tk, dh), F32), pltpu.VMEM((tk, 1), F32)],
        compiler_params=_cp(4),
    )(q3, kv3, kv3, do3, lse_row, delta_row, fcol, frow)


def _merge_fwd(mg3, pr3, pa3):
    bl, s, d = pr3.shape
    ts = _tile(s, 256, SUBLANES)
    half = lambda j: pl.BlockSpec((1, ts, d), lambda b, t, j=j: (b, t, j))

    def kern(mr_ref, ma_ref, pr_ref, pa_ref, o_ref):
        o_ref[0] = (_sigmoid(mr_ref[0]) * pr_ref[0] + _sigmoid(ma_ref[0]) * pa_ref[0]).astype(BF16)

    return pl.pallas_call(
        kern, name="merge_fwd", grid=(bl, s // ts),
        in_specs=[half(0), half(1)] + _act_specs(ts, d, 2), out_specs=_act_specs(ts, d, 1)[0],
        out_shape=jax.ShapeDtypeStruct((bl, s, d), BF16), compiler_params=_cp(2),
    )(mg3, mg3, pr3, pa3)


def _merge_bwd(dm3, mg3, pr3, pa3):
    bl, s, d = pr3.shape
    ts = _tile(s, 256, SUBLANES)
    half = lambda j: pl.BlockSpec((1, ts, d), lambda b, t, j=j: (b, t, j))

    def kern(dm_ref, mr_ref, ma_ref, pr_ref, pa_ref, dpr_ref, dpa_ref, dmr_ref, dma_ref):
        dm = dm_ref[0]
        gr, ga = _sigmoid(mr_ref[0]), _sigmoid(ma_ref[0])
        dpr_ref[0] = (gr * dm).astype(BF16)
        dpa_ref[0] = (ga * dm).astype(BF16)
        dmr_ref[0] = (dm * pr_ref[0] * gr * (1.0 - gr)).astype(BF16)
        dma_ref[0] = (dm * pa_ref[0] * ga * (1.0 - ga)).astype(BF16)

    return pl.pallas_call(
        kern, name="merge_bwd", grid=(bl, s // ts),
        in_specs=_act_specs(ts, d, 1) + [half(0), half(1)] + _act_specs(ts, d, 2), out_specs=_act_specs(ts, d, 4),
        out_shape=[jax.ShapeDtypeStruct((bl, s, d), BF16)] * 4, compiler_params=_cp(2),
    )(dm3, mg3, mg3, pr3, pa3)


def _ffn_conv(gf, cw, cb):
    kw = cw.shape[0]
    y = cb
    for k in range(kw):
        y = y + _shift_down(gf, kw - 1 - k) * cw[k:k + 1, :]
    return y


def _ffn_act_fwd(up3, cw, cb):
    bl, s, two = up3.shape
    dff = two // 2
    kw = cw.shape[0]
    tc = _tile(dff, 256, LANES)
    nc = dff // tc

    def kern(gf_ref, uf_ref, cw_ref, cb_ref, o_ref):
        o_ref[0] = (_gelu(_ffn_conv(gf_ref[0], cw_ref[...], cb_ref[...])) * uf_ref[0]).astype(BF16)

    act = lambda off: pl.BlockSpec((1, s, tc), lambda b, j, off=off: (b, 0, off + j))
    return pl.pallas_call(
        kern, name="ffn_act_fwd", grid=(bl, nc),
        in_specs=[act(0), act(nc), pl.BlockSpec((kw, tc), lambda b, j: (0, j)), pl.BlockSpec((1, tc), lambda b, j: (0, j))],
        out_specs=act(0), out_shape=jax.ShapeDtypeStruct((bl, s, dff), BF16), compiler_params=_cp(2),
    )(up3, up3, cw, cb)


def _ffn_act_bwd(up3, dact3, cw, cb):
    bl, s, two = up3.shape
    dff = two // 2
    kw = cw.shape[0]
    tc = _tile(dff, 256, LANES)
    nc = dff // tc

    def kern(gf_ref, uf_ref, da_ref, cw_ref, cb_ref, dgf_ref, duf_ref, dcw_ref, dcb_ref):
        b = pl.program_id(1)
        gf, cwv, da = gf_ref[0], cw_ref[...], da_ref[0]
        ge, dge = _gelu_and_grad(_ffn_conv(gf, cwv, cb_ref[...]))
        duf_ref[0] = (da * ge).astype(BF16)
        dgc = da * uf_ref[0] * dge
        dgf = jnp.zeros_like(dgc)
        rows = []
        for k in range(kw):
            dgf = dgf + _shift_up(dgc, kw - 1 - k) * cwv[k:k + 1, :]
            rows.append(jnp.sum(dgc * _shift_down(gf, kw - 1 - k), axis=0, keepdims=True))
        dgf_ref[0] = dgf.astype(BF16)

        @pl.when(b == 0)
        def _():
            dcw_ref[...] = jnp.zeros_like(dcw_ref)
            dcb_ref[...] = jnp.zeros_like(dcb_ref)

        for k in range(kw):
            dcw_ref[k:k + 1, :] += rows[k]
        dcb_ref[...] += jnp.sum(dgc, axis=0, keepdims=True)

    act = lambda off: pl.BlockSpec((1, s, tc), lambda j, b, off=off: (b, 0, off + j))
    cws = pl.BlockSpec((kw, tc), lambda j, b: (0, j))
    cbs = pl.BlockSpec((1, tc), lambda j, b: (0, j))
    return pl.pallas_call(
        kern, name="ffn_act_bwd", grid=(nc, bl),
        in_specs=[act(0), act(nc), act(0), cws, cbs], out_specs=[act(0), act(0), cws, cbs],
        out_shape=[jax.ShapeDtypeStruct((bl, s, dff), BF16), jax.ShapeDtypeStruct((bl, s, dff), BF16),
                   jax.ShapeDtypeStruct((kw, dff), F32), jax.ShapeDtypeStruct((1, dff), F32)],
        compiler_params=_cp(2),
    )(up3, up3, dact3, cw, cb)


_HBM = pl.BlockSpec(memory_space=pltpu.HBM)


def _place():
    x, y, c = lax.axis_index("x"), lax.axis_index("y"), lax.axis_index("c")
    chips = dict(me=2 * x + y, nx=2 * (1 - x) + y, ny=2 * x + (1 - y), diag=2 * (1 - x) + (1 - y))
    peers = dict(nx=(1 - x, y, c), ny=(x, 1 - y, c), sib=(x, y, 1 - c))
    return c, chips, peers


def _remote(src, dst, sems, k, to):
    return pltpu.make_async_remote_copy(src_ref=src, dst_ref=dst, send_sem=sems[0].at[k], recv_sem=sems[1].at[k],
                                        device_id=to, device_id_type=MESH)


RS_STEPS = 2


def _piece(q, idx, n=1):
    start = idx * q
    if not isinstance(start, int):
        start = pl.multiple_of(start, SUBLANES)
    return pl.ds(start, n * q)


def _all_gather_chips(xs, name):
    nt = len(xs)

    def body(*refs):
        x_refs, o_refs = refs[:nt], refs[nt:2 * nt]
        send_sems, recv_sems, local_sems = refs[2 * nt:]
        c, chip, peer = _place()
        sems = (send_sems, recv_sems)
        me, nx, ny, dg = chip["me"], chip["nx"], chip["ny"], chip["diag"]
        sends, copies = [], []

        def arrive(k, dst):
            _remote(dst, dst, sems, k, peer["sib"]).wait_recv()

        def pass_on(k, blk, to):
            cp = _remote(blk, blk, sems, k, peer[to])
            cp.start()
            sends.append(cp)

        for t in range(nt):
            q = xs[t].shape[0] // 4
            cp = pltpu.make_async_copy(x_refs[t], o_refs[t].at[me], local_sems.at[t])
            cp.start()
            copies.append(cp)
            half = _piece(q, 2 * c, 2)
            for k, to in ((0, "nx"), (1, "ny")):
                cp = _remote(x_refs[t].at[half], o_refs[t].at[me, half], sems, 8 * t + k, peer[to])
                cp.start()
                sends.append(cp)
        for t in range(nt):
            q, o, k0 = xs[t].shape[0] // 4, o_refs[t], 8 * t
            half, sub0, sub1 = _piece(q, 2 * c, 2), _piece(q, 2 * c), _piece(q, 2 * c + 1)
            arrive(k0 + 0, o.at[nx, half])
            pass_on(k0 + 2, o.at[nx, sub0], "ny")
            pass_on(k0 + 4, o.at[nx, half], "sib")
            arrive(k0 + 1, o.at[ny, half])
            pass_on(k0 + 3, o.at[ny, sub1], "nx")
            pass_on(k0 + 5, o.at[ny, half], "sib")
            arrive(k0 + 2, o.at[dg, sub0])
            pass_on(k0 + 6, o.at[dg, sub0], "sib")
            arrive(k0 + 3, o.at[dg, sub1])
            pass_on(k0 + 7, o.at[dg, sub1], "sib")
        for t in range(nt):
            q, o, k0 = xs[t].shape[0] // 4, o_refs[t], 8 * t
            arrive(k0 + 4, o.at[nx, _piece(q, 2 * (1 - c), 2)])
            arrive(k0 + 5, o.at[ny, _piece(q, 2 * (1 - c), 2)])
            arrive(k0 + 6, o.at[dg, _piece(q, 2 * (1 - c))])
            arrive(k0 + 7, o.at[dg, _piece(q, 2 * (1 - c) + 1)])
        for cp in sends:
            cp.wait_send()
        for cp in copies:
            cp.wait()

    return pl.pallas_call(
        body, name=name, in_specs=[_HBM] * nt, out_specs=[_HBM] * nt,
        out_shape=[jax.ShapeDtypeStruct((N_CHIPS,) + x.shape, x.dtype) for x in xs],
        scratch_shapes=[pltpu.SemaphoreType.DMA((8 * nt,)), pltpu.SemaphoreType.DMA((8 * nt,)),
                        pltpu.SemaphoreType.DMA((nt,))],
    )(*xs)


def _exchange(name, xs, out_shapes, plan):
    nt = len(xs)

    def body(*refs):
        x_refs, o_refs = refs[:nt], refs[nt:2 * nt]
        send_sems, recv_sems = refs[2 * nt:]
        c, chip, peer = _place()
        cps = []
        for t in range(nt):
            for src, dst, to in plan(c, chip, x_refs[t], o_refs[t], xs[t].shape):
                cps.append(_remote(src, dst, (send_sems, recv_sems), len(cps), peer[to]))
        for cp in cps:
            cp.start()
        for cp in cps:
            cp.wait()

    n_copies = nt * len(plan(0, dict(me=0, nx=2, ny=1, diag=3), None, None, xs[0].shape, count_only=True))
    return pl.pallas_call(
        body, name=name, in_specs=[_HBM] * nt, out_specs=[_HBM] * nt,
        out_shape=[jax.ShapeDtypeStruct(s, F32) for s in out_shapes],
        scratch_shapes=[pltpu.SemaphoreType.DMA((n_copies,)), pltpu.SemaphoreType.DMA((n_copies,))],
    )(*xs)


def _plan_sibling(c, chip, g, out, shape, count_only=False):
    if count_only:
        return [None] * N_CHIPS
    q = shape[1] // 4
    return [(g.at[j, _piece(q, 2 * (1 - c), 2)], out.at[j], "sib") for j in range(N_CHIPS)]


def _plan_first(c, chip, p, out, shape, count_only=False):
    if count_only:
        return [None] * 4
    q = shape[1] // 2
    return [(p.at[chip["nx"], _piece(q, 0)], out.at[0], "nx"), (p.at[chip["diag"], _piece(q, 0)], out.at[1], "nx"),
            (p.at[chip["ny"], _piece(q, 1)], out.at[2], "ny"), (p.at[chip["diag"], _piece(q, 1)], out.at[3], "ny")]


def _plan_second(c, chip, p, out, shape, count_only=False):
    if count_only:
        return [None] * 2
    return [(p.at[1], out.at[0], "ny"), (p.at[3], out.at[1], "nx")]


def _rs_last(ps):
    nt = len(ps)

    def body(*refs):
        p_refs, o_refs = refs[:nt], refs[nt:2 * nt]
        send_sems, recv_sems, local_sems = refs[2 * nt:]
        c, _, peer = _place()
        sems = (send_sems, recv_sems)
        started = []
        for t in range(nt):
            q = ps[t].shape[0] // 2
            mine = pltpu.make_async_copy(p_refs[t], o_refs[t].at[_piece(q, 2 * c, 2)], local_sems.at[t])
            mine.start()
            cp = _remote(p_refs[t], o_refs[t].at[_piece(q, 2 * c, 2)], sems, t, peer["sib"])
            cp.start()
            started.append((mine, cp))
        for t in range(nt):
            q = ps[t].shape[0] // 2
            mine, cp = started[t]
            cp.wait_send()
            _remote(p_refs[t], o_refs[t].at[_piece(q, 2 * (1 - c), 2)], sems, t, peer["sib"]).wait_recv()
            mine.wait()

    return pl.pallas_call(
        body, name="rs_last", in_specs=[_HBM] * nt, out_specs=[_HBM] * nt,
        out_shape=[jax.ShapeDtypeStruct((2 * p.shape[0], p.shape[1]), F32) for p in ps],
        scratch_shapes=[pltpu.SemaphoreType.DMA((nt,)), pltpu.SemaphoreType.DMA((nt,)), pltpu.SemaphoreType.DMA((nt,))],
    )(*ps)


def _add_stage(name, grid, a_list, b_list, a_map, b_map, tbs, out_shapes, out_map, prefetch=None):
    nt = len(a_list)
    lead = lambda shape: (None,) * (len(shape) - 2)

    def kern(*refs):
        refs = refs[(1 if prefetch is not None else 0):]
        for t in range(nt):
            refs[2 * nt + t][...] = refs[t][...] + refs[nt + t][...]

    in_specs = [pl.BlockSpec(lead(a.shape) + (tb, a.shape[-1]), a_map) for a, tb in zip(a_list, tbs)]
    in_specs += [pl.BlockSpec(lead(b.shape) + (tb, b.shape[-1]), b_map) for b, tb in zip(b_list, tbs)]
    out_specs = [pl.BlockSpec(lead(s) + (tb, s[-1]), out_map) for s, tb in zip(out_shapes, tbs)]
    out_shape = [jax.ShapeDtypeStruct(s, F32) for s in out_shapes]
    if prefetch is None:
        return pl.pallas_call(kern, name=name, grid=grid, in_specs=in_specs, out_specs=out_specs, out_shape=out_shape,
                              compiler_params=_cp(len(grid)))(*a_list, *b_list)
    return pl.pallas_call(
        kern, name=name,
        grid_spec=pltpu.PrefetchScalarGridSpec(num_scalar_prefetch=1, grid=grid, in_specs=in_specs, out_specs=out_specs),
        out_shape=out_shape, compiler_params=_cp(len(grid)))(prefetch, *a_list, *b_list)


def _reduce_scatter_chips(gs):
    x, y, c = lax.axis_index("x"), lax.axis_index("y"), lax.axis_index("c")
    me, nx, ny = 2 * x + y, 2 * (1 - x) + y, 2 * x + (1 - y)
    st = RS_STEPS
    qs = [g.shape[1] // 4 for g in gs]
    tbs = [q // st for q in qs]
    for g, tb in zip(gs, tbs):
        assert g.shape[1] == 4 * st * tb and tb % SUBLANES == 0, g.shape
    cols = [g.shape[2] for g in gs]

    got = _exchange("rs_sibling", gs, [(N_CHIPS, 2 * q, cc) for q, cc in zip(qs, cols)], _plan_sibling)
    p0 = _add_stage("rs_add_sibling", (N_CHIPS, 2, st), gs, got,
                    lambda j, h, s, c_ref: (j, (2 * c_ref[0] + h) * st + s, 0), lambda j, h, s, c_ref: (j, h * st + s, 0),
                    tbs, [(N_CHIPS, 2 * q, cc) for q, cc in zip(qs, cols)], lambda j, h, s, c_ref: (j, h * st + s, 0),
                    prefetch=jnp.reshape(c, (1,)).astype(jnp.int32))
    got = _exchange("rs_first", p0, [(4, q, cc) for q, cc in zip(qs, cols)], _plan_first)
    p1 = _add_stage("rs_add_first", (4, st), p0, got,
                    lambda k, s, i_ref: (i_ref[k], (k // 2) * st + s, 0), lambda k, s, i_ref: (k, s, 0),
                    tbs, [(4, q, cc) for q, cc in zip(qs, cols)], lambda k, s, i_ref: (k, s, 0),
                    prefetch=jnp.stack([me, ny, me, nx]).astype(jnp.int32))
    got = _exchange("rs_second", p1, [(2, q, cc) for q, cc in zip(qs, cols)], _plan_second)
    p2 = _add_stage("rs_add_second", (2, st), p1, got, lambda h, s: (2 * h, s, 0), lambda h, s: (h, s, 0),
                    tbs, [(2 * q, cc) for q, cc in zip(qs, cols)], lambda h, s: (h * st + s, 0))
    return _rs_last(p2)


def _adamw(g, w, m, v, name):
    rows, cc = g.shape
    tr = _tile(rows, max(SUBLANES, (1 << 18) // cc), SUBLANES)
    k1 = 1.0 - ADAM_B1 ** ADAM_STEP
    k2 = 1.0 - ADAM_B2 ** ADAM_STEP

    def kern(g_ref, w_ref, m_ref, v_ref, d_ref, nm_ref, nv_ref):
        gv = g_ref[...]
        nm = ADAM_B1 * m_ref[...] + (1.0 - ADAM_B1) * gv
        nv = ADAM_B2 * v_ref[...] + (1.0 - ADAM_B2) * (gv * gv)
        nm_ref[...] = nm
        nv_ref[...] = nv
        d_ref[...] = -ADAM_LR * ((nm / k1) / (jnp.sqrt(nv / k2) + ADAM_EPS) + ADAM_WD * w_ref[...])

    spec = pl.BlockSpec((tr, cc), lambda t: (t, 0))
    return pl.pallas_call(
        kern, name=name, grid=(rows // tr,), in_specs=[spec] * 4, out_specs=[spec] * 3,
        out_shape=[jax.ShapeDtypeStruct((rows, cc), F32)] * 3, compiler_params=_cp(1),
    )(g, w, m, v)


def _flat_pad(parts, total):
    flat = jnp.concatenate([p.reshape(-1) for p in parts])
    return jnp.pad(flat, (0, total - flat.shape[0]))


def _split_flat(flat, shapes):
    out, pos = [], 0
    for shp in shapes:
        size = math.prod(shp)
        out.append(flat[pos:pos + size].reshape(shp))
        pos += size
    return out


def _cols_of_chunks(chunks, lo, hi):
    width = chunks.shape[2]
    parts = []
    for j in range(chunks.shape[0]):
        a, b = max(lo, j * width), min(hi, (j + 1) * width)
        if a < b:
            parts.append(chunks[j, :, a - j * width:b - j * width])
    return parts[0] if len(parts) == 1 else jnp.concatenate(parts, axis=1)


def _chunks_of_cols(segments, n_chunks):
    total = sum(s.shape[1] for s in segments)
    width = total // n_chunks
    chunks = []
    for j in range(n_chunks):
        lo, hi, pos, parts = j * width, (j + 1) * width, 0, []
        for s in segments:
            a, b = max(lo, pos), min(hi, pos + s.shape[1])
            if a < b:
                parts.append(s[:, a - pos:b - pos])
            pos += s.shape[1]
        chunks.append(parts[0] if len(parts) == 1 else jnp.concatenate(parts, axis=1))
    return jnp.stack(chunks)


_WEIGHTS = ['w_ada', 'b_ada', 'g_norm1', 'w_in', 'w_rnn_conv', 'b_rnn_conv', 'w_lru_a', 'b_lru_a', 'w_lru_i', 'b_lru_i',
            'lru_lambda', 'b_fgate', 'w_proj_rnn', 'w_proj_attn', 'w_out', 'g_norm2', 'w_ffn_up', 'w_ffn_conv',
            'b_ffn_conv', 'w_ffn_down', 'w_ada_final', 'b_ada_final', 'g_final']
_MATMUL = dict(w_ada=True, w_in=True, w_proj_rnn=False, w_proj_attn=False, w_out=False, w_ffn_up=True,
               w_ffn_down=False, w_ada_final=True)
_CONV = ['w_rnn_conv', 'w_ffn_conv']
_REPLICATED = [n for n in _WEIGHTS if n not in _MATMUL and n not in _CONV]


def kernel(x, c, w_ada, b_ada, g_norm1, w_in, w_rnn_conv, b_rnn_conv, w_lru_a, b_lru_a, w_lru_i, b_lru_i, lru_lambda, b_fgate, w_proj_rnn, w_proj_attn, w_out, g_norm2, w_ffn_up, w_ffn_conv, b_ffn_conv, w_ffn_down, w_ada_final, b_ada_final, g_final, loss_target, m_w_ada, m_b_ada, m_g_norm1, m_w_in, m_w_rnn_conv, m_b_rnn_conv, m_w_lru_a, m_b_lru_a, m_w_lru_i, m_b_lru_i, m_lru_lambda, m_b_fgate, m_w_proj_rnn, m_w_proj_attn, m_w_out, m_g_norm2, m_w_ffn_up, m_w_ffn_conv, m_b_ffn_conv, m_w_ffn_down, m_w_ada_final, m_b_ada_final, m_g_final, v_w_ada, v_b_ada, v_g_norm1, v_w_in, v_w_rnn_conv, v_b_rnn_conv, v_w_lru_a, v_b_lru_a, v_w_lru_i, v_b_lru_i, v_lru_lambda, v_b_fgate, v_w_proj_rnn, v_w_proj_attn, v_w_out, v_g_norm2, v_w_ffn_up, v_w_ffn_conv, v_b_ffn_conv, v_w_ffn_down, v_w_ada_final, v_b_ada_final, v_g_final):
    args = locals()
    shape_of = {n: args[n].shape for n in _WEIGHTS}

    def view(a):
        if a.ndim >= 3:
            return a[0]
        return a[None, :] if a.ndim == 1 else a

    w2 = {n: view(args[n]) for n in _WEIGHTS}
    m2 = {n: args['m_' + n].reshape(w2[n].shape) for n in _WEIGHTS}
    v2 = {n: args['v_' + n].reshape(w2[n].shape) for n in _WEIGHTS}

    bl, s, d = x.shape
    t = bl * s
    nh = b_fgate.shape[-1]
    nb, rb = w_lru_a.shape[1], w_lru_a.shape[2]
    dr = nb * rb
    da = w2['w_proj_attn'].shape[0] * N_CHIPS
    dh = da // nh
    dff = w2['w_ffn_conv'].shape[1] * N_CHIPS
    scale = dh ** -0.5
    chip = 2 * lax.axis_index("x") + lax.axis_index("y")

    names = list(_MATMUL)
    gathered = dict(zip(names, _all_gather_chips([w2[n].astype(BF16) for n in names], "ag_weights")))
    n_conv = sum(w2[n].size for n in _CONV)
    rows_conv = -(-n_conv // (FLAT_COLS * 32)) * 32
    conv_local = _flat_pad([w2[n] for n in _CONV], rows_conv * FLAT_COLS).reshape(rows_conv, FLAT_COLS)
    conv_all = _all_gather_chips([conv_local], "ag_conv")[0].reshape(N_CHIPS, -1)
    conv_full, pos = {}, 0
    for n in _CONV:
        r, n4 = w2[n].shape
        blocks = conv_all[:, pos:pos + r * n4].reshape(N_CHIPS, r, n4)
        conv_full[n] = jnp.concatenate([blocks[j] for j in range(N_CHIPS)], axis=1)
        pos += r * n4
    rowmajor = lambda n: gathered[n].reshape(-1, gathered[n].shape[2])
    w_proj_rnn_f, w_proj_attn_f, w_out_f, w_ffn_down_f = (rowmajor(n) for n in ('w_proj_rnn', 'w_proj_attn', 'w_out', 'w_ffn_down'))
    ada_chunk, adaf_chunk, up_chunk = (w2[n].shape[1] for n in ('w_ada', 'w_ada_final', 'w_ffn_up'))

    o_q, o_k, o_fl = 2 * dr, 2 * dr + da, 2 * dr + 3 * da
    o_mg = o_fl + nh
    g_in_w = gathered['w_in']
    w_rnn, w_q = _cols_of_chunks(g_in_w, 0, o_q), _cols_of_chunks(g_in_w, o_q, o_k)
    w_kv, w_mg = _cols_of_chunks(g_in_w, o_k, o_fl), _cols_of_chunks(g_in_w, o_mg, o_mg + 2 * d)
    w_fl = jnp.pad(_cols_of_chunks(g_in_w, o_fl, o_mg), ((0, 0), (0, LANES - nh)))
    w_main = jnp.concatenate([w_rnn, w_q, w_kv, w_mg], axis=1)
    bf_pad = jnp.pad(w2['b_fgate'], ((0, 0), (0, LANES - nh)))

    c_act = _silu_pad(c, 16)
    mod = _mm(c_act, gathered['w_ada'], "nn", name="ada_fwd", bias=w2['b_ada'], b_chunk=ada_chunk)[:bl]
    sh1, sc1, gt1, sh2, sc2, gt2 = [mod[:, i * d:(i + 1) * d].reshape(bl, 1, d) for i in range(6)]
    modf = _mm(c_act, gathered['w_ada_final'], "nn", name="ada_final_fwd", bias=w2['b_ada_final'], b_chunk=adaf_chunk)[:bl]
    shf, scf = modf[:, :d].reshape(bl, 1, d), modf[:, d:].reshape(bl, 1, d)

    h1 = _norm_mod_fwd(x, w2['g_norm1'], sh1, sc1)
    h1f = h1.reshape(t, d)
    zr = _mm(h1f, w_rnn, "nn", name="in_rnn").reshape(bl, s, 2 * dr)
    q3 = _mm(h1f, w_q, "nn", name="in_q", out_dtype=BF16, scale=scale).reshape(bl, s, da)
    kv3 = _mm(h1f, w_kv, "nn", name="in_kv", out_dtype=BF16).reshape(bl, s, 2 * da)
    mg3 = _mm(h1f, w_mg, "nn", name="in_mg").reshape(bl, s, 2 * d)
    zf3 = _mm(h1f, w_fl, "nn", name="in_fl").reshape(bl, s, LANES)

    lru = (conv_full['w_rnn_conv'], w2['b_rnn_conv'], w2['w_lru_a'], w2['b_lru_a'], w2['w_lru_i'], w2['b_lru_i'], w2['lru_lambda'])
    hseq, y_rnn = _rnn_fwd(zr, *lru)

    f3 = _fgate_fwd(zf3, bf_pad)
    f_heads = f3[:, :, :nh].transpose(0, 2, 1).reshape(bl * nh, s)
    fcol = jnp.broadcast_to(f_heads[:, :, None], (bl * nh, s, LANES))
    frow = f_heads.reshape(bl * nh, 1, s)
    o3, lse = _attn_fwd(q3, kv3, fcol, frow, nh)

    pr3 = _mm(y_rnn.reshape(t, dr), w_proj_rnn_f, "nn", name="proj_rnn").reshape(bl, s, d)
    pa3 = _mm(o3.reshape(t, da), w_proj_attn_f, "nn", name="proj_attn").reshape(bl, s, d)
    merged = _merge_fwd(mg3, pr3, pa3)
    mo3 = _mm(merged.reshape(t, d), w_out_f, "nn", name="mix_out").reshape(bl, s, d)
    x1, h2 = _resid_norm_fwd(x, mo3, gt1, w2['g_norm2'], sh2, sc2)
    h2f = h2.reshape(t, d)
    up3 = _mm(h2f, gathered['w_ffn_up'], "nn", name="ffn_up", b_chunk=up_chunk).reshape(bl, s, 2 * dff)
    act3 = _ffn_act_fwd(up3, conv_full['w_ffn_conv'], w2['b_ffn_conv'])
    yf3 = _mm(act3.reshape(t, dff), w_ffn_down_f, "nn", name="ffn_down").reshape(bl, s, d)

    dx2, dshf, dscf, dg_final, loss_part = _final_fwd_bwd(x1, yf3, gt2, w2['g_final'], shf, scf, loss_target)
    loss = lax.psum(loss_part[0, 0], ("x", "y", "c"))

    dyf, dgt2 = _gate_bwd(dx2, yf3, gt2, "ffn_gate_bwd")
    dyf_f = dyf.reshape(t, d)
    g_ffn_down = _mm(act3.reshape(t, dff), dyf_f, "tn", name="dw_ffn_down")
    dact3 = _mm(dyf_f, w_ffn_down_f, "nt", name="d_ffn_act").reshape(bl, s, dff)
    dgf, duf, g_ffn_conv, g_b_ffn_conv = _ffn_act_bwd(up3, dact3, conv_full['w_ffn_conv'], w2['b_ffn_conv'])
    dgf_f, duf_f = dgf.reshape(t, dff), duf.reshape(t, dff)
    g_ffn_up = jnp.concatenate([_mm(h2f, dgf_f, "tn", name="dw_ffn_up_gate", out_chunk=up_chunk),
                                _mm(h2f, duf_f, "tn", name="dw_ffn_up_value", out_chunk=up_chunk)], axis=0)
    dh2 = _mm([dgf_f, duf_f], gathered['w_ffn_up'], "nt", name="d_h2", b_chunk=up_chunk).reshape(bl, s, d)
    dx1, dsh2, dsc2, dg_norm2 = _norm_mod_bwd(dh2, x1, dx2, w2['g_norm2'], sc2, "norm2_bwd")

    dmo, dgt1 = _gate_bwd(dx1, mo3, gt1, "mix_gate_bwd")
    dmo_f = dmo.reshape(t, d)
    g_out = _mm(merged.reshape(t, d), dmo_f, "tn", name="dw_out")
    dm3 = _mm(dmo_f, w_out_f, "nt", name="d_merged").reshape(bl, s, d)
    dpr, dpa, dmr, dma = _merge_bwd(dm3, mg3, pr3, pa3)
    g_proj_rnn = _mm(y_rnn.reshape(t, dr), dpr.reshape(t, d), "tn", name="dw_proj_rnn")
    g_proj_attn = _mm(o3.reshape(t, da), dpa.reshape(t, d), "tn", name="dw_proj_attn")
    dyr3 = _mm(dpr.reshape(t, d), w_proj_rnn_f, "nt", name="d_y_rnn").reshape(bl, s, dr)
    do3 = _mm(dpa.reshape(t, d), w_proj_attn_f, "nt", name="d_y_attn").reshape(bl, s, da)

    dq3, delta, dfq = _attn_bwd_dq(q3, kv3, do3, o3, lse, fcol, frow, nh, scale)
    lse_row = lse[:, :, 0].reshape(bl * nh, 1, s)
    delta_row = delta[:, :, 0].reshape(bl * nh, 1, s)
    dk3, dv3, dfk = _attn_bwd_dkv(q3, kv3, do3, lse_row, delta_row, fcol, frow, nh)
    heads_last = lambda a: jnp.pad(a[:, :, 0].reshape(bl, nh, s).transpose(0, 2, 1), ((0, 0), (0, 0), (0, LANES - nh)))
    dzf3, g_bf = _fgate_bwd(heads_last(dfk), heads_last(dfq), zf3, bf_pad)

    dxr, dgr, g_rnn_conv, g_b_rnn_conv, g_lru_a, g_b_lru_a, g_lru_i, g_b_lru_i, g_lam = _rnn_bwd(zr, hseq, dyr3, *lru)

    dz = [a.reshape(t, -1) for a in (dxr, dgr, dq3, dk3, dv3, dmr, dma)]
    dzf_f = dzf3.reshape(t, LANES)
    seg_names = ("xr", "gr", "q", "k", "v", "mr", "ma")
    g_seg = [_mm(h1f, a, "tn", name="dw_in_" + n) for n, a in zip(seg_names, dz)]
    g_in_fl = _mm(h1f, dzf_f, "tn", name="dw_in_fl")[:, :nh]
    g_in = _chunks_of_cols(g_seg[:5] + [g_in_fl] + g_seg[5:], N_CHIPS)
    dh1 = _mm(dzf_f, w_fl, "nt", name="d_h1_fl")
    dh1 = _mm(dz, w_main, "nt", name="d_h1", add=dh1, tk=256).reshape(bl, s, d)
    grad_x, dsh1, dsc1, dg_norm1 = _norm_mod_bwd(dh1, x, dx1, w2['g_norm1'], sc1, "norm1_bwd")

    pad_rows = lambda a: jnp.pad(a.reshape(bl, -1), ((0, 16 - bl), (0, 0)))
    dmod = pad_rows(jnp.concatenate([dsh1, dsc1, dgt1, dsh2, dsc2, dgt2], axis=-1))
    dmodf = pad_rows(jnp.concatenate([dshf, dscf], axis=-1))
    g_ada = _mm(c_act, dmod, "tn", name="dw_ada", out_chunk=ada_chunk)
    g_ada_final = _mm(c_act, dmodf, "tn", name="dw_ada_final", out_chunk=adaf_chunk)
    g_b_ada = _rowsum(dmod, "db_ada")
    g_b_ada_final = _rowsum(dmodf, "db_ada_final")

    rowchunks = lambda g: g.reshape(N_CHIPS, g.shape[0] // N_CHIPS, g.shape[1])
    full = dict(w_ada=g_ada, w_in=g_in, w_proj_rnn=rowchunks(g_proj_rnn), w_proj_attn=rowchunks(g_proj_attn),
                w_out=rowchunks(g_out), w_ffn_up=g_ffn_up, w_ffn_down=rowchunks(g_ffn_down), w_ada_final=g_ada_final)
    small = dict(b_ada=g_b_ada, g_norm1=dg_norm1, w_rnn_conv=g_rnn_conv, b_rnn_conv=g_b_rnn_conv, w_lru_a=g_lru_a,
                 b_lru_a=g_b_lru_a, w_lru_i=g_lru_i, b_lru_i=g_b_lru_i, lru_lambda=g_lam, b_fgate=g_bf[:, :nh],
                 g_norm2=dg_norm2, w_ffn_conv=g_ffn_conv, b_ffn_conv=g_b_ffn_conv, b_ada_final=g_b_ada_final,
                 g_final=dg_final)

    small_names = _REPLICATED + _CONV
    n_small = sum(small[n].size for n in small_names)
    rows_q = -(-n_small // (N_CHIPS * FLAT_COLS * 32 * RS_STEPS)) * 32 * RS_STEPS
    small_flat = _flat_pad([small[n] for n in small_names], N_CHIPS * rows_q * FLAT_COLS).reshape(N_CHIPS, rows_q, FLAT_COLS)
    reduced = _reduce_scatter_chips([full[n] for n in names] + [small_flat])
    grad = dict(zip(names, reduced[:-1]))
    small_all = _all_gather_chips([reduced[-1]], "ag_small_grads")[0].reshape(-1)
    grad.update(zip(small_names, _split_flat(small_all, [small[n].shape for n in small_names])))
    for n in _CONV:
        n4 = w2[n].shape[1]
        grad[n] = lax.dynamic_slice_in_dim(grad[n], chip * n4, n4, axis=1)

    delta_w, new_m, new_v = {}, {}, {}
    for n in names:
        delta_w[n], new_m[n], new_v[n] = _adamw(grad[n], w2[n], m2[n], v2[n], "adamw_" + n)
    rows_small = -(-sum(w2[n].size for n in small_names) // (FLAT_COLS * SUBLANES)) * SUBLANES
    flat_small = lambda src: _flat_pad([src[n] for n in small_names], rows_small * FLAT_COLS).reshape(rows_small, FLAT_COLS)
    small_out = _adamw(flat_small(grad), flat_small(w2), flat_small(m2), flat_small(v2), "adamw_small")
    for dst, flat in zip((delta_w, new_m, new_v), small_out):
        dst.update(zip(small_names, _split_flat(flat.reshape(-1), [w2[n].shape for n in small_names])))

    out = [loss, grad_x]
    for src in (grad, delta_w, new_m, new_v):
        out += [src[n].reshape(shape_of[n]) for n in _WEIGHTS]
    return tuple(out)
```

```python
import functools
import math

import jax
import jax.numpy as jnp
from jax import lax
from jax.experimental import pallas as pl
from jax.experimental.pallas import tpu as pltpu

F32 = jnp.float32
BF16 = jnp.bfloat16
MESH = pl.DeviceIdType.MESH

RMS_EPS = 1e-6
LRU_C = 8.0
ADAM_LR = 0.001
ADAM_B1 = 0.9
ADAM_B2 = 0.999
ADAM_EPS = 1e-08
ADAM_WD = 0.01
ADAM_STEP = 10

LANES = 128
SUBLANES = 8
N_CHIPS = 4
FLAT_COLS = 1024
SCAN_SEGMENTS = SUBLANES
VMEM_LIMIT = 48 * 1024 * 1024
NEG_BIG = -1e30


def _cp(n_axes):
    return pltpu.CompilerParams(dimension_semantics=("arbitrary",) * n_axes, vmem_limit_bytes=VMEM_LIMIT)


def _tile(n, target, align):
    if n <= target:
        return n
    t = (target // align) * align
    while t >= align:
        if n % t == 0:
            return t
        t -= align
    return n


def _nice_rows(n, align):
    r = -(-n // align) * align
    while True:
        if r <= 640:
            return r, r
        t = _tile(r, 640, align)
        if 128 <= t <= 640:
            return r, t
        r += align


def _sigmoid(x):
    return jax.nn.sigmoid(x)


def _softplus(x):
    return jnp.maximum(x, 0.0) + jnp.log1p(jnp.exp(-jnp.abs(x)))


def _expm1(x, exp_x):
    small = x * (1.0 + 0.5 * x * (1.0 + (1.0 / 3.0) * x * (1.0 + 0.25 * x)))
    return jnp.where(jnp.abs(x) < 0.05, small, exp_x - 1.0)


_GELU_K = math.sqrt(2.0 / math.pi)
_GELU_C = 0.044715


def _gelu(x):
    t = jnp.tanh(_GELU_K * (x + _GELU_C * x * x * x))
    return 0.5 * x * (1.0 + t)


def _gelu_and_grad(x):
    t = jnp.tanh(_GELU_K * (x + _GELU_C * x * x * x))
    g = 0.5 * x * (1.0 + t)
    dg = 0.5 * (1.0 + t) + 0.5 * x * (1.0 - t * t) * _GELU_K * (1.0 + 3.0 * _GELU_C * x * x)
    return g, dg


def _shift_down(x, k):
    if k == 0:
        return x
    rows = lax.broadcasted_iota(jnp.int32, x.shape, 0)
    return jnp.where(rows >= k, pltpu.roll(x, k, 0), 0.0)


def _shift_up(x, k):
    if k == 0:
        return x
    s = x.shape[0]
    rows = lax.broadcasted_iota(jnp.int32, x.shape, 0)
    return jnp.where(rows < s - k, pltpu.roll(x, s - k, 0), 0.0)


def _dot(a, b, dims):
    return lax.dot_general(a.astype(BF16), b.astype(BF16), (dims, ((), ())), preferred_element_type=F32)


_NN = ((1,), (0,))
_NT = ((1,), (1,))
_TN = ((0,), (0,))


def _mm(a, b, mode, *, name, out_dtype=F32, scale=None, bias=None, add=None, tm=1024, tn=1024, tk=1024,
        b_chunk=None, out_chunk=None):
    pieces = list(a) if isinstance(a, (list, tuple)) else [a]
    ksize = lambda p: p.shape[0] if mode == "tn" else p.shape[1]
    if b_chunk is None:
        brows, bcols = b.shape
    else:
        brows, bcols = b.shape[1], b.shape[0] * b_chunk
    k = sum(ksize(p) for p in pieces)
    if mode == "nt":
        m, n = pieces[0].shape[0], brows
        assert bcols == k, (bcols, k)
    else:
        m, n = (pieces[0].shape[1] if mode == "tn" else pieces[0].shape[0]), bcols
        assert brows == k, (brows, k)
    tm = _tile(m, tm, LANES)
    ncut = n
    if b_chunk is not None and mode != "nt":
        ncut = b_chunk
    if out_chunk is not None:
        ncut = math.gcd(ncut, out_chunk)
    tn = _tile(ncut, tn, LANES)
    kcut = b_chunk if (b_chunk is not None and mode == "nt") else k
    for p in pieces:
        kcut = math.gcd(kcut, ksize(p))
    tk = _tile(kcut, tk, LANES)
    nk = k // tk
    dims = {"nn": _NN, "nt": _NT, "tn": _TN}[mode]
    counts = [ksize(p) // tk for p in pieces]
    starts = [sum(counts[:i]) for i in range(len(pieces))]
    n_pieces = len(pieces)

    def a_spec(s0, cnt):
        kmap = (lambda kk: kk) if n_pieces == 1 else (lambda kk: jnp.clip(kk - s0, 0, cnt - 1))
        if mode == "tn":
            return pl.BlockSpec((tk, tm), lambda i, j, kk: (kmap(kk), i))
        return pl.BlockSpec((tm, tk), lambda i, j, kk: (i, kmap(kk)))

    if b_chunk is None:
        if mode == "nt":
            b_spec = pl.BlockSpec((tn, tk), lambda i, j, kk: (j, kk))
        else:
            b_spec = pl.BlockSpec((tk, tn), lambda i, j, kk: (kk, j))
    elif mode == "nt":
        per_b = b_chunk // tk
        b_spec = pl.BlockSpec((None, tn, tk), lambda i, j, kk: (kk // per_b, j, kk % per_b))
    else:
        per_b = b_chunk // tn
        b_spec = pl.BlockSpec((None, tk, tn), lambda i, j, kk: (j // per_b, kk, j % per_b))
    if out_chunk is None:
        out_spec = pl.BlockSpec((tm, tn), lambda i, j, kk: (i, j))
        out_shape = jax.ShapeDtypeStruct((m, n), out_dtype)
    else:
        per_o = out_chunk // tn
        out_spec = pl.BlockSpec((None, tm, tn), lambda i, j, kk: (j // per_o, i, j % per_o))
        out_shape = jax.ShapeDtypeStruct((n // out_chunk, m, out_chunk), out_dtype)
    in_specs = [a_spec(s0, cnt) for s0, cnt in zip(starts, counts)] + [b_spec]
    args = pieces + [b]
    if bias is not None:
        in_specs.append(pl.BlockSpec((1, tn), lambda i, j, kk: (0, j)))
        args.append(bias)
    if add is not None:
        in_specs.append(pl.BlockSpec((tm, tn), lambda i, j, kk: (i, j)))
        args.append(add)

    def kern(*refs):
        b_ref = refs[n_pieces]
        o_ref = refs[n_pieces + 1 + (bias is not None) + (add is not None)]

        def finish(r):
            if scale is not None:
                r = r * scale
            pos = n_pieces + 1
            if bias is not None:
                r = r + refs[pos][...]
                pos += 1
            if add is not None:
                r = r + refs[pos][...]
            o_ref[...] = r.astype(out_dtype)

        if nk == 1:
            finish(_dot(refs[0][...], b_ref[...], dims))
            return
        acc = refs[-1]
        kk = pl.program_id(2)

        @pl.when(kk == 0)
        def _():
            acc[...] = jnp.zeros_like(acc)

        if n_pieces == 1:
            acc[...] += _dot(refs[0][...], b_ref[...], dims)
        else:
            for idx in range(n_pieces):
                @pl.when((kk >= starts[idx]) & (kk < starts[idx] + counts[idx]))
                def _(idx=idx):
                    acc[...] += _dot(refs[idx][...], b_ref[...], dims)

        @pl.when(kk == nk - 1)
        def _():
            finish(acc[...])

    return pl.pallas_call(
        kern, name=name,
        grid=(m // tm, n // tn, nk),
        in_specs=in_specs, out_specs=out_spec, out_shape=out_shape,
        scratch_shapes=[pltpu.VMEM((tm, tn), F32)] if nk > 1 else [],
        compiler_params=_cp(3),
    )(*args)


def _silu_pad(c, rows):
    bl, d = c.shape

    def kern(c_ref, o_ref):
        o_ref[...] = jnp.zeros_like(o_ref)
        v = c_ref[...]
        o_ref[0:bl, :] = v * _sigmoid(v)

    return pl.pallas_call(kern, name="silu_pad", out_shape=jax.ShapeDtypeStruct((rows, d), F32))(c)


def _rowsum(x, name):
    r, n = x.shape

    def kern(x_ref, o_ref):
        o_ref[...] = jnp.sum(x_ref[...], axis=0, keepdims=True)

    return pl.pallas_call(kern, name=name, out_shape=jax.ShapeDtypeStruct((1, n), F32))(x)


def _norm_parts(x, g):
    r = lax.rsqrt(jnp.mean(x * x, axis=-1, keepdims=True) + RMS_EPS)
    xh = x * r
    return r, xh, xh * g


def _norm_bwd_parts(dh, xh, r, g, sc):
    n = xh * g
    dn = dh * (1.0 + sc)
    dxh = dn * g
    dx = r * (dxh - xh * jnp.mean(dxh * xh, axis=-1, keepdims=True))
    return dx, dh, dh * n, dn * xh


def _act_specs(ts, d, n):
    return [pl.BlockSpec((1, ts, d), lambda b, t: (b, t, 0)) for _ in range(n)]


def _vec_spec(d):
    return pl.BlockSpec((1, 1, d), lambda b, t: (b, 0, 0))


def _par_spec(d):
    return pl.BlockSpec((1, d), lambda b, t: (0, 0))


def _norm_mod_fwd(x3, g, sh, sc):
    bl, s, d = x3.shape
    ts = _tile(s, 512, SUBLANES)

    def kern(x_ref, g_ref, sh_ref, sc_ref, h_ref):
        _, _, n = _norm_parts(x_ref[0], g_ref[...])
        h_ref[0] = (n * (1.0 + sc_ref[0]) + sh_ref[0]).astype(BF16)

    return pl.pallas_call(
        kern, name="norm_mod_fwd", grid=(bl, s // ts),
        in_specs=_act_specs(ts, d, 1) + [_par_spec(d), _vec_spec(d), _vec_spec(d)],
        out_specs=_act_specs(ts, d, 1)[0],
        out_shape=jax.ShapeDtypeStruct((bl, s, d), BF16),
        compiler_params=_cp(2),
    )(x3, g, sh, sc)


def _resid_norm_fwd(x3, y3, gate, g, sh, sc):
    bl, s, d = x3.shape
    ts = _tile(s, 512, SUBLANES)

    def kern(x_ref, y_ref, gate_ref, g_ref, sh_ref, sc_ref, x1_ref, h_ref):
        x1 = x_ref[0] + gate_ref[0] * y_ref[0]
        x1_ref[0] = x1
        _, _, n = _norm_parts(x1, g_ref[...])
        h_ref[0] = (n * (1.0 + sc_ref[0]) + sh_ref[0]).astype(BF16)

    return pl.pallas_call(
        kern, name="resid_norm_fwd", grid=(bl, s // ts),
        in_specs=_act_specs(ts, d, 2) + [_vec_spec(d), _par_spec(d), _vec_spec(d), _vec_spec(d)],
        out_specs=_act_specs(ts, d, 2),
        out_shape=[jax.ShapeDtypeStruct((bl, s, d), F32), jax.ShapeDtypeStruct((bl, s, d), BF16)],
        compiler_params=_cp(2),
    )(x3, y3, gate, g, sh, sc)


def _norm_mod_bwd(dh3, x3, dres3, g, sc, name):
    bl, s, d = x3.shape
    ts = _tile(s, 512, SUBLANES)

    def kern(dh_ref, x_ref, dres_ref, g_ref, sc_ref, dx_ref, dsh_ref, dsc_ref, dg_ref):
        b, t = pl.program_id(0), pl.program_id(1)
        gv = g_ref[...]
        r, xh, _ = _norm_parts(x_ref[0], gv)
        dx, a, bb, cc = _norm_bwd_parts(dh_ref[0], xh, r, gv, sc_ref[0])
        dx_ref[0] = dres_ref[0] + dx

        @pl.when(t == 0)
        def _():
            dsh_ref[...] = jnp.zeros_like(dsh_ref)
            dsc_ref[...] = jnp.zeros_like(dsc_ref)

        @pl.when((t == 0) & (b == 0))
        def _():
            dg_ref[...] = jnp.zeros_like(dg_ref)

        dsh_ref[0] += jnp.sum(a, axis=0, keepdims=True)
        dsc_ref[0] += jnp.sum(bb, axis=0, keepdims=True)
        dg_ref[...] += jnp.sum(cc, axis=0, keepdims=True)

    return pl.pallas_call(
        kern, name=name, grid=(bl, s // ts),
        in_specs=_act_specs(ts, d, 3) + [_par_spec(d), _vec_spec(d)],
        out_specs=[_act_specs(ts, d, 1)[0], _vec_spec(d), _vec_spec(d), _par_spec(d)],
        out_shape=[jax.ShapeDtypeStruct((bl, s, d), F32), jax.ShapeDtypeStruct((bl, 1, d), F32),
                   jax.ShapeDtypeStruct((bl, 1, d), F32), jax.ShapeDtypeStruct((1, d), F32)],
        compiler_params=_cp(2),
    )(dh3, x3, dres3, g, sc)


def _gate_bwd(dx3, y3, gate, name):
    bl, s, d = dx3.shape
    ts = _tile(s, 512, SUBLANES)

    def kern(dx_ref, y_ref, gate_ref, dy_ref, dgate_ref):
        t = pl.program_id(1)
        dx = dx_ref[0]
        dy_ref[0] = (gate_ref[0] * dx).astype(BF16)

        @pl.when(t == 0)
        def _():
            dgate_ref[...] = jnp.zeros_like(dgate_ref)

        dgate_ref[0] += jnp.sum(dx * y_ref[0], axis=0, keepdims=True)

    return pl.pallas_call(
        kern, name=name, grid=(bl, s // ts),
        in_specs=_act_specs(ts, d, 2) + [_vec_spec(d)],
        out_specs=[_act_specs(ts, d, 1)[0], _vec_spec(d)],
        out_shape=[jax.ShapeDtypeStruct((bl, s, d), BF16), jax.ShapeDtypeStruct((bl, 1, d), F32)],
        compiler_params=_cp(2),
    )(dx3, y3, gate)


def _final_fwd_bwd(x1, yf, gate2, g, shf, scf, tgt):
    bl, s, d = x1.shape
    ts = _tile(s, 512, SUBLANES)

    def kern(x1_ref, yf_ref, gate_ref, g_ref, sh_ref, sc_ref, tgt_ref, dx_ref, dsh_ref, dsc_ref, dg_ref, loss_ref):
        b, t = pl.program_id(0), pl.program_id(1)
        gv, sc = g_ref[...], sc_ref[0]
        x2 = x1_ref[0] + gate_ref[0] * yf_ref[0]
        r, xh, n = _norm_parts(x2, gv)
        err = n * (1.0 + sc) + sh_ref[0] - tgt_ref[0]
        dx, a, bb, cc = _norm_bwd_parts(err * (1.0 / d), xh, r, gv, sc)
        dx_ref[0] = dx

        @pl.when(t == 0)
        def _():
            dsh_ref[...] = jnp.zeros_like(dsh_ref)
            dsc_ref[...] = jnp.zeros_like(dsc_ref)

        @pl.when((t == 0) & (b == 0))
        def _():
            dg_ref[...] = jnp.zeros_like(dg_ref)
            loss_ref[...] = jnp.zeros_like(loss_ref)

        dsh_ref[0] += jnp.sum(a, axis=0, keepdims=True)
        dsc_ref[0] += jnp.sum(bb, axis=0, keepdims=True)
        dg_ref[...] += jnp.sum(cc, axis=0, keepdims=True)
        tok = jnp.mean(err * err, axis=-1, keepdims=True)
        loss_ref[...] += 0.5 * jnp.sum(tok, axis=0, keepdims=True)

    return pl.pallas_call(
        kern, name="final_fwd_bwd", grid=(bl, s // ts),
        in_specs=_act_specs(ts, d, 2) + [_vec_spec(d), _par_spec(d), _vec_spec(d), _vec_spec(d)] + _act_specs(ts, d, 1),
        out_specs=[_act_specs(ts, d, 1)[0], _vec_spec(d), _vec_spec(d), _par_spec(d),
                   pl.BlockSpec((1, 1), lambda b, t: (0, 0))],
        out_shape=[jax.ShapeDtypeStruct((bl, s, d), F32), jax.ShapeDtypeStruct((bl, 1, d), F32),
                   jax.ShapeDtypeStruct((bl, 1, d), F32), jax.ShapeDtypeStruct((1, d), F32),
                   jax.ShapeDtypeStruct((1, 1), F32)],
        compiler_params=_cp(2),
    )(x1, yf, gate2, g, shf, scf, tgt)


def _rnn_gates(xr, cw, cb, wa, ba, wi, bi, lam):
    kw = cw.shape[0]
    xc = cb
    for k in range(kw):
        xc = xc + _shift_down(xr, kw - 1 - k) * cw[k:k + 1, :]
    r = _sigmoid(_dot(xc, wa, _NN) + ba)
    i = _sigmoid(_dot(xc, wi, _NN) + bi)
    sp = _softplus(-lam)
    log_a = -LRU_C * r * sp
    a = jnp.exp(log_a)
    mult = jnp.sqrt(-_expm1(2.0 * log_a, a * a))
    return xc, r, i, sp, a, mult


def _segment_scan(a_s, u_s, h_s, p_s, reverse):
    s, c = a_s.shape
    seg = s // SCAN_SEGMENTS

    def step(n, carry):
        t = (seg - 1 - n) if reverse else n
        h, p = carry
        av = a_s[pl.ds(t, SCAN_SEGMENTS, stride=seg), :]
        uv = u_s[pl.ds(t, SCAN_SEGMENTS, stride=seg), :]
        h = av * h + uv
        p = p * av
        h_s[pl.ds(t, SCAN_SEGMENTS, stride=seg), :] = h
        p_s[pl.ds(t, SCAN_SEGMENTS, stride=seg), :] = p
        return h, p

    lax.fori_loop(0, seg, step, (jnp.zeros((SCAN_SEGMENTS, c), F32), jnp.ones((SCAN_SEGMENTS, c), F32)), unroll=8)
    carry = jnp.zeros((1, c), F32)
    order = range(SCAN_SEGMENTS - 1, -1, -1) if reverse else range(SCAN_SEGMENTS)
    for j in order:
        rows = pl.ds(j * seg, seg)
        fixed = h_s[rows, :] + p_s[rows, :] * carry
        h_s[rows, :] = fixed
        carry = fixed[0:1, :] if reverse else fixed[seg - 1:seg, :]


def _rnn_specs(s, rb, nb):
    act = lambda off: pl.BlockSpec((1, s, rb), lambda b, n, off=off: (b, 0, off + n))
    par = pl.BlockSpec((1, rb), lambda b, n: (0, n))
    wsp = pl.BlockSpec((1, rb, rb), lambda b, n: (n, 0, 0))
    return act, par, wsp


def _rnn_fwd(zr3, cw, cb, wa, ba, wi, bi, lam):
    bl, s, two = zr3.shape
    nb, rb, _ = wa.shape
    dr = nb * rb
    kw = cw.shape[0]
    act, par, wsp = _rnn_specs(s, rb, nb)

    def kern(xr_ref, gr_ref, cw_ref, cb_ref, wa_ref, ba_ref, wi_ref, bi_ref, lam_ref, h_ref, y_ref, a_s, u_s, h_s, p_s):
        xc, r, i, sp, a, mult = _rnn_gates(xr_ref[0], cw_ref[...], cb_ref[...], wa_ref[0], ba_ref[...],
                                           wi_ref[0], bi_ref[...], lam_ref[...])
        a_s[...] = a
        u_s[...] = mult * (i * xc)
        _segment_scan(a_s, u_s, h_s, p_s, reverse=False)
        h = h_s[...]
        h_ref[0] = h
        y_ref[0] = (_gelu(gr_ref[0]) * h).astype(BF16)

    return pl.pallas_call(
        kern, name="rnn_fwd", grid=(bl, nb),
        in_specs=[act(0), act(nb), pl.BlockSpec((kw, rb), lambda b, n: (0, n)), par, wsp, par, wsp, par, par],
        out_specs=[act(0), act(0)],
        out_shape=[jax.ShapeDtypeStruct((bl, s, dr), F32), jax.ShapeDtypeStruct((bl, s, dr), BF16)],
        scratch_shapes=[pltpu.VMEM((s, rb), F32)] * 4,
        compiler_params=_cp(2),
    )(zr3, zr3, cw, cb, wa, ba, wi, bi, lam)


def _rnn_bwd(zr3, h3, dy3, cw, cb, wa, ba, wi, bi, lam):
    bl, s, _ = zr3.shape
    nb, rb, _ = wa.shape
    dr = nb * rb
    kw = cw.shape[0]
    act = lambda off: pl.BlockSpec((1, s, rb), lambda n, b, off=off: (b, 0, off + n))
    par = pl.BlockSpec((1, rb), lambda n, b: (0, n))
    wsp = pl.BlockSpec((1, rb, rb), lambda n, b: (n, 0, 0))
    cws = pl.BlockSpec((kw, rb), lambda n, b: (0, n))

    def kern(xr_ref, gr_ref, h_ref, dy_ref, cw_ref, cb_ref, wa_ref, ba_ref, wi_ref, bi_ref, lam_ref,
             dxr_ref, dgr_ref, dcw_ref, dcb_ref, dwa_ref, dba_ref, dwi_ref, dbi_ref, dlam_ref, a_s, u_s, h_s, p_s):
        b = pl.program_id(1)
        xr, cwv, lamv = xr_ref[0], cw_ref[...], lam_ref[...]
        wav, wiv = wa_ref[0], wi_ref[0]
        xc, r, i, sp, a, mult = _rnn_gates(xr, cwv, cb_ref[...], wav, ba_ref[...], wiv, bi_ref[...], lamv)
        h, dy = h_ref[0], dy_ref[0]
        ge, dge = _gelu_and_grad(gr_ref[0])
        dgr_ref[0] = (dy * h * dge).astype(BF16)
        a_s[...] = _shift_up(a, 1)
        u_s[...] = dy * ge
        _segment_scan(a_s, u_s, h_s, p_s, reverse=True)
        g = h_s[...]
        da = g * _shift_down(h, 1)
        ix = i * xc
        dlog_a = da * a + (g * ix) * (-(a * a) / mult)
        di = g * mult * xc
        dpa = (dlog_a * (-LRU_C * sp)) * r * (1.0 - r)
        dpi = di * i * (1.0 - i)
        dxc = g * mult * i + _dot(dpa, wav, _NT) + _dot(dpi, wiv, _NT)
        dxr = jnp.zeros_like(dxc)
        dcw_rows = []
        for k in range(kw):
            dxr = dxr + _shift_up(dxc, kw - 1 - k) * cwv[k:k + 1, :]
            dcw_rows.append(jnp.sum(dxc * _shift_down(xr, kw - 1 - k), axis=0, keepdims=True))
        dxr_ref[0] = dxr.astype(BF16)

        @pl.when(b == 0)
        def _():
            for ref in (dcw_ref, dcb_ref, dwa_ref, dba_ref, dwi_ref, dbi_ref, dlam_ref):
                ref[...] = jnp.zeros_like(ref)

        for k in range(kw):
            dcw_ref[k:k + 1, :] += dcw_rows[k]
        dcb_ref[...] += jnp.sum(dxc, axis=0, keepdims=True)
        dwa_ref[0] += _dot(xc, dpa, _TN)
        dwi_ref[0] += _dot(xc, dpi, _TN)
        dba_ref[...] += jnp.sum(dpa, axis=0, keepdims=True)
        dbi_ref[...] += jnp.sum(dpi, axis=0, keepdims=True)
        dsp = jnp.sum(dlog_a * (-LRU_C * r), axis=0, keepdims=True)
        dlam_ref[...] += dsp * (-_sigmoid(-lamv))

    vec = jax.ShapeDtypeStruct((1, dr), F32)
    wsh = jax.ShapeDtypeStruct((nb, rb, rb), F32)
    return pl.pallas_call(
        kern, name="rnn_bwd", grid=(nb, bl),
        in_specs=[act(0), act(nb), act(0), act(0), cws, par, wsp, par, wsp, par, par],
        out_specs=[act(0), act(0), cws, par, wsp, par, wsp, par, par],
        out_shape=[jax.ShapeDtypeStruct((bl, s, dr), BF16), jax.ShapeDtypeStruct((bl, s, dr), BF16),
                   jax.ShapeDtypeStruct((kw, dr), F32), vec, wsh, vec, wsh, vec, vec],
        scratch_shapes=[pltpu.VMEM((s, rb), F32)] * 4,
        compiler_params=_cp(2),
    )(zr3, zr3, h3, dy3, cw, cb, wa, ba, wi, bi, lam)


def _tri(n, upper):
    r = lax.broadcasted_iota(jnp.int32, (n, n), 0)
    c = lax.broadcasted_iota(jnp.int32, (n, n), 1)
    return jnp.where((c >= r) if upper else (c <= r), 1.0, 0.0).astype(F32)


def _fgate_fwd(zf3, bf):
    bl, s, w = zf3.shape
    ch = _tile(s, 256, SUBLANES)

    def kern(z_ref, b_ref, f_ref):
        tri = _tri(ch, upper=False)
        carry = jnp.zeros((1, w), F32)
        for j in range(s // ch):
            rows = pl.ds(j * ch, ch)
            lf = -_softplus(-(z_ref[0, rows, :] + b_ref[...]))
            out = jnp.dot(tri, lf, precision=lax.Precision.HIGHEST, preferred_element_type=F32) + carry
            f_ref[0, rows, :] = out
            carry = out[ch - 1:ch, :]

    return pl.pallas_call(
        kern, name="fgate_fwd", grid=(bl,),
        in_specs=[pl.BlockSpec((1, s, w), lambda b: (b, 0, 0)), pl.BlockSpec((1, w), lambda b: (0, 0))],
        out_specs=pl.BlockSpec((1, s, w), lambda b: (b, 0, 0)),
        out_shape=jax.ShapeDtypeStruct((bl, s, w), F32),
        compiler_params=_cp(1),
    )(zf3, bf)


def _fgate_bwd(dfk3, dfq3, zf3, bf):
    bl, s, w = zf3.shape
    ch = _tile(s, 256, SUBLANES)

    def kern(dfk_ref, dfq_ref, z_ref, b_ref, dz_ref, db_ref):
        b = pl.program_id(0)
        tri = _tri(ch, upper=True)
        carry = jnp.zeros((1, w), F32)
        dbsum = jnp.zeros((1, w), F32)
        for j in range(s // ch - 1, -1, -1):
            rows = pl.ds(j * ch, ch)
            df = dfk_ref[0, rows, :] + dfq_ref[0, rows, :]
            dlf = jnp.dot(tri, df, precision=lax.Precision.HIGHEST, preferred_element_type=F32) + carry
            carry = dlf[0:1, :]
            dz = dlf * _sigmoid(-(z_ref[0, rows, :] + b_ref[...]))
            dz_ref[0, rows, :] = dz.astype(BF16)
            dbsum = dbsum + jnp.sum(dz, axis=0, keepdims=True)

        @pl.when(b == 0)
        def _():
            db_ref[...] = jnp.zeros_like(db_ref)

        db_ref[...] += dbsum

    return pl.pallas_call(
        kern, name="fgate_bwd", grid=(bl,),
        in_specs=[pl.BlockSpec((1, s, w), lambda b: (b, 0, 0))] * 3 + [pl.BlockSpec((1, w), lambda b: (0, 0))],
        out_specs=[pl.BlockSpec((1, s, w), lambda b: (b, 0, 0)), pl.BlockSpec((1, w), lambda b: (0, 0))],
        out_shape=[jax.ShapeDtypeStruct((bl, s, w), BF16), jax.ShapeDtypeStruct((1, w), F32)],
        compiler_params=_cp(1),
    )(dfk3, dfq3, zf3, bf)


def _lanes(col, width):
    return col if width == LANES else jnp.concatenate([col] * (width // LANES), axis=1)


def _causal(sc, row0, col0, transposed):
    r = lax.broadcasted_iota(jnp.int32, sc.shape, 0) + row0
    c = lax.broadcasted_iota(jnp.int32, sc.shape, 1) + col0
    return jnp.where((c >= r) if transposed else (r >= c), sc, NEG_BIG)


def _attn_fwd(q3, kv3, fcol, frow, nh):
    bl, s, da = q3.shape
    dh = da // nh
    tq = _tile(s, 512, LANES)
    nq = s // tq

    def kern(q_ref, k_ref, v_ref, fq_ref, fk_ref, o_ref, lse_ref, m_s, l_s, acc):
        iq, ik = pl.program_id(2), pl.program_id(3)

        @pl.when(ik == 0)
        def _():
            m_s[...] = jnp.full_like(m_s, NEG_BIG)
            l_s[...] = jnp.zeros_like(l_s)
            acc[...] = jnp.zeros_like(acc)

        def block(masked):
            sc = _dot(q_ref[0], k_ref[0], _NT) + _lanes(fq_ref[0], tq) - fk_ref[0]
            if masked:
                sc = _causal(sc, iq * tq, ik * tq, False)
            m_old = m_s[...]
            m_new = jnp.maximum(m_old, jnp.max(sc, axis=-1, keepdims=True))
            alpha = jnp.exp(m_old - m_new)
            p = jnp.exp(sc - m_new)
            l_s[...] = alpha * l_s[...] + jnp.sum(p, axis=-1, keepdims=True)
            acc[...] = alpha * acc[...] + _dot(p, v_ref[0], _NN)
            m_s[...] = m_new

        pl.when(ik < iq)(functools.partial(block, False))
        pl.when(ik == iq)(functools.partial(block, True))

        @pl.when(ik == nq - 1)
        def _():
            l = l_s[...]
            o_ref[0] = acc[...] / l
            lse_ref[0] = jnp.broadcast_to(m_s[...] + jnp.log(l), (tq, LANES))

    kmap = lambda off: (lambda b, h, iq, ik: (b, jnp.minimum(ik, iq), off + h))
    return pl.pallas_call(
        kern, name="attn_fwd", grid=(bl, nh, nq, nq),
        in_specs=[pl.BlockSpec((1, tq, dh), lambda b, h, iq, ik: (b, iq, h)),
                  pl.BlockSpec((1, tq, dh), kmap(0)), pl.BlockSpec((1, tq, dh), kmap(nh)),
                  pl.BlockSpec((1, tq, LANES), lambda b, h, iq, ik: (b * nh + h, iq, 0)),
                  pl.BlockSpec((1, 1, tq), lambda b, h, iq, ik: (b * nh + h, 0, jnp.minimum(ik, iq)))],
        out_specs=[pl.BlockSpec((1, tq, dh), lambda b, h, iq, ik: (b, iq, h)),
                   pl.BlockSpec((1, tq, LANES), lambda b, h, iq, ik: (b * nh + h, iq, 0))],
        out_shape=[jax.ShapeDtypeStruct((bl, s, da), F32), jax.ShapeDtypeStruct((bl * nh, s, LANES), F32)],
        scratch_shapes=[pltpu.VMEM((tq, 1), F32), pltpu.VMEM((tq, 1), F32), pltpu.VMEM((tq, dh), F32)],
        compiler_params=_cp(4),
    )(q3, kv3, kv3, fcol, frow)


def _attn_bwd_dq(q3, kv3, do3, o3, lse, fcol, frow, nh, scale):
    bl, s, da = q3.shape
    dh = da // nh
    tq = _tile(s, 512, LANES)
    nq = s // tq

    def kern(q_ref, k_ref, v_ref, do_ref, o_ref, lse_ref, fq_ref, fk_ref, dq_ref, delta_ref, dfq_ref, acc, d_s, r_s):
        iq, ik = pl.program_id(2), pl.program_id(3)

        @pl.when(ik == 0)
        def _():
            acc[...] = jnp.zeros_like(acc)
            r_s[...] = jnp.zeros_like(r_s)
            dl = jnp.sum(do_ref[0] * o_ref[0], axis=-1, keepdims=True)
            d_s[...] = dl
            delta_ref[0] = jnp.broadcast_to(dl, (tq, LANES))

        def block(masked):
            sc = _dot(q_ref[0], k_ref[0], _NT) + _lanes(fq_ref[0], tq) - fk_ref[0]
            if masked:
                sc = _causal(sc, iq * tq, ik * tq, False)
            p = jnp.exp(sc - _lanes(lse_ref[0], tq))
            dp = _dot(do_ref[0], v_ref[0], _NT)
            ds = p * (dp - d_s[...])
            acc[...] += _dot(ds, k_ref[0], _NN)
            r_s[...] += jnp.sum(ds, axis=-1, keepdims=True)

        pl.when(ik < iq)(functools.partial(block, False))
        pl.when(ik == iq)(functools.partial(block, True))

        @pl.when(ik == nq - 1)
        def _():
            dq_ref[0] = (acc[...] * scale).astype(BF16)
            dfq_ref[0] = jnp.broadcast_to(r_s[...], (tq, LANES))

    qmap = lambda b, h, iq, ik: (b, iq, h)
    bmap = lambda b, h, iq, ik: (b * nh + h, iq, 0)
    kmap = lambda off: (lambda b, h, iq, ik: (b, jnp.minimum(ik, iq), off + h))
    col = jax.ShapeDtypeStruct((bl * nh, s, LANES), F32)
    return pl.pallas_call(
        kern, name="attn_bwd_dq", grid=(bl, nh, nq, nq),
        in_specs=[pl.BlockSpec((1, tq, dh), qmap), pl.BlockSpec((1, tq, dh), kmap(0)), pl.BlockSpec((1, tq, dh), kmap(nh)),
                  pl.BlockSpec((1, tq, dh), qmap), pl.BlockSpec((1, tq, dh), qmap),
                  pl.BlockSpec((1, tq, LANES), bmap), pl.BlockSpec((1, tq, LANES), bmap),
                  pl.BlockSpec((1, 1, tq), lambda b, h, iq, ik: (b * nh + h, 0, jnp.minimum(ik, iq)))],
        out_specs=[pl.BlockSpec((1, tq, dh), qmap), pl.BlockSpec((1, tq, LANES), bmap), pl.BlockSpec((1, tq, LANES), bmap)],
        out_shape=[jax.ShapeDtypeStruct((bl, s, da), BF16), col, col],
        scratch_shapes=[pltpu.VMEM((tq, dh), F32), pltpu.VMEM((tq, 1), F32), pltpu.VMEM((tq, 1), F32)],
        compiler_params=_cp(4),
    )(q3, kv3, kv3, do3, o3, lse, fcol, frow)


def _attn_bwd_dkv(q3, kv3, do3, lse_row, delta_row, fcol, frow, nh):
    bl, s, da = q3.shape
    dh = da // nh
    tk = _tile(s, 512, LANES)
    nk = s // tk

    def kern(q_ref, k_ref, v_ref, do_ref, lse_ref, dl_ref, fk_ref, fq_ref, dk_ref, dv_ref, df_ref, dk_acc, dv_acc, df_acc):
        ik, iq = pl.program_id(2), pl.program_id(3)

        @pl.when(iq == 0)
        def _():
            dk_acc[...] = jnp.zeros_like(dk_acc)
            dv_acc[...] = jnp.zeros_like(dv_acc)
            df_acc[...] = jnp.zeros_like(df_acc)

        def block(masked):
            st = _dot(k_ref[0], q_ref[0], _NT) - _lanes(fk_ref[0], tk) + fq_ref[0]
            if masked:
                st = _causal(st, ik * tk, iq * tk, True)
            pt = jnp.exp(st - lse_ref[0])
            dv_acc[...] += _dot(pt, do_ref[0], _NN)
            dpt = _dot(v_ref[0], do_ref[0], _NT)
            dst = pt * (dpt - dl_ref[0])
            dk_acc[...] += _dot(dst, q_ref[0], _NN)
            df_acc[...] += jnp.sum(dst, axis=-1, keepdims=True)

        pl.when(iq > ik)(functools.partial(block, False))
        pl.when(iq == ik)(functools.partial(block, True))

        @pl.when(iq == nk - 1)
        def _():
            dk_ref[0] = dk_acc[...].astype(BF16)
            dv_ref[0] = dv_acc[...].astype(BF16)
            df_ref[0] = jnp.broadcast_to(-df_acc[...], (tk, LANES))

    qmap = lambda b, h, ik, iq: (b, jnp.maximum(iq, ik), h)
    rmap = lambda b, h, ik, iq: (b * nh + h, 0, jnp.maximum(iq, ik))
    kmap = lambda off: (lambda b, h, ik, iq: (b, ik, off + h))
    bmap = lambda b, h, ik, iq: (b * nh + h, ik, 0)
    return pl.pallas_call(
        kern, name="attn_bwd_dkv", grid=(bl, nh, nk, nk),
        in_specs=[pl.BlockSpec((1, tk, dh), qmap), pl.BlockSpec((1, tk, dh), kmap(0)), pl.BlockSpec((1, tk, dh), kmap(nh)),
                  pl.BlockSpec((1, tk, dh), qmap), pl.BlockSpec((1, 1, tk), rmap), pl.BlockSpec((1, 1, tk), rmap),
                  pl.BlockSpec((1, tk, LANES), bmap), pl.BlockSpec((1, 1, tk), rmap)],
        out_specs=[pl.BlockSpec((1, tk, dh), kmap(0)), pl.BlockSpec((1, tk, dh), kmap(0)), pl.BlockSpec((1, tk, LANES), bmap)],
        out_shape=[jax.ShapeDtypeStruct((bl, s, da), BF16), jax.ShapeDtypeStruct((bl, s, da), BF16),
                   jax.ShapeDtypeStruct((bl * nh, s, LANES), F32)],
        scratch_shapes=[pltpu.VMEM((tk, dh), F32), pltpu.VMEM((tk, dh), F32), pltpu.VMEM((tk, 1), F32)],
        compiler_params=_cp(4),
    )(q3, kv3, kv3, do3, lse_row, delta_row, fcol, frow)


def _merge_fwd(mg3, pr3, pa3):
    bl, s, d = pr3.shape
    ts = _tile(s, 256, SUBLANES)
    half = lambda j: pl.BlockSpec((1, ts, d), lambda b, t, j=j: (b, t, j))

    def kern(mr_ref, ma_ref, pr_ref, pa_ref, o_ref):
        o_ref[0] = (_sigmoid(mr_ref[0]) * pr_ref[0] + _sigmoid(ma_ref[0]) * pa_ref[0]).astype(BF16)

    return pl.pallas_call(
        kern, name="merge_fwd", grid=(bl, s // ts),
        in_specs=[half(0), half(1)] + _act_specs(ts, d, 2), out_specs=_act_specs(ts, d, 1)[0],
        out_shape=jax.ShapeDtypeStruct((bl, s, d), BF16), compiler_params=_cp(2),
    )(mg3, mg3, pr3, pa3)


def _merge_bwd(dm3, mg3, pr3, pa3):
    bl, s, d = pr3.shape
    ts = _tile(s, 256, SUBLANES)
    half = lambda j: pl.BlockSpec((1, ts, d), lambda b, t, j=j: (b, t, j))

    def kern(dm_ref, mr_ref, ma_ref, pr_ref, pa_ref, dpr_ref, dpa_ref, dmr_ref, dma_ref):
        dm = dm_ref[0]
        gr, ga = _sigmoid(mr_ref[0]), _sigmoid(ma_ref[0])
        dpr_ref[0] = (gr * dm).astype(BF16)
        dpa_ref[0] = (ga * dm).astype(BF16)
        dmr_ref[0] = (dm * pr_ref[0] * gr * (1.0 - gr)).astype(BF16)
        dma_ref[0] = (dm * pa_ref[0] * ga * (1.0 - ga)).astype(BF16)

    return pl.pallas_call(
        kern, name="merge_bwd", grid=(bl, s // ts),
        in_specs=_act_specs(ts, d, 1) + [half(0), half(1)] + _act_specs(ts, d, 2), out_specs=_act_specs(ts, d, 4),
        out_shape=[jax.ShapeDtypeStruct((bl, s, d), BF16)] * 4, compiler_params=_cp(2),
    )(dm3, mg3, mg3, pr3, pa3)


def _ffn_conv(gf, cw, cb):
    kw = cw.shape[0]
    y = cb
    for k in range(kw):
        y = y + _shift_down(gf, kw - 1 - k) * cw[k:k + 1, :]
    return y


def _ffn_act_fwd(up3, cw, cb):
    bl, s, two = up3.shape
    dff = two // 2
    kw = cw.shape[0]
    tc = _tile(dff, 256, LANES)
    nc = dff // tc

    def kern(gf_ref, uf_ref, cw_ref, cb_ref, o_ref):
        o_ref[0] = (_gelu(_ffn_conv(gf_ref[0], cw_ref[...], cb_ref[...])) * uf_ref[0]).astype(BF16)

    act = lambda off: pl.BlockSpec((1, s, tc), lambda b, j, off=off: (b, 0, off + j))
    return pl.pallas_call(
        kern, name="ffn_act_fwd", grid=(bl, nc),
        in_specs=[act(0), act(nc), pl.BlockSpec((kw, tc), lambda b, j: (0, j)), pl.BlockSpec((1, tc), lambda b, j: (0, j))],
        out_specs=act(0), out_shape=jax.ShapeDtypeStruct((bl, s, dff), BF16), compiler_params=_cp(2),
    )(up3, up3, cw, cb)


def _ffn_act_bwd(up3, dact3, cw, cb):
    bl, s, two = up3.shape
    dff = two // 2
    kw = cw.shape[0]
    tc = _tile(dff, 256, LANES)
    nc = dff // tc

    def kern(gf_ref, uf_ref, da_ref, cw_ref, cb_ref, dgf_ref, duf_ref, dcw_ref, dcb_ref):
        b = pl.program_id(1)
        gf, cwv, da = gf_ref[0], cw_ref[...], da_ref[0]
        ge, dge = _gelu_and_grad(_ffn_conv(gf, cwv, cb_ref[...]))
        duf_ref[0] = (da * ge).astype(BF16)
        dgc = da * uf_ref[0] * dge
        dgf = jnp.zeros_like(dgc)
        rows = []
        for k in range(kw):
            dgf = dgf + _shift_up(dgc, kw - 1 - k) * cwv[k:k + 1, :]
            rows.append(jnp.sum(dgc * _shift_down(gf, kw - 1 - k), axis=0, keepdims=True))
        dgf_ref[0] = dgf.astype(BF16)

        @pl.when(b == 0)
        def _():
            dcw_ref[...] = jnp.zeros_like(dcw_ref)
            dcb_ref[...] = jnp.zeros_like(dcb_ref)

        for k in range(kw):
            dcw_ref[k:k + 1, :] += rows[k]
        dcb_ref[...] += jnp.sum(dgc, axis=0, keepdims=True)

    act = lambda off: pl.BlockSpec((1, s, tc), lambda j, b, off=off: (b, 0, off + j))
    cws = pl.BlockSpec((kw, tc), lambda j, b: (0, j))
    cbs = pl.BlockSpec((1, tc), lambda j, b: (0, j))
    return pl.pallas_call(
        kern, name="ffn_act_bwd", grid=(nc, bl),
        in_specs=[act(0), act(nc), act(0), cws, cbs], out_specs=[act(0), act(0), cws, cbs],
        out_shape=[jax.ShapeDtypeStruct((bl, s, dff), BF16), jax.ShapeDtypeStruct((bl, s, dff), BF16),
                   jax.ShapeDtypeStruct((kw, dff), F32), jax.ShapeDtypeStruct((1, dff), F32)],
        compiler_params=_cp(2),
    )(up3, up3, dact3, cw, cb)


_HBM = pl.BlockSpec(memory_space=pltpu.HBM)


def _place():
    x, y, c = lax.axis_index("x"), lax.axis_index("y"), lax.axis_index("c")
    chips = dict(me=2 * x + y, nx=2 * (1 - x) + y, ny=2 * x + (1 - y), diag=2 * (1 - x) + (1 - y))
    peers = dict(nx=(1 - x, y, c), ny=(x, 1 - y, c), sib=(x, y, 1 - c))
    return c, chips, peers


def _remote(src, dst, sems, k, to):
    return pltpu.make_async_remote_copy(src_ref=src, dst_ref=dst, send_sem=sems[0].at[k], recv_sem=sems[1].at[k],
                                        device_id=to, device_id_type=MESH)


RS_STEPS = 2


def _piece(q, idx, n=1):
    start = idx * q
    if not isinstance(start, int):
        start = pl.multiple_of(start, SUBLANES)
    return pl.ds(start, n * q)


def _all_gather_chips(xs, name):
    nt = len(xs)
    per = 9

    def body(*refs):
        x_refs, o_refs = refs[:nt], refs[nt:2 * nt]
        send_sems, recv_sems = refs[2 * nt:]
        c, chip, peer = _place()
        sems = (send_sems, recv_sems)
        me, nx, ny, dg = chip["me"], chip["nx"], chip["ny"], chip["diag"]
        sends = []

        def arrive(k, dst):
            _remote(dst, dst, sems, k, peer["sib"]).wait_recv()

        def pass_on(k, blk, to):
            cp = _remote(blk, blk, sems, k, peer[to])
            cp.start()
            sends.append(cp)

        for t in range(nt):
            q = xs[t].shape[0] // 4
            half = _piece(q, 2 * c, 2)
            for k, to in ((0, "nx"), (1, "ny")):
                cp = _remote(x_refs[t].at[half], o_refs[t].at[me, half], sems, per * t + k, peer[to])
                cp.start()
                sends.append(cp)
            cp = _remote(x_refs[t], o_refs[t].at[me], sems, per * t + 8, peer["sib"])
            cp.start()
            sends.append(cp)
        for t in range(nt):
            q, o, k0 = xs[t].shape[0] // 4, o_refs[t], per * t
            half, sub0, sub1 = _piece(q, 2 * c, 2), _piece(q, 2 * c), _piece(q, 2 * c + 1)
            arrive(k0 + 0, o.at[nx, half])
            pass_on(k0 + 2, o.at[nx, sub0], "ny")
            pass_on(k0 + 4, o.at[nx, half], "sib")
            arrive(k0 + 1, o.at[ny, half])
            pass_on(k0 + 3, o.at[ny, sub1], "nx")
            pass_on(k0 + 5, o.at[ny, half], "sib")
            arrive(k0 + 2, o.at[dg, sub0])
            pass_on(k0 + 6, o.at[dg, sub0], "sib")
            arrive(k0 + 3, o.at[dg, sub1])
            pass_on(k0 + 7, o.at[dg, sub1], "sib")
        for t in range(nt):
            q, o, k0 = xs[t].shape[0] // 4, o_refs[t], per * t
            arrive(k0 + 4, o.at[nx, _piece(q, 2 * (1 - c), 2)])
            arrive(k0 + 5, o.at[ny, _piece(q, 2 * (1 - c), 2)])
            arrive(k0 + 6, o.at[dg, _piece(q, 2 * (1 - c))])
            arrive(k0 + 7, o.at[dg, _piece(q, 2 * (1 - c) + 1)])
            arrive(k0 + 8, o.at[me])
        for cp in sends:
            cp.wait_send()

    return pl.pallas_call(
        body, name=name, in_specs=[_HBM] * nt, out_specs=[_HBM] * nt,
        out_shape=[jax.ShapeDtypeStruct((N_CHIPS,) + x.shape, x.dtype) for x in xs],
        scratch_shapes=[pltpu.SemaphoreType.DMA((per * nt,)), pltpu.SemaphoreType.DMA((per * nt,))],
    )(*xs)


def _exchange(name, xs, out_shapes, plan):
    nt = len(xs)

    def body(*refs):
        x_refs, o_refs = refs[:nt], refs[nt:2 * nt]
        send_sems, recv_sems = refs[2 * nt:]
        c, chip, peer = _place()
        cps = []
        for t in range(nt):
            for src, dst, to in plan(c, chip, x_refs[t], o_refs[t], xs[t].shape):
                cps.append(_remote(src, dst, (send_sems, recv_sems), len(cps), peer[to]))
        for cp in cps:
            cp.start()
        for cp in cps:
            cp.wait()

    n_copies = nt * len(plan(0, dict(me=0, nx=2, ny=1, diag=3), None, None, xs[0].shape, count_only=True))
    return pl.pallas_call(
        body, name=name, in_specs=[_HBM] * nt, out_specs=[_HBM] * nt,
        out_shape=[jax.ShapeDtypeStruct(s, F32) for s in out_shapes],
        scratch_shapes=[pltpu.SemaphoreType.DMA((n_copies,)), pltpu.SemaphoreType.DMA((n_copies,))],
    )(*xs)


def _plan_sibling(c, chip, g, out, shape, count_only=False):
    if count_only:
        return [None] * N_CHIPS
    q = shape[1] // 4
    return [(g.at[j, _piece(q, 2 * (1 - c), 2)], out.at[j], "sib") for j in range(N_CHIPS)]


def _plan_first(c, chip, p, out, shape, count_only=False):
    if count_only:
        return [None] * 4
    q = shape[1] // 2
    return [(p.at[chip["nx"], _piece(q, 0)], out.at[0], "nx"), (p.at[chip["diag"], _piece(q, 0)], out.at[1], "nx"),
            (p.at[chip["ny"], _piece(q, 1)], out.at[2], "ny"), (p.at[chip["diag"], _piece(q, 1)], out.at[3], "ny")]


def _plan_second(c, chip, p, out, shape, count_only=False):
    if count_only:
        return [None] * 2
    return [(p.at[1], out.at[0], "ny"), (p.at[3], out.at[1], "nx")]


def _rs_last(ps):
    nt = len(ps)

    def body(*refs):
        p_refs, o_refs = refs[:nt], refs[nt:2 * nt]
        send_sems, recv_sems = refs[2 * nt:]
        c, _, peer = _place()
        sems = (send_sems, recv_sems)
        cps = []
        for t in range(nt):
            q = ps[t].shape[0] // 4
            mine = _piece(q, 2 * c, 2)
            cps.append(_remote(p_refs[t].at[mine], o_refs[t].at[mine], sems, t, peer["sib"]))
            cps[-1].start()
        for t in range(nt):
            q = ps[t].shape[0] // 4
            theirs = _piece(q, 2 * (1 - c), 2)
            cps[t].wait_send()
            _remote(p_refs[t].at[theirs], o_refs[t].at[theirs], sems, t, peer["sib"]).wait_recv()

    return pl.pallas_call(
        body, name="rs_last", in_specs=[_HBM] * nt, out_specs=[_HBM] * nt,
        out_shape=[jax.ShapeDtypeStruct(p.shape, F32) for p in ps],
        input_output_aliases={t: t for t in range(nt)},
        scratch_shapes=[pltpu.SemaphoreType.DMA((nt,)), pltpu.SemaphoreType.DMA((nt,))],
    )(*ps)


def _add_stage(name, grid, a_list, b_list, a_map, b_map, tbs, out_shapes, out_map, prefetch=None):
    nt = len(a_list)
    lead = lambda shape: (None,) * (len(shape) - 2)

    def kern(*refs):
        refs = refs[(1 if prefetch is not None else 0):]
        for t in range(nt):
            refs[2 * nt + t][...] = refs[t][...] + refs[nt + t][...]

    in_specs = [pl.BlockSpec(lead(a.shape) + (tb, a.shape[-1]), a_map) for a, tb in zip(a_list, tbs)]
    in_specs += [pl.BlockSpec(lead(b.shape) + (tb, b.shape[-1]), b_map) for b, tb in zip(b_list, tbs)]
    out_specs = [pl.BlockSpec(lead(s) + (tb, s[-1]), out_map) for s, tb in zip(out_shapes, tbs)]
    out_shape = [jax.ShapeDtypeStruct(s, F32) for s in out_shapes]
    if prefetch is None:
        return pl.pallas_call(kern, name=name, grid=grid, in_specs=in_specs, out_specs=out_specs, out_shape=out_shape,
                              compiler_params=_cp(len(grid)))(*a_list, *b_list)
    return pl.pallas_call(
        kern, name=name,
        grid_spec=pltpu.PrefetchScalarGridSpec(num_scalar_prefetch=1, grid=grid, in_specs=in_specs, out_specs=out_specs),
        out_shape=out_shape, compiler_params=_cp(len(grid)))(prefetch, *a_list, *b_list)


def _reduce_scatter_chips(gs):
    x, y, c = lax.axis_index("x"), lax.axis_index("y"), lax.axis_index("c")
    me, nx, ny = 2 * x + y, 2 * (1 - x) + y, 2 * x + (1 - y)
    st = RS_STEPS
    qs = [g.shape[1] // 4 for g in gs]
    tbs = [q // st for q in qs]
    for g, tb in zip(gs, tbs):
        assert g.shape[1] == 4 * st * tb and tb % SUBLANES == 0, g.shape
    cols = [g.shape[2] for g in gs]
    core = jnp.reshape(c, (1,)).astype(jnp.int32)

    got = _exchange("rs_sibling", gs, [(N_CHIPS, 2 * q, cc) for q, cc in zip(qs, cols)], _plan_sibling)
    p0 = _add_stage("rs_add_sibling", (N_CHIPS, 2, st), gs, got,
                    lambda j, h, s, c_ref: (j, (2 * c_ref[0] + h) * st + s, 0), lambda j, h, s, c_ref: (j, h * st + s, 0),
                    tbs, [(N_CHIPS, 2 * q, cc) for q, cc in zip(qs, cols)], lambda j, h, s, c_ref: (j, h * st + s, 0),
                    prefetch=core)
    got = _exchange("rs_first", p0, [(4, q, cc) for q, cc in zip(qs, cols)], _plan_first)
    p1 = _add_stage("rs_add_first", (4, st), p0, got,
                    lambda k, s, i_ref: (i_ref[k], (k // 2) * st + s, 0), lambda k, s, i_ref: (k, s, 0),
                    tbs, [(4, q, cc) for q, cc in zip(qs, cols)], lambda k, s, i_ref: (k, s, 0),
                    prefetch=jnp.stack([me, ny, me, nx]).astype(jnp.int32))
    got = _exchange("rs_second", p1, [(2, q, cc) for q, cc in zip(qs, cols)], _plan_second)
    p2 = _add_stage("rs_add_second", (2, st), p1, got, lambda h, s, c_ref: (2 * h, s, 0), lambda h, s, c_ref: (h, s, 0),
                    tbs, [(4 * q, cc) for q, cc in zip(qs, cols)], lambda h, s, c_ref: ((2 * c_ref[0] + h) * st + s, 0),
                    prefetch=core)
    return _rs_last(p2)


def _adamw(g, w, m, v, name):
    rows, cc = g.shape
    tr = _tile(rows, max(SUBLANES, (1 << 18) // cc), SUBLANES)
    k1 = 1.0 - ADAM_B1 ** ADAM_STEP
    k2 = 1.0 - ADAM_B2 ** ADAM_STEP

    def kern(g_ref, w_ref, m_ref, v_ref, d_ref, nm_ref, nv_ref):
        gv = g_ref[...]
        nm = ADAM_B1 * m_ref[...] + (1.0 - ADAM_B1) * gv
        nv = ADAM_B2 * v_ref[...] + (1.0 - ADAM_B2) * (gv * gv)
        nm_ref[...] = nm
        nv_ref[...] = nv
        d_ref[...] = -ADAM_LR * ((nm / k1) / (jnp.sqrt(nv / k2) + ADAM_EPS) + ADAM_WD * w_ref[...])

    spec = pl.BlockSpec((tr, cc), lambda t: (t, 0))
    return pl.pallas_call(
        kern, name=name, grid=(rows // tr,), in_specs=[spec] * 4, out_specs=[spec] * 3,
        out_shape=[jax.ShapeDtypeStruct((rows, cc), F32)] * 3, compiler_params=_cp(1),
    )(g, w, m, v)


def _flat_pad(parts, total):
    flat = jnp.concatenate([p.reshape(-1) for p in parts])
    return jnp.pad(flat, (0, total - flat.shape[0]))


def _split_flat(flat, shapes):
    out, pos = [], 0
    for shp in shapes:
        size = math.prod(shp)
        out.append(flat[pos:pos + size].reshape(shp))
        pos += size
    return out


def _cols_of_chunks(chunks, lo, hi):
    width = chunks.shape[2]
    parts = []
    for j in range(chunks.shape[0]):
        a, b = max(lo, j * width), min(hi, (j + 1) * width)
        if a < b:
            parts.append(chunks[j, :, a - j * width:b - j * width])
    return parts[0] if len(parts) == 1 else jnp.concatenate(parts, axis=1)


def _chunks_of_cols(segments, n_chunks):
    total = sum(s.shape[1] for s in segments)
    width = total // n_chunks
    chunks = []
    for j in range(n_chunks):
        lo, hi, pos, parts = j * width, (j + 1) * width, 0, []
        for s in segments:
            a, b = max(lo, pos), min(hi, pos + s.shape[1])
            if a < b:
                parts.append(s[:, a - pos:b - pos])
            pos += s.shape[1]
        chunks.append(parts[0] if len(parts) == 1 else jnp.concatenate(parts, axis=1))
    return jnp.stack(chunks)


_WEIGHTS = ['w_ada', 'b_ada', 'g_norm1', 'w_in', 'w_rnn_conv', 'b_rnn_conv', 'w_lru_a', 'b_lru_a', 'w_lru_i', 'b_lru_i',
            'lru_lambda', 'b_fgate', 'w_proj_rnn', 'w_proj_attn', 'w_out', 'g_norm2', 'w_ffn_up', 'w_ffn_conv',
            'b_ffn_conv', 'w_ffn_down', 'w_ada_final', 'b_ada_final', 'g_final']
_MATMUL = dict(w_ada=True, w_in=True, w_proj_rnn=False, w_proj_attn=False, w_out=False, w_ffn_up=True,
               w_ffn_down=False, w_ada_final=True)
_CONV = ['w_rnn_conv', 'w_ffn_conv']
_REPLICATED = [n for n in _WEIGHTS if n not in _MATMUL and n not in _CONV]


def kernel(x, c, w_ada, b_ada, g_norm1, w_in, w_rnn_conv, b_rnn_conv, w_lru_a, b_lru_a, w_lru_i, b_lru_i, lru_lambda, b_fgate, w_proj_rnn, w_proj_attn, w_out, g_norm2, w_ffn_up, w_ffn_conv, b_ffn_conv, w_ffn_down, w_ada_final, b_ada_final, g_final, loss_target, m_w_ada, m_b_ada, m_g_norm1, m_w_in, m_w_rnn_conv, m_b_rnn_conv, m_w_lru_a, m_b_lru_a, m_w_lru_i, m_b_lru_i, m_lru_lambda, m_b_fgate, m_w_proj_rnn, m_w_proj_attn, m_w_out, m_g_norm2, m_w_ffn_up, m_w_ffn_conv, m_b_ffn_conv, m_w_ffn_down, m_w_ada_final, m_b_ada_final, m_g_final, v_w_ada, v_b_ada, v_g_norm1, v_w_in, v_w_rnn_conv, v_b_rnn_conv, v_w_lru_a, v_b_lru_a, v_w_lru_i, v_b_lru_i, v_lru_lambda, v_b_fgate, v_w_proj_rnn, v_w_proj_attn, v_w_out, v_g_norm2, v_w_ffn_up, v_w_ffn_conv, v_b_ffn_conv, v_w_ffn_down, v_w_ada_final, v_b_ada_final, v_g_final):
    args = locals()
    shape_of = {n: args[n].shape for n in _WEIGHTS}

    def view(a):
        if a.ndim >= 3:
            return a[0]
        return a[None, :] if a.ndim == 1 else a

    w2 = {n: view(args[n]) for n in _WEIGHTS}
    m2 = {n: args['m_' + n].reshape(w2[n].shape) for n in _WEIGHTS}
    v2 = {n: args['v_' + n].reshape(w2[n].shape) for n in _WEIGHTS}

    bl, s, d = x.shape
    t = bl * s
    nh = b_fgate.shape[-1]
    nb, rb = w_lru_a.shape[1], w_lru_a.shape[2]
    dr = nb * rb
    da = w2['w_proj_attn'].shape[0] * N_CHIPS
    dh = da // nh
    dff = w2['w_ffn_conv'].shape[1] * N_CHIPS
    scale = dh ** -0.5
    chip = 2 * lax.axis_index("x") + lax.axis_index("y")

    names = list(_MATMUL)
    gathered = dict(zip(names, _all_gather_chips([w2[n].astype(BF16) for n in names], "ag_weights")))
    n_conv = sum(w2[n].size for n in _CONV)
    rows_conv = -(-n_conv // (FLAT_COLS * 32)) * 32
    conv_local = _flat_pad([w2[n] for n in _CONV], rows_conv * FLAT_COLS).reshape(rows_conv, FLAT_COLS)
    conv_all = _all_gather_chips([conv_local], "ag_conv")[0].reshape(N_CHIPS, -1)
    conv_full, pos = {}, 0
    for n in _CONV:
        r, n4 = w2[n].shape
        blocks = conv_all[:, pos:pos + r * n4].reshape(N_CHIPS, r, n4)
        conv_full[n] = jnp.concatenate([blocks[j] for j in range(N_CHIPS)], axis=1)
        pos += r * n4
    rowmajor = lambda n: gathered[n].reshape(-1, gathered[n].shape[2])
    w_proj_rnn_f, w_proj_attn_f, w_out_f, w_ffn_down_f = (rowmajor(n) for n in ('w_proj_rnn', 'w_proj_attn', 'w_out', 'w_ffn_down'))
    ada_chunk, adaf_chunk, up_chunk = (w2[n].shape[1] for n in ('w_ada', 'w_ada_final', 'w_ffn_up'))

    o_q, o_k, o_fl = 2 * dr, 2 * dr + da, 2 * dr + 3 * da
    o_mg = o_fl + nh
    g_in_w = gathered['w_in']
    w_rnn, w_q = _cols_of_chunks(g_in_w, 0, o_q), _cols_of_chunks(g_in_w, o_q, o_k)
    w_kv, w_mg = _cols_of_chunks(g_in_w, o_k, o_fl), _cols_of_chunks(g_in_w, o_mg, o_mg + 2 * d)
    w_fl = jnp.pad(_cols_of_chunks(g_in_w, o_fl, o_mg), ((0, 0), (0, LANES - nh)))
    w_main = jnp.concatenate([w_rnn, w_q, w_kv, w_mg], axis=1)
    bf_pad = jnp.pad(w2['b_fgate'], ((0, 0), (0, LANES - nh)))

    c_act = _silu_pad(c, 16)
    mod = _mm(c_act, gathered['w_ada'], "nn", name="ada_fwd", bias=w2['b_ada'], b_chunk=ada_chunk)[:bl]
    sh1, sc1, gt1, sh2, sc2, gt2 = [mod[:, i * d:(i + 1) * d].reshape(bl, 1, d) for i in range(6)]
    modf = _mm(c_act, gathered['w_ada_final'], "nn", name="ada_final_fwd", bias=w2['b_ada_final'], b_chunk=adaf_chunk)[:bl]
    shf, scf = modf[:, :d].reshape(bl, 1, d), modf[:, d:].reshape(bl, 1, d)

    h1 = _norm_mod_fwd(x, w2['g_norm1'], sh1, sc1)
    h1f = h1.reshape(t, d)
    zr = _mm(h1f, w_rnn, "nn", name="in_rnn").reshape(bl, s, 2 * dr)
    q3 = _mm(h1f, w_q, "nn", name="in_q", out_dtype=BF16, scale=scale).reshape(bl, s, da)
    kv3 = _mm(h1f, w_kv, "nn", name="in_kv", out_dtype=BF16).reshape(bl, s, 2 * da)
    mg3 = _mm(h1f, w_mg, "nn", name="in_mg").reshape(bl, s, 2 * d)
    zf3 = _mm(h1f, w_fl, "nn", name="in_fl").reshape(bl, s, LANES)

    lru = (conv_full['w_rnn_conv'], w2['b_rnn_conv'], w2['w_lru_a'], w2['b_lru_a'], w2['w_lru_i'], w2['b_lru_i'], w2['lru_lambda'])
    hseq, y_rnn = _rnn_fwd(zr, *lru)

    f3 = _fgate_fwd(zf3, bf_pad)
    f_heads = f3[:, :, :nh].transpose(0, 2, 1).reshape(bl * nh, s)
    fcol = jnp.broadcast_to(f_heads[:, :, None], (bl * nh, s, LANES))
    frow = f_heads.reshape(bl * nh, 1, s)
    o3, lse = _attn_fwd(q3, kv3, fcol, frow, nh)

    pr3 = _mm(y_rnn.reshape(t, dr), w_proj_rnn_f, "nn", name="proj_rnn").reshape(bl, s, d)
    pa3 = _mm(o3.reshape(t, da), w_proj_attn_f, "nn", name="proj_attn").reshape(bl, s, d)
    merged = _merge_fwd(mg3, pr3, pa3)
    mo3 = _mm(merged.reshape(t, d), w_out_f, "nn", name="mix_out").reshape(bl, s, d)
    x1, h2 = _resid_norm_fwd(x, mo3, gt1, w2['g_norm2'], sh2, sc2)
    h2f = h2.reshape(t, d)
    up3 = _mm(h2f, gathered['w_ffn_up'], "nn", name="ffn_up", b_chunk=up_chunk).reshape(bl, s, 2 * dff)
    act3 = _ffn_act_fwd(up3, conv_full['w_ffn_conv'], w2['b_ffn_conv'])
    yf3 = _mm(act3.reshape(t, dff), w_ffn_down_f, "nn", name="ffn_down").reshape(bl, s, d)

    dx2, dshf, dscf, dg_final, loss_part = _final_fwd_bwd(x1, yf3, gt2, w2['g_final'], shf, scf, loss_target)
    loss = lax.psum(loss_part[0, 0], ("x", "y", "c"))

    dyf, dgt2 = _gate_bwd(dx2, yf3, gt2, "ffn_gate_bwd")
    dyf_f = dyf.reshape(t, d)
    g_ffn_down = _mm(act3.reshape(t, dff), dyf_f, "tn", name="dw_ffn_down")
    dact3 = _mm(dyf_f, w_ffn_down_f, "nt", name="d_ffn_act").reshape(bl, s, dff)
    dgf, duf, g_ffn_conv, g_b_ffn_conv = _ffn_act_bwd(up3, dact3, conv_full['w_ffn_conv'], w2['b_ffn_conv'])
    dgf_f, duf_f = dgf.reshape(t, dff), duf.reshape(t, dff)
    g_ffn_up = jnp.concatenate([_mm(h2f, dgf_f, "tn", name="dw_ffn_up_gate", out_chunk=up_chunk),
                                _mm(h2f, duf_f, "tn", name="dw_ffn_up_value", out_chunk=up_chunk)], axis=0)
    dh2 = _mm([dgf_f, duf_f], gathered['w_ffn_up'], "nt", name="d_h2", b_chunk=up_chunk).reshape(bl, s, d)
    dx1, dsh2, dsc2, dg_norm2 = _norm_mod_bwd(dh2, x1, dx2, w2['g_norm2'], sc2, "norm2_bwd")

    dmo, dgt1 = _gate_bwd(dx1, mo3, gt1, "mix_gate_bwd")
    dmo_f = dmo.reshape(t, d)
    g_out = _mm(merged.reshape(t, d), dmo_f, "tn", name="dw_out")
    dm3 = _mm(dmo_f, w_out_f, "nt", name="d_merged").reshape(bl, s, d)
    dpr, dpa, dmr, dma = _merge_bwd(dm3, mg3, pr3, pa3)
    g_proj_rnn = _mm(y_rnn.reshape(t, dr), dpr.reshape(t, d), "tn", name="dw_proj_rnn")
    g_proj_attn = _mm(o3.reshape(t, da), dpa.reshape(t, d), "tn", name="dw_proj_attn")
    dyr3 = _mm(dpr.reshape(t, d), w_proj_rnn_f, "nt", name="d_y_rnn").reshape(bl, s, dr)
    do3 = _mm(dpa.reshape(t, d), w_proj_attn_f, "nt", name="d_y_attn").reshape(bl, s, da)

    dq3, delta, dfq = _attn_bwd_dq(q3, kv3, do3, o3, lse, fcol, frow, nh, scale)
    lse_row = lse[:, :, 0].reshape(bl * nh, 1, s)
    delta_row = delta[:, :, 0].reshape(bl * nh, 1, s)
    dk3, dv3, dfk = _attn_bwd_dkv(q3, kv3, do3, lse_row, delta_row, fcol, frow, nh)
    heads_last = lambda a: jnp.pad(a[:, :, 0].reshape(bl, nh, s).transpose(0, 2, 1), ((0, 0), (0, 0), (0, LANES - nh)))
    dzf3, g_bf = _fgate_bwd(heads_last(dfk), heads_last(dfq), zf3, bf_pad)

    dxr, dgr, g_rnn_conv, g_b_rnn_conv, g_lru_a, g_b_lru_a, g_lru_i, g_b_lru_i, g_lam = _rnn_bwd(zr, hseq, dyr3, *lru)

    dz = [a.reshape(t, -1) for a in (dxr, dgr, dq3, dk3, dv3, dmr, dma)]
    dzf_f = dzf3.reshape(t, LANES)
    seg_names = ("xr", "gr", "q", "k", "v", "mr", "ma")
    g_seg = [_mm(h1f, a, "tn", name="dw_in_" + n) for n, a in zip(seg_names, dz)]
    g_in_fl = _mm(h1f, dzf_f, "tn", name="dw_in_fl")[:, :nh]
    g_in = _chunks_of_cols(g_seg[:5] + [g_in_fl] + g_seg[5:], N_CHIPS)
    dh1 = _mm(dzf_f, w_fl, "nt", name="d_h1_fl")
    dh1 = _mm(dz, w_main, "nt", name="d_h1", add=dh1, tk=256).reshape(bl, s, d)
    grad_x, dsh1, dsc1, dg_norm1 = _norm_mod_bwd(dh1, x, dx1, w2['g_norm1'], sc1, "norm1_bwd")

    pad_rows = lambda a: jnp.pad(a.reshape(bl, -1), ((0, 16 - bl), (0, 0)))
    dmod = pad_rows(jnp.concatenate([dsh1, dsc1, dgt1, dsh2, dsc2, dgt2], axis=-1))
    dmodf = pad_rows(jnp.concatenate([dshf, dscf], axis=-1))
    g_ada = _mm(c_act, dmod, "tn", name="dw_ada", out_chunk=ada_chunk)
    g_ada_final = _mm(c_act, dmodf, "tn", name="dw_ada_final", out_chunk=adaf_chunk)
    g_b_ada = _rowsum(dmod, "db_ada")
    g_b_ada_final = _rowsum(dmodf, "db_ada_final")

    rowchunks = lambda g: g.reshape(N_CHIPS, g.shape[0] // N_CHIPS, g.shape[1])
    full = dict(w_ada=g_ada, w_in=g_in, w_proj_rnn=rowchunks(g_proj_rnn), w_proj_attn=rowchunks(g_proj_attn),
                w_out=rowchunks(g_out), w_ffn_up=g_ffn_up, w_ffn_down=rowchunks(g_ffn_down), w_ada_final=g_ada_final)
    small = dict(b_ada=g_b_ada, g_norm1=dg_norm1, w_rnn_conv=g_rnn_conv, b_rnn_conv=g_b_rnn_conv, w_lru_a=g_lru_a,
                 b_lru_a=g_b_lru_a, w_lru_i=g_lru_i, b_lru_i=g_b_lru_i, lru_lambda=g_lam, b_fgate=g_bf[:, :nh],
                 g_norm2=dg_norm2, w_ffn_conv=g_ffn_conv, b_ffn_conv=g_b_ffn_conv, b_ada_final=g_b_ada_final,
                 g_final=dg_final)

    small_names = _REPLICATED + _CONV
    n_small = sum(small[n].size for n in small_names)
    rows_q = -(-n_small // (N_CHIPS * FLAT_COLS * 32 * RS_STEPS)) * 32 * RS_STEPS
    small_flat = _flat_pad([small[n] for n in small_names], N_CHIPS * rows_q * FLAT_COLS).reshape(N_CHIPS, rows_q, FLAT_COLS)
    reduced = _reduce_scatter_chips([full[n] for n in names] + [small_flat])
    grad = dict(zip(names, reduced[:-1]))
    small_all = _all_gather_chips([reduced[-1]], "ag_small_grads")[0].reshape(-1)
    grad.update(zip(small_names, _split_flat(small_all, [small[n].shape for n in small_names])))
    for n in _CONV:
        n4 = w2[n].shape[1]
        grad[n] = lax.dynamic_slice_in_dim(grad[n], chip * n4, n4, axis=1)

    delta_w, new_m, new_v = {}, {}, {}
    for n in names:
        delta_w[n], new_m[n], new_v[n] = _adamw(grad[n], w2[n], m2[n], v2[n], "adamw_" + n)
    rows_small = -(-sum(w2[n].size for n in small_names) // (FLAT_COLS * SUBLANES)) * SUBLANES
    flat_small = lambda src: _flat_pad([src[n] for n in small_names], rows_small * FLAT_COLS).reshape(rows_small, FLAT_COLS)
    small_out = _adamw(flat_small(grad), flat_small(w2), flat_small(m2), flat_small(v2), "adamw_small")
    for dst, flat in zip((delta_w, new_m, new_v), small_out):
        dst.update(zip(small_names, _split_flat(flat.reshape(-1), [w2[n].shape for n in small_names])))

    out = [loss, grad_x]
    for src in (grad, delta_w, new_m, new_v):
        out += [src[n].reshape(shape_of[n]) for n in _WEIGHTS]
    return tuple(out)
```

```python
import functools
import math

import jax
import jax.numpy as jnp
from jax import lax
from jax.experimental import pallas as pl
from jax.experimental.pallas import tpu as pltpu

F32 = jnp.float32
BF16 = jnp.bfloat16
MESH = pl.DeviceIdType.MESH

RMS_EPS = 1e-6
LRU_C = 8.0
ADAM_LR = 0.001
ADAM_B1 = 0.9
ADAM_B2 = 0.999
ADAM_EPS = 1e-08
ADAM_WD = 0.01
ADAM_STEP = 10

LANES = 128
SUBLANES = 8
N_CHIPS = 4
FLAT_COLS = 1024
SCAN_SEGMENTS = SUBLANES
VMEM_LIMIT = 48 * 1024 * 1024
NEG_BIG = -1e30


def _cp(n_axes):
    return pltpu.CompilerParams(dimension_semantics=("arbitrary",) * n_axes, vmem_limit_bytes=VMEM_LIMIT)


def _tile(n, target, align):
    if n <= target:
        return n
    t = (target // align) * align
    while t >= align:
        if n % t == 0:
            return t
        t -= align
    return n


def _nice_rows(n, align):
    r = -(-n // align) * align
    while True:
        if r <= 640:
            return r, r
        t = _tile(r, 640, align)
        if 128 <= t <= 640:
            return r, t
        r += align


def _sigmoid(x):
    return jax.nn.sigmoid(x)


def _softplus(x):
    return jnp.maximum(x, 0.0) + jnp.log1p(jnp.exp(-jnp.abs(x)))


def _expm1(x, exp_x):
    small = x * (1.0 + 0.5 * x * (1.0 + (1.0 / 3.0) * x * (1.0 + 0.25 * x)))
    return jnp.where(jnp.abs(x) < 0.05, small, exp_x - 1.0)


_GELU_K = math.sqrt(2.0 / math.pi)
_GELU_C = 0.044715


def _gelu(x):
    t = jnp.tanh(_GELU_K * (x + _GELU_C * x * x * x))
    return 0.5 * x * (1.0 + t)


def _gelu_and_grad(x):
    t = jnp.tanh(_GELU_K * (x + _GELU_C * x * x * x))
    g = 0.5 * x * (1.0 + t)
    dg = 0.5 * (1.0 + t) + 0.5 * x * (1.0 - t * t) * _GELU_K * (1.0 + 3.0 * _GELU_C * x * x)
    return g, dg


def _shift_down(x, k):
    if k == 0:
        return x
    rows = lax.broadcasted_iota(jnp.int32, x.shape, 0)
    return jnp.where(rows >= k, pltpu.roll(x, k, 0), 0.0)


def _shift_up(x, k):
    if k == 0:
        return x
    s = x.shape[0]
    rows = lax.broadcasted_iota(jnp.int32, x.shape, 0)
    return jnp.where(rows < s - k, pltpu.roll(x, s - k, 0), 0.0)


def _dot(a, b, dims):
    return lax.dot_general(a.astype(BF16), b.astype(BF16), (dims, ((), ())), preferred_element_type=F32)


_NN = ((1,), (0,))
_NT = ((1,), (1,))
_TN = ((0,), (0,))


def _mm(a, b, mode, *, name, out_dtype=F32, scale=None, bias=None, add=None, tm=1024, tn=1024, tk=1024,
        b_chunk=None, out_chunk=None):
    pieces = list(a) if isinstance(a, (list, tuple)) else [a]
    ksize = lambda p: p.shape[0] if mode == "tn" else p.shape[1]
    if b_chunk is None:
        brows, bcols = b.shape
    else:
        brows, bcols = b.shape[1], b.shape[0] * b_chunk
    k = sum(ksize(p) for p in pieces)
    if mode == "nt":
        m, n = pieces[0].shape[0], brows
        assert bcols == k, (bcols, k)
    else:
        m, n = (pieces[0].shape[1] if mode == "tn" else pieces[0].shape[0]), bcols
        assert brows == k, (brows, k)
    tm = _tile(m, tm, LANES)
    ncut = n
    if b_chunk is not None and mode != "nt":
        ncut = b_chunk
    if out_chunk is not None:
        ncut = math.gcd(ncut, out_chunk)
    tn = _tile(ncut, tn, LANES)
    kcut = b_chunk if (b_chunk is not None and mode == "nt") else k
    for p in pieces:
        kcut = math.gcd(kcut, ksize(p))
    tk = _tile(kcut, tk, LANES)
    nk = k // tk
    dims = {"nn": _NN, "nt": _NT, "tn": _TN}[mode]
    counts = [ksize(p) // tk for p in pieces]
    starts = [sum(counts[:i]) for i in range(len(pieces))]
    n_pieces = len(pieces)

    def a_spec(s0, cnt):
        kmap = (lambda kk: kk) if n_pieces == 1 else (lambda kk: jnp.clip(kk - s0, 0, cnt - 1))
        if mode == "tn":
            return pl.BlockSpec((tk, tm), lambda i, j, kk: (kmap(kk), i))
        return pl.BlockSpec((tm, tk), lambda i, j, kk: (i, kmap(kk)))

    if b_chunk is None:
        if mode == "nt":
            b_spec = pl.BlockSpec((tn, tk), lambda i, j, kk: (j, kk))
        else:
            b_spec = pl.BlockSpec((tk, tn), lambda i, j, kk: (kk, j))
    elif mode == "nt":
        per_b = b_chunk // tk
        b_spec = pl.BlockSpec((None, tn, tk), lambda i, j, kk: (kk // per_b, j, kk % per_b))
    else:
        per_b = b_chunk // tn
        b_spec = pl.BlockSpec((None, tk, tn), lambda i, j, kk: (j // per_b, kk, j % per_b))
    if out_chunk is None:
        out_spec = pl.BlockSpec((tm, tn), lambda i, j, kk: (i, j))
        out_shape = jax.ShapeDtypeStruct((m, n), out_dtype)
    else:
        per_o = out_chunk // tn
        out_spec = pl.BlockSpec((None, tm, tn), lambda i, j, kk: (j // per_o, i, j % per_o))
        out_shape = jax.ShapeDtypeStruct((n // out_chunk, m, out_chunk), out_dtype)
    in_specs = [a_spec(s0, cnt) for s0, cnt in zip(starts, counts)] + [b_spec]
    args = pieces + [b]
    if bias is not None:
        in_specs.append(pl.BlockSpec((1, tn), lambda i, j, kk: (0, j)))
        args.append(bias)
    if add is not None:
        in_specs.append(pl.BlockSpec((tm, tn), lambda i, j, kk: (i, j)))
        args.append(add)

    def kern(*refs):
        b_ref = refs[n_pieces]
        o_ref = refs[n_pieces + 1 + (bias is not None) + (add is not None)]

        def finish(r):
            if scale is not None:
                r = r * scale
            pos = n_pieces + 1
            if bias is not None:
                r = r + refs[pos][...]
                pos += 1
            if add is not None:
                r = r + refs[pos][...]
            o_ref[...] = r.astype(out_dtype)

        if nk == 1:
            finish(_dot(refs[0][...], b_ref[...], dims))
            return
        acc = refs[-1]
        kk = pl.program_id(2)

        @pl.when(kk == 0)
        def _():
            acc[...] = jnp.zeros_like(acc)

        if n_pieces == 1:
            acc[...] += _dot(refs[0][...], b_ref[...], dims)
        else:
            for idx in range(n_pieces):
                @pl.when((kk >= starts[idx]) & (kk < starts[idx] + counts[idx]))
                def _(idx=idx):
                    acc[...] += _dot(refs[idx][...], b_ref[...], dims)

        @pl.when(kk == nk - 1)
        def _():
            finish(acc[...])

    return pl.pallas_call(
        kern, name=name,
        grid=(m // tm, n // tn, nk),
        in_specs=in_specs, out_specs=out_spec, out_shape=out_shape,
        scratch_shapes=[pltpu.VMEM((tm, tn), F32)] if nk > 1 else [],
        compiler_params=_cp(3),
    )(*args)


def _silu_pad(c, rows):
    bl, d = c.shape

    def kern(c_ref, o_ref):
        o_ref[...] = jnp.zeros_like(o_ref)
        v = c_ref[...]
        o_ref[0:bl, :] = v * _sigmoid(v)

    return pl.pallas_call(kern, name="silu_pad", out_shape=jax.ShapeDtypeStruct((rows, d), F32))(c)


def _rowsum(x, name):
    r, n = x.shape

    def kern(x_ref, o_ref):
        o_ref[...] = jnp.sum(x_ref[...], axis=0, keepdims=True)

    return pl.pallas_call(kern, name=name, out_shape=jax.ShapeDtypeStruct((1, n), F32))(x)


def _norm_parts(x, g):
    r = lax.rsqrt(jnp.mean(x * x, axis=-1, keepdims=True) + RMS_EPS)
    xh = x * r
    return r, xh, xh * g


def _norm_bwd_parts(dh, xh, r, g, sc):
    n = xh * g
    dn = dh * (1.0 + sc)
    dxh = dn * g
    dx = r * (dxh - xh * jnp.mean(dxh * xh, axis=-1, keepdims=True))
    return dx, dh, dh * n, dn * xh


def _act_specs(ts, d, n):
    return [pl.BlockSpec((1, ts, d), lambda b, t: (b, t, 0)) for _ in range(n)]


def _vec_spec(d):
    return pl.BlockSpec((1, 1, d), lambda b, t: (b, 0, 0))


def _par_spec(d):
    return pl.BlockSpec((1, d), lambda b, t: (0, 0))


def _norm_mod_fwd(x3, g, sh, sc):
    bl, s, d = x3.shape
    ts = _tile(s, 512, SUBLANES)

    def kern(x_ref, g_ref, sh_ref, sc_ref, h_ref):
        _, _, n = _norm_parts(x_ref[0], g_ref[...])
        h_ref[0] = (n * (1.0 + sc_ref[0]) + sh_ref[0]).astype(BF16)

    return pl.pallas_call(
        kern, name="norm_mod_fwd", grid=(bl, s // ts),
        in_specs=_act_specs(ts, d, 1) + [_par_spec(d), _vec_spec(d), _vec_spec(d)],
        out_specs=_act_specs(ts, d, 1)[0],
        out_shape=jax.ShapeDtypeStruct((bl, s, d), BF16),
        compiler_params=_cp(2),
    )(x3, g, sh, sc)


def _resid_norm_fwd(x3, y3, gate, g, sh, sc):
    bl, s, d = x3.shape
    ts = _tile(s, 512, SUBLANES)

    def kern(x_ref, y_ref, gate_ref, g_ref, sh_ref, sc_ref, x1_ref, h_ref):
        x1 = x_ref[0] + gate_ref[0] * y_ref[0]
        x1_ref[0] = x1
        _, _, n = _norm_parts(x1, g_ref[...])
        h_ref[0] = (n * (1.0 + sc_ref[0]) + sh_ref[0]).astype(BF16)

    return pl.pallas_call(
        kern, name="resid_norm_fwd", grid=(bl, s // ts),
        in_specs=_act_specs(ts, d, 2) + [_vec_spec(d), _par_spec(d), _vec_spec(d), _vec_spec(d)],
        out_specs=_act_specs(ts, d, 2),
        out_shape=[jax.ShapeDtypeStruct((bl, s, d), F32), jax.ShapeDtypeStruct((bl, s, d), BF16)],
        compiler_params=_cp(2),
    )(x3, y3, gate, g, sh, sc)


def _norm_mod_bwd(dh3, x3, dres3, g, sc, name):
    bl, s, d = x3.shape
    ts = _tile(s, 512, SUBLANES)

    def kern(dh_ref, x_ref, dres_ref, g_ref, sc_ref, dx_ref, dsh_ref, dsc_ref, dg_ref):
        b, t = pl.program_id(0), pl.program_id(1)
        gv = g_ref[...]
        r, xh, _ = _norm_parts(x_ref[0], gv)
        dx, a, bb, cc = _norm_bwd_parts(dh_ref[0], xh, r, gv, sc_ref[0])
        dx_ref[0] = dres_ref[0] + dx

        @pl.when(t == 0)
        def _():
            dsh_ref[...] = jnp.zeros_like(dsh_ref)
            dsc_ref[...] = jnp.zeros_like(dsc_ref)

        @pl.when((t == 0) & (b == 0))
        def _():
            dg_ref[...] = jnp.zeros_like(dg_ref)

        dsh_ref[0] += jnp.sum(a, axis=0, keepdims=True)
        dsc_ref[0] += jnp.sum(bb, axis=0, keepdims=True)
        dg_ref[...] += jnp.sum(cc, axis=0, keepdims=True)

    return pl.pallas_call(
        kern, name=name, grid=(bl, s // ts),
        in_specs=_act_specs(ts, d, 3) + [_par_spec(d), _vec_spec(d)],
        out_specs=[_act_specs(ts, d, 1)[0], _vec_spec(d), _vec_spec(d), _par_spec(d)],
        out_shape=[jax.ShapeDtypeStruct((bl, s, d), F32), jax.ShapeDtypeStruct((bl, 1, d), F32),
                   jax.ShapeDtypeStruct((bl, 1, d), F32), jax.ShapeDtypeStruct((1, d), F32)],
        compiler_params=_cp(2),
    )(dh3, x3, dres3, g, sc)


def _gate_bwd(dx3, y3, gate, name):
    bl, s, d = dx3.shape
    ts = _tile(s, 512, SUBLANES)

    def kern(dx_ref, y_ref, gate_ref, dy_ref, dgate_ref):
        t = pl.program_id(1)
        dx = dx_ref[0]
        dy_ref[0] = (gate_ref[0] * dx).astype(BF16)

        @pl.when(t == 0)
        def _():
            dgate_ref[...] = jnp.zeros_like(dgate_ref)

        dgate_ref[0] += jnp.sum(dx * y_ref[0], axis=0, keepdims=True)

    return pl.pallas_call(
        kern, name=name, grid=(bl, s // ts),
        in_specs=_act_specs(ts, d, 2) + [_vec_spec(d)],
        out_specs=[_act_specs(ts, d, 1)[0], _vec_spec(d)],
        out_shape=[jax.ShapeDtypeStruct((bl, s, d), BF16), jax.ShapeDtypeStruct((bl, 1, d), F32)],
        compiler_params=_cp(2),
    )(dx3, y3, gate)


def _final_fwd_bwd(x1, yf, gate2, g, shf, scf, tgt):
    bl, s, d = x1.shape
    ts = _tile(s, 512, SUBLANES)

    def kern(x1_ref, yf_ref, gate_ref, g_ref, sh_ref, sc_ref, tgt_ref, dx_ref, dsh_ref, dsc_ref, dg_ref, loss_ref):
        b, t = pl.program_id(0), pl.program_id(1)
        gv, sc = g_ref[...], sc_ref[0]
        x2 = x1_ref[0] + gate_ref[0] * yf_ref[0]
        r, xh, n = _norm_parts(x2, gv)
        err = n * (1.0 + sc) + sh_ref[0] - tgt_ref[0]
        dx, a, bb, cc = _norm_bwd_parts(err * (1.0 / d), xh, r, gv, sc)
        dx_ref[0] = dx

        @pl.when(t == 0)
        def _():
            dsh_ref[...] = jnp.zeros_like(dsh_ref)
            dsc_ref[...] = jnp.zeros_like(dsc_ref)

        @pl.when((t == 0) & (b == 0))
        def _():
            dg_ref[...] = jnp.zeros_like(dg_ref)
            loss_ref[...] = jnp.zeros_like(loss_ref)

        dsh_ref[0] += jnp.sum(a, axis=0, keepdims=True)
        dsc_ref[0] += jnp.sum(bb, axis=0, keepdims=True)
        dg_ref[...] += jnp.sum(cc, axis=0, keepdims=True)
        tok = jnp.mean(err * err, axis=-1, keepdims=True)
        loss_ref[...] += 0.5 * jnp.sum(tok, axis=0, keepdims=True)

    return pl.pallas_call(
        kern, name="final_fwd_bwd", grid=(bl, s // ts),
        in_specs=_act_specs(ts, d, 2) + [_vec_spec(d), _par_spec(d), _vec_spec(d), _vec_spec(d)] + _act_specs(ts, d, 1),
        out_specs=[_act_specs(ts, d, 1)[0], _vec_spec(d), _vec_spec(d), _par_spec(d),
                   pl.BlockSpec((1, 1), lambda b, t: (0, 0))],
        out_shape=[jax.ShapeDtypeStruct((bl, s, d), F32), jax.ShapeDtypeStruct((bl, 1, d), F32),
                   jax.ShapeDtypeStruct((bl, 1, d), F32), jax.ShapeDtypeStruct((1, d), F32),
                   jax.ShapeDtypeStruct((1, 1), F32)],
        compiler_params=_cp(2),
    )(x1, yf, gate2, g, shf, scf, tgt)


def _rnn_gates(xr, cw, cb, wa, ba, wi, bi, lam):
    kw = cw.shape[0]
    xc = cb
    for k in range(kw):
        xc = xc + _shift_down(xr, kw - 1 - k) * cw[k:k + 1, :]
    r = _sigmoid(_dot(xc, wa, _NN) + ba)
    i = _sigmoid(_dot(xc, wi, _NN) + bi)
    sp = _softplus(-lam)
    log_a = -LRU_C * r * sp
    a = jnp.exp(log_a)
    mult = jnp.sqrt(-_expm1(2.0 * log_a, a * a))
    return xc, r, i, sp, a, mult


def _segment_scan(a_s, u_s, h_s, p_s, reverse):
    s, c = a_s.shape
    seg = s // SCAN_SEGMENTS

    def step(n, carry):
        t = (seg - 1 - n) if reverse else n
        h, p = carry
        av = a_s[pl.ds(t, SCAN_SEGMENTS, stride=seg), :]
        uv = u_s[pl.ds(t, SCAN_SEGMENTS, stride=seg), :]
        h = av * h + uv
        p = p * av
        h_s[pl.ds(t, SCAN_SEGMENTS, stride=seg), :] = h
        p_s[pl.ds(t, SCAN_SEGMENTS, stride=seg), :] = p
        return h, p

    lax.fori_loop(0, seg, step, (jnp.zeros((SCAN_SEGMENTS, c), F32), jnp.ones((SCAN_SEGMENTS, c), F32)), unroll=8)
    carry = jnp.zeros((1, c), F32)
    order = range(SCAN_SEGMENTS - 1, -1, -1) if reverse else range(SCAN_SEGMENTS)
    for j in order:
        rows = pl.ds(j * seg, seg)
        fixed = h_s[rows, :] + p_s[rows, :] * carry
        h_s[rows, :] = fixed
        carry = fixed[0:1, :] if reverse else fixed[seg - 1:seg, :]


def _rnn_specs(s, rb, nb):
    act = lambda off: pl.BlockSpec((1, s, rb), lambda b, n, off=off: (b, 0, off + n))
    par = pl.BlockSpec((1, rb), lambda b, n: (0, n))
    wsp = pl.BlockSpec((1, rb, rb), lambda b, n: (n, 0, 0))
    return act, par, wsp


def _rnn_fwd(zr3, cw, cb, wa, ba, wi, bi, lam):
    bl, s, two = zr3.shape
    nb, rb, _ = wa.shape
    dr = nb * rb
    kw = cw.shape[0]
    act, par, wsp = _rnn_specs(s, rb, nb)

    def kern(xr_ref, gr_ref, cw_ref, cb_ref, wa_ref, ba_ref, wi_ref, bi_ref, lam_ref, h_ref, y_ref, a_s, u_s, h_s, p_s):
        xc, r, i, sp, a, mult = _rnn_gates(xr_ref[0], cw_ref[...], cb_ref[...], wa_ref[0], ba_ref[...],
                                           wi_ref[0], bi_ref[...], lam_ref[...])
        a_s[...] = a
        u_s[...] = mult * (i * xc)
        _segment_scan(a_s, u_s, h_s, p_s, reverse=False)
        h = h_s[...]
        h_ref[0] = h
        y_ref[0] = (_gelu(gr_ref[0]) * h).astype(BF16)

    return pl.pallas_call(
        kern, name="rnn_fwd", grid=(bl, nb),
        in_specs=[act(0), act(nb), pl.BlockSpec((kw, rb), lambda b, n: (0, n)), par, wsp, par, wsp, par, par],
        out_specs=[act(0), act(0)],
        out_shape=[jax.ShapeDtypeStruct((bl, s, dr), F32), jax.ShapeDtypeStruct((bl, s, dr), BF16)],
        scratch_shapes=[pltpu.VMEM((s, rb), F32)] * 4,
        compiler_params=_cp(2),
    )(zr3, zr3, cw, cb, wa, ba, wi, bi, lam)


def _rnn_bwd(zr3, h3, dy3, cw, cb, wa, ba, wi, bi, lam):
    bl, s, _ = zr3.shape
    nb, rb, _ = wa.shape
    dr = nb * rb
    kw = cw.shape[0]
    act = lambda off: pl.BlockSpec((1, s, rb), lambda n, b, off=off: (b, 0, off + n))
    par = pl.BlockSpec((1, rb), lambda n, b: (0, n))
    wsp = pl.BlockSpec((1, rb, rb), lambda n, b: (n, 0, 0))
    cws = pl.BlockSpec((kw, rb), lambda n, b: (0, n))

    def kern(xr_ref, gr_ref, h_ref, dy_ref, cw_ref, cb_ref, wa_ref, ba_ref, wi_ref, bi_ref, lam_ref,
             dxr_ref, dgr_ref, dcw_ref, dcb_ref, dwa_ref, dba_ref, dwi_ref, dbi_ref, dlam_ref, a_s, u_s, h_s, p_s):
        b = pl.program_id(1)
        xr, cwv, lamv = xr_ref[0], cw_ref[...], lam_ref[...]
        wav, wiv = wa_ref[0], wi_ref[0]
        xc, r, i, sp, a, mult = _rnn_gates(xr, cwv, cb_ref[...], wav, ba_ref[...], wiv, bi_ref[...], lamv)
        h, dy = h_ref[0], dy_ref[0]
        ge, dge = _gelu_and_grad(gr_ref[0])
        dgr_ref[0] = (dy * h * dge).astype(BF16)
        a_s[...] = _shift_up(a, 1)
        u_s[...] = dy * ge
        _segment_scan(a_s, u_s, h_s, p_s, reverse=True)
        g = h_s[...]
        da = g * _shift_down(h, 1)
        ix = i * xc
        dlog_a = da * a + (g * ix) * (-(a * a) / mult)
        di = g * mult * xc
        dpa = (dlog_a * (-LRU_C * sp)) * r * (1.0 - r)
        dpi = di * i * (1.0 - i)
        dxc = g * mult * i + _dot(dpa, wav, _NT) + _dot(dpi, wiv, _NT)
        dxr = jnp.zeros_like(dxc)
        dcw_rows = []
        for k in range(kw):
            dxr = dxr + _shift_up(dxc, kw - 1 - k) * cwv[k:k + 1, :]
            dcw_rows.append(jnp.sum(dxc * _shift_down(xr, kw - 1 - k), axis=0, keepdims=True))
        dxr_ref[0] = dxr.astype(BF16)

        @pl.when(b == 0)
        def _():
            for ref in (dcw_ref, dcb_ref, dwa_ref, dba_ref, dwi_ref, dbi_ref, dlam_ref):
                ref[...] = jnp.zeros_like(ref)

        for k in range(kw):
            dcw_ref[k:k + 1, :] += dcw_rows[k]
        dcb_ref[...] += jnp.sum(dxc, axis=0, keepdims=True)
        dwa_ref[0] += _dot(xc, dpa, _TN)
        dwi_ref[0] += _dot(xc, dpi, _TN)
        dba_ref[...] += jnp.sum(dpa, axis=0, keepdims=True)
        dbi_ref[...] += jnp.sum(dpi, axis=0, keepdims=True)
        dsp = jnp.sum(dlog_a * (-LRU_C * r), axis=0, keepdims=True)
        dlam_ref[...] += dsp * (-_sigmoid(-lamv))

    vec = jax.ShapeDtypeStruct((1, dr), F32)
    wsh = jax.ShapeDtypeStruct((nb, rb, rb), F32)
    return pl.pallas_call(
        kern, name="rnn_bwd", grid=(nb, bl),
        in_specs=[act(0), act(nb), act(0), act(0), cws, par, wsp, par, wsp, par, par],
        out_specs=[act(0), act(0), cws, par, wsp, par, wsp, par, par],
        out_shape=[jax.ShapeDtypeStruct((bl, s, dr), BF16), jax.ShapeDtypeStruct((bl, s, dr), BF16),
                   jax.ShapeDtypeStruct((kw, dr), F32), vec, wsh, vec, wsh, vec, vec],
        scratch_shapes=[pltpu.VMEM((s, rb), F32)] * 4,
        compiler_params=_cp(2),
    )(zr3, zr3, h3, dy3, cw, cb, wa, ba, wi, bi, lam)


def _tri(n, upper):
    r = lax.broadcasted_iota(jnp.int32, (n, n), 0)
    c = lax.broadcasted_iota(jnp.int32, (n, n), 1)
    return jnp.where((c >= r) if upper else (c <= r), 1.0, 0.0).astype(F32)


def _fgate_fwd(zf3, bf):
    bl, s, w = zf3.shape
    ch = _tile(s, 256, SUBLANES)

    def kern(z_ref, b_ref, f_ref):
        tri = _tri(ch, upper=False)
        carry = jnp.zeros((1, w), F32)
        for j in range(s // ch):
            rows = pl.ds(j * ch, ch)
            lf = -_softplus(-(z_ref[0, rows, :] + b_ref[...]))
            out = jnp.dot(tri, lf, precision=lax.Precision.HIGHEST, preferred_element_type=F32) + carry
            f_ref[0, rows, :] = out
            carry = out[ch - 1:ch, :]

    return pl.pallas_call(
        kern, name="fgate_fwd", grid=(bl,),
        in_specs=[pl.BlockSpec((1, s, w), lambda b: (b, 0, 0)), pl.BlockSpec((1, w), lambda b: (0, 0))],
        out_specs=pl.BlockSpec((1, s, w), lambda b: (b, 0, 0)),
        out_shape=jax.ShapeDtypeStruct((bl, s, w), F32),
        compiler_params=_cp(1),
    )(zf3, bf)


def _fgate_bwd(dfk3, dfq3, zf3, bf):
    bl, s, w = zf3.shape
    ch = _tile(s, 256, SUBLANES)

    def kern(dfk_ref, dfq_ref, z_ref, b_ref, dz_ref, db_ref):
        b = pl.program_id(0)
        tri = _tri(ch, upper=True)
        carry = jnp.zeros((1, w), F32)
        dbsum = jnp.zeros((1, w), F32)
        for j in range(s // ch - 1, -1, -1):
            rows = pl.ds(j * ch, ch)
            df = dfk_ref[0, rows, :] + dfq_ref[0, rows, :]
            dlf = jnp.dot(tri, df, precision=lax.Precision.HIGHEST, preferred_element_type=F32) + carry
            carry = dlf[0:1, :]
            dz = dlf * _sigmoid(-(z_ref[0, rows, :] + b_ref[...]))
            dz_ref[0, rows, :] = dz.astype(BF16)
            dbsum = dbsum + jnp.sum(dz, axis=0, keepdims=True)

        @pl.when(b == 0)
        def _():
            db_ref[...] = jnp.zeros_like(db_ref)

        db_ref[...] += dbsum

    return pl.pallas_call(
        kern, name="fgate_bwd", grid=(bl,),
        in_specs=[pl.BlockSpec((1, s, w), lambda b: (b, 0, 0))] * 3 + [pl.BlockSpec((1, w), lambda b: (0, 0))],
        out_specs=[pl.BlockSpec((1, s, w), lambda b: (b, 0, 0)), pl.BlockSpec((1, w), lambda b: (0, 0))],
        out_shape=[jax.ShapeDtypeStruct((bl, s, w), BF16), jax.ShapeDtypeStruct((1, w), F32)],
        compiler_params=_cp(1),
    )(dfk3, dfq3, zf3, bf)


def _lanes(col, width):
    return col if width == LANES else jnp.concatenate([col] * (width // LANES), axis=1)


def _causal(sc, row0, col0, transposed):
    r = lax.broadcasted_iota(jnp.int32, sc.shape, 0) + row0
    c = lax.broadcasted_iota(jnp.int32, sc.shape, 1) + col0
    return jnp.where((c >= r) if transposed else (r >= c), sc, NEG_BIG)


def _attn_fwd(q3, kv3, fcol, frow, nh):
    bl, s, da = q3.shape
    dh = da // nh
    tq = _tile(s, 512, LANES)
    nq = s // tq

    def kern(q_ref, k_ref, v_ref, fk_ref, fq_ref, o_ref, lse_ref, m_s, l_s, acc):
        iq, ik = pl.program_id(2), pl.program_id(3)

        @pl.when(ik == 0)
        def _():
            m_s[...] = jnp.full_like(m_s, NEG_BIG)
            l_s[...] = jnp.zeros_like(l_s)
            acc[...] = jnp.zeros_like(acc)

        def block(masked):
            st = _dot(k_ref[0], q_ref[0], _NT) - _lanes(fk_ref[0], tq) + fq_ref[0]
            if masked:
                st = _causal(st, ik * tq, iq * tq, True)
            m_old = m_s[...]
            m_new = jnp.maximum(m_old, jnp.max(st, axis=0, keepdims=True))
            alpha = jnp.exp(m_old - m_new)
            pt = jnp.exp(st - m_new)
            l_s[...] = alpha * l_s[...] + jnp.sum(pt, axis=0, keepdims=True)
            acc[...] = alpha * acc[...] + _dot(v_ref[0], pt, _TN)
            m_s[...] = m_new

        pl.when(ik < iq)(functools.partial(block, False))
        pl.when(ik == iq)(functools.partial(block, True))

        @pl.when(ik == nq - 1)
        def _():
            l = l_s[...]
            o_ref[0] = (acc[...] / l).T
            lse_ref[0] = m_s[...] + jnp.log(l)

    kmap = lambda off: (lambda b, h, iq, ik: (b, jnp.minimum(ik, iq), off + h))
    return pl.pallas_call(
        kern, name="attn_fwd", grid=(bl, nh, nq, nq),
        in_specs=[pl.BlockSpec((1, tq, dh), lambda b, h, iq, ik: (b, iq, h)),
                  pl.BlockSpec((1, tq, dh), kmap(0)), pl.BlockSpec((1, tq, dh), kmap(nh)),
                  pl.BlockSpec((1, tq, LANES), lambda b, h, iq, ik: (b * nh + h, jnp.minimum(ik, iq), 0)),
                  pl.BlockSpec((1, 1, tq), lambda b, h, iq, ik: (b * nh + h, 0, iq))],
        out_specs=[pl.BlockSpec((1, tq, dh), lambda b, h, iq, ik: (b, iq, h)),
                   pl.BlockSpec((1, 1, tq), lambda b, h, iq, ik: (b * nh + h, 0, iq))],
        out_shape=[jax.ShapeDtypeStruct((bl, s, da), F32), jax.ShapeDtypeStruct((bl * nh, 1, s), F32)],
        scratch_shapes=[pltpu.VMEM((1, tq), F32), pltpu.VMEM((1, tq), F32), pltpu.VMEM((dh, tq), F32)],
        compiler_params=_cp(4),
    )(q3, kv3, kv3, fcol, frow)


def _attn_bwd(q3, kv3, do3, o3, lse_row, fcol, frow, nh, scale):
    bl, s, da = q3.shape
    dh = da // nh
    tk = _tile(s, 512, LANES)
    nk = s // tk

    def kern(q_ref, k_ref, v_ref, do_ref, o_ref, lse_ref, fk_ref, fq_ref, dq_ref, dk_ref, dv_ref, dfk_ref, dfq_ref,
             dq_acc, dk_acc, dv_acc, dfq_acc, delta_s):
        ik, iq = pl.program_id(2), pl.program_id(3)
        qrow = pl.ds(iq, 1)

        @pl.when((ik == 0) & (iq == 0))
        def _():
            dq_acc[...] = jnp.zeros_like(dq_acc)
            dfq_acc[...] = jnp.zeros_like(dfq_acc)

        @pl.when(ik == 0)
        def _():
            prod = do_ref[0] * o_ref[0]
            rows = lax.dot_general(jnp.ones((SUBLANES, dh), F32), prod, (_NT, ((), ())),
                                   precision=lax.Precision.HIGHEST, preferred_element_type=F32)
            delta_s[qrow, :] = rows[0:1, :]

        @pl.when(iq == 0)
        def _():
            dk_acc[...] = jnp.zeros_like(dk_acc)
            dv_acc[...] = jnp.zeros_like(dv_acc)

        def block(masked):
            q = q_ref[0]
            st = _dot(k_ref[0], q, _NT) - _lanes(fk_ref[0], tk) + fq_ref[0]
            if masked:
                st = _causal(st, ik * tk, iq * tk, True)
            pt = jnp.exp(st - lse_ref[0])
            dv_acc[...] += _dot(pt, do_ref[0], _NN)
            dpt = _dot(v_ref[0], do_ref[0], _NT)
            dst = (pt * (dpt - delta_s[qrow, :])).astype(BF16)
            q_ones = jnp.concatenate([q, jnp.ones_like(q)], axis=1)
            dk_acc[...] += _dot(dst, q_ones, _NN)
            qrows = pl.ds(pl.multiple_of(iq * tk, tk), tk)
            dq_acc[qrows, :] += _dot(dst, k_ref[0], _TN)
            dfq_acc[qrow, :] += jnp.sum(dst.astype(F32), axis=0, keepdims=True)

        pl.when(iq > ik)(functools.partial(block, False))
        pl.when(iq == ik)(functools.partial(block, True))

        @pl.when(iq == nk - 1)
        def _():
            ext = dk_acc[...]
            dk_ref[0] = ext[:, :dh].astype(BF16)
            dfk_ref[0] = -ext[:, dh:]
            dv_ref[0] = dv_acc[...].astype(BF16)

        @pl.when((ik == nk - 1) & (iq == nk - 1))
        def _():
            dq_ref[0] = (dq_acc[...] * scale).astype(BF16)
            dfq_ref[0] = dfq_acc[...]

    later = lambda ik, iq: jnp.maximum(iq, ik)
    qmap = lambda b, h, ik, iq: (b, later(ik, iq), h)
    omap = lambda b, h, ik, iq: (b, jnp.where(ik == 0, iq, 0), h)
    rmap = lambda b, h, ik, iq: (b * nh + h, 0, later(ik, iq))
    kmap = lambda off: (lambda b, h, ik, iq: (b, ik, off + h))
    bmap = lambda b, h, ik, iq: (b * nh + h, ik, 0)
    return pl.pallas_call(
        kern, name="attn_bwd", grid=(bl, nh, nk, nk),
        in_specs=[pl.BlockSpec((1, tk, dh), qmap), pl.BlockSpec((1, tk, dh), kmap(0)), pl.BlockSpec((1, tk, dh), kmap(nh)),
                  pl.BlockSpec((1, tk, dh), qmap), pl.BlockSpec((1, tk, dh), omap), pl.BlockSpec((1, 1, tk), rmap),
                  pl.BlockSpec((1, tk, LANES), bmap), pl.BlockSpec((1, 1, tk), rmap)],
        out_specs=[pl.BlockSpec((1, s, dh), lambda b, h, ik, iq: (b, 0, h)),
                   pl.BlockSpec((1, tk, dh), kmap(0)), pl.BlockSpec((1, tk, dh), kmap(0)),
                   pl.BlockSpec((1, tk, LANES), bmap),
                   pl.BlockSpec((1, nk, tk), lambda b, h, ik, iq: (b * nh + h, 0, 0))],
        out_shape=[jax.ShapeDtypeStruct((bl, s, da), BF16), jax.ShapeDtypeStruct((bl, s, da), BF16),
                   jax.ShapeDtypeStruct((bl, s, da), BF16), jax.ShapeDtypeStruct((bl * nh, s, LANES), F32),
                   jax.ShapeDtypeStruct((bl * nh, nk, tk), F32)],
        scratch_shapes=[pltpu.VMEM((s, dh), F32), pltpu.VMEM((tk, 2 * dh), F32), pltpu.VMEM((tk, dh), F32),
                        pltpu.VMEM((nk, tk), F32), pltpu.VMEM((nk, tk), F32)],
        compiler_params=_cp(4),
    )(q3, kv3, kv3, do3, o3, lse_row, fcol, frow)


def _merge_fwd(mg3, pr3, pa3):
    bl, s, d = pr3.shape
    ts = _tile(s, 256, SUBLANES)
    half = lambda j: pl.BlockSpec((1, ts, d), lambda b, t, j=j: (b, t, j))

    def kern(mr_ref, ma_ref, pr_ref, pa_ref, o_ref):
        o_ref[0] = (_sigmoid(mr_ref[0]) * pr_ref[0] + _sigmoid(ma_ref[0]) * pa_ref[0]).astype(BF16)

    return pl.pallas_call(
        kern, name="merge_fwd", grid=(bl, s // ts),
        in_specs=[half(0), half(1)] + _act_specs(ts, d, 2), out_specs=_act_specs(ts, d, 1)[0],
        out_shape=jax.ShapeDtypeStruct((bl, s, d), BF16), compiler_params=_cp(2),
    )(mg3, mg3, pr3, pa3)


def _merge_bwd(dm3, mg3, pr3, pa3):
    bl, s, d = pr3.shape
    ts = _tile(s, 256, SUBLANES)
    half = lambda j: pl.BlockSpec((1, ts, d), lambda b, t, j=j: (b, t, j))

    def kern(dm_ref, mr_ref, ma_ref, pr_ref, pa_ref, dpr_ref, dpa_ref, dmr_ref, dma_ref):
        dm = dm_ref[0]
        gr, ga = _sigmoid(mr_ref[0]), _sigmoid(ma_ref[0])
        dpr_ref[0] = (gr * dm).astype(BF16)
        dpa_ref[0] = (ga * dm).astype(BF16)
        dmr_ref[0] = (dm * pr_ref[0] * gr * (1.0 - gr)).astype(BF16)
        dma_ref[0] = (dm * pa_ref[0] * ga * (1.0 - ga)).astype(BF16)

    return pl.pallas_call(
        kern, name="merge_bwd", grid=(bl, s // ts),
        in_specs=_act_specs(ts, d, 1) + [half(0), half(1)] + _act_specs(ts, d, 2), out_specs=_act_specs(ts, d, 4),
        out_shape=[jax.ShapeDtypeStruct((bl, s, d), BF16)] * 4, compiler_params=_cp(2),
    )(dm3, mg3, mg3, pr3, pa3)


def _ffn_conv(gf, cw, cb):
    kw = cw.shape[0]
    y = cb
    for k in range(kw):
        y = y + _shift_down(gf, kw - 1 - k) * cw[k:k + 1, :]
    return y


def _ffn_act_fwd(up3, cw, cb):
    bl, s, two = up3.shape
    dff = two // 2
    kw = cw.shape[0]
    tc = _tile(dff, 256, LANES)
    nc = dff // tc

    def kern(gf_ref, uf_ref, cw_ref, cb_ref, o_ref):
        o_ref[0] = (_gelu(_ffn_conv(gf_ref[0], cw_ref[...], cb_ref[...])) * uf_ref[0]).astype(BF16)

    act = lambda off: pl.BlockSpec((1, s, tc), lambda b, j, off=off: (b, 0, off + j))
    return pl.pallas_call(
        kern, name="ffn_act_fwd", grid=(bl, nc),
        in_specs=[act(0), act(nc), pl.BlockSpec((kw, tc), lambda b, j: (0, j)), pl.BlockSpec((1, tc), lambda b, j: (0, j))],
        out_specs=act(0), out_shape=jax.ShapeDtypeStruct((bl, s, dff), BF16), compiler_params=_cp(2),
    )(up3, up3, cw, cb)


def _ffn_act_bwd(up3, dact3, cw, cb):
    bl, s, two = up3.shape
    dff = two // 2
    kw = cw.shape[0]
    tc = _tile(dff, 256, LANES)
    nc = dff // tc

    def kern(gf_ref, uf_ref, da_ref, cw_ref, cb_ref, dgf_ref, duf_ref, dcw_ref, dcb_ref):
        b = pl.program_id(1)
        gf, cwv, da = gf_ref[0], cw_ref[...], da_ref[0]
        ge, dge = _gelu_and_grad(_ffn_conv(gf, cwv, cb_ref[...]))
        duf_ref[0] = (da * ge).astype(BF16)
        dgc = da * uf_ref[0] * dge
        dgf = jnp.zeros_like(dgc)
        rows = []
        for k in range(kw):
            dgf = dgf + _shift_up(dgc, kw - 1 - k) * cwv[k:k + 1, :]
            rows.append(jnp.sum(dgc * _shift_down(gf, kw - 1 - k), axis=0, keepdims=True))
        dgf_ref[0] = dgf.astype(BF16)

        @pl.when(b == 0)
        def _():
            dcw_ref[...] = jnp.zeros_like(dcw_ref)
            dcb_ref[...] = jnp.zeros_like(dcb_ref)

        for k in range(kw):
            dcw_ref[k:k + 1, :] += rows[k]
        dcb_ref[...] += jnp.sum(dgc, axis=0, keepdims=True)

    act = lambda off: pl.BlockSpec((1, s, tc), lambda j, b, off=off: (b, 0, off + j))
    cws = pl.BlockSpec((kw, tc), lambda j, b: (0, j))
    cbs = pl.BlockSpec((1, tc), lambda j, b: (0, j))
    return pl.pallas_call(
        kern, name="ffn_act_bwd", grid=(nc, bl),
        in_specs=[act(0), act(nc), act(0), cws, cbs], out_specs=[act(0), act(0), cws, cbs],
        out_shape=[jax.ShapeDtypeStruct((bl, s, dff), BF16), jax.ShapeDtypeStruct((bl, s, dff), BF16),
                   jax.ShapeDtypeStruct((kw, dff), F32), jax.ShapeDtypeStruct((1, dff), F32)],
        compiler_params=_cp(2),
    )(up3, up3, dact3, cw, cb)


_HBM = pl.BlockSpec(memory_space=pltpu.HBM)


def _place():
    x, y, c = lax.axis_index("x"), lax.axis_index("y"), lax.axis_index("c")
    chips = dict(me=2 * x + y, nx=2 * (1 - x) + y, ny=2 * x + (1 - y), diag=2 * (1 - x) + (1 - y))
    peers = dict(nx=(1 - x, y, c), ny=(x, 1 - y, c), sib=(x, y, 1 - c))
    return c, chips, peers


def _remote(src, dst, sems, k, to):
    return pltpu.make_async_remote_copy(src_ref=src, dst_ref=dst, send_sem=sems[0].at[k], recv_sem=sems[1].at[k],
                                        device_id=to, device_id_type=MESH)


RS_STEPS = 2


def _piece(q, idx, n=1):
    start = idx * q
    if not isinstance(start, int):
        start = pl.multiple_of(start, SUBLANES)
    return pl.ds(start, n * q)


def _all_gather_chips(xs, name):
    nt = len(xs)
    per = 9

    def body(*refs):
        x_refs, o_refs = refs[:nt], refs[nt:2 * nt]
        send_sems, recv_sems = refs[2 * nt:]
        c, chip, peer = _place()
        sems = (send_sems, recv_sems)
        me, nx, ny, dg = chip["me"], chip["nx"], chip["ny"], chip["diag"]
        sends = []

        def arrive(k, dst):
            _remote(dst, dst, sems, k, peer["sib"]).wait_recv()

        def pass_on(k, blk, to):
            cp = _remote(blk, blk, sems, k, peer[to])
            cp.start()
            sends.append(cp)

        for t in range(nt):
            q = xs[t].shape[0] // 4
            half = _piece(q, 2 * c, 2)
            for k, to in ((0, "nx"), (1, "ny")):
                cp = _remote(x_refs[t].at[half], o_refs[t].at[me, half], sems, per * t + k, peer[to])
                cp.start()
                sends.append(cp)
            cp = _remote(x_refs[t], o_refs[t].at[me], sems, per * t + 8, peer["sib"])
            cp.start()
            sends.append(cp)
        for t in range(nt):
            q, o, k0 = xs[t].shape[0] // 4, o_refs[t], per * t
            half, sub0, sub1 = _piece(q, 2 * c, 2), _piece(q, 2 * c), _piece(q, 2 * c + 1)
            arrive(k0 + 0, o.at[nx, half])
            pass_on(k0 + 2, o.at[nx, sub0], "ny")
            pass_on(k0 + 4, o.at[nx, half], "sib")
            arrive(k0 + 1, o.at[ny, half])
            pass_on(k0 + 3, o.at[ny, sub1], "nx")
            pass_on(k0 + 5, o.at[ny, half], "sib")
            arrive(k0 + 2, o.at[dg, sub0])
            pass_on(k0 + 6, o.at[dg, sub0], "sib")
            arrive(k0 + 3, o.at[dg, sub1])
            pass_on(k0 + 7, o.at[dg, sub1], "sib")
        for t in range(nt):
            q, o, k0 = xs[t].shape[0] // 4, o_refs[t], per * t
            arrive(k0 + 4, o.at[nx, _piece(q, 2 * (1 - c), 2)])
            arrive(k0 + 5, o.at[ny, _piece(q, 2 * (1 - c), 2)])
            arrive(k0 + 6, o.at[dg, _piece(q, 2 * (1 - c))])
            arrive(k0 + 7, o.at[dg, _piece(q, 2 * (1 - c) + 1)])
            arrive(k0 + 8, o.at[me])
        for cp in sends:
            cp.wait_send()

    return pl.pallas_call(
        body, name=name, in_specs=[_HBM] * nt, out_specs=[_HBM] * nt,
        out_shape=[jax.ShapeDtypeStruct((N_CHIPS,) + x.shape, x.dtype) for x in xs],
        scratch_shapes=[pltpu.SemaphoreType.DMA((per * nt,)), pltpu.SemaphoreType.DMA((per * nt,))],
    )(*xs)


def _exchange(name, xs, out_shapes, plan):
    nt = len(xs)

    def body(*refs):
        x_refs, o_refs = refs[:nt], refs[nt:2 * nt]
        send_sems, recv_sems = refs[2 * nt:]
        c, chip, peer = _place()
        cps = []
        for t in range(nt):
            for src, dst, to in plan(c, chip, x_refs[t], o_refs[t], xs[t].shape):
                cps.append(_remote(src, dst, (send_sems, recv_sems), len(cps), peer[to]))
        for cp in cps:
            cp.start()
        for cp in cps:
            cp.wait()

    n_copies = nt * len(plan(0, dict(me=0, nx=2, ny=1, diag=3), None, None, xs[0].shape, count_only=True))
    return pl.pallas_call(
        body, name=name, in_specs=[_HBM] * nt, out_specs=[_HBM] * nt,
        out_shape=[jax.ShapeDtypeStruct(s, F32) for s in out_shapes],
        scratch_shapes=[pltpu.SemaphoreType.DMA((n_copies,)), pltpu.SemaphoreType.DMA((n_copies,))],
    )(*xs)


def _plan_sibling(c, chip, g, out, shape, count_only=False):
    if count_only:
        return [None] * N_CHIPS
    q = shape[1] // 4
    return [(g.at[j, _piece(q, 2 * (1 - c), 2)], out.at[j], "sib") for j in range(N_CHIPS)]


def _plan_first(c, chip, p, out, shape, count_only=False):
    if count_only:
        return [None] * 4
    q = shape[1] // 2
    return [(p.at[chip["nx"], _piece(q, 0)], out.at[0], "nx"), (p.at[chip["diag"], _piece(q, 0)], out.at[1], "nx"),
            (p.at[chip["ny"], _piece(q, 1)], out.at[2], "ny"), (p.at[chip["diag"], _piece(q, 1)], out.at[3], "ny")]


def _plan_second(c, chip, p, out, shape, count_only=False):
    if count_only:
        return [None] * 2
    return [(p.at[1], out.at[0], "ny"), (p.at[3], out.at[1], "nx")]


def _rs_last(ps):
    nt = len(ps)

    def body(*refs):
        p_refs, o_refs = refs[:nt], refs[nt:2 * nt]
        send_sems, recv_sems = refs[2 * nt:]
        c, _, peer = _place()
        sems = (send_sems, recv_sems)
        cps = []
        for t in range(nt):
            q = ps[t].shape[0] // 4
            mine = _piece(q, 2 * c, 2)
            cps.append(_remote(p_refs[t].at[mine], o_refs[t].at[mine], sems, t, peer["sib"]))
            cps[-1].start()
        for t in range(nt):
            q = ps[t].shape[0] // 4
            theirs = _piece(q, 2 * (1 - c), 2)
            cps[t].wait_send()
            _remote(p_refs[t].at[theirs], o_refs[t].at[theirs], sems, t, peer["sib"]).wait_recv()

    return pl.pallas_call(
        body, name="rs_last", in_specs=[_HBM] * nt, out_specs=[_HBM] * nt,
        out_shape=[jax.ShapeDtypeStruct(p.shape, F32) for p in ps],
        input_output_aliases={t: t for t in range(nt)},
        scratch_shapes=[pltpu.SemaphoreType.DMA((nt,)), pltpu.SemaphoreType.DMA((nt,))],
    )(*ps)


def _add_stage(name, grid, a_list, b_list, a_map, b_map, tbs, out_shapes, out_map, prefetch=None):
    nt = len(a_list)
    lead = lambda shape: (None,) * (len(shape) - 2)

    def kern(*refs):
        refs = refs[(1 if prefetch is not None else 0):]
        for t in range(nt):
            refs[2 * nt + t][...] = refs[t][...] + refs[nt + t][...]

    in_specs = [pl.BlockSpec(lead(a.shape) + (tb, a.shape[-1]), a_map) for a, tb in zip(a_list, tbs)]
    in_specs += [pl.BlockSpec(lead(b.shape) + (tb, b.shape[-1]), b_map) for b, tb in zip(b_list, tbs)]
    out_specs = [pl.BlockSpec(lead(s) + (tb, s[-1]), out_map) for s, tb in zip(out_shapes, tbs)]
    out_shape = [jax.ShapeDtypeStruct(s, F32) for s in out_shapes]
    if prefetch is None:
        return pl.pallas_call(kern, name=name, grid=grid, in_specs=in_specs, out_specs=out_specs, out_shape=out_shape,
                              compiler_params=_cp(len(grid)))(*a_list, *b_list)
    return pl.pallas_call(
        kern, name=name,
        grid_spec=pltpu.PrefetchScalarGridSpec(num_scalar_prefetch=1, grid=grid, in_specs=in_specs, out_specs=out_specs),
        out_shape=out_shape, compiler_params=_cp(len(grid)))(prefetch, *a_list, *b_list)


def _reduce_scatter_chips(gs):
    x, y, c = lax.axis_index("x"), lax.axis_index("y"), lax.axis_index("c")
    me, nx, ny = 2 * x + y, 2 * (1 - x) + y, 2 * x + (1 - y)
    st = RS_STEPS
    qs = [g.shape[1] // 4 for g in gs]
    tbs = [q // st for q in qs]
    for g, tb in zip(gs, tbs):
        assert g.shape[1] == 4 * st * tb and tb % SUBLANES == 0, g.shape
    cols = [g.shape[2] for g in gs]
    core = jnp.reshape(c, (1,)).astype(jnp.int32)

    got = _exchange("rs_sibling", gs, [(N_CHIPS, 2 * q, cc) for q, cc in zip(qs, cols)], _plan_sibling)
    p0 = _add_stage("rs_add_sibling", (N_CHIPS, 2, st), gs, got,
                    lambda j, h, s, c_ref: (j, (2 * c_ref[0] + h) * st + s, 0), lambda j, h, s, c_ref: (j, h * st + s, 0),
                    tbs, [(N_CHIPS, 2 * q, cc) for q, cc in zip(qs, cols)], lambda j, h, s, c_ref: (j, h * st + s, 0),
                    prefetch=core)
    got = _exchange("rs_first", p0, [(4, q, cc) for q, cc in zip(qs, cols)], _plan_first)
    p1 = _add_stage("rs_add_first", (4, st), p0, got,
                    lambda k, s, i_ref: (i_ref[k], (k // 2) * st + s, 0), lambda k, s, i_ref: (k, s, 0),
                    tbs, [(4, q, cc) for q, cc in zip(qs, cols)], lambda k, s, i_ref: (k, s, 0),
                    prefetch=jnp.stack([me, ny, me, nx]).astype(jnp.int32))
    got = _exchange("rs_second", p1, [(2, q, cc) for q, cc in zip(qs, cols)], _plan_second)
    p2 = _add_stage("rs_add_second", (2, st), p1, got, lambda h, s, c_ref: (2 * h, s, 0), lambda h, s, c_ref: (h, s, 0),
                    tbs, [(4 * q, cc) for q, cc in zip(qs, cols)], lambda h, s, c_ref: ((2 * c_ref[0] + h) * st + s, 0),
                    prefetch=core)
    return _rs_last(p2)


def _adamw(g, w, m, v, name):
    rows, cc = g.shape
    tr = _tile(rows, max(SUBLANES, (1 << 18) // cc), SUBLANES)
    k1 = 1.0 - ADAM_B1 ** ADAM_STEP
    k2 = 1.0 - ADAM_B2 ** ADAM_STEP

    def kern(g_ref, w_ref, m_ref, v_ref, d_ref, nm_ref, nv_ref):
        gv = g_ref[...]
        nm = ADAM_B1 * m_ref[...] + (1.0 - ADAM_B1) * gv
        nv = ADAM_B2 * v_ref[...] + (1.0 - ADAM_B2) * (gv * gv)
        nm_ref[...] = nm
        nv_ref[...] = nv
        d_ref[...] = -ADAM_LR * ((nm / k1) / (jnp.sqrt(nv / k2) + ADAM_EPS) + ADAM_WD * w_ref[...])

    spec = pl.BlockSpec((tr, cc), lambda t: (t, 0))
    return pl.pallas_call(
        kern, name=name, grid=(rows // tr,), in_specs=[spec] * 4, out_specs=[spec] * 3,
        out_shape=[jax.ShapeDtypeStruct((rows, cc), F32)] * 3, compiler_params=_cp(1),
    )(g, w, m, v)


def _flat_pad(parts, total):
    flat = jnp.concatenate([p.reshape(-1) for p in parts])
    return jnp.pad(flat, (0, total - flat.shape[0]))


def _split_flat(flat, shapes):
    out, pos = [], 0
    for shp in shapes:
        size = math.prod(shp)
        out.append(flat[pos:pos + size].reshape(shp))
        pos += size
    return out


def _cols_of_chunks(chunks, lo, hi):
    width = chunks.shape[2]
    parts = []
    for j in range(chunks.shape[0]):
        a, b = max(lo, j * width), min(hi, (j + 1) * width)
        if a < b:
            parts.append(chunks[j, :, a - j * width:b - j * width])
    return parts[0] if len(parts) == 1 else jnp.concatenate(parts, axis=1)


def _chunks_of_cols(segments, n_chunks):
    total = sum(s.shape[1] for s in segments)
    width = total // n_chunks
    chunks = []
    for j in range(n_chunks):
        lo, hi, pos, parts = j * width, (j + 1) * width, 0, []
        for s in segments:
            a, b = max(lo, pos), min(hi, pos + s.shape[1])
            if a < b:
                parts.append(s[:, a - pos:b - pos])
            pos += s.shape[1]
        chunks.append(parts[0] if len(parts) == 1 else jnp.concatenate(parts, axis=1))
    return jnp.stack(chunks)


_WEIGHTS = ['w_ada', 'b_ada', 'g_norm1', 'w_in', 'w_rnn_conv', 'b_rnn_conv', 'w_lru_a', 'b_lru_a', 'w_lru_i', 'b_lru_i',
            'lru_lambda', 'b_fgate', 'w_proj_rnn', 'w_proj_attn', 'w_out', 'g_norm2', 'w_ffn_up', 'w_ffn_conv',
            'b_ffn_conv', 'w_ffn_down', 'w_ada_final', 'b_ada_final', 'g_final']
_MATMUL = dict(w_ada=True, w_in=True, w_proj_rnn=False, w_proj_attn=False, w_out=False, w_ffn_up=True,
               w_ffn_down=False, w_ada_final=True)
_CONV = ['w_rnn_conv', 'w_ffn_conv']
_REPLICATED = [n for n in _WEIGHTS if n not in _MATMUL and n not in _CONV]


def kernel(x, c, w_ada, b_ada, g_norm1, w_in, w_rnn_conv, b_rnn_conv, w_lru_a, b_lru_a, w_lru_i, b_lru_i, lru_lambda, b_fgate, w_proj_rnn, w_proj_attn, w_out, g_norm2, w_ffn_up, w_ffn_conv, b_ffn_conv, w_ffn_down, w_ada_final, b_ada_final, g_final, loss_target, m_w_ada, m_b_ada, m_g_norm1, m_w_in, m_w_rnn_conv, m_b_rnn_conv, m_w_lru_a, m_b_lru_a, m_w_lru_i, m_b_lru_i, m_lru_lambda, m_b_fgate, m_w_proj_rnn, m_w_proj_attn, m_w_out, m_g_norm2, m_w_ffn_up, m_w_ffn_conv, m_b_ffn_conv, m_w_ffn_down, m_w_ada_final, m_b_ada_final, m_g_final, v_w_ada, v_b_ada, v_g_norm1, v_w_in, v_w_rnn_conv, v_b_rnn_conv, v_w_lru_a, v_b_lru_a, v_w_lru_i, v_b_lru_i, v_lru_lambda, v_b_fgate, v_w_proj_rnn, v_w_proj_attn, v_w_out, v_g_norm2, v_w_ffn_up, v_w_ffn_conv, v_b_ffn_conv, v_w_ffn_down, v_w_ada_final, v_b_ada_final, v_g_final):
    args = locals()
    shape_of = {n: args[n].shape for n in _WEIGHTS}

    def view(a):
        if a.ndim >= 3:
            return a[0]
        return a[None, :] if a.ndim == 1 else a

    w2 = {n: view(args[n]) for n in _WEIGHTS}
    m2 = {n: args['m_' + n].reshape(w2[n].shape) for n in _WEIGHTS}
    v2 = {n: args['v_' + n].reshape(w2[n].shape) for n in _WEIGHTS}

    bl, s, d = x.shape
    t = bl * s
    nh = b_fgate.shape[-1]
    nb, rb = w_lru_a.shape[1], w_lru_a.shape[2]
    dr = nb * rb
    da = w2['w_proj_attn'].shape[0] * N_CHIPS
    dh = da // nh
    dff = w2['w_ffn_conv'].shape[1] * N_CHIPS
    scale = dh ** -0.5
    chip = 2 * lax.axis_index("x") + lax.axis_index("y")

    names = list(_MATMUL)
    gathered = dict(zip(names, _all_gather_chips([w2[n].astype(BF16) for n in names], "ag_weights")))
    n_conv = sum(w2[n].size for n in _CONV)
    rows_conv = -(-n_conv // (FLAT_COLS * 32)) * 32
    conv_local = _flat_pad([w2[n] for n in _CONV], rows_conv * FLAT_COLS).reshape(rows_conv, FLAT_COLS)
    conv_all = _all_gather_chips([conv_local], "ag_conv")[0].reshape(N_CHIPS, -1)
    conv_full, pos = {}, 0
    for n in _CONV:
        r, n4 = w2[n].shape
        blocks = conv_all[:, pos:pos + r * n4].reshape(N_CHIPS, r, n4)
        conv_full[n] = jnp.concatenate([blocks[j] for j in range(N_CHIPS)], axis=1)
        pos += r * n4
    rowmajor = lambda n: gathered[n].reshape(-1, gathered[n].shape[2])
    w_proj_rnn_f, w_proj_attn_f, w_out_f, w_ffn_down_f = (rowmajor(n) for n in ('w_proj_rnn', 'w_proj_attn', 'w_out', 'w_ffn_down'))
    ada_chunk, adaf_chunk, up_chunk = (w2[n].shape[1] for n in ('w_ada', 'w_ada_final', 'w_ffn_up'))

    o_q, o_k, o_fl = 2 * dr, 2 * dr + da, 2 * dr + 3 * da
    o_mg = o_fl + nh
    g_in_w = gathered['w_in']
    w_rnn, w_q = _cols_of_chunks(g_in_w, 0, o_q), _cols_of_chunks(g_in_w, o_q, o_k)
    w_kv, w_mg = _cols_of_chunks(g_in_w, o_k, o_fl), _cols_of_chunks(g_in_w, o_mg, o_mg + 2 * d)
    w_fl = jnp.pad(_cols_of_chunks(g_in_w, o_fl, o_mg), ((0, 0), (0, LANES - nh)))
    w_main = jnp.concatenate([w_rnn, w_q, w_kv, w_mg], axis=1)
    bf_pad = jnp.pad(w2['b_fgate'], ((0, 0), (0, LANES - nh)))

    c_act = _silu_pad(c, 16)
    mod = _mm(c_act, gathered['w_ada'], "nn", name="ada_fwd", bias=w2['b_ada'], b_chunk=ada_chunk)[:bl]
    sh1, sc1, gt1, sh2, sc2, gt2 = [mod[:, i * d:(i + 1) * d].reshape(bl, 1, d) for i in range(6)]
    modf = _mm(c_act, gathered['w_ada_final'], "nn", name="ada_final_fwd", bias=w2['b_ada_final'], b_chunk=adaf_chunk)[:bl]
    shf, scf = modf[:, :d].reshape(bl, 1, d), modf[:, d:].reshape(bl, 1, d)

    h1 = _norm_mod_fwd(x, w2['g_norm1'], sh1, sc1)
    h1f = h1.reshape(t, d)
    zr = _mm(h1f, w_rnn, "nn", name="in_rnn").reshape(bl, s, 2 * dr)
    q3 = _mm(h1f, w_q, "nn", name="in_q", out_dtype=BF16, scale=scale).reshape(bl, s, da)
    kv3 = _mm(h1f, w_kv, "nn", name="in_kv", out_dtype=BF16).reshape(bl, s, 2 * da)
    mg3 = _mm(h1f, w_mg, "nn", name="in_mg").reshape(bl, s, 2 * d)
    zf3 = _mm(h1f, w_fl, "nn", name="in_fl").reshape(bl, s, LANES)

    lru = (conv_full['w_rnn_conv'], w2['b_rnn_conv'], w2['w_lru_a'], w2['b_lru_a'], w2['w_lru_i'], w2['b_lru_i'], w2['lru_lambda'])
    hseq, y_rnn = _rnn_fwd(zr, *lru)

    f3 = _fgate_fwd(zf3, bf_pad)
    f_heads = f3[:, :, :nh].transpose(0, 2, 1).reshape(bl * nh, s)
    fcol = jnp.broadcast_to(f_heads[:, :, None], (bl * nh, s, LANES))
    frow = f_heads.reshape(bl * nh, 1, s)
    o3, lse_row = _attn_fwd(q3, kv3, fcol, frow, nh)

    pr3 = _mm(y_rnn.reshape(t, dr), w_proj_rnn_f, "nn", name="proj_rnn").reshape(bl, s, d)
    pa3 = _mm(o3.reshape(t, da), w_proj_attn_f, "nn", name="proj_attn").reshape(bl, s, d)
    merged = _merge_fwd(mg3, pr3, pa3)
    mo3 = _mm(merged.reshape(t, d), w_out_f, "nn", name="mix_out").reshape(bl, s, d)
    x1, h2 = _resid_norm_fwd(x, mo3, gt1, w2['g_norm2'], sh2, sc2)
    h2f = h2.reshape(t, d)
    up3 = _mm(h2f, gathered['w_ffn_up'], "nn", name="ffn_up", b_chunk=up_chunk).reshape(bl, s, 2 * dff)
    act3 = _ffn_act_fwd(up3, conv_full['w_ffn_conv'], w2['b_ffn_conv'])
    yf3 = _mm(act3.reshape(t, dff), w_ffn_down_f, "nn", name="ffn_down").reshape(bl, s, d)

    dx2, dshf, dscf, dg_final, loss_part = _final_fwd_bwd(x1, yf3, gt2, w2['g_final'], shf, scf, loss_target)
    loss = lax.psum(loss_part[0, 0], ("x", "y", "c"))

    dyf, dgt2 = _gate_bwd(dx2, yf3, gt2, "ffn_gate_bwd")
    dyf_f = dyf.reshape(t, d)
    g_ffn_down = _mm(act3.reshape(t, dff), dyf_f, "tn", name="dw_ffn_down")
    dact3 = _mm(dyf_f, w_ffn_down_f, "nt", name="d_ffn_act").reshape(bl, s, dff)
    dgf, duf, g_ffn_conv, g_b_ffn_conv = _ffn_act_bwd(up3, dact3, conv_full['w_ffn_conv'], w2['b_ffn_conv'])
    dgf_f, duf_f = dgf.reshape(t, dff), duf.reshape(t, dff)
    g_ffn_up = jnp.concatenate([_mm(h2f, dgf_f, "tn", name="dw_ffn_up_gate", out_chunk=up_chunk),
                                _mm(h2f, duf_f, "tn", name="dw_ffn_up_value", out_chunk=up_chunk)], axis=0)
    dh2 = _mm([dgf_f, duf_f], gathered['w_ffn_up'], "nt", name="d_h2", b_chunk=up_chunk).reshape(bl, s, d)
    dx1, dsh2, dsc2, dg_norm2 = _norm_mod_bwd(dh2, x1, dx2, w2['g_norm2'], sc2, "norm2_bwd")

    dmo, dgt1 = _gate_bwd(dx1, mo3, gt1, "mix_gate_bwd")
    dmo_f = dmo.reshape(t, d)
    g_out = _mm(merged.reshape(t, d), dmo_f, "tn", name="dw_out")
    dm3 = _mm(dmo_f, w_out_f, "nt", name="d_merged").reshape(bl, s, d)
    dpr, dpa, dmr, dma = _merge_bwd(dm3, mg3, pr3, pa3)
    g_proj_rnn = _mm(y_rnn.reshape(t, dr), dpr.reshape(t, d), "tn", name="dw_proj_rnn")
    g_proj_attn = _mm(o3.reshape(t, da), dpa.reshape(t, d), "tn", name="dw_proj_attn")
    dyr3 = _mm(dpr.reshape(t, d), w_proj_rnn_f, "nt", name="d_y_rnn").reshape(bl, s, dr)
    do3 = _mm(dpa.reshape(t, d), w_proj_attn_f, "nt", name="d_y_attn").reshape(bl, s, da)

    dq3, dk3, dv3, dfk, dfq = _attn_bwd(q3, kv3, do3, o3, lse_row, fcol, frow, nh, scale)
    heads_last = lambda a: jnp.pad(a.reshape(bl, nh, s).transpose(0, 2, 1), ((0, 0), (0, 0), (0, LANES - nh)))
    dzf3, g_bf = _fgate_bwd(heads_last(dfk[:, :, 0]), heads_last(dfq), zf3, bf_pad)

    dxr, dgr, g_rnn_conv, g_b_rnn_conv, g_lru_a, g_b_lru_a, g_lru_i, g_b_lru_i, g_lam = _rnn_bwd(zr, hseq, dyr3, *lru)

    dz = [a.reshape(t, -1) for a in (dxr, dgr, dq3, dk3, dv3, dmr, dma)]
    dzf_f = dzf3.reshape(t, LANES)
    seg_names = ("xr", "gr", "q", "k", "v", "mr", "ma")
    g_seg = [_mm(h1f, a, "tn", name="dw_in_" + n) for n, a in zip(seg_names, dz)]
    g_in_fl = _mm(h1f, dzf_f, "tn", name="dw_in_fl")[:, :nh]
    g_in = _chunks_of_cols(g_seg[:5] + [g_in_fl] + g_seg[5:], N_CHIPS)
    dh1 = _mm(dzf_f, w_fl, "nt", name="d_h1_fl")
    dh1 = _mm(dz, w_main, "nt", name="d_h1", add=dh1, tk=256).reshape(bl, s, d)
    grad_x, dsh1, dsc1, dg_norm1 = _norm_mod_bwd(dh1, x, dx1, w2['g_norm1'], sc1, "norm1_bwd")

    pad_rows = lambda a: jnp.pad(a.reshape(bl, -1), ((0, 16 - bl), (0, 0)))
    dmod = pad_rows(jnp.concatenate([dsh1, dsc1, dgt1, dsh2, dsc2, dgt2], axis=-1))
    dmodf = pad_rows(jnp.concatenate([dshf, dscf], axis=-1))
    g_ada = _mm(c_act, dmod, "tn", name="dw_ada", out_chunk=ada_chunk)
    g_ada_final = _mm(c_act, dmodf, "tn", name="dw_ada_final", out_chunk=adaf_chunk)
    g_b_ada = _rowsum(dmod, "db_ada")
    g_b_ada_final = _rowsum(dmodf, "db_ada_final")

    rowchunks = lambda g: g.reshape(N_CHIPS, g.shape[0] // N_CHIPS, g.shape[1])
    full = dict(w_ada=g_ada, w_in=g_in, w_proj_rnn=rowchunks(g_proj_rnn), w_proj_attn=rowchunks(g_proj_attn),
                w_out=rowchunks(g_out), w_ffn_up=g_ffn_up, w_ffn_down=rowchunks(g_ffn_down), w_ada_final=g_ada_final)
    small = dict(b_ada=g_b_ada, g_norm1=dg_norm1, w_rnn_conv=g_rnn_conv, b_rnn_conv=g_b_rnn_conv, w_lru_a=g_lru_a,
                 b_lru_a=g_b_lru_a, w_lru_i=g_lru_i, b_lru_i=g_b_lru_i, lru_lambda=g_lam, b_fgate=g_bf[:, :nh],
                 g_norm2=dg_norm2, w_ffn_conv=g_ffn_conv, b_ffn_conv=g_b_ffn_conv, b_ada_final=g_b_ada_final,
                 g_final=dg_final)

    small_names = _REPLICATED + _CONV
    n_small = sum(small[n].size for n in small_names)
    rows_q = -(-n_small // (N_CHIPS * FLAT_COLS * 32 * RS_STEPS)) * 32 * RS_STEPS
    small_flat = _flat_pad([small[n] for n in small_names], N_CHIPS * rows_q * FLAT_COLS).reshape(N_CHIPS, rows_q, FLAT_COLS)
    reduced = _reduce_scatter_chips([full[n] for n in names] + [small_flat])
    grad = dict(zip(names, reduced[:-1]))
    small_all = _all_gather_chips([reduced[-1]], "ag_small_grads")[0].reshape(-1)
    grad.update(zip(small_names, _split_flat(small_all, [small[n].shape for n in small_names])))
    for n in _CONV:
        n4 = w2[n].shape[1]
        grad[n] = lax.dynamic_slice_in_dim(grad[n], chip * n4, n4, axis=1)

    delta_w, new_m, new_v = {}, {}, {}
    for n in names:
        delta_w[n], new_m[n], new_v[n] = _adamw(grad[n], w2[n], m2[n], v2[n], "adamw_" + n)
    rows_small = -(-sum(w2[n].size for n in small_names) // (FLAT_COLS * SUBLANES)) * SUBLANES
    flat_small = lambda src: _flat_pad([src[n] for n in small_names], rows_small * FLAT_COLS).reshape(rows_small, FLAT_COLS)
    small_out = _adamw(flat_small(grad), flat_small(w2), flat_small(m2), flat_small(v2), "adamw_small")
    for dst, flat in zip((delta_w, new_m, new_v), small_out):
        dst.update(zip(small_names, _split_flat(flat.reshape(-1), [w2[n].shape for n in small_names])))

    out = [loss, grad_x]
    for src in (grad, delta_w, new_m, new_v):
        out += [src[n].reshape(shape_of[n]) for n in _WEIGHTS]
    return tuple(out)
```

```python
import functools
import math

import jax
import jax.numpy as jnp
from jax import lax
from jax.experimental import pallas as pl
from jax.experimental.pallas import tpu as pltpu

F32 = jnp.float32
BF16 = jnp.bfloat16
MESH = pl.DeviceIdType.MESH

RMS_EPS = 1e-6
LRU_C = 8.0
ADAM_LR = 0.001
ADAM_B1 = 0.9
ADAM_B2 = 0.999
ADAM_EPS = 1e-08
ADAM_WD = 0.01
ADAM_STEP = 10

LANES = 128
SUBLANES = 8
N_CHIPS = 4
FLAT_COLS = 1024
SCAN_SEGMENTS = SUBLANES
VMEM_LIMIT = 48 * 1024 * 1024
NEG_BIG = -1e30


def _cp(n_axes):
    return pltpu.CompilerParams(dimension_semantics=("arbitrary",) * n_axes, vmem_limit_bytes=VMEM_LIMIT)


def _tile(n, target, align):
    if n <= target:
        return n
    t = (target // align) * align
    while t >= align:
        if n % t == 0:
            return t
        t -= align
    return n


def _nice_rows(n, align):
    r = -(-n // align) * align
    while True:
        if r <= 640:
            return r, r
        t = _tile(r, 640, align)
        if 128 <= t <= 640:
            return r, t
        r += align


def _sigmoid(x):
    return jax.nn.sigmoid(x)


def _softplus(x):
    return jnp.maximum(x, 0.0) + jnp.log1p(jnp.exp(-jnp.abs(x)))


def _expm1(x, exp_x):
    small = x * (1.0 + 0.5 * x * (1.0 + (1.0 / 3.0) * x * (1.0 + 0.25 * x)))
    return jnp.where(jnp.abs(x) < 0.05, small, exp_x - 1.0)


_GELU_K = math.sqrt(2.0 / math.pi)
_GELU_C = 0.044715


def _gelu(x):
    t = jnp.tanh(_GELU_K * (x + _GELU_C * x * x * x))
    return 0.5 * x * (1.0 + t)


def _gelu_and_grad(x):
    t = jnp.tanh(_GELU_K * (x + _GELU_C * x * x * x))
    g = 0.5 * x * (1.0 + t)
    dg = 0.5 * (1.0 + t) + 0.5 * x * (1.0 - t * t) * _GELU_K * (1.0 + 3.0 * _GELU_C * x * x)
    return g, dg


def _shift_down(x, k):
    if k == 0:
        return x
    rows = lax.broadcasted_iota(jnp.int32, x.shape, 0)
    return jnp.where(rows >= k, pltpu.roll(x, k, 0), 0.0)


def _shift_up(x, k):
    if k == 0:
        return x
    s = x.shape[0]
    rows = lax.broadcasted_iota(jnp.int32, x.shape, 0)
    return jnp.where(rows < s - k, pltpu.roll(x, s - k, 0), 0.0)


def _dot(a, b, dims):
    return lax.dot_general(a.astype(BF16), b.astype(BF16), (dims, ((), ())), preferred_element_type=F32)


_NN = ((1,), (0,))
_NT = ((1,), (1,))
_TN = ((0,), (0,))


def _mm(a, b, mode, *, name, out_dtype=F32, scale=None, bias=None, add=None, tm=1024, tn=1024, tk=1024,
        b_chunk=None, out_chunk=None):
    pieces = list(a) if isinstance(a, (list, tuple)) else [a]
    ksize = lambda p: p.shape[0] if mode == "tn" else p.shape[1]
    if b_chunk is None:
        brows, bcols = b.shape
    else:
        brows, bcols = b.shape[1], b.shape[0] * b_chunk
    k = sum(ksize(p) for p in pieces)
    if mode == "nt":
        m, n = pieces[0].shape[0], brows
        assert bcols == k, (bcols, k)
    else:
        m, n = (pieces[0].shape[1] if mode == "tn" else pieces[0].shape[0]), bcols
        assert brows == k, (brows, k)
    tm = _tile(m, tm, LANES)
    ncut = n
    if b_chunk is not None and mode != "nt":
        ncut = b_chunk
    if out_chunk is not None:
        ncut = math.gcd(ncut, out_chunk)
    tn = _tile(ncut, tn, LANES)
    kcut = b_chunk if (b_chunk is not None and mode == "nt") else k
    for p in pieces:
        kcut = math.gcd(kcut, ksize(p))
    tk = _tile(kcut, tk, LANES)
    nk = k // tk
    dims = {"nn": _NN, "nt": _NT, "tn": _TN}[mode]
    counts = [ksize(p) // tk for p in pieces]
    starts = [sum(counts[:i]) for i in range(len(pieces))]
    n_pieces = len(pieces)

    def a_spec(s0, cnt):
        kmap = (lambda kk: kk) if n_pieces == 1 else (lambda kk: jnp.clip(kk - s0, 0, cnt - 1))
        if mode == "tn":
            return pl.BlockSpec((tk, tm), lambda i, j, kk: (kmap(kk), i))
        return pl.BlockSpec((tm, tk), lambda i, j, kk: (i, kmap(kk)))

    if b_chunk is None:
        if mode == "nt":
            b_spec = pl.BlockSpec((tn, tk), lambda i, j, kk: (j, kk))
        else:
            b_spec = pl.BlockSpec((tk, tn), lambda i, j, kk: (kk, j))
    elif mode == "nt":
        per_b = b_chunk // tk
        b_spec = pl.BlockSpec((None, tn, tk), lambda i, j, kk: (kk // per_b, j, kk % per_b))
    else:
        per_b = b_chunk // tn
        b_spec = pl.BlockSpec((None, tk, tn), lambda i, j, kk: (j // per_b, kk, j % per_b))
    if out_chunk is None:
        out_spec = pl.BlockSpec((tm, tn), lambda i, j, kk: (i, j))
        out_shape = jax.ShapeDtypeStruct((m, n), out_dtype)
    else:
        per_o = out_chunk // tn
        out_spec = pl.BlockSpec((None, tm, tn), lambda i, j, kk: (j // per_o, i, j % per_o))
        out_shape = jax.ShapeDtypeStruct((n // out_chunk, m, out_chunk), out_dtype)
    in_specs = [a_spec(s0, cnt) for s0, cnt in zip(starts, counts)] + [b_spec]
    args = pieces + [b]
    if bias is not None:
        in_specs.append(pl.BlockSpec((1, tn), lambda i, j, kk: (0, j)))
        args.append(bias)
    if add is not None:
        in_specs.append(pl.BlockSpec((tm, tn), lambda i, j, kk: (i, j)))
        args.append(add)

    def kern(*refs):
        b_ref = refs[n_pieces]
        o_ref = refs[n_pieces + 1 + (bias is not None) + (add is not None)]

        def finish(r):
            if scale is not None:
                r = r * scale
            pos = n_pieces + 1
            if bias is not None:
                r = r + refs[pos][...]
                pos += 1
            if add is not None:
                r = r + refs[pos][...]
            o_ref[...] = r.astype(out_dtype)

        if nk == 1:
            finish(_dot(refs[0][...], b_ref[...], dims))
            return
        acc = refs[-1]
        kk = pl.program_id(2)

        @pl.when(kk == 0)
        def _():
            acc[...] = jnp.zeros_like(acc)

        if n_pieces == 1:
            acc[...] += _dot(refs[0][...], b_ref[...], dims)
        else:
            for idx in range(n_pieces):
                @pl.when((kk >= starts[idx]) & (kk < starts[idx] + counts[idx]))
                def _(idx=idx):
                    acc[...] += _dot(refs[idx][...], b_ref[...], dims)

        @pl.when(kk == nk - 1)
        def _():
            finish(acc[...])

    return pl.pallas_call(
        kern, name=name,
        grid=(m // tm, n // tn, nk),
        in_specs=in_specs, out_specs=out_spec, out_shape=out_shape,
        scratch_shapes=[pltpu.VMEM((tm, tn), F32)] if nk > 1 else [],
        compiler_params=_cp(3),
    )(*args)


def _silu_pad(c, rows):
    bl, d = c.shape

    def kern(c_ref, o_ref):
        o_ref[...] = jnp.zeros_like(o_ref)
        v = c_ref[...]
        o_ref[0:bl, :] = v * _sigmoid(v)

    return pl.pallas_call(kern, name="silu_pad", out_shape=jax.ShapeDtypeStruct((rows, d), F32))(c)


def _rowsum(x, name):
    r, n = x.shape

    def kern(x_ref, o_ref):
        o_ref[...] = jnp.sum(x_ref[...], axis=0, keepdims=True)

    return pl.pallas_call(kern, name=name, out_shape=jax.ShapeDtypeStruct((1, n), F32))(x)


def _norm_parts(x, g):
    r = lax.rsqrt(jnp.mean(x * x, axis=-1, keepdims=True) + RMS_EPS)
    xh = x * r
    return r, xh, xh * g


def _norm_bwd_parts(dh, xh, r, g, sc):
    n = xh * g
    dn = dh * (1.0 + sc)
    dxh = dn * g
    dx = r * (dxh - xh * jnp.mean(dxh * xh, axis=-1, keepdims=True))
    return dx, dh, dh * n, dn * xh


def _act_specs(ts, d, n):
    return [pl.BlockSpec((1, ts, d), lambda b, t: (b, t, 0)) for _ in range(n)]


def _vec_spec(d):
    return pl.BlockSpec((1, 1, d), lambda b, t: (b, 0, 0))


def _par_spec(d):
    return pl.BlockSpec((1, d), lambda b, t: (0, 0))


def _norm_mod_fwd(x3, g, sh, sc):
    bl, s, d = x3.shape
    ts = _tile(s, 512, SUBLANES)

    def kern(x_ref, g_ref, sh_ref, sc_ref, h_ref):
        _, _, n = _norm_parts(x_ref[0], g_ref[...])
        h_ref[0] = (n * (1.0 + sc_ref[0]) + sh_ref[0]).astype(BF16)

    return pl.pallas_call(
        kern, name="norm_mod_fwd", grid=(bl, s // ts),
        in_specs=_act_specs(ts, d, 1) + [_par_spec(d), _vec_spec(d), _vec_spec(d)],
        out_specs=_act_specs(ts, d, 1)[0],
        out_shape=jax.ShapeDtypeStruct((bl, s, d), BF16),
        compiler_params=_cp(2),
    )(x3, g, sh, sc)


def _resid_norm_fwd(x3, y3, gate, g, sh, sc):
    bl, s, d = x3.shape
    ts = _tile(s, 512, SUBLANES)

    def kern(x_ref, y_ref, gate_ref, g_ref, sh_ref, sc_ref, x1_ref, h_ref):
        x1 = x_ref[0] + gate_ref[0] * y_ref[0]
        x1_ref[0] = x1
        _, _, n = _norm_parts(x1, g_ref[...])
        h_ref[0] = (n * (1.0 + sc_ref[0]) + sh_ref[0]).astype(BF16)

    return pl.pallas_call(
        kern, name="resid_norm_fwd", grid=(bl, s // ts),
        in_specs=_act_specs(ts, d, 2) + [_vec_spec(d), _par_spec(d), _vec_spec(d), _vec_spec(d)],
        out_specs=_act_specs(ts, d, 2),
        out_shape=[jax.ShapeDtypeStruct((bl, s, d), F32), jax.ShapeDtypeStruct((bl, s, d), BF16)],
        compiler_params=_cp(2),
    )(x3, y3, gate, g, sh, sc)


def _norm_mod_bwd(dh3, x3, dres3, g, sc, name):
    bl, s, d = x3.shape
    ts = _tile(s, 512, SUBLANES)

    def kern(dh_ref, x_ref, dres_ref, g_ref, sc_ref, dx_ref, dsh_ref, dsc_ref, dg_ref):
        b, t = pl.program_id(0), pl.program_id(1)
        gv = g_ref[...]
        r, xh, _ = _norm_parts(x_ref[0], gv)
        dx, a, bb, cc = _norm_bwd_parts(dh_ref[0], xh, r, gv, sc_ref[0])
        dx_ref[0] = dres_ref[0] + dx

        @pl.when(t == 0)
        def _():
            dsh_ref[...] = jnp.zeros_like(dsh_ref)
            dsc_ref[...] = jnp.zeros_like(dsc_ref)

        @pl.when((t == 0) & (b == 0))
        def _():
            dg_ref[...] = jnp.zeros_like(dg_ref)

        dsh_ref[0] += jnp.sum(a, axis=0, keepdims=True)
        dsc_ref[0] += jnp.sum(bb, axis=0, keepdims=True)
        dg_ref[...] += jnp.sum(cc, axis=0, keepdims=True)

    return pl.pallas_call(
        kern, name=name, grid=(bl, s // ts),
        in_specs=_act_specs(ts, d, 3) + [_par_spec(d), _vec_spec(d)],
        out_specs=[_act_specs(ts, d, 1)[0], _vec_spec(d), _vec_spec(d), _par_spec(d)],
        out_shape=[jax.ShapeDtypeStruct((bl, s, d), F32), jax.ShapeDtypeStruct((bl, 1, d), F32),
                   jax.ShapeDtypeStruct((bl, 1, d), F32), jax.ShapeDtypeStruct((1, d), F32)],
        compiler_params=_cp(2),
    )(dh3, x3, dres3, g, sc)


def _gate_bwd(dx3, y3, gate, name):
    bl, s, d = dx3.shape
    ts = _tile(s, 512, SUBLANES)

    def kern(dx_ref, y_ref, gate_ref, dy_ref, dgate_ref):
        t = pl.program_id(1)
        dx = dx_ref[0]
        dy_ref[0] = (gate_ref[0] * dx).astype(BF16)

        @pl.when(t == 0)
        def _():
            dgate_ref[...] = jnp.zeros_like(dgate_ref)

        dgate_ref[0] += jnp.sum(dx * y_ref[0], axis=0, keepdims=True)

    return pl.pallas_call(
        kern, name=name, grid=(bl, s // ts),
        in_specs=_act_specs(ts, d, 2) + [_vec_spec(d)],
        out_specs=[_act_specs(ts, d, 1)[0], _vec_spec(d)],
        out_shape=[jax.ShapeDtypeStruct((bl, s, d), BF16), jax.ShapeDtypeStruct((bl, 1, d), F32)],
        compiler_params=_cp(2),
    )(dx3, y3, gate)


def _final_fwd_bwd(x1, yf, gate2, g, shf, scf, tgt):
    bl, s, d = x1.shape
    ts = _tile(s, 512, SUBLANES)

    def kern(x1_ref, yf_ref, gate_ref, g_ref, sh_ref, sc_ref, tgt_ref, dx_ref, dsh_ref, dsc_ref, dg_ref, loss_ref):
        b, t = pl.program_id(0), pl.program_id(1)
        gv, sc = g_ref[...], sc_ref[0]
        x2 = x1_ref[0] + gate_ref[0] * yf_ref[0]
        r, xh, n = _norm_parts(x2, gv)
        err = n * (1.0 + sc) + sh_ref[0] - tgt_ref[0]
        dx, a, bb, cc = _norm_bwd_parts(err * (1.0 / d), xh, r, gv, sc)
        dx_ref[0] = dx

        @pl.when(t == 0)
        def _():
            dsh_ref[...] = jnp.zeros_like(dsh_ref)
            dsc_ref[...] = jnp.zeros_like(dsc_ref)

        @pl.when((t == 0) & (b == 0))
        def _():
            dg_ref[...] = jnp.zeros_like(dg_ref)
            loss_ref[...] = jnp.zeros_like(loss_ref)

        dsh_ref[0] += jnp.sum(a, axis=0, keepdims=True)
        dsc_ref[0] += jnp.sum(bb, axis=0, keepdims=True)
        dg_ref[...] += jnp.sum(cc, axis=0, keepdims=True)
        tok = jnp.mean(err * err, axis=-1, keepdims=True)
        loss_ref[...] += 0.5 * jnp.sum(tok, axis=0, keepdims=True)

    return pl.pallas_call(
        kern, name="final_fwd_bwd", grid=(bl, s // ts),
        in_specs=_act_specs(ts, d, 2) + [_vec_spec(d), _par_spec(d), _vec_spec(d), _vec_spec(d)] + _act_specs(ts, d, 1),
        out_specs=[_act_specs(ts, d, 1)[0], _vec_spec(d), _vec_spec(d), _par_spec(d),
                   pl.BlockSpec((1, 1), lambda b, t: (0, 0))],
        out_shape=[jax.ShapeDtypeStruct((bl, s, d), F32), jax.ShapeDtypeStruct((bl, 1, d), F32),
                   jax.ShapeDtypeStruct((bl, 1, d), F32), jax.ShapeDtypeStruct((1, d), F32),
                   jax.ShapeDtypeStruct((1, 1), F32)],
        compiler_params=_cp(2),
    )(x1, yf, gate2, g, shf, scf, tgt)


def _rnn_gates(xr, cw, cb, wa, ba, wi, bi, lam):
    kw = cw.shape[0]
    xc = cb
    for k in range(kw):
        xc = xc + _shift_down(xr, kw - 1 - k) * cw[k:k + 1, :]
    r = _sigmoid(_dot(xc, wa, _NN) + ba)
    i = _sigmoid(_dot(xc, wi, _NN) + bi)
    sp = _softplus(-lam)
    log_a = -LRU_C * r * sp
    a = jnp.exp(log_a)
    mult = jnp.sqrt(-_expm1(2.0 * log_a, a * a))
    return xc, r, i, sp, a, mult


def _segment_scan(a_s, u_s, h_s, p_s, reverse):
    s, c = a_s.shape
    seg = s // SCAN_SEGMENTS

    unroll = math.gcd(seg, 8)

    def steps(n, carry):
        h, p = carry
        for j in range(unroll):
            t = n * unroll + j
            t = (seg - 1 - t) if reverse else t
            av = a_s[pl.ds(t, SCAN_SEGMENTS, stride=seg), :]
            uv = u_s[pl.ds(t, SCAN_SEGMENTS, stride=seg), :]
            h = av * h + uv
            p = p * av
            h_s[pl.ds(t, SCAN_SEGMENTS, stride=seg), :] = h
            p_s[pl.ds(t, SCAN_SEGMENTS, stride=seg), :] = p
        return h, p

    lax.fori_loop(0, seg // unroll, steps, (jnp.zeros((SCAN_SEGMENTS, c), F32), jnp.ones((SCAN_SEGMENTS, c), F32)))
    carry = jnp.zeros((1, c), F32)
    order = range(SCAN_SEGMENTS - 1, -1, -1) if reverse else range(SCAN_SEGMENTS)
    for j in order:
        rows = pl.ds(j * seg, seg)
        fixed = h_s[rows, :] + p_s[rows, :] * carry
        h_s[rows, :] = fixed
        carry = fixed[0:1, :] if reverse else fixed[seg - 1:seg, :]


def _rnn_specs(s, rb, nb):
    act = lambda off: pl.BlockSpec((1, s, rb), lambda b, n, off=off: (b, 0, off + n))
    par = pl.BlockSpec((1, rb), lambda b, n: (0, n))
    wsp = pl.BlockSpec((1, rb, rb), lambda b, n: (n, 0, 0))
    return act, par, wsp


def _rnn_fwd(zr3, cw, cb, wa, ba, wi, bi, lam):
    bl, s, two = zr3.shape
    nb, rb, _ = wa.shape
    dr = nb * rb
    kw = cw.shape[0]
    act, par, wsp = _rnn_specs(s, rb, nb)

    def kern(xr_ref, gr_ref, cw_ref, cb_ref, wa_ref, ba_ref, wi_ref, bi_ref, lam_ref, h_ref, y_ref, a_s, u_s, h_s, p_s):
        xc, r, i, sp, a, mult = _rnn_gates(xr_ref[0], cw_ref[...], cb_ref[...], wa_ref[0], ba_ref[...],
                                           wi_ref[0], bi_ref[...], lam_ref[...])
        a_s[...] = a
        u_s[...] = mult * (i * xc)
        _segment_scan(a_s, u_s, h_s, p_s, reverse=False)
        h = h_s[...]
        h_ref[0] = h
        y_ref[0] = (_gelu(gr_ref[0]) * h).astype(BF16)

    return pl.pallas_call(
        kern, name="rnn_fwd", grid=(bl, nb),
        in_specs=[act(0), act(nb), pl.BlockSpec((kw, rb), lambda b, n: (0, n)), par, wsp, par, wsp, par, par],
        out_specs=[act(0), act(0)],
        out_shape=[jax.ShapeDtypeStruct((bl, s, dr), F32), jax.ShapeDtypeStruct((bl, s, dr), BF16)],
        scratch_shapes=[pltpu.VMEM((s, rb), F32)] * 4,
        compiler_params=_cp(2),
    )(zr3, zr3, cw, cb, wa, ba, wi, bi, lam)


def _rnn_bwd(zr3, h3, dy3, cw, cb, wa, ba, wi, bi, lam):
    bl, s, _ = zr3.shape
    nb, rb, _ = wa.shape
    dr = nb * rb
    kw = cw.shape[0]
    act = lambda off: pl.BlockSpec((1, s, rb), lambda n, b, off=off: (b, 0, off + n))
    par = pl.BlockSpec((1, rb), lambda n, b: (0, n))
    wsp = pl.BlockSpec((1, rb, rb), lambda n, b: (n, 0, 0))
    cws = pl.BlockSpec((kw, rb), lambda n, b: (0, n))

    def kern(xr_ref, gr_ref, h_ref, dy_ref, cw_ref, cb_ref, wa_ref, ba_ref, wi_ref, bi_ref, lam_ref,
             dxr_ref, dgr_ref, dcw_ref, dcb_ref, dwa_ref, dba_ref, dwi_ref, dbi_ref, dlam_ref, a_s, u_s, h_s, p_s):
        b = pl.program_id(1)
        xr, cwv, lamv = xr_ref[0], cw_ref[...], lam_ref[...]
        wav, wiv = wa_ref[0], wi_ref[0]
        xc, r, i, sp, a, mult = _rnn_gates(xr, cwv, cb_ref[...], wav, ba_ref[...], wiv, bi_ref[...], lamv)
        h, dy = h_ref[0], dy_ref[0]
        ge, dge = _gelu_and_grad(gr_ref[0])
        dgr_ref[0] = (dy * h * dge).astype(BF16)
        a_s[...] = _shift_up(a, 1)
        u_s[...] = dy * ge
        _segment_scan(a_s, u_s, h_s, p_s, reverse=True)
        g = h_s[...]
        da = g * _shift_down(h, 1)
        ix = i * xc
        dlog_a = da * a + (g * ix) * (-(a * a) / mult)
        di = g * mult * xc
        dpa = (dlog_a * (-LRU_C * sp)) * r * (1.0 - r)
        dpi = di * i * (1.0 - i)
        dxc = g * mult * i + _dot(dpa, wav, _NT) + _dot(dpi, wiv, _NT)
        dxr = jnp.zeros_like(dxc)
        dcw_rows = []
        for k in range(kw):
            dxr = dxr + _shift_up(dxc, kw - 1 - k) * cwv[k:k + 1, :]
            dcw_rows.append(jnp.sum(dxc * _shift_down(xr, kw - 1 - k), axis=0, keepdims=True))
        dxr_ref[0] = dxr.astype(BF16)

        @pl.when(b == 0)
        def _():
            for ref in (dcw_ref, dcb_ref, dwa_ref, dba_ref, dwi_ref, dbi_ref, dlam_ref):
                ref[...] = jnp.zeros_like(ref)

        for k in range(kw):
            dcw_ref[k:k + 1, :] += dcw_rows[k]
        dcb_ref[...] += jnp.sum(dxc, axis=0, keepdims=True)
        dwa_ref[0] += _dot(xc, dpa, _TN)
        dwi_ref[0] += _dot(xc, dpi, _TN)
        dba_ref[...] += jnp.sum(dpa, axis=0, keepdims=True)
        dbi_ref[...] += jnp.sum(dpi, axis=0, keepdims=True)
        dsp = jnp.sum(dlog_a * (-LRU_C * r), axis=0, keepdims=True)
        dlam_ref[...] += dsp * (-_sigmoid(-lamv))

    vec = jax.ShapeDtypeStruct((1, dr), F32)
    wsh = jax.ShapeDtypeStruct((nb, rb, rb), F32)
    return pl.pallas_call(
        kern, name="rnn_bwd", grid=(nb, bl),
        in_specs=[act(0), act(nb), act(0), act(0), cws, par, wsp, par, wsp, par, par],
        out_specs=[act(0), act(0), cws, par, wsp, par, wsp, par, par],
        out_shape=[jax.ShapeDtypeStruct((bl, s, dr), BF16), jax.ShapeDtypeStruct((bl, s, dr), BF16),
                   jax.ShapeDtypeStruct((kw, dr), F32), vec, wsh, vec, wsh, vec, vec],
        scratch_shapes=[pltpu.VMEM((s, rb), F32)] * 4,
        compiler_params=_cp(2),
    )(zr3, zr3, h3, dy3, cw, cb, wa, ba, wi, bi, lam)


def _tri(n, upper):
    r = lax.broadcasted_iota(jnp.int32, (n, n), 0)
    c = lax.broadcasted_iota(jnp.int32, (n, n), 1)
    return jnp.where((c >= r) if upper else (c <= r), 1.0, 0.0).astype(F32)


def _fgate_fwd(zf3, bf):
    bl, s, w = zf3.shape
    ch = _tile(s, 256, SUBLANES)

    def kern(z_ref, b_ref, f_ref):
        tri = _tri(ch, upper=False)
        carry = jnp.zeros((1, w), F32)
        for j in range(s // ch):
            rows = pl.ds(j * ch, ch)
            lf = -_softplus(-(z_ref[0, rows, :] + b_ref[...]))
            out = jnp.dot(tri, lf, precision=lax.Precision.HIGHEST, preferred_element_type=F32) + carry
            f_ref[0, rows, :] = out
            carry = out[ch - 1:ch, :]

    return pl.pallas_call(
        kern, name="fgate_fwd", grid=(bl,),
        in_specs=[pl.BlockSpec((1, s, w), lambda b: (b, 0, 0)), pl.BlockSpec((1, w), lambda b: (0, 0))],
        out_specs=pl.BlockSpec((1, s, w), lambda b: (b, 0, 0)),
        out_shape=jax.ShapeDtypeStruct((bl, s, w), F32),
        compiler_params=_cp(1),
    )(zf3, bf)


def _fgate_bwd(dfk3, dfq3, zf3, bf):
    bl, s, w = zf3.shape
    ch = _tile(s, 256, SUBLANES)

    def kern(dfk_ref, dfq_ref, z_ref, b_ref, dz_ref, db_ref):
        b = pl.program_id(0)
        tri = _tri(ch, upper=True)
        carry = jnp.zeros((1, w), F32)
        dbsum = jnp.zeros((1, w), F32)
        for j in range(s // ch - 1, -1, -1):
            rows = pl.ds(j * ch, ch)
            df = dfk_ref[0, rows, :] + dfq_ref[0, rows, :]
            dlf = jnp.dot(tri, df, precision=lax.Precision.HIGHEST, preferred_element_type=F32) + carry
            carry = dlf[0:1, :]
            dz = dlf * _sigmoid(-(z_ref[0, rows, :] + b_ref[...]))
            dz_ref[0, rows, :] = dz.astype(BF16)
            dbsum = dbsum + jnp.sum(dz, axis=0, keepdims=True)

        @pl.when(b == 0)
        def _():
            db_ref[...] = jnp.zeros_like(db_ref)

        db_ref[...] += dbsum

    return pl.pallas_call(
        kern, name="fgate_bwd", grid=(bl,),
        in_specs=[pl.BlockSpec((1, s, w), lambda b: (b, 0, 0))] * 3 + [pl.BlockSpec((1, w), lambda b: (0, 0))],
        out_specs=[pl.BlockSpec((1, s, w), lambda b: (b, 0, 0)), pl.BlockSpec((1, w), lambda b: (0, 0))],
        out_shape=[jax.ShapeDtypeStruct((bl, s, w), BF16), jax.ShapeDtypeStruct((1, w), F32)],
        compiler_params=_cp(1),
    )(dfk3, dfq3, zf3, bf)


def _lanes(col, width):
    return col if width == LANES else jnp.concatenate([col] * (width // LANES), axis=1)


def _causal(sc, row0, col0, transposed):
    r = lax.broadcasted_iota(jnp.int32, sc.shape, 0) + row0
    c = lax.broadcasted_iota(jnp.int32, sc.shape, 1) + col0
    return jnp.where((c >= r) if transposed else (r >= c), sc, NEG_BIG)


def _attn_fwd(q3, kv3, fcol, frow, nh):
    bl, s, da = q3.shape
    dh = da // nh
    tq = _tile(s, 512, LANES)
    nq = s // tq

    def kern(q_ref, k_ref, v_ref, fk_ref, fq_ref, o_ref, lse_ref, m_s, l_s, acc):
        iq, ik = pl.program_id(2), pl.program_id(3)

        @pl.when(ik == 0)
        def _():
            m_s[...] = jnp.full_like(m_s, NEG_BIG)
            l_s[...] = jnp.zeros_like(l_s)
            acc[...] = jnp.zeros_like(acc)

        def block(masked):
            st = _dot(k_ref[0], q_ref[0], _NT) - _lanes(fk_ref[0], tq) + fq_ref[0]
            if masked:
                st = _causal(st, ik * tq, iq * tq, True)
            m_old = m_s[...]
            m_new = jnp.maximum(m_old, jnp.max(st, axis=0, keepdims=True))
            alpha = jnp.exp(m_old - m_new)
            pt = jnp.exp(st - m_new)
            l_s[...] = alpha * l_s[...] + jnp.sum(pt, axis=0, keepdims=True)
            acc[...] = alpha * acc[...] + _dot(v_ref[0], pt, _TN)
            m_s[...] = m_new

        pl.when(ik < iq)(functools.partial(block, False))
        pl.when(ik == iq)(functools.partial(block, True))

        @pl.when(ik == nq - 1)
        def _():
            l = l_s[...]
            o_ref[0] = (acc[...] / l).T
            lse_ref[0] = m_s[...] + jnp.log(l)

    kmap = lambda off: (lambda b, h, iq, ik: (b, jnp.minimum(ik, iq), off + h))
    return pl.pallas_call(
        kern, name="attn_fwd", grid=(bl, nh, nq, nq),
        in_specs=[pl.BlockSpec((1, tq, dh), lambda b, h, iq, ik: (b, iq, h)),
                  pl.BlockSpec((1, tq, dh), kmap(0)), pl.BlockSpec((1, tq, dh), kmap(nh)),
                  pl.BlockSpec((1, tq, LANES), lambda b, h, iq, ik: (b * nh + h, jnp.minimum(ik, iq), 0)),
                  pl.BlockSpec((1, 1, tq), lambda b, h, iq, ik: (b * nh + h, 0, iq))],
        out_specs=[pl.BlockSpec((1, tq, dh), lambda b, h, iq, ik: (b, iq, h)),
                   pl.BlockSpec((1, 1, tq), lambda b, h, iq, ik: (b * nh + h, 0, iq))],
        out_shape=[jax.ShapeDtypeStruct((bl, s, da), F32), jax.ShapeDtypeStruct((bl * nh, 1, s), F32)],
        scratch_shapes=[pltpu.VMEM((1, tq), F32), pltpu.VMEM((1, tq), F32), pltpu.VMEM((dh, tq), F32)],
        compiler_params=_cp(4),
    )(q3, kv3, kv3, fcol, frow)


def _attn_bwd(q3, kv3, do3, o3, lse_row, fcol, frow, nh, scale):
    bl, s, da = q3.shape
    dh = da // nh
    tk = _tile(s, 512, LANES)
    nk = s // tk

    def kern(q_ref, k_ref, v_ref, do_ref, o_ref, lse_ref, fk_ref, fq_ref, dq_ref, dk_ref, dv_ref, dfk_ref, dfq_ref,
             dq_acc, dk_acc, dv_acc, dfq_acc, delta_s):
        ik, iq = pl.program_id(2), pl.program_id(3)
        qrow = pl.ds(iq, 1)

        @pl.when((ik == 0) & (iq == 0))
        def _():
            dq_acc[...] = jnp.zeros_like(dq_acc)
            dfq_acc[...] = jnp.zeros_like(dfq_acc)

        @pl.when(ik == 0)
        def _():
            prod = do_ref[0] * o_ref[0]
            rows = lax.dot_general(jnp.ones((SUBLANES, dh), F32), prod, (_NT, ((), ())),
                                   precision=lax.Precision.HIGHEST, preferred_element_type=F32)
            delta_s[qrow, :] = rows[0:1, :]

        @pl.when(iq == 0)
        def _():
            dk_acc[...] = jnp.zeros_like(dk_acc)
            dv_acc[...] = jnp.zeros_like(dv_acc)

        def block(masked):
            q = q_ref[0]
            st = _dot(k_ref[0], q, _NT) - _lanes(fk_ref[0], tk) + fq_ref[0]
            if masked:
                st = _causal(st, ik * tk, iq * tk, True)
            pt = jnp.exp(st - lse_ref[0])
            dv_acc[...] += _dot(pt, do_ref[0], _NN)
            dpt = _dot(v_ref[0], do_ref[0], _NT)
            dst = (pt * (dpt - delta_s[qrow, :])).astype(BF16)
            q_ones = jnp.concatenate([q, jnp.ones_like(q)], axis=1)
            dk_acc[...] += _dot(dst, q_ones, _NN)
            qrows = pl.ds(pl.multiple_of(iq * tk, tk), tk)
            dq_acc[qrows, :] += _dot(dst, k_ref[0], _TN)
            dfq_acc[qrow, :] += jnp.sum(dst.astype(F32), axis=0, keepdims=True)

        pl.when(iq > ik)(functools.partial(block, False))
        pl.when(iq == ik)(functools.partial(block, True))

        @pl.when(iq == nk - 1)
        def _():
            ext = dk_acc[...]
            dk_ref[0] = ext[:, :dh].astype(BF16)
            dfk_ref[0] = -ext[:, dh:]
            dv_ref[0] = dv_acc[...].astype(BF16)

        @pl.when((ik == nk - 1) & (iq == nk - 1))
        def _():
            dq_ref[0] = (dq_acc[...] * scale).astype(BF16)
            dfq_ref[0] = dfq_acc[...]

    later = lambda ik, iq: jnp.maximum(iq, ik)
    qmap = lambda b, h, ik, iq: (b, later(ik, iq), h)
    omap = lambda b, h, ik, iq: (b, jnp.where(ik == 0, iq, 0), h)
    rmap = lambda b, h, ik, iq: (b * nh + h, 0, later(ik, iq))
    kmap = lambda off: (lambda b, h, ik, iq: (b, ik, off + h))
    bmap = lambda b, h, ik, iq: (b * nh + h, ik, 0)
    return pl.pallas_call(
        kern, name="attn_bwd", grid=(bl, nh, nk, nk),
        in_specs=[pl.BlockSpec((1, tk, dh), qmap), pl.BlockSpec((1, tk, dh), kmap(0)), pl.BlockSpec((1, tk, dh), kmap(nh)),
                  pl.BlockSpec((1, tk, dh), qmap), pl.BlockSpec((1, tk, dh), omap), pl.BlockSpec((1, 1, tk), rmap),
                  pl.BlockSpec((1, tk, LANES), bmap), pl.BlockSpec((1, 1, tk), rmap)],
        out_specs=[pl.BlockSpec((1, s, dh), lambda b, h, ik, iq: (b, 0, h)),
                   pl.BlockSpec((1, tk, dh), kmap(0)), pl.BlockSpec((1, tk, dh), kmap(0)),
                   pl.BlockSpec((1, tk, LANES), bmap),
                   pl.BlockSpec((1, nk, tk), lambda b, h, ik, iq: (b * nh + h, 0, 0))],
        out_shape=[jax.ShapeDtypeStruct((bl, s, da), BF16), jax.ShapeDtypeStruct((bl, s, da), BF16),
                   jax.ShapeDtypeStruct((bl, s, da), BF16), jax.ShapeDtypeStruct((bl * nh, s, LANES), F32),
                   jax.ShapeDtypeStruct((bl * nh, nk, tk), F32)],
        scratch_shapes=[pltpu.VMEM((s, dh), F32), pltpu.VMEM((tk, 2 * dh), F32), pltpu.VMEM((tk, dh), F32),
                        pltpu.VMEM((nk, tk), F32), pltpu.VMEM((nk, tk), F32)],
        compiler_params=_cp(4),
    )(q3, kv3, kv3, do3, o3, lse_row, fcol, frow)


def _merge_fwd(mg3, pr3, pa3):
    bl, s, d = pr3.shape
    ts = _tile(s, 256, SUBLANES)
    half = lambda j: pl.BlockSpec((1, ts, d), lambda b, t, j=j: (b, t, j))

    def kern(mr_ref, ma_ref, pr_ref, pa_ref, o_ref):
        o_ref[0] = (_sigmoid(mr_ref[0]) * pr_ref[0] + _sigmoid(ma_ref[0]) * pa_ref[0]).astype(BF16)

    return pl.pallas_call(
        kern, name="merge_fwd", grid=(bl, s // ts),
        in_specs=[half(0), half(1)] + _act_specs(ts, d, 2), out_specs=_act_specs(ts, d, 1)[0],
        out_shape=jax.ShapeDtypeStruct((bl, s, d), BF16), compiler_params=_cp(2),
    )(mg3, mg3, pr3, pa3)


def _merge_bwd(dm3, mg3, pr3, pa3):
    bl, s, d = pr3.shape
    ts = _tile(s, 256, SUBLANES)
    half = lambda j: pl.BlockSpec((1, ts, d), lambda b, t, j=j: (b, t, j))

    def kern(dm_ref, mr_ref, ma_ref, pr_ref, pa_ref, dpr_ref, dpa_ref, dmr_ref, dma_ref):
        dm = dm_ref[0]
        gr, ga = _sigmoid(mr_ref[0]), _sigmoid(ma_ref[0])
        dpr_ref[0] = (gr * dm).astype(BF16)
        dpa_ref[0] = (ga * dm).astype(BF16)
        dmr_ref[0] = (dm * pr_ref[0] * gr * (1.0 - gr)).astype(BF16)
        dma_ref[0] = (dm * pa_ref[0] * ga * (1.0 - ga)).astype(BF16)

    return pl.pallas_call(
        kern, name="merge_bwd", grid=(bl, s // ts),
        in_specs=_act_specs(ts, d, 1) + [half(0), half(1)] + _act_specs(ts, d, 2), out_specs=_act_specs(ts, d, 4),
        out_shape=[jax.ShapeDtypeStruct((bl, s, d), BF16)] * 4, compiler_params=_cp(2),
    )(dm3, mg3, mg3, pr3, pa3)


def _ffn_conv(gf, cw, cb):
    kw = cw.shape[0]
    y = cb
    for k in range(kw):
        y = y + _shift_down(gf, kw - 1 - k) * cw[k:k + 1, :]
    return y


def _ffn_act_fwd(up3, cw, cb):
    bl, s, two = up3.shape
    dff = two // 2
    kw = cw.shape[0]
    tc = _tile(dff, 256, LANES)
    nc = dff // tc

    def kern(gf_ref, uf_ref, cw_ref, cb_ref, o_ref):
        o_ref[0] = (_gelu(_ffn_conv(gf_ref[0], cw_ref[...], cb_ref[...])) * uf_ref[0]).astype(BF16)

    act = lambda off: pl.BlockSpec((1, s, tc), lambda b, j, off=off: (b, 0, off + j))
    return pl.pallas_call(
        kern, name="ffn_act_fwd", grid=(bl, nc),
        in_specs=[act(0), act(nc), pl.BlockSpec((kw, tc), lambda b, j: (0, j)), pl.BlockSpec((1, tc), lambda b, j: (0, j))],
        out_specs=act(0), out_shape=jax.ShapeDtypeStruct((bl, s, dff), BF16), compiler_params=_cp(2),
    )(up3, up3, cw, cb)


def _ffn_act_bwd(up3, dact3, cw, cb):
    bl, s, two = up3.shape
    dff = two // 2
    kw = cw.shape[0]
    tc = _tile(dff, 256, LANES)
    nc = dff // tc

    def kern(gf_ref, uf_ref, da_ref, cw_ref, cb_ref, dgf_ref, duf_ref, dcw_ref, dcb_ref):
        b = pl.program_id(1)
        gf, cwv, da = gf_ref[0], cw_ref[...], da_ref[0]
        ge, dge = _gelu_and_grad(_ffn_conv(gf, cwv, cb_ref[...]))
        duf_ref[0] = (da * ge).astype(BF16)
        dgc = da * uf_ref[0] * dge
        dgf = jnp.zeros_like(dgc)
        rows = []
        for k in range(kw):
            dgf = dgf + _shift_up(dgc, kw - 1 - k) * cwv[k:k + 1, :]
            rows.append(jnp.sum(dgc * _shift_down(gf, kw - 1 - k), axis=0, keepdims=True))
        dgf_ref[0] = dgf.astype(BF16)

        @pl.when(b == 0)
        def _():
            dcw_ref[...] = jnp.zeros_like(dcw_ref)
            dcb_ref[...] = jnp.zeros_like(dcb_ref)

        for k in range(kw):
            dcw_ref[k:k + 1, :] += rows[k]
        dcb_ref[...] += jnp.sum(dgc, axis=0, keepdims=True)

    act = lambda off: pl.BlockSpec((1, s, tc), lambda j, b, off=off: (b, 0, off + j))
    cws = pl.BlockSpec((kw, tc), lambda j, b: (0, j))
    cbs = pl.BlockSpec((1, tc), lambda j, b: (0, j))
    return pl.pallas_call(
        kern, name="ffn_act_bwd", grid=(nc, bl),
        in_specs=[act(0), act(nc), act(0), cws, cbs], out_specs=[act(0), act(0), cws, cbs],
        out_shape=[jax.ShapeDtypeStruct((bl, s, dff), BF16), jax.ShapeDtypeStruct((bl, s, dff), BF16),
                   jax.ShapeDtypeStruct((kw, dff), F32), jax.ShapeDtypeStruct((1, dff), F32)],
        compiler_params=_cp(2),
    )(up3, up3, dact3, cw, cb)


_HBM = pl.BlockSpec(memory_space=pltpu.HBM)


def _place():
    x, y, c = lax.axis_index("x"), lax.axis_index("y"), lax.axis_index("c")
    chips = dict(me=2 * x + y, nx=2 * (1 - x) + y, ny=2 * x + (1 - y), diag=2 * (1 - x) + (1 - y))
    peers = dict(nx=(1 - x, y, c), ny=(x, 1 - y, c), sib=(x, y, 1 - c))
    return c, chips, peers


def _remote(src, dst, sems, k, to):
    return pltpu.make_async_remote_copy(src_ref=src, dst_ref=dst, send_sem=sems[0].at[k], recv_sem=sems[1].at[k],
                                        device_id=to, device_id_type=MESH)


RS_STEPS = 2


def _piece(q, idx, n=1):
    start = idx * q
    if not isinstance(start, int):
        start = pl.multiple_of(start, SUBLANES)
    return pl.ds(start, n * q)


def _all_gather_chips(xs, name):
    nt = len(xs)
    per = 9

    def body(*refs):
        x_refs, o_refs = refs[:nt], refs[nt:2 * nt]
        send_sems, recv_sems = refs[2 * nt:]
        c, chip, peer = _place()
        sems = (send_sems, recv_sems)
        me, nx, ny, dg = chip["me"], chip["nx"], chip["ny"], chip["diag"]
        sends = []

        def arrive(k, dst):
            _remote(dst, dst, sems, k, peer["sib"]).wait_recv()

        def pass_on(k, blk, to):
            cp = _remote(blk, blk, sems, k, peer[to])
            cp.start()
            sends.append(cp)

        for t in range(nt):
            q = xs[t].shape[0] // 4
            half = _piece(q, 2 * c, 2)
            for k, to in ((0, "nx"), (1, "ny")):
                cp = _remote(x_refs[t].at[half], o_refs[t].at[me, half], sems, per * t + k, peer[to])
                cp.start()
                sends.append(cp)
            cp = _remote(x_refs[t], o_refs[t].at[me], sems, per * t + 8, peer["sib"])
            cp.start()
            sends.append(cp)
        for t in range(nt):
            q, o, k0 = xs[t].shape[0] // 4, o_refs[t], per * t
            half, sub0, sub1 = _piece(q, 2 * c, 2), _piece(q, 2 * c), _piece(q, 2 * c + 1)
            arrive(k0 + 0, o.at[nx, half])
            pass_on(k0 + 2, o.at[nx, sub0], "ny")
            pass_on(k0 + 4, o.at[nx, half], "sib")
            arrive(k0 + 1, o.at[ny, half])
            pass_on(k0 + 3, o.at[ny, sub1], "nx")
            pass_on(k0 + 5, o.at[ny, half], "sib")
            arrive(k0 + 2, o.at[dg, sub0])
            pass_on(k0 + 6, o.at[dg, sub0], "sib")
            arrive(k0 + 3, o.at[dg, sub1])
            pass_on(k0 + 7, o.at[dg, sub1], "sib")
        for t in range(nt):
            q, o, k0 = xs[t].shape[0] // 4, o_refs[t], per * t
            arrive(k0 + 4, o.at[nx, _piece(q, 2 * (1 - c), 2)])
            arrive(k0 + 5, o.at[ny, _piece(q, 2 * (1 - c), 2)])
            arrive(k0 + 6, o.at[dg, _piece(q, 2 * (1 - c))])
            arrive(k0 + 7, o.at[dg, _piece(q, 2 * (1 - c) + 1)])
            arrive(k0 + 8, o.at[me])
        for cp in sends:
            cp.wait_send()

    return pl.pallas_call(
        body, name=name, in_specs=[_HBM] * nt, out_specs=[_HBM] * nt,
        out_shape=[jax.ShapeDtypeStruct((N_CHIPS,) + x.shape, x.dtype) for x in xs],
        scratch_shapes=[pltpu.SemaphoreType.DMA((per * nt,)), pltpu.SemaphoreType.DMA((per * nt,))],
    )(*xs)


def _exchange(name, xs, out_shapes, plan):
    nt = len(xs)

    def body(*refs):
        x_refs, o_refs = refs[:nt], refs[nt:2 * nt]
        send_sems, recv_sems = refs[2 * nt:]
        c, chip, peer = _place()
        cps = []
        for t in range(nt):
            for src, dst, to in plan(c, chip, x_refs[t], o_refs[t], xs[t].shape):
                cps.append(_remote(src, dst, (send_sems, recv_sems), len(cps), peer[to]))
        for cp in cps:
            cp.start()
        for cp in cps:
            cp.wait()

    n_copies = nt * len(plan(0, dict(me=0, nx=2, ny=1, diag=3), None, None, xs[0].shape, count_only=True))
    return pl.pallas_call(
        body, name=name, in_specs=[_HBM] * nt, out_specs=[_HBM] * nt,
        out_shape=[jax.ShapeDtypeStruct(s, x.dtype) for s, x in zip(out_shapes, xs)],
        scratch_shapes=[pltpu.SemaphoreType.DMA((n_copies,)), pltpu.SemaphoreType.DMA((n_copies,))],
    )(*xs)


def _plan_sibling(c, chip, g, out, shape, count_only=False):
    if count_only:
        return [None] * N_CHIPS
    q = shape[1] // 4
    return [(g.at[j, _piece(q, 2 * (1 - c), 2)], out.at[j], "sib") for j in range(N_CHIPS)]


def _plan_first(c, chip, p, out, shape, count_only=False):
    if count_only:
        return [None] * 4
    q = shape[1] // 2
    return [(p.at[chip["nx"], _piece(q, 0)], out.at[0], "nx"), (p.at[chip["diag"], _piece(q, 0)], out.at[1], "nx"),
            (p.at[chip["ny"], _piece(q, 1)], out.at[2], "ny"), (p.at[chip["diag"], _piece(q, 1)], out.at[3], "ny")]


def _plan_second(c, chip, p, out, shape, count_only=False):
    if count_only:
        return [None] * 2
    return [(p.at[1], out.at[0], "ny"), (p.at[3], out.at[1], "nx")]


def _rs_last(ps):
    nt = len(ps)

    def body(*refs):
        p_refs, o_refs = refs[:nt], refs[nt:2 * nt]
        send_sems, recv_sems = refs[2 * nt:]
        c, _, peer = _place()
        sems = (send_sems, recv_sems)
        cps = []
        for t in range(nt):
            q = ps[t].shape[0] // 4
            mine = _piece(q, 2 * c, 2)
            cps.append(_remote(p_refs[t].at[mine], o_refs[t].at[mine], sems, t, peer["sib"]))
            cps[-1].start()
        for t in range(nt):
            q = ps[t].shape[0] // 4
            theirs = _piece(q, 2 * (1 - c), 2)
            cps[t].wait_send()
            _remote(p_refs[t].at[theirs], o_refs[t].at[theirs], sems, t, peer["sib"]).wait_recv()

    return pl.pallas_call(
        body, name="rs_last", in_specs=[_HBM] * nt, out_specs=[_HBM] * nt,
        out_shape=[jax.ShapeDtypeStruct(p.shape, F32) for p in ps],
        input_output_aliases={t: t for t in range(nt)},
        scratch_shapes=[pltpu.SemaphoreType.DMA((nt,)), pltpu.SemaphoreType.DMA((nt,))],
    )(*ps)


def _add_stage(name, grid, a_list, b_list, a_map, b_map, tbs, out_shapes, out_map, out_dtype, prefetch=None):
    nt = len(a_list)
    lead = lambda shape: (None,) * (len(shape) - 2)

    def kern(*refs):
        refs = refs[(1 if prefetch is not None else 0):]
        for t in range(nt):
            refs[2 * nt + t][...] = (refs[t][...].astype(F32) + refs[nt + t][...].astype(F32)).astype(out_dtype)

    in_specs = [pl.BlockSpec(lead(a.shape) + (tb, a.shape[-1]), a_map) for a, tb in zip(a_list, tbs)]
    in_specs += [pl.BlockSpec(lead(b.shape) + (tb, b.shape[-1]), b_map) for b, tb in zip(b_list, tbs)]
    out_specs = [pl.BlockSpec(lead(s) + (tb, s[-1]), out_map) for s, tb in zip(out_shapes, tbs)]
    out_shape = [jax.ShapeDtypeStruct(s, out_dtype) for s in out_shapes]
    if prefetch is None:
        return pl.pallas_call(kern, name=name, grid=grid, in_specs=in_specs, out_specs=out_specs, out_shape=out_shape,
                              compiler_params=_cp(len(grid)))(*a_list, *b_list)
    return pl.pallas_call(
        kern, name=name,
        grid_spec=pltpu.PrefetchScalarGridSpec(num_scalar_prefetch=1, grid=grid, in_specs=in_specs, out_specs=out_specs),
        out_shape=out_shape, compiler_params=_cp(len(grid)))(prefetch, *a_list, *b_list)


def _reduce_scatter_chips(gs):
    x, y, c = lax.axis_index("x"), lax.axis_index("y"), lax.axis_index("c")
    me, nx, ny = 2 * x + y, 2 * (1 - x) + y, 2 * x + (1 - y)
    st = RS_STEPS
    unit = 4 * st * 2 * SUBLANES
    rows = [g.shape[1] for g in gs]
    gs = [jnp.pad(g, ((0, 0), (0, -g.shape[1] % unit), (0, 0))) for g in gs]
    qs = [g.shape[1] // 4 for g in gs]
    tbs = [q // st for q in qs]
    cols = [g.shape[2] for g in gs]
    core = jnp.reshape(c, (1,)).astype(jnp.int32)

    got = _exchange("rs_sibling", gs, [(N_CHIPS, 2 * q, cc) for q, cc in zip(qs, cols)], _plan_sibling)
    p0 = _add_stage("rs_add_sibling", (N_CHIPS, 2, st), gs, got,
                    lambda j, h, s, c_ref: (j, (2 * c_ref[0] + h) * st + s, 0), lambda j, h, s, c_ref: (j, h * st + s, 0),
                    tbs, [(N_CHIPS, 2 * q, cc) for q, cc in zip(qs, cols)], lambda j, h, s, c_ref: (j, h * st + s, 0),
                    BF16, prefetch=core)
    got = _exchange("rs_first", p0, [(4, q, cc) for q, cc in zip(qs, cols)], _plan_first)
    p1 = _add_stage("rs_add_first", (4, st), p0, got,
                    lambda k, s, i_ref: (i_ref[k], (k // 2) * st + s, 0), lambda k, s, i_ref: (k, s, 0),
                    tbs, [(4, q, cc) for q, cc in zip(qs, cols)], lambda k, s, i_ref: (k, s, 0),
                    BF16, prefetch=jnp.stack([me, ny, me, nx]).astype(jnp.int32))
    got = _exchange("rs_second", p1, [(2, q, cc) for q, cc in zip(qs, cols)], _plan_second)
    p2 = _add_stage("rs_add_second", (2, st), p1, got, lambda h, s, c_ref: (2 * h, s, 0), lambda h, s, c_ref: (h, s, 0),
                    tbs, [(4 * q, cc) for q, cc in zip(qs, cols)], lambda h, s, c_ref: ((2 * c_ref[0] + h) * st + s, 0),
                    F32, prefetch=core)
    return [out[:r] for out, r in zip(_rs_last(p2), rows)]


def _adamw(g, w, m, v, name):
    rows, cc = g.shape
    tr = _tile(rows, max(SUBLANES, (1 << 18) // cc), SUBLANES)
    k1 = 1.0 - ADAM_B1 ** ADAM_STEP
    k2 = 1.0 - ADAM_B2 ** ADAM_STEP

    def kern(g_ref, w_ref, m_ref, v_ref, d_ref, nm_ref, nv_ref):
        gv = g_ref[...]
        nm = ADAM_B1 * m_ref[...] + (1.0 - ADAM_B1) * gv
        nv = ADAM_B2 * v_ref[...] + (1.0 - ADAM_B2) * (gv * gv)
        nm_ref[...] = nm
        nv_ref[...] = nv
        d_ref[...] = -ADAM_LR * ((nm / k1) / (jnp.sqrt(nv / k2) + ADAM_EPS) + ADAM_WD * w_ref[...])

    spec = pl.BlockSpec((tr, cc), lambda t: (t, 0))
    return pl.pallas_call(
        kern, name=name, grid=(rows // tr,), in_specs=[spec] * 4, out_specs=[spec] * 3,
        out_shape=[jax.ShapeDtypeStruct((rows, cc), F32)] * 3, compiler_params=_cp(1),
    )(g, w, m, v)


def _flat_pad(parts, total):
    flat = jnp.concatenate([p.reshape(-1) for p in parts])
    return jnp.pad(flat, (0, total - flat.shape[0]))


def _split_flat(flat, shapes):
    out, pos = [], 0
    for shp in shapes:
        size = math.prod(shp)
        out.append(flat[pos:pos + size].reshape(shp))
        pos += size
    return out


def _cols_of_chunks(chunks, lo, hi):
    width = chunks.shape[2]
    parts = []
    for j in range(chunks.shape[0]):
        a, b = max(lo, j * width), min(hi, (j + 1) * width)
        if a < b:
            parts.append(chunks[j, :, a - j * width:b - j * width])
    return parts[0] if len(parts) == 1 else jnp.concatenate(parts, axis=1)


def _chunks_of_cols(segments, n_chunks):
    total = sum(s.shape[1] for s in segments)
    width = total // n_chunks
    chunks = []
    for j in range(n_chunks):
        lo, hi, pos, parts = j * width, (j + 1) * width, 0, []
        for s in segments:
            a, b = max(lo, pos), min(hi, pos + s.shape[1])
            if a < b:
                parts.append(s[:, a - pos:b - pos])
            pos += s.shape[1]
        chunks.append(parts[0] if len(parts) == 1 else jnp.concatenate(parts, axis=1))
    return jnp.stack(chunks)


_WEIGHTS = ['w_ada', 'b_ada', 'g_norm1', 'w_in', 'w_rnn_conv', 'b_rnn_conv', 'w_lru_a', 'b_lru_a', 'w_lru_i', 'b_lru_i',
            'lru_lambda', 'b_fgate', 'w_proj_rnn', 'w_proj_attn', 'w_out', 'g_norm2', 'w_ffn_up', 'w_ffn_conv',
            'b_ffn_conv', 'w_ffn_down', 'w_ada_final', 'b_ada_final', 'g_final']
_MATMUL = dict(w_ada=True, w_in=True, w_proj_rnn=False, w_proj_attn=False, w_out=False, w_ffn_up=True,
               w_ffn_down=False, w_ada_final=True)
_CONV = ['w_rnn_conv', 'w_ffn_conv']
_REPLICATED = [n for n in _WEIGHTS if n not in _MATMUL and n not in _CONV]


def kernel(x, c, w_ada, b_ada, g_norm1, w_in, w_rnn_conv, b_rnn_conv, w_lru_a, b_lru_a, w_lru_i, b_lru_i, lru_lambda, b_fgate, w_proj_rnn, w_proj_attn, w_out, g_norm2, w_ffn_up, w_ffn_conv, b_ffn_conv, w_ffn_down, w_ada_final, b_ada_final, g_final, loss_target, m_w_ada, m_b_ada, m_g_norm1, m_w_in, m_w_rnn_conv, m_b_rnn_conv, m_w_lru_a, m_b_lru_a, m_w_lru_i, m_b_lru_i, m_lru_lambda, m_b_fgate, m_w_proj_rnn, m_w_proj_attn, m_w_out, m_g_norm2, m_w_ffn_up, m_w_ffn_conv, m_b_ffn_conv, m_w_ffn_down, m_w_ada_final, m_b_ada_final, m_g_final, v_w_ada, v_b_ada, v_g_norm1, v_w_in, v_w_rnn_conv, v_b_rnn_conv, v_w_lru_a, v_b_lru_a, v_w_lru_i, v_b_lru_i, v_lru_lambda, v_b_fgate, v_w_proj_rnn, v_w_proj_attn, v_w_out, v_g_norm2, v_w_ffn_up, v_w_ffn_conv, v_b_ffn_conv, v_w_ffn_down, v_w_ada_final, v_b_ada_final, v_g_final):
    args = locals()
    shape_of = {n: args[n].shape for n in _WEIGHTS}

    def view(a):
        if a.ndim >= 3:
            return a[0]
        return a[None, :] if a.ndim == 1 else a

    w2 = {n: view(args[n]) for n in _WEIGHTS}
    m2 = {n: args['m_' + n].reshape(w2[n].shape) for n in _WEIGHTS}
    v2 = {n: args['v_' + n].reshape(w2[n].shape) for n in _WEIGHTS}

    bl, s, d = x.shape
    t = bl * s
    nh = b_fgate.shape[-1]
    nb, rb = w_lru_a.shape[1], w_lru_a.shape[2]
    dr = nb * rb
    da = w2['w_proj_attn'].shape[0] * N_CHIPS
    dh = da // nh
    dff = w2['w_ffn_conv'].shape[1] * N_CHIPS
    scale = dh ** -0.5
    chip = 2 * lax.axis_index("x") + lax.axis_index("y")

    names = list(_MATMUL)
    gathered = dict(zip(names, _all_gather_chips([w2[n].astype(BF16) for n in names], "ag_weights")))
    n_conv = sum(w2[n].size for n in _CONV)
    rows_conv = -(-n_conv // (FLAT_COLS * 32)) * 32
    conv_local = _flat_pad([w2[n] for n in _CONV], rows_conv * FLAT_COLS).reshape(rows_conv, FLAT_COLS)
    conv_all = _all_gather_chips([conv_local], "ag_conv")[0].reshape(N_CHIPS, -1)
    conv_full, pos = {}, 0
    for n in _CONV:
        r, n4 = w2[n].shape
        blocks = conv_all[:, pos:pos + r * n4].reshape(N_CHIPS, r, n4)
        conv_full[n] = jnp.concatenate([blocks[j] for j in range(N_CHIPS)], axis=1)
        pos += r * n4
    rowmajor = lambda n: gathered[n].reshape(-1, gathered[n].shape[2])
    w_proj_rnn_f, w_proj_attn_f, w_out_f, w_ffn_down_f = (rowmajor(n) for n in ('w_proj_rnn', 'w_proj_attn', 'w_out', 'w_ffn_down'))
    ada_chunk, adaf_chunk, up_chunk = (w2[n].shape[1] for n in ('w_ada', 'w_ada_final', 'w_ffn_up'))

    o_q, o_k, o_fl = 2 * dr, 2 * dr + da, 2 * dr + 3 * da
    o_mg = o_fl + nh
    g_in_w = gathered['w_in']
    w_rnn, w_q = _cols_of_chunks(g_in_w, 0, o_q), _cols_of_chunks(g_in_w, o_q, o_k)
    w_kv, w_mg = _cols_of_chunks(g_in_w, o_k, o_fl), _cols_of_chunks(g_in_w, o_mg, o_mg + 2 * d)
    w_fl = jnp.pad(_cols_of_chunks(g_in_w, o_fl, o_mg), ((0, 0), (0, LANES - nh)))
    w_rest = jnp.concatenate([w_q, w_kv, w_mg], axis=1)
    bf_pad = jnp.pad(w2['b_fgate'], ((0, 0), (0, LANES - nh)))

    c_act = _silu_pad(c, 16)
    mod = _mm(c_act, gathered['w_ada'], "nn", name="ada_fwd", bias=w2['b_ada'], b_chunk=ada_chunk)[:bl]
    sh1, sc1, gt1, sh2, sc2, gt2 = [mod[:, i * d:(i + 1) * d].reshape(bl, 1, d) for i in range(6)]
    modf = _mm(c_act, gathered['w_ada_final'], "nn", name="ada_final_fwd", bias=w2['b_ada_final'], b_chunk=adaf_chunk)[:bl]
    shf, scf = modf[:, :d].reshape(bl, 1, d), modf[:, d:].reshape(bl, 1, d)

    h1 = _norm_mod_fwd(x, w2['g_norm1'], sh1, sc1)
    h1f = h1.reshape(t, d)
    zr = _mm(h1f, w_rnn, "nn", name="in_rnn").reshape(bl, s, 2 * dr)
    q3 = _mm(h1f, w_q, "nn", name="in_q", out_dtype=BF16, scale=scale).reshape(bl, s, da)
    kv3 = _mm(h1f, w_kv, "nn", name="in_kv", out_dtype=BF16).reshape(bl, s, 2 * da)
    mg3 = _mm(h1f, w_mg, "nn", name="in_mg").reshape(bl, s, 2 * d)
    zf3 = _mm(h1f, w_fl, "nn", name="in_fl").reshape(bl, s, LANES)

    lru = (conv_full['w_rnn_conv'], w2['b_rnn_conv'], w2['w_lru_a'], w2['b_lru_a'], w2['w_lru_i'], w2['b_lru_i'], w2['lru_lambda'])
    hseq, y_rnn = _rnn_fwd(zr, *lru)

    f3 = _fgate_fwd(zf3, bf_pad)
    f_heads = f3[:, :, :nh].transpose(0, 2, 1).reshape(bl * nh, s)
    fcol = jnp.broadcast_to(f_heads[:, :, None], (bl * nh, s, LANES))
    frow = f_heads.reshape(bl * nh, 1, s)
    o3, lse_row = _attn_fwd(q3, kv3, fcol, frow, nh)

    pr3 = _mm(y_rnn.reshape(t, dr), w_proj_rnn_f, "nn", name="proj_rnn").reshape(bl, s, d)
    pa3 = _mm(o3.reshape(t, da), w_proj_attn_f, "nn", name="proj_attn").reshape(bl, s, d)
    merged = _merge_fwd(mg3, pr3, pa3)
    mo3 = _mm(merged.reshape(t, d), w_out_f, "nn", name="mix_out").reshape(bl, s, d)
    x1, h2 = _resid_norm_fwd(x, mo3, gt1, w2['g_norm2'], sh2, sc2)
    h2f = h2.reshape(t, d)
    up3 = _mm(h2f, gathered['w_ffn_up'], "nn", name="ffn_up", b_chunk=up_chunk).reshape(bl, s, 2 * dff)
    act3 = _ffn_act_fwd(up3, conv_full['w_ffn_conv'], w2['b_ffn_conv'])
    yf3 = _mm(act3.reshape(t, dff), w_ffn_down_f, "nn", name="ffn_down").reshape(bl, s, d)

    dx2, dshf, dscf, dg_final, loss_part = _final_fwd_bwd(x1, yf3, gt2, w2['g_final'], shf, scf, loss_target)
    loss = lax.psum(loss_part[0, 0], ("x", "y", "c"))

    dyf, dgt2 = _gate_bwd(dx2, yf3, gt2, "ffn_gate_bwd")
    dyf_f = dyf.reshape(t, d)
    g_ffn_down = _mm(act3.reshape(t, dff), dyf_f, "tn", name="dw_ffn_down")
    dact3 = _mm(dyf_f, w_ffn_down_f, "nt", name="d_ffn_act").reshape(bl, s, dff)
    dgf, duf, g_ffn_conv, g_b_ffn_conv = _ffn_act_bwd(up3, dact3, conv_full['w_ffn_conv'], w2['b_ffn_conv'])
    dgf_f, duf_f = dgf.reshape(t, dff), duf.reshape(t, dff)
    g_ffn_up = jnp.concatenate([_mm(h2f, dgf_f, "tn", name="dw_ffn_up_gate", out_chunk=up_chunk),
                                _mm(h2f, duf_f, "tn", name="dw_ffn_up_value", out_chunk=up_chunk)], axis=0)
    dh2 = _mm([dgf_f, duf_f], gathered['w_ffn_up'], "nt", name="d_h2", b_chunk=up_chunk).reshape(bl, s, d)
    dx1, dsh2, dsc2, dg_norm2 = _norm_mod_bwd(dh2, x1, dx2, w2['g_norm2'], sc2, "norm2_bwd")

    dmo, dgt1 = _gate_bwd(dx1, mo3, gt1, "mix_gate_bwd")
    dmo_f = dmo.reshape(t, d)
    g_out = _mm(merged.reshape(t, d), dmo_f, "tn", name="dw_out")
    dm3 = _mm(dmo_f, w_out_f, "nt", name="d_merged").reshape(bl, s, d)
    dpr, dpa, dmr, dma = _merge_bwd(dm3, mg3, pr3, pa3)
    g_proj_rnn = _mm(y_rnn.reshape(t, dr), dpr.reshape(t, d), "tn", name="dw_proj_rnn")
    g_proj_attn = _mm(o3.reshape(t, da), dpa.reshape(t, d), "tn", name="dw_proj_attn")
    dyr3 = _mm(dpr.reshape(t, d), w_proj_rnn_f, "nt", name="d_y_rnn").reshape(bl, s, dr)
    do3 = _mm(dpa.reshape(t, d), w_proj_attn_f, "nt", name="d_y_attn").reshape(bl, s, da)

    dq3, dk3, dv3, dfk, dfq = _attn_bwd(q3, kv3, do3, o3, lse_row, fcol, frow, nh, scale)
    heads_last = lambda a: jnp.pad(a.reshape(bl, nh, s).transpose(0, 2, 1), ((0, 0), (0, 0), (0, LANES - nh)))
    dzf3, g_bf = _fgate_bwd(heads_last(dfk[:, :, 0]), heads_last(dfq), zf3, bf_pad)

    dxr, dgr, g_rnn_conv, g_b_rnn_conv, g_lru_a, g_b_lru_a, g_lru_i, g_b_lru_i, g_lam = _rnn_bwd(zr, hseq, dyr3, *lru)

    dz = [a.reshape(t, -1) for a in (dxr, dgr, dq3, dk3, dv3, dmr, dma)]
    dzf_f = dzf3.reshape(t, LANES)
    seg_names = ("xr", "gr", "q", "k", "v", "mr", "ma")
    g_seg = [_mm(h1f, a, "tn", name="dw_in_" + n) for n, a in zip(seg_names, dz)]
    g_in_fl = _mm(h1f, dzf_f, "tn", name="dw_in_fl")[:, :nh]
    g_in = _chunks_of_cols(g_seg[:5] + [g_in_fl] + g_seg[5:], N_CHIPS)
    dh1 = _mm(dzf_f, w_fl, "nt", name="d_h1_fl")
    dh1 = _mm(dz[:2], w_rnn, "nt", name="d_h1_rnn", add=dh1)
    dh1 = _mm(dz[2:], w_rest, "nt", name="d_h1", add=dh1).reshape(bl, s, d)
    grad_x, dsh1, dsc1, dg_norm1 = _norm_mod_bwd(dh1, x, dx1, w2['g_norm1'], sc1, "norm1_bwd")

    pad_rows = lambda a: jnp.pad(a.reshape(bl, -1), ((0, 16 - bl), (0, 0)))
    dmod = pad_rows(jnp.concatenate([dsh1, dsc1, dgt1, dsh2, dsc2, dgt2], axis=-1))
    dmodf = pad_rows(jnp.concatenate([dshf, dscf], axis=-1))
    g_ada = _mm(c_act, dmod, "tn", name="dw_ada", out_chunk=ada_chunk)
    g_ada_final = _mm(c_act, dmodf, "tn", name="dw_ada_final", out_chunk=adaf_chunk)
    g_b_ada = _rowsum(dmod, "db_ada")
    g_b_ada_final = _rowsum(dmodf, "db_ada_final")

    rowchunks = lambda g: g.reshape(N_CHIPS, g.shape[0] // N_CHIPS, g.shape[1])
    full = dict(w_ada=g_ada, w_in=g_in, w_proj_rnn=rowchunks(g_proj_rnn), w_proj_attn=rowchunks(g_proj_attn),
                w_out=rowchunks(g_out), w_ffn_up=g_ffn_up, w_ffn_down=rowchunks(g_ffn_down), w_ada_final=g_ada_final)
    small = dict(b_ada=g_b_ada, g_norm1=dg_norm1, w_rnn_conv=g_rnn_conv, b_rnn_conv=g_b_rnn_conv, w_lru_a=g_lru_a,
                 b_lru_a=g_b_lru_a, w_lru_i=g_lru_i, b_lru_i=g_b_lru_i, lru_lambda=g_lam, b_fgate=g_bf[:, :nh],
                 g_norm2=dg_norm2, w_ffn_conv=g_ffn_conv, b_ffn_conv=g_b_ffn_conv, b_ada_final=g_b_ada_final,
                 g_final=dg_final)

    small_names = _REPLICATED + _CONV
    n_small = sum(small[n].size for n in small_names)
    rows_q = -(-n_small // (N_CHIPS * FLAT_COLS * 32 * RS_STEPS)) * 32 * RS_STEPS
    small_flat = _flat_pad([small[n] for n in small_names], N_CHIPS * rows_q * FLAT_COLS).reshape(N_CHIPS, rows_q, FLAT_COLS)
    reduced = _reduce_scatter_chips([full[n] for n in names] + [small_flat])
    grad = dict(zip(names, reduced[:-1]))
    small_all = _all_gather_chips([reduced[-1]], "ag_small_grads")[0].reshape(-1)
    grad.update(zip(small_names, _split_flat(small_all, [small[n].shape for n in small_names])))
    for n in _CONV:
        n4 = w2[n].shape[1]
        grad[n] = lax.dynamic_slice_in_dim(grad[n], chip * n4, n4, axis=1)

    delta_w, new_m, new_v = {}, {}, {}
    for n in names:
        delta_w[n], new_m[n], new_v[n] = _adamw(grad[n], w2[n], m2[n], v2[n], "adamw_" + n)
    rows_small = -(-sum(w2[n].size for n in small_names) // (FLAT_COLS * SUBLANES)) * SUBLANES
    flat_small = lambda src: _flat_pad([src[n] for n in small_names], rows_small * FLAT_COLS).reshape(rows_small, FLAT_COLS)
    small_out = _adamw(flat_small(grad), flat_small(w2), flat_small(m2), flat_small(v2), "adamw_small")
    for dst, flat in zip((delta_w, new_m, new_v), small_out):
        dst.update(zip(small_names, _split_flat(flat.reshape(-1), [w2[n].shape for n in small_names])))

    out = [loss, grad_x]
    for src in (grad, delta_w, new_m, new_v):
        out += [src[n].reshape(shape_of[n]) for n in _WEIGHTS]
    return tuple(out)
```

```python
import functools
import math

import jax
import jax.numpy as jnp
from jax import lax
from jax.experimental import pallas as pl
from jax.experimental.pallas import tpu as pltpu

F32 = jnp.float32
BF16 = jnp.bfloat16
MESH = pl.DeviceIdType.MESH

RMS_EPS = 1e-6
LRU_C = 8.0
ADAM_LR = 0.001
ADAM_B1 = 0.9
ADAM_B2 = 0.999
ADAM_EPS = 1e-08
ADAM_WD = 0.01
ADAM_STEP = 10

LANES = 128
SUBLANES = 8
N_CHIPS = 4
FLAT_COLS = 1024
SCAN_SEGMENTS = 2 * SUBLANES
VMEM_LIMIT = 48 * 1024 * 1024
NEG_BIG = -1e30


def _cp(n_axes):
    return pltpu.CompilerParams(dimension_semantics=("arbitrary",) * n_axes, vmem_limit_bytes=VMEM_LIMIT)


def _tile(n, target, align):
    if n <= target:
        return n
    t = (target // align) * align
    while t >= align:
        if n % t == 0:
            return t
        t -= align
    return n


def _nice_rows(n, align):
    r = -(-n // align) * align
    while True:
        if r <= 640:
            return r, r
        t = _tile(r, 640, align)
        if 128 <= t <= 640:
            return r, t
        r += align


def _sigmoid(x):
    return jax.nn.sigmoid(x)


def _softplus(x):
    return jnp.maximum(x, 0.0) + jnp.log1p(jnp.exp(-jnp.abs(x)))


def _expm1(x, exp_x):
    small = x * (1.0 + 0.5 * x * (1.0 + (1.0 / 3.0) * x * (1.0 + 0.25 * x)))
    return jnp.where(jnp.abs(x) < 0.05, small, exp_x - 1.0)


_GELU_K = math.sqrt(2.0 / math.pi)
_GELU_C = 0.044715


def _gelu(x):
    t = jnp.tanh(_GELU_K * (x + _GELU_C * x * x * x))
    return 0.5 * x * (1.0 + t)


def _gelu_and_grad(x):
    t = jnp.tanh(_GELU_K * (x + _GELU_C * x * x * x))
    g = 0.5 * x * (1.0 + t)
    dg = 0.5 * (1.0 + t) + 0.5 * x * (1.0 - t * t) * _GELU_K * (1.0 + 3.0 * _GELU_C * x * x)
    return g, dg


def _shift_down(x, k):
    if k == 0:
        return x
    rows = lax.broadcasted_iota(jnp.int32, x.shape, 0)
    return jnp.where(rows >= k, pltpu.roll(x, k, 0), 0.0)


def _shift_up(x, k):
    if k == 0:
        return x
    s = x.shape[0]
    rows = lax.broadcasted_iota(jnp.int32, x.shape, 0)
    return jnp.where(rows < s - k, pltpu.roll(x, s - k, 0), 0.0)


def _dot(a, b, dims):
    return lax.dot_general(a.astype(BF16), b.astype(BF16), (dims, ((), ())), preferred_element_type=F32)


_NN = ((1,), (0,))
_NT = ((1,), (1,))
_TN = ((0,), (0,))


def _mm(a, b, mode, *, name, out_dtype=F32, scale=None, bias=None, add=None, tm=1024, tn=1024, tk=1024,
        b_chunk=None, out_chunk=None):
    pieces = list(a) if isinstance(a, (list, tuple)) else [a]
    ksize = lambda p: p.shape[0] if mode == "tn" else p.shape[1]
    if b_chunk is None:
        brows, bcols = b.shape
    else:
        brows, bcols = b.shape[1], b.shape[0] * b_chunk
    k = sum(ksize(p) for p in pieces)
    if mode == "nt":
        m, n = pieces[0].shape[0], brows
        assert bcols == k, (bcols, k)
    else:
        m, n = (pieces[0].shape[1] if mode == "tn" else pieces[0].shape[0]), bcols
        assert brows == k, (brows, k)
    tm = _tile(m, tm, LANES)
    ncut = n
    if b_chunk is not None and mode != "nt":
        ncut = b_chunk
    if out_chunk is not None:
        ncut = math.gcd(ncut, out_chunk)
    tn = _tile(ncut, tn, LANES)
    kcut = b_chunk if (b_chunk is not None and mode == "nt") else k
    for p in pieces:
        kcut = math.gcd(kcut, ksize(p))
    tk = _tile(kcut, tk, LANES)
    nk = k // tk
    dims = {"nn": _NN, "nt": _NT, "tn": _TN}[mode]
    counts = [ksize(p) // tk for p in pieces]
    starts = [sum(counts[:i]) for i in range(len(pieces))]
    n_pieces = len(pieces)

    def a_spec(s0, cnt):
        kmap = (lambda kk: kk) if n_pieces == 1 else (lambda kk: jnp.clip(kk - s0, 0, cnt - 1))
        if mode == "tn":
            return pl.BlockSpec((tk, tm), lambda i, j, kk: (kmap(kk), i))
        return pl.BlockSpec((tm, tk), lambda i, j, kk: (i, kmap(kk)))

    if b_chunk is None:
        if mode == "nt":
            b_spec = pl.BlockSpec((tn, tk), lambda i, j, kk: (j, kk))
        else:
            b_spec = pl.BlockSpec((tk, tn), lambda i, j, kk: (kk, j))
    elif mode == "nt":
        per_b = b_chunk // tk
        b_spec = pl.BlockSpec((None, tn, tk), lambda i, j, kk: (kk // per_b, j, kk % per_b))
    else:
        per_b = b_chunk // tn
        b_spec = pl.BlockSpec((None, tk, tn), lambda i, j, kk: (j // per_b, kk, j % per_b))
    if out_chunk is None:
        out_spec = pl.BlockSpec((tm, tn), lambda i, j, kk: (i, j))
        out_shape = jax.ShapeDtypeStruct((m, n), out_dtype)
    else:
        per_o = out_chunk // tn
        out_spec = pl.BlockSpec((None, tm, tn), lambda i, j, kk: (j // per_o, i, j % per_o))
        out_shape = jax.ShapeDtypeStruct((n // out_chunk, m, out_chunk), out_dtype)
    in_specs = [a_spec(s0, cnt) for s0, cnt in zip(starts, counts)] + [b_spec]
    args = pieces + [b]
    if bias is not None:
        in_specs.append(pl.BlockSpec((1, tn), lambda i, j, kk: (0, j)))
        args.append(bias)
    if add is not None:
        in_specs.append(pl.BlockSpec((tm, tn), lambda i, j, kk: (i, j)))
        args.append(add)

    def kern(*refs):
        b_ref = refs[n_pieces]
        o_ref = refs[n_pieces + 1 + (bias is not None) + (add is not None)]

        def finish(r):
            if scale is not None:
                r = r * scale
            pos = n_pieces + 1
            if bias is not None:
                r = r + refs[pos][...]
                pos += 1
            if add is not None:
                r = r + refs[pos][...]
            o_ref[...] = r.astype(out_dtype)

        if nk == 1:
            finish(_dot(refs[0][...], b_ref[...], dims))
            return
        acc = refs[-1]
        kk = pl.program_id(2)

        @pl.when(kk == 0)
        def _():
            acc[...] = jnp.zeros_like(acc)

        if n_pieces == 1:
            acc[...] += _dot(refs[0][...], b_ref[...], dims)
        else:
            for idx in range(n_pieces):
                @pl.when((kk >= starts[idx]) & (kk < starts[idx] + counts[idx]))
                def _(idx=idx):
                    acc[...] += _dot(refs[idx][...], b_ref[...], dims)

        @pl.when(kk == nk - 1)
        def _():
            finish(acc[...])

    return pl.pallas_call(
        kern, name=name,
        grid=(m // tm, n // tn, nk),
        in_specs=in_specs, out_specs=out_spec, out_shape=out_shape,
        scratch_shapes=[pltpu.VMEM((tm, tn), F32)] if nk > 1 else [],
        compiler_params=_cp(3),
    )(*args)


def _silu_pad(c, rows):
    bl, d = c.shape

    def kern(c_ref, o_ref):
        o_ref[...] = jnp.zeros_like(o_ref)
        v = c_ref[...]
        o_ref[0:bl, :] = v * _sigmoid(v)

    return pl.pallas_call(kern, name="silu_pad", out_shape=jax.ShapeDtypeStruct((rows, d), F32))(c)


def _rowsum(x, name):
    r, n = x.shape

    def kern(x_ref, o_ref):
        o_ref[...] = jnp.sum(x_ref[...], axis=0, keepdims=True)

    return pl.pallas_call(kern, name=name, out_shape=jax.ShapeDtypeStruct((1, n), F32))(x)


def _norm_parts(x, g):
    r = lax.rsqrt(jnp.mean(x * x, axis=-1, keepdims=True) + RMS_EPS)
    xh = x * r
    return r, xh, xh * g


def _norm_bwd_parts(dh, xh, r, g, sc):
    n = xh * g
    dn = dh * (1.0 + sc)
    dxh = dn * g
    dx = r * (dxh - xh * jnp.mean(dxh * xh, axis=-1, keepdims=True))
    return dx, dh, dh * n, dn * xh


def _act_specs(ts, d, n):
    return [pl.BlockSpec((1, ts, d), lambda b, t: (b, t, 0)) for _ in range(n)]


def _vec_spec(d):
    return pl.BlockSpec((1, 1, d), lambda b, t: (b, 0, 0))


def _par_spec(d):
    return pl.BlockSpec((1, d), lambda b, t: (0, 0))


def _norm_mod_fwd(x3, g, sh, sc):
    bl, s, d = x3.shape
    ts = _tile(s, 512, SUBLANES)

    def kern(x_ref, g_ref, sh_ref, sc_ref, h_ref):
        _, _, n = _norm_parts(x_ref[0], g_ref[...])
        h_ref[0] = (n * (1.0 + sc_ref[0]) + sh_ref[0]).astype(BF16)

    return pl.pallas_call(
        kern, name="norm_mod_fwd", grid=(bl, s // ts),
        in_specs=_act_specs(ts, d, 1) + [_par_spec(d), _vec_spec(d), _vec_spec(d)],
        out_specs=_act_specs(ts, d, 1)[0],
        out_shape=jax.ShapeDtypeStruct((bl, s, d), BF16),
        compiler_params=_cp(2),
    )(x3, g, sh, sc)


def _resid_norm_fwd(x3, y3, gate, g, sh, sc):
    bl, s, d = x3.shape
    ts = _tile(s, 512, SUBLANES)

    def kern(x_ref, y_ref, gate_ref, g_ref, sh_ref, sc_ref, x1_ref, h_ref):
        x1 = x_ref[0] + gate_ref[0] * y_ref[0]
        x1_ref[0] = x1
        _, _, n = _norm_parts(x1, g_ref[...])
        h_ref[0] = (n * (1.0 + sc_ref[0]) + sh_ref[0]).astype(BF16)

    return pl.pallas_call(
        kern, name="resid_norm_fwd", grid=(bl, s // ts),
        in_specs=_act_specs(ts, d, 2) + [_vec_spec(d), _par_spec(d), _vec_spec(d), _vec_spec(d)],
        out_specs=_act_specs(ts, d, 2),
        out_shape=[jax.ShapeDtypeStruct((bl, s, d), F32), jax.ShapeDtypeStruct((bl, s, d), BF16)],
        compiler_params=_cp(2),
    )(x3, y3, gate, g, sh, sc)


def _norm_mod_bwd(dh3, x3, dres3, g, sc, name):
    bl, s, d = x3.shape
    ts = _tile(s, 512, SUBLANES)

    def kern(dh_ref, x_ref, dres_ref, g_ref, sc_ref, dx_ref, dsh_ref, dsc_ref, dg_ref):
        b, t = pl.program_id(0), pl.program_id(1)
        gv = g_ref[...]
        r, xh, _ = _norm_parts(x_ref[0], gv)
        dx, a, bb, cc = _norm_bwd_parts(dh_ref[0], xh, r, gv, sc_ref[0])
        dx_ref[0] = dres_ref[0] + dx

        @pl.when(t == 0)
        def _():
            dsh_ref[...] = jnp.zeros_like(dsh_ref)
            dsc_ref[...] = jnp.zeros_like(dsc_ref)

        @pl.when((t == 0) & (b == 0))
        def _():
            dg_ref[...] = jnp.zeros_like(dg_ref)

        dsh_ref[0] += jnp.sum(a, axis=0, keepdims=True)
        dsc_ref[0] += jnp.sum(bb, axis=0, keepdims=True)
        dg_ref[...] += jnp.sum(cc, axis=0, keepdims=True)

    return pl.pallas_call(
        kern, name=name, grid=(bl, s // ts),
        in_specs=_act_specs(ts, d, 3) + [_par_spec(d), _vec_spec(d)],
        out_specs=[_act_specs(ts, d, 1)[0], _vec_spec(d), _vec_spec(d), _par_spec(d)],
        out_shape=[jax.ShapeDtypeStruct((bl, s, d), F32), jax.ShapeDtypeStruct((bl, 1, d), F32),
                   jax.ShapeDtypeStruct((bl, 1, d), F32), jax.ShapeDtypeStruct((1, d), F32)],
        compiler_params=_cp(2),
    )(dh3, x3, dres3, g, sc)


def _gate_bwd(dx3, y3, gate, name):
    bl, s, d = dx3.shape
    ts = _tile(s, 512, SUBLANES)

    def kern(dx_ref, y_ref, gate_ref, dy_ref, dgate_ref):
        t = pl.program_id(1)
        dx = dx_ref[0]
        dy_ref[0] = (gate_ref[0] * dx).astype(BF16)

        @pl.when(t == 0)
        def _():
            dgate_ref[...] = jnp.zeros_like(dgate_ref)

        dgate_ref[0] += jnp.sum(dx * y_ref[0], axis=0, keepdims=True)

    return pl.pallas_call(
        kern, name=name, grid=(bl, s // ts),
        in_specs=_act_specs(ts, d, 2) + [_vec_spec(d)],
        out_specs=[_act_specs(ts, d, 1)[0], _vec_spec(d)],
        out_shape=[jax.ShapeDtypeStruct((bl, s, d), BF16), jax.ShapeDtypeStruct((bl, 1, d), F32)],
        compiler_params=_cp(2),
    )(dx3, y3, gate)


def _final_fwd_bwd(x1, yf, gate2, g, shf, scf, tgt):
    bl, s, d = x1.shape
    ts = _tile(s, 512, SUBLANES)

    def kern(x1_ref, yf_ref, gate_ref, g_ref, sh_ref, sc_ref, tgt_ref, dx_ref, dsh_ref, dsc_ref, dg_ref, loss_ref):
        b, t = pl.program_id(0), pl.program_id(1)
        gv, sc = g_ref[...], sc_ref[0]
        x2 = x1_ref[0] + gate_ref[0] * yf_ref[0]
        r, xh, n = _norm_parts(x2, gv)
        err = n * (1.0 + sc) + sh_ref[0] - tgt_ref[0]
        dx, a, bb, cc = _norm_bwd_parts(err * (1.0 / d), xh, r, gv, sc)
        dx_ref[0] = dx

        @pl.when(t == 0)
        def _():
            dsh_ref[...] = jnp.zeros_like(dsh_ref)
            dsc_ref[...] = jnp.zeros_like(dsc_ref)

        @pl.when((t == 0) & (b == 0))
        def _():
            dg_ref[...] = jnp.zeros_like(dg_ref)
            loss_ref[...] = jnp.zeros_like(loss_ref)

        dsh_ref[0] += jnp.sum(a, axis=0, keepdims=True)
        dsc_ref[0] += jnp.sum(bb, axis=0, keepdims=True)
        dg_ref[...] += jnp.sum(cc, axis=0, keepdims=True)
        tok = jnp.mean(err * err, axis=-1, keepdims=True)
        loss_ref[...] += 0.5 * jnp.sum(tok, axis=0, keepdims=True)

    return pl.pallas_call(
        kern, name="final_fwd_bwd", grid=(bl, s // ts),
        in_specs=_act_specs(ts, d, 2) + [_vec_spec(d), _par_spec(d), _vec_spec(d), _vec_spec(d)] + _act_specs(ts, d, 1),
        out_specs=[_act_specs(ts, d, 1)[0], _vec_spec(d), _vec_spec(d), _par_spec(d),
                   pl.BlockSpec((1, 1), lambda b, t: (0, 0))],
        out_shape=[jax.ShapeDtypeStruct((bl, s, d), F32), jax.ShapeDtypeStruct((bl, 1, d), F32),
                   jax.ShapeDtypeStruct((bl, 1, d), F32), jax.ShapeDtypeStruct((1, d), F32),
                   jax.ShapeDtypeStruct((1, 1), F32)],
        compiler_params=_cp(2),
    )(x1, yf, gate2, g, shf, scf, tgt)


def _rnn_gates(xr, cw, cb, wa, ba, wi, bi, lam):
    kw = cw.shape[0]
    xc = cb
    for k in range(kw):
        xc = xc + _shift_down(xr, kw - 1 - k) * cw[k:k + 1, :]
    r = _sigmoid(_dot(xc, wa, _NN) + ba)
    i = _sigmoid(_dot(xc, wi, _NN) + bi)
    sp = _softplus(-lam)
    log_a = -LRU_C * r * sp
    a = jnp.exp(log_a)
    mult = jnp.sqrt(-_expm1(2.0 * log_a, a * a))
    return xc, r, i, sp, a, mult


def _segment_scan(a_s, u_s, h_s, p_s, reverse):
    s, c = a_s.shape
    seg = s // SCAN_SEGMENTS

    unroll = math.gcd(seg, 8)

    def steps(n, carry):
        h, p = carry
        for j in range(unroll):
            t = n * unroll + j
            t = (seg - 1 - t) if reverse else t
            av = a_s[pl.ds(t, SCAN_SEGMENTS, stride=seg), :]
            uv = u_s[pl.ds(t, SCAN_SEGMENTS, stride=seg), :]
            h = av * h + uv
            p = p * av
            h_s[pl.ds(t, SCAN_SEGMENTS, stride=seg), :] = h
            p_s[pl.ds(t, SCAN_SEGMENTS, stride=seg), :] = p
        return h, p

    lax.fori_loop(0, seg // unroll, steps, (jnp.zeros((SCAN_SEGMENTS, c), F32), jnp.ones((SCAN_SEGMENTS, c), F32)))
    carry = jnp.zeros((1, c), F32)
    order = range(SCAN_SEGMENTS - 1, -1, -1) if reverse else range(SCAN_SEGMENTS)
    for j in order:
        rows = pl.ds(j * seg, seg)
        fixed = h_s[rows, :] + p_s[rows, :] * carry
        h_s[rows, :] = fixed
        carry = fixed[0:1, :] if reverse else fixed[seg - 1:seg, :]


def _rnn_specs(s, rb, nb):
    act = lambda off: pl.BlockSpec((1, s, rb), lambda b, n, off=off: (b, 0, off + n))
    par = pl.BlockSpec((1, rb), lambda b, n: (0, n))
    wsp = pl.BlockSpec((1, rb, rb), lambda b, n: (n, 0, 0))
    return act, par, wsp


def _rnn_fwd(zr3, cw, cb, wa, ba, wi, bi, lam):
    bl, s, two = zr3.shape
    nb, rb, _ = wa.shape
    dr = nb * rb
    kw = cw.shape[0]
    act, par, wsp = _rnn_specs(s, rb, nb)

    def kern(xr_ref, gr_ref, cw_ref, cb_ref, wa_ref, ba_ref, wi_ref, bi_ref, lam_ref, h_ref, y_ref, a_s, u_s, h_s, p_s):
        xc, r, i, sp, a, mult = _rnn_gates(xr_ref[0], cw_ref[...], cb_ref[...], wa_ref[0], ba_ref[...],
                                           wi_ref[0], bi_ref[...], lam_ref[...])
        a_s[...] = a
        u_s[...] = mult * (i * xc)
        _segment_scan(a_s, u_s, h_s, p_s, reverse=False)
        h = h_s[...]
        h_ref[0] = h
        y_ref[0] = (_gelu(gr_ref[0]) * h).astype(BF16)

    return pl.pallas_call(
        kern, name="rnn_fwd", grid=(bl, nb),
        in_specs=[act(0), act(nb), pl.BlockSpec((kw, rb), lambda b, n: (0, n)), par, wsp, par, wsp, par, par],
        out_specs=[act(0), act(0)],
        out_shape=[jax.ShapeDtypeStruct((bl, s, dr), F32), jax.ShapeDtypeStruct((bl, s, dr), BF16)],
        scratch_shapes=[pltpu.VMEM((s, rb), F32)] * 4,
        compiler_params=_cp(2),
    )(zr3, zr3, cw, cb, wa, ba, wi, bi, lam)


def _rnn_bwd(zr3, h3, dy3, cw, cb, wa, ba, wi, bi, lam):
    bl, s, _ = zr3.shape
    nb, rb, _ = wa.shape
    dr = nb * rb
    kw = cw.shape[0]
    act = lambda off: pl.BlockSpec((1, s, rb), lambda n, b, off=off: (b, 0, off + n))
    par = pl.BlockSpec((1, rb), lambda n, b: (0, n))
    wsp = pl.BlockSpec((1, rb, rb), lambda n, b: (n, 0, 0))
    cws = pl.BlockSpec((kw, rb), lambda n, b: (0, n))

    def kern(xr_ref, gr_ref, h_ref, dy_ref, cw_ref, cb_ref, wa_ref, ba_ref, wi_ref, bi_ref, lam_ref,
             dxr_ref, dgr_ref, dcw_ref, dcb_ref, dwa_ref, dba_ref, dwi_ref, dbi_ref, dlam_ref, a_s, u_s, h_s, p_s):
        b = pl.program_id(1)
        xr, cwv, lamv = xr_ref[0], cw_ref[...], lam_ref[...]
        wav, wiv = wa_ref[0], wi_ref[0]
        xc, r, i, sp, a, mult = _rnn_gates(xr, cwv, cb_ref[...], wav, ba_ref[...], wiv, bi_ref[...], lamv)
        h, dy = h_ref[0], dy_ref[0]
        ge, dge = _gelu_and_grad(gr_ref[0])
        dgr_ref[0] = (dy * h * dge).astype(BF16)
        a_s[...] = _shift_up(a, 1)
        u_s[...] = dy * ge
        _segment_scan(a_s, u_s, h_s, p_s, reverse=True)
        g = h_s[...]
        da = g * _shift_down(h, 1)
        ix = i * xc
        dlog_a = da * a + (g * ix) * (-(a * a) / mult)
        di = g * mult * xc
        dpa = (dlog_a * (-LRU_C * sp)) * r * (1.0 - r)
        dpi = di * i * (1.0 - i)
        dxc = g * mult * i + _dot(dpa, wav, _NT) + _dot(dpi, wiv, _NT)
        dxr = jnp.zeros_like(dxc)
        dcw_rows = []
        for k in range(kw):
            dxr = dxr + _shift_up(dxc, kw - 1 - k) * cwv[k:k + 1, :]
            dcw_rows.append(jnp.sum(dxc * _shift_down(xr, kw - 1 - k), axis=0, keepdims=True))
        dxr_ref[0] = dxr.astype(BF16)

        @pl.when(b == 0)
        def _():
            for ref in (dcw_ref, dcb_ref, dwa_ref, dba_ref, dwi_ref, dbi_ref, dlam_ref):
                ref[...] = jnp.zeros_like(ref)

        for k in range(kw):
            dcw_ref[k:k + 1, :] += dcw_rows[k]
        dcb_ref[...] += jnp.sum(dxc, axis=0, keepdims=True)
        dwa_ref[0] += _dot(xc, dpa, _TN)
        dwi_ref[0] += _dot(xc, dpi, _TN)
        dba_ref[...] += jnp.sum(dpa, axis=0, keepdims=True)
        dbi_ref[...] += jnp.sum(dpi, axis=0, keepdims=True)
        dsp = jnp.sum(dlog_a * (-LRU_C * r), axis=0, keepdims=True)
        dlam_ref[...] += dsp * (-_sigmoid(-lamv))

    vec = jax.ShapeDtypeStruct((1, dr), F32)
    wsh = jax.ShapeDtypeStruct((nb, rb, rb), F32)
    return pl.pallas_call(
        kern, name="rnn_bwd", grid=(nb, bl),
        in_specs=[act(0), act(nb), act(0), act(0), cws, par, wsp, par, wsp, par, par],
        out_specs=[act(0), act(0), cws, par, wsp, par, wsp, par, par],
        out_shape=[jax.ShapeDtypeStruct((bl, s, dr), BF16), jax.ShapeDtypeStruct((bl, s, dr), BF16),
                   jax.ShapeDtypeStruct((kw, dr), F32), vec, wsh, vec, wsh, vec, vec],
        scratch_shapes=[pltpu.VMEM((s, rb), F32)] * 4,
        compiler_params=_cp(2),
    )(zr3, zr3, h3, dy3, cw, cb, wa, ba, wi, bi, lam)


def _tri(n, upper):
    r = lax.broadcasted_iota(jnp.int32, (n, n), 0)
    c = lax.broadcasted_iota(jnp.int32, (n, n), 1)
    return jnp.where((c >= r) if upper else (c <= r), 1.0, 0.0).astype(F32)


def _fgate_fwd(zf3, bf):
    bl, s, w = zf3.shape
    ch = _tile(s, 256, SUBLANES)

    def kern(z_ref, b_ref, f_ref):
        tri = _tri(ch, upper=False)
        carry = jnp.zeros((1, w), F32)
        for j in range(s // ch):
            rows = pl.ds(j * ch, ch)
            lf = -_softplus(-(z_ref[0, rows, :] + b_ref[...]))
            out = jnp.dot(tri, lf, precision=lax.Precision.HIGHEST, preferred_element_type=F32) + carry
            f_ref[0, rows, :] = out
            carry = out[ch - 1:ch, :]

    return pl.pallas_call(
        kern, name="fgate_fwd", grid=(bl,),
        in_specs=[pl.BlockSpec((1, s, w), lambda b: (b, 0, 0)), pl.BlockSpec((1, w), lambda b: (0, 0))],
        out_specs=pl.BlockSpec((1, s, w), lambda b: (b, 0, 0)),
        out_shape=jax.ShapeDtypeStruct((bl, s, w), F32),
        compiler_params=_cp(1),
    )(zf3, bf)


def _fgate_bwd(dfk3, dfq3, zf3, bf):
    bl, s, w = zf3.shape
    ch = _tile(s, 256, SUBLANES)

    def kern(dfk_ref, dfq_ref, z_ref, b_ref, dz_ref, db_ref):
        b = pl.program_id(0)
        tri = _tri(ch, upper=True)
        carry = jnp.zeros((1, w), F32)
        dbsum = jnp.zeros((1, w), F32)
        for j in range(s // ch - 1, -1, -1):
            rows = pl.ds(j * ch, ch)
            df = dfk_ref[0, rows, :] + dfq_ref[0, rows, :]
            dlf = jnp.dot(tri, df, precision=lax.Precision.HIGHEST, preferred_element_type=F32) + carry
            carry = dlf[0:1, :]
            dz = dlf * _sigmoid(-(z_ref[0, rows, :] + b_ref[...]))
            dz_ref[0, rows, :] = dz.astype(BF16)
            dbsum = dbsum + jnp.sum(dz, axis=0, keepdims=True)

        @pl.when(b == 0)
        def _():
            db_ref[...] = jnp.zeros_like(db_ref)

        db_ref[...] += dbsum

    return pl.pallas_call(
        kern, name="fgate_bwd", grid=(bl,),
        in_specs=[pl.BlockSpec((1, s, w), lambda b: (b, 0, 0))] * 3 + [pl.BlockSpec((1, w), lambda b: (0, 0))],
        out_specs=[pl.BlockSpec((1, s, w), lambda b: (b, 0, 0)), pl.BlockSpec((1, w), lambda b: (0, 0))],
        out_shape=[jax.ShapeDtypeStruct((bl, s, w), BF16), jax.ShapeDtypeStruct((1, w), F32)],
        compiler_params=_cp(1),
    )(dfk3, dfq3, zf3, bf)


def _lanes(col, width):
    return col if width == LANES else jnp.concatenate([col] * (width // LANES), axis=1)


def _causal(sc, row0, col0, transposed):
    r = lax.broadcasted_iota(jnp.int32, sc.shape, 0) + row0
    c = lax.broadcasted_iota(jnp.int32, sc.shape, 1) + col0
    return jnp.where((c >= r) if transposed else (r >= c), sc, NEG_BIG)


def _attn_fwd(q3, kv3, fcol, frow, nh):
    bl, s, da = q3.shape
    dh = da // nh
    tq = _tile(s, 512, LANES)
    nq = s // tq

    def kern(q_ref, k_ref, v_ref, fk_ref, fq_ref, o_ref, lse_ref, m_s, l_s, acc):
        iq, ik = pl.program_id(2), pl.program_id(3)

        @pl.when(ik == 0)
        def _():
            m_s[...] = jnp.full_like(m_s, NEG_BIG)
            l_s[...] = jnp.zeros_like(l_s)
            acc[...] = jnp.zeros_like(acc)

        def block(masked):
            st = _dot(k_ref[0], q_ref[0], _NT) - _lanes(fk_ref[0], tq) + fq_ref[0]
            if masked:
                st = _causal(st, ik * tq, iq * tq, True)
            m_old = m_s[...]
            m_new = jnp.maximum(m_old, jnp.max(st, axis=0, keepdims=True))
            alpha = jnp.exp(m_old - m_new)
            pt = jnp.exp(st - m_new)
            l_s[...] = alpha * l_s[...] + jnp.sum(pt, axis=0, keepdims=True)
            acc[...] = alpha * acc[...] + _dot(v_ref[0], pt, _TN)
            m_s[...] = m_new

        pl.when(ik < iq)(functools.partial(block, False))
        pl.when(ik == iq)(functools.partial(block, True))

        @pl.when(ik == nq - 1)
        def _():
            l = l_s[...]
            o_ref[0] = (acc[...] / l).T
            lse_ref[0] = m_s[...] + jnp.log(l)

    kmap = lambda off: (lambda b, h, iq, ik: (b, jnp.minimum(ik, iq), off + h))
    return pl.pallas_call(
        kern, name="attn_fwd", grid=(bl, nh, nq, nq),
        in_specs=[pl.BlockSpec((1, tq, dh), lambda b, h, iq, ik: (b, iq, h)),
                  pl.BlockSpec((1, tq, dh), kmap(0)), pl.BlockSpec((1, tq, dh), kmap(nh)),
                  pl.BlockSpec((1, tq, LANES), lambda b, h, iq, ik: (b * nh + h, jnp.minimum(ik, iq), 0)),
                  pl.BlockSpec((1, 1, tq), lambda b, h, iq, ik: (b * nh + h, 0, iq))],
        out_specs=[pl.BlockSpec((1, tq, dh), lambda b, h, iq, ik: (b, iq, h)),
                   pl.BlockSpec((1, 1, tq), lambda b, h, iq, ik: (b * nh + h, 0, iq))],
        out_shape=[jax.ShapeDtypeStruct((bl, s, da), F32), jax.ShapeDtypeStruct((bl * nh, 1, s), F32)],
        scratch_shapes=[pltpu.VMEM((1, tq), F32), pltpu.VMEM((1, tq), F32), pltpu.VMEM((dh, tq), F32)],
        compiler_params=_cp(4),
    )(q3, kv3, kv3, fcol, frow)


def _attn_bwd(q3, kv3, do3, o3, lse_row, fcol, frow, nh, scale):
    bl, s, da = q3.shape
    dh = da // nh
    tk = _tile(s, 512, LANES)
    nk = s // tk

    def kern(q_ref, k_ref, v_ref, do_ref, o_ref, lse_ref, fk_ref, fq_ref, dq_ref, dk_ref, dv_ref, dfk_ref, dfq_ref,
             dq_acc, dk_acc, dv_acc, dfq_acc, delta_s):
        ik, iq = pl.program_id(2), pl.program_id(3)
        qrow = pl.ds(iq, 1)

        @pl.when((ik == 0) & (iq == 0))
        def _():
            dq_acc[...] = jnp.zeros_like(dq_acc)
            dfq_acc[...] = jnp.zeros_like(dfq_acc)

        @pl.when(ik == 0)
        def _():
            prod = do_ref[0] * o_ref[0]
            rows = lax.dot_general(jnp.ones((SUBLANES, dh), F32), prod, (_NT, ((), ())),
                                   precision=lax.Precision.HIGHEST, preferred_element_type=F32)
            delta_s[qrow, :] = rows[0:1, :]

        @pl.when(iq == 0)
        def _():
            dk_acc[...] = jnp.zeros_like(dk_acc)
            dv_acc[...] = jnp.zeros_like(dv_acc)

        def block(masked):
            q = q_ref[0]
            st = _dot(k_ref[0], q, _NT) - _lanes(fk_ref[0], tk) + fq_ref[0]
            if masked:
                st = _causal(st, ik * tk, iq * tk, True)
            pt = jnp.exp(st - lse_ref[0])
            dv_acc[...] += _dot(pt, do_ref[0], _NN)
            dpt = _dot(v_ref[0], do_ref[0], _NT)
            dst = (pt * (dpt - delta_s[qrow, :])).astype(BF16)
            q_ones = jnp.concatenate([q, jnp.ones_like(q)], axis=1)
            dk_acc[...] += _dot(dst, q_ones, _NN)
            qrows = pl.ds(pl.multiple_of(iq * tk, tk), tk)
            dq_acc[qrows, :] += _dot(dst, k_ref[0], _TN)
            dfq_acc[qrow, :] += jnp.sum(dst.astype(F32), axis=0, keepdims=True)

        pl.when(iq > ik)(functools.partial(block, False))
        pl.when(iq == ik)(functools.partial(block, True))

        @pl.when(iq == nk - 1)
        def _():
            ext = dk_acc[...]
            dk_ref[0] = ext[:, :dh].astype(BF16)
            dfk_ref[0] = -ext[:, dh:]
            dv_ref[0] = dv_acc[...].astype(BF16)

        @pl.when((ik == nk - 1) & (iq == nk - 1))
        def _():
            dq_ref[0] = (dq_acc[...] * scale).astype(BF16)
            dfq_ref[0] = dfq_acc[...]

    later = lambda ik, iq: jnp.maximum(iq, ik)
    qmap = lambda b, h, ik, iq: (b, later(ik, iq), h)
    omap = lambda b, h, ik, iq: (b, jnp.where(ik == 0, iq, 0), h)
    rmap = lambda b, h, ik, iq: (b * nh + h, 0, later(ik, iq))
    kmap = lambda off: (lambda b, h, ik, iq: (b, ik, off + h))
    bmap = lambda b, h, ik, iq: (b * nh + h, ik, 0)
    return pl.pallas_call(
        kern, name="attn_bwd", grid=(bl, nh, nk, nk),
        in_specs=[pl.BlockSpec((1, tk, dh), qmap), pl.BlockSpec((1, tk, dh), kmap(0)), pl.BlockSpec((1, tk, dh), kmap(nh)),
                  pl.BlockSpec((1, tk, dh), qmap), pl.BlockSpec((1, tk, dh), omap), pl.BlockSpec((1, 1, tk), rmap),
                  pl.BlockSpec((1, tk, LANES), bmap), pl.BlockSpec((1, 1, tk), rmap)],
        out_specs=[pl.BlockSpec((1, s, dh), lambda b, h, ik, iq: (b, 0, h)),
                   pl.BlockSpec((1, tk, dh), kmap(0)), pl.BlockSpec((1, tk, dh), kmap(0)),
                   pl.BlockSpec((1, tk, LANES), bmap),
                   pl.BlockSpec((1, nk, tk), lambda b, h, ik, iq: (b * nh + h, 0, 0))],
        out_shape=[jax.ShapeDtypeStruct((bl, s, da), BF16), jax.ShapeDtypeStruct((bl, s, da), BF16),
                   jax.ShapeDtypeStruct((bl, s, da), BF16), jax.ShapeDtypeStruct((bl * nh, s, LANES), F32),
                   jax.ShapeDtypeStruct((bl * nh, nk, tk), F32)],
        scratch_shapes=[pltpu.VMEM((s, dh), F32), pltpu.VMEM((tk, 2 * dh), F32), pltpu.VMEM((tk, dh), F32),
                        pltpu.VMEM((nk, tk), F32), pltpu.VMEM((nk, tk), F32)],
        compiler_params=_cp(4),
    )(q3, kv3, kv3, do3, o3, lse_row, fcol, frow)


def _merge_fwd(mg3, pr3, pa3):
    bl, s, d = pr3.shape
    ts = _tile(s, 256, SUBLANES)
    half = lambda j: pl.BlockSpec((1, ts, d), lambda b, t, j=j: (b, t, j))

    def kern(mr_ref, ma_ref, pr_ref, pa_ref, o_ref):
        o_ref[0] = (_sigmoid(mr_ref[0]) * pr_ref[0] + _sigmoid(ma_ref[0]) * pa_ref[0]).astype(BF16)

    return pl.pallas_call(
        kern, name="merge_fwd", grid=(bl, s // ts),
        in_specs=[half(0), half(1)] + _act_specs(ts, d, 2), out_specs=_act_specs(ts, d, 1)[0],
        out_shape=jax.ShapeDtypeStruct((bl, s, d), BF16), compiler_params=_cp(2),
    )(mg3, mg3, pr3, pa3)


def _merge_bwd(dm3, mg3, pr3, pa3):
    bl, s, d = pr3.shape
    ts = _tile(s, 256, SUBLANES)
    half = lambda j: pl.BlockSpec((1, ts, d), lambda b, t, j=j: (b, t, j))

    def kern(dm_ref, mr_ref, ma_ref, pr_ref, pa_ref, dpr_ref, dpa_ref, dmr_ref, dma_ref):
        dm = dm_ref[0]
        gr, ga = _sigmoid(mr_ref[0]), _sigmoid(ma_ref[0])
        dpr_ref[0] = (gr * dm).astype(BF16)
        dpa_ref[0] = (ga * dm).astype(BF16)
        dmr_ref[0] = (dm * pr_ref[0] * gr * (1.0 - gr)).astype(BF16)
        dma_ref[0] = (dm * pa_ref[0] * ga * (1.0 - ga)).astype(BF16)

    return pl.pallas_call(
        kern, name="merge_bwd", grid=(bl, s // ts),
        in_specs=_act_specs(ts, d, 1) + [half(0), half(1)] + _act_specs(ts, d, 2), out_specs=_act_specs(ts, d, 4),
        out_shape=[jax.ShapeDtypeStruct((bl, s, d), BF16)] * 4, compiler_params=_cp(2),
    )(dm3, mg3, mg3, pr3, pa3)


def _ffn_conv(gf, cw, cb):
    kw = cw.shape[0]
    y = cb
    for k in range(kw):
        y = y + _shift_down(gf, kw - 1 - k) * cw[k:k + 1, :]
    return y


def _ffn_act_fwd(up3, cw, cb):
    bl, s, two = up3.shape
    dff = two // 2
    kw = cw.shape[0]
    tc = _tile(dff, 256, LANES)
    nc = dff // tc

    def kern(gf_ref, uf_ref, cw_ref, cb_ref, o_ref):
        o_ref[0] = (_gelu(_ffn_conv(gf_ref[0], cw_ref[...], cb_ref[...])) * uf_ref[0]).astype(BF16)

    act = lambda off: pl.BlockSpec((1, s, tc), lambda b, j, off=off: (b, 0, off + j))
    return pl.pallas_call(
        kern, name="ffn_act_fwd", grid=(bl, nc),
        in_specs=[act(0), act(nc), pl.BlockSpec((kw, tc), lambda b, j: (0, j)), pl.BlockSpec((1, tc), lambda b, j: (0, j))],
        out_specs=act(0), out_shape=jax.ShapeDtypeStruct((bl, s, dff), BF16), compiler_params=_cp(2),
    )(up3, up3, cw, cb)


def _ffn_act_bwd(up3, dact3, cw, cb):
    bl, s, two = up3.shape
    dff = two // 2
    kw = cw.shape[0]
    tc = _tile(dff, 256, LANES)
    nc = dff // tc

    def kern(gf_ref, uf_ref, da_ref, cw_ref, cb_ref, dgf_ref, duf_ref, dcw_ref, dcb_ref):
        b = pl.program_id(1)
        gf, cwv, da = gf_ref[0], cw_ref[...], da_ref[0]
        ge, dge = _gelu_and_grad(_ffn_conv(gf, cwv, cb_ref[...]))
        duf_ref[0] = (da * ge).astype(BF16)
        dgc = da * uf_ref[0] * dge
        dgf = jnp.zeros_like(dgc)
        rows = []
        for k in range(kw):
            dgf = dgf + _shift_up(dgc, kw - 1 - k) * cwv[k:k + 1, :]
            rows.append(jnp.sum(dgc * _shift_down(gf, kw - 1 - k), axis=0, keepdims=True))
        dgf_ref[0] = dgf.astype(BF16)

        @pl.when(b == 0)
        def _():
            dcw_ref[...] = jnp.zeros_like(dcw_ref)
            dcb_ref[...] = jnp.zeros_like(dcb_ref)

        for k in range(kw):
            dcw_ref[k:k + 1, :] += rows[k]
        dcb_ref[...] += jnp.sum(dgc, axis=0, keepdims=True)

    act = lambda off: pl.BlockSpec((1, s, tc), lambda j, b, off=off: (b, 0, off + j))
    cws = pl.BlockSpec((kw, tc), lambda j, b: (0, j))
    cbs = pl.BlockSpec((1, tc), lambda j, b: (0, j))
    return pl.pallas_call(
        kern, name="ffn_act_bwd", grid=(nc, bl),
        in_specs=[act(0), act(nc), act(0), cws, cbs], out_specs=[act(0), act(0), cws, cbs],
        out_shape=[jax.ShapeDtypeStruct((bl, s, dff), BF16), jax.ShapeDtypeStruct((bl, s, dff), BF16),
                   jax.ShapeDtypeStruct((kw, dff), F32), jax.ShapeDtypeStruct((1, dff), F32)],
        compiler_params=_cp(2),
    )(up3, up3, dact3, cw, cb)


_HBM = pl.BlockSpec(memory_space=pltpu.HBM)


def _place():
    x, y, c = lax.axis_index("x"), lax.axis_index("y"), lax.axis_index("c")
    chips = dict(me=2 * x + y, nx=2 * (1 - x) + y, ny=2 * x + (1 - y), diag=2 * (1 - x) + (1 - y))
    peers = dict(nx=(1 - x, y, c), ny=(x, 1 - y, c), sib=(x, y, 1 - c))
    return c, chips, peers


def _remote(src, dst, sems, k, to):
    return pltpu.make_async_remote_copy(src_ref=src, dst_ref=dst, send_sem=sems[0].at[k], recv_sem=sems[1].at[k],
                                        device_id=to, device_id_type=MESH)


RS_STEPS = 2


def _piece(q, idx, n=1):
    start = idx * q
    if not isinstance(start, int):
        start = pl.multiple_of(start, SUBLANES)
    return pl.ds(start, n * q)


def _all_gather_chips(xs, name):
    nt = len(xs)
    per = 9

    def body(*refs):
        x_refs, o_refs = refs[:nt], refs[nt:2 * nt]
        send_sems, recv_sems = refs[2 * nt:]
        c, chip, peer = _place()
        sems = (send_sems, recv_sems)
        me, nx, ny, dg = chip["me"], chip["nx"], chip["ny"], chip["diag"]
        sends = []

        def arrive(k, dst):
            _remote(dst, dst, sems, k, peer["sib"]).wait_recv()

        def pass_on(k, blk, to):
            cp = _remote(blk, blk, sems, k, peer[to])
            cp.start()
            sends.append(cp)

        for t in range(nt):
            q = xs[t].shape[0] // 4
            half = _piece(q, 2 * c, 2)
            for k, to in ((0, "nx"), (1, "ny")):
                cp = _remote(x_refs[t].at[half], o_refs[t].at[me, half], sems, per * t + k, peer[to])
                cp.start()
                sends.append(cp)
            cp = _remote(x_refs[t], o_refs[t].at[me], sems, per * t + 8, peer["sib"])
            cp.start()
            sends.append(cp)
        for t in range(nt):
            q, o, k0 = xs[t].shape[0] // 4, o_refs[t], per * t
            half, sub0, sub1 = _piece(q, 2 * c, 2), _piece(q, 2 * c), _piece(q, 2 * c + 1)
            arrive(k0 + 0, o.at[nx, half])
            pass_on(k0 + 2, o.at[nx, sub0], "ny")
            pass_on(k0 + 4, o.at[nx, half], "sib")
            arrive(k0 + 1, o.at[ny, half])
            pass_on(k0 + 3, o.at[ny, sub1], "nx")
            pass_on(k0 + 5, o.at[ny, half], "sib")
            arrive(k0 + 2, o.at[dg, sub0])
            pass_on(k0 + 6, o.at[dg, sub0], "sib")
            arrive(k0 + 3, o.at[dg, sub1])
            pass_on(k0 + 7, o.at[dg, sub1], "sib")
        for t in range(nt):
            q, o, k0 = xs[t].shape[0] // 4, o_refs[t], per * t
            arrive(k0 + 4, o.at[nx, _piece(q, 2 * (1 - c), 2)])
            arrive(k0 + 5, o.at[ny, _piece(q, 2 * (1 - c), 2)])
            arrive(k0 + 6, o.at[dg, _piece(q, 2 * (1 - c))])
            arrive(k0 + 7, o.at[dg, _piece(q, 2 * (1 - c) + 1)])
            arrive(k0 + 8, o.at[me])
        for cp in sends:
            cp.wait_send()

    return pl.pallas_call(
        body, name=name, in_specs=[_HBM] * nt, out_specs=[_HBM] * nt,
        out_shape=[jax.ShapeDtypeStruct((N_CHIPS,) + x.shape, x.dtype) for x in xs],
        scratch_shapes=[pltpu.SemaphoreType.DMA((per * nt,)), pltpu.SemaphoreType.DMA((per * nt,))],
    )(*xs)


def _all_gather_devices(xs, name):
    nt = len(xs)
    per = 7

    def body(*refs):
        x_refs, o_refs = refs[:nt], refs[nt:2 * nt]
        send_sems, recv_sems, local_sems = refs[2 * nt:]
        x, y, c = lax.axis_index("x"), lax.axis_index("y"), lax.axis_index("c")
        sems = (send_sems, recv_sems)
        sib = (x, y, 1 - c)
        chips = [(1 - x, y), (x, 1 - y), (1 - x, 1 - y)]
        slot = lambda px, py, pc: 4 * px + 2 * py + pc
        me = slot(x, y, c)
        sends, copies = [], []

        def arrive(k, dst):
            _remote(dst, dst, sems, k, sib).wait_recv()

        for t in range(nt):
            cp = pltpu.make_async_copy(x_refs[t], o_refs[t].at[me], local_sems.at[t])
            cp.start()
            copies.append(cp)
            for k, to in enumerate([sib] + [(*chip, c) for chip in chips]):
                cp = _remote(x_refs[t], o_refs[t].at[me], sems, per * t + k, to)
                cp.start()
                sends.append(cp)
        for t in range(nt):
            for j, chip in enumerate(chips):
                blk = o_refs[t].at[slot(*chip, c)]
                arrive(per * t + 1 + j, blk)
                cp = _remote(blk, blk, sems, per * t + 4 + j, sib)
                cp.start()
                sends.append(cp)
        for t in range(nt):
            arrive(per * t, o_refs[t].at[slot(x, y, 1 - c)])
            for j, chip in enumerate(chips):
                arrive(per * t + 4 + j, o_refs[t].at[slot(*chip, 1 - c)])
        for cp in sends:
            cp.wait_send()
        for cp in copies:
            cp.wait()

    return pl.pallas_call(
        body, name=name, in_specs=[_HBM] * nt, out_specs=[_HBM] * nt,
        out_shape=[jax.ShapeDtypeStruct((2 * N_CHIPS,) + a.shape, a.dtype) for a in xs],
        scratch_shapes=[pltpu.SemaphoreType.DMA((per * nt,)), pltpu.SemaphoreType.DMA((per * nt,)),
                        pltpu.SemaphoreType.DMA((nt,))],
    )(*xs)


def _exchange(name, xs, out_shapes, plan):
    nt = len(xs)

    def body(*refs):
        x_refs, o_refs = refs[:nt], refs[nt:2 * nt]
        send_sems, recv_sems = refs[2 * nt:]
        c, chip, peer = _place()
        cps = []
        for t in range(nt):
            for src, dst, to in plan(c, chip, x_refs[t], o_refs[t], xs[t].shape):
                cps.append(_remote(src, dst, (send_sems, recv_sems), len(cps), peer[to]))
        for cp in cps:
            cp.start()
        for cp in cps:
            cp.wait()

    n_copies = nt * len(plan(0, dict(me=0, nx=2, ny=1, diag=3), None, None, xs[0].shape, count_only=True))
    return pl.pallas_call(
        body, name=name, in_specs=[_HBM] * nt, out_specs=[_HBM] * nt,
        out_shape=[jax.ShapeDtypeStruct(s, x.dtype) for s, x in zip(out_shapes, xs)],
        scratch_shapes=[pltpu.SemaphoreType.DMA((n_copies,)), pltpu.SemaphoreType.DMA((n_copies,))],
    )(*xs)


def _plan_sibling(c, chip, g, out, shape, count_only=False):
    if count_only:
        return [None] * N_CHIPS
    q = shape[1] // 4
    return [(g.at[j, _piece(q, 2 * (1 - c), 2)], out.at[j], "sib") for j in range(N_CHIPS)]


def _plan_first(c, chip, p, out, shape, count_only=False):
    if count_only:
        return [None] * 4
    q = shape[1] // 2
    return [(p.at[chip["nx"], _piece(q, 0)], out.at[0], "nx"), (p.at[chip["diag"], _piece(q, 0)], out.at[1], "nx"),
            (p.at[chip["ny"], _piece(q, 1)], out.at[2], "ny"), (p.at[chip["diag"], _piece(q, 1)], out.at[3], "ny")]


def _plan_second(c, chip, p, out, shape, count_only=False):
    if count_only:
        return [None] * 2
    return [(p.at[1], out.at[0], "ny"), (p.at[3], out.at[1], "nx")]


def _rs_last(ps):
    nt = len(ps)

    def body(*refs):
        p_refs, o_refs = refs[:nt], refs[nt:2 * nt]
        send_sems, recv_sems = refs[2 * nt:]
        c, _, peer = _place()
        sems = (send_sems, recv_sems)
        cps = []
        for t in range(nt):
            q = ps[t].shape[0] // 4
            mine = _piece(q, 2 * c, 2)
            cps.append(_remote(p_refs[t].at[mine], o_refs[t].at[mine], sems, t, peer["sib"]))
            cps[-1].start()
        for t in range(nt):
            q = ps[t].shape[0] // 4
            theirs = _piece(q, 2 * (1 - c), 2)
            cps[t].wait_send()
            _remote(p_refs[t].at[theirs], o_refs[t].at[theirs], sems, t, peer["sib"]).wait_recv()

    return pl.pallas_call(
        body, name="rs_last", in_specs=[_HBM] * nt, out_specs=[_HBM] * nt,
        out_shape=[jax.ShapeDtypeStruct(p.shape, F32) for p in ps],
        input_output_aliases={t: t for t in range(nt)},
        scratch_shapes=[pltpu.SemaphoreType.DMA((nt,)), pltpu.SemaphoreType.DMA((nt,))],
    )(*ps)


def _add_stage(name, grid, a_list, b_list, a_map, b_map, tbs, out_shapes, out_map, out_dtype, prefetch=None):
    nt = len(a_list)
    lead = lambda shape: (None,) * (len(shape) - 2)

    def kern(*refs):
        refs = refs[(1 if prefetch is not None else 0):]
        for t in range(nt):
            refs[2 * nt + t][...] = (refs[t][...].astype(F32) + refs[nt + t][...].astype(F32)).astype(out_dtype)

    in_specs = [pl.BlockSpec(lead(a.shape) + (tb, a.shape[-1]), a_map) for a, tb in zip(a_list, tbs)]
    in_specs += [pl.BlockSpec(lead(b.shape) + (tb, b.shape[-1]), b_map) for b, tb in zip(b_list, tbs)]
    out_specs = [pl.BlockSpec(lead(s) + (tb, s[-1]), out_map) for s, tb in zip(out_shapes, tbs)]
    out_shape = [jax.ShapeDtypeStruct(s, out_dtype) for s in out_shapes]
    if prefetch is None:
        return pl.pallas_call(kern, name=name, grid=grid, in_specs=in_specs, out_specs=out_specs, out_shape=out_shape,
                              compiler_params=_cp(len(grid)))(*a_list, *b_list)
    return pl.pallas_call(
        kern, name=name,
        grid_spec=pltpu.PrefetchScalarGridSpec(num_scalar_prefetch=1, grid=grid, in_specs=in_specs, out_specs=out_specs),
        out_shape=out_shape, compiler_params=_cp(len(grid)))(prefetch, *a_list, *b_list)


def _reduce_scatter_chips(gs):
    x, y, c = lax.axis_index("x"), lax.axis_index("y"), lax.axis_index("c")
    me, nx, ny = 2 * x + y, 2 * (1 - x) + y, 2 * x + (1 - y)
    st = RS_STEPS
    unit = 4 * st * 2 * SUBLANES
    rows = [g.shape[1] for g in gs]
    gs = [jnp.pad(g, ((0, 0), (0, -g.shape[1] % unit), (0, 0))) for g in gs]
    qs = [g.shape[1] // 4 for g in gs]
    tbs = [q // st for q in qs]
    cols = [g.shape[2] for g in gs]
    core = jnp.reshape(c, (1,)).astype(jnp.int32)

    got = _exchange("rs_sibling", gs, [(N_CHIPS, 2 * q, cc) for q, cc in zip(qs, cols)], _plan_sibling)
    p0 = _add_stage("rs_add_sibling", (N_CHIPS, 2, st), gs, got,
                    lambda j, h, s, c_ref: (j, (2 * c_ref[0] + h) * st + s, 0), lambda j, h, s, c_ref: (j, h * st + s, 0),
                    tbs, [(N_CHIPS, 2 * q, cc) for q, cc in zip(qs, cols)], lambda j, h, s, c_ref: (j, h * st + s, 0),
                    BF16, prefetch=core)
    got = _exchange("rs_first", p0, [(4, q, cc) for q, cc in zip(qs, cols)], _plan_first)
    p1 = _add_stage("rs_add_first", (4, st), p0, got,
                    lambda k, s, i_ref: (i_ref[k], (k // 2) * st + s, 0), lambda k, s, i_ref: (k, s, 0),
                    tbs, [(4, q, cc) for q, cc in zip(qs, cols)], lambda k, s, i_ref: (k, s, 0),
                    BF16, prefetch=jnp.stack([me, ny, me, nx]).astype(jnp.int32))
    got = _exchange("rs_second", p1, [(2, q, cc) for q, cc in zip(qs, cols)], _plan_second)
    p2 = _add_stage("rs_add_second", (2, st), p1, got, lambda h, s, c_ref: (2 * h, s, 0), lambda h, s, c_ref: (h, s, 0),
                    tbs, [(4 * q, cc) for q, cc in zip(qs, cols)], lambda h, s, c_ref: ((2 * c_ref[0] + h) * st + s, 0),
                    F32, prefetch=core)
    return [out[:r] for out, r in zip(_rs_last(p2), rows)]


def _adamw(g, w, m, v, name):
    rows, cc = g.shape
    tr = _tile(rows, max(SUBLANES, (1 << 18) // cc), SUBLANES)
    k1 = 1.0 - ADAM_B1 ** ADAM_STEP
    k2 = 1.0 - ADAM_B2 ** ADAM_STEP

    def kern(g_ref, w_ref, m_ref, v_ref, d_ref, nm_ref, nv_ref):
        gv = g_ref[...]
        nm = ADAM_B1 * m_ref[...] + (1.0 - ADAM_B1) * gv
        nv = ADAM_B2 * v_ref[...] + (1.0 - ADAM_B2) * (gv * gv)
        nm_ref[...] = nm
        nv_ref[...] = nv
        d_ref[...] = -ADAM_LR * ((nm / k1) / (jnp.sqrt(nv / k2) + ADAM_EPS) + ADAM_WD * w_ref[...])

    spec = pl.BlockSpec((tr, cc), lambda t: (t, 0))
    return pl.pallas_call(
        kern, name=name, grid=(rows // tr,), in_specs=[spec] * 4, out_specs=[spec] * 3,
        out_shape=[jax.ShapeDtypeStruct((rows, cc), F32)] * 3, compiler_params=_cp(1),
    )(g, w, m, v)


def _flat_pad(parts, total):
    flat = jnp.concatenate([p.reshape(-1) for p in parts])
    return jnp.pad(flat, (0, total - flat.shape[0]))


def _split_flat(flat, shapes):
    out, pos = [], 0
    for shp in shapes:
        size = math.prod(shp)
        out.append(flat[pos:pos + size].reshape(shp))
        pos += size
    return out


def _cols_of_chunks(chunks, lo, hi):
    width = chunks.shape[2]
    parts = []
    for j in range(chunks.shape[0]):
        a, b = max(lo, j * width), min(hi, (j + 1) * width)
        if a < b:
            parts.append(chunks[j, :, a - j * width:b - j * width])
    return parts[0] if len(parts) == 1 else jnp.concatenate(parts, axis=1)


def _chunks_of_cols(segments, n_chunks):
    total = sum(s.shape[1] for s in segments)
    width = total // n_chunks
    chunks = []
    for j in range(n_chunks):
        lo, hi, pos, parts = j * width, (j + 1) * width, 0, []
        for s in segments:
            a, b = max(lo, pos), min(hi, pos + s.shape[1])
            if a < b:
                parts.append(s[:, a - pos:b - pos])
            pos += s.shape[1]
        chunks.append(parts[0] if len(parts) == 1 else jnp.concatenate(parts, axis=1))
    return jnp.stack(chunks)


_WEIGHTS = ['w_ada', 'b_ada', 'g_norm1', 'w_in', 'w_rnn_conv', 'b_rnn_conv', 'w_lru_a', 'b_lru_a', 'w_lru_i', 'b_lru_i',
            'lru_lambda', 'b_fgate', 'w_proj_rnn', 'w_proj_attn', 'w_out', 'g_norm2', 'w_ffn_up', 'w_ffn_conv',
            'b_ffn_conv', 'w_ffn_down', 'w_ada_final', 'b_ada_final', 'g_final']
_MATMUL = ['w_in', 'w_proj_rnn', 'w_proj_attn', 'w_out', 'w_ffn_up', 'w_ffn_down']
_ADA = ['w_ada', 'w_ada_final']
_ADA_BIAS = ['b_ada', 'b_ada_final']
_CONV = ['w_rnn_conv', 'w_ffn_conv']
_REPLICATED = [n for n in _WEIGHTS if n not in _MATMUL + _ADA + _ADA_BIAS + _CONV]


def kernel(x, c, w_ada, b_ada, g_norm1, w_in, w_rnn_conv, b_rnn_conv, w_lru_a, b_lru_a, w_lru_i, b_lru_i, lru_lambda, b_fgate, w_proj_rnn, w_proj_attn, w_out, g_norm2, w_ffn_up, w_ffn_conv, b_ffn_conv, w_ffn_down, w_ada_final, b_ada_final, g_final, loss_target, m_w_ada, m_b_ada, m_g_norm1, m_w_in, m_w_rnn_conv, m_b_rnn_conv, m_w_lru_a, m_b_lru_a, m_w_lru_i, m_b_lru_i, m_lru_lambda, m_b_fgate, m_w_proj_rnn, m_w_proj_attn, m_w_out, m_g_norm2, m_w_ffn_up, m_w_ffn_conv, m_b_ffn_conv, m_w_ffn_down, m_w_ada_final, m_b_ada_final, m_g_final, v_w_ada, v_b_ada, v_g_norm1, v_w_in, v_w_rnn_conv, v_b_rnn_conv, v_w_lru_a, v_b_lru_a, v_w_lru_i, v_b_lru_i, v_lru_lambda, v_b_fgate, v_w_proj_rnn, v_w_proj_attn, v_w_out, v_g_norm2, v_w_ffn_up, v_w_ffn_conv, v_b_ffn_conv, v_w_ffn_down, v_w_ada_final, v_b_ada_final, v_g_final):
    args = locals()
    shape_of = {n: args[n].shape for n in _WEIGHTS}

    def view(a):
        if a.ndim >= 3:
            return a[0]
        return a[None, :] if a.ndim == 1 else a

    w2 = {n: view(args[n]) for n in _WEIGHTS}
    m2 = {n: args['m_' + n].reshape(w2[n].shape) for n in _WEIGHTS}
    v2 = {n: args['v_' + n].reshape(w2[n].shape) for n in _WEIGHTS}

    bl, s, d = x.shape
    t = bl * s
    nh = b_fgate.shape[-1]
    nb, rb = w_lru_a.shape[1], w_lru_a.shape[2]
    dr = nb * rb
    da = w2['w_proj_attn'].shape[0] * N_CHIPS
    dh = da // nh
    dff = w2['w_ffn_conv'].shape[1] * N_CHIPS
    scale = dh ** -0.5
    chip = 2 * lax.axis_index("x") + lax.axis_index("y")
    dev = 2 * chip + lax.axis_index("c")

    names = list(_MATMUL)
    gathered = dict(zip(names, _all_gather_chips([w2[n].astype(BF16) for n in names], "ag_weights")))
    n_conv = sum(w2[n].size for n in _CONV)
    rows_conv = -(-n_conv // (FLAT_COLS * 32)) * 32
    conv_local = _flat_pad([w2[n] for n in _CONV], rows_conv * FLAT_COLS).reshape(rows_conv, FLAT_COLS)
    conv_all = _all_gather_chips([conv_local], "ag_conv")[0].reshape(N_CHIPS, -1)
    conv_full, pos = {}, 0
    for n in _CONV:
        r, n4 = w2[n].shape
        blocks = conv_all[:, pos:pos + r * n4].reshape(N_CHIPS, r, n4)
        conv_full[n] = jnp.concatenate([blocks[j] for j in range(N_CHIPS)], axis=1)
        pos += r * n4
    rowmajor = lambda n: gathered[n].reshape(-1, gathered[n].shape[2])
    w_proj_rnn_f, w_proj_attn_f, w_out_f, w_ffn_down_f = (rowmajor(n) for n in ('w_proj_rnn', 'w_proj_attn', 'w_out', 'w_ffn_down'))
    up_chunk = w2['w_ffn_up'].shape[1]

    o_q, o_k, o_fl = 2 * dr, 2 * dr + da, 2 * dr + 3 * da
    o_mg = o_fl + nh
    g_in_w = gathered['w_in']
    w_rnn, w_q = _cols_of_chunks(g_in_w, 0, o_q), _cols_of_chunks(g_in_w, o_q, o_k)
    w_kv, w_mg = _cols_of_chunks(g_in_w, o_k, o_fl), _cols_of_chunks(g_in_w, o_mg, o_mg + 2 * d)
    w_fl = jnp.pad(_cols_of_chunks(g_in_w, o_fl, o_mg), ((0, 0), (0, LANES - nh)))
    w_rest = jnp.concatenate([w_q, w_kv, w_mg], axis=1)
    bf_pad = jnp.pad(w2['b_fgate'], ((0, 0), (0, LANES - nh)))

    nd = 2 * N_CHIPS
    c_act = _silu_pad(_all_gather_devices([c], "ag_cond")[0].reshape(nd * bl, d), nd * bl)
    my_cols = lambda a, n: lax.dynamic_slice_in_dim(a, chip * w2[n].shape[1], w2[n].shape[1], axis=1)
    mod_cols = [_mm(c_act, w2[n].astype(BF16), "nn", name=n + "_fwd", bias=my_cols(w2[b], n)) for n, b in zip(_ADA, _ADA_BIAS)]
    my_rows = lambda g: lax.dynamic_slice_in_dim(g, dev * bl, bl, axis=1).transpose(1, 0, 2).reshape(bl, -1)
    mod, modf = (my_rows(g) for g in _all_gather_chips(mod_cols, "ag_mod"))
    sh1, sc1, gt1, sh2, sc2, gt2 = [mod[:, i * d:(i + 1) * d].reshape(bl, 1, d) for i in range(6)]
    shf, scf = modf[:, :d].reshape(bl, 1, d), modf[:, d:].reshape(bl, 1, d)

    h1 = _norm_mod_fwd(x, w2['g_norm1'], sh1, sc1)
    h1f = h1.reshape(t, d)
    zr = _mm(h1f, w_rnn, "nn", name="in_rnn").reshape(bl, s, 2 * dr)
    q3 = _mm(h1f, w_q, "nn", name="in_q", out_dtype=BF16, scale=scale).reshape(bl, s, da)
    kv3 = _mm(h1f, w_kv, "nn", name="in_kv", out_dtype=BF16).reshape(bl, s, 2 * da)
    mg3 = _mm(h1f, w_mg, "nn", name="in_mg").reshape(bl, s, 2 * d)
    zf3 = _mm(h1f, w_fl, "nn", name="in_fl").reshape(bl, s, LANES)

    lru = (conv_full['w_rnn_conv'], w2['b_rnn_conv'], w2['w_lru_a'], w2['b_lru_a'], w2['w_lru_i'], w2['b_lru_i'], w2['lru_lambda'])
    hseq, y_rnn = _rnn_fwd(zr, *lru)

    f3 = _fgate_fwd(zf3, bf_pad)
    f_heads = f3[:, :, :nh].transpose(0, 2, 1).reshape(bl * nh, s)
    fcol = jnp.broadcast_to(f_heads[:, :, None], (bl * nh, s, LANES))
    frow = f_heads.reshape(bl * nh, 1, s)
    o3, lse_row = _attn_fwd(q3, kv3, fcol, frow, nh)

    pr3 = _mm(y_rnn.reshape(t, dr), w_proj_rnn_f, "nn", name="proj_rnn").reshape(bl, s, d)
    pa3 = _mm(o3.reshape(t, da), w_proj_attn_f, "nn", name="proj_attn").reshape(bl, s, d)
    merged = _merge_fwd(mg3, pr3, pa3)
    mo3 = _mm(merged.reshape(t, d), w_out_f, "nn", name="mix_out").reshape(bl, s, d)
    x1, h2 = _resid_norm_fwd(x, mo3, gt1, w2['g_norm2'], sh2, sc2)
    h2f = h2.reshape(t, d)
    up3 = _mm(h2f, gathered['w_ffn_up'], "nn", name="ffn_up", b_chunk=up_chunk).reshape(bl, s, 2 * dff)
    act3 = _ffn_act_fwd(up3, conv_full['w_ffn_conv'], w2['b_ffn_conv'])
    yf3 = _mm(act3.reshape(t, dff), w_ffn_down_f, "nn", name="ffn_down").reshape(bl, s, d)

    dx2, dshf, dscf, dg_final, loss_part = _final_fwd_bwd(x1, yf3, gt2, w2['g_final'], shf, scf, loss_target)
    loss = lax.psum(loss_part[0, 0], ("x", "y", "c"))

    dyf, dgt2 = _gate_bwd(dx2, yf3, gt2, "ffn_gate_bwd")
    dyf_f = dyf.reshape(t, d)
    g_ffn_down = _mm(act3.reshape(t, dff), dyf_f, "tn", name="dw_ffn_down")
    dact3 = _mm(dyf_f, w_ffn_down_f, "nt", name="d_ffn_act").reshape(bl, s, dff)
    dgf, duf, g_ffn_conv, g_b_ffn_conv = _ffn_act_bwd(up3, dact3, conv_full['w_ffn_conv'], w2['b_ffn_conv'])
    dgf_f, duf_f = dgf.reshape(t, dff), duf.reshape(t, dff)
    g_ffn_up = jnp.concatenate([_mm(h2f, dgf_f, "tn", name="dw_ffn_up_gate", out_chunk=up_chunk),
                                _mm(h2f, duf_f, "tn", name="dw_ffn_up_value", out_chunk=up_chunk)], axis=0)
    dh2 = _mm([dgf_f, duf_f], gathered['w_ffn_up'], "nt", name="d_h2", b_chunk=up_chunk).reshape(bl, s, d)
    dx1, dsh2, dsc2, dg_norm2 = _norm_mod_bwd(dh2, x1, dx2, w2['g_norm2'], sc2, "norm2_bwd")

    dmo, dgt1 = _gate_bwd(dx1, mo3, gt1, "mix_gate_bwd")
    dmo_f = dmo.reshape(t, d)
    g_out = _mm(merged.reshape(t, d), dmo_f, "tn", name="dw_out")
    dm3 = _mm(dmo_f, w_out_f, "nt", name="d_merged").reshape(bl, s, d)
    dpr, dpa, dmr, dma = _merge_bwd(dm3, mg3, pr3, pa3)
    g_proj_rnn = _mm(y_rnn.reshape(t, dr), dpr.reshape(t, d), "tn", name="dw_proj_rnn")
    g_proj_attn = _mm(o3.reshape(t, da), dpa.reshape(t, d), "tn", name="dw_proj_attn")
    dyr3 = _mm(dpr.reshape(t, d), w_proj_rnn_f, "nt", name="d_y_rnn").reshape(bl, s, dr)
    do3 = _mm(dpa.reshape(t, d), w_proj_attn_f, "nt", name="d_y_attn").reshape(bl, s, da)

    dq3, dk3, dv3, dfk, dfq = _attn_bwd(q3, kv3, do3, o3, lse_row, fcol, frow, nh, scale)
    heads_last = lambda a: jnp.pad(a.reshape(bl, nh, s).transpose(0, 2, 1), ((0, 0), (0, 0), (0, LANES - nh)))
    dzf3, g_bf = _fgate_bwd(heads_last(dfk[:, :, 0]), heads_last(dfq), zf3, bf_pad)

    dxr, dgr, g_rnn_conv, g_b_rnn_conv, g_lru_a, g_b_lru_a, g_lru_i, g_b_lru_i, g_lam = _rnn_bwd(zr, hseq, dyr3, *lru)

    dz = [a.reshape(t, -1) for a in (dxr, dgr, dq3, dk3, dv3, dmr, dma)]
    dzf_f = dzf3.reshape(t, LANES)
    seg_names = ("xr", "gr", "q", "k", "v", "mr", "ma")
    g_seg = [_mm(h1f, a, "tn", name="dw_in_" + n) for n, a in zip(seg_names, dz)]
    g_in_fl = _mm(h1f, dzf_f, "tn", name="dw_in_fl")[:, :nh]
    g_in = _chunks_of_cols(g_seg[:5] + [g_in_fl] + g_seg[5:], N_CHIPS)
    dh1 = _mm(dzf_f, w_fl, "nt", name="d_h1_fl")
    dh1 = _mm(dz[:2], w_rnn, "nt", name="d_h1_rnn", add=dh1)
    dh1 = _mm(dz[2:], w_rest, "nt", name="d_h1", add=dh1).reshape(bl, s, d)
    grad_x, dsh1, dsc1, dg_norm1 = _norm_mod_bwd(dh1, x, dx1, w2['g_norm1'], sc1, "norm1_bwd")

    dmods = [jnp.concatenate([dsh1, dsc1, dgt1, dsh2, dsc2, dgt2], axis=-1).reshape(bl, -1),
             jnp.concatenate([dshf, dscf], axis=-1).reshape(bl, -1)]
    dmods = [g.reshape(nd * bl, -1) for g in _all_gather_devices(dmods, "ag_dmod")]
    grad = {n: _mm(c_act, my_cols(g, n), "tn", name="dw_" + n) for n, g in zip(_ADA, dmods)}
    grad.update({b: _rowsum(g, "d" + b) for b, g in zip(_ADA_BIAS, dmods)})

    rowchunks = lambda g: g.reshape(N_CHIPS, g.shape[0] // N_CHIPS, g.shape[1])
    full = dict(w_in=g_in, w_proj_rnn=rowchunks(g_proj_rnn), w_proj_attn=rowchunks(g_proj_attn),
                w_out=rowchunks(g_out), w_ffn_up=g_ffn_up, w_ffn_down=rowchunks(g_ffn_down))
    small = dict(g_norm1=dg_norm1, w_rnn_conv=g_rnn_conv, b_rnn_conv=g_b_rnn_conv, w_lru_a=g_lru_a,
                 b_lru_a=g_b_lru_a, w_lru_i=g_lru_i, b_lru_i=g_b_lru_i, lru_lambda=g_lam, b_fgate=g_bf[:, :nh],
                 g_norm2=dg_norm2, w_ffn_conv=g_ffn_conv, b_ffn_conv=g_b_ffn_conv, g_final=dg_final)

    small_names = _REPLICATED + _CONV
    n_small = sum(small[n].size for n in small_names)
    rows_q = -(-n_small // (N_CHIPS * FLAT_COLS * 32 * RS_STEPS)) * 32 * RS_STEPS
    small_flat = _flat_pad([small[n] for n in small_names], N_CHIPS * rows_q * FLAT_COLS).reshape(N_CHIPS, rows_q, FLAT_COLS)
    reduced = _reduce_scatter_chips([full[n] for n in names] + [small_flat])
    grad.update(zip(names, reduced[:-1]))
    small_all = _all_gather_chips([reduced[-1]], "ag_small_grads")[0].reshape(-1)
    grad.update(zip(small_names, _split_flat(small_all, [small[n].shape for n in small_names])))
    for n in _CONV:
        n4 = w2[n].shape[1]
        grad[n] = lax.dynamic_slice_in_dim(grad[n], chip * n4, n4, axis=1)

    delta_w, new_m, new_v = {}, {}, {}
    for n in names + _ADA:
        delta_w[n], new_m[n], new_v[n] = _adamw(grad[n], w2[n], m2[n], v2[n], "adamw_" + n)
    small_names = small_names + _ADA_BIAS
    rows_small = -(-sum(w2[n].size for n in small_names) // (FLAT_COLS * SUBLANES)) * SUBLANES
    flat_small = lambda src: _flat_pad([src[n] for n in small_names], rows_small * FLAT_COLS).reshape(rows_small, FLAT_COLS)
    small_out = _adamw(flat_small(grad), flat_small(w2), flat_small(m2), flat_small(v2), "adamw_small")
    for dst, flat in zip((delta_w, new_m, new_v), small_out):
        dst.update(zip(small_names, _split_flat(flat.reshape(-1), [w2[n].shape for n in small_names])))

    out = [loss, grad_x]
    for src in (grad, delta_w, new_m, new_v):
        out += [src[n].reshape(shape_of[n]) for n in _WEIGHTS]
    return tuple(out)
```

```python
import functools
import math

import jax
import jax.numpy as jnp
from jax import lax
from jax.experimental import pallas as pl
from jax.experimental.pallas import tpu as pltpu

F32 = jnp.float32
BF16 = jnp.bfloat16
MESH = pl.DeviceIdType.MESH

RMS_EPS = 1e-6
LRU_C = 8.0
ADAM_LR = 0.001
ADAM_B1 = 0.9
ADAM_B2 = 0.999
ADAM_EPS = 1e-08
ADAM_WD = 0.01
ADAM_STEP = 10

LANES = 128
SUBLANES = 8
N_CHIPS = 4
FLAT_COLS = 1024
SCAN_SEGMENTS = 2 * SUBLANES
VMEM_LIMIT = 48 * 1024 * 1024
NEG_BIG = -1e30


def _cp(n_axes):
    return pltpu.CompilerParams(dimension_semantics=("arbitrary",) * n_axes, vmem_limit_bytes=VMEM_LIMIT)


def _tile(n, target, align):
    if n <= target:
        return n
    t = (target // align) * align
    while t >= align:
        if n % t == 0:
            return t
        t -= align
    return n


def _nice_rows(n, align):
    r = -(-n // align) * align
    while True:
        if r <= 640:
            return r, r
        t = _tile(r, 640, align)
        if 128 <= t <= 640:
            return r, t
        r += align


def _sigmoid(x):
    return jax.nn.sigmoid(x)


def _softplus(x):
    return jnp.maximum(x, 0.0) + jnp.log1p(jnp.exp(-jnp.abs(x)))


def _expm1(x, exp_x):
    small = x * (1.0 + 0.5 * x * (1.0 + (1.0 / 3.0) * x * (1.0 + 0.25 * x)))
    return jnp.where(jnp.abs(x) < 0.05, small, exp_x - 1.0)


_GELU_K = math.sqrt(2.0 / math.pi)
_GELU_C = 0.044715


def _gelu(x):
    t = jnp.tanh(_GELU_K * (x + _GELU_C * x * x * x))
    return 0.5 * x * (1.0 + t)


def _gelu_and_grad(x):
    t = jnp.tanh(_GELU_K * (x + _GELU_C * x * x * x))
    g = 0.5 * x * (1.0 + t)
    dg = 0.5 * (1.0 + t) + 0.5 * x * (1.0 - t * t) * _GELU_K * (1.0 + 3.0 * _GELU_C * x * x)
    return g, dg


def _shift_down(x, k):
    if k == 0:
        return x
    rows = lax.broadcasted_iota(jnp.int32, x.shape, 0)
    return jnp.where(rows >= k, pltpu.roll(x, k, 0), 0.0)


def _shift_up(x, k):
    if k == 0:
        return x
    s = x.shape[0]
    rows = lax.broadcasted_iota(jnp.int32, x.shape, 0)
    return jnp.where(rows < s - k, pltpu.roll(x, s - k, 0), 0.0)


def _dot(a, b, dims):
    return lax.dot_general(a.astype(BF16), b.astype(BF16), (dims, ((), ())), preferred_element_type=F32)


_NN = ((1,), (0,))
_NT = ((1,), (1,))
_TN = ((0,), (0,))


def _mm(a, b, mode, *, name, out_dtype=F32, scale=None, bias=None, add=None, tm=1024, tn=1024, tk=1024,
        b_chunk=None, out_chunk=None):
    pieces = list(a) if isinstance(a, (list, tuple)) else [a]
    ksize = lambda p: p.shape[0] if mode == "tn" else p.shape[1]
    if b_chunk is None:
        brows, bcols = b.shape
    else:
        brows, bcols = b.shape[1], b.shape[0] * b_chunk
    k = sum(ksize(p) for p in pieces)
    if mode == "nt":
        m, n = pieces[0].shape[0], brows
        assert bcols == k, (bcols, k)
    else:
        m, n = (pieces[0].shape[1] if mode == "tn" else pieces[0].shape[0]), bcols
        assert brows == k, (brows, k)
    tm = _tile(m, tm, LANES)
    ncut = n
    if b_chunk is not None and mode != "nt":
        ncut = b_chunk
    if out_chunk is not None:
        ncut = math.gcd(ncut, out_chunk)
    tn = _tile(ncut, tn, LANES)
    kcut = b_chunk if (b_chunk is not None and mode == "nt") else k
    for p in pieces:
        kcut = math.gcd(kcut, ksize(p))
    tk = _tile(kcut, tk, LANES)
    nk = k // tk
    dims = {"nn": _NN, "nt": _NT, "tn": _TN}[mode]
    counts = [ksize(p) // tk for p in pieces]
    starts = [sum(counts[:i]) for i in range(len(pieces))]
    n_pieces = len(pieces)

    def a_spec(s0, cnt):
        kmap = (lambda kk: kk) if n_pieces == 1 else (lambda kk: jnp.clip(kk - s0, 0, cnt - 1))
        if mode == "tn":
            return pl.BlockSpec((tk, tm), lambda i, j, kk: (kmap(kk), i))
        return pl.BlockSpec((tm, tk), lambda i, j, kk: (i, kmap(kk)))

    if b_chunk is None:
        if mode == "nt":
            b_spec = pl.BlockSpec((tn, tk), lambda i, j, kk: (j, kk))
        else:
            b_spec = pl.BlockSpec((tk, tn), lambda i, j, kk: (kk, j))
    elif mode == "nt":
        per_b = b_chunk // tk
        b_spec = pl.BlockSpec((None, tn, tk), lambda i, j, kk: (kk // per_b, j, kk % per_b))
    else:
        per_b = b_chunk // tn
        b_spec = pl.BlockSpec((None, tk, tn), lambda i, j, kk: (j // per_b, kk, j % per_b))
    if out_chunk is None:
        out_spec = pl.BlockSpec((tm, tn), lambda i, j, kk: (i, j))
        out_shape = jax.ShapeDtypeStruct((m, n), out_dtype)
    else:
        per_o = out_chunk // tn
        out_spec = pl.BlockSpec((None, tm, tn), lambda i, j, kk: (j // per_o, i, j % per_o))
        out_shape = jax.ShapeDtypeStruct((n // out_chunk, m, out_chunk), out_dtype)
    in_specs = [a_spec(s0, cnt) for s0, cnt in zip(starts, counts)] + [b_spec]
    args = pieces + [b]
    if bias is not None:
        in_specs.append(pl.BlockSpec((1, tn), lambda i, j, kk: (0, j)))
        args.append(bias)
    if add is not None:
        in_specs.append(pl.BlockSpec((tm, tn), lambda i, j, kk: (i, j)))
        args.append(add)

    def kern(*refs):
        b_ref = refs[n_pieces]
        o_ref = refs[n_pieces + 1 + (bias is not None) + (add is not None)]

        def finish(r):
            if scale is not None:
                r = r * scale
            pos = n_pieces + 1
            if bias is not None:
                r = r + refs[pos][...]
                pos += 1
            if add is not None:
                r = r + refs[pos][...]
            o_ref[...] = r.astype(out_dtype)

        if nk == 1:
            finish(_dot(refs[0][...], b_ref[...], dims))
            return
        acc = refs[-1]
        kk = pl.program_id(2)

        @pl.when(kk == 0)
        def _():
            acc[...] = jnp.zeros_like(acc)

        if n_pieces == 1:
            acc[...] += _dot(refs[0][...], b_ref[...], dims)
        else:
            for idx in range(n_pieces):
                @pl.when((kk >= starts[idx]) & (kk < starts[idx] + counts[idx]))
                def _(idx=idx):
                    acc[...] += _dot(refs[idx][...], b_ref[...], dims)

        @pl.when(kk == nk - 1)
        def _():
            finish(acc[...])

    return pl.pallas_call(
        kern, name=name,
        grid=(m // tm, n // tn, nk),
        in_specs=in_specs, out_specs=out_spec, out_shape=out_shape,
        scratch_shapes=[pltpu.VMEM((tm, tn), F32)] if nk > 1 else [],
        compiler_params=_cp(3),
    )(*args)


def _silu_pad(c, rows):
    bl, d = c.shape

    def kern(c_ref, o_ref):
        o_ref[...] = jnp.zeros_like(o_ref)
        v = c_ref[...]
        o_ref[0:bl, :] = v * _sigmoid(v)

    return pl.pallas_call(kern, name="silu_pad", out_shape=jax.ShapeDtypeStruct((rows, d), F32))(c)


def _rowsum(x, name):
    r, n = x.shape

    def kern(x_ref, o_ref):
        o_ref[...] = jnp.sum(x_ref[...], axis=0, keepdims=True)

    return pl.pallas_call(kern, name=name, out_shape=jax.ShapeDtypeStruct((1, n), F32))(x)


def _norm_parts(x, g):
    r = lax.rsqrt(jnp.mean(x * x, axis=-1, keepdims=True) + RMS_EPS)
    xh = x * r
    return r, xh, xh * g


def _norm_bwd_parts(dh, xh, r, g, sc):
    n = xh * g
    dn = dh * (1.0 + sc)
    dxh = dn * g
    dx = r * (dxh - xh * jnp.mean(dxh * xh, axis=-1, keepdims=True))
    return dx, dh, dh * n, dn * xh


def _act_specs(ts, d, n):
    return [pl.BlockSpec((1, ts, d), lambda b, t: (b, t, 0)) for _ in range(n)]


def _vec_spec(d):
    return pl.BlockSpec((1, 1, d), lambda b, t: (b, 0, 0))


def _par_spec(d):
    return pl.BlockSpec((1, d), lambda b, t: (0, 0))


def _norm_mod_fwd(x3, g, sh, sc):
    bl, s, d = x3.shape
    ts = _tile(s, 512, SUBLANES)

    def kern(x_ref, g_ref, sh_ref, sc_ref, h_ref):
        _, _, n = _norm_parts(x_ref[0], g_ref[...])
        h_ref[0] = (n * (1.0 + sc_ref[0]) + sh_ref[0]).astype(BF16)

    return pl.pallas_call(
        kern, name="norm_mod_fwd", grid=(bl, s // ts),
        in_specs=_act_specs(ts, d, 1) + [_par_spec(d), _vec_spec(d), _vec_spec(d)],
        out_specs=_act_specs(ts, d, 1)[0],
        out_shape=jax.ShapeDtypeStruct((bl, s, d), BF16),
        compiler_params=_cp(2),
    )(x3, g, sh, sc)


def _resid_norm_fwd(x3, y3, gate, g, sh, sc):
    bl, s, d = x3.shape
    ts = _tile(s, 512, SUBLANES)

    def kern(x_ref, y_ref, gate_ref, g_ref, sh_ref, sc_ref, x1_ref, h_ref):
        x1 = x_ref[0] + gate_ref[0] * y_ref[0]
        x1_ref[0] = x1
        _, _, n = _norm_parts(x1, g_ref[...])
        h_ref[0] = (n * (1.0 + sc_ref[0]) + sh_ref[0]).astype(BF16)

    return pl.pallas_call(
        kern, name="resid_norm_fwd", grid=(bl, s // ts),
        in_specs=_act_specs(ts, d, 2) + [_vec_spec(d), _par_spec(d), _vec_spec(d), _vec_spec(d)],
        out_specs=_act_specs(ts, d, 2),
        out_shape=[jax.ShapeDtypeStruct((bl, s, d), F32), jax.ShapeDtypeStruct((bl, s, d), BF16)],
        compiler_params=_cp(2),
    )(x3, y3, gate, g, sh, sc)


def _norm_mod_bwd(dh3, x3, dres3, g, sc, name):
    bl, s, d = x3.shape
    ts = _tile(s, 512, SUBLANES)

    def kern(dh_ref, x_ref, dres_ref, g_ref, sc_ref, dx_ref, dsh_ref, dsc_ref, dg_ref):
        b, t = pl.program_id(0), pl.program_id(1)
        gv = g_ref[...]
        r, xh, _ = _norm_parts(x_ref[0], gv)
        dx, a, bb, cc = _norm_bwd_parts(dh_ref[0], xh, r, gv, sc_ref[0])
        dx_ref[0] = dres_ref[0] + dx

        @pl.when(t == 0)
        def _():
            dsh_ref[...] = jnp.zeros_like(dsh_ref)
            dsc_ref[...] = jnp.zeros_like(dsc_ref)

        @pl.when((t == 0) & (b == 0))
        def _():
            dg_ref[...] = jnp.zeros_like(dg_ref)

        dsh_ref[0] += jnp.sum(a, axis=0, keepdims=True)
        dsc_ref[0] += jnp.sum(bb, axis=0, keepdims=True)
        dg_ref[...] += jnp.sum(cc, axis=0, keepdims=True)

    return pl.pallas_call(
        kern, name=name, grid=(bl, s // ts),
        in_specs=_act_specs(ts, d, 3) + [_par_spec(d), _vec_spec(d)],
        out_specs=[_act_specs(ts, d, 1)[0], _vec_spec(d), _vec_spec(d), _par_spec(d)],
        out_shape=[jax.ShapeDtypeStruct((bl, s, d), F32), jax.ShapeDtypeStruct((bl, 1, d), F32),
                   jax.ShapeDtypeStruct((bl, 1, d), F32), jax.ShapeDtypeStruct((1, d), F32)],
        compiler_params=_cp(2),
    )(dh3, x3, dres3, g, sc)


def _gate_bwd(dx3, y3, gate, name):
    bl, s, d = dx3.shape
    ts = _tile(s, 512, SUBLANES)

    def kern(dx_ref, y_ref, gate_ref, dy_ref, dgate_ref):
        t = pl.program_id(1)
        dx = dx_ref[0]
        dy_ref[0] = (gate_ref[0] * dx).astype(BF16)

        @pl.when(t == 0)
        def _():
            dgate_ref[...] = jnp.zeros_like(dgate_ref)

        dgate_ref[0] += jnp.sum(dx * y_ref[0], axis=0, keepdims=True)

    return pl.pallas_call(
        kern, name=name, grid=(bl, s // ts),
        in_specs=_act_specs(ts, d, 2) + [_vec_spec(d)],
        out_specs=[_act_specs(ts, d, 1)[0], _vec_spec(d)],
        out_shape=[jax.ShapeDtypeStruct((bl, s, d), BF16), jax.ShapeDtypeStruct((bl, 1, d), F32)],
        compiler_params=_cp(2),
    )(dx3, y3, gate)


def _final_fwd_bwd(x1, yf, gate2, g, shf, scf, tgt):
    bl, s, d = x1.shape
    ts = _tile(s, 512, SUBLANES)

    def kern(x1_ref, yf_ref, gate_ref, g_ref, sh_ref, sc_ref, tgt_ref, dx_ref, dsh_ref, dsc_ref, dg_ref, loss_ref):
        b, t = pl.program_id(0), pl.program_id(1)
        gv, sc = g_ref[...], sc_ref[0]
        x2 = x1_ref[0] + gate_ref[0] * yf_ref[0]
        r, xh, n = _norm_parts(x2, gv)
        err = n * (1.0 + sc) + sh_ref[0] - tgt_ref[0]
        dx, a, bb, cc = _norm_bwd_parts(err * (1.0 / d), xh, r, gv, sc)
        dx_ref[0] = dx

        @pl.when(t == 0)
        def _():
            dsh_ref[...] = jnp.zeros_like(dsh_ref)
            dsc_ref[...] = jnp.zeros_like(dsc_ref)

        @pl.when((t == 0) & (b == 0))
        def _():
            dg_ref[...] = jnp.zeros_like(dg_ref)
            loss_ref[...] = jnp.zeros_like(loss_ref)

        dsh_ref[0] += jnp.sum(a, axis=0, keepdims=True)
        dsc_ref[0] += jnp.sum(bb, axis=0, keepdims=True)
        dg_ref[...] += jnp.sum(cc, axis=0, keepdims=True)
        tok = jnp.mean(err * err, axis=-1, keepdims=True)
        loss_ref[...] += 0.5 * jnp.sum(tok, axis=0, keepdims=True)

    return pl.pallas_call(
        kern, name="final_fwd_bwd", grid=(bl, s // ts),
        in_specs=_act_specs(ts, d, 2) + [_vec_spec(d), _par_spec(d), _vec_spec(d), _vec_spec(d)] + _act_specs(ts, d, 1),
        out_specs=[_act_specs(ts, d, 1)[0], _vec_spec(d), _vec_spec(d), _par_spec(d),
                   pl.BlockSpec((1, 1), lambda b, t: (0, 0))],
        out_shape=[jax.ShapeDtypeStruct((bl, s, d), F32), jax.ShapeDtypeStruct((bl, 1, d), F32),
                   jax.ShapeDtypeStruct((bl, 1, d), F32), jax.ShapeDtypeStruct((1, d), F32),
                   jax.ShapeDtypeStruct((1, 1), F32)],
        compiler_params=_cp(2),
    )(x1, yf, gate2, g, shf, scf, tgt)


def _rnn_gates(xr, cw, cb, wa, ba, wi, bi, lam):
    kw = cw.shape[0]
    xc = cb
    for k in range(kw):
        xc = xc + _shift_down(xr, kw - 1 - k) * cw[k:k + 1, :]
    r = _sigmoid(_dot(xc, wa, _NN) + ba)
    i = _sigmoid(_dot(xc, wi, _NN) + bi)
    sp = _softplus(-lam)
    log_a = -LRU_C * r * sp
    a = jnp.exp(log_a)
    mult = jnp.sqrt(-_expm1(2.0 * log_a, a * a))
    return xc, r, i, sp, a, mult


def _segment_scan(a_s, u_s, h_s, p_s, reverse):
    s, c = a_s.shape
    seg = s // SCAN_SEGMENTS

    unroll = math.gcd(seg, 8)

    def steps(n, carry):
        h, p = carry
        for j in range(unroll):
            t = n * unroll + j
            t = (seg - 1 - t) if reverse else t
            av = a_s[pl.ds(t, SCAN_SEGMENTS, stride=seg), :]
            uv = u_s[pl.ds(t, SCAN_SEGMENTS, stride=seg), :]
            h = av * h + uv
            p = p * av
            h_s[pl.ds(t, SCAN_SEGMENTS, stride=seg), :] = h
            p_s[pl.ds(t, SCAN_SEGMENTS, stride=seg), :] = p
        return h, p

    lax.fori_loop(0, seg // unroll, steps, (jnp.zeros((SCAN_SEGMENTS, c), F32), jnp.ones((SCAN_SEGMENTS, c), F32)))
    carry = jnp.zeros((1, c), F32)
    order = range(SCAN_SEGMENTS - 1, -1, -1) if reverse else range(SCAN_SEGMENTS)
    for j in order:
        rows = pl.ds(j * seg, seg)
        fixed = h_s[rows, :] + p_s[rows, :] * carry
        h_s[rows, :] = fixed
        carry = fixed[0:1, :] if reverse else fixed[seg - 1:seg, :]


def _rnn_specs(s, rb, nb):
    act = lambda off: pl.BlockSpec((1, s, rb), lambda b, n, off=off: (b, 0, off + n))
    par = pl.BlockSpec((1, rb), lambda b, n: (0, n))
    wsp = pl.BlockSpec((1, rb, rb), lambda b, n: (n, 0, 0))
    return act, par, wsp


def _rnn_fwd(zr3, cw, cb, wa, ba, wi, bi, lam):
    bl, s, two = zr3.shape
    nb, rb, _ = wa.shape
    dr = nb * rb
    kw = cw.shape[0]
    act, par, wsp = _rnn_specs(s, rb, nb)

    def kern(xr_ref, gr_ref, cw_ref, cb_ref, wa_ref, ba_ref, wi_ref, bi_ref, lam_ref, h_ref, y_ref, a_s, u_s, h_s, p_s):
        xc, r, i, sp, a, mult = _rnn_gates(xr_ref[0], cw_ref[...], cb_ref[...], wa_ref[0], ba_ref[...],
                                           wi_ref[0], bi_ref[...], lam_ref[...])
        a_s[...] = a
        u_s[...] = mult * (i * xc)
        _segment_scan(a_s, u_s, h_s, p_s, reverse=False)
        h = h_s[...]
        h_ref[0] = h
        y_ref[0] = (_gelu(gr_ref[0]) * h).astype(BF16)

    return pl.pallas_call(
        kern, name="rnn_fwd", grid=(bl, nb),
        in_specs=[act(0), act(nb), pl.BlockSpec((kw, rb), lambda b, n: (0, n)), par, wsp, par, wsp, par, par],
        out_specs=[act(0), act(0)],
        out_shape=[jax.ShapeDtypeStruct((bl, s, dr), F32), jax.ShapeDtypeStruct((bl, s, dr), BF16)],
        scratch_shapes=[pltpu.VMEM((s, rb), F32)] * 4,
        compiler_params=_cp(2),
    )(zr3, zr3, cw, cb, wa, ba, wi, bi, lam)


def _rnn_bwd(zr3, h3, dy3, cw, cb, wa, ba, wi, bi, lam):
    bl, s, _ = zr3.shape
    nb, rb, _ = wa.shape
    dr = nb * rb
    kw = cw.shape[0]
    act = lambda off: pl.BlockSpec((1, s, rb), lambda n, b, off=off: (b, 0, off + n))
    par = pl.BlockSpec((1, rb), lambda n, b: (0, n))
    wsp = pl.BlockSpec((1, rb, rb), lambda n, b: (n, 0, 0))
    cws = pl.BlockSpec((kw, rb), lambda n, b: (0, n))

    def kern(xr_ref, gr_ref, h_ref, dy_ref, cw_ref, cb_ref, wa_ref, ba_ref, wi_ref, bi_ref, lam_ref,
             dxr_ref, dgr_ref, dcw_ref, dcb_ref, dwa_ref, dba_ref, dwi_ref, dbi_ref, dlam_ref, a_s, u_s, h_s, p_s):
        b = pl.program_id(1)
        xr, cwv, lamv = xr_ref[0], cw_ref[...], lam_ref[...]
        wav, wiv = wa_ref[0], wi_ref[0]
        xc, r, i, sp, a, mult = _rnn_gates(xr, cwv, cb_ref[...], wav, ba_ref[...], wiv, bi_ref[...], lamv)
        h, dy = h_ref[0], dy_ref[0]
        ge, dge = _gelu_and_grad(gr_ref[0])
        dgr_ref[0] = (dy * h * dge).astype(BF16)
        a_s[...] = _shift_up(a, 1)
        u_s[...] = dy * ge
        _segment_scan(a_s, u_s, h_s, p_s, reverse=True)
        g = h_s[...]
        da = g * _shift_down(h, 1)
        ix = i * xc
        dlog_a = da * a + (g * ix) * (-(a * a) / mult)
        di = g * mult * xc
        dpa = (dlog_a * (-LRU_C * sp)) * r * (1.0 - r)
        dpi = di * i * (1.0 - i)
        dxc = g * mult * i + _dot(dpa, wav, _NT) + _dot(dpi, wiv, _NT)
        dxr = jnp.zeros_like(dxc)
        dcw_rows = []
        for k in range(kw):
            dxr = dxr + _shift_up(dxc, kw - 1 - k) * cwv[k:k + 1, :]
            dcw_rows.append(jnp.sum(dxc * _shift_down(xr, kw - 1 - k), axis=0, keepdims=True))
        dxr_ref[0] = dxr.astype(BF16)

        @pl.when(b == 0)
        def _():
            for ref in (dcw_ref, dcb_ref, dwa_ref, dba_ref, dwi_ref, dbi_ref, dlam_ref):
                ref[...] = jnp.zeros_like(ref)

        for k in range(kw):
            dcw_ref[k:k + 1, :] += dcw_rows[k]
        dcb_ref[...] += jnp.sum(dxc, axis=0, keepdims=True)
        dwa_ref[0] += _dot(xc, dpa, _TN)
        dwi_ref[0] += _dot(xc, dpi, _TN)
        dba_ref[...] += jnp.sum(dpa, axis=0, keepdims=True)
        dbi_ref[...] += jnp.sum(dpi, axis=0, keepdims=True)
        dsp = jnp.sum(dlog_a * (-LRU_C * r), axis=0, keepdims=True)
        dlam_ref[...] += dsp * (-_sigmoid(-lamv))

    vec = jax.ShapeDtypeStruct((1, dr), F32)
    wsh = jax.ShapeDtypeStruct((nb, rb, rb), F32)
    return pl.pallas_call(
        kern, name="rnn_bwd", grid=(nb, bl),
        in_specs=[act(0), act(nb), act(0), act(0), cws, par, wsp, par, wsp, par, par],
        out_specs=[act(0), act(0), cws, par, wsp, par, wsp, par, par],
        out_shape=[jax.ShapeDtypeStruct((bl, s, dr), BF16), jax.ShapeDtypeStruct((bl, s, dr), BF16),
                   jax.ShapeDtypeStruct((kw, dr), F32), vec, wsh, vec, wsh, vec, vec],
        scratch_shapes=[pltpu.VMEM((s, rb), F32)] * 4,
        compiler_params=_cp(2),
    )(zr3, zr3, h3, dy3, cw, cb, wa, ba, wi, bi, lam)


def _tri(n, upper):
    r = lax.broadcasted_iota(jnp.int32, (n, n), 0)
    c = lax.broadcasted_iota(jnp.int32, (n, n), 1)
    return jnp.where((c >= r) if upper else (c <= r), 1.0, 0.0).astype(F32)


def _fgate_fwd(zf3, bf):
    bl, s, w = zf3.shape
    ch = _tile(s, 256, SUBLANES)

    def kern(z_ref, b_ref, f_ref):
        tri = _tri(ch, upper=False)
        carry = jnp.zeros((1, w), F32)
        for j in range(s // ch):
            rows = pl.ds(j * ch, ch)
            lf = -_softplus(-(z_ref[0, rows, :] + b_ref[...]))
            out = jnp.dot(tri, lf, precision=lax.Precision.HIGHEST, preferred_element_type=F32) + carry
            f_ref[0, rows, :] = out
            carry = out[ch - 1:ch, :]

    return pl.pallas_call(
        kern, name="fgate_fwd", grid=(bl,),
        in_specs=[pl.BlockSpec((1, s, w), lambda b: (b, 0, 0)), pl.BlockSpec((1, w), lambda b: (0, 0))],
        out_specs=pl.BlockSpec((1, s, w), lambda b: (b, 0, 0)),
        out_shape=jax.ShapeDtypeStruct((bl, s, w), F32),
        compiler_params=_cp(1),
    )(zf3, bf)


def _fgate_bwd(dfk3, dfq3, zf3, bf):
    bl, s, w = zf3.shape
    ch = _tile(s, 256, SUBLANES)

    def kern(dfk_ref, dfq_ref, z_ref, b_ref, dz_ref, db_ref):
        b = pl.program_id(0)
        tri = _tri(ch, upper=True)
        carry = jnp.zeros((1, w), F32)
        dbsum = jnp.zeros((1, w), F32)
        for j in range(s // ch - 1, -1, -1):
            rows = pl.ds(j * ch, ch)
            df = dfk_ref[0, rows, :] + dfq_ref[0, rows, :]
            dlf = jnp.dot(tri, df, precision=lax.Precision.HIGHEST, preferred_element_type=F32) + carry
            carry = dlf[0:1, :]
            dz = dlf * _sigmoid(-(z_ref[0, rows, :] + b_ref[...]))
            dz_ref[0, rows, :] = dz.astype(BF16)
            dbsum = dbsum + jnp.sum(dz, axis=0, keepdims=True)

        @pl.when(b == 0)
        def _():
            db_ref[...] = jnp.zeros_like(db_ref)

        db_ref[...] += dbsum

    return pl.pallas_call(
        kern, name="fgate_bwd", grid=(bl,),
        in_specs=[pl.BlockSpec((1, s, w), lambda b: (b, 0, 0))] * 3 + [pl.BlockSpec((1, w), lambda b: (0, 0))],
        out_specs=[pl.BlockSpec((1, s, w), lambda b: (b, 0, 0)), pl.BlockSpec((1, w), lambda b: (0, 0))],
        out_shape=[jax.ShapeDtypeStruct((bl, s, w), BF16), jax.ShapeDtypeStruct((1, w), F32)],
        compiler_params=_cp(1),
    )(dfk3, dfq3, zf3, bf)


def _lanes(col, width):
    return col if width == LANES else jnp.concatenate([col] * (width // LANES), axis=1)


def _causal(sc, row0, col0, transposed):
    r = lax.broadcasted_iota(jnp.int32, sc.shape, 0) + row0
    c = lax.broadcasted_iota(jnp.int32, sc.shape, 1) + col0
    return jnp.where((c >= r) if transposed else (r >= c), sc, NEG_BIG)


def _attn_fwd(q3, kv3, fcol, frow, nh):
    bl, s, da = q3.shape
    dh = da // nh
    tq = _tile(s, 512, LANES)
    nq = s // tq

    def kern(iq_tab, ik_tab, q_ref, k_ref, v_ref, fk_ref, fq_ref, o_ref, lse_ref, m_s, l_s, acc):
        iq, ik = iq_tab[pl.program_id(2)], ik_tab[pl.program_id(2)]

        @pl.when(ik == 0)
        def _():
            m_s[...] = jnp.full_like(m_s, NEG_BIG)
            l_s[...] = jnp.zeros_like(l_s)
            acc[...] = jnp.zeros_like(acc)

        def block(masked):
            st = _dot(k_ref[0], q_ref[0], _NT) - _lanes(fk_ref[0], tq) + fq_ref[0]
            if masked:
                st = _causal(st, ik * tq, iq * tq, True)
            m_old = m_s[...]
            m_new = jnp.maximum(m_old, jnp.max(st, axis=0, keepdims=True))
            alpha = jnp.exp(m_old - m_new)
            pt = jnp.exp(st - m_new)
            l_s[...] = alpha * l_s[...] + jnp.sum(pt, axis=0, keepdims=True)
            acc[...] = alpha * acc[...] + _dot(v_ref[0], pt, _TN)
            m_s[...] = m_new

        pl.when(ik < iq)(functools.partial(block, False))

        @pl.when(ik == iq)
        def _():
            block(True)
            l = l_s[...]
            o_ref[0] = (acc[...] / l).T
            lse_ref[0] = m_s[...] + jnp.log(l)

    pairs = [(i, j) for i in range(nq) for j in range(i + 1)]
    iq_tab, ik_tab = (jnp.asarray(col, jnp.int32) for col in zip(*pairs))
    qmap = lambda b, h, p, iqt, ikt: (b, iqt[p], h)
    kmap = lambda off: (lambda b, h, p, iqt, ikt: (b, ikt[p], off + h))
    return pl.pallas_call(
        kern, name="attn_fwd",
        grid_spec=pltpu.PrefetchScalarGridSpec(
            num_scalar_prefetch=2, grid=(bl, nh, len(pairs)),
            in_specs=[pl.BlockSpec((1, tq, dh), qmap), pl.BlockSpec((1, tq, dh), kmap(0)), pl.BlockSpec((1, tq, dh), kmap(nh)),
                      pl.BlockSpec((1, tq, LANES), lambda b, h, p, iqt, ikt: (b * nh + h, ikt[p], 0)),
                      pl.BlockSpec((1, 1, tq), lambda b, h, p, iqt, ikt: (b * nh + h, 0, iqt[p]))],
            out_specs=[pl.BlockSpec((1, tq, dh), qmap),
                       pl.BlockSpec((1, 1, tq), lambda b, h, p, iqt, ikt: (b * nh + h, 0, iqt[p]))],
            scratch_shapes=[pltpu.VMEM((1, tq), F32), pltpu.VMEM((1, tq), F32), pltpu.VMEM((dh, tq), F32)]),
        out_shape=[jax.ShapeDtypeStruct((bl, s, da), F32), jax.ShapeDtypeStruct((bl * nh, 1, s), F32)],
        compiler_params=_cp(3),
    )(iq_tab, ik_tab, q3, kv3, kv3, fcol, frow)


def _attn_bwd(q3, kv3, do3, o3, lse_row, fcol, frow, nh, scale):
    bl, s, da = q3.shape
    dh = da // nh
    tk = _tile(s, 512, LANES)
    nk = s // tk

    pairs = [(j, i) for j in range(nk) for i in range(j, nk)]

    def kern(ik_tab, iq_tab, q_ref, k_ref, v_ref, do_ref, o_ref, lse_ref, fk_ref, fq_ref, dq_ref, dk_ref, dv_ref, dfk_ref,
             dfq_ref, dq_acc, dk_acc, dv_acc, dfq_acc, delta_s):
        step = pl.program_id(2)
        ik, iq = ik_tab[step], iq_tab[step]
        qrow = pl.ds(iq, 1)

        @pl.when(step == 0)
        def _():
            dq_acc[...] = jnp.zeros_like(dq_acc)
            dfq_acc[...] = jnp.zeros_like(dfq_acc)

        @pl.when(ik == 0)
        def _():
            prod = do_ref[0] * o_ref[0]
            rows = lax.dot_general(jnp.ones((SUBLANES, dh), F32), prod, (_NT, ((), ())),
                                   precision=lax.Precision.HIGHEST, preferred_element_type=F32)
            delta_s[qrow, :] = rows[0:1, :]

        @pl.when(iq == ik)
        def _():
            dk_acc[...] = jnp.zeros_like(dk_acc)
            dv_acc[...] = jnp.zeros_like(dv_acc)

        def block(masked):
            q = q_ref[0]
            st = _dot(k_ref[0], q, _NT) - _lanes(fk_ref[0], tk) + fq_ref[0]
            if masked:
                st = _causal(st, ik * tk, iq * tk, True)
            pt = jnp.exp(st - lse_ref[0])
            dv_acc[...] += _dot(pt, do_ref[0], _NN)
            dpt = _dot(v_ref[0], do_ref[0], _NT)
            dst = (pt * (dpt - delta_s[qrow, :])).astype(BF16)
            q_ones = jnp.concatenate([q, jnp.ones_like(q)], axis=1)
            dk_acc[...] += _dot(dst, q_ones, _NN)
            qrows = pl.ds(pl.multiple_of(iq * tk, tk), tk)
            dq_acc[qrows, :] += _dot(dst, k_ref[0], _TN)
            dfq_acc[qrow, :] += jnp.sum(dst.astype(F32), axis=0, keepdims=True)

        pl.when(iq > ik)(functools.partial(block, False))
        pl.when(iq == ik)(functools.partial(block, True))

        @pl.when(iq == nk - 1)
        def _():
            ext = dk_acc[...]
            dk_ref[0] = ext[:, :dh].astype(BF16)
            dfk_ref[0] = -ext[:, dh:]
            dv_ref[0] = dv_acc[...].astype(BF16)

        @pl.when(step == len(pairs) - 1)
        def _():
            dq_ref[0] = (dq_acc[...] * scale).astype(BF16)
            dfq_ref[0] = dfq_acc[...]

    ik_tab, iq_tab = (jnp.asarray(col, jnp.int32) for col in zip(*pairs))
    qmap = lambda b, h, p, ikt, iqt: (b, iqt[p], h)
    omap = lambda b, h, p, ikt, iqt: (b, jnp.where(ikt[p] == 0, iqt[p], 0), h)
    rmap = lambda b, h, p, ikt, iqt: (b * nh + h, 0, iqt[p])
    kmap = lambda off: (lambda b, h, p, ikt, iqt: (b, ikt[p], off + h))
    bmap = lambda b, h, p, ikt, iqt: (b * nh + h, ikt[p], 0)
    return pl.pallas_call(
        kern, name="attn_bwd",
        grid_spec=pltpu.PrefetchScalarGridSpec(
            num_scalar_prefetch=2, grid=(bl, nh, len(pairs)),
            in_specs=[pl.BlockSpec((1, tk, dh), qmap), pl.BlockSpec((1, tk, dh), kmap(0)), pl.BlockSpec((1, tk, dh), kmap(nh)),
                      pl.BlockSpec((1, tk, dh), qmap), pl.BlockSpec((1, tk, dh), omap), pl.BlockSpec((1, 1, tk), rmap),
                      pl.BlockSpec((1, tk, LANES), bmap), pl.BlockSpec((1, 1, tk), rmap)],
            out_specs=[pl.BlockSpec((1, s, dh), lambda b, h, p, ikt, iqt: (b, 0, h)),
                       pl.BlockSpec((1, tk, dh), kmap(0)), pl.BlockSpec((1, tk, dh), kmap(0)),
                       pl.BlockSpec((1, tk, LANES), bmap),
                       pl.BlockSpec((1, nk, tk), lambda b, h, p, ikt, iqt: (b * nh + h, 0, 0))],
            scratch_shapes=[pltpu.VMEM((s, dh), F32), pltpu.VMEM((tk, 2 * dh), F32), pltpu.VMEM((tk, dh), F32),
                            pltpu.VMEM((nk, tk), F32), pltpu.VMEM((nk, tk), F32)]),
        out_shape=[jax.ShapeDtypeStruct((bl, s, da), BF16), jax.ShapeDtypeStruct((bl, s, da), BF16),
                   jax.ShapeDtypeStruct((bl, s, da), BF16), jax.ShapeDtypeStruct((bl * nh, s, LANES), F32),
                   jax.ShapeDtypeStruct((bl * nh, nk, tk), F32)],
        compiler_params=_cp(3),
    )(ik_tab, iq_tab, q3, kv3, kv3, do3, o3, lse_row, fcol, frow)


def _merge_fwd(mg3, pr3, pa3):
    bl, s, d = pr3.shape
    ts = _tile(s, 256, SUBLANES)
    half = lambda j: pl.BlockSpec((1, ts, d), lambda b, t, j=j: (b, t, j))

    def kern(mr_ref, ma_ref, pr_ref, pa_ref, o_ref):
        o_ref[0] = (_sigmoid(mr_ref[0]) * pr_ref[0] + _sigmoid(ma_ref[0]) * pa_ref[0]).astype(BF16)

    return pl.pallas_call(
        kern, name="merge_fwd", grid=(bl, s // ts),
        in_specs=[half(0), half(1)] + _act_specs(ts, d, 2), out_specs=_act_specs(ts, d, 1)[0],
        out_shape=jax.ShapeDtypeStruct((bl, s, d), BF16), compiler_params=_cp(2),
    )(mg3, mg3, pr3, pa3)


def _merge_bwd(dm3, mg3, pr3, pa3):
    bl, s, d = pr3.shape
    ts = _tile(s, 256, SUBLANES)
    half = lambda j: pl.BlockSpec((1, ts, d), lambda b, t, j=j: (b, t, j))

    def kern(dm_ref, mr_ref, ma_ref, pr_ref, pa_ref, dpr_ref, dpa_ref, dmr_ref, dma_ref):
        dm = dm_ref[0]
        gr, ga = _sigmoid(mr_ref[0]), _sigmoid(ma_ref[0])
        dpr_ref[0] = (gr * dm).astype(BF16)
        dpa_ref[0] = (ga * dm).astype(BF16)
        dmr_ref[0] = (dm * pr_ref[0] * gr * (1.0 - gr)).astype(BF16)
        dma_ref[0] = (dm * pa_ref[0] * ga * (1.0 - ga)).astype(BF16)

    return pl.pallas_call(
        kern, name="merge_bwd", grid=(bl, s // ts),
        in_specs=_act_specs(ts, d, 1) + [half(0), half(1)] + _act_specs(ts, d, 2), out_specs=_act_specs(ts, d, 4),
        out_shape=[jax.ShapeDtypeStruct((bl, s, d), BF16)] * 4, compiler_params=_cp(2),
    )(dm3, mg3, mg3, pr3, pa3)


def _ffn_conv(gf, cw, cb):
    kw = cw.shape[0]
    y = cb
    for k in range(kw):
        y = y + _shift_down(gf, kw - 1 - k) * cw[k:k + 1, :]
    return y


def _ffn_act_fwd(up3, cw, cb):
    bl, s, two = up3.shape
    dff = two // 2
    kw = cw.shape[0]
    tc = _tile(dff, 256, LANES)
    nc = dff // tc

    def kern(gf_ref, uf_ref, cw_ref, cb_ref, o_ref):
        o_ref[0] = (_gelu(_ffn_conv(gf_ref[0], cw_ref[...], cb_ref[...])) * uf_ref[0]).astype(BF16)

    act = lambda off: pl.BlockSpec((1, s, tc), lambda b, j, off=off: (b, 0, off + j))
    return pl.pallas_call(
        kern, name="ffn_act_fwd", grid=(bl, nc),
        in_specs=[act(0), act(nc), pl.BlockSpec((kw, tc), lambda b, j: (0, j)), pl.BlockSpec((1, tc), lambda b, j: (0, j))],
        out_specs=act(0), out_shape=jax.ShapeDtypeStruct((bl, s, dff), BF16), compiler_params=_cp(2),
    )(up3, up3, cw, cb)


def _ffn_act_bwd(up3, dact3, cw, cb):
    bl, s, two = up3.shape
    dff = two // 2
    kw = cw.shape[0]
    tc = _tile(dff, 256, LANES)
    nc = dff // tc

    def kern(gf_ref, uf_ref, da_ref, cw_ref, cb_ref, dgf_ref, duf_ref, dcw_ref, dcb_ref):
        b = pl.program_id(1)
        gf, cwv, da = gf_ref[0], cw_ref[...], da_ref[0]
        ge, dge = _gelu_and_grad(_ffn_conv(gf, cwv, cb_ref[...]))
        duf_ref[0] = (da * ge).astype(BF16)
        dgc = da * uf_ref[0] * dge
        dgf = jnp.zeros_like(dgc)
        rows = []
        for k in range(kw):
            dgf = dgf + _shift_up(dgc, kw - 1 - k) * cwv[k:k + 1, :]
            rows.append(jnp.sum(dgc * _shift_down(gf, kw - 1 - k), axis=0, keepdims=True))
        dgf_ref[0] = dgf.astype(BF16)

        @pl.when(b == 0)
        def _():
            dcw_ref[...] = jnp.zeros_like(dcw_ref)
            dcb_ref[...] = jnp.zeros_like(dcb_ref)

        for k in range(kw):
            dcw_ref[k:k + 1, :] += rows[k]
        dcb_ref[...] += jnp.sum(dgc, axis=0, keepdims=True)

    act = lambda off: pl.BlockSpec((1, s, tc), lambda j, b, off=off: (b, 0, off + j))
    cws = pl.BlockSpec((kw, tc), lambda j, b: (0, j))
    cbs = pl.BlockSpec((1, tc), lambda j, b: (0, j))
    return pl.pallas_call(
        kern, name="ffn_act_bwd", grid=(nc, bl),
        in_specs=[act(0), act(nc), act(0), cws, cbs], out_specs=[act(0), act(0), cws, cbs],
        out_shape=[jax.ShapeDtypeStruct((bl, s, dff), BF16), jax.ShapeDtypeStruct((bl, s, dff), BF16),
                   jax.ShapeDtypeStruct((kw, dff), F32), jax.ShapeDtypeStruct((1, dff), F32)],
        compiler_params=_cp(2),
    )(up3, up3, dact3, cw, cb)


_HBM = pl.BlockSpec(memory_space=pltpu.HBM)


def _place():
    x, y, c = lax.axis_index("x"), lax.axis_index("y"), lax.axis_index("c")
    chips = dict(me=2 * x + y, nx=2 * (1 - x) + y, ny=2 * x + (1 - y), diag=2 * (1 - x) + (1 - y))
    peers = dict(nx=(1 - x, y, c), ny=(x, 1 - y, c), sib=(x, y, 1 - c))
    return c, chips, peers


def _remote(src, dst, sems, k, to):
    return pltpu.make_async_remote_copy(src_ref=src, dst_ref=dst, send_sem=sems[0].at[k], recv_sem=sems[1].at[k],
                                        device_id=to, device_id_type=MESH)


RS_STEPS = 2


def _piece(q, idx, n=1):
    start = idx * q
    if not isinstance(start, int):
        start = pl.multiple_of(start, SUBLANES)
    return pl.ds(start, n * q)


def _all_gather_chips(xs, name):
    nt = len(xs)
    per = 9

    def body(*refs):
        x_refs, o_refs = refs[:nt], refs[nt:2 * nt]
        send_sems, recv_sems = refs[2 * nt:]
        c, chip, peer = _place()
        sems = (send_sems, recv_sems)
        me, nx, ny, dg = chip["me"], chip["nx"], chip["ny"], chip["diag"]
        sends = []

        def arrive(k, dst):
            _remote(dst, dst, sems, k, peer["sib"]).wait_recv()

        def pass_on(k, blk, to):
            cp = _remote(blk, blk, sems, k, peer[to])
            cp.start()
            sends.append(cp)

        for t in range(nt):
            q = xs[t].shape[0] // 4
            half = _piece(q, 2 * c, 2)
            for k, to in ((0, "nx"), (1, "ny")):
                cp = _remote(x_refs[t].at[half], o_refs[t].at[me, half], sems, per * t + k, peer[to])
                cp.start()
                sends.append(cp)
            cp = _remote(x_refs[t], o_refs[t].at[me], sems, per * t + 8, peer["sib"])
            cp.start()
            sends.append(cp)
        for t in range(nt):
            q, o, k0 = xs[t].shape[0] // 4, o_refs[t], per * t
            half, sub0, sub1 = _piece(q, 2 * c, 2), _piece(q, 2 * c), _piece(q, 2 * c + 1)
            arrive(k0 + 0, o.at[nx, half])
            pass_on(k0 + 2, o.at[nx, sub0], "ny")
            pass_on(k0 + 4, o.at[nx, half], "sib")
            arrive(k0 + 1, o.at[ny, half])
            pass_on(k0 + 3, o.at[ny, sub1], "nx")
            pass_on(k0 + 5, o.at[ny, half], "sib")
            arrive(k0 + 2, o.at[dg, sub0])
            pass_on(k0 + 6, o.at[dg, sub0], "sib")
            arrive(k0 + 3, o.at[dg, sub1])
            pass_on(k0 + 7, o.at[dg, sub1], "sib")
        for t in range(nt):
            q, o, k0 = xs[t].shape[0] // 4, o_refs[t], per * t
            arrive(k0 + 4, o.at[nx, _piece(q, 2 * (1 - c), 2)])
            arrive(k0 + 5, o.at[ny, _piece(q, 2 * (1 - c), 2)])
            arrive(k0 + 6, o.at[dg, _piece(q, 2 * (1 - c))])
            arrive(k0 + 7, o.at[dg, _piece(q, 2 * (1 - c) + 1)])
            arrive(k0 + 8, o.at[me])
        for cp in sends:
            cp.wait_send()

    return pl.pallas_call(
        body, name=name, in_specs=[_HBM] * nt, out_specs=[_HBM] * nt,
        out_shape=[jax.ShapeDtypeStruct((N_CHIPS,) + x.shape, x.dtype) for x in xs],
        scratch_shapes=[pltpu.SemaphoreType.DMA((per * nt,)), pltpu.SemaphoreType.DMA((per * nt,))],
    )(*xs)


def _all_gather_devices(xs, name):
    nt = len(xs)
    per = 7

    def body(*refs):
        x_refs, o_refs = refs[:nt], refs[nt:2 * nt]
        send_sems, recv_sems, local_sems = refs[2 * nt:]
        x, y, c = lax.axis_index("x"), lax.axis_index("y"), lax.axis_index("c")
        sems = (send_sems, recv_sems)
        sib = (x, y, 1 - c)
        chips = [(1 - x, y), (x, 1 - y), (1 - x, 1 - y)]
        slot = lambda px, py, pc: 4 * px + 2 * py + pc
        me = slot(x, y, c)
        sends, copies = [], []

        def arrive(k, dst):
            _remote(dst, dst, sems, k, sib).wait_recv()

        for t in range(nt):
            cp = pltpu.make_async_copy(x_refs[t], o_refs[t].at[me], local_sems.at[t])
            cp.start()
            copies.append(cp)
            for k, to in enumerate([sib] + [(*chip, c) for chip in chips]):
                cp = _remote(x_refs[t], o_refs[t].at[me], sems, per * t + k, to)
                cp.start()
                sends.append(cp)
        for t in range(nt):
            for j, chip in enumerate(chips):
                blk = o_refs[t].at[slot(*chip, c)]
                arrive(per * t + 1 + j, blk)
                cp = _remote(blk, blk, sems, per * t + 4 + j, sib)
                cp.start()
                sends.append(cp)
        for t in range(nt):
            arrive(per * t, o_refs[t].at[slot(x, y, 1 - c)])
            for j, chip in enumerate(chips):
                arrive(per * t + 4 + j, o_refs[t].at[slot(*chip, 1 - c)])
        for cp in sends:
            cp.wait_send()
        for cp in copies:
            cp.wait()

    return pl.pallas_call(
        body, name=name, in_specs=[_HBM] * nt, out_specs=[_HBM] * nt,
        out_shape=[jax.ShapeDtypeStruct((2 * N_CHIPS,) + a.shape, a.dtype) for a in xs],
        scratch_shapes=[pltpu.SemaphoreType.DMA((per * nt,)), pltpu.SemaphoreType.DMA((per * nt,)),
                        pltpu.SemaphoreType.DMA((nt,))],
    )(*xs)


def _exchange(name, xs, out_shapes, plan):
    nt = len(xs)

    def body(*refs):
        x_refs, o_refs = refs[:nt], refs[nt:2 * nt]
        send_sems, recv_sems = refs[2 * nt:]
        c, chip, peer = _place()
        cps = []
        for t in range(nt):
            for src, dst, to in plan(c, chip, x_refs[t], o_refs[t], xs[t].shape):
                cps.append(_remote(src, dst, (send_sems, recv_sems), len(cps), peer[to]))
        for cp in cps:
            cp.start()
        for cp in cps:
            cp.wait()

    n_copies = nt * len(plan(0, dict(me=0, nx=2, ny=1, diag=3), None, None, xs[0].shape, count_only=True))
    return pl.pallas_call(
        body, name=name, in_specs=[_HBM] * nt, out_specs=[_HBM] * nt,
        out_shape=[jax.ShapeDtypeStruct(s, x.dtype) for s, x in zip(out_shapes, xs)],
        scratch_shapes=[pltpu.SemaphoreType.DMA((n_copies,)), pltpu.SemaphoreType.DMA((n_copies,))],
    )(*xs)


def _plan_sibling(c, chip, g, out, shape, count_only=False):
    if count_only:
        return [None] * N_CHIPS
    q = shape[1] // 4
    return [(g.at[j, _piece(q, 2 * (1 - c), 2)], out.at[j], "sib") for j in range(N_CHIPS)]


def _plan_first(c, chip, p, out, shape, count_only=False):
    if count_only:
        return [None] * 4
    q = shape[1] // 2
    return [(p.at[chip["nx"], _piece(q, 0)], out.at[0], "nx"), (p.at[chip["diag"], _piece(q, 0)], out.at[1], "nx"),
            (p.at[chip["ny"], _piece(q, 1)], out.at[2], "ny"), (p.at[chip["diag"], _piece(q, 1)], out.at[3], "ny")]


def _plan_second(c, chip, p, out, shape, count_only=False):
    if count_only:
        return [None] * 2
    return [(p.at[1], out.at[0], "ny"), (p.at[3], out.at[1], "nx")]


def _rs_last(ps):
    nt = len(ps)

    def body(*refs):
        p_refs, o_refs = refs[:nt], refs[nt:2 * nt]
        send_sems, recv_sems = refs[2 * nt:]
        c, _, peer = _place()
        sems = (send_sems, recv_sems)
        cps = []
        for t in range(nt):
            q = ps[t].shape[0] // 4
            mine = _piece(q, 2 * c, 2)
            cps.append(_remote(p_refs[t].at[mine], o_refs[t].at[mine], sems, t, peer["sib"]))
            cps[-1].start()
        for t in range(nt):
            q = ps[t].shape[0] // 4
            theirs = _piece(q, 2 * (1 - c), 2)
            cps[t].wait_send()
            _remote(p_refs[t].at[theirs], o_refs[t].at[theirs], sems, t, peer["sib"]).wait_recv()

    return pl.pallas_call(
        body, name="rs_last", in_specs=[_HBM] * nt, out_specs=[_HBM] * nt,
        out_shape=[jax.ShapeDtypeStruct(p.shape, F32) for p in ps],
        input_output_aliases={t: t for t in range(nt)},
        scratch_shapes=[pltpu.SemaphoreType.DMA((nt,)), pltpu.SemaphoreType.DMA((nt,))],
    )(*ps)


def _add_stage(name, grid, a_list, b_list, a_map, b_map, tbs, out_shapes, out_map, out_dtype, prefetch=None):
    nt = len(a_list)
    lead = lambda shape: (None,) * (len(shape) - 2)

    def kern(*refs):
        refs = refs[(1 if prefetch is not None else 0):]
        for t in range(nt):
            refs[2 * nt + t][...] = (refs[t][...].astype(F32) + refs[nt + t][...].astype(F32)).astype(out_dtype)

    in_specs = [pl.BlockSpec(lead(a.shape) + (tb, a.shape[-1]), a_map) for a, tb in zip(a_list, tbs)]
    in_specs += [pl.BlockSpec(lead(b.shape) + (tb, b.shape[-1]), b_map) for b, tb in zip(b_list, tbs)]
    out_specs = [pl.BlockSpec(lead(s) + (tb, s[-1]), out_map) for s, tb in zip(out_shapes, tbs)]
    out_shape = [jax.ShapeDtypeStruct(s, out_dtype) for s in out_shapes]
    if prefetch is None:
        return pl.pallas_call(kern, name=name, grid=grid, in_specs=in_specs, out_specs=out_specs, out_shape=out_shape,
                              compiler_params=_cp(len(grid)))(*a_list, *b_list)
    return pl.pallas_call(
        kern, name=name,
        grid_spec=pltpu.PrefetchScalarGridSpec(num_scalar_prefetch=1, grid=grid, in_specs=in_specs, out_specs=out_specs),
        out_shape=out_shape, compiler_params=_cp(len(grid)))(prefetch, *a_list, *b_list)


def _reduce_scatter_chips(gs):
    x, y, c = lax.axis_index("x"), lax.axis_index("y"), lax.axis_index("c")
    me, nx, ny = 2 * x + y, 2 * (1 - x) + y, 2 * x + (1 - y)
    st = RS_STEPS
    unit = 4 * st * 2 * SUBLANES
    rows = [g.shape[1] for g in gs]
    gs = [jnp.pad(g, ((0, 0), (0, -g.shape[1] % unit), (0, 0))) for g in gs]
    qs = [g.shape[1] // 4 for g in gs]
    tbs = [q // st for q in qs]
    cols = [g.shape[2] for g in gs]
    core = jnp.reshape(c, (1,)).astype(jnp.int32)

    got = _exchange("rs_sibling", gs, [(N_CHIPS, 2 * q, cc) for q, cc in zip(qs, cols)], _plan_sibling)
    p0 = _add_stage("rs_add_sibling", (N_CHIPS, 2, st), gs, got,
                    lambda j, h, s, c_ref: (j, (2 * c_ref[0] + h) * st + s, 0), lambda j, h, s, c_ref: (j, h * st + s, 0),
                    tbs, [(N_CHIPS, 2 * q, cc) for q, cc in zip(qs, cols)], lambda j, h, s, c_ref: (j, h * st + s, 0),
                    BF16, prefetch=core)
    got = _exchange("rs_first", p0, [(4, q, cc) for q, cc in zip(qs, cols)], _plan_first)
    p1 = _add_stage("rs_add_first", (4, st), p0, got,
                    lambda k, s, i_ref: (i_ref[k], (k // 2) * st + s, 0), lambda k, s, i_ref: (k, s, 0),
                    tbs, [(4, q, cc) for q, cc in zip(qs, cols)], lambda k, s, i_ref: (k, s, 0),
                    BF16, prefetch=jnp.stack([me, ny, me, nx]).astype(jnp.int32))
    got = _exchange("rs_second", p1, [(2, q, cc) for q, cc in zip(qs, cols)], _plan_second)
    p2 = _add_stage("rs_add_second", (2, st), p1, got, lambda h, s, c_ref: (2 * h, s, 0), lambda h, s, c_ref: (h, s, 0),
                    tbs, [(4 * q, cc) for q, cc in zip(qs, cols)], lambda h, s, c_ref: ((2 * c_ref[0] + h) * st + s, 0),
                    F32, prefetch=core)
    return [out[:r] for out, r in zip(_rs_last(p2), rows)]


def _adamw(g, w, m, v, name):
    rows, cc = g.shape
    tr = _tile(rows, max(SUBLANES, (1 << 18) // cc), SUBLANES)
    k1 = 1.0 - ADAM_B1 ** ADAM_STEP
    k2 = 1.0 - ADAM_B2 ** ADAM_STEP

    def kern(g_ref, w_ref, m_ref, v_ref, d_ref, nm_ref, nv_ref):
        gv = g_ref[...]
        nm = ADAM_B1 * m_ref[...] + (1.0 - ADAM_B1) * gv
        nv = ADAM_B2 * v_ref[...] + (1.0 - ADAM_B2) * (gv * gv)
        nm_ref[...] = nm
        nv_ref[...] = nv
        d_ref[...] = -ADAM_LR * ((nm / k1) / (jnp.sqrt(nv / k2) + ADAM_EPS) + ADAM_WD * w_ref[...])

    spec = pl.BlockSpec((tr, cc), lambda t: (t, 0))
    return pl.pallas_call(
        kern, name=name, grid=(rows // tr,), in_specs=[spec] * 4, out_specs=[spec] * 3,
        out_shape=[jax.ShapeDtypeStruct((rows, cc), F32)] * 3, compiler_params=_cp(1),
    )(g, w, m, v)


def _flat_pad(parts, total):
    flat = jnp.concatenate([p.reshape(-1) for p in parts])
    return jnp.pad(flat, (0, total - flat.shape[0]))


def _split_flat(flat, shapes):
    out, pos = [], 0
    for shp in shapes:
        size = math.prod(shp)
        out.append(flat[pos:pos + size].reshape(shp))
        pos += size
    return out


def _cols_of_chunks(chunks, lo, hi):
    width = chunks.shape[2]
    parts = []
    for j in range(chunks.shape[0]):
        a, b = max(lo, j * width), min(hi, (j + 1) * width)
        if a < b:
            parts.append(chunks[j, :, a - j * width:b - j * width])
    return parts[0] if len(parts) == 1 else jnp.concatenate(parts, axis=1)


def _chunks_of_cols(segments, n_chunks):
    total = sum(s.shape[1] for s in segments)
    width = total // n_chunks
    chunks = []
    for j in range(n_chunks):
        lo, hi, pos, parts = j * width, (j + 1) * width, 0, []
        for s in segments:
            a, b = max(lo, pos), min(hi, pos + s.shape[1])
            if a < b:
                parts.append(s[:, a - pos:b - pos])
            pos += s.shape[1]
        chunks.append(parts[0] if len(parts) == 1 else jnp.concatenate(parts, axis=1))
    return jnp.stack(chunks)


_WEIGHTS = ['w_ada', 'b_ada', 'g_norm1', 'w_in', 'w_rnn_conv', 'b_rnn_conv', 'w_lru_a', 'b_lru_a', 'w_lru_i', 'b_lru_i',
            'lru_lambda', 'b_fgate', 'w_proj_rnn', 'w_proj_attn', 'w_out', 'g_norm2', 'w_ffn_up', 'w_ffn_conv',
            'b_ffn_conv', 'w_ffn_down', 'w_ada_final', 'b_ada_final', 'g_final']
_MATMUL = ['w_in', 'w_proj_rnn', 'w_proj_attn', 'w_out', 'w_ffn_up', 'w_ffn_down']
_ADA = ['w_ada', 'w_ada_final']
_ADA_BIAS = ['b_ada', 'b_ada_final']
_CONV = ['w_rnn_conv', 'w_ffn_conv']
_REPLICATED = [n for n in _WEIGHTS if n not in _MATMUL + _ADA + _ADA_BIAS + _CONV]


def kernel(x, c, w_ada, b_ada, g_norm1, w_in, w_rnn_conv, b_rnn_conv, w_lru_a, b_lru_a, w_lru_i, b_lru_i, lru_lambda, b_fgate, w_proj_rnn, w_proj_attn, w_out, g_norm2, w_ffn_up, w_ffn_conv, b_ffn_conv, w_ffn_down, w_ada_final, b_ada_final, g_final, loss_target, m_w_ada, m_b_ada, m_g_norm1, m_w_in, m_w_rnn_conv, m_b_rnn_conv, m_w_lru_a, m_b_lru_a, m_w_lru_i, m_b_lru_i, m_lru_lambda, m_b_fgate, m_w_proj_rnn, m_w_proj_attn, m_w_out, m_g_norm2, m_w_ffn_up, m_w_ffn_conv, m_b_ffn_conv, m_w_ffn_down, m_w_ada_final, m_b_ada_final, m_g_final, v_w_ada, v_b_ada, v_g_norm1, v_w_in, v_w_rnn_conv, v_b_rnn_conv, v_w_lru_a, v_b_lru_a, v_w_lru_i, v_b_lru_i, v_lru_lambda, v_b_fgate, v_w_proj_rnn, v_w_proj_attn, v_w_out, v_g_norm2, v_w_ffn_up, v_w_ffn_conv, v_b_ffn_conv, v_w_ffn_down, v_w_ada_final, v_b_ada_final, v_g_final):
    args = locals()
    shape_of = {n: args[n].shape for n in _WEIGHTS}

    def view(a):
        if a.ndim >= 3:
            return a[0]
        return a[None, :] if a.ndim == 1 else a

    w2 = {n: view(args[n]) for n in _WEIGHTS}
    m2 = {n: args['m_' + n].reshape(w2[n].shape) for n in _WEIGHTS}
    v2 = {n: args['v_' + n].reshape(w2[n].shape) for n in _WEIGHTS}

    bl, s, d = x.shape
    t = bl * s
    nh = b_fgate.shape[-1]
    nb, rb = w_lru_a.shape[1], w_lru_a.shape[2]
    dr = nb * rb
    da = w2['w_proj_attn'].shape[0] * N_CHIPS
    dh = da // nh
    dff = w2['w_ffn_conv'].shape[1] * N_CHIPS
    scale = dh ** -0.5
    chip = 2 * lax.axis_index("x") + lax.axis_index("y")
    dev = 2 * chip + lax.axis_index("c")

    names = list(_MATMUL)
    gathered = dict(zip(names, _all_gather_chips([w2[n].astype(BF16) for n in names], "ag_weights")))
    n_conv = sum(w2[n].size for n in _CONV)
    rows_conv = -(-n_conv // (FLAT_COLS * 32)) * 32
    conv_local = _flat_pad([w2[n] for n in _CONV], rows_conv * FLAT_COLS).reshape(rows_conv, FLAT_COLS)
    conv_all = _all_gather_chips([conv_local], "ag_conv")[0].reshape(N_CHIPS, -1)
    conv_full, pos = {}, 0
    for n in _CONV:
        r, n4 = w2[n].shape
        blocks = conv_all[:, pos:pos + r * n4].reshape(N_CHIPS, r, n4)
        conv_full[n] = jnp.concatenate([blocks[j] for j in range(N_CHIPS)], axis=1)
        pos += r * n4
    rowmajor = lambda n: gathered[n].reshape(-1, gathered[n].shape[2])
    w_proj_rnn_f, w_proj_attn_f, w_out_f, w_ffn_down_f = (rowmajor(n) for n in ('w_proj_rnn', 'w_proj_attn', 'w_out', 'w_ffn_down'))
    up_chunk = w2['w_ffn_up'].shape[1]

    o_q, o_k, o_fl = 2 * dr, 2 * dr + da, 2 * dr + 3 * da
    o_mg = o_fl + nh
    g_in_w = gathered['w_in']
    w_rnn, w_q = _cols_of_chunks(g_in_w, 0, o_q), _cols_of_chunks(g_in_w, o_q, o_k)
    w_kv, w_mg = _cols_of_chunks(g_in_w, o_k, o_fl), _cols_of_chunks(g_in_w, o_mg, o_mg + 2 * d)
    w_fl = jnp.pad(_cols_of_chunks(g_in_w, o_fl, o_mg), ((0, 0), (0, LANES - nh)))
    w_rest = jnp.concatenate([w_q, w_kv, w_mg], axis=1)
    bf_pad = jnp.pad(w2['b_fgate'], ((0, 0), (0, LANES - nh)))

    nd = 2 * N_CHIPS
    c_act = _silu_pad(_all_gather_devices([c], "ag_cond")[0].reshape(nd * bl, d), nd * bl)
    my_cols = lambda a, n: lax.dynamic_slice_in_dim(a, chip * w2[n].shape[1], w2[n].shape[1], axis=1)
    mod_cols = [_mm(c_act, w2[n].astype(BF16), "nn", name=n + "_fwd", bias=my_cols(w2[b], n)) for n, b in zip(_ADA, _ADA_BIAS)]
    my_rows = lambda g: lax.dynamic_slice_in_dim(g, dev * bl, bl, axis=1).transpose(1, 0, 2).reshape(bl, -1)
    mod, modf = (my_rows(g) for g in _all_gather_chips(mod_cols, "ag_mod"))
    sh1, sc1, gt1, sh2, sc2, gt2 = [mod[:, i * d:(i + 1) * d].reshape(bl, 1, d) for i in range(6)]
    shf, scf = modf[:, :d].reshape(bl, 1, d), modf[:, d:].reshape(bl, 1, d)

    h1 = _norm_mod_fwd(x, w2['g_norm1'], sh1, sc1)
    h1f = h1.reshape(t, d)
    zr = _mm(h1f, w_rnn, "nn", name="in_rnn").reshape(bl, s, 2 * dr)
    q3 = _mm(h1f, w_q, "nn", name="in_q", out_dtype=BF16, scale=scale).reshape(bl, s, da)
    kv3 = _mm(h1f, w_kv, "nn", name="in_kv", out_dtype=BF16).reshape(bl, s, 2 * da)
    mg3 = _mm(h1f, w_mg, "nn", name="in_mg").reshape(bl, s, 2 * d)
    zf3 = _mm(h1f, w_fl, "nn", name="in_fl").reshape(bl, s, LANES)

    lru = (conv_full['w_rnn_conv'], w2['b_rnn_conv'], w2['w_lru_a'], w2['b_lru_a'], w2['w_lru_i'], w2['b_lru_i'], w2['lru_lambda'])
    hseq, y_rnn = _rnn_fwd(zr, *lru)

    f3 = _fgate_fwd(zf3, bf_pad)
    f_heads = f3[:, :, :nh].transpose(0, 2, 1).reshape(bl * nh, s)
    fcol = jnp.broadcast_to(f_heads[:, :, None], (bl * nh, s, LANES))
    frow = f_heads.reshape(bl * nh, 1, s)
    o3, lse_row = _attn_fwd(q3, kv3, fcol, frow, nh)

    pr3 = _mm(y_rnn.reshape(t, dr), w_proj_rnn_f, "nn", name="proj_rnn").reshape(bl, s, d)
    pa3 = _mm(o3.reshape(t, da), w_proj_attn_f, "nn", name="proj_attn").reshape(bl, s, d)
    merged = _merge_fwd(mg3, pr3, pa3)
    mo3 = _mm(merged.reshape(t, d), w_out_f, "nn", name="mix_out").reshape(bl, s, d)
    x1, h2 = _resid_norm_fwd(x, mo3, gt1, w2['g_norm2'], sh2, sc2)
    h2f = h2.reshape(t, d)
    up3 = _mm(h2f, gathered['w_ffn_up'], "nn", name="ffn_up", b_chunk=up_chunk).reshape(bl, s, 2 * dff)
    act3 = _ffn_act_fwd(up3, conv_full['w_ffn_conv'], w2['b_ffn_conv'])
    yf3 = _mm(act3.reshape(t, dff), w_ffn_down_f, "nn", name="ffn_down").reshape(bl, s, d)

    dx2, dshf, dscf, dg_final, loss_part = _final_fwd_bwd(x1, yf3, gt2, w2['g_final'], shf, scf, loss_target)
    loss = lax.psum(loss_part[0, 0], ("x", "y", "c"))

    dyf, dgt2 = _gate_bwd(dx2, yf3, gt2, "ffn_gate_bwd")
    dyf_f = dyf.reshape(t, d)
    g_ffn_down = _mm(act3.reshape(t, dff), dyf_f, "tn", name="dw_ffn_down")
    dact3 = _mm(dyf_f, w_ffn_down_f, "nt", name="d_ffn_act").reshape(bl, s, dff)
    dgf, duf, g_ffn_conv, g_b_ffn_conv = _ffn_act_bwd(up3, dact3, conv_full['w_ffn_conv'], w2['b_ffn_conv'])
    dgf_f, duf_f = dgf.reshape(t, dff), duf.reshape(t, dff)
    g_ffn_up = jnp.concatenate([_mm(h2f, dgf_f, "tn", name="dw_ffn_up_gate", out_chunk=up_chunk),
                                _mm(h2f, duf_f, "tn", name="dw_ffn_up_value", out_chunk=up_chunk)], axis=0)
    dh2 = _mm([dgf_f, duf_f], gathered['w_ffn_up'], "nt", name="d_h2", b_chunk=up_chunk).reshape(bl, s, d)
    dx1, dsh2, dsc2, dg_norm2 = _norm_mod_bwd(dh2, x1, dx2, w2['g_norm2'], sc2, "norm2_bwd")

    dmo, dgt1 = _gate_bwd(dx1, mo3, gt1, "mix_gate_bwd")
    dmo_f = dmo.reshape(t, d)
    g_out = _mm(merged.reshape(t, d), dmo_f, "tn", name="dw_out")
    dm3 = _mm(dmo_f, w_out_f, "nt", name="d_merged").reshape(bl, s, d)
    dpr, dpa, dmr, dma = _merge_bwd(dm3, mg3, pr3, pa3)
    g_proj_rnn = _mm(y_rnn.reshape(t, dr), dpr.reshape(t, d), "tn", name="dw_proj_rnn")
    g_proj_attn = _mm(o3.reshape(t, da), dpa.reshape(t, d), "tn", name="dw_proj_attn")
    dyr3 = _mm(dpr.reshape(t, d), w_proj_rnn_f, "nt", name="d_y_rnn").reshape(bl, s, dr)
    do3 = _mm(dpa.reshape(t, d), w_proj_attn_f, "nt", name="d_y_attn").reshape(bl, s, da)

    dq3, dk3, dv3, dfk, dfq = _attn_bwd(q3, kv3, do3, o3, lse_row, fcol, frow, nh, scale)
    heads_last = lambda a: jnp.pad(a.reshape(bl, nh, s).transpose(0, 2, 1), ((0, 0), (0, 0), (0, LANES - nh)))
    dzf3, g_bf = _fgate_bwd(heads_last(dfk[:, :, 0]), heads_last(dfq), zf3, bf_pad)

    dxr, dgr, g_rnn_conv, g_b_rnn_conv, g_lru_a, g_b_lru_a, g_lru_i, g_b_lru_i, g_lam = _rnn_bwd(zr, hseq, dyr3, *lru)

    dz = [a.reshape(t, -1) for a in (dxr, dgr, dq3, dk3, dv3, dmr, dma)]
    dzf_f = dzf3.reshape(t, LANES)
    seg_names = ("xr", "gr", "q", "k", "v", "mr", "ma")
    g_seg = [_mm(h1f, a, "tn", name="dw_in_" + n) for n, a in zip(seg_names, dz)]
    g_in_fl = _mm(h1f, dzf_f, "tn", name="dw_in_fl")[:, :nh]
    g_in = _chunks_of_cols(g_seg[:5] + [g_in_fl] + g_seg[5:], N_CHIPS)
    dh1 = _mm(dzf_f, w_fl, "nt", name="d_h1_fl")
    dh1 = _mm(dz[:2], w_rnn, "nt", name="d_h1_rnn", add=dh1)
    dh1 = _mm(dz[2:], w_rest, "nt", name="d_h1", add=dh1).reshape(bl, s, d)
    grad_x, dsh1, dsc1, dg_norm1 = _norm_mod_bwd(dh1, x, dx1, w2['g_norm1'], sc1, "norm1_bwd")

    dmods = [jnp.concatenate([dsh1, dsc1, dgt1, dsh2, dsc2, dgt2], axis=-1).reshape(bl, -1),
             jnp.concatenate([dshf, dscf], axis=-1).reshape(bl, -1)]
    dmods = [g.reshape(nd * bl, -1) for g in _all_gather_devices(dmods, "ag_dmod")]
    grad = {n: _mm(c_act, my_cols(g, n), "tn", name="dw_" + n) for n, g in zip(_ADA, dmods)}
    grad.update({b: _rowsum(g, "d" + b) for b, g in zip(_ADA_BIAS, dmods)})

    rowchunks = lambda g: g.reshape(N_CHIPS, g.shape[0] // N_CHIPS, g.shape[1])
    full = dict(w_in=g_in, w_proj_rnn=rowchunks(g_proj_rnn), w_proj_attn=rowchunks(g_proj_attn),
                w_out=rowchunks(g_out), w_ffn_up=g_ffn_up, w_ffn_down=rowchunks(g_ffn_down))
    small = dict(g_norm1=dg_norm1, w_rnn_conv=g_rnn_conv, b_rnn_conv=g_b_rnn_conv, w_lru_a=g_lru_a,
                 b_lru_a=g_b_lru_a, w_lru_i=g_lru_i, b_lru_i=g_b_lru_i, lru_lambda=g_lam, b_fgate=g_bf[:, :nh],
                 g_norm2=dg_norm2, w_ffn_conv=g_ffn_conv, b_ffn_conv=g_b_ffn_conv, g_final=dg_final)

    small_names = _REPLICATED + _CONV
    n_small = sum(small[n].size for n in small_names)
    rows_q = -(-n_small // (N_CHIPS * FLAT_COLS * 32 * RS_STEPS)) * 32 * RS_STEPS
    small_flat = _flat_pad([small[n] for n in small_names], N_CHIPS * rows_q * FLAT_COLS).reshape(N_CHIPS, rows_q, FLAT_COLS)
    reduced = _reduce_scatter_chips([full[n] for n in names] + [small_flat])
    grad.update(zip(names, reduced[:-1]))
    small_all = _all_gather_chips([reduced[-1]], "ag_small_grads")[0].reshape(-1)
    grad.update(zip(small_names, _split_flat(small_all, [small[n].shape for n in small_names])))
    for n in _CONV:
        n4 = w2[n].shape[1]
        grad[n] = lax.dynamic_slice_in_dim(grad[n], chip * n4, n4, axis=1)

    delta_w, new_m, new_v = {}, {}, {}
    for n in names + _ADA:
        delta_w[n], new_m[n], new_v[n] = _adamw(grad[n], w2[n], m2[n], v2[n], "adamw_" + n)
    small_names = small_names + _ADA_BIAS
    rows_small = -(-sum(w2[n].size for n in small_names) // (FLAT_COLS * SUBLANES)) * SUBLANES
    flat_small = lambda src: _flat_pad([src[n] for n in small_names], rows_small * FLAT_COLS).reshape(rows_small, FLAT_COLS)
    small_out = _adamw(flat_small(grad), flat_small(w2), flat_small(m2), flat_small(v2), "adamw_small")
    for dst, flat in zip((delta_w, new_m, new_v), small_out):
        dst.update(zip(small_names, _split_flat(flat.reshape(-1), [w2[n].shape for n in small_names])))

    out = [loss, grad_x]
    for src in (grad, delta_w, new_m, new_v):
        out += [src[n].reshape(shape_of[n]) for n in _WEIGHTS]
    return tuple(out)
```

```python
import functools
import math

import jax
import jax.numpy as jnp
from jax import lax
from jax.experimental import pallas as pl
from jax.experimental.pallas import tpu as pltpu

F32 = jnp.float32
BF16 = jnp.bfloat16
MESH = pl.DeviceIdType.MESH

RMS_EPS = 1e-6
LRU_C = 8.0
ADAM_LR = 0.001
ADAM_B1 = 0.9
ADAM_B2 = 0.999
ADAM_EPS = 1e-08
ADAM_WD = 0.01
ADAM_STEP = 10

LANES = 128
SUBLANES = 8
N_CHIPS = 4
FLAT_COLS = 1024
SCAN_SEGMENTS = 2 * SUBLANES
VMEM_LIMIT = 48 * 1024 * 1024
NEG_BIG = -1e30


def _cp(n_axes):
    return pltpu.CompilerParams(dimension_semantics=("arbitrary",) * n_axes, vmem_limit_bytes=VMEM_LIMIT)


def _tile(n, target, align):
    if n <= target:
        return n
    t = (target // align) * align
    while t >= align:
        if n % t == 0:
            return t
        t -= align
    return n


def _nice_rows(n, align):
    r = -(-n // align) * align
    while True:
        if r <= 640:
            return r, r
        t = _tile(r, 640, align)
        if 128 <= t <= 640:
            return r, t
        r += align


def _sigmoid(x):
    return 0.5 * jnp.tanh(0.5 * x) + 0.5


def _softplus(x):
    return jnp.maximum(x, 0.0) + jnp.log1p(jnp.exp(-jnp.abs(x)))


def _expm1(x, exp_x):
    small = x * (1.0 + 0.5 * x * (1.0 + (1.0 / 3.0) * x * (1.0 + 0.25 * x)))
    return jnp.where(jnp.abs(x) < 0.05, small, exp_x - 1.0)


_GELU_K = math.sqrt(2.0 / math.pi)
_GELU_C = 0.044715


def _gelu(x):
    t = jnp.tanh(_GELU_K * (x + _GELU_C * x * x * x))
    return 0.5 * x * (1.0 + t)


def _gelu_and_grad(x):
    t = jnp.tanh(_GELU_K * (x + _GELU_C * x * x * x))
    g = 0.5 * x * (1.0 + t)
    dg = 0.5 * (1.0 + t) + 0.5 * x * (1.0 - t * t) * _GELU_K * (1.0 + 3.0 * _GELU_C * x * x)
    return g, dg


def _shift_down(x, k):
    if k == 0:
        return x
    y = pltpu.roll(x, k, 0)
    rows = lax.broadcasted_iota(jnp.int32, (SUBLANES, x.shape[1]), 0)
    return jnp.concatenate([jnp.where(rows >= k, y[:SUBLANES], 0.0), y[SUBLANES:]], axis=0)


def _shift_up(x, k):
    if k == 0:
        return x
    s = x.shape[0]
    y = pltpu.roll(x, s - k, 0)
    rows = lax.broadcasted_iota(jnp.int32, (SUBLANES, x.shape[1]), 0)
    return jnp.concatenate([y[:s - SUBLANES], jnp.where(rows < SUBLANES - k, y[s - SUBLANES:], 0.0)], axis=0)


def _dot(a, b, dims):
    return lax.dot_general(a.astype(BF16), b.astype(BF16), (dims, ((), ())), preferred_element_type=F32)


_NN = ((1,), (0,))
_NT = ((1,), (1,))
_TN = ((0,), (0,))


def _mm(a, b, mode, *, name, out_dtype=F32, scale=None, bias=None, add=None, tm=1024, tn=1024, tk=1024,
        b_chunk=None, out_chunk=None):
    pieces = list(a) if isinstance(a, (list, tuple)) else [a]
    ksize = lambda p: p.shape[0] if mode == "tn" else p.shape[1]
    if b_chunk is None:
        brows, bcols = b.shape
    else:
        brows, bcols = b.shape[1], b.shape[0] * b_chunk
    k = sum(ksize(p) for p in pieces)
    if mode == "nt":
        m, n = pieces[0].shape[0], brows
        assert bcols == k, (bcols, k)
    else:
        m, n = (pieces[0].shape[1] if mode == "tn" else pieces[0].shape[0]), bcols
        assert brows == k, (brows, k)
    tm = _tile(m, tm, LANES)
    ncut = n
    if b_chunk is not None and mode != "nt":
        ncut = b_chunk
    if out_chunk is not None:
        ncut = math.gcd(ncut, out_chunk)
    tn = _tile(ncut, tn, LANES)
    kcut = b_chunk if (b_chunk is not None and mode == "nt") else k
    for p in pieces:
        kcut = math.gcd(kcut, ksize(p))
    tk = _tile(kcut, tk, LANES)
    nk = k // tk
    dims = {"nn": _NN, "nt": _NT, "tn": _TN}[mode]
    counts = [ksize(p) // tk for p in pieces]
    starts = [sum(counts[:i]) for i in range(len(pieces))]
    n_pieces = len(pieces)

    def a_spec(s0, cnt):
        kmap = (lambda kk: kk) if n_pieces == 1 else (lambda kk: jnp.clip(kk - s0, 0, cnt - 1))
        if mode == "tn":
            return pl.BlockSpec((tk, tm), lambda i, j, kk: (kmap(kk), i))
        return pl.BlockSpec((tm, tk), lambda i, j, kk: (i, kmap(kk)))

    if b_chunk is None:
        if mode == "nt":
            b_spec = pl.BlockSpec((tn, tk), lambda i, j, kk: (j, kk))
        else:
            b_spec = pl.BlockSpec((tk, tn), lambda i, j, kk: (kk, j))
    elif mode == "nt":
        per_b = b_chunk // tk
        b_spec = pl.BlockSpec((None, tn, tk), lambda i, j, kk: (kk // per_b, j, kk % per_b))
    else:
        per_b = b_chunk // tn
        b_spec = pl.BlockSpec((None, tk, tn), lambda i, j, kk: (j // per_b, kk, j % per_b))
    if out_chunk is None:
        out_spec = pl.BlockSpec((tm, tn), lambda i, j, kk: (i, j))
        out_shape = jax.ShapeDtypeStruct((m, n), out_dtype)
    else:
        per_o = out_chunk // tn
        out_spec = pl.BlockSpec((None, tm, tn), lambda i, j, kk: (j // per_o, i, j % per_o))
        out_shape = jax.ShapeDtypeStruct((n // out_chunk, m, out_chunk), out_dtype)
    in_specs = [a_spec(s0, cnt) for s0, cnt in zip(starts, counts)] + [b_spec]
    args = pieces + [b]
    if bias is not None:
        in_specs.append(pl.BlockSpec((1, tn), lambda i, j, kk: (0, j)))
        args.append(bias)
    if add is not None:
        in_specs.append(pl.BlockSpec((tm, tn), lambda i, j, kk: (i, j)))
        args.append(add)

    def kern(*refs):
        b_ref = refs[n_pieces]
        o_ref = refs[n_pieces + 1 + (bias is not None) + (add is not None)]

        def finish(r):
            if scale is not None:
                r = r * scale
            pos = n_pieces + 1
            if bias is not None:
                r = r + refs[pos][...]
                pos += 1
            if add is not None:
                r = r + refs[pos][...]
            o_ref[...] = r.astype(out_dtype)

        if nk == 1:
            finish(_dot(refs[0][...], b_ref[...], dims))
            return
        acc = refs[-1]
        kk = pl.program_id(2)

        @pl.when(kk == 0)
        def _():
            acc[...] = jnp.zeros_like(acc)

        if n_pieces == 1:
            acc[...] += _dot(refs[0][...], b_ref[...], dims)
        else:
            for idx in range(n_pieces):
                @pl.when((kk >= starts[idx]) & (kk < starts[idx] + counts[idx]))
                def _(idx=idx):
                    acc[...] += _dot(refs[idx][...], b_ref[...], dims)

        @pl.when(kk == nk - 1)
        def _():
            finish(acc[...])

    return pl.pallas_call(
        kern, name=name,
        grid=(m // tm, n // tn, nk),
        in_specs=in_specs, out_specs=out_spec, out_shape=out_shape,
        scratch_shapes=[pltpu.VMEM((tm, tn), F32)] if nk > 1 else [],
        compiler_params=_cp(3),
    )(*args)


def _silu_pad(c, rows):
    bl, d = c.shape

    def kern(c_ref, o_ref):
        o_ref[...] = jnp.zeros_like(o_ref)
        v = c_ref[...]
        o_ref[0:bl, :] = v * _sigmoid(v)

    return pl.pallas_call(kern, name="silu_pad", out_shape=jax.ShapeDtypeStruct((rows, d), F32))(c)


def _rowsum(x, name):
    r, n = x.shape

    def kern(x_ref, o_ref):
        o_ref[...] = jnp.sum(x_ref[...], axis=0, keepdims=True)

    return pl.pallas_call(kern, name=name, out_shape=jax.ShapeDtypeStruct((1, n), F32))(x)


def _norm_parts(x, g):
    r = lax.rsqrt(jnp.mean(x * x, axis=-1, keepdims=True) + RMS_EPS)
    xh = x * r
    return r, xh, xh * g


def _norm_bwd_parts(dh, xh, r, g, sc):
    n = xh * g
    dn = dh * (1.0 + sc)
    dxh = dn * g
    dx = r * (dxh - xh * jnp.mean(dxh * xh, axis=-1, keepdims=True))
    return dx, dh, dh * n, dn * xh


def _act_specs(ts, d, n):
    return [pl.BlockSpec((1, ts, d), lambda b, t: (b, t, 0)) for _ in range(n)]


def _vec_spec(d):
    return pl.BlockSpec((1, 1, d), lambda b, t: (b, 0, 0))


def _par_spec(d):
    return pl.BlockSpec((1, d), lambda b, t: (0, 0))


def _norm_mod_fwd(x3, g, sh, sc):
    bl, s, d = x3.shape
    ts = _tile(s, 512, SUBLANES)

    def kern(x_ref, g_ref, sh_ref, sc_ref, h_ref):
        _, _, n = _norm_parts(x_ref[0], g_ref[...])
        h_ref[0] = (n * (1.0 + sc_ref[0]) + sh_ref[0]).astype(BF16)

    return pl.pallas_call(
        kern, name="norm_mod_fwd", grid=(bl, s // ts),
        in_specs=_act_specs(ts, d, 1) + [_par_spec(d), _vec_spec(d), _vec_spec(d)],
        out_specs=_act_specs(ts, d, 1)[0],
        out_shape=jax.ShapeDtypeStruct((bl, s, d), BF16),
        compiler_params=_cp(2),
    )(x3, g, sh, sc)


def _resid_norm_fwd(x3, y3, gate, g, sh, sc):
    bl, s, d = x3.shape
    ts = _tile(s, 512, SUBLANES)

    def kern(x_ref, y_ref, gate_ref, g_ref, sh_ref, sc_ref, x1_ref, h_ref):
        x1 = x_ref[0] + gate_ref[0] * y_ref[0]
        x1_ref[0] = x1
        _, _, n = _norm_parts(x1, g_ref[...])
        h_ref[0] = (n * (1.0 + sc_ref[0]) + sh_ref[0]).astype(BF16)

    return pl.pallas_call(
        kern, name="resid_norm_fwd", grid=(bl, s // ts),
        in_specs=_act_specs(ts, d, 2) + [_vec_spec(d), _par_spec(d), _vec_spec(d), _vec_spec(d)],
        out_specs=_act_specs(ts, d, 2),
        out_shape=[jax.ShapeDtypeStruct((bl, s, d), F32), jax.ShapeDtypeStruct((bl, s, d), BF16)],
        compiler_params=_cp(2),
    )(x3, y3, gate, g, sh, sc)


def _norm_mod_bwd(dh3, x3, dres3, g, sc, name):
    bl, s, d = x3.shape
    ts = _tile(s, 512, SUBLANES)

    def kern(dh_ref, x_ref, dres_ref, g_ref, sc_ref, dx_ref, dsh_ref, dsc_ref, dg_ref):
        b, t = pl.program_id(0), pl.program_id(1)
        gv = g_ref[...]
        r, xh, _ = _norm_parts(x_ref[0], gv)
        dx, a, bb, cc = _norm_bwd_parts(dh_ref[0], xh, r, gv, sc_ref[0])
        dx_ref[0] = dres_ref[0] + dx

        @pl.when(t == 0)
        def _():
            dsh_ref[...] = jnp.zeros_like(dsh_ref)
            dsc_ref[...] = jnp.zeros_like(dsc_ref)

        @pl.when((t == 0) & (b == 0))
        def _():
            dg_ref[...] = jnp.zeros_like(dg_ref)

        dsh_ref[0] += jnp.sum(a, axis=0, keepdims=True)
        dsc_ref[0] += jnp.sum(bb, axis=0, keepdims=True)
        dg_ref[...] += jnp.sum(cc, axis=0, keepdims=True)

    return pl.pallas_call(
        kern, name=name, grid=(bl, s // ts),
        in_specs=_act_specs(ts, d, 3) + [_par_spec(d), _vec_spec(d)],
        out_specs=[_act_specs(ts, d, 1)[0], _vec_spec(d), _vec_spec(d), _par_spec(d)],
        out_shape=[jax.ShapeDtypeStruct((bl, s, d), F32), jax.ShapeDtypeStruct((bl, 1, d), F32),
                   jax.ShapeDtypeStruct((bl, 1, d), F32), jax.ShapeDtypeStruct((1, d), F32)],
        compiler_params=_cp(2),
    )(dh3, x3, dres3, g, sc)


def _gate_bwd(dx3, y3, gate, name):
    bl, s, d = dx3.shape
    ts = _tile(s, 512, SUBLANES)

    def kern(dx_ref, y_ref, gate_ref, dy_ref, dgate_ref):
        t = pl.program_id(1)
        dx = dx_ref[0]
        dy_ref[0] = (gate_ref[0] * dx).astype(BF16)

        @pl.when(t == 0)
        def _():
            dgate_ref[...] = jnp.zeros_like(dgate_ref)

        dgate_ref[0] += jnp.sum(dx * y_ref[0], axis=0, keepdims=True)

    return pl.pallas_call(
        kern, name=name, grid=(bl, s // ts),
        in_specs=_act_specs(ts, d, 2) + [_vec_spec(d)],
        out_specs=[_act_specs(ts, d, 1)[0], _vec_spec(d)],
        out_shape=[jax.ShapeDtypeStruct((bl, s, d), BF16), jax.ShapeDtypeStruct((bl, 1, d), F32)],
        compiler_params=_cp(2),
    )(dx3, y3, gate)


def _final_fwd_bwd(x1, yf, gate2, g, shf, scf, tgt):
    bl, s, d = x1.shape
    ts = _tile(s, 512, SUBLANES)

    def kern(x1_ref, yf_ref, gate_ref, g_ref, sh_ref, sc_ref, tgt_ref, dx_ref, dsh_ref, dsc_ref, dg_ref, loss_ref):
        b, t = pl.program_id(0), pl.program_id(1)
        gv, sc = g_ref[...], sc_ref[0]
        x2 = x1_ref[0] + gate_ref[0] * yf_ref[0]
        r, xh, n = _norm_parts(x2, gv)
        err = n * (1.0 + sc) + sh_ref[0] - tgt_ref[0]
        dx, a, bb, cc = _norm_bwd_parts(err * (1.0 / d), xh, r, gv, sc)
        dx_ref[0] = dx

        @pl.when(t == 0)
        def _():
            dsh_ref[...] = jnp.zeros_like(dsh_ref)
            dsc_ref[...] = jnp.zeros_like(dsc_ref)

        @pl.when((t == 0) & (b == 0))
        def _():
            dg_ref[...] = jnp.zeros_like(dg_ref)
            loss_ref[...] = jnp.zeros_like(loss_ref)

        dsh_ref[0] += jnp.sum(a, axis=0, keepdims=True)
        dsc_ref[0] += jnp.sum(bb, axis=0, keepdims=True)
        dg_ref[...] += jnp.sum(cc, axis=0, keepdims=True)
        tok = jnp.mean(err * err, axis=-1, keepdims=True)
        loss_ref[...] += 0.5 * jnp.sum(tok, axis=0, keepdims=True)

    return pl.pallas_call(
        kern, name="final_fwd_bwd", grid=(bl, s // ts),
        in_specs=_act_specs(ts, d, 2) + [_vec_spec(d), _par_spec(d), _vec_spec(d), _vec_spec(d)] + _act_specs(ts, d, 1),
        out_specs=[_act_specs(ts, d, 1)[0], _vec_spec(d), _vec_spec(d), _par_spec(d),
                   pl.BlockSpec((1, 1), lambda b, t: (0, 0))],
        out_shape=[jax.ShapeDtypeStruct((bl, s, d), F32), jax.ShapeDtypeStruct((bl, 1, d), F32),
                   jax.ShapeDtypeStruct((bl, 1, d), F32), jax.ShapeDtypeStruct((1, d), F32),
                   jax.ShapeDtypeStruct((1, 1), F32)],
        compiler_params=_cp(2),
    )(x1, yf, gate2, g, shf, scf, tgt)


def _rnn_gates(xr, cw, cb, wa, ba, wi, bi, lam):
    kw = cw.shape[0]
    xc = cb
    for k in range(kw):
        xc = xc + _shift_down(xr, kw - 1 - k) * cw[k:k + 1, :]
    r = _sigmoid(_dot(xc, wa, _NN) + ba)
    i = _sigmoid(_dot(xc, wi, _NN) + bi)
    sp = _softplus(-lam)
    log_a = -LRU_C * r * sp
    a = jnp.exp(log_a)
    mult = jnp.sqrt(-_expm1(2.0 * log_a, a * a))
    return xc, r, i, sp, a, mult


def _segment_scan(a_s, u_s, h_s, p_s, reverse):
    s, c = a_s.shape
    seg = s // SCAN_SEGMENTS

    unroll = math.gcd(seg, 8)

    def steps(n, carry):
        h, p = carry
        for j in range(unroll):
            t = n * unroll + j
            t = (seg - 1 - t) if reverse else t
            av = a_s[pl.ds(t, SCAN_SEGMENTS, stride=seg), :]
            uv = u_s[pl.ds(t, SCAN_SEGMENTS, stride=seg), :]
            h = av * h + uv
            p = p * av
            h_s[pl.ds(t, SCAN_SEGMENTS, stride=seg), :] = h
            p_s[pl.ds(t, SCAN_SEGMENTS, stride=seg), :] = p
        return h, p

    lax.fori_loop(0, seg // unroll, steps, (jnp.zeros((SCAN_SEGMENTS, c), F32), jnp.ones((SCAN_SEGMENTS, c), F32)))
    carry = jnp.zeros((1, c), F32)
    order = range(SCAN_SEGMENTS - 1, -1, -1) if reverse else range(SCAN_SEGMENTS)
    for j in order:
        rows = pl.ds(j * seg, seg)
        fixed = h_s[rows, :] + p_s[rows, :] * carry
        h_s[rows, :] = fixed
        carry = fixed[0:1, :] if reverse else fixed[seg - 1:seg, :]


def _rnn_specs(s, rb, nb):
    act = lambda off: pl.BlockSpec((1, s, rb), lambda b, n, off=off: (b, 0, off + n))
    par = pl.BlockSpec((1, rb), lambda b, n: (0, n))
    wsp = pl.BlockSpec((1, rb, rb), lambda b, n: (n, 0, 0))
    return act, par, wsp


def _rnn_fwd(zr3, cw, cb, wa, ba, wi, bi, lam):
    bl, s, two = zr3.shape
    nb, rb, _ = wa.shape
    dr = nb * rb
    kw = cw.shape[0]
    act, par, wsp = _rnn_specs(s, rb, nb)

    def kern(xr_ref, gr_ref, cw_ref, cb_ref, wa_ref, ba_ref, wi_ref, bi_ref, lam_ref, h_ref, y_ref, a_s, u_s, h_s, p_s):
        xc, r, i, sp, a, mult = _rnn_gates(xr_ref[0], cw_ref[...], cb_ref[...], wa_ref[0], ba_ref[...],
                                           wi_ref[0], bi_ref[...], lam_ref[...])
        a_s[...] = a
        u_s[...] = mult * (i * xc)
        _segment_scan(a_s, u_s, h_s, p_s, reverse=False)
        h = h_s[...]
        h_ref[0] = h
        y_ref[0] = (_gelu(gr_ref[0]) * h).astype(BF16)

    return pl.pallas_call(
        kern, name="rnn_fwd", grid=(bl, nb),
        in_specs=[act(0), act(nb), pl.BlockSpec((kw, rb), lambda b, n: (0, n)), par, wsp, par, wsp, par, par],
        out_specs=[act(0), act(0)],
        out_shape=[jax.ShapeDtypeStruct((bl, s, dr), F32), jax.ShapeDtypeStruct((bl, s, dr), BF16)],
        scratch_shapes=[pltpu.VMEM((s, rb), F32)] * 4,
        compiler_params=_cp(2),
    )(zr3, zr3, cw, cb, wa, ba, wi, bi, lam)


def _rnn_bwd(zr3, h3, dy3, cw, cb, wa, ba, wi, bi, lam):
    bl, s, _ = zr3.shape
    nb, rb, _ = wa.shape
    dr = nb * rb
    kw = cw.shape[0]
    act = lambda off: pl.BlockSpec((1, s, rb), lambda n, b, off=off: (b, 0, off + n))
    par = pl.BlockSpec((1, rb), lambda n, b: (0, n))
    wsp = pl.BlockSpec((1, rb, rb), lambda n, b: (n, 0, 0))
    cws = pl.BlockSpec((kw, rb), lambda n, b: (0, n))

    def kern(xr_ref, gr_ref, h_ref, dy_ref, cw_ref, cb_ref, wa_ref, ba_ref, wi_ref, bi_ref, lam_ref,
             dxr_ref, dgr_ref, dcw_ref, dcb_ref, dwa_ref, dba_ref, dwi_ref, dbi_ref, dlam_ref, a_s, u_s, h_s, p_s):
        b = pl.program_id(1)
        xr, cwv, lamv = xr_ref[0], cw_ref[...], lam_ref[...]
        wav, wiv = wa_ref[0], wi_ref[0]
        xc, r, i, sp, a, mult = _rnn_gates(xr, cwv, cb_ref[...], wav, ba_ref[...], wiv, bi_ref[...], lamv)
        h, dy = h_ref[0], dy_ref[0]
        ge, dge = _gelu_and_grad(gr_ref[0])
        dgr_ref[0] = (dy * h * dge).astype(BF16)
        a_s[...] = _shift_up(a, 1)
        u_s[...] = dy * ge
        _segment_scan(a_s, u_s, h_s, p_s, reverse=True)
        g = h_s[...]
        da = g * _shift_down(h, 1)
        ix = i * xc
        dlog_a = da * a + (g * ix) * (-(a * a) / mult)
        di = g * mult * xc
        dpa = (dlog_a * (-LRU_C * sp)) * r * (1.0 - r)
        dpi = di * i * (1.0 - i)
        dxc = g * mult * i + _dot(dpa, wav, _NT) + _dot(dpi, wiv, _NT)
        dxr = jnp.zeros_like(dxc)
        dcw_rows = []
        for k in range(kw):
            dxr = dxr + _shift_up(dxc, kw - 1 - k) * cwv[k:k + 1, :]
            dcw_rows.append(jnp.sum(dxc * _shift_down(xr, kw - 1 - k), axis=0, keepdims=True))
        dxr_ref[0] = dxr.astype(BF16)

        @pl.when(b == 0)
        def _():
            for ref in (dcw_ref, dcb_ref, dwa_ref, dba_ref, dwi_ref, dbi_ref, dlam_ref):
                ref[...] = jnp.zeros_like(ref)

        for k in range(kw):
            dcw_ref[k:k + 1, :] += dcw_rows[k]
        dcb_ref[...] += jnp.sum(dxc, axis=0, keepdims=True)
        dwa_ref[0] += _dot(xc, dpa, _TN)
        dwi_ref[0] += _dot(xc, dpi, _TN)
        dba_ref[...] += jnp.sum(dpa, axis=0, keepdims=True)
        dbi_ref[...] += jnp.sum(dpi, axis=0, keepdims=True)
        dsp = jnp.sum(dlog_a * (-LRU_C * r), axis=0, keepdims=True)
        dlam_ref[...] += dsp * (-_sigmoid(-lamv))

    vec = jax.ShapeDtypeStruct((1, dr), F32)
    wsh = jax.ShapeDtypeStruct((nb, rb, rb), F32)
    return pl.pallas_call(
        kern, name="rnn_bwd", grid=(nb, bl),
        in_specs=[act(0), act(nb), act(0), act(0), cws, par, wsp, par, wsp, par, par],
        out_specs=[act(0), act(0), cws, par, wsp, par, wsp, par, par],
        out_shape=[jax.ShapeDtypeStruct((bl, s, dr), BF16), jax.ShapeDtypeStruct((bl, s, dr), BF16),
                   jax.ShapeDtypeStruct((kw, dr), F32), vec, wsh, vec, wsh, vec, vec],
        scratch_shapes=[pltpu.VMEM((s, rb), F32)] * 4,
        compiler_params=_cp(2),
    )(zr3, zr3, h3, dy3, cw, cb, wa, ba, wi, bi, lam)


def _tri(n, upper):
    r = lax.broadcasted_iota(jnp.int32, (n, n), 0)
    c = lax.broadcasted_iota(jnp.int32, (n, n), 1)
    return jnp.where((c >= r) if upper else (c <= r), 1.0, 0.0).astype(F32)


def _fgate_fwd(zf3, bf):
    bl, s, w = zf3.shape
    ch = _tile(s, 256, SUBLANES)

    def kern(z_ref, b_ref, f_ref):
        tri = _tri(ch, upper=False)
        carry = jnp.zeros((1, w), F32)
        for j in range(s // ch):
            rows = pl.ds(j * ch, ch)
            lf = -_softplus(-(z_ref[0, rows, :] + b_ref[...]))
            out = jnp.dot(tri, lf, precision=lax.Precision.HIGHEST, preferred_element_type=F32) + carry
            f_ref[0, rows, :] = out
            carry = out[ch - 1:ch, :]

    return pl.pallas_call(
        kern, name="fgate_fwd", grid=(bl,),
        in_specs=[pl.BlockSpec((1, s, w), lambda b: (b, 0, 0)), pl.BlockSpec((1, w), lambda b: (0, 0))],
        out_specs=pl.BlockSpec((1, s, w), lambda b: (b, 0, 0)),
        out_shape=jax.ShapeDtypeStruct((bl, s, w), F32),
        compiler_params=_cp(1),
    )(zf3, bf)


def _fgate_bwd(dfk3, dfq3, zf3, bf):
    bl, s, w = zf3.shape
    ch = _tile(s, 256, SUBLANES)

    def kern(dfk_ref, dfq_ref, z_ref, b_ref, dz_ref, db_ref):
        b = pl.program_id(0)
        tri = _tri(ch, upper=True)
        carry = jnp.zeros((1, w), F32)
        dbsum = jnp.zeros((1, w), F32)
        for j in range(s // ch - 1, -1, -1):
            rows = pl.ds(j * ch, ch)
            df = dfk_ref[0, rows, :] + dfq_ref[0, rows, :]
            dlf = jnp.dot(tri, df, precision=lax.Precision.HIGHEST, preferred_element_type=F32) + carry
            carry = dlf[0:1, :]
            dz = dlf * _sigmoid(-(z_ref[0, rows, :] + b_ref[...]))
            dz_ref[0, rows, :] = dz.astype(BF16)
            dbsum = dbsum + jnp.sum(dz, axis=0, keepdims=True)

        @pl.when(b == 0)
        def _():
            db_ref[...] = jnp.zeros_like(db_ref)

        db_ref[...] += dbsum

    return pl.pallas_call(
        kern, name="fgate_bwd", grid=(bl,),
        in_specs=[pl.BlockSpec((1, s, w), lambda b: (b, 0, 0))] * 3 + [pl.BlockSpec((1, w), lambda b: (0, 0))],
        out_specs=[pl.BlockSpec((1, s, w), lambda b: (b, 0, 0)), pl.BlockSpec((1, w), lambda b: (0, 0))],
        out_shape=[jax.ShapeDtypeStruct((bl, s, w), BF16), jax.ShapeDtypeStruct((1, w), F32)],
        compiler_params=_cp(1),
    )(dfk3, dfq3, zf3, bf)


def _lanes(col, width):
    return col if width == LANES else jnp.concatenate([col] * (width // LANES), axis=1)


def _causal(sc, row0, col0, transposed):
    r = lax.broadcasted_iota(jnp.int32, sc.shape, 0) + row0
    c = lax.broadcasted_iota(jnp.int32, sc.shape, 1) + col0
    return jnp.where((c >= r) if transposed else (r >= c), sc, NEG_BIG)


def _attn_fwd(q3, kv3, fcol, frow, nh):
    bl, s, da = q3.shape
    dh = da // nh
    tq = _tile(s, 512, LANES)
    nq = s // tq

    def kern(iq_tab, ik_tab, q_ref, k_ref, v_ref, fk_ref, fq_ref, o_ref, lse_ref, m_s, l_s, acc):
        iq, ik = iq_tab[pl.program_id(2)], ik_tab[pl.program_id(2)]

        @pl.when(ik == 0)
        def _():
            m_s[...] = jnp.full_like(m_s, NEG_BIG)
            l_s[...] = jnp.zeros_like(l_s)
            acc[...] = jnp.zeros_like(acc)

        def block(masked):
            st = _dot(k_ref[0], q_ref[0], _NT) - _lanes(fk_ref[0], tq) + fq_ref[0]
            if masked:
                st = _causal(st, ik * tq, iq * tq, True)
            m_old = m_s[...]
            m_new = jnp.maximum(m_old, jnp.max(st, axis=0, keepdims=True))
            alpha = jnp.exp(m_old - m_new)
            pt = jnp.exp(st - m_new)
            l_s[...] = alpha * l_s[...] + jnp.sum(pt, axis=0, keepdims=True)
            acc[...] = alpha * acc[...] + _dot(v_ref[0], pt, _TN)
            m_s[...] = m_new

        pl.when(ik < iq)(functools.partial(block, False))

        @pl.when(ik == iq)
        def _():
            block(True)
            l = l_s[...]
            o_ref[0] = (acc[...] / l).T
            lse_ref[0] = m_s[...] + jnp.log(l)

    pairs = [(i, j) for i in range(nq) for j in range(i + 1)]
    iq_tab, ik_tab = (jnp.asarray(col, jnp.int32) for col in zip(*pairs))
    qmap = lambda b, h, p, iqt, ikt: (b, iqt[p], h)
    kmap = lambda off: (lambda b, h, p, iqt, ikt: (b, ikt[p], off + h))
    return pl.pallas_call(
        kern, name="attn_fwd",
        grid_spec=pltpu.PrefetchScalarGridSpec(
            num_scalar_prefetch=2, grid=(bl, nh, len(pairs)),
            in_specs=[pl.BlockSpec((1, tq, dh), qmap), pl.BlockSpec((1, tq, dh), kmap(0)), pl.BlockSpec((1, tq, dh), kmap(nh)),
                      pl.BlockSpec((1, tq, LANES), lambda b, h, p, iqt, ikt: (b * nh + h, ikt[p], 0)),
                      pl.BlockSpec((1, 1, tq), lambda b, h, p, iqt, ikt: (b * nh + h, 0, iqt[p]))],
            out_specs=[pl.BlockSpec((1, tq, dh), qmap),
                       pl.BlockSpec((1, 1, tq), lambda b, h, p, iqt, ikt: (b * nh + h, 0, iqt[p]))],
            scratch_shapes=[pltpu.VMEM((1, tq), F32), pltpu.VMEM((1, tq), F32), pltpu.VMEM((dh, tq), F32)]),
        out_shape=[jax.ShapeDtypeStruct((bl, s, da), F32), jax.ShapeDtypeStruct((bl * nh, 1, s), F32)],
        compiler_params=_cp(3),
    )(iq_tab, ik_tab, q3, kv3, kv3, fcol, frow)


def _attn_bwd(q3, kv3, do3, o3, lse_row, fcol, frow, nh, scale):
    bl, s, da = q3.shape
    dh = da // nh
    tk = _tile(s, 512, LANES)
    nk = s // tk

    pairs = [(j, i) for j in range(nk) for i in range(j, nk)]

    def kern(ik_tab, iq_tab, q_ref, k_ref, v_ref, do_ref, o_ref, lse_ref, fk_ref, fq_ref, dq_ref, dk_ref, dv_ref, dfk_ref,
             dfq_ref, dq_acc, dk_acc, dv_acc, dfq_acc, delta_s):
        step = pl.program_id(2)
        ik, iq = ik_tab[step], iq_tab[step]
        qrow = pl.ds(iq, 1)

        @pl.when(step == 0)
        def _():
            dq_acc[...] = jnp.zeros_like(dq_acc)
            dfq_acc[...] = jnp.zeros_like(dfq_acc)

        @pl.when(ik == 0)
        def _():
            prod = do_ref[0] * o_ref[0]
            rows = lax.dot_general(jnp.ones((SUBLANES, dh), F32), prod, (_NT, ((), ())),
                                   precision=lax.Precision.HIGHEST, preferred_element_type=F32)
            delta_s[qrow, :] = rows[0:1, :]

        @pl.when(iq == ik)
        def _():
            dk_acc[...] = jnp.zeros_like(dk_acc)
            dv_acc[...] = jnp.zeros_like(dv_acc)

        def block(masked):
            q = q_ref[0]
            st = _dot(k_ref[0], q, _NT) - _lanes(fk_ref[0], tk) + fq_ref[0]
            if masked:
                st = _causal(st, ik * tk, iq * tk, True)
            pt = jnp.exp(st - lse_ref[0])
            dv_acc[...] += _dot(pt, do_ref[0], _NN)
            dpt = _dot(v_ref[0], do_ref[0], _NT)
            dst = (pt * (dpt - delta_s[qrow, :])).astype(BF16)
            q_ones = jnp.concatenate([q, jnp.ones_like(q)], axis=1)
            dk_acc[...] += _dot(dst, q_ones, _NN)
            qrows = pl.ds(pl.multiple_of(iq * tk, tk), tk)
            dq_acc[qrows, :] += _dot(dst, k_ref[0], _TN)
            dfq_acc[qrow, :] += jnp.sum(dst.astype(F32), axis=0, keepdims=True)

        pl.when(iq > ik)(functools.partial(block, False))
        pl.when(iq == ik)(functools.partial(block, True))

        @pl.when(iq == nk - 1)
        def _():
            ext = dk_acc[...]
            dk_ref[0] = ext[:, :dh].astype(BF16)
            dfk_ref[0] = -ext[:, dh:]
            dv_ref[0] = dv_acc[...].astype(BF16)

        @pl.when(step == len(pairs) - 1)
        def _():
            dq_ref[0] = (dq_acc[...] * scale).astype(BF16)
            dfq_ref[0] = dfq_acc[...]

    ik_tab, iq_tab = (jnp.asarray(col, jnp.int32) for col in zip(*pairs))
    qmap = lambda b, h, p, ikt, iqt: (b, iqt[p], h)
    omap = lambda b, h, p, ikt, iqt: (b, jnp.where(ikt[p] == 0, iqt[p], 0), h)
    rmap = lambda b, h, p, ikt, iqt: (b * nh + h, 0, iqt[p])
    kmap = lambda off: (lambda b, h, p, ikt, iqt: (b, ikt[p], off + h))
    bmap = lambda b, h, p, ikt, iqt: (b * nh + h, ikt[p], 0)
    return pl.pallas_call(
        kern, name="attn_bwd",
        grid_spec=pltpu.PrefetchScalarGridSpec(
            num_scalar_prefetch=2, grid=(bl, nh, len(pairs)),
            in_specs=[pl.BlockSpec((1, tk, dh), qmap), pl.BlockSpec((1, tk, dh), kmap(0)), pl.BlockSpec((1, tk, dh), kmap(nh)),
                      pl.BlockSpec((1, tk, dh), qmap), pl.BlockSpec((1, tk, dh), omap), pl.BlockSpec((1, 1, tk), rmap),
                      pl.BlockSpec((1, tk, LANES), bmap), pl.BlockSpec((1, 1, tk), rmap)],
            out_specs=[pl.BlockSpec((1, s, dh), lambda b, h, p, ikt, iqt: (b, 0, h)),
                       pl.BlockSpec((1, tk, dh), kmap(0)), pl.BlockSpec((1, tk, dh), kmap(0)),
                       pl.BlockSpec((1, tk, LANES), bmap),
                       pl.BlockSpec((1, nk, tk), lambda b, h, p, ikt, iqt: (b * nh + h, 0, 0))],
            scratch_shapes=[pltpu.VMEM((s, dh), F32), pltpu.VMEM((tk, 2 * dh), F32), pltpu.VMEM((tk, dh), F32),
                            pltpu.VMEM((nk, tk), F32), pltpu.VMEM((nk, tk), F32)]),
        out_shape=[jax.ShapeDtypeStruct((bl, s, da), BF16), jax.ShapeDtypeStruct((bl, s, da), BF16),
                   jax.ShapeDtypeStruct((bl, s, da), BF16), jax.ShapeDtypeStruct((bl * nh, s, LANES), F32),
                   jax.ShapeDtypeStruct((bl * nh, nk, tk), F32)],
        compiler_params=_cp(3),
    )(ik_tab, iq_tab, q3, kv3, kv3, do3, o3, lse_row, fcol, frow)


def _merge_fwd(mg3, pr3, pa3):
    bl, s, d = pr3.shape
    ts = _tile(s, 256, SUBLANES)
    half = lambda j: pl.BlockSpec((1, ts, d), lambda b, t, j=j: (b, t, j))

    def kern(mr_ref, ma_ref, pr_ref, pa_ref, o_ref):
        o_ref[0] = (_sigmoid(mr_ref[0]) * pr_ref[0] + _sigmoid(ma_ref[0]) * pa_ref[0]).astype(BF16)

    return pl.pallas_call(
        kern, name="merge_fwd", grid=(bl, s // ts),
        in_specs=[half(0), half(1)] + _act_specs(ts, d, 2), out_specs=_act_specs(ts, d, 1)[0],
        out_shape=jax.ShapeDtypeStruct((bl, s, d), BF16), compiler_params=_cp(2),
    )(mg3, mg3, pr3, pa3)


def _merge_bwd(dm3, mg3, pr3, pa3):
    bl, s, d = pr3.shape
    ts = _tile(s, 256, SUBLANES)
    half = lambda j: pl.BlockSpec((1, ts, d), lambda b, t, j=j: (b, t, j))

    def kern(dm_ref, mr_ref, ma_ref, pr_ref, pa_ref, dpr_ref, dpa_ref, dmr_ref, dma_ref):
        dm = dm_ref[0]
        gr, ga = _sigmoid(mr_ref[0]), _sigmoid(ma_ref[0])
        dpr_ref[0] = (gr * dm).astype(BF16)
        dpa_ref[0] = (ga * dm).astype(BF16)
        dmr_ref[0] = (dm * pr_ref[0] * gr * (1.0 - gr)).astype(BF16)
        dma_ref[0] = (dm * pa_ref[0] * ga * (1.0 - ga)).astype(BF16)

    return pl.pallas_call(
        kern, name="merge_bwd", grid=(bl, s // ts),
        in_specs=_act_specs(ts, d, 1) + [half(0), half(1)] + _act_specs(ts, d, 2), out_specs=_act_specs(ts, d, 4),
        out_shape=[jax.ShapeDtypeStruct((bl, s, d), BF16)] * 4, compiler_params=_cp(2),
    )(dm3, mg3, mg3, pr3, pa3)


def _ffn_conv(gf, cw, cb):
    kw = cw.shape[0]
    y = cb
    for k in range(kw):
        y = y + _shift_down(gf, kw - 1 - k) * cw[k:k + 1, :]
    return y


def _ffn_act_fwd(up3, cw, cb):
    bl, s, two = up3.shape
    dff = two // 2
    kw = cw.shape[0]
    tc = _tile(dff, 256, LANES)
    nc = dff // tc

    def kern(gf_ref, uf_ref, cw_ref, cb_ref, o_ref):
        o_ref[0] = (_gelu(_ffn_conv(gf_ref[0], cw_ref[...], cb_ref[...])) * uf_ref[0]).astype(BF16)

    act = lambda off: pl.BlockSpec((1, s, tc), lambda b, j, off=off: (b, 0, off + j))
    return pl.pallas_call(
        kern, name="ffn_act_fwd", grid=(bl, nc),
        in_specs=[act(0), act(nc), pl.BlockSpec((kw, tc), lambda b, j: (0, j)), pl.BlockSpec((1, tc), lambda b, j: (0, j))],
        out_specs=act(0), out_shape=jax.ShapeDtypeStruct((bl, s, dff), BF16), compiler_params=_cp(2),
    )(up3, up3, cw, cb)


def _ffn_act_bwd(up3, dact3, cw, cb):
    bl, s, two = up3.shape
    dff = two // 2
    kw = cw.shape[0]
    tc = _tile(dff, 256, LANES)
    nc = dff // tc

    def kern(gf_ref, uf_ref, da_ref, cw_ref, cb_ref, dgf_ref, duf_ref, dcw_ref, dcb_ref):
        b = pl.program_id(1)
        gf, cwv, da = gf_ref[0], cw_ref[...], da_ref[0]
        ge, dge = _gelu_and_grad(_ffn_conv(gf, cwv, cb_ref[...]))
        duf_ref[0] = (da * ge).astype(BF16)
        dgc = da * uf_ref[0] * dge
        dgf = jnp.zeros_like(dgc)
        rows = []
        for k in range(kw):
            dgf = dgf + _shift_up(dgc, kw - 1 - k) * cwv[k:k + 1, :]
            rows.append(jnp.sum(dgc * _shift_down(gf, kw - 1 - k), axis=0, keepdims=True))
        dgf_ref[0] = dgf.astype(BF16)

        @pl.when(b == 0)
        def _():
            dcw_ref[...] = jnp.zeros_like(dcw_ref)
            dcb_ref[...] = jnp.zeros_like(dcb_ref)

        for k in range(kw):
            dcw_ref[k:k + 1, :] += rows[k]
        dcb_ref[...] += jnp.sum(dgc, axis=0, keepdims=True)

    act = lambda off: pl.BlockSpec((1, s, tc), lambda j, b, off=off: (b, 0, off + j))
    cws = pl.BlockSpec((kw, tc), lambda j, b: (0, j))
    cbs = pl.BlockSpec((1, tc), lambda j, b: (0, j))
    return pl.pallas_call(
        kern, name="ffn_act_bwd", grid=(nc, bl),
        in_specs=[act(0), act(nc), act(0), cws, cbs], out_specs=[act(0), act(0), cws, cbs],
        out_shape=[jax.ShapeDtypeStruct((bl, s, dff), BF16), jax.ShapeDtypeStruct((bl, s, dff), BF16),
                   jax.ShapeDtypeStruct((kw, dff), F32), jax.ShapeDtypeStruct((1, dff), F32)],
        compiler_params=_cp(2),
    )(up3, up3, dact3, cw, cb)


_HBM = pl.BlockSpec(memory_space=pltpu.HBM)


def _place():
    x, y, c = lax.axis_index("x"), lax.axis_index("y"), lax.axis_index("c")
    chips = dict(me=2 * x + y, nx=2 * (1 - x) + y, ny=2 * x + (1 - y), diag=2 * (1 - x) + (1 - y))
    peers = dict(nx=(1 - x, y, c), ny=(x, 1 - y, c), sib=(x, y, 1 - c))
    return c, chips, peers


def _remote(src, dst, sems, k, to):
    return pltpu.make_async_remote_copy(src_ref=src, dst_ref=dst, send_sem=sems[0].at[k], recv_sem=sems[1].at[k],
                                        device_id=to, device_id_type=MESH)


RS_STEPS = 2


def _piece(q, idx, n=1):
    start = idx * q
    if not isinstance(start, int):
        start = pl.multiple_of(start, SUBLANES)
    return pl.ds(start, n * q)


def _all_gather_chips(xs, name):
    nt = len(xs)
    per = 9

    def body(*refs):
        x_refs, o_refs = refs[:nt], refs[nt:2 * nt]
        send_sems, recv_sems = refs[2 * nt:]
        c, chip, peer = _place()
        sems = (send_sems, recv_sems)
        me, nx, ny, dg = chip["me"], chip["nx"], chip["ny"], chip["diag"]
        sends = []

        def arrive(k, dst):
            _remote(dst, dst, sems, k, peer["sib"]).wait_recv()

        def pass_on(k, blk, to):
            cp = _remote(blk, blk, sems, k, peer[to])
            cp.start()
            sends.append(cp)

        for t in range(nt):
            q = xs[t].shape[0] // 4
            half = _piece(q, 2 * c, 2)
            for k, to in ((0, "nx"), (1, "ny")):
                cp = _remote(x_refs[t].at[half], o_refs[t].at[me, half], sems, per * t + k, peer[to])
                cp.start()
                sends.append(cp)
            cp = _remote(x_refs[t], o_refs[t].at[me], sems, per * t + 8, peer["sib"])
            cp.start()
            sends.append(cp)
        for t in range(nt):
            q, o, k0 = xs[t].shape[0] // 4, o_refs[t], per * t
            half, sub0, sub1 = _piece(q, 2 * c, 2), _piece(q, 2 * c), _piece(q, 2 * c + 1)
            arrive(k0 + 0, o.at[nx, half])
            pass_on(k0 + 2, o.at[nx, sub0], "ny")
            pass_on(k0 + 4, o.at[nx, half], "sib")
            arrive(k0 + 1, o.at[ny, half])
            pass_on(k0 + 3, o.at[ny, sub1], "nx")
            pass_on(k0 + 5, o.at[ny, half], "sib")
            arrive(k0 + 2, o.at[dg, sub0])
            pass_on(k0 + 6, o.at[dg, sub0], "sib")
            arrive(k0 + 3, o.at[dg, sub1])
            pass_on(k0 + 7, o.at[dg, sub1], "sib")
        for t in range(nt):
            q, o, k0 = xs[t].shape[0] // 4, o_refs[t], per * t
            arrive(k0 + 4, o.at[nx, _piece(q, 2 * (1 - c), 2)])
            arrive(k0 + 5, o.at[ny, _piece(q, 2 * (1 - c), 2)])
            arrive(k0 + 6, o.at[dg, _piece(q, 2 * (1 - c))])
            arrive(k0 + 7, o.at[dg, _piece(q, 2 * (1 - c) + 1)])
            arrive(k0 + 8, o.at[me])
        for cp in sends:
            cp.wait_send()

    return pl.pallas_call(
        body, name=name, in_specs=[_HBM] * nt, out_specs=[_HBM] * nt,
        out_shape=[jax.ShapeDtypeStruct((N_CHIPS,) + x.shape, x.dtype) for x in xs],
        scratch_shapes=[pltpu.SemaphoreType.DMA((per * nt,)), pltpu.SemaphoreType.DMA((per * nt,))],
    )(*xs)


def _all_gather_devices(xs, name):
    nt = len(xs)
    per = 7

    def body(*refs):
        x_refs, o_refs = refs[:nt], refs[nt:2 * nt]
        send_sems, recv_sems, local_sems = refs[2 * nt:]
        x, y, c = lax.axis_index("x"), lax.axis_index("y"), lax.axis_index("c")
        sems = (send_sems, recv_sems)
        sib = (x, y, 1 - c)
        chips = [(1 - x, y), (x, 1 - y), (1 - x, 1 - y)]
        slot = lambda px, py, pc: 4 * px + 2 * py + pc
        me = slot(x, y, c)
        sends, copies = [], []

        def arrive(k, dst):
            _remote(dst, dst, sems, k, sib).wait_recv()

        for t in range(nt):
            cp = pltpu.make_async_copy(x_refs[t], o_refs[t].at[me], local_sems.at[t])
            cp.start()
            copies.append(cp)
            for k, to in enumerate([sib] + [(*chip, c) for chip in chips]):
                cp = _remote(x_refs[t], o_refs[t].at[me], sems, per * t + k, to)
                cp.start()
                sends.append(cp)
        for t in range(nt):
            for j, chip in enumerate(chips):
                blk = o_refs[t].at[slot(*chip, c)]
                arrive(per * t + 1 + j, blk)
                cp = _remote(blk, blk, sems, per * t + 4 + j, sib)
                cp.start()
                sends.append(cp)
        for t in range(nt):
            arrive(per * t, o_refs[t].at[slot(x, y, 1 - c)])
            for j, chip in enumerate(chips):
                arrive(per * t + 4 + j, o_refs[t].at[slot(*chip, 1 - c)])
        for cp in sends:
            cp.wait_send()
        for cp in copies:
            cp.wait()

    return pl.pallas_call(
        body, name=name, in_specs=[_HBM] * nt, out_specs=[_HBM] * nt,
        out_shape=[jax.ShapeDtypeStruct((2 * N_CHIPS,) + a.shape, a.dtype) for a in xs],
        scratch_shapes=[pltpu.SemaphoreType.DMA((per * nt,)), pltpu.SemaphoreType.DMA((per * nt,)),
                        pltpu.SemaphoreType.DMA((nt,))],
    )(*xs)


def _exchange(name, xs, out_shapes, plan):
    nt = len(xs)

    def body(*refs):
        x_refs, o_refs = refs[:nt], refs[nt:2 * nt]
        send_sems, recv_sems = refs[2 * nt:]
        c, chip, peer = _place()
        cps = []
        for t in range(nt):
            for src, dst, to in plan(c, chip, x_refs[t], o_refs[t], xs[t].shape):
                cps.append(_remote(src, dst, (send_sems, recv_sems), len(cps), peer[to]))
        for cp in cps:
            cp.start()
        for cp in cps:
            cp.wait()

    n_copies = nt * len(plan(0, dict(me=0, nx=2, ny=1, diag=3), None, None, xs[0].shape, count_only=True))
    return pl.pallas_call(
        body, name=name, in_specs=[_HBM] * nt, out_specs=[_HBM] * nt,
        out_shape=[jax.ShapeDtypeStruct(s, x.dtype) for s, x in zip(out_shapes, xs)],
        scratch_shapes=[pltpu.SemaphoreType.DMA((n_copies,)), pltpu.SemaphoreType.DMA((n_copies,))],
    )(*xs)


def _plan_sibling(c, chip, g, out, shape, count_only=False):
    if count_only:
        return [None] * N_CHIPS
    q = shape[1] // 4
    return [(g.at[j, _piece(q, 2 * (1 - c), 2)], out.at[j], "sib") for j in range(N_CHIPS)]


def _plan_first(c, chip, p, out, shape, count_only=False):
    if count_only:
        return [None] * 4
    q = shape[1] // 2
    return [(p.at[chip["nx"], _piece(q, 0)], out.at[0], "nx"), (p.at[chip["diag"], _piece(q, 0)], out.at[1], "nx"),
            (p.at[chip["ny"], _piece(q, 1)], out.at[2], "ny"), (p.at[chip["diag"], _piece(q, 1)], out.at[3], "ny")]


def _plan_second(c, chip, p, out, shape, count_only=False):
    if count_only:
        return [None] * 2
    return [(p.at[1], out.at[0], "ny"), (p.at[3], out.at[1], "nx")]


def _rs_last(ps):
    nt = len(ps)

    def body(*refs):
        p_refs, o_refs = refs[:nt], refs[nt:2 * nt]
        send_sems, recv_sems = refs[2 * nt:]
        c, _, peer = _place()
        sems = (send_sems, recv_sems)
        cps = []
        for t in range(nt):
            q = ps[t].shape[0] // 4
            mine = _piece(q, 2 * c, 2)
            cps.append(_remote(p_refs[t].at[mine], o_refs[t].at[mine], sems, t, peer["sib"]))
            cps[-1].start()
        for t in range(nt):
            q = ps[t].shape[0] // 4
            theirs = _piece(q, 2 * (1 - c), 2)
            cps[t].wait_send()
            _remote(p_refs[t].at[theirs], o_refs[t].at[theirs], sems, t, peer["sib"]).wait_recv()

    return pl.pallas_call(
        body, name="rs_last", in_specs=[_HBM] * nt, out_specs=[_HBM] * nt,
        out_shape=[jax.ShapeDtypeStruct(p.shape, F32) for p in ps],
        input_output_aliases={t: t for t in range(nt)},
        scratch_shapes=[pltpu.SemaphoreType.DMA((nt,)), pltpu.SemaphoreType.DMA((nt,))],
    )(*ps)


def _add_stage(name, grid, a_list, b_list, a_map, b_map, tbs, out_shapes, out_map, out_dtype, prefetch=None):
    nt = len(a_list)
    lead = lambda shape: (None,) * (len(shape) - 2)

    def kern(*refs):
        refs = refs[(1 if prefetch is not None else 0):]
        for t in range(nt):
            refs[2 * nt + t][...] = (refs[t][...].astype(F32) + refs[nt + t][...].astype(F32)).astype(out_dtype)

    in_specs = [pl.BlockSpec(lead(a.shape) + (tb, a.shape[-1]), a_map) for a, tb in zip(a_list, tbs)]
    in_specs += [pl.BlockSpec(lead(b.shape) + (tb, b.shape[-1]), b_map) for b, tb in zip(b_list, tbs)]
    out_specs = [pl.BlockSpec(lead(s) + (tb, s[-1]), out_map) for s, tb in zip(out_shapes, tbs)]
    out_shape = [jax.ShapeDtypeStruct(s, out_dtype) for s in out_shapes]
    if prefetch is None:
        return pl.pallas_call(kern, name=name, grid=grid, in_specs=in_specs, out_specs=out_specs, out_shape=out_shape,
                              compiler_params=_cp(len(grid)))(*a_list, *b_list)
    return pl.pallas_call(
        kern, name=name,
        grid_spec=pltpu.PrefetchScalarGridSpec(num_scalar_prefetch=1, grid=grid, in_specs=in_specs, out_specs=out_specs),
        out_shape=out_shape, compiler_params=_cp(len(grid)))(prefetch, *a_list, *b_list)


def _reduce_scatter_chips(gs):
    x, y, c = lax.axis_index("x"), lax.axis_index("y"), lax.axis_index("c")
    me, nx, ny = 2 * x + y, 2 * (1 - x) + y, 2 * x + (1 - y)
    st = RS_STEPS
    unit = 4 * st * 2 * SUBLANES
    rows = [g.shape[1] for g in gs]
    gs = [jnp.pad(g, ((0, 0), (0, -g.shape[1] % unit), (0, 0))) for g in gs]
    qs = [g.shape[1] // 4 for g in gs]
    tbs = [q // st for q in qs]
    cols = [g.shape[2] for g in gs]
    core = jnp.reshape(c, (1,)).astype(jnp.int32)

    got = _exchange("rs_sibling", gs, [(N_CHIPS, 2 * q, cc) for q, cc in zip(qs, cols)], _plan_sibling)
    p0 = _add_stage("rs_add_sibling", (N_CHIPS, 2, st), gs, got,
                    lambda j, h, s, c_ref: (j, (2 * c_ref[0] + h) * st + s, 0), lambda j, h, s, c_ref: (j, h * st + s, 0),
                    tbs, [(N_CHIPS, 2 * q, cc) for q, cc in zip(qs, cols)], lambda j, h, s, c_ref: (j, h * st + s, 0),
                    BF16, prefetch=core)
    got = _exchange("rs_first", p0, [(4, q, cc) for q, cc in zip(qs, cols)], _plan_first)
    p1 = _add_stage("rs_add_first", (4, st), p0, got,
                    lambda k, s, i_ref: (i_ref[k], (k // 2) * st + s, 0), lambda k, s, i_ref: (k, s, 0),
                    tbs, [(4, q, cc) for q, cc in zip(qs, cols)], lambda k, s, i_ref: (k, s, 0),
                    BF16, prefetch=jnp.stack([me, ny, me, nx]).astype(jnp.int32))
    got = _exchange("rs_second", p1, [(2, q, cc) for q, cc in zip(qs, cols)], _plan_second)
    p2 = _add_stage("rs_add_second", (2, st), p1, got, lambda h, s, c_ref: (2 * h, s, 0), lambda h, s, c_ref: (h, s, 0),
                    tbs, [(4 * q, cc) for q, cc in zip(qs, cols)], lambda h, s, c_ref: ((2 * c_ref[0] + h) * st + s, 0),
                    F32, prefetch=core)
    return [out[:r] for out, r in zip(_rs_last(p2), rows)]


def _adamw(g, w, m, v, name):
    rows, cc = g.shape
    tr = _tile(rows, max(SUBLANES, (1 << 18) // cc), SUBLANES)
    k1 = 1.0 - ADAM_B1 ** ADAM_STEP
    k2 = 1.0 - ADAM_B2 ** ADAM_STEP

    def kern(g_ref, w_ref, m_ref, v_ref, d_ref, nm_ref, nv_ref):
        gv = g_ref[...]
        nm = ADAM_B1 * m_ref[...] + (1.0 - ADAM_B1) * gv
        nv = ADAM_B2 * v_ref[...] + (1.0 - ADAM_B2) * (gv * gv)
        nm_ref[...] = nm
        nv_ref[...] = nv
        d_ref[...] = -ADAM_LR * ((nm / k1) / (jnp.sqrt(nv / k2) + ADAM_EPS) + ADAM_WD * w_ref[...])

    spec = pl.BlockSpec((tr, cc), lambda t: (t, 0))
    return pl.pallas_call(
        kern, name=name, grid=(rows // tr,), in_specs=[spec] * 4, out_specs=[spec] * 3,
        out_shape=[jax.ShapeDtypeStruct((rows, cc), F32)] * 3, compiler_params=_cp(1),
    )(g, w, m, v)


def _flat_pad(parts, total):
    flat = jnp.concatenate([p.reshape(-1) for p in parts])
    return jnp.pad(flat, (0, total - flat.shape[0]))


def _split_flat(flat, shapes):
    out, pos = [], 0
    for shp in shapes:
        size = math.prod(shp)
        out.append(flat[pos:pos + size].reshape(shp))
        pos += size
    return out


def _cols_of_chunks(chunks, lo, hi):
    width = chunks.shape[2]
    parts = []
    for j in range(chunks.shape[0]):
        a, b = max(lo, j * width), min(hi, (j + 1) * width)
        if a < b:
            parts.append(chunks[j, :, a - j * width:b - j * width])
    return parts[0] if len(parts) == 1 else jnp.concatenate(parts, axis=1)


def _chunks_of_cols(segments, n_chunks):
    total = sum(s.shape[1] for s in segments)
    width = total // n_chunks
    chunks = []
    for j in range(n_chunks):
        lo, hi, pos, parts = j * width, (j + 1) * width, 0, []
        for s in segments:
            a, b = max(lo, pos), min(hi, pos + s.shape[1])
            if a < b:
                parts.append(s[:, a - pos:b - pos])
            pos += s.shape[1]
        chunks.append(parts[0] if len(parts) == 1 else jnp.concatenate(parts, axis=1))
    return jnp.stack(chunks)


_WEIGHTS = ['w_ada', 'b_ada', 'g_norm1', 'w_in', 'w_rnn_conv', 'b_rnn_conv', 'w_lru_a', 'b_lru_a', 'w_lru_i', 'b_lru_i',
            'lru_lambda', 'b_fgate', 'w_proj_rnn', 'w_proj_attn', 'w_out', 'g_norm2', 'w_ffn_up', 'w_ffn_conv',
            'b_ffn_conv', 'w_ffn_down', 'w_ada_final', 'b_ada_final', 'g_final']
_MATMUL = ['w_in', 'w_proj_rnn', 'w_proj_attn', 'w_out', 'w_ffn_up', 'w_ffn_down']
_ADA = ['w_ada', 'w_ada_final']
_ADA_BIAS = ['b_ada', 'b_ada_final']
_CONV = ['w_rnn_conv', 'w_ffn_conv']
_REPLICATED = [n for n in _WEIGHTS if n not in _MATMUL + _ADA + _ADA_BIAS + _CONV]


def kernel(x, c, w_ada, b_ada, g_norm1, w_in, w_rnn_conv, b_rnn_conv, w_lru_a, b_lru_a, w_lru_i, b_lru_i, lru_lambda, b_fgate, w_proj_rnn, w_proj_attn, w_out, g_norm2, w_ffn_up, w_ffn_conv, b_ffn_conv, w_ffn_down, w_ada_final, b_ada_final, g_final, loss_target, m_w_ada, m_b_ada, m_g_norm1, m_w_in, m_w_rnn_conv, m_b_rnn_conv, m_w_lru_a, m_b_lru_a, m_w_lru_i, m_b_lru_i, m_lru_lambda, m_b_fgate, m_w_proj_rnn, m_w_proj_attn, m_w_out, m_g_norm2, m_w_ffn_up, m_w_ffn_conv, m_b_ffn_conv, m_w_ffn_down, m_w_ada_final, m_b_ada_final, m_g_final, v_w_ada, v_b_ada, v_g_norm1, v_w_in, v_w_rnn_conv, v_b_rnn_conv, v_w_lru_a, v_b_lru_a, v_w_lru_i, v_b_lru_i, v_lru_lambda, v_b_fgate, v_w_proj_rnn, v_w_proj_attn, v_w_out, v_g_norm2, v_w_ffn_up, v_w_ffn_conv, v_b_ffn_conv, v_w_ffn_down, v_w_ada_final, v_b_ada_final, v_g_final):
    args = locals()
    shape_of = {n: args[n].shape for n in _WEIGHTS}

    def view(a):
        if a.ndim >= 3:
            return a[0]
        return a[None, :] if a.ndim == 1 else a

    w2 = {n: view(args[n]) for n in _WEIGHTS}
    m2 = {n: args['m_' + n].reshape(w2[n].shape) for n in _WEIGHTS}
    v2 = {n: args['v_' + n].reshape(w2[n].shape) for n in _WEIGHTS}

    bl, s, d = x.shape
    t = bl * s
    nh = b_fgate.shape[-1]
    nb, rb = w_lru_a.shape[1], w_lru_a.shape[2]
    dr = nb * rb
    da = w2['w_proj_attn'].shape[0] * N_CHIPS
    dh = da // nh
    dff = w2['w_ffn_conv'].shape[1] * N_CHIPS
    scale = dh ** -0.5
    chip = 2 * lax.axis_index("x") + lax.axis_index("y")
    dev = 2 * chip + lax.axis_index("c")

    names = list(_MATMUL)
    gathered = dict(zip(names, _all_gather_chips([w2[n].astype(BF16) for n in names], "ag_weights")))
    n_conv = sum(w2[n].size for n in _CONV)
    rows_conv = -(-n_conv // (FLAT_COLS * 32)) * 32
    conv_local = _flat_pad([w2[n] for n in _CONV], rows_conv * FLAT_COLS).reshape(rows_conv, FLAT_COLS)
    conv_all = _all_gather_chips([conv_local], "ag_conv")[0].reshape(N_CHIPS, -1)
    conv_full, pos = {}, 0
    for n in _CONV:
        r, n4 = w2[n].shape
        blocks = conv_all[:, pos:pos + r * n4].reshape(N_CHIPS, r, n4)
        conv_full[n] = jnp.concatenate([blocks[j] for j in range(N_CHIPS)], axis=1)
        pos += r * n4
    rowmajor = lambda n: gathered[n].reshape(-1, gathered[n].shape[2])
    w_proj_rnn_f, w_proj_attn_f, w_out_f, w_ffn_down_f = (rowmajor(n) for n in ('w_proj_rnn', 'w_proj_attn', 'w_out', 'w_ffn_down'))
    up_chunk = w2['w_ffn_up'].shape[1]

    o_q, o_k, o_fl = 2 * dr, 2 * dr + da, 2 * dr + 3 * da
    o_mg = o_fl + nh
    g_in_w = gathered['w_in']
    w_rnn, w_q = _cols_of_chunks(g_in_w, 0, o_q), _cols_of_chunks(g_in_w, o_q, o_k)
    w_kv, w_mg = _cols_of_chunks(g_in_w, o_k, o_fl), _cols_of_chunks(g_in_w, o_mg, o_mg + 2 * d)
    w_fl = jnp.pad(_cols_of_chunks(g_in_w, o_fl, o_mg), ((0, 0), (0, LANES - nh)))
    w_rest = jnp.concatenate([w_q, w_kv, w_mg], axis=1)
    bf_pad = jnp.pad(w2['b_fgate'], ((0, 0), (0, LANES - nh)))

    nd = 2 * N_CHIPS
    c_act = _silu_pad(_all_gather_devices([c], "ag_cond")[0].reshape(nd * bl, d), nd * bl)
    my_cols = lambda a, n: lax.dynamic_slice_in_dim(a, chip * w2[n].shape[1], w2[n].shape[1], axis=1)
    mod_cols = [_mm(c_act, w2[n].astype(BF16), "nn", name=n + "_fwd", bias=my_cols(w2[b], n)) for n, b in zip(_ADA, _ADA_BIAS)]
    my_rows = lambda g: lax.dynamic_slice_in_dim(g, dev * bl, bl, axis=1).transpose(1, 0, 2).reshape(bl, -1)
    mod, modf = (my_rows(g) for g in _all_gather_chips(mod_cols, "ag_mod"))
    sh1, sc1, gt1, sh2, sc2, gt2 = [mod[:, i * d:(i + 1) * d].reshape(bl, 1, d) for i in range(6)]
    shf, scf = modf[:, :d].reshape(bl, 1, d), modf[:, d:].reshape(bl, 1, d)

    h1 = _norm_mod_fwd(x, w2['g_norm1'], sh1, sc1)
    h1f = h1.reshape(t, d)
    zr = _mm(h1f, w_rnn, "nn", name="in_rnn").reshape(bl, s, 2 * dr)
    q3 = _mm(h1f, w_q, "nn", name="in_q", out_dtype=BF16, scale=scale).reshape(bl, s, da)
    kv3 = _mm(h1f, w_kv, "nn", name="in_kv", out_dtype=BF16).reshape(bl, s, 2 * da)
    mg3 = _mm(h1f, w_mg, "nn", name="in_mg").reshape(bl, s, 2 * d)
    zf3 = _mm(h1f, w_fl, "nn", name="in_fl").reshape(bl, s, LANES)

    lru = (conv_full['w_rnn_conv'], w2['b_rnn_conv'], w2['w_lru_a'], w2['b_lru_a'], w2['w_lru_i'], w2['b_lru_i'], w2['lru_lambda'])
    hseq, y_rnn = _rnn_fwd(zr, *lru)

    f3 = _fgate_fwd(zf3, bf_pad)
    f_heads = f3[:, :, :nh].transpose(0, 2, 1).reshape(bl * nh, s)
    fcol = jnp.broadcast_to(f_heads[:, :, None], (bl * nh, s, LANES))
    frow = f_heads.reshape(bl * nh, 1, s)
    o3, lse_row = _attn_fwd(q3, kv3, fcol, frow, nh)

    pr3 = _mm(y_rnn.reshape(t, dr), w_proj_rnn_f, "nn", name="proj_rnn").reshape(bl, s, d)
    pa3 = _mm(o3.reshape(t, da), w_proj_attn_f, "nn", name="proj_attn").reshape(bl, s, d)
    merged = _merge_fwd(mg3, pr3, pa3)
    mo3 = _mm(merged.reshape(t, d), w_out_f, "nn", name="mix_out").reshape(bl, s, d)
    x1, h2 = _resid_norm_fwd(x, mo3, gt1, w2['g_norm2'], sh2, sc2)
    h2f = h2.reshape(t, d)
    up3 = _mm(h2f, gathered['w_ffn_up'], "nn", name="ffn_up", b_chunk=up_chunk, tn=up_chunk).reshape(bl, s, 2 * dff)
    act3 = _ffn_act_fwd(up3, conv_full['w_ffn_conv'], w2['b_ffn_conv'])
    yf3 = _mm(act3.reshape(t, dff), w_ffn_down_f, "nn", name="ffn_down").reshape(bl, s, d)

    dx2, dshf, dscf, dg_final, loss_part = _final_fwd_bwd(x1, yf3, gt2, w2['g_final'], shf, scf, loss_target)
    loss = lax.psum(loss_part[0, 0], ("x", "y", "c"))

    dyf, dgt2 = _gate_bwd(dx2, yf3, gt2, "ffn_gate_bwd")
    dyf_f = dyf.reshape(t, d)
    g_ffn_down = _mm(act3.reshape(t, dff), dyf_f, "tn", name="dw_ffn_down")
    dact3 = _mm(dyf_f, w_ffn_down_f, "nt", name="d_ffn_act").reshape(bl, s, dff)
    dgf, duf, g_ffn_conv, g_b_ffn_conv = _ffn_act_bwd(up3, dact3, conv_full['w_ffn_conv'], w2['b_ffn_conv'])
    dgf_f, duf_f = dgf.reshape(t, dff), duf.reshape(t, dff)
    g_ffn_up = jnp.concatenate([_mm(h2f, dgf_f, "tn", name="dw_ffn_up_gate", out_chunk=up_chunk),
                                _mm(h2f, duf_f, "tn", name="dw_ffn_up_value", out_chunk=up_chunk)], axis=0)
    dh2 = _mm([dgf_f, duf_f], gathered['w_ffn_up'], "nt", name="d_h2", b_chunk=up_chunk).reshape(bl, s, d)
    dx1, dsh2, dsc2, dg_norm2 = _norm_mod_bwd(dh2, x1, dx2, w2['g_norm2'], sc2, "norm2_bwd")

    dmo, dgt1 = _gate_bwd(dx1, mo3, gt1, "mix_gate_bwd")
    dmo_f = dmo.reshape(t, d)
    g_out = _mm(merged.reshape(t, d), dmo_f, "tn", name="dw_out")
    dm3 = _mm(dmo_f, w_out_f, "nt", name="d_merged").reshape(bl, s, d)
    dpr, dpa, dmr, dma = _merge_bwd(dm3, mg3, pr3, pa3)
    g_proj_rnn = _mm(y_rnn.reshape(t, dr), dpr.reshape(t, d), "tn", name="dw_proj_rnn")
    g_proj_attn = _mm(o3.reshape(t, da), dpa.reshape(t, d), "tn", name="dw_proj_attn")
    dyr3 = _mm(dpr.reshape(t, d), w_proj_rnn_f, "nt", name="d_y_rnn").reshape(bl, s, dr)
    do3 = _mm(dpa.reshape(t, d), w_proj_attn_f, "nt", name="d_y_attn").reshape(bl, s, da)

    dq3, dk3, dv3, dfk, dfq = _attn_bwd(q3, kv3, do3, o3, lse_row, fcol, frow, nh, scale)
    heads_last = lambda a: jnp.pad(a.reshape(bl, nh, s).transpose(0, 2, 1), ((0, 0), (0, 0), (0, LANES - nh)))
    dzf3, g_bf = _fgate_bwd(heads_last(dfk[:, :, 0]), heads_last(dfq), zf3, bf_pad)

    dxr, dgr, g_rnn_conv, g_b_rnn_conv, g_lru_a, g_b_lru_a, g_lru_i, g_b_lru_i, g_lam = _rnn_bwd(zr, hseq, dyr3, *lru)

    dz = [a.reshape(t, -1) for a in (dxr, dgr, dq3, dk3, dv3, dmr, dma)]
    dzf_f = dzf3.reshape(t, LANES)
    seg_names = ("xr", "gr", "q", "k", "v", "mr", "ma")
    g_seg = [_mm(h1f, a, "tn", name="dw_in_" + n) for n, a in zip(seg_names, dz)]
    g_in_fl = _mm(h1f, dzf_f, "tn", name="dw_in_fl")[:, :nh]
    g_in = _chunks_of_cols(g_seg[:5] + [g_in_fl] + g_seg[5:], N_CHIPS)
    dh1 = _mm(dzf_f, w_fl, "nt", name="d_h1_fl")
    dh1 = _mm(dz[:2], w_rnn, "nt", name="d_h1_rnn", add=dh1)
    dh1 = _mm(dz[2:], w_rest, "nt", name="d_h1", add=dh1).reshape(bl, s, d)
    grad_x, dsh1, dsc1, dg_norm1 = _norm_mod_bwd(dh1, x, dx1, w2['g_norm1'], sc1, "norm1_bwd")

    dmods = [jnp.concatenate([dsh1, dsc1, dgt1, dsh2, dsc2, dgt2], axis=-1).reshape(bl, -1),
             jnp.concatenate([dshf, dscf], axis=-1).reshape(bl, -1)]
    dmods = [g.reshape(nd * bl, -1) for g in _all_gather_devices(dmods, "ag_dmod")]
    grad = {n: _mm(c_act, my_cols(g, n), "tn", name="dw_" + n) for n, g in zip(_ADA, dmods)}
    grad.update({b: _rowsum(g, "d" + b) for b, g in zip(_ADA_BIAS, dmods)})

    rowchunks = lambda g: g.reshape(N_CHIPS, g.shape[0] // N_CHIPS, g.shape[1])
    full = dict(w_in=g_in, w_proj_rnn=rowchunks(g_proj_rnn), w_proj_attn=rowchunks(g_proj_attn),
                w_out=rowchunks(g_out), w_ffn_up=g_ffn_up, w_ffn_down=rowchunks(g_ffn_down))
    small = dict(g_norm1=dg_norm1, w_rnn_conv=g_rnn_conv, b_rnn_conv=g_b_rnn_conv, w_lru_a=g_lru_a,
                 b_lru_a=g_b_lru_a, w_lru_i=g_lru_i, b_lru_i=g_b_lru_i, lru_lambda=g_lam, b_fgate=g_bf[:, :nh],
                 g_norm2=dg_norm2, w_ffn_conv=g_ffn_conv, b_ffn_conv=g_b_ffn_conv, g_final=dg_final)

    small_names = _REPLICATED + _CONV
    n_small = sum(small[n].size for n in small_names)
    rows_q = -(-n_small // (N_CHIPS * FLAT_COLS * 32 * RS_STEPS)) * 32 * RS_STEPS
    small_flat = _flat_pad([small[n] for n in small_names], N_CHIPS * rows_q * FLAT_COLS).reshape(N_CHIPS, rows_q, FLAT_COLS)
    reduced = _reduce_scatter_chips([full[n] for n in names] + [small_flat])
    grad.update(zip(names, reduced[:-1]))
    small_all = _all_gather_chips([reduced[-1]], "ag_small_grads")[0].reshape(-1)
    grad.update(zip(small_names, _split_flat(small_all, [small[n].shape for n in small_names])))
    for n in _CONV:
        n4 = w2[n].shape[1]
        grad[n] = lax.dynamic_slice_in_dim(grad[n], chip * n4, n4, axis=1)

    delta_w, new_m, new_v = {}, {}, {}
    for n in names + _ADA:
        delta_w[n], new_m[n], new_v[n] = _adamw(grad[n], w2[n], m2[n], v2[n], "adamw_" + n)
    small_names = small_names + _ADA_BIAS
    rows_small = -(-sum(w2[n].size for n in small_names) // (FLAT_COLS * SUBLANES)) * SUBLANES
    flat_small = lambda src: _flat_pad([src[n] for n in small_names], rows_small * FLAT_COLS).reshape(rows_small, FLAT_COLS)
    small_out = _adamw(flat_small(grad), flat_small(w2), flat_small(m2), flat_small(v2), "adamw_small")
    for dst, flat in zip((delta_w, new_m, new_v), small_out):
        dst.update(zip(small_names, _split_flat(flat.reshape(-1), [w2[n].shape for n in small_names])))

    out = [loss, grad_x]
    for src in (grad, delta_w, new_m, new_v):
        out += [src[n].reshape(shape_of[n]) for n in _WEIGHTS]
    return tuple(out)
```

```python
import functools
import math

import jax
import jax.numpy as jnp
from jax import lax
from jax.experimental import pallas as pl
from jax.experimental.pallas import tpu as pltpu

F32 = jnp.float32
BF16 = jnp.bfloat16
MESH = pl.DeviceIdType.MESH

RMS_EPS = 1e-6
LRU_C = 8.0
ADAM_LR = 0.001
ADAM_B1 = 0.9
ADAM_B2 = 0.999
ADAM_EPS = 1e-08
ADAM_WD = 0.01
ADAM_STEP = 10

LANES = 128
SUBLANES = 8
N_CHIPS = 4
FLAT_COLS = 1024
SCAN_SEGMENTS = 2 * SUBLANES
VMEM_LIMIT = 48 * 1024 * 1024
NEG_BIG = -1e30


def _cp(n_axes):
    return pltpu.CompilerParams(dimension_semantics=("arbitrary",) * n_axes, vmem_limit_bytes=VMEM_LIMIT)


def _tile(n, target, align):
    if n <= target:
        return n
    t = (target // align) * align
    while t >= align:
        if n % t == 0:
            return t
        t -= align
    return n


def _nice_rows(n, align):
    r = -(-n // align) * align
    while True:
        if r <= 640:
            return r, r
        t = _tile(r, 640, align)
        if 128 <= t <= 640:
            return r, t
        r += align


def _sigmoid(x):
    return 0.5 * jnp.tanh(0.5 * x) + 0.5


def _softplus(x):
    return jnp.maximum(x, 0.0) + jnp.log1p(jnp.exp(-jnp.abs(x)))


def _expm1(x, exp_x):
    small = x * (1.0 + 0.5 * x * (1.0 + (1.0 / 3.0) * x * (1.0 + 0.25 * x)))
    return jnp.where(jnp.abs(x) < 0.05, small, exp_x - 1.0)


_GELU_K = math.sqrt(2.0 / math.pi)
_GELU_C = 0.044715


def _gelu(x):
    t = jnp.tanh(_GELU_K * (x + _GELU_C * x * x * x))
    return 0.5 * x * (1.0 + t)


def _gelu_and_grad(x):
    t = jnp.tanh(_GELU_K * (x + _GELU_C * x * x * x))
    g = 0.5 * x * (1.0 + t)
    dg = 0.5 * (1.0 + t) + 0.5 * x * (1.0 - t * t) * _GELU_K * (1.0 + 3.0 * _GELU_C * x * x)
    return g, dg


def _shift_down(x, k):
    if k == 0:
        return x
    y = pltpu.roll(x, k, 0)
    rows = lax.broadcasted_iota(jnp.int32, (SUBLANES, x.shape[1]), 0)
    return jnp.concatenate([jnp.where(rows >= k, y[:SUBLANES], 0.0), y[SUBLANES:]], axis=0)


def _shift_up(x, k):
    if k == 0:
        return x
    s = x.shape[0]
    y = pltpu.roll(x, s - k, 0)
    rows = lax.broadcasted_iota(jnp.int32, (SUBLANES, x.shape[1]), 0)
    return jnp.concatenate([y[:s - SUBLANES], jnp.where(rows < SUBLANES - k, y[s - SUBLANES:], 0.0)], axis=0)


def _dot(a, b, dims):
    return lax.dot_general(a.astype(BF16), b.astype(BF16), (dims, ((), ())), preferred_element_type=F32)


_NN = ((1,), (0,))
_NT = ((1,), (1,))
_TN = ((0,), (0,))


def _mm(a, b, mode, *, name, out_dtype=F32, scale=None, bias=None, add=None, tm=1024, tn=1024, tk=1024,
        b_chunk=None, out_chunk=None):
    pieces = list(a) if isinstance(a, (list, tuple)) else [a]
    ksize = lambda p: p.shape[0] if mode == "tn" else p.shape[1]
    if b_chunk is None:
        brows, bcols = b.shape
    else:
        brows, bcols = b.shape[1], b.shape[0] * b_chunk
    k = sum(ksize(p) for p in pieces)
    if mode == "nt":
        m, n = pieces[0].shape[0], brows
        assert bcols == k, (bcols, k)
    else:
        m, n = (pieces[0].shape[1] if mode == "tn" else pieces[0].shape[0]), bcols
        assert brows == k, (brows, k)
    tm = _tile(m, tm, LANES)
    ncut = n
    if b_chunk is not None and mode != "nt":
        ncut = b_chunk
    if out_chunk is not None:
        ncut = math.gcd(ncut, out_chunk)
    tn = _tile(ncut, tn, LANES)
    kcut = b_chunk if (b_chunk is not None and mode == "nt") else k
    for p in pieces:
        kcut = math.gcd(kcut, ksize(p))
    tk = _tile(kcut, tk, LANES)
    nk = k // tk
    dims = {"nn": _NN, "nt": _NT, "tn": _TN}[mode]
    counts = [ksize(p) // tk for p in pieces]
    starts = [sum(counts[:i]) for i in range(len(pieces))]
    n_pieces = len(pieces)

    def a_spec(s0, cnt):
        kmap = (lambda kk: kk) if n_pieces == 1 else (lambda kk: jnp.clip(kk - s0, 0, cnt - 1))
        if mode == "tn":
            return pl.BlockSpec((tk, tm), lambda i, j, kk: (kmap(kk), i))
        return pl.BlockSpec((tm, tk), lambda i, j, kk: (i, kmap(kk)))

    if b_chunk is None:
        if mode == "nt":
            b_spec = pl.BlockSpec((tn, tk), lambda i, j, kk: (j, kk))
        else:
            b_spec = pl.BlockSpec((tk, tn), lambda i, j, kk: (kk, j))
    elif mode == "nt":
        per_b = b_chunk // tk
        b_spec = pl.BlockSpec((None, tn, tk), lambda i, j, kk: (kk // per_b, j, kk % per_b))
    else:
        per_b = b_chunk // tn
        b_spec = pl.BlockSpec((None, tk, tn), lambda i, j, kk: (j // per_b, kk, j % per_b))
    if out_chunk is None:
        out_spec = pl.BlockSpec((tm, tn), lambda i, j, kk: (i, j))
        out_shape = jax.ShapeDtypeStruct((m, n), out_dtype)
    else:
        per_o = out_chunk // tn
        out_spec = pl.BlockSpec((None, tm, tn), lambda i, j, kk: (j // per_o, i, j % per_o))
        out_shape = jax.ShapeDtypeStruct((n // out_chunk, m, out_chunk), out_dtype)
    in_specs = [a_spec(s0, cnt) for s0, cnt in zip(starts, counts)] + [b_spec]
    args = pieces + [b]
    if bias is not None:
        in_specs.append(pl.BlockSpec((1, tn), lambda i, j, kk: (0, j)))
        args.append(bias)
    if add is not None:
        in_specs.append(pl.BlockSpec((tm, tn), lambda i, j, kk: (i, j)))
        args.append(add)

    def kern(*refs):
        b_ref = refs[n_pieces]
        o_ref = refs[n_pieces + 1 + (bias is not None) + (add is not None)]

        def finish(r):
            if scale is not None:
                r = r * scale
            pos = n_pieces + 1
            if bias is not None:
                r = r + refs[pos][...]
                pos += 1
            if add is not None:
                r = r + refs[pos][...]
            o_ref[...] = r.astype(out_dtype)

        if nk == 1:
            finish(_dot(refs[0][...], b_ref[...], dims))
            return
        acc = refs[-1]
        kk = pl.program_id(2)

        @pl.when(kk == 0)
        def _():
            acc[...] = jnp.zeros_like(acc)

        if n_pieces == 1:
            acc[...] += _dot(refs[0][...], b_ref[...], dims)
        else:
            for idx in range(n_pieces):
                @pl.when((kk >= starts[idx]) & (kk < starts[idx] + counts[idx]))
                def _(idx=idx):
                    acc[...] += _dot(refs[idx][...], b_ref[...], dims)

        @pl.when(kk == nk - 1)
        def _():
            finish(acc[...])

    return pl.pallas_call(
        kern, name=name,
        grid=(m // tm, n // tn, nk),
        in_specs=in_specs, out_specs=out_spec, out_shape=out_shape,
        scratch_shapes=[pltpu.VMEM((tm, tn), F32)] if nk > 1 else [],
        compiler_params=_cp(3),
    )(*args)


def _silu_pad(c, rows):
    bl, d = c.shape

    def kern(c_ref, o_ref):
        o_ref[...] = jnp.zeros_like(o_ref)
        v = c_ref[...]
        o_ref[0:bl, :] = v * _sigmoid(v)

    return pl.pallas_call(kern, name="silu_pad", out_shape=jax.ShapeDtypeStruct((rows, d), F32))(c)


def _rowsum(x, name):
    r, n = x.shape

    def kern(x_ref, o_ref):
        o_ref[...] = jnp.sum(x_ref[...], axis=0, keepdims=True)

    return pl.pallas_call(kern, name=name, out_shape=jax.ShapeDtypeStruct((1, n), F32))(x)


def _norm_parts(x, g):
    r = lax.rsqrt(jnp.mean(x * x, axis=-1, keepdims=True) + RMS_EPS)
    xh = x * r
    return r, xh, xh * g


def _norm_bwd_parts(dh, xh, r, g, sc):
    n = xh * g
    dn = dh * (1.0 + sc)
    dxh = dn * g
    dx = r * (dxh - xh * jnp.mean(dxh * xh, axis=-1, keepdims=True))
    return dx, dh, dh * n, dn * xh


def _act_specs(ts, d, n):
    return [pl.BlockSpec((1, ts, d), lambda b, t: (b, t, 0)) for _ in range(n)]


def _vec_spec(d):
    return pl.BlockSpec((1, 1, d), lambda b, t: (b, 0, 0))


def _par_spec(d):
    return pl.BlockSpec((1, d), lambda b, t: (0, 0))


def _norm_mod_fwd(x3, g, sh, sc):
    bl, s, d = x3.shape
    ts = _tile(s, 512, SUBLANES)

    def kern(x_ref, g_ref, sh_ref, sc_ref, h_ref):
        _, _, n = _norm_parts(x_ref[0], g_ref[...])
        h_ref[0] = (n * (1.0 + sc_ref[0]) + sh_ref[0]).astype(BF16)

    return pl.pallas_call(
        kern, name="norm_mod_fwd", grid=(bl, s // ts),
        in_specs=_act_specs(ts, d, 1) + [_par_spec(d), _vec_spec(d), _vec_spec(d)],
        out_specs=_act_specs(ts, d, 1)[0],
        out_shape=jax.ShapeDtypeStruct((bl, s, d), BF16),
        compiler_params=_cp(2),
    )(x3, g, sh, sc)


def _resid_norm_fwd(x3, y3, gate, g, sh, sc):
    bl, s, d = x3.shape
    ts = _tile(s, 512, SUBLANES)

    def kern(x_ref, y_ref, gate_ref, g_ref, sh_ref, sc_ref, x1_ref, h_ref):
        x1 = x_ref[0] + gate_ref[0] * y_ref[0]
        x1_ref[0] = x1
        _, _, n = _norm_parts(x1, g_ref[...])
        h_ref[0] = (n * (1.0 + sc_ref[0]) + sh_ref[0]).astype(BF16)

    return pl.pallas_call(
        kern, name="resid_norm_fwd", grid=(bl, s // ts),
        in_specs=_act_specs(ts, d, 2) + [_vec_spec(d), _par_spec(d), _vec_spec(d), _vec_spec(d)],
        out_specs=_act_specs(ts, d, 2),
        out_shape=[jax.ShapeDtypeStruct((bl, s, d), F32), jax.ShapeDtypeStruct((bl, s, d), BF16)],
        compiler_params=_cp(2),
    )(x3, y3, gate, g, sh, sc)


def _norm_mod_bwd(dh3, x3, dres3, g, sc, name):
    bl, s, d = x3.shape
    ts = _tile(s, 512, SUBLANES)

    def kern(dh_ref, x_ref, dres_ref, g_ref, sc_ref, dx_ref, dsh_ref, dsc_ref, dg_ref):
        b, t = pl.program_id(0), pl.program_id(1)
        gv = g_ref[...]
        r, xh, _ = _norm_parts(x_ref[0], gv)
        dx, a, bb, cc = _norm_bwd_parts(dh_ref[0], xh, r, gv, sc_ref[0])
        dx_ref[0] = dres_ref[0] + dx

        @pl.when(t == 0)
        def _():
            dsh_ref[...] = jnp.zeros_like(dsh_ref)
            dsc_ref[...] = jnp.zeros_like(dsc_ref)

        @pl.when((t == 0) & (b == 0))
        def _():
            dg_ref[...] = jnp.zeros_like(dg_ref)

        dsh_ref[0] += jnp.sum(a, axis=0, keepdims=True)
        dsc_ref[0] += jnp.sum(bb, axis=0, keepdims=True)
        dg_ref[...] += jnp.sum(cc, axis=0, keepdims=True)

    return pl.pallas_call(
        kern, name=name, grid=(bl, s // ts),
        in_specs=_act_specs(ts, d, 3) + [_par_spec(d), _vec_spec(d)],
        out_specs=[_act_specs(ts, d, 1)[0], _vec_spec(d), _vec_spec(d), _par_spec(d)],
        out_shape=[jax.ShapeDtypeStruct((bl, s, d), F32), jax.ShapeDtypeStruct((bl, 1, d), F32),
                   jax.ShapeDtypeStruct((bl, 1, d), F32), jax.ShapeDtypeStruct((1, d), F32)],
        compiler_params=_cp(2),
    )(dh3, x3, dres3, g, sc)


def _gate_bwd(dx3, y3, gate, name):
    bl, s, d = dx3.shape
    ts = _tile(s, 512, SUBLANES)

    def kern(dx_ref, y_ref, gate_ref, dy_ref, dgate_ref):
        t = pl.program_id(1)
        dx = dx_ref[0]
        dy_ref[0] = (gate_ref[0] * dx).astype(BF16)

        @pl.when(t == 0)
        def _():
            dgate_ref[...] = jnp.zeros_like(dgate_ref)

        dgate_ref[0] += jnp.sum(dx * y_ref[0], axis=0, keepdims=True)

    return pl.pallas_call(
        kern, name=name, grid=(bl, s // ts),
        in_specs=_act_specs(ts, d, 2) + [_vec_spec(d)],
        out_specs=[_act_specs(ts, d, 1)[0], _vec_spec(d)],
        out_shape=[jax.ShapeDtypeStruct((bl, s, d), BF16), jax.ShapeDtypeStruct((bl, 1, d), F32)],
        compiler_params=_cp(2),
    )(dx3, y3, gate)


def _final_fwd_bwd(x1, yf, gate2, g, shf, scf, tgt):
    bl, s, d = x1.shape
    ts = _tile(s, 512, SUBLANES)

    def kern(x1_ref, yf_ref, gate_ref, g_ref, sh_ref, sc_ref, tgt_ref, dx_ref, dsh_ref, dsc_ref, dg_ref, loss_ref):
        b, t = pl.program_id(0), pl.program_id(1)
        gv, sc = g_ref[...], sc_ref[0]
        x2 = x1_ref[0] + gate_ref[0] * yf_ref[0]
        r, xh, n = _norm_parts(x2, gv)
        err = n * (1.0 + sc) + sh_ref[0] - tgt_ref[0]
        dx, a, bb, cc = _norm_bwd_parts(err * (1.0 / d), xh, r, gv, sc)
        dx_ref[0] = dx

        @pl.when(t == 0)
        def _():
            dsh_ref[...] = jnp.zeros_like(dsh_ref)
            dsc_ref[...] = jnp.zeros_like(dsc_ref)

        @pl.when((t == 0) & (b == 0))
        def _():
            dg_ref[...] = jnp.zeros_like(dg_ref)
            loss_ref[...] = jnp.zeros_like(loss_ref)

        dsh_ref[0] += jnp.sum(a, axis=0, keepdims=True)
        dsc_ref[0] += jnp.sum(bb, axis=0, keepdims=True)
        dg_ref[...] += jnp.sum(cc, axis=0, keepdims=True)
        tok = jnp.mean(err * err, axis=-1, keepdims=True)
        loss_ref[...] += 0.5 * jnp.sum(tok, axis=0, keepdims=True)

    return pl.pallas_call(
        kern, name="final_fwd_bwd", grid=(bl, s // ts),
        in_specs=_act_specs(ts, d, 2) + [_vec_spec(d), _par_spec(d), _vec_spec(d), _vec_spec(d)] + _act_specs(ts, d, 1),
        out_specs=[_act_specs(ts, d, 1)[0], _vec_spec(d), _vec_spec(d), _par_spec(d),
                   pl.BlockSpec((1, 1), lambda b, t: (0, 0))],
        out_shape=[jax.ShapeDtypeStruct((bl, s, d), F32), jax.ShapeDtypeStruct((bl, 1, d), F32),
                   jax.ShapeDtypeStruct((bl, 1, d), F32), jax.ShapeDtypeStruct((1, d), F32),
                   jax.ShapeDtypeStruct((1, 1), F32)],
        compiler_params=_cp(2),
    )(x1, yf, gate2, g, shf, scf, tgt)


def _rnn_gates(xr, cw, cb, wa, ba, wi, bi, lam):
    kw = cw.shape[0]
    xc = cb
    for k in range(kw):
        xc = xc + _shift_down(xr, kw - 1 - k) * cw[k:k + 1, :]
    r = _sigmoid(_dot(xc, wa, _NN) + ba)
    i = _sigmoid(_dot(xc, wi, _NN) + bi)
    sp = _softplus(-lam)
    log_a = -LRU_C * r * sp
    a = jnp.exp(log_a)
    mult = jnp.sqrt(-_expm1(2.0 * log_a, a * a))
    return xc, r, i, sp, a, mult


def _segment_scan(a_s, u_s, h_s, p_s, reverse):
    s, c = a_s.shape
    seg = s // SCAN_SEGMENTS

    unroll = math.gcd(seg, 8)

    def steps(n, carry):
        h, p = carry
        for j in range(unroll):
            t = n * unroll + j
            t = (seg - 1 - t) if reverse else t
            av = a_s[pl.ds(t, SCAN_SEGMENTS, stride=seg), :]
            uv = u_s[pl.ds(t, SCAN_SEGMENTS, stride=seg), :]
            h = av * h + uv
            p = p * av
            h_s[pl.ds(t, SCAN_SEGMENTS, stride=seg), :] = h
            p_s[pl.ds(t, SCAN_SEGMENTS, stride=seg), :] = p
        return h, p

    lax.fori_loop(0, seg // unroll, steps, (jnp.zeros((SCAN_SEGMENTS, c), F32), jnp.ones((SCAN_SEGMENTS, c), F32)))
    carry = jnp.zeros((1, c), F32)
    order = range(SCAN_SEGMENTS - 1, -1, -1) if reverse else range(SCAN_SEGMENTS)
    for j in order:
        rows = pl.ds(j * seg, seg)
        fixed = h_s[rows, :] + p_s[rows, :] * carry
        h_s[rows, :] = fixed
        carry = fixed[0:1, :] if reverse else fixed[seg - 1:seg, :]


def _rnn_specs(s, rb, nb):
    act = lambda off: pl.BlockSpec((1, s, rb), lambda b, n, off=off: (b, 0, off + n))
    par = pl.BlockSpec((1, rb), lambda b, n: (0, n))
    wsp = pl.BlockSpec((1, rb, rb), lambda b, n: (n, 0, 0))
    return act, par, wsp


def _rnn_fwd(zr3, cw, cb, wa, ba, wi, bi, lam):
    bl, s, two = zr3.shape
    nb, rb, _ = wa.shape
    dr = nb * rb
    kw = cw.shape[0]
    act, par, wsp = _rnn_specs(s, rb, nb)

    def kern(xr_ref, gr_ref, cw_ref, cb_ref, wa_ref, ba_ref, wi_ref, bi_ref, lam_ref,
             y_ref, h_ref, xc_ref, r_ref, i_ref, a_ref, mult_ref, a_s, u_s, h_s, p_s):
        xc, r, i, sp, a, mult = _rnn_gates(xr_ref[0], cw_ref[...], cb_ref[...], wa_ref[0], ba_ref[...],
                                           wi_ref[0], bi_ref[...], lam_ref[...])
        for ref, val in ((xc_ref, xc), (r_ref, r), (i_ref, i), (a_ref, a), (mult_ref, mult)):
            ref[0] = val
        a_s[...] = a
        u_s[...] = mult * (i * xc)
        _segment_scan(a_s, u_s, h_s, p_s, reverse=False)
        h = h_s[...]
        h_ref[0] = h
        y_ref[0] = (_gelu(gr_ref[0]) * h).astype(BF16)

    kept = jax.ShapeDtypeStruct((bl, s, dr), F32)
    return pl.pallas_call(
        kern, name="rnn_fwd", grid=(bl, nb),
        in_specs=[act(0), act(nb), pl.BlockSpec((kw, rb), lambda b, n: (0, n)), par, wsp, par, wsp, par, par],
        out_specs=[act(0)] * 7,
        out_shape=[jax.ShapeDtypeStruct((bl, s, dr), BF16)] + [kept] * 6,
        scratch_shapes=[pltpu.VMEM((s, rb), F32)] * 4,
        compiler_params=_cp(2),
    )(zr3, zr3, cw, cb, wa, ba, wi, bi, lam)


def _rnn_bwd(zr3, kept, dy3, cw, wa, wi, lam):
    bl, s, _ = zr3.shape
    nb, rb, _ = wa.shape
    dr = nb * rb
    kw = cw.shape[0]
    act = lambda off: pl.BlockSpec((1, s, rb), lambda n, b, off=off: (b, 0, off + n))
    par = pl.BlockSpec((1, rb), lambda n, b: (0, n))
    wsp = pl.BlockSpec((1, rb, rb), lambda n, b: (n, 0, 0))
    cws = pl.BlockSpec((kw, rb), lambda n, b: (0, n))

    def kern(xr_ref, gr_ref, h_ref, xc_ref, r_ref, i_ref, a_ref, mult_ref, dy_ref, cw_ref, wa_ref, wi_ref, lam_ref,
             dxr_ref, dgr_ref, dcw_ref, dcb_ref, dwa_ref, dba_ref, dwi_ref, dbi_ref, dlam_ref, a_s, u_s, h_s, p_s):
        b = pl.program_id(1)
        xr, cwv, lamv = xr_ref[0], cw_ref[...], lam_ref[...]
        wav, wiv = wa_ref[0], wi_ref[0]
        xc, r, i, a, mult = xc_ref[0], r_ref[0], i_ref[0], a_ref[0], mult_ref[0]
        sp = _softplus(-lamv)
        h, dy = h_ref[0], dy_ref[0]
        ge, dge = _gelu_and_grad(gr_ref[0])
        dgr_ref[0] = (dy * h * dge).astype(BF16)
        a_s[...] = _shift_up(a, 1)
        u_s[...] = dy * ge
        _segment_scan(a_s, u_s, h_s, p_s, reverse=True)
        g = h_s[...]
        da = g * _shift_down(h, 1)
        ix = i * xc
        dlog_a = da * a + (g * ix) * (-(a * a) / mult)
        di = g * mult * xc
        dpa = (dlog_a * (-LRU_C * sp)) * r * (1.0 - r)
        dpi = di * i * (1.0 - i)
        dxc = g * mult * i + _dot(dpa, wav, _NT) + _dot(dpi, wiv, _NT)
        dxr = jnp.zeros_like(dxc)
        dcw_rows = []
        for k in range(kw):
            dxr = dxr + _shift_up(dxc, kw - 1 - k) * cwv[k:k + 1, :]
            dcw_rows.append(jnp.sum(dxc * _shift_down(xr, kw - 1 - k), axis=0, keepdims=True))
        dxr_ref[0] = dxr.astype(BF16)

        @pl.when(b == 0)
        def _():
            for ref in (dcw_ref, dcb_ref, dwa_ref, dba_ref, dwi_ref, dbi_ref, dlam_ref):
                ref[...] = jnp.zeros_like(ref)

        for k in range(kw):
            dcw_ref[k:k + 1, :] += dcw_rows[k]
        dcb_ref[...] += jnp.sum(dxc, axis=0, keepdims=True)
        dwa_ref[0] += _dot(xc, dpa, _TN)
        dwi_ref[0] += _dot(xc, dpi, _TN)
        dba_ref[...] += jnp.sum(dpa, axis=0, keepdims=True)
        dbi_ref[...] += jnp.sum(dpi, axis=0, keepdims=True)
        dsp = jnp.sum(dlog_a * (-LRU_C * r), axis=0, keepdims=True)
        dlam_ref[...] += dsp * (-_sigmoid(-lamv))

    vec = jax.ShapeDtypeStruct((1, dr), F32)
    wsh = jax.ShapeDtypeStruct((nb, rb, rb), F32)
    return pl.pallas_call(
        kern, name="rnn_bwd", grid=(nb, bl),
        in_specs=[act(0), act(nb)] + [act(0)] * 7 + [cws, wsp, wsp, par],
        out_specs=[act(0), act(0), cws, par, wsp, par, wsp, par, par],
        out_shape=[jax.ShapeDtypeStruct((bl, s, dr), BF16), jax.ShapeDtypeStruct((bl, s, dr), BF16),
                   jax.ShapeDtypeStruct((kw, dr), F32), vec, wsh, vec, wsh, vec, vec],
        scratch_shapes=[pltpu.VMEM((s, rb), F32)] * 4,
        compiler_params=_cp(2),
    )(zr3, zr3, *kept, dy3, cw, wa, wi, lam)


def _tri(n, upper):
    r = lax.broadcasted_iota(jnp.int32, (n, n), 0)
    c = lax.broadcasted_iota(jnp.int32, (n, n), 1)
    return jnp.where((c >= r) if upper else (c <= r), 1.0, 0.0).astype(F32)


def _fgate_fwd(zf3, bf):
    bl, s, w = zf3.shape
    ch = _tile(s, 256, SUBLANES)

    def kern(z_ref, b_ref, f_ref):
        tri = _tri(ch, upper=False)
        carry = jnp.zeros((1, w), F32)
        for j in range(s // ch):
            rows = pl.ds(j * ch, ch)
            lf = -_softplus(-(z_ref[0, rows, :] + b_ref[...]))
            out = jnp.dot(tri, lf, precision=lax.Precision.HIGHEST, preferred_element_type=F32) + carry
            f_ref[0, rows, :] = out
            carry = out[ch - 1:ch, :]

    return pl.pallas_call(
        kern, name="fgate_fwd", grid=(bl,),
        in_specs=[pl.BlockSpec((1, s, w), lambda b: (b, 0, 0)), pl.BlockSpec((1, w), lambda b: (0, 0))],
        out_specs=pl.BlockSpec((1, s, w), lambda b: (b, 0, 0)),
        out_shape=jax.ShapeDtypeStruct((bl, s, w), F32),
        compiler_params=_cp(1),
    )(zf3, bf)


def _fgate_bwd(dfk3, dfq3, zf3, bf):
    bl, s, w = zf3.shape
    ch = _tile(s, 256, SUBLANES)

    def kern(dfk_ref, dfq_ref, z_ref, b_ref, dz_ref, db_ref):
        b = pl.program_id(0)
        tri = _tri(ch, upper=True)
        carry = jnp.zeros((1, w), F32)
        dbsum = jnp.zeros((1, w), F32)
        for j in range(s // ch - 1, -1, -1):
            rows = pl.ds(j * ch, ch)
            df = dfk_ref[0, rows, :] + dfq_ref[0, rows, :]
            dlf = jnp.dot(tri, df, precision=lax.Precision.HIGHEST, preferred_element_type=F32) + carry
            carry = dlf[0:1, :]
            dz = dlf * _sigmoid(-(z_ref[0, rows, :] + b_ref[...]))
            dz_ref[0, rows, :] = dz.astype(BF16)
            dbsum = dbsum + jnp.sum(dz, axis=0, keepdims=True)

        @pl.when(b == 0)
        def _():
            db_ref[...] = jnp.zeros_like(db_ref)

        db_ref[...] += dbsum

    return pl.pallas_call(
        kern, name="fgate_bwd", grid=(bl,),
        in_specs=[pl.BlockSpec((1, s, w), lambda b: (b, 0, 0))] * 3 + [pl.BlockSpec((1, w), lambda b: (0, 0))],
        out_specs=[pl.BlockSpec((1, s, w), lambda b: (b, 0, 0)), pl.BlockSpec((1, w), lambda b: (0, 0))],
        out_shape=[jax.ShapeDtypeStruct((bl, s, w), BF16), jax.ShapeDtypeStruct((1, w), F32)],
        compiler_params=_cp(1),
    )(dfk3, dfq3, zf3, bf)


def _lanes(col, width):
    return col if width == LANES else jnp.concatenate([col] * (width // LANES), axis=1)


def _causal(sc, row0, col0, transposed):
    r = lax.broadcasted_iota(jnp.int32, sc.shape, 0) + row0
    c = lax.broadcasted_iota(jnp.int32, sc.shape, 1) + col0
    return jnp.where((c >= r) if transposed else (r >= c), sc, NEG_BIG)


def _attn_fwd(q3, kv3, fcol, frow, nh):
    bl, s, da = q3.shape
    dh = da // nh
    tq = _tile(s, 512, LANES)
    nq = s // tq

    def kern(iq_tab, ik_tab, q_ref, k_ref, v_ref, fk_ref, fq_ref, o_ref, lse_ref, m_s, l_s, acc):
        iq, ik = iq_tab[pl.program_id(2)], ik_tab[pl.program_id(2)]

        @pl.when(ik == 0)
        def _():
            m_s[...] = jnp.full_like(m_s, NEG_BIG)
            l_s[...] = jnp.zeros_like(l_s)
            acc[...] = jnp.zeros_like(acc)

        def block(masked):
            st = _dot(k_ref[0], q_ref[0], _NT) - _lanes(fk_ref[0], tq) + fq_ref[0]
            if masked:
                st = _causal(st, ik * tq, iq * tq, True)
            m_old = m_s[...]
            m_new = jnp.maximum(m_old, jnp.max(st, axis=0, keepdims=True))
            alpha = jnp.exp(m_old - m_new)
            pt = jnp.exp(st - m_new)
            l_s[...] = alpha * l_s[...] + jnp.sum(pt, axis=0, keepdims=True)
            acc[...] = alpha * acc[...] + _dot(v_ref[0], pt, _TN)
            m_s[...] = m_new

        pl.when(ik < iq)(functools.partial(block, False))

        @pl.when(ik == iq)
        def _():
            block(True)
            l = l_s[...]
            o_ref[0] = (acc[...] / l).T
            lse_ref[0] = m_s[...] + jnp.log(l)

    pairs = [(i, j) for i in range(nq) for j in range(i + 1)]
    iq_tab, ik_tab = (jnp.asarray(col, jnp.int32) for col in zip(*pairs))
    qmap = lambda b, h, p, iqt, ikt: (b, iqt[p], h)
    kmap = lambda off: (lambda b, h, p, iqt, ikt: (b, ikt[p], off + h))
    return pl.pallas_call(
        kern, name="attn_fwd",
        grid_spec=pltpu.PrefetchScalarGridSpec(
            num_scalar_prefetch=2, grid=(bl, nh, len(pairs)),
            in_specs=[pl.BlockSpec((1, tq, dh), qmap), pl.BlockSpec((1, tq, dh), kmap(0)), pl.BlockSpec((1, tq, dh), kmap(nh)),
                      pl.BlockSpec((1, tq, LANES), lambda b, h, p, iqt, ikt: (b * nh + h, ikt[p], 0)),
                      pl.BlockSpec((1, 1, tq), lambda b, h, p, iqt, ikt: (b * nh + h, 0, iqt[p]))],
            out_specs=[pl.BlockSpec((1, tq, dh), qmap),
                       pl.BlockSpec((1, 1, tq), lambda b, h, p, iqt, ikt: (b * nh + h, 0, iqt[p]))],
            scratch_shapes=[pltpu.VMEM((1, tq), F32), pltpu.VMEM((1, tq), F32), pltpu.VMEM((dh, tq), F32)]),
        out_shape=[jax.ShapeDtypeStruct((bl, s, da), F32), jax.ShapeDtypeStruct((bl * nh, 1, s), F32)],
        compiler_params=_cp(3),
    )(iq_tab, ik_tab, q3, kv3, kv3, fcol, frow)


def _attn_bwd(q3, kv3, do3, o3, lse_row, fcol, frow, nh, scale):
    bl, s, da = q3.shape
    dh = da // nh
    tk = _tile(s, 512, LANES)
    nk = s // tk

    pairs = [(j, i) for j in range(nk) for i in range(j, nk)]

    def kern(ik_tab, iq_tab, q_ref, k_ref, v_ref, do_ref, o_ref, lse_ref, fk_ref, fq_ref, dq_ref, dk_ref, dv_ref, dfk_ref,
             dfq_ref, dq_acc, dk_acc, dv_acc, dfq_acc, delta_s):
        step = pl.program_id(2)
        ik, iq = ik_tab[step], iq_tab[step]
        qrow = pl.ds(iq, 1)

        @pl.when(step == 0)
        def _():
            dq_acc[...] = jnp.zeros_like(dq_acc)
            dfq_acc[...] = jnp.zeros_like(dfq_acc)

        @pl.when(ik == 0)
        def _():
            prod = do_ref[0] * o_ref[0]
            rows = lax.dot_general(jnp.ones((SUBLANES, dh), F32), prod, (_NT, ((), ())),
                                   precision=lax.Precision.HIGHEST, preferred_element_type=F32)
            delta_s[qrow, :] = rows[0:1, :]

        @pl.when(iq == ik)
        def _():
            dk_acc[...] = jnp.zeros_like(dk_acc)
            dv_acc[...] = jnp.zeros_like(dv_acc)

        def block(masked):
            q = q_ref[0]
            st = _dot(k_ref[0], q, _NT) - _lanes(fk_ref[0], tk) + fq_ref[0]
            if masked:
                st = _causal(st, ik * tk, iq * tk, True)
            pt = jnp.exp(st - lse_ref[0])
            dv_acc[...] += _dot(pt, do_ref[0], _NN)
            dpt = _dot(v_ref[0], do_ref[0], _NT)
            dst = (pt * (dpt - delta_s[qrow, :])).astype(BF16)
            q_ones = jnp.concatenate([q, jnp.ones_like(q)], axis=1)
            dk_acc[...] += _dot(dst, q_ones, _NN)
            qrows = pl.ds(pl.multiple_of(iq * tk, tk), tk)
            dq_acc[qrows, :] += _dot(dst, k_ref[0], _TN)
            dfq_acc[qrow, :] += jnp.sum(dst.astype(F32), axis=0, keepdims=True)

        pl.when(iq > ik)(functools.partial(block, False))
        pl.when(iq == ik)(functools.partial(block, True))

        @pl.when(iq == nk - 1)
        def _():
            ext = dk_acc[...]
            dk_ref[0] = ext[:, :dh].astype(BF16)
            dfk_ref[0] = -ext[:, dh:]
            dv_ref[0] = dv_acc[...].astype(BF16)

        @pl.when(step == len(pairs) - 1)
        def _():
            dq_ref[0] = (dq_acc[...] * scale).astype(BF16)
            dfq_ref[0] = dfq_acc[...]

    ik_tab, iq_tab = (jnp.asarray(col, jnp.int32) for col in zip(*pairs))
    qmap = lambda b, h, p, ikt, iqt: (b, iqt[p], h)
    omap = lambda b, h, p, ikt, iqt: (b, jnp.where(ikt[p] == 0, iqt[p], 0), h)
    rmap = lambda b, h, p, ikt, iqt: (b * nh + h, 0, iqt[p])
    kmap = lambda off: (lambda b, h, p, ikt, iqt: (b, ikt[p], off + h))
    bmap = lambda b, h, p, ikt, iqt: (b * nh + h, ikt[p], 0)
    return pl.pallas_call(
        kern, name="attn_bwd",
        grid_spec=pltpu.PrefetchScalarGridSpec(
            num_scalar_prefetch=2, grid=(bl, nh, len(pairs)),
            in_specs=[pl.BlockSpec((1, tk, dh), qmap), pl.BlockSpec((1, tk, dh), kmap(0)), pl.BlockSpec((1, tk, dh), kmap(nh)),
                      pl.BlockSpec((1, tk, dh), qmap), pl.BlockSpec((1, tk, dh), omap), pl.BlockSpec((1, 1, tk), rmap),
                      pl.BlockSpec((1, tk, LANES), bmap), pl.BlockSpec((1, 1, tk), rmap)],
            out_specs=[pl.BlockSpec((1, s, dh), lambda b, h, p, ikt, iqt: (b, 0, h)),
                       pl.BlockSpec((1, tk, dh), kmap(0)), pl.BlockSpec((1, tk, dh), kmap(0)),
                       pl.BlockSpec((1, tk, LANES), bmap),
                       pl.BlockSpec((1, nk, tk), lambda b, h, p, ikt, iqt: (b * nh + h, 0, 0))],
            scratch_shapes=[pltpu.VMEM((s, dh), F32), pltpu.VMEM((tk, 2 * dh), F32), pltpu.VMEM((tk, dh), F32),
                            pltpu.VMEM((nk, tk), F32), pltpu.VMEM((nk, tk), F32)]),
        out_shape=[jax.ShapeDtypeStruct((bl, s, da), BF16), jax.ShapeDtypeStruct((bl, s, da), BF16),
                   jax.ShapeDtypeStruct((bl, s, da), BF16), jax.ShapeDtypeStruct((bl * nh, s, LANES), F32),
                   jax.ShapeDtypeStruct((bl * nh, nk, tk), F32)],
        compiler_params=_cp(3),
    )(ik_tab, iq_tab, q3, kv3, kv3, do3, o3, lse_row, fcol, frow)


def _merge_fwd(mg3, pr3, pa3):
    bl, s, d = pr3.shape
    ts = _tile(s, 256, SUBLANES)
    half = lambda j: pl.BlockSpec((1, ts, d), lambda b, t, j=j: (b, t, j))

    def kern(mr_ref, ma_ref, pr_ref, pa_ref, o_ref):
        o_ref[0] = (_sigmoid(mr_ref[0]) * pr_ref[0] + _sigmoid(ma_ref[0]) * pa_ref[0]).astype(BF16)

    return pl.pallas_call(
        kern, name="merge_fwd", grid=(bl, s // ts),
        in_specs=[half(0), half(1)] + _act_specs(ts, d, 2), out_specs=_act_specs(ts, d, 1)[0],
        out_shape=jax.ShapeDtypeStruct((bl, s, d), BF16), compiler_params=_cp(2),
    )(mg3, mg3, pr3, pa3)


def _merge_bwd(dm3, mg3, pr3, pa3):
    bl, s, d = pr3.shape
    ts = _tile(s, 256, SUBLANES)
    half = lambda j: pl.BlockSpec((1, ts, d), lambda b, t, j=j: (b, t, j))

    def kern(dm_ref, mr_ref, ma_ref, pr_ref, pa_ref, dpr_ref, dpa_ref, dmr_ref, dma_ref):
        dm = dm_ref[0]
        gr, ga = _sigmoid(mr_ref[0]), _sigmoid(ma_ref[0])
        dpr_ref[0] = (gr * dm).astype(BF16)
        dpa_ref[0] = (ga * dm).astype(BF16)
        dmr_ref[0] = (dm * pr_ref[0] * gr * (1.0 - gr)).astype(BF16)
        dma_ref[0] = (dm * pa_ref[0] * ga * (1.0 - ga)).astype(BF16)

    return pl.pallas_call(
        kern, name="merge_bwd", grid=(bl, s // ts),
        in_specs=_act_specs(ts, d, 1) + [half(0), half(1)] + _act_specs(ts, d, 2), out_specs=_act_specs(ts, d, 4),
        out_shape=[jax.ShapeDtypeStruct((bl, s, d), BF16)] * 4, compiler_params=_cp(2),
    )(dm3, mg3, mg3, pr3, pa3)


def _ffn_conv(gf, cw, cb):
    kw = cw.shape[0]
    y = cb
    for k in range(kw):
        y = y + _shift_down(gf, kw - 1 - k) * cw[k:k + 1, :]
    return y


def _ffn_act_fwd(up3, cw, cb):
    bl, s, two = up3.shape
    dff = two // 2
    kw = cw.shape[0]
    tc = _tile(dff, 256, LANES)
    nc = dff // tc

    def kern(gf_ref, uf_ref, cw_ref, cb_ref, o_ref):
        o_ref[0] = (_gelu(_ffn_conv(gf_ref[0], cw_ref[...], cb_ref[...])) * uf_ref[0]).astype(BF16)

    act = lambda off: pl.BlockSpec((1, s, tc), lambda b, j, off=off: (b, 0, off + j))
    return pl.pallas_call(
        kern, name="ffn_act_fwd", grid=(bl, nc),
        in_specs=[act(0), act(nc), pl.BlockSpec((kw, tc), lambda b, j: (0, j)), pl.BlockSpec((1, tc), lambda b, j: (0, j))],
        out_specs=act(0), out_shape=jax.ShapeDtypeStruct((bl, s, dff), BF16), compiler_params=_cp(2),
    )(up3, up3, cw, cb)


def _ffn_act_bwd(up3, dact3, cw, cb):
    bl, s, two = up3.shape
    dff = two // 2
    kw = cw.shape[0]
    tc = _tile(dff, 256, LANES)
    nc = dff // tc

    def kern(gf_ref, uf_ref, da_ref, cw_ref, cb_ref, dgf_ref, duf_ref, dcw_ref, dcb_ref):
        b = pl.program_id(1)
        gf, cwv, da = gf_ref[0], cw_ref[...], da_ref[0]
        ge, dge = _gelu_and_grad(_ffn_conv(gf, cwv, cb_ref[...]))
        duf_ref[0] = (da * ge).astype(BF16)
        dgc = da * uf_ref[0] * dge
        dgf = jnp.zeros_like(dgc)
        rows = []
        for k in range(kw):
            dgf = dgf + _shift_up(dgc, kw - 1 - k) * cwv[k:k + 1, :]
            rows.append(jnp.sum(dgc * _shift_down(gf, kw - 1 - k), axis=0, keepdims=True))
        dgf_ref[0] = dgf.astype(BF16)

        @pl.when(b == 0)
        def _():
            dcw_ref[...] = jnp.zeros_like(dcw_ref)
            dcb_ref[...] = jnp.zeros_like(dcb_ref)

        for k in range(kw):
            dcw_ref[k:k + 1, :] += rows[k]
        dcb_ref[...] += jnp.sum(dgc, axis=0, keepdims=True)

    act = lambda off: pl.BlockSpec((1, s, tc), lambda j, b, off=off: (b, 0, off + j))
    cws = pl.BlockSpec((kw, tc), lambda j, b: (0, j))
    cbs = pl.BlockSpec((1, tc), lambda j, b: (0, j))
    return pl.pallas_call(
        kern, name="ffn_act_bwd", grid=(nc, bl),
        in_specs=[act(0), act(nc), act(0), cws, cbs], out_specs=[act(0), act(0), cws, cbs],
        out_shape=[jax.ShapeDtypeStruct((bl, s, dff), BF16), jax.ShapeDtypeStruct((bl, s, dff), BF16),
                   jax.ShapeDtypeStruct((kw, dff), F32), jax.ShapeDtypeStruct((1, dff), F32)],
        compiler_params=_cp(2),
    )(up3, up3, dact3, cw, cb)


_HBM = pl.BlockSpec(memory_space=pltpu.HBM)


def _place():
    x, y, c = lax.axis_index("x"), lax.axis_index("y"), lax.axis_index("c")
    chips = dict(me=2 * x + y, nx=2 * (1 - x) + y, ny=2 * x + (1 - y), diag=2 * (1 - x) + (1 - y))
    peers = dict(nx=(1 - x, y, c), ny=(x, 1 - y, c), sib=(x, y, 1 - c))
    return c, chips, peers


def _remote(src, dst, sems, k, to):
    return pltpu.make_async_remote_copy(src_ref=src, dst_ref=dst, send_sem=sems[0].at[k], recv_sem=sems[1].at[k],
                                        device_id=to, device_id_type=MESH)


RS_STEPS = 2


def _piece(q, idx, n=1):
    start = idx * q
    if not isinstance(start, int):
        start = pl.multiple_of(start, SUBLANES)
    return pl.ds(start, n * q)


def _all_gather_chips(xs, name):
    nt = len(xs)
    per = 9

    def body(*refs):
        x_refs, o_refs = refs[:nt], refs[nt:2 * nt]
        send_sems, recv_sems = refs[2 * nt:]
        c, chip, peer = _place()
        sems = (send_sems, recv_sems)
        me, nx, ny, dg = chip["me"], chip["nx"], chip["ny"], chip["diag"]
        sends = []

        def arrive(k, dst):
            _remote(dst, dst, sems, k, peer["sib"]).wait_recv()

        def pass_on(k, blk, to):
            cp = _remote(blk, blk, sems, k, peer[to])
            cp.start()
            sends.append(cp)

        for t in range(nt):
            q = xs[t].shape[0] // 4
            half = _piece(q, 2 * c, 2)
            for k, to in ((0, "nx"), (1, "ny")):
                cp = _remote(x_refs[t].at[half], o_refs[t].at[me, half], sems, per * t + k, peer[to])
                cp.start()
                sends.append(cp)
            cp = _remote(x_refs[t], o_refs[t].at[me], sems, per * t + 8, peer["sib"])
            cp.start()
            sends.append(cp)
        for t in range(nt):
            q, o, k0 = xs[t].shape[0] // 4, o_refs[t], per * t
            half, sub0, sub1 = _piece(q, 2 * c, 2), _piece(q, 2 * c), _piece(q, 2 * c + 1)
            arrive(k0 + 0, o.at[nx, half])
            pass_on(k0 + 2, o.at[nx, sub0], "ny")
            pass_on(k0 + 4, o.at[nx, half], "sib")
            arrive(k0 + 1, o.at[ny, half])
            pass_on(k0 + 3, o.at[ny, sub1], "nx")
            pass_on(k0 + 5, o.at[ny, half], "sib")
            arrive(k0 + 2, o.at[dg, sub0])
            pass_on(k0 + 6, o.at[dg, sub0], "sib")
            arrive(k0 + 3, o.at[dg, sub1])
            pass_on(k0 + 7, o.at[dg, sub1], "sib")
        for t in range(nt):
            q, o, k0 = xs[t].shape[0] // 4, o_refs[t], per * t
            arrive(k0 + 4, o.at[nx, _piece(q, 2 * (1 - c), 2)])
            arrive(k0 + 5, o.at[ny, _piece(q, 2 * (1 - c), 2)])
            arrive(k0 + 6, o.at[dg, _piece(q, 2 * (1 - c))])
            arrive(k0 + 7, o.at[dg, _piece(q, 2 * (1 - c) + 1)])
            arrive(k0 + 8, o.at[me])
        for cp in sends:
            cp.wait_send()

    return pl.pallas_call(
        body, name=name, in_specs=[_HBM] * nt, out_specs=[_HBM] * nt,
        out_shape=[jax.ShapeDtypeStruct((N_CHIPS,) + x.shape, x.dtype) for x in xs],
        scratch_shapes=[pltpu.SemaphoreType.DMA((per * nt,)), pltpu.SemaphoreType.DMA((per * nt,))],
    )(*xs)


def _all_gather_devices(xs, name):
    nt = len(xs)
    per = 7

    def body(*refs):
        x_refs, o_refs = refs[:nt], refs[nt:2 * nt]
        send_sems, recv_sems, local_sems = refs[2 * nt:]
        x, y, c = lax.axis_index("x"), lax.axis_index("y"), lax.axis_index("c")
        sems = (send_sems, recv_sems)
        sib = (x, y, 1 - c)
        chips = [(1 - x, y), (x, 1 - y), (1 - x, 1 - y)]
        slot = lambda px, py, pc: 4 * px + 2 * py + pc
        me = slot(x, y, c)
        sends, copies = [], []

        def arrive(k, dst):
            _remote(dst, dst, sems, k, sib).wait_recv()

        for t in range(nt):
            cp = pltpu.make_async_copy(x_refs[t], o_refs[t].at[me], local_sems.at[t])
            cp.start()
            copies.append(cp)
            for k, to in enumerate([sib] + [(*chip, c) for chip in chips]):
                cp = _remote(x_refs[t], o_refs[t].at[me], sems, per * t + k, to)
                cp.start()
                sends.append(cp)
        for t in range(nt):
            for j, chip in enumerate(chips):
                blk = o_refs[t].at[slot(*chip, c)]
                arrive(per * t + 1 + j, blk)
                cp = _remote(blk, blk, sems, per * t + 4 + j, sib)
                cp.start()
                sends.append(cp)
        for t in range(nt):
            arrive(per * t, o_refs[t].at[slot(x, y, 1 - c)])
            for j, chip in enumerate(chips):
                arrive(per * t + 4 + j, o_refs[t].at[slot(*chip, 1 - c)])
        for cp in sends:
            cp.wait_send()
        for cp in copies:
            cp.wait()

    return pl.pallas_call(
        body, name=name, in_specs=[_HBM] * nt, out_specs=[_HBM] * nt,
        out_shape=[jax.ShapeDtypeStruct((2 * N_CHIPS,) + a.shape, a.dtype) for a in xs],
        scratch_shapes=[pltpu.SemaphoreType.DMA((per * nt,)), pltpu.SemaphoreType.DMA((per * nt,)),
                        pltpu.SemaphoreType.DMA((nt,))],
    )(*xs)


def _exchange(name, xs, out_shapes, plan):
    nt = len(xs)

    def body(*refs):
        x_refs, o_refs = refs[:nt], refs[nt:2 * nt]
        send_sems, recv_sems = refs[2 * nt:]
        c, chip, peer = _place()
        cps = []
        for t in range(nt):
            for src, dst, to in plan(c, chip, x_refs[t], o_refs[t], xs[t].shape):
                cps.append(_remote(src, dst, (send_sems, recv_sems), len(cps), peer[to]))
        for cp in cps:
            cp.start()
        for cp in cps:
            cp.wait()

    n_copies = nt * len(plan(0, dict(me=0, nx=2, ny=1, diag=3), None, None, xs[0].shape, count_only=True))
    return pl.pallas_call(
        body, name=name, in_specs=[_HBM] * nt, out_specs=[_HBM] * nt,
        out_shape=[jax.ShapeDtypeStruct(s, x.dtype) for s, x in zip(out_shapes, xs)],
        scratch_shapes=[pltpu.SemaphoreType.DMA((n_copies,)), pltpu.SemaphoreType.DMA((n_copies,))],
    )(*xs)


def _plan_sibling(c, chip, g, out, shape, count_only=False):
    if count_only:
        return [None] * N_CHIPS
    q = shape[1] // 4
    return [(g.at[j, _piece(q, 2 * (1 - c), 2)], out.at[j], "sib") for j in range(N_CHIPS)]


def _plan_first(c, chip, p, out, shape, count_only=False):
    if count_only:
        return [None] * 4
    q = shape[1] // 2
    return [(p.at[chip["nx"], _piece(q, 0)], out.at[0], "nx"), (p.at[chip["diag"], _piece(q, 0)], out.at[1], "nx"),
            (p.at[chip["ny"], _piece(q, 1)], out.at[2], "ny"), (p.at[chip["diag"], _piece(q, 1)], out.at[3], "ny")]


def _plan_second(c, chip, p, out, shape, count_only=False):
    if count_only:
        return [None] * 2
    return [(p.at[1], out.at[0], "ny"), (p.at[3], out.at[1], "nx")]


def _rs_last(ps):
    nt = len(ps)

    def body(*refs):
        p_refs, o_refs = refs[:nt], refs[nt:2 * nt]
        send_sems, recv_sems = refs[2 * nt:]
        c, _, peer = _place()
        sems = (send_sems, recv_sems)
        cps = []
        for t in range(nt):
            q = ps[t].shape[0] // 4
            mine = _piece(q, 2 * c, 2)
            cps.append(_remote(p_refs[t].at[mine], o_refs[t].at[mine], sems, t, peer["sib"]))
            cps[-1].start()
        for t in range(nt):
            q = ps[t].shape[0] // 4
            theirs = _piece(q, 2 * (1 - c), 2)
            cps[t].wait_send()
            _remote(p_refs[t].at[theirs], o_refs[t].at[theirs], sems, t, peer["sib"]).wait_recv()

    return pl.pallas_call(
        body, name="rs_last", in_specs=[_HBM] * nt, out_specs=[_HBM] * nt,
        out_shape=[jax.ShapeDtypeStruct(p.shape, F32) for p in ps],
        input_output_aliases={t: t for t in range(nt)},
        scratch_shapes=[pltpu.SemaphoreType.DMA((nt,)), pltpu.SemaphoreType.DMA((nt,))],
    )(*ps)


def _add_stage(name, grid, a_list, b_list, a_map, b_map, tbs, out_shapes, out_map, out_dtype, prefetch=None):
    nt = len(a_list)
    lead = lambda shape: (None,) * (len(shape) - 2)

    def kern(*refs):
        refs = refs[(1 if prefetch is not None else 0):]
        for t in range(nt):
            refs[2 * nt + t][...] = (refs[t][...].astype(F32) + refs[nt + t][...].astype(F32)).astype(out_dtype)

    in_specs = [pl.BlockSpec(lead(a.shape) + (tb, a.shape[-1]), a_map) for a, tb in zip(a_list, tbs)]
    in_specs += [pl.BlockSpec(lead(b.shape) + (tb, b.shape[-1]), b_map) for b, tb in zip(b_list, tbs)]
    out_specs = [pl.BlockSpec(lead(s) + (tb, s[-1]), out_map) for s, tb in zip(out_shapes, tbs)]
    out_shape = [jax.ShapeDtypeStruct(s, out_dtype) for s in out_shapes]
    if prefetch is None:
        return pl.pallas_call(kern, name=name, grid=grid, in_specs=in_specs, out_specs=out_specs, out_shape=out_shape,
                              compiler_params=_cp(len(grid)))(*a_list, *b_list)
    return pl.pallas_call(
        kern, name=name,
        grid_spec=pltpu.PrefetchScalarGridSpec(num_scalar_prefetch=1, grid=grid, in_specs=in_specs, out_specs=out_specs),
        out_shape=out_shape, compiler_params=_cp(len(grid)))(prefetch, *a_list, *b_list)


def _reduce_scatter_chips(gs):
    x, y, c = lax.axis_index("x"), lax.axis_index("y"), lax.axis_index("c")
    me, nx, ny = 2 * x + y, 2 * (1 - x) + y, 2 * x + (1 - y)
    st = RS_STEPS
    unit = 4 * st * 2 * SUBLANES
    rows = [g.shape[1] for g in gs]
    gs = [jnp.pad(g, ((0, 0), (0, -g.shape[1] % unit), (0, 0))) for g in gs]
    qs = [g.shape[1] // 4 for g in gs]
    tbs = [q // st for q in qs]
    cols = [g.shape[2] for g in gs]
    core = jnp.reshape(c, (1,)).astype(jnp.int32)

    got = _exchange("rs_sibling", gs, [(N_CHIPS, 2 * q, cc) for q, cc in zip(qs, cols)], _plan_sibling)
    p0 = _add_stage("rs_add_sibling", (N_CHIPS, 2, st), gs, got,
                    lambda j, h, s, c_ref: (j, (2 * c_ref[0] + h) * st + s, 0), lambda j, h, s, c_ref: (j, h * st + s, 0),
                    tbs, [(N_CHIPS, 2 * q, cc) for q, cc in zip(qs, cols)], lambda j, h, s, c_ref: (j, h * st + s, 0),
                    BF16, prefetch=core)
    got = _exchange("rs_first", p0, [(4, q, cc) for q, cc in zip(qs, cols)], _plan_first)
    p1 = _add_stage("rs_add_first", (4, st), p0, got,
                    lambda k, s, i_ref: (i_ref[k], (k // 2) * st + s, 0), lambda k, s, i_ref: (k, s, 0),
                    tbs, [(4, q, cc) for q, cc in zip(qs, cols)], lambda k, s, i_ref: (k, s, 0),
                    BF16, prefetch=jnp.stack([me, ny, me, nx]).astype(jnp.int32))
    got = _exchange("rs_second", p1, [(2, q, cc) for q, cc in zip(qs, cols)], _plan_second)
    p2 = _add_stage("rs_add_second", (2, st), p1, got, lambda h, s, c_ref: (2 * h, s, 0), lambda h, s, c_ref: (h, s, 0),
                    tbs, [(4 * q, cc) for q, cc in zip(qs, cols)], lambda h, s, c_ref: ((2 * c_ref[0] + h) * st + s, 0),
                    F32, prefetch=core)
    return [out[:r] for out, r in zip(_rs_last(p2), rows)]


def _adamw(g, w, m, v, name):
    rows, cc = g.shape
    tr = _tile(rows, max(SUBLANES, (1 << 18) // cc), SUBLANES)
    k1 = 1.0 - ADAM_B1 ** ADAM_STEP
    k2 = 1.0 - ADAM_B2 ** ADAM_STEP

    def kern(g_ref, w_ref, m_ref, v_ref, d_ref, nm_ref, nv_ref):
        gv = g_ref[...]
        nm = ADAM_B1 * m_ref[...] + (1.0 - ADAM_B1) * gv
        nv = ADAM_B2 * v_ref[...] + (1.0 - ADAM_B2) * (gv * gv)
        nm_ref[...] = nm
        nv_ref[...] = nv
        d_ref[...] = -ADAM_LR * ((nm / k1) / (jnp.sqrt(nv / k2) + ADAM_EPS) + ADAM_WD * w_ref[...])

    spec = pl.BlockSpec((tr, cc), lambda t: (t, 0))
    return pl.pallas_call(
        kern, name=name, grid=(rows // tr,), in_specs=[spec] * 4, out_specs=[spec] * 3,
        out_shape=[jax.ShapeDtypeStruct((rows, cc), F32)] * 3, compiler_params=_cp(1),
    )(g, w, m, v)


def _flat_pad(parts, total):
    flat = jnp.concatenate([p.reshape(-1) for p in parts])
    return jnp.pad(flat, (0, total - flat.shape[0]))


def _split_flat(flat, shapes):
    out, pos = [], 0
    for shp in shapes:
        size = math.prod(shp)
        out.append(flat[pos:pos + size].reshape(shp))
        pos += size
    return out


def _cols_of_chunks(chunks, lo, hi):
    width = chunks.shape[2]
    parts = []
    for j in range(chunks.shape[0]):
        a, b = max(lo, j * width), min(hi, (j + 1) * width)
        if a < b:
            parts.append(chunks[j, :, a - j * width:b - j * width])
    return parts[0] if len(parts) == 1 else jnp.concatenate(parts, axis=1)


def _chunks_of_cols(segments, n_chunks):
    total = sum(s.shape[1] for s in segments)
    width = total // n_chunks
    chunks = []
    for j in range(n_chunks):
        lo, hi, pos, parts = j * width, (j + 1) * width, 0, []
        for s in segments:
            a, b = max(lo, pos), min(hi, pos + s.shape[1])
            if a < b:
                parts.append(s[:, a - pos:b - pos])
            pos += s.shape[1]
        chunks.append(parts[0] if len(parts) == 1 else jnp.concatenate(parts, axis=1))
    return jnp.stack(chunks)


_WEIGHTS = ['w_ada', 'b_ada', 'g_norm1', 'w_in', 'w_rnn_conv', 'b_rnn_conv', 'w_lru_a', 'b_lru_a', 'w_lru_i', 'b_lru_i',
            'lru_lambda', 'b_fgate', 'w_proj_rnn', 'w_proj_attn', 'w_out', 'g_norm2', 'w_ffn_up', 'w_ffn_conv',
            'b_ffn_conv', 'w_ffn_down', 'w_ada_final', 'b_ada_final', 'g_final']
_MATMUL = ['w_in', 'w_proj_rnn', 'w_proj_attn', 'w_out', 'w_ffn_up', 'w_ffn_down']
_ADA = ['w_ada', 'w_ada_final']
_ADA_BIAS = ['b_ada', 'b_ada_final']
_CONV = ['w_rnn_conv', 'w_ffn_conv']
_REPLICATED = [n for n in _WEIGHTS if n not in _MATMUL + _ADA + _ADA_BIAS + _CONV]


def kernel(x, c, w_ada, b_ada, g_norm1, w_in, w_rnn_conv, b_rnn_conv, w_lru_a, b_lru_a, w_lru_i, b_lru_i, lru_lambda, b_fgate, w_proj_rnn, w_proj_attn, w_out, g_norm2, w_ffn_up, w_ffn_conv, b_ffn_conv, w_ffn_down, w_ada_final, b_ada_final, g_final, loss_target, m_w_ada, m_b_ada, m_g_norm1, m_w_in, m_w_rnn_conv, m_b_rnn_conv, m_w_lru_a, m_b_lru_a, m_w_lru_i, m_b_lru_i, m_lru_lambda, m_b_fgate, m_w_proj_rnn, m_w_proj_attn, m_w_out, m_g_norm2, m_w_ffn_up, m_w_ffn_conv, m_b_ffn_conv, m_w_ffn_down, m_w_ada_final, m_b_ada_final, m_g_final, v_w_ada, v_b_ada, v_g_norm1, v_w_in, v_w_rnn_conv, v_b_rnn_conv, v_w_lru_a, v_b_lru_a, v_w_lru_i, v_b_lru_i, v_lru_lambda, v_b_fgate, v_w_proj_rnn, v_w_proj_attn, v_w_out, v_g_norm2, v_w_ffn_up, v_w_ffn_conv, v_b_ffn_conv, v_w_ffn_down, v_w_ada_final, v_b_ada_final, v_g_final):
    args = locals()
    shape_of = {n: args[n].shape for n in _WEIGHTS}

    def view(a):
        if a.ndim >= 3:
            return a[0]
        return a[None, :] if a.ndim == 1 else a

    w2 = {n: view(args[n]) for n in _WEIGHTS}
    m2 = {n: args['m_' + n].reshape(w2[n].shape) for n in _WEIGHTS}
    v2 = {n: args['v_' + n].reshape(w2[n].shape) for n in _WEIGHTS}

    bl, s, d = x.shape
    t = bl * s
    nh = b_fgate.shape[-1]
    nb, rb = w_lru_a.shape[1], w_lru_a.shape[2]
    dr = nb * rb
    da = w2['w_proj_attn'].shape[0] * N_CHIPS
    dh = da // nh
    dff = w2['w_ffn_conv'].shape[1] * N_CHIPS
    scale = dh ** -0.5
    chip = 2 * lax.axis_index("x") + lax.axis_index("y")
    dev = 2 * chip + lax.axis_index("c")

    names = list(_MATMUL)
    gathered = dict(zip(names, _all_gather_chips([w2[n].astype(BF16) for n in names], "ag_weights")))
    n_conv = sum(w2[n].size for n in _CONV)
    rows_conv = -(-n_conv // (FLAT_COLS * 32)) * 32
    conv_local = _flat_pad([w2[n] for n in _CONV], rows_conv * FLAT_COLS).reshape(rows_conv, FLAT_COLS)
    conv_all = _all_gather_chips([conv_local], "ag_conv")[0].reshape(N_CHIPS, -1)
    conv_full, pos = {}, 0
    for n in _CONV:
        r, n4 = w2[n].shape
        blocks = conv_all[:, pos:pos + r * n4].reshape(N_CHIPS, r, n4)
        conv_full[n] = jnp.concatenate([blocks[j] for j in range(N_CHIPS)], axis=1)
        pos += r * n4
    rowmajor = lambda n: gathered[n].reshape(-1, gathered[n].shape[2])
    w_proj_rnn_f, w_proj_attn_f, w_out_f, w_ffn_down_f = (rowmajor(n) for n in ('w_proj_rnn', 'w_proj_attn', 'w_out', 'w_ffn_down'))
    up_chunk = w2['w_ffn_up'].shape[1]

    o_q, o_k, o_fl = 2 * dr, 2 * dr + da, 2 * dr + 3 * da
    o_mg = o_fl + nh
    g_in_w = gathered['w_in']
    w_rnn, w_q = _cols_of_chunks(g_in_w, 0, o_q), _cols_of_chunks(g_in_w, o_q, o_k)
    w_kv, w_mg = _cols_of_chunks(g_in_w, o_k, o_fl), _cols_of_chunks(g_in_w, o_mg, o_mg + 2 * d)
    w_fl = jnp.pad(_cols_of_chunks(g_in_w, o_fl, o_mg), ((0, 0), (0, LANES - nh)))
    w_rest = jnp.concatenate([w_q, w_kv, w_mg], axis=1)
    bf_pad = jnp.pad(w2['b_fgate'], ((0, 0), (0, LANES - nh)))

    nd = 2 * N_CHIPS
    c_act = _silu_pad(_all_gather_devices([c], "ag_cond")[0].reshape(nd * bl, d), nd * bl)
    my_cols = lambda a, n: lax.dynamic_slice_in_dim(a, chip * w2[n].shape[1], w2[n].shape[1], axis=1)
    mod_cols = [_mm(c_act, w2[n].astype(BF16), "nn", name=n + "_fwd", bias=my_cols(w2[b], n)) for n, b in zip(_ADA, _ADA_BIAS)]
    my_rows = lambda g: lax.dynamic_slice_in_dim(g, dev * bl, bl, axis=1).transpose(1, 0, 2).reshape(bl, -1)
    mod, modf = (my_rows(g) for g in _all_gather_chips(mod_cols, "ag_mod"))
    sh1, sc1, gt1, sh2, sc2, gt2 = [mod[:, i * d:(i + 1) * d].reshape(bl, 1, d) for i in range(6)]
    shf, scf = modf[:, :d].reshape(bl, 1, d), modf[:, d:].reshape(bl, 1, d)

    h1 = _norm_mod_fwd(x, w2['g_norm1'], sh1, sc1)
    h1f = h1.reshape(t, d)
    zr = _mm(h1f, w_rnn, "nn", name="in_rnn").reshape(bl, s, 2 * dr)
    q3 = _mm(h1f, w_q, "nn", name="in_q", out_dtype=BF16, scale=scale).reshape(bl, s, da)
    kv3 = _mm(h1f, w_kv, "nn", name="in_kv", out_dtype=BF16).reshape(bl, s, 2 * da)
    mg3 = _mm(h1f, w_mg, "nn", name="in_mg").reshape(bl, s, 2 * d)
    zf3 = _mm(h1f, w_fl, "nn", name="in_fl").reshape(bl, s, LANES)

    lru = (conv_full['w_rnn_conv'], w2['b_rnn_conv'], w2['w_lru_a'], w2['b_lru_a'], w2['w_lru_i'], w2['b_lru_i'], w2['lru_lambda'])
    y_rnn, *rnn_kept = _rnn_fwd(zr, *lru)

    f3 = _fgate_fwd(zf3, bf_pad)
    f_heads = f3[:, :, :nh].transpose(0, 2, 1).reshape(bl * nh, s)
    fcol = jnp.broadcast_to(f_heads[:, :, None], (bl * nh, s, LANES))
    frow = f_heads.reshape(bl * nh, 1, s)
    o3, lse_row = _attn_fwd(q3, kv3, fcol, frow, nh)

    pr3 = _mm(y_rnn.reshape(t, dr), w_proj_rnn_f, "nn", name="proj_rnn").reshape(bl, s, d)
    pa3 = _mm(o3.reshape(t, da), w_proj_attn_f, "nn", name="proj_attn").reshape(bl, s, d)
    merged = _merge_fwd(mg3, pr3, pa3)
    mo3 = _mm(merged.reshape(t, d), w_out_f, "nn", name="mix_out").reshape(bl, s, d)
    x1, h2 = _resid_norm_fwd(x, mo3, gt1, w2['g_norm2'], sh2, sc2)
    h2f = h2.reshape(t, d)
    up3 = _mm(h2f, gathered['w_ffn_up'], "nn", name="ffn_up", b_chunk=up_chunk, tn=up_chunk).reshape(bl, s, 2 * dff)
    act3 = _ffn_act_fwd(up3, conv_full['w_ffn_conv'], w2['b_ffn_conv'])
    yf3 = _mm(act3.reshape(t, dff), w_ffn_down_f, "nn", name="ffn_down").reshape(bl, s, d)

    dx2, dshf, dscf, dg_final, loss_part = _final_fwd_bwd(x1, yf3, gt2, w2['g_final'], shf, scf, loss_target)
    loss = lax.psum(loss_part[0, 0], ("x", "y", "c"))

    dyf, dgt2 = _gate_bwd(dx2, yf3, gt2, "ffn_gate_bwd")
    dyf_f = dyf.reshape(t, d)
    g_ffn_down = _mm(act3.reshape(t, dff), dyf_f, "tn", name="dw_ffn_down")
    dact3 = _mm(dyf_f, w_ffn_down_f, "nt", name="d_ffn_act").reshape(bl, s, dff)
    dgf, duf, g_ffn_conv, g_b_ffn_conv = _ffn_act_bwd(up3, dact3, conv_full['w_ffn_conv'], w2['b_ffn_conv'])
    dgf_f, duf_f = dgf.reshape(t, dff), duf.reshape(t, dff)
    g_ffn_up = jnp.concatenate([_mm(h2f, dgf_f, "tn", name="dw_ffn_up_gate", out_chunk=up_chunk),
                                _mm(h2f, duf_f, "tn", name="dw_ffn_up_value", out_chunk=up_chunk)], axis=0)
    dh2 = _mm([dgf_f, duf_f], gathered['w_ffn_up'], "nt", name="d_h2", b_chunk=up_chunk).reshape(bl, s, d)
    dx1, dsh2, dsc2, dg_norm2 = _norm_mod_bwd(dh2, x1, dx2, w2['g_norm2'], sc2, "norm2_bwd")

    dmo, dgt1 = _gate_bwd(dx1, mo3, gt1, "mix_gate_bwd")
    dmo_f = dmo.reshape(t, d)
    g_out = _mm(merged.reshape(t, d), dmo_f, "tn", name="dw_out")
    dm3 = _mm(dmo_f, w_out_f, "nt", name="d_merged").reshape(bl, s, d)
    dpr, dpa, dmr, dma = _merge_bwd(dm3, mg3, pr3, pa3)
    g_proj_rnn = _mm(y_rnn.reshape(t, dr), dpr.reshape(t, d), "tn", name="dw_proj_rnn")
    g_proj_attn = _mm(o3.reshape(t, da), dpa.reshape(t, d), "tn", name="dw_proj_attn")
    dyr3 = _mm(dpr.reshape(t, d), w_proj_rnn_f, "nt", name="d_y_rnn").reshape(bl, s, dr)
    do3 = _mm(dpa.reshape(t, d), w_proj_attn_f, "nt", name="d_y_attn").reshape(bl, s, da)

    dq3, dk3, dv3, dfk, dfq = _attn_bwd(q3, kv3, do3, o3, lse_row, fcol, frow, nh, scale)
    heads_last = lambda a: jnp.pad(a.reshape(bl, nh, s).transpose(0, 2, 1), ((0, 0), (0, 0), (0, LANES - nh)))
    dzf3, g_bf = _fgate_bwd(heads_last(dfk[:, :, 0]), heads_last(dfq), zf3, bf_pad)

    dxr, dgr, g_rnn_conv, g_b_rnn_conv, g_lru_a, g_b_lru_a, g_lru_i, g_b_lru_i, g_lam = _rnn_bwd(
        zr, rnn_kept, dyr3, lru[0], lru[2], lru[4], lru[6])

    dz = [a.reshape(t, -1) for a in (dxr, dgr, dq3, dk3, dv3, dmr, dma)]
    dzf_f = dzf3.reshape(t, LANES)
    seg_names = ("xr", "gr", "q", "k", "v", "mr", "ma")
    g_seg = [_mm(h1f, a, "tn", name="dw_in_" + n) for n, a in zip(seg_names, dz)]
    g_in_fl = _mm(h1f, dzf_f, "tn", name="dw_in_fl")[:, :nh]
    g_in = _chunks_of_cols(g_seg[:5] + [g_in_fl] + g_seg[5:], N_CHIPS)
    dh1 = _mm(dzf_f, w_fl, "nt", name="d_h1_fl")
    dh1 = _mm(dz[:2], w_rnn, "nt", name="d_h1_rnn", add=dh1)
    dh1 = _mm(dz[2:], w_rest, "nt", name="d_h1", add=dh1).reshape(bl, s, d)
    grad_x, dsh1, dsc1, dg_norm1 = _norm_mod_bwd(dh1, x, dx1, w2['g_norm1'], sc1, "norm1_bwd")

    dmods = [jnp.concatenate([dsh1, dsc1, dgt1, dsh2, dsc2, dgt2], axis=-1).reshape(bl, -1),
             jnp.concatenate([dshf, dscf], axis=-1).reshape(bl, -1)]
    dmods = [g.reshape(nd * bl, -1) for g in _all_gather_devices(dmods, "ag_dmod")]
    grad = {n: _mm(c_act, my_cols(g, n), "tn", name="dw_" + n) for n, g in zip(_ADA, dmods)}
    grad.update({b: _rowsum(g, "d" + b) for b, g in zip(_ADA_BIAS, dmods)})

    rowchunks = lambda g: g.reshape(N_CHIPS, g.shape[0] // N_CHIPS, g.shape[1])
    full = dict(w_in=g_in, w_proj_rnn=rowchunks(g_proj_rnn), w_proj_attn=rowchunks(g_proj_attn),
                w_out=rowchunks(g_out), w_ffn_up=g_ffn_up, w_ffn_down=rowchunks(g_ffn_down))
    small = dict(g_norm1=dg_norm1, w_rnn_conv=g_rnn_conv, b_rnn_conv=g_b_rnn_conv, w_lru_a=g_lru_a,
                 b_lru_a=g_b_lru_a, w_lru_i=g_lru_i, b_lru_i=g_b_lru_i, lru_lambda=g_lam, b_fgate=g_bf[:, :nh],
                 g_norm2=dg_norm2, w_ffn_conv=g_ffn_conv, b_ffn_conv=g_b_ffn_conv, g_final=dg_final)

    small_names = _REPLICATED + _CONV
    n_small = sum(small[n].size for n in small_names)
    rows_q = -(-n_small // (N_CHIPS * FLAT_COLS * 32 * RS_STEPS)) * 32 * RS_STEPS
    small_flat = _flat_pad([small[n] for n in small_names], N_CHIPS * rows_q * FLAT_COLS).reshape(N_CHIPS, rows_q, FLAT_COLS)
    reduced = _reduce_scatter_chips([full[n] for n in names] + [small_flat])
    grad.update(zip(names, reduced[:-1]))
    small_all = _all_gather_chips([reduced[-1]], "ag_small_grads")[0].reshape(-1)
    grad.update(zip(small_names, _split_flat(small_all, [small[n].shape for n in small_names])))
    for n in _CONV:
        n4 = w2[n].shape[1]
        grad[n] = lax.dynamic_slice_in_dim(grad[n], chip * n4, n4, axis=1)

    delta_w, new_m, new_v = {}, {}, {}
    for n in names + _ADA:
        delta_w[n], new_m[n], new_v[n] = _adamw(grad[n], w2[n], m2[n], v2[n], "adamw_" + n)
    small_names = small_names + _ADA_BIAS
    rows_small = -(-sum(w2[n].size for n in small_names) // (FLAT_COLS * SUBLANES)) * SUBLANES
    flat_small = lambda src: _flat_pad([src[n] for n in small_names], rows_small * FLAT_COLS).reshape(rows_small, FLAT_COLS)
    small_out = _adamw(flat_small(grad), flat_small(w2), flat_small(m2), flat_small(v2), "adamw_small")
    for dst, flat in zip((delta_w, new_m, new_v), small_out):
        dst.update(zip(small_names, _split_flat(flat.reshape(-1), [w2[n].shape for n in small_names])))

    out = [loss, grad_x]
    for src in (grad, delta_w, new_m, new_v):
        out += [src[n].reshape(shape_of[n]) for n in _WEIGHTS]
    return tuple(out)
```

```python
import functools
import math

import jax
import jax.numpy as jnp
from jax import lax
from jax.experimental import pallas as pl
from jax.experimental.pallas import tpu as pltpu

F32 = jnp.float32
BF16 = jnp.bfloat16
MESH = pl.DeviceIdType.MESH

RMS_EPS = 1e-6
LRU_C = 8.0
ADAM_LR = 0.001
ADAM_B1 = 0.9
ADAM_B2 = 0.999
ADAM_EPS = 1e-08
ADAM_WD = 0.01
ADAM_STEP = 10

LANES = 128
SUBLANES = 8
N_CHIPS = 4
FLAT_COLS = 1024
SCAN_SEGMENTS = 2 * SUBLANES
VMEM_LIMIT = 48 * 1024 * 1024
NEG_BIG = -1e30


def _cp(n_axes):
    return pltpu.CompilerParams(dimension_semantics=("arbitrary",) * n_axes, vmem_limit_bytes=VMEM_LIMIT)


def _tile(n, target, align):
    if n <= target:
        return n
    t = (target // align) * align
    while t >= align:
        if n % t == 0:
            return t
        t -= align
    return n


def _nice_rows(n, align):
    r = -(-n // align) * align
    while True:
        if r <= 640:
            return r, r
        t = _tile(r, 640, align)
        if 128 <= t <= 640:
            return r, t
        r += align


def _sigmoid(x):
    return 0.5 * jnp.tanh(0.5 * x) + 0.5


def _softplus(x):
    return jnp.maximum(x, 0.0) + jnp.log1p(jnp.exp(-jnp.abs(x)))


def _expm1(x, exp_x):
    small = x * (1.0 + 0.5 * x * (1.0 + (1.0 / 3.0) * x * (1.0 + 0.25 * x)))
    return jnp.where(jnp.abs(x) < 0.05, small, exp_x - 1.0)


_GELU_K = math.sqrt(2.0 / math.pi)
_GELU_C = 0.044715


def _gelu(x):
    t = jnp.tanh(_GELU_K * (x + _GELU_C * x * x * x))
    return 0.5 * x * (1.0 + t)


def _gelu_and_grad(x):
    t = jnp.tanh(_GELU_K * (x + _GELU_C * x * x * x))
    g = 0.5 * x * (1.0 + t)
    dg = 0.5 * (1.0 + t) + 0.5 * x * (1.0 - t * t) * _GELU_K * (1.0 + 3.0 * _GELU_C * x * x)
    return g, dg


def _shift_down(x, k):
    if k == 0:
        return x
    y = pltpu.roll(x, k, 0)
    rows = lax.broadcasted_iota(jnp.int32, (SUBLANES, x.shape[1]), 0)
    return jnp.concatenate([jnp.where(rows >= k, y[:SUBLANES], 0.0), y[SUBLANES:]], axis=0)


def _shift_up(x, k):
    if k == 0:
        return x
    s = x.shape[0]
    y = pltpu.roll(x, s - k, 0)
    rows = lax.broadcasted_iota(jnp.int32, (SUBLANES, x.shape[1]), 0)
    return jnp.concatenate([y[:s - SUBLANES], jnp.where(rows < SUBLANES - k, y[s - SUBLANES:], 0.0)], axis=0)


def _dot(a, b, dims):
    return lax.dot_general(a.astype(BF16), b.astype(BF16), (dims, ((), ())), preferred_element_type=F32)


_NN = ((1,), (0,))
_NT = ((1,), (1,))
_TN = ((0,), (0,))


def _mm(a, b, mode, *, name, out_dtype=F32, scale=None, bias=None, add=None, tm=1024, tn=1024, tk=1024,
        b_chunk=None, out_chunk=None):
    pieces = list(a) if isinstance(a, (list, tuple)) else [a]
    ksize = lambda p: p.shape[0] if mode == "tn" else p.shape[1]
    if b_chunk is None:
        brows, bcols = b.shape
    else:
        brows, bcols = b.shape[1], b.shape[0] * b_chunk
    k = sum(ksize(p) for p in pieces)
    if mode == "nt":
        m, n = pieces[0].shape[0], brows
        assert bcols == k, (bcols, k)
    else:
        m, n = (pieces[0].shape[1] if mode == "tn" else pieces[0].shape[0]), bcols
        assert brows == k, (brows, k)
    tm = _tile(m, tm, LANES)
    ncut = n
    if b_chunk is not None and mode != "nt":
        ncut = b_chunk
    if out_chunk is not None:
        ncut = math.gcd(ncut, out_chunk)
    tn = _tile(ncut, tn, LANES)
    kcut = b_chunk if (b_chunk is not None and mode == "nt") else k
    for p in pieces:
        kcut = math.gcd(kcut, ksize(p))
    tk = _tile(kcut, tk, LANES)
    nk = k // tk
    dims = {"nn": _NN, "nt": _NT, "tn": _TN}[mode]
    counts = [ksize(p) // tk for p in pieces]
    starts = [sum(counts[:i]) for i in range(len(pieces))]
    n_pieces = len(pieces)

    def a_spec(s0, cnt):
        kmap = (lambda kk: kk) if n_pieces == 1 else (lambda kk: jnp.clip(kk - s0, 0, cnt - 1))
        if mode == "tn":
            return pl.BlockSpec((tk, tm), lambda i, j, kk: (kmap(kk), i))
        return pl.BlockSpec((tm, tk), lambda i, j, kk: (i, kmap(kk)))

    if b_chunk is None:
        if mode == "nt":
            b_spec = pl.BlockSpec((tn, tk), lambda i, j, kk: (j, kk))
        else:
            b_spec = pl.BlockSpec((tk, tn), lambda i, j, kk: (kk, j))
    elif mode == "nt":
        per_b = b_chunk // tk
        b_spec = pl.BlockSpec((None, tn, tk), lambda i, j, kk: (kk // per_b, j, kk % per_b))
    else:
        per_b = b_chunk // tn
        b_spec = pl.BlockSpec((None, tk, tn), lambda i, j, kk: (j // per_b, kk, j % per_b))
    if out_chunk is None:
        out_spec = pl.BlockSpec((tm, tn), lambda i, j, kk: (i, j))
        out_shape = jax.ShapeDtypeStruct((m, n), out_dtype)
    else:
        per_o = out_chunk // tn
        out_spec = pl.BlockSpec((None, tm, tn), lambda i, j, kk: (j // per_o, i, j % per_o))
        out_shape = jax.ShapeDtypeStruct((n // out_chunk, m, out_chunk), out_dtype)
    in_specs = [a_spec(s0, cnt) for s0, cnt in zip(starts, counts)] + [b_spec]
    args = pieces + [b]
    if bias is not None:
        in_specs.append(pl.BlockSpec((1, tn), lambda i, j, kk: (0, j)))
        args.append(bias)
    if add is not None:
        in_specs.append(pl.BlockSpec((tm, tn), lambda i, j, kk: (i, j)))
        args.append(add)

    def kern(*refs):
        b_ref = refs[n_pieces]
        o_ref = refs[n_pieces + 1 + (bias is not None) + (add is not None)]

        def finish(r):
            if scale is not None:
                r = r * scale
            pos = n_pieces + 1
            if bias is not None:
                r = r + refs[pos][...]
                pos += 1
            if add is not None:
                r = r + refs[pos][...]
            o_ref[...] = r.astype(out_dtype)

        if nk == 1:
            finish(_dot(refs[0][...], b_ref[...], dims))
            return
        acc = refs[-1]
        kk = pl.program_id(2)

        @pl.when(kk == 0)
        def _():
            acc[...] = jnp.zeros_like(acc)

        if n_pieces == 1:
            acc[...] += _dot(refs[0][...], b_ref[...], dims)
        else:
            for idx in range(n_pieces):
                @pl.when((kk >= starts[idx]) & (kk < starts[idx] + counts[idx]))
                def _(idx=idx):
                    acc[...] += _dot(refs[idx][...], b_ref[...], dims)

        @pl.when(kk == nk - 1)
        def _():
            finish(acc[...])

    return pl.pallas_call(
        kern, name=name,
        grid=(m // tm, n // tn, nk),
        in_specs=in_specs, out_specs=out_spec, out_shape=out_shape,
        scratch_shapes=[pltpu.VMEM((tm, tn), F32)] if nk > 1 else [],
        compiler_params=_cp(3),
    )(*args)


def _silu_pad(c, rows):
    bl, d = c.shape

    def kern(c_ref, o_ref):
        o_ref[...] = jnp.zeros_like(o_ref)
        v = c_ref[...]
        o_ref[0:bl, :] = v * _sigmoid(v)

    return pl.pallas_call(kern, name="silu_pad", out_shape=jax.ShapeDtypeStruct((rows, d), F32))(c)


def _rowsum(x, name):
    r, n = x.shape

    def kern(x_ref, o_ref):
        o_ref[...] = jnp.sum(x_ref[...], axis=0, keepdims=True)

    return pl.pallas_call(kern, name=name, out_shape=jax.ShapeDtypeStruct((1, n), F32))(x)


def _norm_parts(x, g):
    r = lax.rsqrt(jnp.mean(x * x, axis=-1, keepdims=True) + RMS_EPS)
    xh = x * r
    return r, xh, xh * g


def _norm_bwd_parts(dh, xh, r, g, sc):
    n = xh * g
    dn = dh * (1.0 + sc)
    dxh = dn * g
    dx = r * (dxh - xh * jnp.mean(dxh * xh, axis=-1, keepdims=True))
    return dx, dh, dh * n, dn * xh


def _act_specs(ts, d, n):
    return [pl.BlockSpec((1, ts, d), lambda b, t: (b, t, 0)) for _ in range(n)]


def _vec_spec(d):
    return pl.BlockSpec((1, 1, d), lambda b, t: (b, 0, 0))


def _par_spec(d):
    return pl.BlockSpec((1, d), lambda b, t: (0, 0))


def _norm_mod_fwd(x3, g, sh, sc):
    bl, s, d = x3.shape
    ts = _tile(s, 512, SUBLANES)

    def kern(x_ref, g_ref, sh_ref, sc_ref, h_ref):
        _, _, n = _norm_parts(x_ref[0], g_ref[...])
        h_ref[0] = (n * (1.0 + sc_ref[0]) + sh_ref[0]).astype(BF16)

    return pl.pallas_call(
        kern, name="norm_mod_fwd", grid=(bl, s // ts),
        in_specs=_act_specs(ts, d, 1) + [_par_spec(d), _vec_spec(d), _vec_spec(d)],
        out_specs=_act_specs(ts, d, 1)[0],
        out_shape=jax.ShapeDtypeStruct((bl, s, d), BF16),
        compiler_params=_cp(2),
    )(x3, g, sh, sc)


def _resid_norm_fwd(x3, y3, gate, g, sh, sc):
    bl, s, d = x3.shape
    ts = _tile(s, 512, SUBLANES)

    def kern(x_ref, y_ref, gate_ref, g_ref, sh_ref, sc_ref, x1_ref, h_ref):
        x1 = x_ref[0] + gate_ref[0] * y_ref[0]
        x1_ref[0] = x1
        _, _, n = _norm_parts(x1, g_ref[...])
        h_ref[0] = (n * (1.0 + sc_ref[0]) + sh_ref[0]).astype(BF16)

    return pl.pallas_call(
        kern, name="resid_norm_fwd", grid=(bl, s // ts),
        in_specs=_act_specs(ts, d, 2) + [_vec_spec(d), _par_spec(d), _vec_spec(d), _vec_spec(d)],
        out_specs=_act_specs(ts, d, 2),
        out_shape=[jax.ShapeDtypeStruct((bl, s, d), F32), jax.ShapeDtypeStruct((bl, s, d), BF16)],
        compiler_params=_cp(2),
    )(x3, y3, gate, g, sh, sc)


def _norm_mod_bwd(dh3, x3, dres3, g, sc, name):
    bl, s, d = x3.shape
    ts = _tile(s, 512, SUBLANES)

    def kern(dh_ref, x_ref, dres_ref, g_ref, sc_ref, dx_ref, dsh_ref, dsc_ref, dg_ref):
        b, t = pl.program_id(0), pl.program_id(1)
        gv = g_ref[...]
        r, xh, _ = _norm_parts(x_ref[0], gv)
        dx, a, bb, cc = _norm_bwd_parts(dh_ref[0], xh, r, gv, sc_ref[0])
        dx_ref[0] = dres_ref[0] + dx

        @pl.when(t == 0)
        def _():
            dsh_ref[...] = jnp.zeros_like(dsh_ref)
            dsc_ref[...] = jnp.zeros_like(dsc_ref)

        @pl.when((t == 0) & (b == 0))
        def _():
            dg_ref[...] = jnp.zeros_like(dg_ref)

        dsh_ref[0] += jnp.sum(a, axis=0, keepdims=True)
        dsc_ref[0] += jnp.sum(bb, axis=0, keepdims=True)
        dg_ref[...] += jnp.sum(cc, axis=0, keepdims=True)

    return pl.pallas_call(
        kern, name=name, grid=(bl, s // ts),
        in_specs=_act_specs(ts, d, 3) + [_par_spec(d), _vec_spec(d)],
        out_specs=[_act_specs(ts, d, 1)[0], _vec_spec(d), _vec_spec(d), _par_spec(d)],
        out_shape=[jax.ShapeDtypeStruct((bl, s, d), F32), jax.ShapeDtypeStruct((bl, 1, d), F32),
                   jax.ShapeDtypeStruct((bl, 1, d), F32), jax.ShapeDtypeStruct((1, d), F32)],
        compiler_params=_cp(2),
    )(dh3, x3, dres3, g, sc)


def _gate_bwd(dx3, y3, gate, name):
    bl, s, d = dx3.shape
    ts = _tile(s, 512, SUBLANES)

    def kern(dx_ref, y_ref, gate_ref, dy_ref, dgate_ref):
        t = pl.program_id(1)
        dx = dx_ref[0]
        dy_ref[0] = (gate_ref[0] * dx).astype(BF16)

        @pl.when(t == 0)
        def _():
            dgate_ref[...] = jnp.zeros_like(dgate_ref)

        dgate_ref[0] += jnp.sum(dx * y_ref[0], axis=0, keepdims=True)

    return pl.pallas_call(
        kern, name=name, grid=(bl, s // ts),
        in_specs=_act_specs(ts, d, 2) + [_vec_spec(d)],
        out_specs=[_act_specs(ts, d, 1)[0], _vec_spec(d)],
        out_shape=[jax.ShapeDtypeStruct((bl, s, d), BF16), jax.ShapeDtypeStruct((bl, 1, d), F32)],
        compiler_params=_cp(2),
    )(dx3, y3, gate)


def _final_fwd_bwd(x1, yf, gate2, g, shf, scf, tgt):
    bl, s, d = x1.shape
    ts = _tile(s, 512, SUBLANES)

    def kern(x1_ref, yf_ref, gate_ref, g_ref, sh_ref, sc_ref, tgt_ref, dx_ref, dsh_ref, dsc_ref, dg_ref, loss_ref):
        b, t = pl.program_id(0), pl.program_id(1)
        gv, sc = g_ref[...], sc_ref[0]
        x2 = x1_ref[0] + gate_ref[0] * yf_ref[0]
        r, xh, n = _norm_parts(x2, gv)
        err = n * (1.0 + sc) + sh_ref[0] - tgt_ref[0]
        dx, a, bb, cc = _norm_bwd_parts(err * (1.0 / d), xh, r, gv, sc)
        dx_ref[0] = dx

        @pl.when(t == 0)
        def _():
            dsh_ref[...] = jnp.zeros_like(dsh_ref)
            dsc_ref[...] = jnp.zeros_like(dsc_ref)

        @pl.when((t == 0) & (b == 0))
        def _():
            dg_ref[...] = jnp.zeros_like(dg_ref)
            loss_ref[...] = jnp.zeros_like(loss_ref)

        dsh_ref[0] += jnp.sum(a, axis=0, keepdims=True)
        dsc_ref[0] += jnp.sum(bb, axis=0, keepdims=True)
        dg_ref[...] += jnp.sum(cc, axis=0, keepdims=True)
        tok = jnp.mean(err * err, axis=-1, keepdims=True)
        loss_ref[...] += 0.5 * jnp.sum(tok, axis=0, keepdims=True)

    return pl.pallas_call(
        kern, name="final_fwd_bwd", grid=(bl, s // ts),
        in_specs=_act_specs(ts, d, 2) + [_vec_spec(d), _par_spec(d), _vec_spec(d), _vec_spec(d)] + _act_specs(ts, d, 1),
        out_specs=[_act_specs(ts, d, 1)[0], _vec_spec(d), _vec_spec(d), _par_spec(d),
                   pl.BlockSpec((1, 1), lambda b, t: (0, 0))],
        out_shape=[jax.ShapeDtypeStruct((bl, s, d), F32), jax.ShapeDtypeStruct((bl, 1, d), F32),
                   jax.ShapeDtypeStruct((bl, 1, d), F32), jax.ShapeDtypeStruct((1, d), F32),
                   jax.ShapeDtypeStruct((1, 1), F32)],
        compiler_params=_cp(2),
    )(x1, yf, gate2, g, shf, scf, tgt)


def _rnn_gates(xr, cw, cb, wa, ba, wi, bi, lam):
    kw = cw.shape[0]
    xc = cb
    for k in range(kw):
        xc = xc + _shift_down(xr, kw - 1 - k) * cw[k:k + 1, :]
    r = _sigmoid(_dot(xc, wa, _NN) + ba)
    i = _sigmoid(_dot(xc, wi, _NN) + bi)
    sp = _softplus(-lam)
    log_a = -LRU_C * r * sp
    a = jnp.exp(log_a)
    mult = jnp.sqrt(-_expm1(2.0 * log_a, a * a))
    return xc, r, i, sp, a, mult


def _segment_scan(a_s, u_s, h_s, p_s, reverse):
    s, c = a_s.shape
    seg = s // SCAN_SEGMENTS

    unroll = math.gcd(seg, 8)

    def steps(n, carry):
        h, p = carry
        for j in range(unroll):
            t = n * unroll + j
            t = (seg - 1 - t) if reverse else t
            av = a_s[pl.ds(t, SCAN_SEGMENTS, stride=seg), :]
            uv = u_s[pl.ds(t, SCAN_SEGMENTS, stride=seg), :]
            h = av * h + uv
            p = p * av
            h_s[pl.ds(t, SCAN_SEGMENTS, stride=seg), :] = h
            p_s[pl.ds(t, SCAN_SEGMENTS, stride=seg), :] = p
        return h, p

    lax.fori_loop(0, seg // unroll, steps, (jnp.zeros((SCAN_SEGMENTS, c), F32), jnp.ones((SCAN_SEGMENTS, c), F32)))
    carry = jnp.zeros((1, c), F32)
    order = range(SCAN_SEGMENTS - 1, -1, -1) if reverse else range(SCAN_SEGMENTS)
    for j in order:
        rows = pl.ds(j * seg, seg)
        fixed = h_s[rows, :] + p_s[rows, :] * carry
        h_s[rows, :] = fixed
        carry = fixed[0:1, :] if reverse else fixed[seg - 1:seg, :]


def _rnn_specs(s, rb, nb):
    act = lambda off: pl.BlockSpec((1, s, rb), lambda b, n, off=off: (b, 0, off + n))
    par = pl.BlockSpec((1, rb), lambda b, n: (0, n))
    wsp = pl.BlockSpec((1, rb, rb), lambda b, n: (n, 0, 0))
    return act, par, wsp


def _rnn_fwd(zr3, cw, cb, wa, ba, wi, bi, lam):
    bl, s, two = zr3.shape
    nb, rb, _ = wa.shape
    dr = nb * rb
    kw = cw.shape[0]
    act, par, wsp = _rnn_specs(s, rb, nb)

    def kern(xr_ref, gr_ref, cw_ref, cb_ref, wa_ref, ba_ref, wi_ref, bi_ref, lam_ref,
             y_ref, h_ref, xc_ref, r_ref, i_ref, a_ref, mult_ref, a_s, u_s, h_s, p_s):
        xc, r, i, sp, a, mult = _rnn_gates(xr_ref[0], cw_ref[...], cb_ref[...], wa_ref[0], ba_ref[...],
                                           wi_ref[0], bi_ref[...], lam_ref[...])
        for ref, val in ((xc_ref, xc), (r_ref, r), (i_ref, i), (a_ref, a), (mult_ref, mult)):
            ref[0] = val
        a_s[...] = a
        u_s[...] = mult * (i * xc)
        _segment_scan(a_s, u_s, h_s, p_s, reverse=False)
        h = h_s[...]
        h_ref[0] = h
        y_ref[0] = (_gelu(gr_ref[0]) * h).astype(BF16)

    kept = jax.ShapeDtypeStruct((bl, s, dr), F32)
    return pl.pallas_call(
        kern, name="rnn_fwd", grid=(bl, nb),
        in_specs=[act(0), act(nb), pl.BlockSpec((kw, rb), lambda b, n: (0, n)), par, wsp, par, wsp, par, par],
        out_specs=[act(0)] * 7,
        out_shape=[jax.ShapeDtypeStruct((bl, s, dr), BF16)] + [kept] * 6,
        scratch_shapes=[pltpu.VMEM((s, rb), F32)] * 4,
        compiler_params=_cp(2),
    )(zr3, zr3, cw, cb, wa, ba, wi, bi, lam)


def _rnn_bwd(zr3, kept, dy3, cw, wa, wi, lam):
    bl, s, _ = zr3.shape
    nb, rb, _ = wa.shape
    dr = nb * rb
    kw = cw.shape[0]
    act = lambda off: pl.BlockSpec((1, s, rb), lambda n, b, off=off: (b, 0, off + n))
    par = pl.BlockSpec((1, rb), lambda n, b: (0, n))
    wsp = pl.BlockSpec((1, rb, rb), lambda n, b: (n, 0, 0))
    cws = pl.BlockSpec((kw, rb), lambda n, b: (0, n))

    def kern(xr_ref, gr_ref, h_ref, xc_ref, r_ref, i_ref, a_ref, mult_ref, dy_ref, cw_ref, wa_ref, wi_ref, lam_ref,
             dxr_ref, dgr_ref, dcw_ref, dcb_ref, dwa_ref, dba_ref, dwi_ref, dbi_ref, dlam_ref, a_s, u_s, h_s, p_s):
        b = pl.program_id(1)
        xr, cwv, lamv = xr_ref[0], cw_ref[...], lam_ref[...]
        wav, wiv = wa_ref[0], wi_ref[0]
        xc, r, i, a, mult = xc_ref[0], r_ref[0], i_ref[0], a_ref[0], mult_ref[0]
        sp = _softplus(-lamv)
        h, dy = h_ref[0], dy_ref[0]
        ge, dge = _gelu_and_grad(gr_ref[0])
        dgr_ref[0] = (dy * h * dge).astype(BF16)
        a_s[...] = _shift_up(a, 1)
        u_s[...] = dy * ge
        _segment_scan(a_s, u_s, h_s, p_s, reverse=True)
        g = h_s[...]
        da = g * _shift_down(h, 1)
        ix = i * xc
        dlog_a = da * a + (g * ix) * (-(a * a) / mult)
        di = g * mult * xc
        dpa = (dlog_a * (-LRU_C * sp)) * r * (1.0 - r)
        dpi = di * i * (1.0 - i)
        dxc = g * mult * i + _dot(dpa, wav, _NT) + _dot(dpi, wiv, _NT)
        dxr = jnp.zeros_like(dxc)
        dcw_rows = []
        for k in range(kw):
            dxr = dxr + _shift_up(dxc, kw - 1 - k) * cwv[k:k + 1, :]
            dcw_rows.append(jnp.sum(dxc * _shift_down(xr, kw - 1 - k), axis=0, keepdims=True))
        dxr_ref[0] = dxr.astype(BF16)

        @pl.when(b == 0)
        def _():
            for ref in (dcw_ref, dcb_ref, dwa_ref, dba_ref, dwi_ref, dbi_ref, dlam_ref):
                ref[...] = jnp.zeros_like(ref)

        for k in range(kw):
            dcw_ref[k:k + 1, :] += dcw_rows[k]
        dcb_ref[...] += jnp.sum(dxc, axis=0, keepdims=True)
        dwa_ref[0] += _dot(xc, dpa, _TN)
        dwi_ref[0] += _dot(xc, dpi, _TN)
        dba_ref[...] += jnp.sum(dpa, axis=0, keepdims=True)
        dbi_ref[...] += jnp.sum(dpi, axis=0, keepdims=True)
        dsp = jnp.sum(dlog_a * (-LRU_C * r), axis=0, keepdims=True)
        dlam_ref[...] += dsp * (-_sigmoid(-lamv))

    vec = jax.ShapeDtypeStruct((1, dr), F32)
    wsh = jax.ShapeDtypeStruct((nb, rb, rb), F32)
    return pl.pallas_call(
        kern, name="rnn_bwd", grid=(nb, bl),
        in_specs=[act(0), act(nb)] + [act(0)] * 7 + [cws, wsp, wsp, par],
        out_specs=[act(0), act(0), cws, par, wsp, par, wsp, par, par],
        out_shape=[jax.ShapeDtypeStruct((bl, s, dr), BF16), jax.ShapeDtypeStruct((bl, s, dr), BF16),
                   jax.ShapeDtypeStruct((kw, dr), F32), vec, wsh, vec, wsh, vec, vec],
        scratch_shapes=[pltpu.VMEM((s, rb), F32)] * 4,
        compiler_params=_cp(2),
    )(zr3, zr3, *kept, dy3, cw, wa, wi, lam)


def _tri(n, upper):
    r = lax.broadcasted_iota(jnp.int32, (n, n), 0)
    c = lax.broadcasted_iota(jnp.int32, (n, n), 1)
    return jnp.where((c >= r) if upper else (c <= r), 1.0, 0.0).astype(F32)


def _fgate_fwd(zf3, bf):
    bl, s, w = zf3.shape
    ch = _tile(s, 256, SUBLANES)

    def kern(z_ref, b_ref, f_ref):
        tri = _tri(ch, upper=False)
        carry = jnp.zeros((1, w), F32)
        for j in range(s // ch):
            rows = pl.ds(j * ch, ch)
            lf = -_softplus(-(z_ref[0, rows, :] + b_ref[...]))
            out = jnp.dot(tri, lf, precision=lax.Precision.HIGHEST, preferred_element_type=F32) + carry
            f_ref[0, rows, :] = out
            carry = out[ch - 1:ch, :]

    return pl.pallas_call(
        kern, name="fgate_fwd", grid=(bl,),
        in_specs=[pl.BlockSpec((1, s, w), lambda b: (b, 0, 0)), pl.BlockSpec((1, w), lambda b: (0, 0))],
        out_specs=pl.BlockSpec((1, s, w), lambda b: (b, 0, 0)),
        out_shape=jax.ShapeDtypeStruct((bl, s, w), F32),
        compiler_params=_cp(1),
    )(zf3, bf)


def _fgate_bwd(dfk3, dfq3, zf3, bf):
    bl, s, w = zf3.shape
    ch = _tile(s, 256, SUBLANES)

    def kern(dfk_ref, dfq_ref, z_ref, b_ref, dz_ref, db_ref):
        b = pl.program_id(0)
        tri = _tri(ch, upper=True)
        carry = jnp.zeros((1, w), F32)
        dbsum = jnp.zeros((1, w), F32)
        for j in range(s // ch - 1, -1, -1):
            rows = pl.ds(j * ch, ch)
            df = dfk_ref[0, rows, :] + dfq_ref[0, rows, :]
            dlf = jnp.dot(tri, df, precision=lax.Precision.HIGHEST, preferred_element_type=F32) + carry
            carry = dlf[0:1, :]
            dz = dlf * _sigmoid(-(z_ref[0, rows, :] + b_ref[...]))
            dz_ref[0, rows, :] = dz.astype(BF16)
            dbsum = dbsum + jnp.sum(dz, axis=0, keepdims=True)

        @pl.when(b == 0)
        def _():
            db_ref[...] = jnp.zeros_like(db_ref)

        db_ref[...] += dbsum

    return pl.pallas_call(
        kern, name="fgate_bwd", grid=(bl,),
        in_specs=[pl.BlockSpec((1, s, w), lambda b: (b, 0, 0))] * 3 + [pl.BlockSpec((1, w), lambda b: (0, 0))],
        out_specs=[pl.BlockSpec((1, s, w), lambda b: (b, 0, 0)), pl.BlockSpec((1, w), lambda b: (0, 0))],
        out_shape=[jax.ShapeDtypeStruct((bl, s, w), BF16), jax.ShapeDtypeStruct((1, w), F32)],
        compiler_params=_cp(1),
    )(dfk3, dfq3, zf3, bf)


def _lanes(col, width):
    return col if width == LANES else jnp.concatenate([col] * (width // LANES), axis=1)


def _causal(sc, row0, col0, transposed):
    r = lax.broadcasted_iota(jnp.int32, sc.shape, 0) + row0
    c = lax.broadcasted_iota(jnp.int32, sc.shape, 1) + col0
    return jnp.where((c >= r) if transposed else (r >= c), sc, NEG_BIG)


def _attn_fwd(q3, kv3, fcol, frow, nh):
    bl, s, da = q3.shape
    dh = da // nh
    tq = _tile(s, 512, LANES)
    nq = s // tq

    def kern(iq_tab, ik_tab, q_ref, k_ref, v_ref, fk_ref, fq_ref, o_ref, lse_ref, m_s, l_s, acc):
        iq, ik = iq_tab[pl.program_id(2)], ik_tab[pl.program_id(2)]

        @pl.when(ik == 0)
        def _():
            m_s[...] = jnp.full_like(m_s, NEG_BIG)
            l_s[...] = jnp.zeros_like(l_s)
            acc[...] = jnp.zeros_like(acc)

        def block(masked):
            st = _dot(k_ref[0], q_ref[0], _NT) - _lanes(fk_ref[0], tq) + fq_ref[0]
            if masked:
                st = _causal(st, ik * tq, iq * tq, True)
            m_old = m_s[...]
            m_new = jnp.maximum(m_old, jnp.max(st, axis=0, keepdims=True))
            alpha = jnp.exp(m_old - m_new)
            pt = jnp.exp(st - m_new)
            l_s[...] = alpha * l_s[...] + jnp.sum(pt, axis=0, keepdims=True)
            acc[...] = alpha * acc[...] + _dot(v_ref[0], pt, _TN)
            m_s[...] = m_new

        pl.when(ik < iq)(functools.partial(block, False))

        @pl.when(ik == iq)
        def _():
            block(True)
            l = l_s[...]
            o_ref[0] = (acc[...] / l).T
            lse_ref[0] = m_s[...] + jnp.log(l)

    pairs = [(i, j) for i in range(nq) for j in range(i + 1)]
    iq_tab, ik_tab = (jnp.asarray(col, jnp.int32) for col in zip(*pairs))
    qmap = lambda b, h, p, iqt, ikt: (b, iqt[p], h)
    kmap = lambda off: (lambda b, h, p, iqt, ikt: (b, ikt[p], off + h))
    return pl.pallas_call(
        kern, name="attn_fwd",
        grid_spec=pltpu.PrefetchScalarGridSpec(
            num_scalar_prefetch=2, grid=(bl, nh, len(pairs)),
            in_specs=[pl.BlockSpec((1, tq, dh), qmap), pl.BlockSpec((1, tq, dh), kmap(0)), pl.BlockSpec((1, tq, dh), kmap(nh)),
                      pl.BlockSpec((1, tq, LANES), lambda b, h, p, iqt, ikt: (b * nh + h, ikt[p], 0)),
                      pl.BlockSpec((1, 1, tq), lambda b, h, p, iqt, ikt: (b * nh + h, 0, iqt[p]))],
            out_specs=[pl.BlockSpec((1, tq, dh), qmap),
                       pl.BlockSpec((1, 1, tq), lambda b, h, p, iqt, ikt: (b * nh + h, 0, iqt[p]))],
            scratch_shapes=[pltpu.VMEM((1, tq), F32), pltpu.VMEM((1, tq), F32), pltpu.VMEM((dh, tq), F32)]),
        out_shape=[jax.ShapeDtypeStruct((bl, s, da), F32), jax.ShapeDtypeStruct((bl * nh, 1, s), F32)],
        compiler_params=_cp(3),
    )(iq_tab, ik_tab, q3, kv3, kv3, fcol, frow)


def _attn_bwd(q3, kv3, do3, o3, lse_row, fcol, frow, nh, scale):
    bl, s, da = q3.shape
    dh = da // nh
    tk = _tile(s, 512, LANES)
    nk = s // tk

    pairs = [(j, i) for j in range(nk) for i in range(j, nk)]

    def kern(ik_tab, iq_tab, q_ref, k_ref, v_ref, do_ref, o_ref, lse_ref, fk_ref, fq_ref, dq_ref, dk_ref, dv_ref, dfk_ref,
             dfq_ref, dq_acc, dk_acc, dv_acc, dfq_acc, delta_s):
        step = pl.program_id(2)
        ik, iq = ik_tab[step], iq_tab[step]
        qrow = pl.ds(iq, 1)

        @pl.when(step == 0)
        def _():
            dq_acc[...] = jnp.zeros_like(dq_acc)
            dfq_acc[...] = jnp.zeros_like(dfq_acc)

        @pl.when(ik == 0)
        def _():
            prod = do_ref[0] * o_ref[0]
            rows = lax.dot_general(jnp.ones((SUBLANES, dh), F32), prod, (_NT, ((), ())),
                                   precision=lax.Precision.HIGHEST, preferred_element_type=F32)
            delta_s[qrow, :] = rows[0:1, :]

        @pl.when(iq == ik)
        def _():
            dk_acc[...] = jnp.zeros_like(dk_acc)
            dv_acc[...] = jnp.zeros_like(dv_acc)

        def block(masked):
            q = q_ref[0]
            st = _dot(k_ref[0], q, _NT) - _lanes(fk_ref[0], tk) + fq_ref[0]
            if masked:
                st = _causal(st, ik * tk, iq * tk, True)
            pt = jnp.exp(st - lse_ref[0])
            dv_acc[...] += _dot(pt, do_ref[0], _NN)
            dpt = _dot(v_ref[0], do_ref[0], _NT)
            dst = (pt * (dpt - delta_s[qrow, :])).astype(BF16)
            q_ones = jnp.concatenate([q, jnp.ones_like(q)], axis=1)
            dk_acc[...] += _dot(dst, q_ones, _NN)
            qrows = pl.ds(pl.multiple_of(iq * tk, tk), tk)
            dq_acc[qrows, :] += _dot(dst, k_ref[0], _TN)
            dfq_acc[qrow, :] += jnp.sum(dst.astype(F32), axis=0, keepdims=True)

        pl.when(iq > ik)(functools.partial(block, False))
        pl.when(iq == ik)(functools.partial(block, True))

        @pl.when(iq == nk - 1)
        def _():
            ext = dk_acc[...]
            dk_ref[0] = ext[:, :dh].astype(BF16)
            dfk_ref[0] = -ext[:, dh:]
            dv_ref[0] = dv_acc[...].astype(BF16)

        @pl.when(step == len(pairs) - 1)
        def _():
            dq_ref[0] = (dq_acc[...] * scale).astype(BF16)
            dfq_ref[0] = dfq_acc[...]

    ik_tab, iq_tab = (jnp.asarray(col, jnp.int32) for col in zip(*pairs))
    qmap = lambda b, h, p, ikt, iqt: (b, iqt[p], h)
    omap = lambda b, h, p, ikt, iqt: (b, jnp.where(ikt[p] == 0, iqt[p], 0), h)
    rmap = lambda b, h, p, ikt, iqt: (b * nh + h, 0, iqt[p])
    kmap = lambda off: (lambda b, h, p, ikt, iqt: (b, ikt[p], off + h))
    bmap = lambda b, h, p, ikt, iqt: (b * nh + h, ikt[p], 0)
    return pl.pallas_call(
        kern, name="attn_bwd",
        grid_spec=pltpu.PrefetchScalarGridSpec(
            num_scalar_prefetch=2, grid=(bl, nh, len(pairs)),
            in_specs=[pl.BlockSpec((1, tk, dh), qmap), pl.BlockSpec((1, tk, dh), kmap(0)), pl.BlockSpec((1, tk, dh), kmap(nh)),
                      pl.BlockSpec((1, tk, dh), qmap), pl.BlockSpec((1, tk, dh), omap), pl.BlockSpec((1, 1, tk), rmap),
                      pl.BlockSpec((1, tk, LANES), bmap), pl.BlockSpec((1, 1, tk), rmap)],
            out_specs=[pl.BlockSpec((1, s, dh), lambda b, h, p, ikt, iqt: (b, 0, h)),
                       pl.BlockSpec((1, tk, dh), kmap(0)), pl.BlockSpec((1, tk, dh), kmap(0)),
                       pl.BlockSpec((1, tk, LANES), bmap),
                       pl.BlockSpec((1, nk, tk), lambda b, h, p, ikt, iqt: (b * nh + h, 0, 0))],
            scratch_shapes=[pltpu.VMEM((s, dh), F32), pltpu.VMEM((tk, 2 * dh), F32), pltpu.VMEM((tk, dh), F32),
                            pltpu.VMEM((nk, tk), F32), pltpu.VMEM((nk, tk), F32)]),
        out_shape=[jax.ShapeDtypeStruct((bl, s, da), BF16), jax.ShapeDtypeStruct((bl, s, da), BF16),
                   jax.ShapeDtypeStruct((bl, s, da), BF16), jax.ShapeDtypeStruct((bl * nh, s, LANES), F32),
                   jax.ShapeDtypeStruct((bl * nh, nk, tk), F32)],
        compiler_params=_cp(3),
    )(ik_tab, iq_tab, q3, kv3, kv3, do3, o3, lse_row, fcol, frow)


def _merge_fwd(mg3, pr3, pa3):
    bl, s, d = pr3.shape
    ts = _tile(s, 256, SUBLANES)
    half = lambda j: pl.BlockSpec((1, ts, d), lambda b, t, j=j: (b, t, j))

    def kern(mr_ref, ma_ref, pr_ref, pa_ref, o_ref):
        o_ref[0] = (_sigmoid(mr_ref[0]) * pr_ref[0] + _sigmoid(ma_ref[0]) * pa_ref[0]).astype(BF16)

    return pl.pallas_call(
        kern, name="merge_fwd", grid=(bl, s // ts),
        in_specs=[half(0), half(1)] + _act_specs(ts, d, 2), out_specs=_act_specs(ts, d, 1)[0],
        out_shape=jax.ShapeDtypeStruct((bl, s, d), BF16), compiler_params=_cp(2),
    )(mg3, mg3, pr3, pa3)


def _merge_bwd(dm3, mg3, pr3, pa3):
    bl, s, d = pr3.shape
    ts = _tile(s, 256, SUBLANES)
    half = lambda j: pl.BlockSpec((1, ts, d), lambda b, t, j=j: (b, t, j))

    def kern(dm_ref, mr_ref, ma_ref, pr_ref, pa_ref, dpr_ref, dpa_ref, dmr_ref, dma_ref):
        dm = dm_ref[0]
        gr, ga = _sigmoid(mr_ref[0]), _sigmoid(ma_ref[0])
        dpr_ref[0] = (gr * dm).astype(BF16)
        dpa_ref[0] = (ga * dm).astype(BF16)
        dmr_ref[0] = (dm * pr_ref[0] * gr * (1.0 - gr)).astype(BF16)
        dma_ref[0] = (dm * pa_ref[0] * ga * (1.0 - ga)).astype(BF16)

    return pl.pallas_call(
        kern, name="merge_bwd", grid=(bl, s // ts),
        in_specs=_act_specs(ts, d, 1) + [half(0), half(1)] + _act_specs(ts, d, 2), out_specs=_act_specs(ts, d, 4),
        out_shape=[jax.ShapeDtypeStruct((bl, s, d), BF16)] * 4, compiler_params=_cp(2),
    )(dm3, mg3, mg3, pr3, pa3)


def _ffn_conv(gf, cw, cb):
    kw = cw.shape[0]
    y = cb
    for k in range(kw):
        y = y + _shift_down(gf, kw - 1 - k) * cw[k:k + 1, :]
    return y


def _ffn_act_fwd(up3, cw, cb):
    bl, s, two = up3.shape
    dff = two // 2
    kw = cw.shape[0]
    tc = _tile(dff, 256, LANES)
    nc = dff // tc

    def kern(gf_ref, uf_ref, cw_ref, cb_ref, o_ref):
        o_ref[0] = (_gelu(_ffn_conv(gf_ref[0], cw_ref[...], cb_ref[...])) * uf_ref[0]).astype(BF16)

    act = lambda off: pl.BlockSpec((1, s, tc), lambda b, j, off=off: (b, 0, off + j))
    return pl.pallas_call(
        kern, name="ffn_act_fwd", grid=(bl, nc),
        in_specs=[act(0), act(nc), pl.BlockSpec((kw, tc), lambda b, j: (0, j)), pl.BlockSpec((1, tc), lambda b, j: (0, j))],
        out_specs=act(0), out_shape=jax.ShapeDtypeStruct((bl, s, dff), BF16), compiler_params=_cp(2),
    )(up3, up3, cw, cb)


def _ffn_act_bwd(up3, dact3, cw, cb):
    bl, s, two = up3.shape
    dff = two // 2
    kw = cw.shape[0]
    tc = _tile(dff, 256, LANES)
    nc = dff // tc

    def kern(gf_ref, uf_ref, da_ref, cw_ref, cb_ref, dgf_ref, duf_ref, dcw_ref, dcb_ref):
        b = pl.program_id(1)
        gf, cwv, da = gf_ref[0], cw_ref[...], da_ref[0]
        ge, dge = _gelu_and_grad(_ffn_conv(gf, cwv, cb_ref[...]))
        duf_ref[0] = (da * ge).astype(BF16)
        dgc = da * uf_ref[0] * dge
        dgf = jnp.zeros_like(dgc)
        rows = []
        for k in range(kw):
            dgf = dgf + _shift_up(dgc, kw - 1 - k) * cwv[k:k + 1, :]
            rows.append(jnp.sum(dgc * _shift_down(gf, kw - 1 - k), axis=0, keepdims=True))
        dgf_ref[0] = dgf.astype(BF16)

        @pl.when(b == 0)
        def _():
            dcw_ref[...] = jnp.zeros_like(dcw_ref)
            dcb_ref[...] = jnp.zeros_like(dcb_ref)

        for k in range(kw):
            dcw_ref[k:k + 1, :] += rows[k]
        dcb_ref[...] += jnp.sum(dgc, axis=0, keepdims=True)

    act = lambda off: pl.BlockSpec((1, s, tc), lambda j, b, off=off: (b, 0, off + j))
    cws = pl.BlockSpec((kw, tc), lambda j, b: (0, j))
    cbs = pl.BlockSpec((1, tc), lambda j, b: (0, j))
    return pl.pallas_call(
        kern, name="ffn_act_bwd", grid=(nc, bl),
        in_specs=[act(0), act(nc), act(0), cws, cbs], out_specs=[act(0), act(0), cws, cbs],
        out_shape=[jax.ShapeDtypeStruct((bl, s, dff), BF16), jax.ShapeDtypeStruct((bl, s, dff), BF16),
                   jax.ShapeDtypeStruct((kw, dff), F32), jax.ShapeDtypeStruct((1, dff), F32)],
        compiler_params=_cp(2),
    )(up3, up3, dact3, cw, cb)


_HBM = pl.BlockSpec(memory_space=pltpu.HBM)


def _place():
    x, y, c = lax.axis_index("x"), lax.axis_index("y"), lax.axis_index("c")
    chips = dict(me=2 * x + y, nx=2 * (1 - x) + y, ny=2 * x + (1 - y), diag=2 * (1 - x) + (1 - y))
    peers = dict(nx=(1 - x, y, c), ny=(x, 1 - y, c), sib=(x, y, 1 - c))
    return c, chips, peers


def _remote(src, dst, sems, k, to):
    return pltpu.make_async_remote_copy(src_ref=src, dst_ref=dst, send_sem=sems[0].at[k], recv_sem=sems[1].at[k],
                                        device_id=to, device_id_type=MESH)


RS_STEPS = 2


def _piece(q, idx, n=1):
    start = idx * q
    if not isinstance(start, int):
        start = pl.multiple_of(start, SUBLANES)
    return pl.ds(start, n * q)


def _all_gather_chips(xs, name):
    nt = len(xs)
    per = 9

    def body(*refs):
        x_refs, o_refs = refs[:nt], refs[nt:2 * nt]
        send_sems, recv_sems = refs[2 * nt:]
        c, chip, peer = _place()
        sems = (send_sems, recv_sems)
        me, nx, ny, dg = chip["me"], chip["nx"], chip["ny"], chip["diag"]
        sends = []

        def arrive(k, dst):
            _remote(dst, dst, sems, k, peer["sib"]).wait_recv()

        def pass_on(k, blk, to):
            cp = _remote(blk, blk, sems, k, peer[to])
            cp.start()
            sends.append(cp)

        for t in range(nt):
            q = xs[t].shape[0] // 4
            half = _piece(q, 2 * c, 2)
            for k, to in ((0, "nx"), (1, "ny")):
                cp = _remote(x_refs[t].at[half], o_refs[t].at[me, half], sems, per * t + k, peer[to])
                cp.start()
                sends.append(cp)
            cp = _remote(x_refs[t], o_refs[t].at[me], sems, per * t + 8, peer["sib"])
            cp.start()
            sends.append(cp)
        for t in range(nt):
            q, o, k0 = xs[t].shape[0] // 4, o_refs[t], per * t
            half, sub0, sub1 = _piece(q, 2 * c, 2), _piece(q, 2 * c), _piece(q, 2 * c + 1)
            arrive(k0 + 0, o.at[nx, half])
            pass_on(k0 + 2, o.at[nx, sub0], "ny")
            pass_on(k0 + 4, o.at[nx, half], "sib")
            arrive(k0 + 1, o.at[ny, half])
            pass_on(k0 + 3, o.at[ny, sub1], "nx")
            pass_on(k0 + 5, o.at[ny, half], "sib")
            arrive(k0 + 2, o.at[dg, sub0])
            pass_on(k0 + 6, o.at[dg, sub0], "sib")
            arrive(k0 + 3, o.at[dg, sub1])
            pass_on(k0 + 7, o.at[dg, sub1], "sib")
        for t in range(nt):
            q, o, k0 = xs[t].shape[0] // 4, o_refs[t], per * t
            arrive(k0 + 4, o.at[nx, _piece(q, 2 * (1 - c), 2)])
            arrive(k0 + 5, o.at[ny, _piece(q, 2 * (1 - c), 2)])
            arrive(k0 + 6, o.at[dg, _piece(q, 2 * (1 - c))])
            arrive(k0 + 7, o.at[dg, _piece(q, 2 * (1 - c) + 1)])
            arrive(k0 + 8, o.at[me])
        for cp in sends:
            cp.wait_send()

    return pl.pallas_call(
        body, name=name, in_specs=[_HBM] * nt, out_specs=[_HBM] * nt,
        out_shape=[jax.ShapeDtypeStruct((N_CHIPS,) + x.shape, x.dtype) for x in xs],
        scratch_shapes=[pltpu.SemaphoreType.DMA((per * nt,)), pltpu.SemaphoreType.DMA((per * nt,))],
    )(*xs)


def _all_gather_devices(xs, name):
    nt = len(xs)
    per = 7

    def body(*refs):
        x_refs, o_refs = refs[:nt], refs[nt:2 * nt]
        send_sems, recv_sems, local_sems = refs[2 * nt:]
        x, y, c = lax.axis_index("x"), lax.axis_index("y"), lax.axis_index("c")
        sems = (send_sems, recv_sems)
        sib = (x, y, 1 - c)
        chips = [(1 - x, y), (x, 1 - y), (1 - x, 1 - y)]
        slot = lambda px, py, pc: 4 * px + 2 * py + pc
        me = slot(x, y, c)
        sends, copies = [], []

        def arrive(k, dst):
            _remote(dst, dst, sems, k, sib).wait_recv()

        for t in range(nt):
            cp = pltpu.make_async_copy(x_refs[t], o_refs[t].at[me], local_sems.at[t])
            cp.start()
            copies.append(cp)
            for k, to in enumerate([sib] + [(*chip, c) for chip in chips]):
                cp = _remote(x_refs[t], o_refs[t].at[me], sems, per * t + k, to)
                cp.start()
                sends.append(cp)
        for t in range(nt):
            for j, chip in enumerate(chips):
                blk = o_refs[t].at[slot(*chip, c)]
                arrive(per * t + 1 + j, blk)
                cp = _remote(blk, blk, sems, per * t + 4 + j, sib)
                cp.start()
                sends.append(cp)
        for t in range(nt):
            arrive(per * t, o_refs[t].at[slot(x, y, 1 - c)])
            for j, chip in enumerate(chips):
                arrive(per * t + 4 + j, o_refs[t].at[slot(*chip, 1 - c)])
        for cp in sends:
            cp.wait_send()
        for cp in copies:
            cp.wait()

    return pl.pallas_call(
        body, name=name, in_specs=[_HBM] * nt, out_specs=[_HBM] * nt,
        out_shape=[jax.ShapeDtypeStruct((2 * N_CHIPS,) + a.shape, a.dtype) for a in xs],
        scratch_shapes=[pltpu.SemaphoreType.DMA((per * nt,)), pltpu.SemaphoreType.DMA((per * nt,)),
                        pltpu.SemaphoreType.DMA((nt,))],
    )(*xs)


def _exchange(name, xs, out_shapes, plan):
    nt = len(xs)

    def body(*refs):
        x_refs, o_refs = refs[:nt], refs[nt:2 * nt]
        send_sems, recv_sems = refs[2 * nt:]
        c, chip, peer = _place()
        cps = []
        for t in range(nt):
            for src, dst, to in plan(c, chip, x_refs[t], o_refs[t], xs[t].shape):
                cps.append(_remote(src, dst, (send_sems, recv_sems), len(cps), peer[to]))
        for cp in cps:
            cp.start()
        for cp in cps:
            cp.wait()

    n_copies = nt * len(plan(0, dict(me=0, nx=2, ny=1, diag=3), None, None, xs[0].shape, count_only=True))
    return pl.pallas_call(
        body, name=name, in_specs=[_HBM] * nt, out_specs=[_HBM] * nt,
        out_shape=[jax.ShapeDtypeStruct(s, x.dtype) for s, x in zip(out_shapes, xs)],
        scratch_shapes=[pltpu.SemaphoreType.DMA((n_copies,)), pltpu.SemaphoreType.DMA((n_copies,))],
    )(*xs)


def _plan_sibling(c, chip, g, out, shape, count_only=False):
    if count_only:
        return [None] * N_CHIPS
    q = shape[1] // 4
    return [(g.at[j, _piece(q, 2 * (1 - c), 2)], out.at[j], "sib") for j in range(N_CHIPS)]


def _plan_first(c, chip, p, out, shape, count_only=False):
    if count_only:
        return [None] * 4
    q = shape[1] // 2
    return [(p.at[chip["nx"], _piece(q, 0)], out.at[0], "nx"), (p.at[chip["diag"], _piece(q, 0)], out.at[1], "nx"),
            (p.at[chip["ny"], _piece(q, 1)], out.at[2], "ny"), (p.at[chip["diag"], _piece(q, 1)], out.at[3], "ny")]


def _plan_second(c, chip, p, out, shape, count_only=False):
    if count_only:
        return [None] * 2
    return [(p.at[1], out.at[0], "ny"), (p.at[3], out.at[1], "nx")]


def _rs_last(ps):
    nt = len(ps)

    def body(*refs):
        p_refs, o_refs = refs[:nt], refs[nt:2 * nt]
        send_sems, recv_sems = refs[2 * nt:]
        c, _, peer = _place()
        sems = (send_sems, recv_sems)
        cps = []
        for t in range(nt):
            q = ps[t].shape[0] // 4
            mine = _piece(q, 2 * c, 2)
            cps.append(_remote(p_refs[t].at[mine], o_refs[t].at[mine], sems, t, peer["sib"]))
            cps[-1].start()
        for t in range(nt):
            q = ps[t].shape[0] // 4
            theirs = _piece(q, 2 * (1 - c), 2)
            cps[t].wait_send()
            _remote(p_refs[t].at[theirs], o_refs[t].at[theirs], sems, t, peer["sib"]).wait_recv()

    return pl.pallas_call(
        body, name="rs_last", in_specs=[_HBM] * nt, out_specs=[_HBM] * nt,
        out_shape=[jax.ShapeDtypeStruct(p.shape, F32) for p in ps],
        input_output_aliases={t: t for t in range(nt)},
        scratch_shapes=[pltpu.SemaphoreType.DMA((nt,)), pltpu.SemaphoreType.DMA((nt,))],
    )(*ps)


def _add_stage(name, grid, a_list, b_list, a_map, b_map, tbs, out_shapes, out_map, out_dtype, prefetch=None):
    nt = len(a_list)
    lead = lambda shape: (None,) * (len(shape) - 2)

    def kern(*refs):
        refs = refs[(1 if prefetch is not None else 0):]
        for t in range(nt):
            refs[2 * nt + t][...] = (refs[t][...].astype(F32) + refs[nt + t][...].astype(F32)).astype(out_dtype)

    in_specs = [pl.BlockSpec(lead(a.shape) + (tb, a.shape[-1]), a_map) for a, tb in zip(a_list, tbs)]
    in_specs += [pl.BlockSpec(lead(b.shape) + (tb, b.shape[-1]), b_map) for b, tb in zip(b_list, tbs)]
    out_specs = [pl.BlockSpec(lead(s) + (tb, s[-1]), out_map) for s, tb in zip(out_shapes, tbs)]
    out_shape = [jax.ShapeDtypeStruct(s, out_dtype) for s in out_shapes]
    if prefetch is None:
        return pl.pallas_call(kern, name=name, grid=grid, in_specs=in_specs, out_specs=out_specs, out_shape=out_shape,
                              compiler_params=_cp(len(grid)))(*a_list, *b_list)
    return pl.pallas_call(
        kern, name=name,
        grid_spec=pltpu.PrefetchScalarGridSpec(num_scalar_prefetch=1, grid=grid, in_specs=in_specs, out_specs=out_specs),
        out_shape=out_shape, compiler_params=_cp(len(grid)))(prefetch, *a_list, *b_list)


def _reduce_scatter_chips(gs):
    x, y, c = lax.axis_index("x"), lax.axis_index("y"), lax.axis_index("c")
    me, nx, ny = 2 * x + y, 2 * (1 - x) + y, 2 * x + (1 - y)
    st = RS_STEPS
    unit = 4 * st * 2 * SUBLANES
    rows = [g.shape[1] for g in gs]
    gs = [jnp.pad(g, ((0, 0), (0, -g.shape[1] % unit), (0, 0))) for g in gs]
    qs = [g.shape[1] // 4 for g in gs]
    tbs = [q // st for q in qs]
    cols = [g.shape[2] for g in gs]
    core = jnp.reshape(c, (1,)).astype(jnp.int32)

    got = _exchange("rs_sibling", gs, [(N_CHIPS, 2 * q, cc) for q, cc in zip(qs, cols)], _plan_sibling)
    p0 = _add_stage("rs_add_sibling", (N_CHIPS, 2, st), gs, got,
                    lambda j, h, s, c_ref: (j, (2 * c_ref[0] + h) * st + s, 0), lambda j, h, s, c_ref: (j, h * st + s, 0),
                    tbs, [(N_CHIPS, 2 * q, cc) for q, cc in zip(qs, cols)], lambda j, h, s, c_ref: (j, h * st + s, 0),
                    BF16, prefetch=core)
    got = _exchange("rs_first", p0, [(4, q, cc) for q, cc in zip(qs, cols)], _plan_first)
    p1 = _add_stage("rs_add_first", (4, st), p0, got,
                    lambda k, s, i_ref: (i_ref[k], (k // 2) * st + s, 0), lambda k, s, i_ref: (k, s, 0),
                    tbs, [(4, q, cc) for q, cc in zip(qs, cols)], lambda k, s, i_ref: (k, s, 0),
                    BF16, prefetch=jnp.stack([me, ny, me, nx]).astype(jnp.int32))
    got = _exchange("rs_second", p1, [(2, q, cc) for q, cc in zip(qs, cols)], _plan_second)
    p2 = _add_stage("rs_add_second", (2, st), p1, got, lambda h, s, c_ref: (2 * h, s, 0), lambda h, s, c_ref: (h, s, 0),
                    tbs, [(4 * q, cc) for q, cc in zip(qs, cols)], lambda h, s, c_ref: ((2 * c_ref[0] + h) * st + s, 0),
                    F32, prefetch=core)
    return [out[:r] for out, r in zip(_rs_last(p2), rows)]


def _adamw(g, w, m, v, name):
    rows, cc = g.shape
    tr = _tile(rows, max(SUBLANES, (1 << 18) // cc), SUBLANES)
    k1 = 1.0 - ADAM_B1 ** ADAM_STEP
    k2 = 1.0 - ADAM_B2 ** ADAM_STEP

    def kern(g_ref, w_ref, m_ref, v_ref, d_ref, nm_ref, nv_ref):
        gv = g_ref[...]
        nm = ADAM_B1 * m_ref[...] + (1.0 - ADAM_B1) * gv
        nv = ADAM_B2 * v_ref[...] + (1.0 - ADAM_B2) * (gv * gv)
        nm_ref[...] = nm
        nv_ref[...] = nv
        d_ref[...] = -ADAM_LR * ((nm / k1) / (jnp.sqrt(nv / k2) + ADAM_EPS) + ADAM_WD * w_ref[...])

    spec = pl.BlockSpec((tr, cc), lambda t: (t, 0))
    wspec = spec if w.ndim == 2 else pl.BlockSpec((None, tr, cc), lambda t: (0, t, 0))
    return pl.pallas_call(
        kern, name=name, grid=(rows // tr,), in_specs=[spec] + [wspec] * 3, out_specs=[wspec] * 3,
        out_shape=[jax.ShapeDtypeStruct(w.shape, F32)] * 3, compiler_params=_cp(1),
    )(g, w, m, v)


def _flat_pad(parts, total):
    flat = jnp.concatenate([p.reshape(-1) for p in parts])
    return jnp.pad(flat, (0, total - flat.shape[0]))


def _split_flat(flat, shapes):
    out, pos = [], 0
    for shp in shapes:
        size = math.prod(shp)
        out.append(flat[pos:pos + size].reshape(shp))
        pos += size
    return out


def _cols_of_chunks(chunks, lo, hi):
    width = chunks.shape[2]
    parts = []
    for j in range(chunks.shape[0]):
        a, b = max(lo, j * width), min(hi, (j + 1) * width)
        if a < b:
            parts.append(chunks[j, :, a - j * width:b - j * width])
    return parts[0] if len(parts) == 1 else jnp.concatenate(parts, axis=1)


def _chunks_of_cols(segments, n_chunks):
    total = sum(s.shape[1] for s in segments)
    width = total // n_chunks
    chunks = []
    for j in range(n_chunks):
        lo, hi, pos, parts = j * width, (j + 1) * width, 0, []
        for s in segments:
            a, b = max(lo, pos), min(hi, pos + s.shape[1])
            if a < b:
                parts.append(s[:, a - pos:b - pos])
            pos += s.shape[1]
        chunks.append(parts[0] if len(parts) == 1 else jnp.concatenate(parts, axis=1))
    return jnp.stack(chunks)


_WEIGHTS = ['w_ada', 'b_ada', 'g_norm1', 'w_in', 'w_rnn_conv', 'b_rnn_conv', 'w_lru_a', 'b_lru_a', 'w_lru_i', 'b_lru_i',
            'lru_lambda', 'b_fgate', 'w_proj_rnn', 'w_proj_attn', 'w_out', 'g_norm2', 'w_ffn_up', 'w_ffn_conv',
            'b_ffn_conv', 'w_ffn_down', 'w_ada_final', 'b_ada_final', 'g_final']
_MATMUL = ['w_in', 'w_proj_rnn', 'w_proj_attn', 'w_out', 'w_ffn_up', 'w_ffn_down']
_ADA = ['w_ada', 'w_ada_final']
_ADA_BIAS = ['b_ada', 'b_ada_final']
_CONV = ['w_rnn_conv', 'w_ffn_conv']
_REPLICATED = [n for n in _WEIGHTS if n not in _MATMUL + _ADA + _ADA_BIAS + _CONV]


def kernel(x, c, w_ada, b_ada, g_norm1, w_in, w_rnn_conv, b_rnn_conv, w_lru_a, b_lru_a, w_lru_i, b_lru_i, lru_lambda, b_fgate, w_proj_rnn, w_proj_attn, w_out, g_norm2, w_ffn_up, w_ffn_conv, b_ffn_conv, w_ffn_down, w_ada_final, b_ada_final, g_final, loss_target, m_w_ada, m_b_ada, m_g_norm1, m_w_in, m_w_rnn_conv, m_b_rnn_conv, m_w_lru_a, m_b_lru_a, m_w_lru_i, m_b_lru_i, m_lru_lambda, m_b_fgate, m_w_proj_rnn, m_w_proj_attn, m_w_out, m_g_norm2, m_w_ffn_up, m_w_ffn_conv, m_b_ffn_conv, m_w_ffn_down, m_w_ada_final, m_b_ada_final, m_g_final, v_w_ada, v_b_ada, v_g_norm1, v_w_in, v_w_rnn_conv, v_b_rnn_conv, v_w_lru_a, v_b_lru_a, v_w_lru_i, v_b_lru_i, v_lru_lambda, v_b_fgate, v_w_proj_rnn, v_w_proj_attn, v_w_out, v_g_norm2, v_w_ffn_up, v_w_ffn_conv, v_b_ffn_conv, v_w_ffn_down, v_w_ada_final, v_b_ada_final, v_g_final):
    args = locals()
    shape_of = {n: args[n].shape for n in _WEIGHTS}

    def view(a):
        if a.ndim >= 3:
            return a[0]
        return a[None, :] if a.ndim == 1 else a

    w2 = {n: view(args[n]) for n in _WEIGHTS}
    m2 = {n: args['m_' + n].reshape(w2[n].shape) for n in _WEIGHTS}
    v2 = {n: args['v_' + n].reshape(w2[n].shape) for n in _WEIGHTS}

    bl, s, d = x.shape
    t = bl * s
    nh = b_fgate.shape[-1]
    nb, rb = w_lru_a.shape[1], w_lru_a.shape[2]
    dr = nb * rb
    da = w2['w_proj_attn'].shape[0] * N_CHIPS
    dh = da // nh
    dff = w2['w_ffn_conv'].shape[1] * N_CHIPS
    scale = dh ** -0.5
    chip = 2 * lax.axis_index("x") + lax.axis_index("y")
    dev = 2 * chip + lax.axis_index("c")

    names = list(_MATMUL)
    n_conv = sum(w2[n].size for n in _CONV)
    rows_conv = -(-n_conv // (FLAT_COLS * 32)) * 32
    conv_local = _flat_pad([w2[n] for n in _CONV], rows_conv * FLAT_COLS).reshape(rows_conv, FLAT_COLS)
    *weights_all, conv_all = _all_gather_chips([w2[n].astype(BF16) for n in names] + [conv_local], "ag_weights")
    gathered = dict(zip(names, weights_all))
    conv_all = conv_all.reshape(N_CHIPS, -1)
    conv_full, pos = {}, 0
    for n in _CONV:
        r, n4 = w2[n].shape
        blocks = conv_all[:, pos:pos + r * n4].reshape(N_CHIPS, r, n4)
        conv_full[n] = jnp.concatenate([blocks[j] for j in range(N_CHIPS)], axis=1)
        pos += r * n4
    rowmajor = lambda n: gathered[n].reshape(-1, gathered[n].shape[2])
    w_proj_rnn_f, w_proj_attn_f, w_out_f, w_ffn_down_f = (rowmajor(n) for n in ('w_proj_rnn', 'w_proj_attn', 'w_out', 'w_ffn_down'))
    up_chunk = w2['w_ffn_up'].shape[1]

    o_q, o_k, o_fl = 2 * dr, 2 * dr + da, 2 * dr + 3 * da
    o_mg = o_fl + nh
    g_in_w = gathered['w_in']
    w_rnn, w_q = _cols_of_chunks(g_in_w, 0, o_q), _cols_of_chunks(g_in_w, o_q, o_k)
    w_kv, w_mg = _cols_of_chunks(g_in_w, o_k, o_fl), _cols_of_chunks(g_in_w, o_mg, o_mg + 2 * d)
    w_fl = jnp.pad(_cols_of_chunks(g_in_w, o_fl, o_mg), ((0, 0), (0, LANES - nh)))
    w_rest = jnp.concatenate([w_q, w_kv, w_mg], axis=1)
    bf_pad = jnp.pad(w2['b_fgate'], ((0, 0), (0, LANES - nh)))

    nd = 2 * N_CHIPS
    c_act = _silu_pad(_all_gather_devices([c], "ag_cond")[0].reshape(nd * bl, d), nd * bl)
    my_cols = lambda a, n: lax.dynamic_slice_in_dim(a, chip * w2[n].shape[1], w2[n].shape[1], axis=1)
    mod_cols = [_mm(c_act, w2[n].astype(BF16), "nn", name=n + "_fwd", bias=my_cols(w2[b], n)) for n, b in zip(_ADA, _ADA_BIAS)]
    my_rows = lambda g: lax.dynamic_slice_in_dim(g, dev * bl, bl, axis=1).transpose(1, 0, 2).reshape(bl, -1)
    mod, modf = (my_rows(g) for g in _all_gather_chips(mod_cols, "ag_mod"))
    sh1, sc1, gt1, sh2, sc2, gt2 = [mod[:, i * d:(i + 1) * d].reshape(bl, 1, d) for i in range(6)]
    shf, scf = modf[:, :d].reshape(bl, 1, d), modf[:, d:].reshape(bl, 1, d)

    h1 = _norm_mod_fwd(x, w2['g_norm1'], sh1, sc1)
    h1f = h1.reshape(t, d)
    zr = _mm(h1f, w_rnn, "nn", name="in_rnn").reshape(bl, s, 2 * dr)
    q3 = _mm(h1f, w_q, "nn", name="in_q", out_dtype=BF16, scale=scale).reshape(bl, s, da)
    kv3 = _mm(h1f, w_kv, "nn", name="in_kv", out_dtype=BF16).reshape(bl, s, 2 * da)
    mg3 = _mm(h1f, w_mg, "nn", name="in_mg").reshape(bl, s, 2 * d)
    zf3 = _mm(h1f, w_fl, "nn", name="in_fl").reshape(bl, s, LANES)

    lru = (conv_full['w_rnn_conv'], w2['b_rnn_conv'], w2['w_lru_a'], w2['b_lru_a'], w2['w_lru_i'], w2['b_lru_i'], w2['lru_lambda'])
    y_rnn, *rnn_kept = _rnn_fwd(zr, *lru)

    f3 = _fgate_fwd(zf3, bf_pad)
    f_heads = f3[:, :, :nh].transpose(0, 2, 1).reshape(bl * nh, s)
    fcol = jnp.broadcast_to(f_heads[:, :, None], (bl * nh, s, LANES))
    frow = f_heads.reshape(bl * nh, 1, s)
    o3, lse_row = _attn_fwd(q3, kv3, fcol, frow, nh)

    pr3 = _mm(y_rnn.reshape(t, dr), w_proj_rnn_f, "nn", name="proj_rnn").reshape(bl, s, d)
    pa3 = _mm(o3.reshape(t, da), w_proj_attn_f, "nn", name="proj_attn").reshape(bl, s, d)
    merged = _merge_fwd(mg3, pr3, pa3)
    mo3 = _mm(merged.reshape(t, d), w_out_f, "nn", name="mix_out").reshape(bl, s, d)
    x1, h2 = _resid_norm_fwd(x, mo3, gt1, w2['g_norm2'], sh2, sc2)
    h2f = h2.reshape(t, d)
    up3 = _mm(h2f, gathered['w_ffn_up'], "nn", name="ffn_up", b_chunk=up_chunk, tn=up_chunk).reshape(bl, s, 2 * dff)
    act3 = _ffn_act_fwd(up3, conv_full['w_ffn_conv'], w2['b_ffn_conv'])
    yf3 = _mm(act3.reshape(t, dff), w_ffn_down_f, "nn", name="ffn_down").reshape(bl, s, d)

    dx2, dshf, dscf, dg_final, loss_part = _final_fwd_bwd(x1, yf3, gt2, w2['g_final'], shf, scf, loss_target)
    loss = lax.psum(loss_part[0, 0], ("x", "y", "c"))

    dyf, dgt2 = _gate_bwd(dx2, yf3, gt2, "ffn_gate_bwd")
    dyf_f = dyf.reshape(t, d)
    g_ffn_down = _mm(act3.reshape(t, dff), dyf_f, "tn", name="dw_ffn_down")
    dact3 = _mm(dyf_f, w_ffn_down_f, "nt", name="d_ffn_act").reshape(bl, s, dff)
    dgf, duf, g_ffn_conv, g_b_ffn_conv = _ffn_act_bwd(up3, dact3, conv_full['w_ffn_conv'], w2['b_ffn_conv'])
    dgf_f, duf_f = dgf.reshape(t, dff), duf.reshape(t, dff)
    g_ffn_up = jnp.concatenate([_mm(h2f, dgf_f, "tn", name="dw_ffn_up_gate", out_chunk=up_chunk),
                                _mm(h2f, duf_f, "tn", name="dw_ffn_up_value", out_chunk=up_chunk)], axis=0)
    dh2 = _mm([dgf_f, duf_f], gathered['w_ffn_up'], "nt", name="d_h2", b_chunk=up_chunk).reshape(bl, s, d)
    dx1, dsh2, dsc2, dg_norm2 = _norm_mod_bwd(dh2, x1, dx2, w2['g_norm2'], sc2, "norm2_bwd")

    dmo, dgt1 = _gate_bwd(dx1, mo3, gt1, "mix_gate_bwd")
    dmo_f = dmo.reshape(t, d)
    g_out = _mm(merged.reshape(t, d), dmo_f, "tn", name="dw_out")
    dm3 = _mm(dmo_f, w_out_f, "nt", name="d_merged").reshape(bl, s, d)
    dpr, dpa, dmr, dma = _merge_bwd(dm3, mg3, pr3, pa3)
    g_proj_rnn = _mm(y_rnn.reshape(t, dr), dpr.reshape(t, d), "tn", name="dw_proj_rnn")
    g_proj_attn = _mm(o3.reshape(t, da), dpa.reshape(t, d), "tn", name="dw_proj_attn")
    dyr3 = _mm(dpr.reshape(t, d), w_proj_rnn_f, "nt", name="d_y_rnn").reshape(bl, s, dr)
    do3 = _mm(dpa.reshape(t, d), w_proj_attn_f, "nt", name="d_y_attn").reshape(bl, s, da)

    dq3, dk3, dv3, dfk, dfq = _attn_bwd(q3, kv3, do3, o3, lse_row, fcol, frow, nh, scale)
    heads_last = lambda a: jnp.pad(a.reshape(bl, nh, s).transpose(0, 2, 1), ((0, 0), (0, 0), (0, LANES - nh)))
    dzf3, g_bf = _fgate_bwd(heads_last(dfk[:, :, 0]), heads_last(dfq), zf3, bf_pad)

    dxr, dgr, g_rnn_conv, g_b_rnn_conv, g_lru_a, g_b_lru_a, g_lru_i, g_b_lru_i, g_lam = _rnn_bwd(
        zr, rnn_kept, dyr3, lru[0], lru[2], lru[4], lru[6])

    dz = [a.reshape(t, -1) for a in (dxr, dgr, dq3, dk3, dv3, dmr, dma)]
    dzf_f = dzf3.reshape(t, LANES)
    seg_names = ("xr", "gr", "q", "k", "v", "mr", "ma")
    g_seg = [_mm(h1f, a, "tn", name="dw_in_" + n) for n, a in zip(seg_names, dz)]
    g_in_fl = _mm(h1f, dzf_f, "tn", name="dw_in_fl")[:, :nh]
    g_in = _chunks_of_cols(g_seg[:5] + [g_in_fl] + g_seg[5:], N_CHIPS)
    dh1 = _mm(dzf_f, w_fl, "nt", name="d_h1_fl")
    dh1 = _mm(dz[:2], w_rnn, "nt", name="d_h1_rnn", add=dh1)
    dh1 = _mm(dz[2:], w_rest, "nt", name="d_h1", add=dh1).reshape(bl, s, d)
    grad_x, dsh1, dsc1, dg_norm1 = _norm_mod_bwd(dh1, x, dx1, w2['g_norm1'], sc1, "norm1_bwd")

    dmods = [jnp.concatenate([dsh1, dsc1, dgt1, dsh2, dsc2, dgt2], axis=-1).reshape(bl, -1),
             jnp.concatenate([dshf, dscf], axis=-1).reshape(bl, -1)]
    dmods = [g.reshape(nd * bl, -1) for g in _all_gather_devices(dmods, "ag_dmod")]
    grad = {n: _mm(c_act, my_cols(g, n), "tn", name="dw_" + n) for n, g in zip(_ADA, dmods)}
    grad.update({b: _rowsum(g, "d" + b) for b, g in zip(_ADA_BIAS, dmods)})

    rowchunks = lambda g: g.reshape(N_CHIPS, g.shape[0] // N_CHIPS, g.shape[1])
    full = dict(w_in=g_in, w_proj_rnn=rowchunks(g_proj_rnn), w_proj_attn=rowchunks(g_proj_attn),
                w_out=rowchunks(g_out), w_ffn_up=g_ffn_up, w_ffn_down=rowchunks(g_ffn_down))
    small = dict(g_norm1=dg_norm1, w_rnn_conv=g_rnn_conv, b_rnn_conv=g_b_rnn_conv, w_lru_a=g_lru_a,
                 b_lru_a=g_b_lru_a, w_lru_i=g_lru_i, b_lru_i=g_b_lru_i, lru_lambda=g_lam, b_fgate=g_bf[:, :nh],
                 g_norm2=dg_norm2, w_ffn_conv=g_ffn_conv, b_ffn_conv=g_b_ffn_conv, g_final=dg_final)

    small_names = _REPLICATED + _CONV
    n_small = sum(small[n].size for n in small_names)
    rows_q = -(-n_small // (N_CHIPS * FLAT_COLS * 32 * RS_STEPS)) * 32 * RS_STEPS
    small_flat = _flat_pad([small[n] for n in small_names], N_CHIPS * rows_q * FLAT_COLS).reshape(N_CHIPS, rows_q, FLAT_COLS)
    reduced = _reduce_scatter_chips([full[n] for n in names] + [small_flat])
    grad.update(zip(names, reduced[:-1]))
    small_all = _all_gather_chips([reduced[-1]], "ag_small_grads")[0].reshape(-1)
    grad.update(zip(small_names, _split_flat(small_all, [small[n].shape for n in small_names])))
    for n in _CONV:
        n4 = w2[n].shape[1]
        grad[n] = lax.dynamic_slice_in_dim(grad[n], chip * n4, n4, axis=1)

    delta_w, new_m, new_v = {}, {}, {}
    for n in names + _ADA:
        delta_w[n], new_m[n], new_v[n] = _adamw(grad[n], args[n], args['m_' + n], args['v_' + n], "adamw_" + n)
    small_names = small_names + _ADA_BIAS
    rows_small = -(-sum(w2[n].size for n in small_names) // (FLAT_COLS * SUBLANES)) * SUBLANES
    flat_small = lambda src: _flat_pad([src[n] for n in small_names], rows_small * FLAT_COLS).reshape(rows_small, FLAT_COLS)
    small_out = _adamw(flat_small(grad), flat_small(w2), flat_small(m2), flat_small(v2), "adamw_small")
    for dst, flat in zip((delta_w, new_m, new_v), small_out):
        dst.update(zip(small_names, _split_flat(flat.reshape(-1), [w2[n].shape for n in small_names])))

    out = [loss, grad_x]
    for src in (grad, delta_w, new_m, new_v):
        out += [src[n].reshape(shape_of[n]) for n in _WEIGHTS]
    return tuple(out)
```

```python
import functools
import math

import jax
import jax.numpy as jnp
from jax import lax
from jax.experimental import pallas as pl
from jax.experimental.pallas import tpu as pltpu

F32 = jnp.float32
BF16 = jnp.bfloat16
MESH = pl.DeviceIdType.MESH

RMS_EPS = 1e-6
LRU_C = 8.0
ADAM_LR = 0.001
ADAM_B1 = 0.9
ADAM_B2 = 0.999
ADAM_EPS = 1e-08
ADAM_WD = 0.01
ADAM_STEP = 10

LANES = 128
SUBLANES = 8
N_CHIPS = 4
FLAT_COLS = 1024
SCAN_SEGMENTS = 2 * SUBLANES
VMEM_LIMIT = 48 * 1024 * 1024
NEG_BIG = -1e30


def _cp(n_axes):
    return pltpu.CompilerParams(dimension_semantics=("arbitrary",) * n_axes, vmem_limit_bytes=VMEM_LIMIT)


def _tile(n, target, align):
    if n <= target:
        return n
    t = (target // align) * align
    while t >= align:
        if n % t == 0:
            return t
        t -= align
    return n


def _nice_rows(n, align):
    r = -(-n // align) * align
    while True:
        if r <= 640:
            return r, r
        t = _tile(r, 640, align)
        if 128 <= t <= 640:
            return r, t
        r += align


def _sigmoid(x):
    return 0.5 * jnp.tanh(0.5 * x) + 0.5


def _softplus(x):
    return jnp.maximum(x, 0.0) + jnp.log1p(jnp.exp(-jnp.abs(x)))


def _expm1(x, exp_x):
    small = x * (1.0 + 0.5 * x * (1.0 + (1.0 / 3.0) * x * (1.0 + 0.25 * x)))
    return jnp.where(jnp.abs(x) < 0.05, small, exp_x - 1.0)


_GELU_K = math.sqrt(2.0 / math.pi)
_GELU_C = 0.044715


def _gelu(x):
    t = jnp.tanh(_GELU_K * (x + _GELU_C * x * x * x))
    return 0.5 * x * (1.0 + t)


def _gelu_and_grad(x):
    t = jnp.tanh(_GELU_K * (x + _GELU_C * x * x * x))
    g = 0.5 * x * (1.0 + t)
    dg = 0.5 * (1.0 + t) + 0.5 * x * (1.0 - t * t) * _GELU_K * (1.0 + 3.0 * _GELU_C * x * x)
    return g, dg


def _shift_down(x, k):
    if k == 0:
        return x
    y = pltpu.roll(x, k, 0)
    rows = lax.broadcasted_iota(jnp.int32, (SUBLANES, x.shape[1]), 0)
    return jnp.concatenate([jnp.where(rows >= k, y[:SUBLANES], 0.0), y[SUBLANES:]], axis=0)


def _shift_up(x, k):
    if k == 0:
        return x
    s = x.shape[0]
    y = pltpu.roll(x, s - k, 0)
    rows = lax.broadcasted_iota(jnp.int32, (SUBLANES, x.shape[1]), 0)
    return jnp.concatenate([y[:s - SUBLANES], jnp.where(rows < SUBLANES - k, y[s - SUBLANES:], 0.0)], axis=0)


def _dot(a, b, dims):
    return lax.dot_general(a.astype(BF16), b.astype(BF16), (dims, ((), ())), preferred_element_type=F32)


_NN = ((1,), (0,))
_NT = ((1,), (1,))
_TN = ((0,), (0,))


def _mm(a, b, mode, *, name, out_dtype=F32, scale=None, bias=None, add=None, tm=1024, tn=1024, tk=1024,
        b_chunk=None, out_chunk=None):
    pieces = list(a) if isinstance(a, (list, tuple)) else [a]
    ksize = lambda p: p.shape[0] if mode == "tn" else p.shape[1]
    if b_chunk is None:
        brows, bcols = b.shape
    else:
        brows, bcols = b.shape[1], b.shape[0] * b_chunk
    k = sum(ksize(p) for p in pieces)
    if mode == "nt":
        m, n = pieces[0].shape[0], brows
        assert bcols == k, (bcols, k)
    else:
        m, n = (pieces[0].shape[1] if mode == "tn" else pieces[0].shape[0]), bcols
        assert brows == k, (brows, k)
    tm = _tile(m, tm, LANES)
    ncut = n
    if b_chunk is not None and mode != "nt":
        ncut = b_chunk
    if out_chunk is not None:
        ncut = math.gcd(ncut, out_chunk)
    tn = _tile(ncut, tn, LANES)
    kcut = b_chunk if (b_chunk is not None and mode == "nt") else k
    for p in pieces:
        kcut = math.gcd(kcut, ksize(p))
    tk = _tile(kcut, tk, LANES)
    nk = k // tk
    dims = {"nn": _NN, "nt": _NT, "tn": _TN}[mode]
    counts = [ksize(p) // tk for p in pieces]
    starts = [sum(counts[:i]) for i in range(len(pieces))]
    n_pieces = len(pieces)

    def a_spec(s0, cnt):
        kmap = (lambda kk: kk) if n_pieces == 1 else (lambda kk: jnp.clip(kk - s0, 0, cnt - 1))
        if mode == "tn":
            return pl.BlockSpec((tk, tm), lambda i, j, kk: (kmap(kk), i))
        return pl.BlockSpec((tm, tk), lambda i, j, kk: (i, kmap(kk)))

    if b_chunk is None:
        if mode == "nt":
            b_spec = pl.BlockSpec((tn, tk), lambda i, j, kk: (j, kk))
        else:
            b_spec = pl.BlockSpec((tk, tn), lambda i, j, kk: (kk, j))
    elif mode == "nt":
        per_b = b_chunk // tk
        b_spec = pl.BlockSpec((None, tn, tk), lambda i, j, kk: (kk // per_b, j, kk % per_b))
    else:
        per_b = b_chunk // tn
        b_spec = pl.BlockSpec((None, tk, tn), lambda i, j, kk: (j // per_b, kk, j % per_b))
    if out_chunk is None:
        out_spec = pl.BlockSpec((tm, tn), lambda i, j, kk: (i, j))
        out_shape = jax.ShapeDtypeStruct((m, n), out_dtype)
    else:
        per_o = out_chunk // tn
        out_spec = pl.BlockSpec((None, tm, tn), lambda i, j, kk: (j // per_o, i, j % per_o))
        out_shape = jax.ShapeDtypeStruct((n // out_chunk, m, out_chunk), out_dtype)
    in_specs = [a_spec(s0, cnt) for s0, cnt in zip(starts, counts)] + [b_spec]
    args = pieces + [b]
    if bias is not None:
        in_specs.append(pl.BlockSpec((1, tn), lambda i, j, kk: (0, j)))
        args.append(bias)
    if add is not None:
        in_specs.append(pl.BlockSpec((tm, tn), lambda i, j, kk: (i, j)))
        args.append(add)

    def kern(*refs):
        b_ref = refs[n_pieces]
        o_ref = refs[n_pieces + 1 + (bias is not None) + (add is not None)]

        def finish(r):
            if scale is not None:
                r = r * scale
            pos = n_pieces + 1
            if bias is not None:
                r = r + refs[pos][...]
                pos += 1
            if add is not None:
                r = r + refs[pos][...]
            o_ref[...] = r.astype(out_dtype)

        if nk == 1:
            finish(_dot(refs[0][...], b_ref[...], dims))
            return
        acc = refs[-1]
        kk = pl.program_id(2)

        @pl.when(kk == 0)
        def _():
            acc[...] = jnp.zeros_like(acc)

        if n_pieces == 1:
            acc[...] += _dot(refs[0][...], b_ref[...], dims)
        else:
            for idx in range(n_pieces):
                @pl.when((kk >= starts[idx]) & (kk < starts[idx] + counts[idx]))
                def _(idx=idx):
                    acc[...] += _dot(refs[idx][...], b_ref[...], dims)

        @pl.when(kk == nk - 1)
        def _():
            finish(acc[...])

    return pl.pallas_call(
        kern, name=name,
        grid=(m // tm, n // tn, nk),
        in_specs=in_specs, out_specs=out_spec, out_shape=out_shape,
        scratch_shapes=[pltpu.VMEM((tm, tn), F32)] if nk > 1 else [],
        compiler_params=_cp(3),
    )(*args)


def _silu_pad(c, rows):
    bl, d = c.shape

    def kern(c_ref, o_ref):
        o_ref[...] = jnp.zeros_like(o_ref)
        v = c_ref[...]
        o_ref[0:bl, :] = v * _sigmoid(v)

    return pl.pallas_call(kern, name="silu_pad", out_shape=jax.ShapeDtypeStruct((rows, d), F32))(c)


def _rowsum(x, name):
    r, n = x.shape

    def kern(x_ref, o_ref):
        o_ref[...] = jnp.sum(x_ref[...], axis=0, keepdims=True)

    return pl.pallas_call(kern, name=name, out_shape=jax.ShapeDtypeStruct((1, n), F32))(x)


def _norm_parts(x, g):
    r = lax.rsqrt(jnp.mean(x * x, axis=-1, keepdims=True) + RMS_EPS)
    xh = x * r
    return r, xh, xh * g


def _norm_bwd_parts(dh, xh, r, g, sc):
    n = xh * g
    dn = dh * (1.0 + sc)
    dxh = dn * g
    dx = r * (dxh - xh * jnp.mean(dxh * xh, axis=-1, keepdims=True))
    return dx, dh, dh * n, dn * xh


def _act_specs(ts, d, n):
    return [pl.BlockSpec((1, ts, d), lambda b, t: (b, t, 0)) for _ in range(n)]


def _vec_spec(d):
    return pl.BlockSpec((1, 1, d), lambda b, t: (b, 0, 0))


def _par_spec(d):
    return pl.BlockSpec((1, d), lambda b, t: (0, 0))


def _norm_mod_fwd(x3, g, sh, sc):
    bl, s, d = x3.shape
    ts = _tile(s, 512, SUBLANES)

    def kern(x_ref, g_ref, sh_ref, sc_ref, h_ref):
        _, _, n = _norm_parts(x_ref[0], g_ref[...])
        h_ref[0] = (n * (1.0 + sc_ref[0]) + sh_ref[0]).astype(BF16)

    return pl.pallas_call(
        kern, name="norm_mod_fwd", grid=(bl, s // ts),
        in_specs=_act_specs(ts, d, 1) + [_par_spec(d), _vec_spec(d), _vec_spec(d)],
        out_specs=_act_specs(ts, d, 1)[0],
        out_shape=jax.ShapeDtypeStruct((bl, s, d), BF16),
        compiler_params=_cp(2),
    )(x3, g, sh, sc)


def _resid_norm_fwd(x3, y3, gate, g, sh, sc):
    bl, s, d = x3.shape
    ts = _tile(s, 512, SUBLANES)

    def kern(x_ref, y_ref, gate_ref, g_ref, sh_ref, sc_ref, x1_ref, h_ref):
        x1 = x_ref[0] + gate_ref[0] * y_ref[0]
        x1_ref[0] = x1
        _, _, n = _norm_parts(x1, g_ref[...])
        h_ref[0] = (n * (1.0 + sc_ref[0]) + sh_ref[0]).astype(BF16)

    return pl.pallas_call(
        kern, name="resid_norm_fwd", grid=(bl, s // ts),
        in_specs=_act_specs(ts, d, 2) + [_vec_spec(d), _par_spec(d), _vec_spec(d), _vec_spec(d)],
        out_specs=_act_specs(ts, d, 2),
        out_shape=[jax.ShapeDtypeStruct((bl, s, d), F32), jax.ShapeDtypeStruct((bl, s, d), BF16)],
        compiler_params=_cp(2),
    )(x3, y3, gate, g, sh, sc)


def _norm_mod_bwd(dh3, x3, dres3, g, sc, name, gated=None):
    bl, s, d = x3.shape
    ts = _tile(s, 512, SUBLANES)

    def kern(dh_ref, x_ref, dres_ref, g_ref, sc_ref, *rest):
        dx_ref, dsh_ref, dsc_ref, dg_ref = rest[-4:]
        b, t = pl.program_id(0), pl.program_id(1)
        gv = g_ref[...]
        r, xh, _ = _norm_parts(x_ref[0], gv)
        dx, a, bb, cc = _norm_bwd_parts(dh_ref[0], xh, r, gv, sc_ref[0])
        dx = dres_ref[0] + dx
        dx_ref[0] = dx

        @pl.when(t == 0)
        def _():
            dsh_ref[...] = jnp.zeros_like(dsh_ref)
            dsc_ref[...] = jnp.zeros_like(dsc_ref)

        @pl.when((t == 0) & (b == 0))
        def _():
            dg_ref[...] = jnp.zeros_like(dg_ref)

        dsh_ref[0] += jnp.sum(a, axis=0, keepdims=True)
        dsc_ref[0] += jnp.sum(bb, axis=0, keepdims=True)
        dg_ref[...] += jnp.sum(cc, axis=0, keepdims=True)
        if gated is not None:
            y_ref, gate_ref, dy_ref, dgate_ref = rest[:4]
            dy_ref[0] = (gate_ref[0] * dx).astype(BF16)

            @pl.when(t == 0)
            def _():
                dgate_ref[...] = jnp.zeros_like(dgate_ref)

            dgate_ref[0] += jnp.sum(dx * y_ref[0], axis=0, keepdims=True)

    extra_in = [] if gated is None else _act_specs(ts, d, 1) + [_vec_spec(d)]
    extra_out = [] if gated is None else [_act_specs(ts, d, 1)[0], _vec_spec(d)]
    extra_shape = [] if gated is None else [jax.ShapeDtypeStruct((bl, s, d), BF16), jax.ShapeDtypeStruct((bl, 1, d), F32)]
    return pl.pallas_call(
        kern, name=name, grid=(bl, s // ts),
        in_specs=_act_specs(ts, d, 3) + [_par_spec(d), _vec_spec(d)] + extra_in,
        out_specs=extra_out + [_act_specs(ts, d, 1)[0], _vec_spec(d), _vec_spec(d), _par_spec(d)],
        out_shape=extra_shape + [jax.ShapeDtypeStruct((bl, s, d), F32), jax.ShapeDtypeStruct((bl, 1, d), F32),
                                 jax.ShapeDtypeStruct((bl, 1, d), F32), jax.ShapeDtypeStruct((1, d), F32)],
        compiler_params=_cp(2),
    )(dh3, x3, dres3, g, sc, *(gated or ()))


def _final_fwd_bwd(x1, yf, gate2, g, shf, scf, tgt):
    bl, s, d = x1.shape
    ts = _tile(s, 512, SUBLANES)

    def kern(x1_ref, yf_ref, gate_ref, g_ref, sh_ref, sc_ref, tgt_ref,
             dx_ref, dyf_ref, dgate_ref, dsh_ref, dsc_ref, dg_ref, loss_ref):
        b, t = pl.program_id(0), pl.program_id(1)
        gv, sc, yf, gate = g_ref[...], sc_ref[0], yf_ref[0], gate_ref[0]
        x2 = x1_ref[0] + gate * yf
        r, xh, n = _norm_parts(x2, gv)
        err = n * (1.0 + sc) + sh_ref[0] - tgt_ref[0]
        dx, a, bb, cc = _norm_bwd_parts(err * (1.0 / d), xh, r, gv, sc)
        dx_ref[0] = dx
        dyf_ref[0] = (gate * dx).astype(BF16)

        @pl.when(t == 0)
        def _():
            dsh_ref[...] = jnp.zeros_like(dsh_ref)
            dsc_ref[...] = jnp.zeros_like(dsc_ref)
            dgate_ref[...] = jnp.zeros_like(dgate_ref)

        @pl.when((t == 0) & (b == 0))
        def _():
            dg_ref[...] = jnp.zeros_like(dg_ref)
            loss_ref[...] = jnp.zeros_like(loss_ref)

        dsh_ref[0] += jnp.sum(a, axis=0, keepdims=True)
        dsc_ref[0] += jnp.sum(bb, axis=0, keepdims=True)
        dg_ref[...] += jnp.sum(cc, axis=0, keepdims=True)
        dgate_ref[0] += jnp.sum(dx * yf, axis=0, keepdims=True)
        tok = jnp.mean(err * err, axis=-1, keepdims=True)
        loss_ref[...] += 0.5 * jnp.sum(tok, axis=0, keepdims=True)

    vec = jax.ShapeDtypeStruct((bl, 1, d), F32)
    return pl.pallas_call(
        kern, name="final_fwd_bwd", grid=(bl, s // ts),
        in_specs=_act_specs(ts, d, 2) + [_vec_spec(d), _par_spec(d), _vec_spec(d), _vec_spec(d)] + _act_specs(ts, d, 1),
        out_specs=_act_specs(ts, d, 2) + [_vec_spec(d), _vec_spec(d), _vec_spec(d), _par_spec(d),
                                          pl.BlockSpec((1, 1), lambda b, t: (0, 0))],
        out_shape=[jax.ShapeDtypeStruct((bl, s, d), F32), jax.ShapeDtypeStruct((bl, s, d), BF16), vec, vec, vec,
                   jax.ShapeDtypeStruct((1, d), F32), jax.ShapeDtypeStruct((1, 1), F32)],
        compiler_params=_cp(2),
    )(x1, yf, gate2, g, shf, scf, tgt)


def _rnn_gates(xr, cw, cb, wa, ba, wi, bi, lam):
    kw = cw.shape[0]
    xc = cb
    for k in range(kw):
        xc = xc + _shift_down(xr, kw - 1 - k) * cw[k:k + 1, :]
    r = _sigmoid(_dot(xc, wa, _NN) + ba)
    i = _sigmoid(_dot(xc, wi, _NN) + bi)
    sp = _softplus(-lam)
    log_a = -LRU_C * r * sp
    a = jnp.exp(log_a)
    mult = jnp.sqrt(-_expm1(2.0 * log_a, a * a))
    return xc, r, i, sp, a, mult


def _segment_scan(a_s, u_s, h_s, p_s, reverse):
    s, c = a_s.shape
    seg = s // SCAN_SEGMENTS

    unroll = math.gcd(seg, 8)

    def steps(n, carry):
        h, p = carry
        for j in range(unroll):
            t = n * unroll + j
            t = (seg - 1 - t) if reverse else t
            av = a_s[pl.ds(t, SCAN_SEGMENTS, stride=seg), :]
            uv = u_s[pl.ds(t, SCAN_SEGMENTS, stride=seg), :]
            h = av * h + uv
            p = p * av
            h_s[pl.ds(t, SCAN_SEGMENTS, stride=seg), :] = h
            p_s[pl.ds(t, SCAN_SEGMENTS, stride=seg), :] = p
        return h, p

    lax.fori_loop(0, seg // unroll, steps, (jnp.zeros((SCAN_SEGMENTS, c), F32), jnp.ones((SCAN_SEGMENTS, c), F32)))
    carry = jnp.zeros((1, c), F32)
    order = range(SCAN_SEGMENTS - 1, -1, -1) if reverse else range(SCAN_SEGMENTS)
    for j in order:
        rows = pl.ds(j * seg, seg)
        fixed = h_s[rows, :] + p_s[rows, :] * carry
        h_s[rows, :] = fixed
        carry = fixed[0:1, :] if reverse else fixed[seg - 1:seg, :]


def _rnn_specs(s, rb, nb):
    act = lambda off: pl.BlockSpec((1, s, rb), lambda b, n, off=off: (b, 0, off + n))
    par = pl.BlockSpec((1, rb), lambda b, n: (0, n))
    wsp = pl.BlockSpec((1, rb, rb), lambda b, n: (n, 0, 0))
    return act, par, wsp


def _rnn_fwd(zr3, cw, cb, wa, ba, wi, bi, lam):
    bl, s, two = zr3.shape
    nb, rb, _ = wa.shape
    dr = nb * rb
    kw = cw.shape[0]
    act, par, wsp = _rnn_specs(s, rb, nb)

    def kern(xr_ref, gr_ref, cw_ref, cb_ref, wa_ref, ba_ref, wi_ref, bi_ref, lam_ref,
             y_ref, h_ref, xc_ref, r_ref, i_ref, a_ref, mult_ref, a_s, u_s, h_s, p_s):
        xc, r, i, sp, a, mult = _rnn_gates(xr_ref[0], cw_ref[...], cb_ref[...], wa_ref[0], ba_ref[...],
                                           wi_ref[0], bi_ref[...], lam_ref[...])
        for ref, val in ((xc_ref, xc), (r_ref, r), (i_ref, i), (a_ref, a), (mult_ref, mult)):
            ref[0] = val
        a_s[...] = a
        u_s[...] = mult * (i * xc)
        _segment_scan(a_s, u_s, h_s, p_s, reverse=False)
        h = h_s[...]
        h_ref[0] = h
        y_ref[0] = (_gelu(gr_ref[0]) * h).astype(BF16)

    kept = jax.ShapeDtypeStruct((bl, s, dr), F32)
    return pl.pallas_call(
        kern, name="rnn_fwd", grid=(bl, nb),
        in_specs=[act(0), act(nb), pl.BlockSpec((kw, rb), lambda b, n: (0, n)), par, wsp, par, wsp, par, par],
        out_specs=[act(0)] * 7,
        out_shape=[jax.ShapeDtypeStruct((bl, s, dr), BF16)] + [kept] * 6,
        scratch_shapes=[pltpu.VMEM((s, rb), F32)] * 4,
        compiler_params=_cp(2),
    )(zr3, zr3, cw, cb, wa, ba, wi, bi, lam)


def _rnn_bwd(zr3, kept, dy3, cw, wa, wi, lam):
    bl, s, _ = zr3.shape
    nb, rb, _ = wa.shape
    dr = nb * rb
    kw = cw.shape[0]
    act = lambda off: pl.BlockSpec((1, s, rb), lambda n, b, off=off: (b, 0, off + n))
    par = pl.BlockSpec((1, rb), lambda n, b: (0, n))
    wsp = pl.BlockSpec((1, rb, rb), lambda n, b: (n, 0, 0))
    cws = pl.BlockSpec((kw, rb), lambda n, b: (0, n))

    def kern(xr_ref, gr_ref, h_ref, xc_ref, r_ref, i_ref, a_ref, mult_ref, dy_ref, cw_ref, wa_ref, wi_ref, lam_ref,
             dxr_ref, dgr_ref, dcw_ref, dcb_ref, dwa_ref, dba_ref, dwi_ref, dbi_ref, dlam_ref, a_s, u_s, h_s, p_s):
        b = pl.program_id(1)
        xr, cwv, lamv = xr_ref[0], cw_ref[...], lam_ref[...]
        wav, wiv = wa_ref[0], wi_ref[0]
        xc, r, i, a, mult = xc_ref[0], r_ref[0], i_ref[0], a_ref[0], mult_ref[0]
        sp = _softplus(-lamv)
        h, dy = h_ref[0], dy_ref[0]
        ge, dge = _gelu_and_grad(gr_ref[0])
        dgr_ref[0] = (dy * h * dge).astype(BF16)
        a_s[...] = _shift_up(a, 1)
        u_s[...] = dy * ge
        _segment_scan(a_s, u_s, h_s, p_s, reverse=True)
        g = h_s[...]
        da = g * _shift_down(h, 1)
        ix = i * xc
        dlog_a = da * a + (g * ix) * (-(a * a) / mult)
        di = g * mult * xc
        dpa = (dlog_a * (-LRU_C * sp)) * r * (1.0 - r)
        dpi = di * i * (1.0 - i)
        dxc = g * mult * i + _dot(dpa, wav, _NT) + _dot(dpi, wiv, _NT)
        dxr = jnp.zeros_like(dxc)
        dcw_rows = []
        for k in range(kw):
            dxr = dxr + _shift_up(dxc, kw - 1 - k) * cwv[k:k + 1, :]
            dcw_rows.append(jnp.sum(dxc * _shift_down(xr, kw - 1 - k), axis=0, keepdims=True))
        dxr_ref[0] = dxr.astype(BF16)

        @pl.when(b == 0)
        def _():
            for ref in (dcw_ref, dcb_ref, dwa_ref, dba_ref, dwi_ref, dbi_ref, dlam_ref):
                ref[...] = jnp.zeros_like(ref)

        for k in range(kw):
            dcw_ref[k:k + 1, :] += dcw_rows[k]
        dcb_ref[...] += jnp.sum(dxc, axis=0, keepdims=True)
        dwa_ref[0] += _dot(xc, dpa, _TN)
        dwi_ref[0] += _dot(xc, dpi, _TN)
        dba_ref[...] += jnp.sum(dpa, axis=0, keepdims=True)
        dbi_ref[...] += jnp.sum(dpi, axis=0, keepdims=True)
        dsp = jnp.sum(dlog_a * (-LRU_C * r), axis=0, keepdims=True)
        dlam_ref[...] += dsp * (-_sigmoid(-lamv))

    vec = jax.ShapeDtypeStruct((1, dr), F32)
    wsh = jax.ShapeDtypeStruct((nb, rb, rb), F32)
    return pl.pallas_call(
        kern, name="rnn_bwd", grid=(nb, bl),
        in_specs=[act(0), act(nb)] + [act(0)] * 7 + [cws, wsp, wsp, par],
        out_specs=[act(0), act(0), cws, par, wsp, par, wsp, par, par],
        out_shape=[jax.ShapeDtypeStruct((bl, s, dr), BF16), jax.ShapeDtypeStruct((bl, s, dr), BF16),
                   jax.ShapeDtypeStruct((kw, dr), F32), vec, wsh, vec, wsh, vec, vec],
        scratch_shapes=[pltpu.VMEM((s, rb), F32)] * 4,
        compiler_params=_cp(2),
    )(zr3, zr3, *kept, dy3, cw, wa, wi, lam)


def _tri(n, upper):
    r = lax.broadcasted_iota(jnp.int32, (n, n), 0)
    c = lax.broadcasted_iota(jnp.int32, (n, n), 1)
    return jnp.where((c >= r) if upper else (c <= r), 1.0, 0.0).astype(F32)


def _fgate_fwd(zf3, bf):
    bl, s, w = zf3.shape
    ch = _tile(s, 256, SUBLANES)

    def kern(z_ref, b_ref, f_ref):
        tri = _tri(ch, upper=False)
        carry = jnp.zeros((1, w), F32)
        for j in range(s // ch):
            rows = pl.ds(j * ch, ch)
            lf = -_softplus(-(z_ref[0, rows, :] + b_ref[...]))
            out = jnp.dot(tri, lf, precision=lax.Precision.HIGHEST, preferred_element_type=F32) + carry
            f_ref[0, rows, :] = out
            carry = out[ch - 1:ch, :]

    return pl.pallas_call(
        kern, name="fgate_fwd", grid=(bl,),
        in_specs=[pl.BlockSpec((1, s, w), lambda b: (b, 0, 0)), pl.BlockSpec((1, w), lambda b: (0, 0))],
        out_specs=pl.BlockSpec((1, s, w), lambda b: (b, 0, 0)),
        out_shape=jax.ShapeDtypeStruct((bl, s, w), F32),
        compiler_params=_cp(1),
    )(zf3, bf)


def _fgate_bwd(dfk3, dfq3, zf3, bf):
    bl, s, w = zf3.shape
    ch = _tile(s, 256, SUBLANES)

    def kern(dfk_ref, dfq_ref, z_ref, b_ref, dz_ref, db_ref):
        b = pl.program_id(0)
        tri = _tri(ch, upper=True)
        carry = jnp.zeros((1, w), F32)
        dbsum = jnp.zeros((1, w), F32)
        for j in range(s // ch - 1, -1, -1):
            rows = pl.ds(j * ch, ch)
            df = dfk_ref[0, rows, :] + dfq_ref[0, rows, :]
            dlf = jnp.dot(tri, df, precision=lax.Precision.HIGHEST, preferred_element_type=F32) + carry
            carry = dlf[0:1, :]
            dz = dlf * _sigmoid(-(z_ref[0, rows, :] + b_ref[...]))
            dz_ref[0, rows, :] = dz.astype(BF16)
            dbsum = dbsum + jnp.sum(dz, axis=0, keepdims=True)

        @pl.when(b == 0)
        def _():
            db_ref[...] = jnp.zeros_like(db_ref)

        db_ref[...] += dbsum

    return pl.pallas_call(
        kern, name="fgate_bwd", grid=(bl,),
        in_specs=[pl.BlockSpec((1, s, w), lambda b: (b, 0, 0))] * 3 + [pl.BlockSpec((1, w), lambda b: (0, 0))],
        out_specs=[pl.BlockSpec((1, s, w), lambda b: (b, 0, 0)), pl.BlockSpec((1, w), lambda b: (0, 0))],
        out_shape=[jax.ShapeDtypeStruct((bl, s, w), BF16), jax.ShapeDtypeStruct((1, w), F32)],
        compiler_params=_cp(1),
    )(dfk3, dfq3, zf3, bf)


def _lanes(col, width):
    return col if width == LANES else jnp.concatenate([col] * (width // LANES), axis=1)


def _causal(sc, row0, col0, transposed):
    r = lax.broadcasted_iota(jnp.int32, sc.shape, 0) + row0
    c = lax.broadcasted_iota(jnp.int32, sc.shape, 1) + col0
    return jnp.where((c >= r) if transposed else (r >= c), sc, NEG_BIG)


def _attn_fwd(q3, kv3, fcol, frow, nh):
    bl, s, da = q3.shape
    dh = da // nh
    tq = _tile(s, 512, LANES)
    nq = s // tq

    def kern(iq_tab, ik_tab, q_ref, k_ref, v_ref, fk_ref, fq_ref, o_ref, lse_ref, m_s, l_s, acc):
        iq, ik = iq_tab[pl.program_id(2)], ik_tab[pl.program_id(2)]

        @pl.when(ik == 0)
        def _():
            m_s[...] = jnp.full_like(m_s, NEG_BIG)
            l_s[...] = jnp.zeros_like(l_s)
            acc[...] = jnp.zeros_like(acc)

        def block(masked):
            st = _dot(k_ref[0], q_ref[0], _NT) - _lanes(fk_ref[0], tq) + fq_ref[0]
            if masked:
                st = _causal(st, ik * tq, iq * tq, True)
            m_old = m_s[...]
            m_new = jnp.maximum(m_old, jnp.max(st, axis=0, keepdims=True))
            alpha = jnp.exp(m_old - m_new)
            pt = jnp.exp(st - m_new)
            l_s[...] = alpha * l_s[...] + jnp.sum(pt, axis=0, keepdims=True)
            acc[...] = alpha * acc[...] + _dot(v_ref[0], pt, _TN)
            m_s[...] = m_new

        pl.when(ik < iq)(functools.partial(block, False))

        @pl.when(ik == iq)
        def _():
            block(True)
            l = l_s[...]
            o_ref[0] = (acc[...] / l).T
            lse_ref[0] = m_s[...] + jnp.log(l)

    pairs = [(i, j) for i in range(nq) for j in range(i + 1)]
    iq_tab, ik_tab = (jnp.asarray(col, jnp.int32) for col in zip(*pairs))
    qmap = lambda b, h, p, iqt, ikt: (b, iqt[p], h)
    kmap = lambda off: (lambda b, h, p, iqt, ikt: (b, ikt[p], off + h))
    return pl.pallas_call(
        kern, name="attn_fwd",
        grid_spec=pltpu.PrefetchScalarGridSpec(
            num_scalar_prefetch=2, grid=(bl, nh, len(pairs)),
            in_specs=[pl.BlockSpec((1, tq, dh), qmap), pl.BlockSpec((1, tq, dh), kmap(0)), pl.BlockSpec((1, tq, dh), kmap(nh)),
                      pl.BlockSpec((1, tq, LANES), lambda b, h, p, iqt, ikt: (b * nh + h, ikt[p], 0)),
                      pl.BlockSpec((1, 1, tq), lambda b, h, p, iqt, ikt: (b * nh + h, 0, iqt[p]))],
            out_specs=[pl.BlockSpec((1, tq, dh), qmap),
                       pl.BlockSpec((1, 1, tq), lambda b, h, p, iqt, ikt: (b * nh + h, 0, iqt[p]))],
            scratch_shapes=[pltpu.VMEM((1, tq), F32), pltpu.VMEM((1, tq), F32), pltpu.VMEM((dh, tq), F32)]),
        out_shape=[jax.ShapeDtypeStruct((bl, s, da), F32), jax.ShapeDtypeStruct((bl * nh, 1, s), F32)],
        compiler_params=_cp(3),
    )(iq_tab, ik_tab, q3, kv3, kv3, fcol, frow)


def _attn_bwd(q3, kv3, do3, o3, lse_row, fcol, frow, nh, scale):
    bl, s, da = q3.shape
    dh = da // nh
    tk = _tile(s, 512, LANES)
    nk = s // tk

    pairs = [(j, i) for j in range(nk) for i in range(j, nk)]

    def kern(ik_tab, iq_tab, q_ref, k_ref, v_ref, do_ref, o_ref, lse_ref, fk_ref, fq_ref, dq_ref, dk_ref, dv_ref, dfk_ref,
             dfq_ref, dq_acc, dk_acc, dv_acc, dfq_acc, delta_s):
        step = pl.program_id(2)
        ik, iq = ik_tab[step], iq_tab[step]
        qrow = pl.ds(iq, 1)

        @pl.when(step == 0)
        def _():
            dq_acc[...] = jnp.zeros_like(dq_acc)
            dfq_acc[...] = jnp.zeros_like(dfq_acc)

        @pl.when(ik == 0)
        def _():
            prod = do_ref[0] * o_ref[0]
            rows = lax.dot_general(jnp.ones((SUBLANES, dh), F32), prod, (_NT, ((), ())),
                                   precision=lax.Precision.HIGHEST, preferred_element_type=F32)
            delta_s[qrow, :] = rows[0:1, :]

        @pl.when(iq == ik)
        def _():
            dk_acc[...] = jnp.zeros_like(dk_acc)
            dv_acc[...] = jnp.zeros_like(dv_acc)

        def block(masked):
            q = q_ref[0]
            st = _dot(k_ref[0], q, _NT) - _lanes(fk_ref[0], tk) + fq_ref[0]
            if masked:
                st = _causal(st, ik * tk, iq * tk, True)
            pt = jnp.exp(st - lse_ref[0])
            dv_acc[...] += _dot(pt, do_ref[0], _NN)
            dpt = _dot(v_ref[0], do_ref[0], _NT)
            dst = (pt * (dpt - delta_s[qrow, :])).astype(BF16)
            q_ones = jnp.concatenate([q, jnp.ones_like(q)], axis=1)
            dk_acc[...] += _dot(dst, q_ones, _NN)
            qrows = pl.ds(pl.multiple_of(iq * tk, tk), tk)
            dq_acc[qrows, :] += _dot(dst, k_ref[0], _TN)
            dfq_acc[qrow, :] += jnp.sum(dst.astype(F32), axis=0, keepdims=True)

        pl.when(iq > ik)(functools.partial(block, False))
        pl.when(iq == ik)(functools.partial(block, True))

        @pl.when(iq == nk - 1)
        def _():
            ext = dk_acc[...]
            dk_ref[0] = ext[:, :dh].astype(BF16)
            dfk_ref[0] = -ext[:, dh:]
            dv_ref[0] = dv_acc[...].astype(BF16)

        @pl.when(step == len(pairs) - 1)
        def _():
            dq_ref[0] = (dq_acc[...] * scale).astype(BF16)
            dfq_ref[0] = dfq_acc[...]

    ik_tab, iq_tab = (jnp.asarray(col, jnp.int32) for col in zip(*pairs))
    qmap = lambda b, h, p, ikt, iqt: (b, iqt[p], h)
    omap = lambda b, h, p, ikt, iqt: (b, jnp.where(ikt[p] == 0, iqt[p], 0), h)
    rmap = lambda b, h, p, ikt, iqt: (b * nh + h, 0, iqt[p])
    kmap = lambda off: (lambda b, h, p, ikt, iqt: (b, ikt[p], off + h))
    bmap = lambda b, h, p, ikt, iqt: (b * nh + h, ikt[p], 0)
    return pl.pallas_call(
        kern, name="attn_bwd",
        grid_spec=pltpu.PrefetchScalarGridSpec(
            num_scalar_prefetch=2, grid=(bl, nh, len(pairs)),
            in_specs=[pl.BlockSpec((1, tk, dh), qmap), pl.BlockSpec((1, tk, dh), kmap(0)), pl.BlockSpec((1, tk, dh), kmap(nh)),
                      pl.BlockSpec((1, tk, dh), qmap), pl.BlockSpec((1, tk, dh), omap), pl.BlockSpec((1, 1, tk), rmap),
                      pl.BlockSpec((1, tk, LANES), bmap), pl.BlockSpec((1, 1, tk), rmap)],
            out_specs=[pl.BlockSpec((1, s, dh), lambda b, h, p, ikt, iqt: (b, 0, h)),
                       pl.BlockSpec((1, tk, dh), kmap(0)), pl.BlockSpec((1, tk, dh), kmap(0)),
                       pl.BlockSpec((1, tk, LANES), bmap),
                       pl.BlockSpec((1, nk, tk), lambda b, h, p, ikt, iqt: (b * nh + h, 0, 0))],
            scratch_shapes=[pltpu.VMEM((s, dh), F32), pltpu.VMEM((tk, 2 * dh), F32), pltpu.VMEM((tk, dh), F32),
                            pltpu.VMEM((nk, tk), F32), pltpu.VMEM((nk, tk), F32)]),
        out_shape=[jax.ShapeDtypeStruct((bl, s, da), BF16), jax.ShapeDtypeStruct((bl, s, da), BF16),
                   jax.ShapeDtypeStruct((bl, s, da), BF16), jax.ShapeDtypeStruct((bl * nh, s, LANES), F32),
                   jax.ShapeDtypeStruct((bl * nh, nk, tk), F32)],
        compiler_params=_cp(3),
    )(ik_tab, iq_tab, q3, kv3, kv3, do3, o3, lse_row, fcol, frow)


def _merge_fwd(mg3, pr3, pa3):
    bl, s, d = pr3.shape
    ts = _tile(s, 256, SUBLANES)
    half = lambda j: pl.BlockSpec((1, ts, d), lambda b, t, j=j: (b, t, j))

    def kern(mr_ref, ma_ref, pr_ref, pa_ref, o_ref):
        o_ref[0] = (_sigmoid(mr_ref[0]) * pr_ref[0] + _sigmoid(ma_ref[0]) * pa_ref[0]).astype(BF16)

    return pl.pallas_call(
        kern, name="merge_fwd", grid=(bl, s // ts),
        in_specs=[half(0), half(1)] + _act_specs(ts, d, 2), out_specs=_act_specs(ts, d, 1)[0],
        out_shape=jax.ShapeDtypeStruct((bl, s, d), BF16), compiler_params=_cp(2),
    )(mg3, mg3, pr3, pa3)


def _merge_bwd(dm3, mg3, pr3, pa3):
    bl, s, d = pr3.shape
    ts = _tile(s, 256, SUBLANES)
    half = lambda j: pl.BlockSpec((1, ts, d), lambda b, t, j=j: (b, t, j))

    def kern(dm_ref, mr_ref, ma_ref, pr_ref, pa_ref, dpr_ref, dpa_ref, dmr_ref, dma_ref):
        dm = dm_ref[0]
        gr, ga = _sigmoid(mr_ref[0]), _sigmoid(ma_ref[0])
        dpr_ref[0] = (gr * dm).astype(BF16)
        dpa_ref[0] = (ga * dm).astype(BF16)
        dmr_ref[0] = (dm * pr_ref[0] * gr * (1.0 - gr)).astype(BF16)
        dma_ref[0] = (dm * pa_ref[0] * ga * (1.0 - ga)).astype(BF16)

    return pl.pallas_call(
        kern, name="merge_bwd", grid=(bl, s // ts),
        in_specs=_act_specs(ts, d, 1) + [half(0), half(1)] + _act_specs(ts, d, 2), out_specs=_act_specs(ts, d, 4),
        out_shape=[jax.ShapeDtypeStruct((bl, s, d), BF16)] * 4, compiler_params=_cp(2),
    )(dm3, mg3, mg3, pr3, pa3)


def _ffn_conv(gf, cw, cb):
    kw = cw.shape[0]
    y = cb
    for k in range(kw):
        y = y + _shift_down(gf, kw - 1 - k) * cw[k:k + 1, :]
    return y


def _ffn_act_fwd(up3, cw, cb):
    bl, s, two = up3.shape
    dff = two // 2
    kw = cw.shape[0]
    tc = _tile(dff, 256, LANES)
    nc = dff // tc

    def kern(gf_ref, uf_ref, cw_ref, cb_ref, o_ref):
        o_ref[0] = (_gelu(_ffn_conv(gf_ref[0], cw_ref[...], cb_ref[...])) * uf_ref[0]).astype(BF16)

    act = lambda off: pl.BlockSpec((1, s, tc), lambda b, j, off=off: (b, 0, off + j))
    return pl.pallas_call(
        kern, name="ffn_act_fwd", grid=(bl, nc),
        in_specs=[act(0), act(nc), pl.BlockSpec((kw, tc), lambda b, j: (0, j)), pl.BlockSpec((1, tc), lambda b, j: (0, j))],
        out_specs=act(0), out_shape=jax.ShapeDtypeStruct((bl, s, dff), BF16), compiler_params=_cp(2),
    )(up3, up3, cw, cb)


def _ffn_act_bwd(up3, dact3, cw, cb):
    bl, s, two = up3.shape
    dff = two // 2
    kw = cw.shape[0]
    tc = _tile(dff, 256, LANES)
    nc = dff // tc

    def kern(gf_ref, uf_ref, da_ref, cw_ref, cb_ref, dgf_ref, duf_ref, dcw_ref, dcb_ref):
        b = pl.program_id(1)
        gf, cwv, da = gf_ref[0], cw_ref[...], da_ref[0]
        ge, dge = _gelu_and_grad(_ffn_conv(gf, cwv, cb_ref[...]))
        duf_ref[0] = (da * ge).astype(BF16)
        dgc = da * uf_ref[0] * dge
        dgf = jnp.zeros_like(dgc)
        rows = []
        for k in range(kw):
            dgf = dgf + _shift_up(dgc, kw - 1 - k) * cwv[k:k + 1, :]
            rows.append(jnp.sum(dgc * _shift_down(gf, kw - 1 - k), axis=0, keepdims=True))
        dgf_ref[0] = dgf.astype(BF16)

        @pl.when(b == 0)
        def _():
            dcw_ref[...] = jnp.zeros_like(dcw_ref)
            dcb_ref[...] = jnp.zeros_like(dcb_ref)

        for k in range(kw):
            dcw_ref[k:k + 1, :] += rows[k]
        dcb_ref[...] += jnp.sum(dgc, axis=0, keepdims=True)

    act = lambda off: pl.BlockSpec((1, s, tc), lambda j, b, off=off: (b, 0, off + j))
    cws = pl.BlockSpec((kw, tc), lambda j, b: (0, j))
    cbs = pl.BlockSpec((1, tc), lambda j, b: (0, j))
    return pl.pallas_call(
        kern, name="ffn_act_bwd", grid=(nc, bl),
        in_specs=[act(0), act(nc), act(0), cws, cbs], out_specs=[act(0), act(0), cws, cbs],
        out_shape=[jax.ShapeDtypeStruct((bl, s, dff), BF16), jax.ShapeDtypeStruct((bl, s, dff), BF16),
                   jax.ShapeDtypeStruct((kw, dff), F32), jax.ShapeDtypeStruct((1, dff), F32)],
        compiler_params=_cp(2),
    )(up3, up3, dact3, cw, cb)


_HBM = pl.BlockSpec(memory_space=pltpu.HBM)


def _place():
    x, y, c = lax.axis_index("x"), lax.axis_index("y"), lax.axis_index("c")
    chips = dict(me=2 * x + y, nx=2 * (1 - x) + y, ny=2 * x + (1 - y), diag=2 * (1 - x) + (1 - y))
    peers = dict(nx=(1 - x, y, c), ny=(x, 1 - y, c), sib=(x, y, 1 - c))
    return c, chips, peers


def _remote(src, dst, sems, k, to):
    return pltpu.make_async_remote_copy(src_ref=src, dst_ref=dst, send_sem=sems[0].at[k], recv_sem=sems[1].at[k],
                                        device_id=to, device_id_type=MESH)


RS_STEPS = 2


def _piece(q, idx, n=1):
    start = idx * q
    if not isinstance(start, int):
        start = pl.multiple_of(start, SUBLANES)
    return pl.ds(start, n * q)


def _all_gather_chips(xs, name):
    nt = len(xs)
    per = 9

    def body(*refs):
        x_refs, o_refs = refs[:nt], refs[nt:2 * nt]
        send_sems, recv_sems = refs[2 * nt:]
        c, chip, peer = _place()
        sems = (send_sems, recv_sems)
        me, nx, ny, dg = chip["me"], chip["nx"], chip["ny"], chip["diag"]
        sends = []

        def arrive(k, dst):
            _remote(dst, dst, sems, k, peer["sib"]).wait_recv()

        def pass_on(k, blk, to):
            cp = _remote(blk, blk, sems, k, peer[to])
            cp.start()
            sends.append(cp)

        for t in range(nt):
            q = xs[t].shape[0] // 4
            half = _piece(q, 2 * c, 2)
            for k, to in ((0, "nx"), (1, "ny")):
                cp = _remote(x_refs[t].at[half], o_refs[t].at[me, half], sems, per * t + k, peer[to])
                cp.start()
                sends.append(cp)
            cp = _remote(x_refs[t], o_refs[t].at[me], sems, per * t + 8, peer["sib"])
            cp.start()
            sends.append(cp)
        for t in range(nt):
            q, o, k0 = xs[t].shape[0] // 4, o_refs[t], per * t
            half, sub0, sub1 = _piece(q, 2 * c, 2), _piece(q, 2 * c), _piece(q, 2 * c + 1)
            arrive(k0 + 0, o.at[nx, half])
            pass_on(k0 + 2, o.at[nx, sub0], "ny")
            pass_on(k0 + 4, o.at[nx, half], "sib")
            arrive(k0 + 1, o.at[ny, half])
            pass_on(k0 + 3, o.at[ny, sub1], "nx")
            pass_on(k0 + 5, o.at[ny, half], "sib")
            arrive(k0 + 2, o.at[dg, sub0])
            pass_on(k0 + 6, o.at[dg, sub0], "sib")
            arrive(k0 + 3, o.at[dg, sub1])
            pass_on(k0 + 7, o.at[dg, sub1], "sib")
        for t in range(nt):
            q, o, k0 = xs[t].shape[0] // 4, o_refs[t], per * t
            arrive(k0 + 4, o.at[nx, _piece(q, 2 * (1 - c), 2)])
            arrive(k0 + 5, o.at[ny, _piece(q, 2 * (1 - c), 2)])
            arrive(k0 + 6, o.at[dg, _piece(q, 2 * (1 - c))])
            arrive(k0 + 7, o.at[dg, _piece(q, 2 * (1 - c) + 1)])
            arrive(k0 + 8, o.at[me])
        for cp in sends:
            cp.wait_send()

    return pl.pallas_call(
        body, name=name, in_specs=[_HBM] * nt, out_specs=[_HBM] * nt,
        out_shape=[jax.ShapeDtypeStruct((N_CHIPS,) + x.shape, x.dtype) for x in xs],
        scratch_shapes=[pltpu.SemaphoreType.DMA((per * nt,)), pltpu.SemaphoreType.DMA((per * nt,))],
    )(*xs)


def _all_gather_devices(xs, name):
    nt = len(xs)
    per = 7

    def body(*refs):
        x_refs, o_refs = refs[:nt], refs[nt:2 * nt]
        send_sems, recv_sems, local_sems = refs[2 * nt:]
        x, y, c = lax.axis_index("x"), lax.axis_index("y"), lax.axis_index("c")
        sems = (send_sems, recv_sems)
        sib = (x, y, 1 - c)
        chips = [(1 - x, y), (x, 1 - y), (1 - x, 1 - y)]
        slot = lambda px, py, pc: 4 * px + 2 * py + pc
        me = slot(x, y, c)
        sends, copies = [], []

        def arrive(k, dst):
            _remote(dst, dst, sems, k, sib).wait_recv()

        for t in range(nt):
            cp = pltpu.make_async_copy(x_refs[t], o_refs[t].at[me], local_sems.at[t])
            cp.start()
            copies.append(cp)
            for k, to in enumerate([sib] + [(*chip, c) for chip in chips]):
                cp = _remote(x_refs[t], o_refs[t].at[me], sems, per * t + k, to)
                cp.start()
                sends.append(cp)
        for t in range(nt):
            for j, chip in enumerate(chips):
                blk = o_refs[t].at[slot(*chip, c)]
                arrive(per * t + 1 + j, blk)
                cp = _remote(blk, blk, sems, per * t + 4 + j, sib)
                cp.start()
                sends.append(cp)
        for t in range(nt):
            arrive(per * t, o_refs[t].at[slot(x, y, 1 - c)])
            for j, chip in enumerate(chips):
                arrive(per * t + 4 + j, o_refs[t].at[slot(*chip, 1 - c)])
        for cp in sends:
            cp.wait_send()
        for cp in copies:
            cp.wait()

    return pl.pallas_call(
        body, name=name, in_specs=[_HBM] * nt, out_specs=[_HBM] * nt,
        out_shape=[jax.ShapeDtypeStruct((2 * N_CHIPS,) + a.shape, a.dtype) for a in xs],
        scratch_shapes=[pltpu.SemaphoreType.DMA((per * nt,)), pltpu.SemaphoreType.DMA((per * nt,)),
                        pltpu.SemaphoreType.DMA((nt,))],
    )(*xs)


def _exchange(name, xs, out_shapes, plan):
    nt = len(xs)

    def body(*refs):
        x_refs, o_refs = refs[:nt], refs[nt:2 * nt]
        send_sems, recv_sems = refs[2 * nt:]
        c, chip, peer = _place()
        cps = []
        for t in range(nt):
            for src, dst, to in plan(c, chip, x_refs[t], o_refs[t], xs[t].shape):
                cps.append(_remote(src, dst, (send_sems, recv_sems), len(cps), peer[to]))
        for cp in cps:
            cp.start()
        for cp in cps:
            cp.wait()

    n_copies = nt * len(plan(0, dict(me=0, nx=2, ny=1, diag=3), None, None, xs[0].shape, count_only=True))
    return pl.pallas_call(
        body, name=name, in_specs=[_HBM] * nt, out_specs=[_HBM] * nt,
        out_shape=[jax.ShapeDtypeStruct(s, x.dtype) for s, x in zip(out_shapes, xs)],
        scratch_shapes=[pltpu.SemaphoreType.DMA((n_copies,)), pltpu.SemaphoreType.DMA((n_copies,))],
    )(*xs)


def _plan_sibling(c, chip, g, out, shape, count_only=False):
    if count_only:
        return [None] * N_CHIPS
    q = shape[1] // 4
    return [(g.at[j, _piece(q, 2 * (1 - c), 2)], out.at[j], "sib") for j in range(N_CHIPS)]


def _plan_first(c, chip, p, out, shape, count_only=False):
    if count_only:
        return [None] * 4
    q = shape[1] // 2
    return [(p.at[chip["nx"], _piece(q, 0)], out.at[0], "nx"), (p.at[chip["diag"], _piece(q, 0)], out.at[1], "nx"),
            (p.at[chip["ny"], _piece(q, 1)], out.at[2], "ny"), (p.at[chip["diag"], _piece(q, 1)], out.at[3], "ny")]


def _plan_second(c, chip, p, out, shape, count_only=False):
    if count_only:
        return [None] * 2
    return [(p.at[1], out.at[0], "ny"), (p.at[3], out.at[1], "nx")]


def _rs_last(ps):
    nt = len(ps)

    def body(*refs):
        p_refs, o_refs = refs[:nt], refs[nt:2 * nt]
        send_sems, recv_sems = refs[2 * nt:]
        c, _, peer = _place()
        sems = (send_sems, recv_sems)
        cps = []
        for t in range(nt):
            q = ps[t].shape[0] // 4
            mine = _piece(q, 2 * c, 2)
            cps.append(_remote(p_refs[t].at[mine], o_refs[t].at[mine], sems, t, peer["sib"]))
            cps[-1].start()
        for t in range(nt):
            q = ps[t].shape[0] // 4
            theirs = _piece(q, 2 * (1 - c), 2)
            cps[t].wait_send()
            _remote(p_refs[t].at[theirs], o_refs[t].at[theirs], sems, t, peer["sib"]).wait_recv()

    return pl.pallas_call(
        body, name="rs_last", in_specs=[_HBM] * nt, out_specs=[_HBM] * nt,
        out_shape=[jax.ShapeDtypeStruct(p.shape, F32) for p in ps],
        input_output_aliases={t: t for t in range(nt)},
        scratch_shapes=[pltpu.SemaphoreType.DMA((nt,)), pltpu.SemaphoreType.DMA((nt,))],
    )(*ps)


def _add_stage(name, grid, a_list, b_list, a_map, b_map, tbs, out_shapes, out_map, out_dtype, prefetch=None):
    nt = len(a_list)
    lead = lambda shape: (None,) * (len(shape) - 2)

    def kern(*refs):
        refs = refs[(1 if prefetch is not None else 0):]
        for t in range(nt):
            refs[2 * nt + t][...] = (refs[t][...].astype(F32) + refs[nt + t][...].astype(F32)).astype(out_dtype)

    in_specs = [pl.BlockSpec(lead(a.shape) + (tb, a.shape[-1]), a_map) for a, tb in zip(a_list, tbs)]
    in_specs += [pl.BlockSpec(lead(b.shape) + (tb, b.shape[-1]), b_map) for b, tb in zip(b_list, tbs)]
    out_specs = [pl.BlockSpec(lead(s) + (tb, s[-1]), out_map) for s, tb in zip(out_shapes, tbs)]
    out_shape = [jax.ShapeDtypeStruct(s, out_dtype) for s in out_shapes]
    if prefetch is None:
        return pl.pallas_call(kern, name=name, grid=grid, in_specs=in_specs, out_specs=out_specs, out_shape=out_shape,
                              compiler_params=_cp(len(grid)))(*a_list, *b_list)
    return pl.pallas_call(
        kern, name=name,
        grid_spec=pltpu.PrefetchScalarGridSpec(num_scalar_prefetch=1, grid=grid, in_specs=in_specs, out_specs=out_specs),
        out_shape=out_shape, compiler_params=_cp(len(grid)))(prefetch, *a_list, *b_list)


def _reduce_scatter_chips(gs):
    x, y, c = lax.axis_index("x"), lax.axis_index("y"), lax.axis_index("c")
    me, nx, ny = 2 * x + y, 2 * (1 - x) + y, 2 * x + (1 - y)
    st = RS_STEPS
    unit = 4 * st * 2 * SUBLANES
    rows = [g.shape[1] for g in gs]
    gs = [jnp.pad(g, ((0, 0), (0, -g.shape[1] % unit), (0, 0))) for g in gs]
    qs = [g.shape[1] // 4 for g in gs]
    tbs = [q // st for q in qs]
    cols = [g.shape[2] for g in gs]
    core = jnp.reshape(c, (1,)).astype(jnp.int32)

    got = _exchange("rs_sibling", gs, [(N_CHIPS, 2 * q, cc) for q, cc in zip(qs, cols)], _plan_sibling)
    p0 = _add_stage("rs_add_sibling", (N_CHIPS, 2, st), gs, got,
                    lambda j, h, s, c_ref: (j, (2 * c_ref[0] + h) * st + s, 0), lambda j, h, s, c_ref: (j, h * st + s, 0),
                    tbs, [(N_CHIPS, 2 * q, cc) for q, cc in zip(qs, cols)], lambda j, h, s, c_ref: (j, h * st + s, 0),
                    BF16, prefetch=core)
    got = _exchange("rs_first", p0, [(4, q, cc) for q, cc in zip(qs, cols)], _plan_first)
    p1 = _add_stage("rs_add_first", (4, st), p0, got,
                    lambda k, s, i_ref: (i_ref[k], (k // 2) * st + s, 0), lambda k, s, i_ref: (k, s, 0),
                    tbs, [(4, q, cc) for q, cc in zip(qs, cols)], lambda k, s, i_ref: (k, s, 0),
                    BF16, prefetch=jnp.stack([me, ny, me, nx]).astype(jnp.int32))
    got = _exchange("rs_second", p1, [(2, q, cc) for q, cc in zip(qs, cols)], _plan_second)
    p2 = _add_stage("rs_add_second", (2, st), p1, got, lambda h, s, c_ref: (2 * h, s, 0), lambda h, s, c_ref: (h, s, 0),
                    tbs, [(4 * q, cc) for q, cc in zip(qs, cols)], lambda h, s, c_ref: ((2 * c_ref[0] + h) * st + s, 0),
                    F32, prefetch=core)
    return [out[:r] for out, r in zip(_rs_last(p2), rows)]


def _adamw(g, w, m, v, name):
    rows, cc = g.shape
    tr = _tile(rows, max(SUBLANES, (1 << 18) // cc), SUBLANES)
    k1 = 1.0 - ADAM_B1 ** ADAM_STEP
    k2 = 1.0 - ADAM_B2 ** ADAM_STEP

    def kern(g_ref, w_ref, m_ref, v_ref, d_ref, nm_ref, nv_ref):
        gv = g_ref[...]
        nm = ADAM_B1 * m_ref[...] + (1.0 - ADAM_B1) * gv
        nv = ADAM_B2 * v_ref[...] + (1.0 - ADAM_B2) * (gv * gv)
        nm_ref[...] = nm
        nv_ref[...] = nv
        d_ref[...] = -ADAM_LR * ((nm / k1) / (jnp.sqrt(nv / k2) + ADAM_EPS) + ADAM_WD * w_ref[...])

    spec = pl.BlockSpec((tr, cc), lambda t: (t, 0))
    return pl.pallas_call(
        kern, name=name, grid=(rows // tr,), in_specs=[spec] * 4, out_specs=[spec] * 3,
        out_shape=[jax.ShapeDtypeStruct((rows, cc), F32)] * 3, compiler_params=_cp(1),
    )(g, w, m, v)


def _flat_pad(parts, total):
    flat = jnp.concatenate([p.reshape(-1) for p in parts])
    return jnp.pad(flat, (0, total - flat.shape[0]))


def _split_flat(flat, shapes):
    out, pos = [], 0
    for shp in shapes:
        size = math.prod(shp)
        out.append(flat[pos:pos + size].reshape(shp))
        pos += size
    return out


def _cols_of_chunks(chunks, lo, hi):
    width = chunks.shape[2]
    parts = []
    for j in range(chunks.shape[0]):
        a, b = max(lo, j * width), min(hi, (j + 1) * width)
        if a < b:
            parts.append(chunks[j, :, a - j * width:b - j * width])
    return parts[0] if len(parts) == 1 else jnp.concatenate(parts, axis=1)


def _chunks_of_cols(segments, n_chunks):
    total = sum(s.shape[1] for s in segments)
    width = total // n_chunks
    chunks = []
    for j in range(n_chunks):
        lo, hi, pos, parts = j * width, (j + 1) * width, 0, []
        for s in segments:
            a, b = max(lo, pos), min(hi, pos + s.shape[1])
            if a < b:
                parts.append(s[:, a - pos:b - pos])
            pos += s.shape[1]
        chunks.append(parts[0] if len(parts) == 1 else jnp.concatenate(parts, axis=1))
    return jnp.stack(chunks)


_WEIGHTS = ['w_ada', 'b_ada', 'g_norm1', 'w_in', 'w_rnn_conv', 'b_rnn_conv', 'w_lru_a', 'b_lru_a', 'w_lru_i', 'b_lru_i',
            'lru_lambda', 'b_fgate', 'w_proj_rnn', 'w_proj_attn', 'w_out', 'g_norm2', 'w_ffn_up', 'w_ffn_conv',
            'b_ffn_conv', 'w_ffn_down', 'w_ada_final', 'b_ada_final', 'g_final']
_MATMUL = ['w_in', 'w_proj_rnn', 'w_proj_attn', 'w_out', 'w_ffn_up', 'w_ffn_down']
_ADA = ['w_ada', 'w_ada_final']
_ADA_BIAS = ['b_ada', 'b_ada_final']
_CONV = ['w_rnn_conv', 'w_ffn_conv']
_REPLICATED = [n for n in _WEIGHTS if n not in _MATMUL + _ADA + _ADA_BIAS + _CONV]


def kernel(x, c, w_ada, b_ada, g_norm1, w_in, w_rnn_conv, b_rnn_conv, w_lru_a, b_lru_a, w_lru_i, b_lru_i, lru_lambda, b_fgate, w_proj_rnn, w_proj_attn, w_out, g_norm2, w_ffn_up, w_ffn_conv, b_ffn_conv, w_ffn_down, w_ada_final, b_ada_final, g_final, loss_target, m_w_ada, m_b_ada, m_g_norm1, m_w_in, m_w_rnn_conv, m_b_rnn_conv, m_w_lru_a, m_b_lru_a, m_w_lru_i, m_b_lru_i, m_lru_lambda, m_b_fgate, m_w_proj_rnn, m_w_proj_attn, m_w_out, m_g_norm2, m_w_ffn_up, m_w_ffn_conv, m_b_ffn_conv, m_w_ffn_down, m_w_ada_final, m_b_ada_final, m_g_final, v_w_ada, v_b_ada, v_g_norm1, v_w_in, v_w_rnn_conv, v_b_rnn_conv, v_w_lru_a, v_b_lru_a, v_w_lru_i, v_b_lru_i, v_lru_lambda, v_b_fgate, v_w_proj_rnn, v_w_proj_attn, v_w_out, v_g_norm2, v_w_ffn_up, v_w_ffn_conv, v_b_ffn_conv, v_w_ffn_down, v_w_ada_final, v_b_ada_final, v_g_final):
    args = locals()
    shape_of = {n: args[n].shape for n in _WEIGHTS}

    def view(a):
        if a.ndim >= 3:
            return a[0]
        return a[None, :] if a.ndim == 1 else a

    w2 = {n: view(args[n]) for n in _WEIGHTS}
    m2 = {n: args['m_' + n].reshape(w2[n].shape) for n in _WEIGHTS}
    v2 = {n: args['v_' + n].reshape(w2[n].shape) for n in _WEIGHTS}

    bl, s, d = x.shape
    t = bl * s
    nh = b_fgate.shape[-1]
    nb, rb = w_lru_a.shape[1], w_lru_a.shape[2]
    dr = nb * rb
    da = w2['w_proj_attn'].shape[0] * N_CHIPS
    dh = da // nh
    dff = w2['w_ffn_conv'].shape[1] * N_CHIPS
    scale = dh ** -0.5
    chip = 2 * lax.axis_index("x") + lax.axis_index("y")
    dev = 2 * chip + lax.axis_index("c")

    names = list(_MATMUL)
    n_conv = sum(w2[n].size for n in _CONV)
    rows_conv = -(-n_conv // (FLAT_COLS * 32)) * 32
    conv_local = _flat_pad([w2[n] for n in _CONV], rows_conv * FLAT_COLS).reshape(rows_conv, FLAT_COLS)
    *weights_all, conv_all = _all_gather_chips([w2[n].astype(BF16) for n in names] + [conv_local], "ag_weights")
    gathered = dict(zip(names, weights_all))
    conv_all = conv_all.reshape(N_CHIPS, -1)
    conv_full, pos = {}, 0
    for n in _CONV:
        r, n4 = w2[n].shape
        blocks = conv_all[:, pos:pos + r * n4].reshape(N_CHIPS, r, n4)
        conv_full[n] = jnp.concatenate([blocks[j] for j in range(N_CHIPS)], axis=1)
        pos += r * n4
    rowmajor = lambda n: gathered[n].reshape(-1, gathered[n].shape[2])
    w_proj_rnn_f, w_proj_attn_f, w_out_f, w_ffn_down_f = (rowmajor(n) for n in ('w_proj_rnn', 'w_proj_attn', 'w_out', 'w_ffn_down'))
    up_chunk = w2['w_ffn_up'].shape[1]

    o_q, o_k, o_fl = 2 * dr, 2 * dr + da, 2 * dr + 3 * da
    o_mg = o_fl + nh
    g_in_w = gathered['w_in']
    w_rnn, w_q = _cols_of_chunks(g_in_w, 0, o_q), _cols_of_chunks(g_in_w, o_q, o_k)
    w_kv, w_mg = _cols_of_chunks(g_in_w, o_k, o_fl), _cols_of_chunks(g_in_w, o_mg, o_mg + 2 * d)
    w_fl = jnp.pad(_cols_of_chunks(g_in_w, o_fl, o_mg), ((0, 0), (0, LANES - nh)))
    w_rest = jnp.concatenate([w_q, w_kv, w_mg], axis=1)
    bf_pad = jnp.pad(w2['b_fgate'], ((0, 0), (0, LANES - nh)))

    nd = 2 * N_CHIPS
    c_act = _silu_pad(_all_gather_devices([c], "ag_cond")[0].reshape(nd * bl, d), nd * bl)
    my_cols = lambda a, n: lax.dynamic_slice_in_dim(a, chip * w2[n].shape[1], w2[n].shape[1], axis=1)
    mod_cols = [_mm(c_act, w2[n].astype(BF16), "nn", name=n + "_fwd", bias=my_cols(w2[b], n)) for n, b in zip(_ADA, _ADA_BIAS)]
    my_rows = lambda g: lax.dynamic_slice_in_dim(g, dev * bl, bl, axis=1).transpose(1, 0, 2).reshape(bl, -1)
    mod, modf = (my_rows(g) for g in _all_gather_chips(mod_cols, "ag_mod"))
    sh1, sc1, gt1, sh2, sc2, gt2 = [mod[:, i * d:(i + 1) * d].reshape(bl, 1, d) for i in range(6)]
    shf, scf = modf[:, :d].reshape(bl, 1, d), modf[:, d:].reshape(bl, 1, d)

    h1 = _norm_mod_fwd(x, w2['g_norm1'], sh1, sc1)
    h1f = h1.reshape(t, d)
    zr = _mm(h1f, w_rnn, "nn", name="in_rnn").reshape(bl, s, 2 * dr)
    q3 = _mm(h1f, w_q, "nn", name="in_q", out_dtype=BF16, scale=scale).reshape(bl, s, da)
    kv3 = _mm(h1f, w_kv, "nn", name="in_kv", out_dtype=BF16).reshape(bl, s, 2 * da)
    mg3 = _mm(h1f, w_mg, "nn", name="in_mg").reshape(bl, s, 2 * d)
    zf3 = _mm(h1f, w_fl, "nn", name="in_fl").reshape(bl, s, LANES)

    lru = (conv_full['w_rnn_conv'], w2['b_rnn_conv'], w2['w_lru_a'], w2['b_lru_a'], w2['w_lru_i'], w2['b_lru_i'], w2['lru_lambda'])
    y_rnn, *rnn_kept = _rnn_fwd(zr, *lru)

    f3 = _fgate_fwd(zf3, bf_pad)
    f_heads = f3[:, :, :nh].transpose(0, 2, 1).reshape(bl * nh, s)
    fcol = jnp.broadcast_to(f_heads[:, :, None], (bl * nh, s, LANES))
    frow = f_heads.reshape(bl * nh, 1, s)
    o3, lse_row = _attn_fwd(q3, kv3, fcol, frow, nh)

    pr3 = _mm(y_rnn.reshape(t, dr), w_proj_rnn_f, "nn", name="proj_rnn").reshape(bl, s, d)
    pa3 = _mm(o3.reshape(t, da), w_proj_attn_f, "nn", name="proj_attn").reshape(bl, s, d)
    merged = _merge_fwd(mg3, pr3, pa3)
    mo3 = _mm(merged.reshape(t, d), w_out_f, "nn", name="mix_out").reshape(bl, s, d)
    x1, h2 = _resid_norm_fwd(x, mo3, gt1, w2['g_norm2'], sh2, sc2)
    h2f = h2.reshape(t, d)
    up3 = _mm(h2f, gathered['w_ffn_up'], "nn", name="ffn_up", b_chunk=up_chunk, tn=up_chunk).reshape(bl, s, 2 * dff)
    act3 = _ffn_act_fwd(up3, conv_full['w_ffn_conv'], w2['b_ffn_conv'])
    yf3 = _mm(act3.reshape(t, dff), w_ffn_down_f, "nn", name="ffn_down").reshape(bl, s, d)

    dx2, dyf, dgt2, dshf, dscf, dg_final, loss_part = _final_fwd_bwd(x1, yf3, gt2, w2['g_final'], shf, scf, loss_target)
    loss = lax.psum(loss_part[0, 0], ("x", "y", "c"))

    dyf_f = dyf.reshape(t, d)
    g_ffn_down = _mm(act3.reshape(t, dff), dyf_f, "tn", name="dw_ffn_down")
    dact3 = _mm(dyf_f, w_ffn_down_f, "nt", name="d_ffn_act").reshape(bl, s, dff)
    dgf, duf, g_ffn_conv, g_b_ffn_conv = _ffn_act_bwd(up3, dact3, conv_full['w_ffn_conv'], w2['b_ffn_conv'])
    dgf_f, duf_f = dgf.reshape(t, dff), duf.reshape(t, dff)
    g_ffn_up = jnp.concatenate([_mm(h2f, dgf_f, "tn", name="dw_ffn_up_gate", out_chunk=up_chunk),
                                _mm(h2f, duf_f, "tn", name="dw_ffn_up_value", out_chunk=up_chunk)], axis=0)
    dh2 = _mm([dgf_f, duf_f], gathered['w_ffn_up'], "nt", name="d_h2", b_chunk=up_chunk).reshape(bl, s, d)
    dmo, dgt1, dx1, dsh2, dsc2, dg_norm2 = _norm_mod_bwd(dh2, x1, dx2, w2['g_norm2'], sc2, "norm2_bwd", gated=(mo3, gt1))

    dmo_f = dmo.reshape(t, d)
    g_out = _mm(merged.reshape(t, d), dmo_f, "tn", name="dw_out")
    dm3 = _mm(dmo_f, w_out_f, "nt", name="d_merged").reshape(bl, s, d)
    dpr, dpa, dmr, dma = _merge_bwd(dm3, mg3, pr3, pa3)
    g_proj_rnn = _mm(y_rnn.reshape(t, dr), dpr.reshape(t, d), "tn", name="dw_proj_rnn")
    g_proj_attn = _mm(o3.reshape(t, da), dpa.reshape(t, d), "tn", name="dw_proj_attn")
    dyr3 = _mm(dpr.reshape(t, d), w_proj_rnn_f, "nt", name="d_y_rnn").reshape(bl, s, dr)
    do3 = _mm(dpa.reshape(t, d), w_proj_attn_f, "nt", name="d_y_attn").reshape(bl, s, da)

    dq3, dk3, dv3, dfk, dfq = _attn_bwd(q3, kv3, do3, o3, lse_row, fcol, frow, nh, scale)
    heads_last = lambda a: jnp.pad(a.reshape(bl, nh, s).transpose(0, 2, 1), ((0, 0), (0, 0), (0, LANES - nh)))
    dzf3, g_bf = _fgate_bwd(heads_last(dfk[:, :, 0]), heads_last(dfq), zf3, bf_pad)

    dxr, dgr, g_rnn_conv, g_b_rnn_conv, g_lru_a, g_b_lru_a, g_lru_i, g_b_lru_i, g_lam = _rnn_bwd(
        zr, rnn_kept, dyr3, lru[0], lru[2], lru[4], lru[6])

    dz = [a.reshape(t, -1) for a in (dxr, dgr, dq3, dk3, dv3, dmr, dma)]
    dzf_f = dzf3.reshape(t, LANES)
    seg_names = ("xr", "gr", "q", "k", "v", "mr", "ma")
    g_seg = [_mm(h1f, a, "tn", name="dw_in_" + n) for n, a in zip(seg_names, dz)]
    g_in_fl = _mm(h1f, dzf_f, "tn", name="dw_in_fl")[:, :nh]
    g_in = _chunks_of_cols(g_seg[:5] + [g_in_fl] + g_seg[5:], N_CHIPS)
    dh1 = _mm(dzf_f, w_fl, "nt", name="d_h1_fl")
    dh1 = _mm(dz[:2], w_rnn, "nt", name="d_h1_rnn", add=dh1)
    dh1 = _mm(dz[2:], w_rest, "nt", name="d_h1", add=dh1).reshape(bl, s, d)
    grad_x, dsh1, dsc1, dg_norm1 = _norm_mod_bwd(dh1, x, dx1, w2['g_norm1'], sc1, "norm1_bwd")

    dmods = [jnp.concatenate([dsh1, dsc1, dgt1, dsh2, dsc2, dgt2], axis=-1).reshape(bl, -1),
             jnp.concatenate([dshf, dscf], axis=-1).reshape(bl, -1)]
    dmods = [g.reshape(nd * bl, -1) for g in _all_gather_devices(dmods, "ag_dmod")]
    grad = {n: _mm(c_act, my_cols(g, n), "tn", name="dw_" + n) for n, g in zip(_ADA, dmods)}
    grad.update({b: _rowsum(g, "d" + b) for b, g in zip(_ADA_BIAS, dmods)})

    rowchunks = lambda g: g.reshape(N_CHIPS, g.shape[0] // N_CHIPS, g.shape[1])
    full = dict(w_in=g_in, w_proj_rnn=rowchunks(g_proj_rnn), w_proj_attn=rowchunks(g_proj_attn),
                w_out=rowchunks(g_out), w_ffn_up=g_ffn_up, w_ffn_down=rowchunks(g_ffn_down))
    small = dict(g_norm1=dg_norm1, w_rnn_conv=g_rnn_conv, b_rnn_conv=g_b_rnn_conv, w_lru_a=g_lru_a,
                 b_lru_a=g_b_lru_a, w_lru_i=g_lru_i, b_lru_i=g_b_lru_i, lru_lambda=g_lam, b_fgate=g_bf[:, :nh],
                 g_norm2=dg_norm2, w_ffn_conv=g_ffn_conv, b_ffn_conv=g_b_ffn_conv, g_final=dg_final)

    small_names = _REPLICATED + _CONV
    n_small = sum(small[n].size for n in small_names)
    rows_q = -(-n_small // (N_CHIPS * FLAT_COLS * 32 * RS_STEPS)) * 32 * RS_STEPS
    small_flat = _flat_pad([small[n] for n in small_names], N_CHIPS * rows_q * FLAT_COLS).reshape(N_CHIPS, rows_q, FLAT_COLS)
    reduced = _reduce_scatter_chips([full[n] for n in names] + [small_flat])
    grad.update(zip(names, reduced[:-1]))
    small_all = _all_gather_chips([reduced[-1]], "ag_small_grads")[0].reshape(-1)
    grad.update(zip(small_names, _split_flat(small_all, [small[n].shape for n in small_names])))
    for n in _CONV:
        n4 = w2[n].shape[1]
        grad[n] = lax.dynamic_slice_in_dim(grad[n], chip * n4, n4, axis=1)

    delta_w, new_m, new_v = {}, {}, {}
    for n in names + _ADA:
        delta_w[n], new_m[n], new_v[n] = _adamw(grad[n], w2[n], m2[n], v2[n], "adamw_" + n)
    small_names = small_names + _ADA_BIAS
    rows_small = -(-sum(w2[n].size for n in small_names) // (FLAT_COLS * SUBLANES)) * SUBLANES
    flat_small = lambda src: _flat_pad([src[n] for n in small_names], rows_small * FLAT_COLS).reshape(rows_small, FLAT_COLS)
    small_out = _adamw(flat_small(grad), flat_small(w2), flat_small(m2), flat_small(v2), "adamw_small")
    for dst, flat in zip((delta_w, new_m, new_v), small_out):
        dst.update(zip(small_names, _split_flat(flat.reshape(-1), [w2[n].shape for n in small_names])))

    out = [loss, grad_x]
    for src in (grad, delta_w, new_m, new_v):
        out += [src[n].reshape(shape_of[n]) for n in _WEIGHTS]
    return tuple(out)
```

```python
import functools
import math

import jax
import jax.numpy as jnp
from jax import lax
from jax.experimental import pallas as pl
from jax.experimental.pallas import tpu as pltpu

F32 = jnp.float32
BF16 = jnp.bfloat16
MESH = pl.DeviceIdType.MESH

RMS_EPS = 1e-6
LRU_C = 8.0
ADAM_LR = 0.001
ADAM_B1 = 0.9
ADAM_B2 = 0.999
ADAM_EPS = 1e-08
ADAM_WD = 0.01
ADAM_STEP = 10

LANES = 128
SUBLANES = 8
N_CHIPS = 4
FLAT_COLS = 1024
SCAN_SEGMENTS = 2 * SUBLANES
VMEM_LIMIT = 48 * 1024 * 1024
NEG_BIG = -1e30


def _cp(n_axes):
    return pltpu.CompilerParams(dimension_semantics=("arbitrary",) * n_axes, vmem_limit_bytes=VMEM_LIMIT)


def _tile(n, target, align):
    if n <= target:
        return n
    t = (target // align) * align
    while t >= align:
        if n % t == 0:
            return t
        t -= align
    return n


def _nice_rows(n, align):
    r = -(-n // align) * align
    while True:
        if r <= 640:
            return r, r
        t = _tile(r, 640, align)
        if 128 <= t <= 640:
            return r, t
        r += align


def _sigmoid(x):
    return 0.5 * jnp.tanh(0.5 * x) + 0.5


def _softplus(x):
    return jnp.maximum(x, 0.0) + jnp.log1p(jnp.exp(-jnp.abs(x)))


def _expm1(x, exp_x):
    small = x * (1.0 + 0.5 * x * (1.0 + (1.0 / 3.0) * x * (1.0 + 0.25 * x)))
    return jnp.where(jnp.abs(x) < 0.05, small, exp_x - 1.0)


_GELU_K = math.sqrt(2.0 / math.pi)
_GELU_C = 0.044715


def _gelu(x):
    t = jnp.tanh(_GELU_K * (x + _GELU_C * x * x * x))
    return 0.5 * x * (1.0 + t)


def _gelu_and_grad(x):
    t = jnp.tanh(_GELU_K * (x + _GELU_C * x * x * x))
    g = 0.5 * x * (1.0 + t)
    dg = 0.5 * (1.0 + t) + 0.5 * x * (1.0 - t * t) * _GELU_K * (1.0 + 3.0 * _GELU_C * x * x)
    return g, dg


def _shift_down(x, k):
    if k == 0:
        return x
    y = pltpu.roll(x, k, 0)
    rows = lax.broadcasted_iota(jnp.int32, (SUBLANES, x.shape[1]), 0)
    return jnp.concatenate([jnp.where(rows >= k, y[:SUBLANES], 0.0), y[SUBLANES:]], axis=0)


def _shift_up(x, k):
    if k == 0:
        return x
    s = x.shape[0]
    y = pltpu.roll(x, s - k, 0)
    rows = lax.broadcasted_iota(jnp.int32, (SUBLANES, x.shape[1]), 0)
    return jnp.concatenate([y[:s - SUBLANES], jnp.where(rows < SUBLANES - k, y[s - SUBLANES:], 0.0)], axis=0)


def _dot(a, b, dims):
    return lax.dot_general(a.astype(BF16), b.astype(BF16), (dims, ((), ())), preferred_element_type=F32)


_NN = ((1,), (0,))
_NT = ((1,), (1,))
_TN = ((0,), (0,))


def _mm(a, b, mode, *, name, out_dtype=F32, scale=None, bias=None, add=None, tm=1024, tn=1024, tk=1024,
        b_chunk=None, out_chunk=None):
    pieces = list(a) if isinstance(a, (list, tuple)) else [a]
    ksize = lambda p: p.shape[0] if mode == "tn" else p.shape[1]
    if b_chunk is None:
        brows, bcols = b.shape
    else:
        brows, bcols = b.shape[1], b.shape[0] * b_chunk
    k = sum(ksize(p) for p in pieces)
    if mode == "nt":
        m, n = pieces[0].shape[0], brows
        assert bcols == k, (bcols, k)
    else:
        m, n = (pieces[0].shape[1] if mode == "tn" else pieces[0].shape[0]), bcols
        assert brows == k, (brows, k)
    tm = _tile(m, tm, LANES)
    ncut = n
    if b_chunk is not None and mode != "nt":
        ncut = b_chunk
    if out_chunk is not None:
        ncut = math.gcd(ncut, out_chunk)
    tn = _tile(ncut, tn, LANES)
    kcut = b_chunk if (b_chunk is not None and mode == "nt") else k
    for p in pieces:
        kcut = math.gcd(kcut, ksize(p))
    tk = _tile(kcut, tk, LANES)
    nk = k // tk
    dims = {"nn": _NN, "nt": _NT, "tn": _TN}[mode]
    counts = [ksize(p) // tk for p in pieces]
    starts = [sum(counts[:i]) for i in range(len(pieces))]
    n_pieces = len(pieces)

    def a_spec(s0, cnt):
        kmap = (lambda kk: kk) if n_pieces == 1 else (lambda kk: jnp.clip(kk - s0, 0, cnt - 1))
        if mode == "tn":
            return pl.BlockSpec((tk, tm), lambda i, j, kk: (kmap(kk), i))
        return pl.BlockSpec((tm, tk), lambda i, j, kk: (i, kmap(kk)))

    if b_chunk is None:
        if mode == "nt":
            b_spec = pl.BlockSpec((tn, tk), lambda i, j, kk: (j, kk))
        else:
            b_spec = pl.BlockSpec((tk, tn), lambda i, j, kk: (kk, j))
    elif mode == "nt":
        per_b = b_chunk // tk
        b_spec = pl.BlockSpec((None, tn, tk), lambda i, j, kk: (kk // per_b, j, kk % per_b))
    else:
        per_b = b_chunk // tn
        b_spec = pl.BlockSpec((None, tk, tn), lambda i, j, kk: (j // per_b, kk, j % per_b))
    if out_chunk is None:
        out_spec = pl.BlockSpec((tm, tn), lambda i, j, kk: (i, j))
        out_shape = jax.ShapeDtypeStruct((m, n), out_dtype)
    else:
        per_o = out_chunk // tn
        out_spec = pl.BlockSpec((None, tm, tn), lambda i, j, kk: (j // per_o, i, j % per_o))
        out_shape = jax.ShapeDtypeStruct((n // out_chunk, m, out_chunk), out_dtype)
    in_specs = [a_spec(s0, cnt) for s0, cnt in zip(starts, counts)] + [b_spec]
    args = pieces + [b]
    if bias is not None:
        in_specs.append(pl.BlockSpec((1, tn), lambda i, j, kk: (0, j)))
        args.append(bias)
    if add is not None:
        in_specs.append(pl.BlockSpec((tm, tn), lambda i, j, kk: (i, j)))
        args.append(add)

    def kern(*refs):
        b_ref = refs[n_pieces]
        o_ref = refs[n_pieces + 1 + (bias is not None) + (add is not None)]

        def finish(r):
            if scale is not None:
                r = r * scale
            pos = n_pieces + 1
            if bias is not None:
                r = r + refs[pos][...]
                pos += 1
            if add is not None:
                r = r + refs[pos][...]
            o_ref[...] = r.astype(out_dtype)

        if nk == 1:
            finish(_dot(refs[0][...], b_ref[...], dims))
            return
        acc = refs[-1]
        kk = pl.program_id(2)

        @pl.when(kk == 0)
        def _():
            acc[...] = jnp.zeros_like(acc)

        if n_pieces == 1:
            acc[...] += _dot(refs[0][...], b_ref[...], dims)
        else:
            for idx in range(n_pieces):
                @pl.when((kk >= starts[idx]) & (kk < starts[idx] + counts[idx]))
                def _(idx=idx):
                    acc[...] += _dot(refs[idx][...], b_ref[...], dims)

        @pl.when(kk == nk - 1)
        def _():
            finish(acc[...])

    return pl.pallas_call(
        kern, name=name,
        grid=(m // tm, n // tn, nk),
        in_specs=in_specs, out_specs=out_spec, out_shape=out_shape,
        scratch_shapes=[pltpu.VMEM((tm, tn), F32)] if nk > 1 else [],
        compiler_params=_cp(3),
    )(*args)


def _silu_pad(c, rows):
    bl, d = c.shape

    def kern(c_ref, o_ref):
        o_ref[...] = jnp.zeros_like(o_ref)
        v = c_ref[...]
        o_ref[0:bl, :] = v * _sigmoid(v)

    return pl.pallas_call(kern, name="silu_pad", out_shape=jax.ShapeDtypeStruct((rows, d), F32))(c)


def _rowsum(x, name):
    r, n = x.shape

    def kern(x_ref, o_ref):
        o_ref[...] = jnp.sum(x_ref[...], axis=0, keepdims=True)

    return pl.pallas_call(kern, name=name, out_shape=jax.ShapeDtypeStruct((1, n), F32))(x)


def _norm_parts(x, g):
    r = lax.rsqrt(jnp.mean(x * x, axis=-1, keepdims=True) + RMS_EPS)
    xh = x * r
    return r, xh, xh * g


def _norm_bwd_parts(dh, xh, r, g, sc):
    n = xh * g
    dn = dh * (1.0 + sc)
    dxh = dn * g
    dx = r * (dxh - xh * jnp.mean(dxh * xh, axis=-1, keepdims=True))
    return dx, dh, dh * n, dn * xh


def _act_specs(ts, d, n):
    return [pl.BlockSpec((1, ts, d), lambda b, t: (b, t, 0)) for _ in range(n)]


def _vec_spec(d):
    return pl.BlockSpec((1, 1, d), lambda b, t: (b, 0, 0))


def _par_spec(d):
    return pl.BlockSpec((1, d), lambda b, t: (0, 0))


def _norm_mod_fwd(x3, g, sh, sc):
    bl, s, d = x3.shape
    ts = _tile(s, 512, SUBLANES)

    def kern(x_ref, g_ref, sh_ref, sc_ref, h_ref):
        _, _, n = _norm_parts(x_ref[0], g_ref[...])
        h_ref[0] = (n * (1.0 + sc_ref[0]) + sh_ref[0]).astype(BF16)

    return pl.pallas_call(
        kern, name="norm_mod_fwd", grid=(bl, s // ts),
        in_specs=_act_specs(ts, d, 1) + [_par_spec(d), _vec_spec(d), _vec_spec(d)],
        out_specs=_act_specs(ts, d, 1)[0],
        out_shape=jax.ShapeDtypeStruct((bl, s, d), BF16),
        compiler_params=_cp(2),
    )(x3, g, sh, sc)


def _resid_norm_fwd(x3, y3, gate, g, sh, sc):
    bl, s, d = x3.shape
    ts = _tile(s, 512, SUBLANES)

    def kern(x_ref, y_ref, gate_ref, g_ref, sh_ref, sc_ref, x1_ref, h_ref):
        x1 = x_ref[0] + gate_ref[0] * y_ref[0]
        x1_ref[0] = x1
        _, _, n = _norm_parts(x1, g_ref[...])
        h_ref[0] = (n * (1.0 + sc_ref[0]) + sh_ref[0]).astype(BF16)

    return pl.pallas_call(
        kern, name="resid_norm_fwd", grid=(bl, s // ts),
        in_specs=_act_specs(ts, d, 2) + [_vec_spec(d), _par_spec(d), _vec_spec(d), _vec_spec(d)],
        out_specs=_act_specs(ts, d, 2),
        out_shape=[jax.ShapeDtypeStruct((bl, s, d), F32), jax.ShapeDtypeStruct((bl, s, d), BF16)],
        compiler_params=_cp(2),
    )(x3, y3, gate, g, sh, sc)


def _norm_mod_bwd(dh3, x3, dres3, g, sc, name, gated=None):
    bl, s, d = x3.shape
    ts = _tile(s, 512, SUBLANES)

    def kern(dh_ref, x_ref, dres_ref, g_ref, sc_ref, *rest):
        dx_ref, dsh_ref, dsc_ref, dg_ref = rest[-4:]
        b, t = pl.program_id(0), pl.program_id(1)
        gv = g_ref[...]
        r, xh, _ = _norm_parts(x_ref[0], gv)
        dx, a, bb, cc = _norm_bwd_parts(dh_ref[0], xh, r, gv, sc_ref[0])
        dx = dres_ref[0] + dx
        dx_ref[0] = dx

        @pl.when(t == 0)
        def _():
            dsh_ref[...] = jnp.zeros_like(dsh_ref)
            dsc_ref[...] = jnp.zeros_like(dsc_ref)

        @pl.when((t == 0) & (b == 0))
        def _():
            dg_ref[...] = jnp.zeros_like(dg_ref)

        dsh_ref[0] += jnp.sum(a, axis=0, keepdims=True)
        dsc_ref[0] += jnp.sum(bb, axis=0, keepdims=True)
        dg_ref[...] += jnp.sum(cc, axis=0, keepdims=True)
        if gated is not None:
            y_ref, gate_ref, dy_ref, dgate_ref = rest[:4]
            dy_ref[0] = (gate_ref[0] * dx).astype(BF16)

            @pl.when(t == 0)
            def _():
                dgate_ref[...] = jnp.zeros_like(dgate_ref)

            dgate_ref[0] += jnp.sum(dx * y_ref[0], axis=0, keepdims=True)

    extra_in = [] if gated is None else _act_specs(ts, d, 1) + [_vec_spec(d)]
    extra_out = [] if gated is None else [_act_specs(ts, d, 1)[0], _vec_spec(d)]
    extra_shape = [] if gated is None else [jax.ShapeDtypeStruct((bl, s, d), BF16), jax.ShapeDtypeStruct((bl, 1, d), F32)]
    return pl.pallas_call(
        kern, name=name, grid=(bl, s // ts),
        in_specs=_act_specs(ts, d, 3) + [_par_spec(d), _vec_spec(d)] + extra_in,
        out_specs=extra_out + [_act_specs(ts, d, 1)[0], _vec_spec(d), _vec_spec(d), _par_spec(d)],
        out_shape=extra_shape + [jax.ShapeDtypeStruct((bl, s, d), F32), jax.ShapeDtypeStruct((bl, 1, d), F32),
                                 jax.ShapeDtypeStruct((bl, 1, d), F32), jax.ShapeDtypeStruct((1, d), F32)],
        compiler_params=_cp(2),
    )(dh3, x3, dres3, g, sc, *(gated or ()))


def _final_fwd_bwd(x1, yf, gate2, g, shf, scf, tgt):
    bl, s, d = x1.shape
    ts = _tile(s, 512, SUBLANES)

    def kern(x1_ref, yf_ref, gate_ref, g_ref, sh_ref, sc_ref, tgt_ref,
             dx_ref, dyf_ref, dgate_ref, dsh_ref, dsc_ref, dg_ref, loss_ref):
        b, t = pl.program_id(0), pl.program_id(1)
        gv, sc, yf, gate = g_ref[...], sc_ref[0], yf_ref[0], gate_ref[0]
        x2 = x1_ref[0] + gate * yf
        r, xh, n = _norm_parts(x2, gv)
        err = n * (1.0 + sc) + sh_ref[0] - tgt_ref[0]
        dx, a, bb, cc = _norm_bwd_parts(err * (1.0 / d), xh, r, gv, sc)
        dx_ref[0] = dx
        dyf_ref[0] = (gate * dx).astype(BF16)

        @pl.when(t == 0)
        def _():
            dsh_ref[...] = jnp.zeros_like(dsh_ref)
            dsc_ref[...] = jnp.zeros_like(dsc_ref)
            dgate_ref[...] = jnp.zeros_like(dgate_ref)

        @pl.when((t == 0) & (b == 0))
        def _():
            dg_ref[...] = jnp.zeros_like(dg_ref)
            loss_ref[...] = jnp.zeros_like(loss_ref)

        dsh_ref[0] += jnp.sum(a, axis=0, keepdims=True)
        dsc_ref[0] += jnp.sum(bb, axis=0, keepdims=True)
        dg_ref[...] += jnp.sum(cc, axis=0, keepdims=True)
        dgate_ref[0] += jnp.sum(dx * yf, axis=0, keepdims=True)
        tok = jnp.mean(err * err, axis=-1, keepdims=True)
        loss_ref[...] += 0.5 * jnp.sum(tok, axis=0, keepdims=True)

    vec = jax.ShapeDtypeStruct((bl, 1, d), F32)
    return pl.pallas_call(
        kern, name="final_fwd_bwd", grid=(bl, s // ts),
        in_specs=_act_specs(ts, d, 2) + [_vec_spec(d), _par_spec(d), _vec_spec(d), _vec_spec(d)] + _act_specs(ts, d, 1),
        out_specs=_act_specs(ts, d, 2) + [_vec_spec(d), _vec_spec(d), _vec_spec(d), _par_spec(d),
                                          pl.BlockSpec((1, 1), lambda b, t: (0, 0))],
        out_shape=[jax.ShapeDtypeStruct((bl, s, d), F32), jax.ShapeDtypeStruct((bl, s, d), BF16), vec, vec, vec,
                   jax.ShapeDtypeStruct((1, d), F32), jax.ShapeDtypeStruct((1, 1), F32)],
        compiler_params=_cp(2),
    )(x1, yf, gate2, g, shf, scf, tgt)


def _rnn_gates(xr, cw, cb, wa, ba, wi, bi, lam):
    kw = cw.shape[0]
    xc = cb
    for k in range(kw):
        xc = xc + _shift_down(xr, kw - 1 - k) * cw[k:k + 1, :]
    r = _sigmoid(_dot(xc, wa, _NN) + ba)
    i = _sigmoid(_dot(xc, wi, _NN) + bi)
    sp = _softplus(-lam)
    log_a = -LRU_C * r * sp
    a = jnp.exp(log_a)
    mult = jnp.sqrt(-_expm1(2.0 * log_a, a * a))
    return xc, r, i, sp, a, mult


def _segment_scan(a_s, u_s, h_s, p_s, reverse):
    s, c = a_s.shape
    seg = s // SCAN_SEGMENTS

    unroll = math.gcd(seg, 8)

    def steps(n, carry):
        h, p = carry
        for j in range(unroll):
            t = n * unroll + j
            t = (seg - 1 - t) if reverse else t
            av = a_s[pl.ds(t, SCAN_SEGMENTS, stride=seg), :]
            uv = u_s[pl.ds(t, SCAN_SEGMENTS, stride=seg), :]
            h = av * h + uv
            p = p * av
            h_s[pl.ds(t, SCAN_SEGMENTS, stride=seg), :] = h
            p_s[pl.ds(t, SCAN_SEGMENTS, stride=seg), :] = p
        return h, p

    lax.fori_loop(0, seg // unroll, steps, (jnp.zeros((SCAN_SEGMENTS, c), F32), jnp.ones((SCAN_SEGMENTS, c), F32)))
    carry = jnp.zeros((1, c), F32)
    order = range(SCAN_SEGMENTS - 1, -1, -1) if reverse else range(SCAN_SEGMENTS)
    for j in order:
        rows = pl.ds(j * seg, seg)
        fixed = h_s[rows, :] + p_s[rows, :] * carry
        h_s[rows, :] = fixed
        carry = fixed[0:1, :] if reverse else fixed[seg - 1:seg, :]


def _rnn_specs(s, rb, nb):
    act = lambda off: pl.BlockSpec((1, s, rb), lambda b, n, off=off: (b, 0, off + n))
    par = pl.BlockSpec((1, rb), lambda b, n: (0, n))
    wsp = pl.BlockSpec((1, rb, rb), lambda b, n: (n, 0, 0))
    return act, par, wsp


def _rnn_fwd(zr3, cw, cb, wa, ba, wi, bi, lam):
    bl, s, two = zr3.shape
    nb, rb, _ = wa.shape
    dr = nb * rb
    kw = cw.shape[0]
    act, par, wsp = _rnn_specs(s, rb, nb)

    def kern(xr_ref, gr_ref, cw_ref, cb_ref, wa_ref, ba_ref, wi_ref, bi_ref, lam_ref,
             y_ref, h_ref, xc_ref, r_ref, i_ref, a_ref, mult_ref, a_s, u_s, h_s, p_s):
        xc, r, i, sp, a, mult = _rnn_gates(xr_ref[0], cw_ref[...], cb_ref[...], wa_ref[0], ba_ref[...],
                                           wi_ref[0], bi_ref[...], lam_ref[...])
        for ref, val in ((xc_ref, xc), (r_ref, r), (i_ref, i), (a_ref, a), (mult_ref, mult)):
            ref[0] = val
        a_s[...] = a
        u_s[...] = mult * (i * xc)
        _segment_scan(a_s, u_s, h_s, p_s, reverse=False)
        h = h_s[...]
        h_ref[0] = h
        y_ref[0] = (_gelu(gr_ref[0]) * h).astype(BF16)

    kept = jax.ShapeDtypeStruct((bl, s, dr), F32)
    return pl.pallas_call(
        kern, name="rnn_fwd", grid=(bl, nb),
        in_specs=[act(0), act(nb), pl.BlockSpec((kw, rb), lambda b, n: (0, n)), par, wsp, par, wsp, par, par],
        out_specs=[act(0)] * 7,
        out_shape=[jax.ShapeDtypeStruct((bl, s, dr), BF16)] + [kept] * 6,
        scratch_shapes=[pltpu.VMEM((s, rb), F32)] * 4,
        compiler_params=_cp(2),
    )(zr3, zr3, cw, cb, wa, ba, wi, bi, lam)


def _rnn_bwd(zr3, kept, dy3, cw, wa, wi, lam):
    bl, s, _ = zr3.shape
    nb, rb, _ = wa.shape
    dr = nb * rb
    kw = cw.shape[0]
    act = lambda off: pl.BlockSpec((1, s, rb), lambda n, b, off=off: (b, 0, off + n))
    par = pl.BlockSpec((1, rb), lambda n, b: (0, n))
    wsp = pl.BlockSpec((1, rb, rb), lambda n, b: (n, 0, 0))
    cws = pl.BlockSpec((kw, rb), lambda n, b: (0, n))

    def kern(xr_ref, gr_ref, h_ref, xc_ref, r_ref, i_ref, a_ref, mult_ref, dy_ref, cw_ref, wa_ref, wi_ref, lam_ref,
             dxr_ref, dgr_ref, dcw_ref, dcb_ref, dwa_ref, dba_ref, dwi_ref, dbi_ref, dlam_ref, a_s, u_s, h_s, p_s):
        b = pl.program_id(1)
        xr, cwv, lamv = xr_ref[0], cw_ref[...], lam_ref[...]
        wav, wiv = wa_ref[0], wi_ref[0]
        xc, r, i, a, mult = xc_ref[0], r_ref[0], i_ref[0], a_ref[0], mult_ref[0]
        sp = _softplus(-lamv)
        h, dy = h_ref[0], dy_ref[0]
        ge, dge = _gelu_and_grad(gr_ref[0])
        dgr_ref[0] = (dy * h * dge).astype(BF16)
        a_s[...] = _shift_up(a, 1)
        u_s[...] = dy * ge
        _segment_scan(a_s, u_s, h_s, p_s, reverse=True)
        g = h_s[...]
        da = g * _shift_down(h, 1)
        ix = i * xc
        dlog_a = da * a + (g * ix) * (-(a * a) / mult)
        di = g * mult * xc
        dpa = (dlog_a * (-LRU_C * sp)) * r * (1.0 - r)
        dpi = di * i * (1.0 - i)
        dxc = g * mult * i + _dot(dpa, wav, _NT) + _dot(dpi, wiv, _NT)
        dxr = jnp.zeros_like(dxc)
        dcw_rows = []
        for k in range(kw):
            dxr = dxr + _shift_up(dxc, kw - 1 - k) * cwv[k:k + 1, :]
            dcw_rows.append(jnp.sum(dxc * _shift_down(xr, kw - 1 - k), axis=0, keepdims=True))
        dxr_ref[0] = dxr.astype(BF16)

        @pl.when(b == 0)
        def _():
            for ref in (dcw_ref, dcb_ref, dwa_ref, dba_ref, dwi_ref, dbi_ref, dlam_ref):
                ref[...] = jnp.zeros_like(ref)

        for k in range(kw):
            dcw_ref[k:k + 1, :] += dcw_rows[k]
        dcb_ref[...] += jnp.sum(dxc, axis=0, keepdims=True)
        dwa_ref[0] += _dot(xc, dpa, _TN)
        dwi_ref[0] += _dot(xc, dpi, _TN)
        dba_ref[...] += jnp.sum(dpa, axis=0, keepdims=True)
        dbi_ref[...] += jnp.sum(dpi, axis=0, keepdims=True)
        dsp = jnp.sum(dlog_a * (-LRU_C * r), axis=0, keepdims=True)
        dlam_ref[...] += dsp * (-_sigmoid(-lamv))

    vec = jax.ShapeDtypeStruct((1, dr), F32)
    wsh = jax.ShapeDtypeStruct((nb, rb, rb), F32)
    return pl.pallas_call(
        kern, name="rnn_bwd", grid=(nb, bl),
        in_specs=[act(0), act(nb)] + [act(0)] * 7 + [cws, wsp, wsp, par],
        out_specs=[act(0), act(0), cws, par, wsp, par, wsp, par, par],
        out_shape=[jax.ShapeDtypeStruct((bl, s, dr), BF16), jax.ShapeDtypeStruct((bl, s, dr), BF16),
                   jax.ShapeDtypeStruct((kw, dr), F32), vec, wsh, vec, wsh, vec, vec],
        scratch_shapes=[pltpu.VMEM((s, rb), F32)] * 4,
        compiler_params=_cp(2),
    )(zr3, zr3, *kept, dy3, cw, wa, wi, lam)


def _tri(n, upper):
    r = lax.broadcasted_iota(jnp.int32, (n, n), 0)
    c = lax.broadcasted_iota(jnp.int32, (n, n), 1)
    return jnp.where((c >= r) if upper else (c <= r), 1.0, 0.0).astype(F32)


def _fgate_fwd(zf3, bf):
    bl, s, w = zf3.shape
    ch = _tile(s, 256, SUBLANES)

    def kern(z_ref, b_ref, f_ref):
        tri = _tri(ch, upper=False)
        carry = jnp.zeros((1, w), F32)
        for j in range(s // ch):
            rows = pl.ds(j * ch, ch)
            lf = -_softplus(-(z_ref[0, rows, :] + b_ref[...]))
            out = jnp.dot(tri, lf, precision=lax.Precision.HIGHEST, preferred_element_type=F32) + carry
            f_ref[0, rows, :] = out
            carry = out[ch - 1:ch, :]

    return pl.pallas_call(
        kern, name="fgate_fwd", grid=(bl,),
        in_specs=[pl.BlockSpec((1, s, w), lambda b: (b, 0, 0)), pl.BlockSpec((1, w), lambda b: (0, 0))],
        out_specs=pl.BlockSpec((1, s, w), lambda b: (b, 0, 0)),
        out_shape=jax.ShapeDtypeStruct((bl, s, w), F32),
        compiler_params=_cp(1),
    )(zf3, bf)


def _fgate_bwd(dfk3, dfq3, zf3, bf):
    bl, s, w = zf3.shape
    ch = _tile(s, 256, SUBLANES)

    def kern(dfk_ref, dfq_ref, z_ref, b_ref, dz_ref, db_ref):
        b = pl.program_id(0)
        tri = _tri(ch, upper=True)
        carry = jnp.zeros((1, w), F32)
        dbsum = jnp.zeros((1, w), F32)
        for j in range(s // ch - 1, -1, -1):
            rows = pl.ds(j * ch, ch)
            df = dfk_ref[0, rows, :] + dfq_ref[0, rows, :]
            dlf = jnp.dot(tri, df, precision=lax.Precision.HIGHEST, preferred_element_type=F32) + carry
            carry = dlf[0:1, :]
            dz = dlf * _sigmoid(-(z_ref[0, rows, :] + b_ref[...]))
            dz_ref[0, rows, :] = dz.astype(BF16)
            dbsum = dbsum + jnp.sum(dz, axis=0, keepdims=True)

        @pl.when(b == 0)
        def _():
            db_ref[...] = jnp.zeros_like(db_ref)

        db_ref[...] += dbsum

    return pl.pallas_call(
        kern, name="fgate_bwd", grid=(bl,),
        in_specs=[pl.BlockSpec((1, s, w), lambda b: (b, 0, 0))] * 3 + [pl.BlockSpec((1, w), lambda b: (0, 0))],
        out_specs=[pl.BlockSpec((1, s, w), lambda b: (b, 0, 0)), pl.BlockSpec((1, w), lambda b: (0, 0))],
        out_shape=[jax.ShapeDtypeStruct((bl, s, w), BF16), jax.ShapeDtypeStruct((1, w), F32)],
        compiler_params=_cp(1),
    )(dfk3, dfq3, zf3, bf)


def _lanes(col, width):
    return col if width == LANES else jnp.concatenate([col] * (width // LANES), axis=1)


def _causal(sc, row0, col0, transposed):
    r = lax.broadcasted_iota(jnp.int32, sc.shape, 0) + row0
    c = lax.broadcasted_iota(jnp.int32, sc.shape, 1) + col0
    return jnp.where((c >= r) if transposed else (r >= c), sc, NEG_BIG)


def _attn_fwd(q3, kv3, fcol, frow, nh):
    bl, s, da = q3.shape
    dh = da // nh
    tq = _tile(s, 512, LANES)
    nq = s // tq

    def kern(iq_tab, ik_tab, q_ref, k_ref, v_ref, fk_ref, fq_ref, o_ref, lse_ref, m_s, l_s, acc):
        iq, ik = iq_tab[pl.program_id(2)], ik_tab[pl.program_id(2)]

        @pl.when(ik == 0)
        def _():
            m_s[...] = jnp.full_like(m_s, NEG_BIG)
            l_s[...] = jnp.zeros_like(l_s)
            acc[...] = jnp.zeros_like(acc)

        def block(masked):
            st = _dot(k_ref[0], q_ref[0], _NT) - _lanes(fk_ref[0], tq) + fq_ref[0]
            if masked:
                st = _causal(st, ik * tq, iq * tq, True)
            m_old = m_s[...]
            m_new = jnp.maximum(m_old, jnp.max(st, axis=0, keepdims=True))
            alpha = jnp.exp(m_old - m_new)
            pt = jnp.exp(st - m_new)
            l_s[...] = alpha * l_s[...] + jnp.sum(pt, axis=0, keepdims=True)
            acc[...] = alpha * acc[...] + _dot(v_ref[0], pt, _TN)
            m_s[...] = m_new

        pl.when(ik < iq)(functools.partial(block, False))

        @pl.when(ik == iq)
        def _():
            block(True)
            l = l_s[...]
            o_ref[0] = (acc[...] / l).T
            lse_ref[0] = m_s[...] + jnp.log(l)

    pairs = [(i, j) for i in range(nq) for j in range(i + 1)]
    iq_tab, ik_tab = (jnp.asarray(col, jnp.int32) for col in zip(*pairs))
    qmap = lambda b, h, p, iqt, ikt: (b, iqt[p], h)
    kmap = lambda off: (lambda b, h, p, iqt, ikt: (b, ikt[p], off + h))
    return pl.pallas_call(
        kern, name="attn_fwd",
        grid_spec=pltpu.PrefetchScalarGridSpec(
            num_scalar_prefetch=2, grid=(bl, nh, len(pairs)),
            in_specs=[pl.BlockSpec((1, tq, dh), qmap), pl.BlockSpec((1, tq, dh), kmap(0)), pl.BlockSpec((1, tq, dh), kmap(nh)),
                      pl.BlockSpec((1, tq, LANES), lambda b, h, p, iqt, ikt: (b * nh + h, ikt[p], 0)),
                      pl.BlockSpec((1, 1, tq), lambda b, h, p, iqt, ikt: (b * nh + h, 0, iqt[p]))],
            out_specs=[pl.BlockSpec((1, tq, dh), qmap),
                       pl.BlockSpec((1, 1, tq), lambda b, h, p, iqt, ikt: (b * nh + h, 0, iqt[p]))],
            scratch_shapes=[pltpu.VMEM((1, tq), F32), pltpu.VMEM((1, tq), F32), pltpu.VMEM((dh, tq), F32)]),
        out_shape=[jax.ShapeDtypeStruct((bl, s, da), F32), jax.ShapeDtypeStruct((bl * nh, 1, s), F32)],
        compiler_params=_cp(3),
    )(iq_tab, ik_tab, q3, kv3, kv3, fcol, frow)


def _attn_bwd(q3, kv3, do3, o3, lse_row, fcol, frow, nh, scale):
    bl, s, da = q3.shape
    dh = da // nh
    tk = _tile(s, 512, LANES)
    nk = s // tk

    pairs = [(j, i) for j in range(nk) for i in range(j, nk)]

    def kern(ik_tab, iq_tab, q_ref, k_ref, v_ref, do_ref, o_ref, lse_ref, fk_ref, fq_ref, dq_ref, dk_ref, dv_ref, dfk_ref,
             dfq_ref, dq_acc, dk_acc, dv_acc, dfq_acc, delta_s):
        step = pl.program_id(2)
        ik, iq = ik_tab[step], iq_tab[step]
        qrow = pl.ds(iq, 1)

        @pl.when(step == 0)
        def _():
            dq_acc[...] = jnp.zeros_like(dq_acc)
            dfq_acc[...] = jnp.zeros_like(dfq_acc)

        @pl.when(ik == 0)
        def _():
            prod = do_ref[0] * o_ref[0]
            rows = lax.dot_general(jnp.ones((SUBLANES, dh), F32), prod, (_NT, ((), ())),
                                   precision=lax.Precision.HIGHEST, preferred_element_type=F32)
            delta_s[qrow, :] = rows[0:1, :]

        @pl.when(iq == ik)
        def _():
            dk_acc[...] = jnp.zeros_like(dk_acc)
            dv_acc[...] = jnp.zeros_like(dv_acc)

        def block(masked):
            q = q_ref[0]
            st = _dot(k_ref[0], q, _NT) - _lanes(fk_ref[0], tk) + fq_ref[0]
            if masked:
                st = _causal(st, ik * tk, iq * tk, True)
            pt = jnp.exp(st - lse_ref[0])
            dv_acc[...] += _dot(pt, do_ref[0], _NN)
            dpt = _dot(v_ref[0], do_ref[0], _NT)
            dst = (pt * (dpt - delta_s[qrow, :])).astype(BF16)
            q_ones = jnp.concatenate([q, jnp.ones_like(q)], axis=1)
            dk_acc[...] += _dot(dst, q_ones, _NN)
            qrows = pl.ds(pl.multiple_of(iq * tk, tk), tk)
            dq_acc[qrows, :] += _dot(dst, k_ref[0], _TN)
            dfq_acc[qrow, :] += jnp.sum(dst.astype(F32), axis=0, keepdims=True)

        pl.when(iq > ik)(functools.partial(block, False))
        pl.when(iq == ik)(functools.partial(block, True))

        @pl.when(iq == nk - 1)
        def _():
            ext = dk_acc[...]
            dk_ref[0] = ext[:, :dh].astype(BF16)
            dfk_ref[0] = -ext[:, dh:]
            dv_ref[0] = dv_acc[...].astype(BF16)

        @pl.when(step == len(pairs) - 1)
        def _():
            dq_ref[0] = (dq_acc[...] * scale).astype(BF16)
            dfq_ref[0] = dfq_acc[...]

    ik_tab, iq_tab = (jnp.asarray(col, jnp.int32) for col in zip(*pairs))
    qmap = lambda b, h, p, ikt, iqt: (b, iqt[p], h)
    omap = lambda b, h, p, ikt, iqt: (b, jnp.where(ikt[p] == 0, iqt[p], 0), h)
    rmap = lambda b, h, p, ikt, iqt: (b * nh + h, 0, iqt[p])
    kmap = lambda off: (lambda b, h, p, ikt, iqt: (b, ikt[p], off + h))
    bmap = lambda b, h, p, ikt, iqt: (b * nh + h, ikt[p], 0)
    return pl.pallas_call(
        kern, name="attn_bwd",
        grid_spec=pltpu.PrefetchScalarGridSpec(
            num_scalar_prefetch=2, grid=(bl, nh, len(pairs)),
            in_specs=[pl.BlockSpec((1, tk, dh), qmap), pl.BlockSpec((1, tk, dh), kmap(0)), pl.BlockSpec((1, tk, dh), kmap(nh)),
                      pl.BlockSpec((1, tk, dh), qmap), pl.BlockSpec((1, tk, dh), omap), pl.BlockSpec((1, 1, tk), rmap),
                      pl.BlockSpec((1, tk, LANES), bmap), pl.BlockSpec((1, 1, tk), rmap)],
            out_specs=[pl.BlockSpec((1, s, dh), lambda b, h, p, ikt, iqt: (b, 0, h)),
                       pl.BlockSpec((1, tk, dh), kmap(0)), pl.BlockSpec((1, tk, dh), kmap(0)),
                       pl.BlockSpec((1, tk, LANES), bmap),
                       pl.BlockSpec((1, nk, tk), lambda b, h, p, ikt, iqt: (b * nh + h, 0, 0))],
            scratch_shapes=[pltpu.VMEM((s, dh), F32), pltpu.VMEM((tk, 2 * dh), F32), pltpu.VMEM((tk, dh), F32),
                            pltpu.VMEM((nk, tk), F32), pltpu.VMEM((nk, tk), F32)]),
        out_shape=[jax.ShapeDtypeStruct((bl, s, da), BF16), jax.ShapeDtypeStruct((bl, s, da), BF16),
                   jax.ShapeDtypeStruct((bl, s, da), BF16), jax.ShapeDtypeStruct((bl * nh, s, LANES), F32),
                   jax.ShapeDtypeStruct((bl * nh, nk, tk), F32)],
        compiler_params=_cp(3),
    )(ik_tab, iq_tab, q3, kv3, kv3, do3, o3, lse_row, fcol, frow)


def _merge_fwd(mg3, pr3, pa3):
    bl, s, d = pr3.shape
    ts = _tile(s, 512, SUBLANES)
    half = lambda j: pl.BlockSpec((1, ts, d), lambda b, t, j=j: (b, t, j))

    def kern(mr_ref, ma_ref, pr_ref, pa_ref, o_ref):
        o_ref[0] = (_sigmoid(mr_ref[0]) * pr_ref[0] + _sigmoid(ma_ref[0]) * pa_ref[0]).astype(BF16)

    return pl.pallas_call(
        kern, name="merge_fwd", grid=(bl, s // ts),
        in_specs=[half(0), half(1)] + _act_specs(ts, d, 2), out_specs=_act_specs(ts, d, 1)[0],
        out_shape=jax.ShapeDtypeStruct((bl, s, d), BF16), compiler_params=_cp(2),
    )(mg3, mg3, pr3, pa3)


def _merge_bwd(dm3, mg3, pr3, pa3):
    bl, s, d = pr3.shape
    ts = _tile(s, 512, SUBLANES)
    half = lambda j: pl.BlockSpec((1, ts, d), lambda b, t, j=j: (b, t, j))

    def kern(dm_ref, mr_ref, ma_ref, pr_ref, pa_ref, dpr_ref, dpa_ref, dmr_ref, dma_ref):
        dm = dm_ref[0]
        gr, ga = _sigmoid(mr_ref[0]), _sigmoid(ma_ref[0])
        dpr_ref[0] = (gr * dm).astype(BF16)
        dpa_ref[0] = (ga * dm).astype(BF16)
        dmr_ref[0] = (dm * pr_ref[0] * gr * (1.0 - gr)).astype(BF16)
        dma_ref[0] = (dm * pa_ref[0] * ga * (1.0 - ga)).astype(BF16)

    return pl.pallas_call(
        kern, name="merge_bwd", grid=(bl, s // ts),
        in_specs=_act_specs(ts, d, 1) + [half(0), half(1)] + _act_specs(ts, d, 2), out_specs=_act_specs(ts, d, 4),
        out_shape=[jax.ShapeDtypeStruct((bl, s, d), BF16)] * 4, compiler_params=_cp(2),
    )(dm3, mg3, mg3, pr3, pa3)


def _ffn_conv(gf, cw, cb):
    kw = cw.shape[0]
    y = cb
    for k in range(kw):
        y = y + _shift_down(gf, kw - 1 - k) * cw[k:k + 1, :]
    return y


def _ffn_act_fwd(up3, cw, cb):
    bl, s, two = up3.shape
    dff = two // 2
    kw = cw.shape[0]
    tc = _tile(dff, 512, LANES)
    nc = dff // tc

    def kern(gf_ref, uf_ref, cw_ref, cb_ref, o_ref):
        o_ref[0] = (_gelu(_ffn_conv(gf_ref[0], cw_ref[...], cb_ref[...])) * uf_ref[0]).astype(BF16)

    act = lambda off: pl.BlockSpec((1, s, tc), lambda b, j, off=off: (b, 0, off + j))
    return pl.pallas_call(
        kern, name="ffn_act_fwd", grid=(bl, nc),
        in_specs=[act(0), act(nc), pl.BlockSpec((kw, tc), lambda b, j: (0, j)), pl.BlockSpec((1, tc), lambda b, j: (0, j))],
        out_specs=act(0), out_shape=jax.ShapeDtypeStruct((bl, s, dff), BF16), compiler_params=_cp(2),
    )(up3, up3, cw, cb)


def _ffn_act_bwd(up3, dact3, cw, cb):
    bl, s, two = up3.shape
    dff = two // 2
    kw = cw.shape[0]
    tc = _tile(dff, 256, LANES)
    nc = dff // tc

    def kern(gf_ref, uf_ref, da_ref, cw_ref, cb_ref, dgf_ref, duf_ref, dcw_ref, dcb_ref):
        b = pl.program_id(1)
        gf, cwv, da = gf_ref[0], cw_ref[...], da_ref[0]
        ge, dge = _gelu_and_grad(_ffn_conv(gf, cwv, cb_ref[...]))
        duf_ref[0] = (da * ge).astype(BF16)
        dgc = da * uf_ref[0] * dge
        dgf = jnp.zeros_like(dgc)
        rows = []
        for k in range(kw):
            dgf = dgf + _shift_up(dgc, kw - 1 - k) * cwv[k:k + 1, :]
            rows.append(jnp.sum(dgc * _shift_down(gf, kw - 1 - k), axis=0, keepdims=True))
        dgf_ref[0] = dgf.astype(BF16)

        @pl.when(b == 0)
        def _():
            dcw_ref[...] = jnp.zeros_like(dcw_ref)
            dcb_ref[...] = jnp.zeros_like(dcb_ref)

        for k in range(kw):
            dcw_ref[k:k + 1, :] += rows[k]
        dcb_ref[...] += jnp.sum(dgc, axis=0, keepdims=True)

    act = lambda off: pl.BlockSpec((1, s, tc), lambda j, b, off=off: (b, 0, off + j))
    cws = pl.BlockSpec((kw, tc), lambda j, b: (0, j))
    cbs = pl.BlockSpec((1, tc), lambda j, b: (0, j))
    return pl.pallas_call(
        kern, name="ffn_act_bwd", grid=(nc, bl),
        in_specs=[act(0), act(nc), act(0), cws, cbs], out_specs=[act(0), act(0), cws, cbs],
        out_shape=[jax.ShapeDtypeStruct((bl, s, dff), BF16), jax.ShapeDtypeStruct((bl, s, dff), BF16),
                   jax.ShapeDtypeStruct((kw, dff), F32), jax.ShapeDtypeStruct((1, dff), F32)],
        compiler_params=_cp(2),
    )(up3, up3, dact3, cw, cb)


_HBM = pl.BlockSpec(memory_space=pltpu.HBM)


def _place():
    x, y, c = lax.axis_index("x"), lax.axis_index("y"), lax.axis_index("c")
    chips = dict(me=2 * x + y, nx=2 * (1 - x) + y, ny=2 * x + (1 - y), diag=2 * (1 - x) + (1 - y))
    peers = dict(nx=(1 - x, y, c), ny=(x, 1 - y, c), sib=(x, y, 1 - c))
    return c, chips, peers


def _remote(src, dst, sems, k, to):
    return pltpu.make_async_remote_copy(src_ref=src, dst_ref=dst, send_sem=sems[0].at[k], recv_sem=sems[1].at[k],
                                        device_id=to, device_id_type=MESH)


RS_STEPS = 2


def _piece(q, idx, n=1):
    start = idx * q
    if not isinstance(start, int):
        start = pl.multiple_of(start, SUBLANES)
    return pl.ds(start, n * q)


def _all_gather_chips(xs, name):
    nt = len(xs)
    per = 9

    def body(*refs):
        x_refs, o_refs = refs[:nt], refs[nt:2 * nt]
        send_sems, recv_sems = refs[2 * nt:]
        c, chip, peer = _place()
        sems = (send_sems, recv_sems)
        me, nx, ny, dg = chip["me"], chip["nx"], chip["ny"], chip["diag"]
        sends = []

        def arrive(k, dst):
            _remote(dst, dst, sems, k, peer["sib"]).wait_recv()

        def pass_on(k, blk, to):
            cp = _remote(blk, blk, sems, k, peer[to])
            cp.start()
            sends.append(cp)

        for t in range(nt):
            q = xs[t].shape[0] // 4
            half = _piece(q, 2 * c, 2)
            for k, to in ((0, "nx"), (1, "ny")):
                cp = _remote(x_refs[t].at[half], o_refs[t].at[me, half], sems, per * t + k, peer[to])
                cp.start()
                sends.append(cp)
            cp = _remote(x_refs[t], o_refs[t].at[me], sems, per * t + 8, peer["sib"])
            cp.start()
            sends.append(cp)
        for t in range(nt):
            q, o, k0 = xs[t].shape[0] // 4, o_refs[t], per * t
            half, sub0, sub1 = _piece(q, 2 * c, 2), _piece(q, 2 * c), _piece(q, 2 * c + 1)
            arrive(k0 + 0, o.at[nx, half])
            pass_on(k0 + 2, o.at[nx, sub0], "ny")
            pass_on(k0 + 4, o.at[nx, half], "sib")
            arrive(k0 + 1, o.at[ny, half])
            pass_on(k0 + 3, o.at[ny, sub1], "nx")
            pass_on(k0 + 5, o.at[ny, half], "sib")
            arrive(k0 + 2, o.at[dg, sub0])
            pass_on(k0 + 6, o.at[dg, sub0], "sib")
            arrive(k0 + 3, o.at[dg, sub1])
            pass_on(k0 + 7, o.at[dg, sub1], "sib")
        for t in range(nt):
            q, o, k0 = xs[t].shape[0] // 4, o_refs[t], per * t
            arrive(k0 + 4, o.at[nx, _piece(q, 2 * (1 - c), 2)])
            arrive(k0 + 5, o.at[ny, _piece(q, 2 * (1 - c), 2)])
            arrive(k0 + 6, o.at[dg, _piece(q, 2 * (1 - c))])
            arrive(k0 + 7, o.at[dg, _piece(q, 2 * (1 - c) + 1)])
            arrive(k0 + 8, o.at[me])
        for cp in sends:
            cp.wait_send()

    return pl.pallas_call(
        body, name=name, in_specs=[_HBM] * nt, out_specs=[_HBM] * nt,
        out_shape=[jax.ShapeDtypeStruct((N_CHIPS,) + x.shape, x.dtype) for x in xs],
        scratch_shapes=[pltpu.SemaphoreType.DMA((per * nt,)), pltpu.SemaphoreType.DMA((per * nt,))],
    )(*xs)


def _all_gather_devices(xs, name):
    nt = len(xs)
    per = 7

    def body(*refs):
        x_refs, o_refs = refs[:nt], refs[nt:2 * nt]
        send_sems, recv_sems, local_sems = refs[2 * nt:]
        x, y, c = lax.axis_index("x"), lax.axis_index("y"), lax.axis_index("c")
        sems = (send_sems, recv_sems)
        sib = (x, y, 1 - c)
        chips = [(1 - x, y), (x, 1 - y), (1 - x, 1 - y)]
        slot = lambda px, py, pc: 4 * px + 2 * py + pc
        me = slot(x, y, c)
        sends, copies = [], []

        def arrive(k, dst):
            _remote(dst, dst, sems, k, sib).wait_recv()

        for t in range(nt):
            cp = pltpu.make_async_copy(x_refs[t], o_refs[t].at[me], local_sems.at[t])
            cp.start()
            copies.append(cp)
            for k, to in enumerate([sib] + [(*chip, c) for chip in chips]):
                cp = _remote(x_refs[t], o_refs[t].at[me], sems, per * t + k, to)
                cp.start()
                sends.append(cp)
        for t in range(nt):
            for j, chip in enumerate(chips):
                blk = o_refs[t].at[slot(*chip, c)]
                arrive(per * t + 1 + j, blk)
                cp = _remote(blk, blk, sems, per * t + 4 + j, sib)
                cp.start()
                sends.append(cp)
        for t in range(nt):
            arrive(per * t, o_refs[t].at[slot(x, y, 1 - c)])
            for j, chip in enumerate(chips):
                arrive(per * t + 4 + j, o_refs[t].at[slot(*chip, 1 - c)])
        for cp in sends:
            cp.wait_send()
        for cp in copies:
            cp.wait()

    return pl.pallas_call(
        body, name=name, in_specs=[_HBM] * nt, out_specs=[_HBM] * nt,
        out_shape=[jax.ShapeDtypeStruct((2 * N_CHIPS,) + a.shape, a.dtype) for a in xs],
        scratch_shapes=[pltpu.SemaphoreType.DMA((per * nt,)), pltpu.SemaphoreType.DMA((per * nt,)),
                        pltpu.SemaphoreType.DMA((nt,))],
    )(*xs)


def _exchange(name, xs, out_shapes, plan):
    nt = len(xs)

    def body(*refs):
        x_refs, o_refs = refs[:nt], refs[nt:2 * nt]
        send_sems, recv_sems = refs[2 * nt:]
        c, chip, peer = _place()
        cps = []
        for t in range(nt):
            for src, dst, to in plan(c, chip, x_refs[t], o_refs[t], xs[t].shape):
                cps.append(_remote(src, dst, (send_sems, recv_sems), len(cps), peer[to]))
        for cp in cps:
            cp.start()
        for cp in cps:
            cp.wait()

    n_copies = nt * len(plan(0, dict(me=0, nx=2, ny=1, diag=3), None, None, xs[0].shape, count_only=True))
    return pl.pallas_call(
        body, name=name, in_specs=[_HBM] * nt, out_specs=[_HBM] * nt,
        out_shape=[jax.ShapeDtypeStruct(s, x.dtype) for s, x in zip(out_shapes, xs)],
        scratch_shapes=[pltpu.SemaphoreType.DMA((n_copies,)), pltpu.SemaphoreType.DMA((n_copies,))],
    )(*xs)


def _plan_sibling(c, chip, g, out, shape, count_only=False):
    if count_only:
        return [None] * N_CHIPS
    q = shape[1] // 4
    return [(g.at[j, _piece(q, 2 * (1 - c), 2)], out.at[j], "sib") for j in range(N_CHIPS)]


def _plan_first(c, chip, p, out, shape, count_only=False):
    if count_only:
        return [None] * 4
    q = shape[1] // 2
    return [(p.at[chip["nx"], _piece(q, 0)], out.at[0], "nx"), (p.at[chip["diag"], _piece(q, 0)], out.at[1], "nx"),
            (p.at[chip["ny"], _piece(q, 1)], out.at[2], "ny"), (p.at[chip["diag"], _piece(q, 1)], out.at[3], "ny")]


def _plan_second(c, chip, p, out, shape, count_only=False):
    if count_only:
        return [None] * 2
    return [(p.at[1], out.at[0], "ny"), (p.at[3], out.at[1], "nx")]


def _rs_last(ps):
    nt = len(ps)

    def body(*refs):
        p_refs, o_refs = refs[:nt], refs[nt:2 * nt]
        send_sems, recv_sems = refs[2 * nt:]
        c, _, peer = _place()
        sems = (send_sems, recv_sems)
        cps = []
        for t in range(nt):
            q = ps[t].shape[0] // 4
            mine = _piece(q, 2 * c, 2)
            cps.append(_remote(p_refs[t].at[mine], o_refs[t].at[mine], sems, t, peer["sib"]))
            cps[-1].start()
        for t in range(nt):
            q = ps[t].shape[0] // 4
            theirs = _piece(q, 2 * (1 - c), 2)
            cps[t].wait_send()
            _remote(p_refs[t].at[theirs], o_refs[t].at[theirs], sems, t, peer["sib"]).wait_recv()

    return pl.pallas_call(
        body, name="rs_last", in_specs=[_HBM] * nt, out_specs=[_HBM] * nt,
        out_shape=[jax.ShapeDtypeStruct(p.shape, F32) for p in ps],
        input_output_aliases={t: t for t in range(nt)},
        scratch_shapes=[pltpu.SemaphoreType.DMA((nt,)), pltpu.SemaphoreType.DMA((nt,))],
    )(*ps)


def _add_stage(name, grid, a_list, b_list, a_map, b_map, tbs, out_shapes, out_map, out_dtype, prefetch=None):
    nt = len(a_list)
    lead = lambda shape: (None,) * (len(shape) - 2)

    def kern(*refs):
        refs = refs[(1 if prefetch is not None else 0):]
        for t in range(nt):
            refs[2 * nt + t][...] = (refs[t][...].astype(F32) + refs[nt + t][...].astype(F32)).astype(out_dtype)

    in_specs = [pl.BlockSpec(lead(a.shape) + (tb, a.shape[-1]), a_map) for a, tb in zip(a_list, tbs)]
    in_specs += [pl.BlockSpec(lead(b.shape) + (tb, b.shape[-1]), b_map) for b, tb in zip(b_list, tbs)]
    out_specs = [pl.BlockSpec(lead(s) + (tb, s[-1]), out_map) for s, tb in zip(out_shapes, tbs)]
    out_shape = [jax.ShapeDtypeStruct(s, out_dtype) for s in out_shapes]
    if prefetch is None:
        return pl.pallas_call(kern, name=name, grid=grid, in_specs=in_specs, out_specs=out_specs, out_shape=out_shape,
                              compiler_params=_cp(len(grid)))(*a_list, *b_list)
    return pl.pallas_call(
        kern, name=name,
        grid_spec=pltpu.PrefetchScalarGridSpec(num_scalar_prefetch=1, grid=grid, in_specs=in_specs, out_specs=out_specs),
        out_shape=out_shape, compiler_params=_cp(len(grid)))(prefetch, *a_list, *b_list)


def _reduce_scatter_chips(gs):
    x, y, c = lax.axis_index("x"), lax.axis_index("y"), lax.axis_index("c")
    me, nx, ny = 2 * x + y, 2 * (1 - x) + y, 2 * x + (1 - y)
    st = RS_STEPS
    unit = 4 * st * 2 * SUBLANES
    rows = [g.shape[1] for g in gs]
    gs = [jnp.pad(g, ((0, 0), (0, -g.shape[1] % unit), (0, 0))) for g in gs]
    qs = [g.shape[1] // 4 for g in gs]
    tbs = [q // st for q in qs]
    cols = [g.shape[2] for g in gs]
    core = jnp.reshape(c, (1,)).astype(jnp.int32)

    got = _exchange("rs_sibling", gs, [(N_CHIPS, 2 * q, cc) for q, cc in zip(qs, cols)], _plan_sibling)
    p0 = _add_stage("rs_add_sibling", (N_CHIPS, 2, st), gs, got,
                    lambda j, h, s, c_ref: (j, (2 * c_ref[0] + h) * st + s, 0), lambda j, h, s, c_ref: (j, h * st + s, 0),
                    tbs, [(N_CHIPS, 2 * q, cc) for q, cc in zip(qs, cols)], lambda j, h, s, c_ref: (j, h * st + s, 0),
                    BF16, prefetch=core)
    got = _exchange("rs_first", p0, [(4, q, cc) for q, cc in zip(qs, cols)], _plan_first)
    p1 = _add_stage("rs_add_first", (4, st), p0, got,
                    lambda k, s, i_ref: (i_ref[k], (k // 2) * st + s, 0), lambda k, s, i_ref: (k, s, 0),
                    tbs, [(4, q, cc) for q, cc in zip(qs, cols)], lambda k, s, i_ref: (k, s, 0),
                    BF16, prefetch=jnp.stack([me, ny, me, nx]).astype(jnp.int32))
    got = _exchange("rs_second", p1, [(2, q, cc) for q, cc in zip(qs, cols)], _plan_second)
    p2 = _add_stage("rs_add_second", (2, st), p1, got, lambda h, s, c_ref: (2 * h, s, 0), lambda h, s, c_ref: (h, s, 0),
                    tbs, [(4 * q, cc) for q, cc in zip(qs, cols)], lambda h, s, c_ref: ((2 * c_ref[0] + h) * st + s, 0),
                    F32, prefetch=core)
    return [out[:r] for out, r in zip(_rs_last(p2), rows)]


def _adamw(g, w, m, v, name):
    rows, cc = g.shape
    tr = _tile(rows, max(SUBLANES, (1 << 18) // cc), SUBLANES)
    k1 = 1.0 - ADAM_B1 ** ADAM_STEP
    k2 = 1.0 - ADAM_B2 ** ADAM_STEP

    def kern(g_ref, w_ref, m_ref, v_ref, d_ref, nm_ref, nv_ref):
        gv = g_ref[...]
        nm = ADAM_B1 * m_ref[...] + (1.0 - ADAM_B1) * gv
        nv = ADAM_B2 * v_ref[...] + (1.0 - ADAM_B2) * (gv * gv)
        nm_ref[...] = nm
        nv_ref[...] = nv
        d_ref[...] = -ADAM_LR * ((nm / k1) / (jnp.sqrt(nv / k2) + ADAM_EPS) + ADAM_WD * w_ref[...])

    spec = pl.BlockSpec((tr, cc), lambda t: (t, 0))
    return pl.pallas_call(
        kern, name=name, grid=(rows // tr,), in_specs=[spec] * 4, out_specs=[spec] * 3,
        out_shape=[jax.ShapeDtypeStruct((rows, cc), F32)] * 3, compiler_params=_cp(1),
    )(g, w, m, v)


def _flat_pad(parts, total):
    flat = jnp.concatenate([p.reshape(-1) for p in parts])
    return jnp.pad(flat, (0, total - flat.shape[0]))


def _split_flat(flat, shapes):
    out, pos = [], 0
    for shp in shapes:
        size = math.prod(shp)
        out.append(flat[pos:pos + size].reshape(shp))
        pos += size
    return out


def _cols_of_chunks(chunks, lo, hi):
    width = chunks.shape[2]
    parts = []
    for j in range(chunks.shape[0]):
        a, b = max(lo, j * width), min(hi, (j + 1) * width)
        if a < b:
            parts.append(chunks[j, :, a - j * width:b - j * width])
    return parts[0] if len(parts) == 1 else jnp.concatenate(parts, axis=1)


def _chunks_of_cols(segments, n_chunks):
    total = sum(s.shape[1] for s in segments)
    width = total // n_chunks
    chunks = []
    for j in range(n_chunks):
        lo, hi, pos, parts = j * width, (j + 1) * width, 0, []
        for s in segments:
            a, b = max(lo, pos), min(hi, pos + s.shape[1])
            if a < b:
                parts.append(s[:, a - pos:b - pos])
            pos += s.shape[1]
        chunks.append(parts[0] if len(parts) == 1 else jnp.concatenate(parts, axis=1))
    return jnp.stack(chunks)


_WEIGHTS = ['w_ada', 'b_ada', 'g_norm1', 'w_in', 'w_rnn_conv', 'b_rnn_conv', 'w_lru_a', 'b_lru_a', 'w_lru_i', 'b_lru_i',
            'lru_lambda', 'b_fgate', 'w_proj_rnn', 'w_proj_attn', 'w_out', 'g_norm2', 'w_ffn_up', 'w_ffn_conv',
            'b_ffn_conv', 'w_ffn_down', 'w_ada_final', 'b_ada_final', 'g_final']
_MATMUL = ['w_in', 'w_proj_rnn', 'w_proj_attn', 'w_out', 'w_ffn_up', 'w_ffn_down']
_ADA = ['w_ada', 'w_ada_final']
_ADA_BIAS = ['b_ada', 'b_ada_final']
_CONV = ['w_rnn_conv', 'w_ffn_conv']
_REPLICATED = [n for n in _WEIGHTS if n not in _MATMUL + _ADA + _ADA_BIAS + _CONV]


def kernel(x, c, w_ada, b_ada, g_norm1, w_in, w_rnn_conv, b_rnn_conv, w_lru_a, b_lru_a, w_lru_i, b_lru_i, lru_lambda, b_fgate, w_proj_rnn, w_proj_attn, w_out, g_norm2, w_ffn_up, w_ffn_conv, b_ffn_conv, w_ffn_down, w_ada_final, b_ada_final, g_final, loss_target, m_w_ada, m_b_ada, m_g_norm1, m_w_in, m_w_rnn_conv, m_b_rnn_conv, m_w_lru_a, m_b_lru_a, m_w_lru_i, m_b_lru_i, m_lru_lambda, m_b_fgate, m_w_proj_rnn, m_w_proj_attn, m_w_out, m_g_norm2, m_w_ffn_up, m_w_ffn_conv, m_b_ffn_conv, m_w_ffn_down, m_w_ada_final, m_b_ada_final, m_g_final, v_w_ada, v_b_ada, v_g_norm1, v_w_in, v_w_rnn_conv, v_b_rnn_conv, v_w_lru_a, v_b_lru_a, v_w_lru_i, v_b_lru_i, v_lru_lambda, v_b_fgate, v_w_proj_rnn, v_w_proj_attn, v_w_out, v_g_norm2, v_w_ffn_up, v_w_ffn_conv, v_b_ffn_conv, v_w_ffn_down, v_w_ada_final, v_b_ada_final, v_g_final):
    args = locals()
    shape_of = {n: args[n].shape for n in _WEIGHTS}

    def view(a):
        if a.ndim >= 3:
            return a[0]
        return a[None, :] if a.ndim == 1 else a

    w2 = {n: view(args[n]) for n in _WEIGHTS}
    m2 = {n: args['m_' + n].reshape(w2[n].shape) for n in _WEIGHTS}
    v2 = {n: args['v_' + n].reshape(w2[n].shape) for n in _WEIGHTS}

    bl, s, d = x.shape
    t = bl * s
    nh = b_fgate.shape[-1]
    nb, rb = w_lru_a.shape[1], w_lru_a.shape[2]
    dr = nb * rb
    da = w2['w_proj_attn'].shape[0] * N_CHIPS
    dh = da // nh
    dff = w2['w_ffn_conv'].shape[1] * N_CHIPS
    scale = dh ** -0.5
    chip = 2 * lax.axis_index("x") + lax.axis_index("y")
    dev = 2 * chip + lax.axis_index("c")

    names = list(_MATMUL)
    n_conv = sum(w2[n].size for n in _CONV)
    rows_conv = -(-n_conv // (FLAT_COLS * 32)) * 32
    conv_local = _flat_pad([w2[n] for n in _CONV], rows_conv * FLAT_COLS).reshape(rows_conv, FLAT_COLS)
    *weights_all, conv_all = _all_gather_chips([w2[n].astype(BF16) for n in names] + [conv_local], "ag_weights")
    gathered = dict(zip(names, weights_all))
    conv_all = conv_all.reshape(N_CHIPS, -1)
    conv_full, pos = {}, 0
    for n in _CONV:
        r, n4 = w2[n].shape
        blocks = conv_all[:, pos:pos + r * n4].reshape(N_CHIPS, r, n4)
        conv_full[n] = jnp.concatenate([blocks[j] for j in range(N_CHIPS)], axis=1)
        pos += r * n4
    rowmajor = lambda n: gathered[n].reshape(-1, gathered[n].shape[2])
    w_proj_rnn_f, w_proj_attn_f, w_out_f, w_ffn_down_f = (rowmajor(n) for n in ('w_proj_rnn', 'w_proj_attn', 'w_out', 'w_ffn_down'))
    up_chunk = w2['w_ffn_up'].shape[1]

    o_q, o_k, o_fl = 2 * dr, 2 * dr + da, 2 * dr + 3 * da
    o_mg = o_fl + nh
    g_in_w = gathered['w_in']
    w_rnn, w_q = _cols_of_chunks(g_in_w, 0, o_q), _cols_of_chunks(g_in_w, o_q, o_k)
    w_kv, w_mg = _cols_of_chunks(g_in_w, o_k, o_fl), _cols_of_chunks(g_in_w, o_mg, o_mg + 2 * d)
    w_fl = jnp.pad(_cols_of_chunks(g_in_w, o_fl, o_mg), ((0, 0), (0, LANES - nh)))
    w_rest = jnp.concatenate([w_q, w_kv, w_mg], axis=1)
    bf_pad = jnp.pad(w2['b_fgate'], ((0, 0), (0, LANES - nh)))

    nd = 2 * N_CHIPS
    c_act = _silu_pad(_all_gather_devices([c], "ag_cond")[0].reshape(nd * bl, d), nd * bl)
    my_cols = lambda a, n: lax.dynamic_slice_in_dim(a, chip * w2[n].shape[1], w2[n].shape[1], axis=1)
    mod_cols = [_mm(c_act, w2[n].astype(BF16), "nn", name=n + "_fwd", bias=my_cols(w2[b], n)) for n, b in zip(_ADA, _ADA_BIAS)]
    my_rows = lambda g: lax.dynamic_slice_in_dim(g, dev * bl, bl, axis=1).transpose(1, 0, 2).reshape(bl, -1)
    mod, modf = (my_rows(g) for g in _all_gather_chips(mod_cols, "ag_mod"))
    sh1, sc1, gt1, sh2, sc2, gt2 = [mod[:, i * d:(i + 1) * d].reshape(bl, 1, d) for i in range(6)]
    shf, scf = modf[:, :d].reshape(bl, 1, d), modf[:, d:].reshape(bl, 1, d)

    h1 = _norm_mod_fwd(x, w2['g_norm1'], sh1, sc1)
    h1f = h1.reshape(t, d)
    zr = _mm(h1f, w_rnn, "nn", name="in_rnn", tn=dr).reshape(bl, s, 2 * dr)
    q3 = _mm(h1f, w_q, "nn", name="in_q", out_dtype=BF16, scale=scale).reshape(bl, s, da)
    kv3 = _mm(h1f, w_kv, "nn", name="in_kv", out_dtype=BF16).reshape(bl, s, 2 * da)
    mg3 = _mm(h1f, w_mg, "nn", name="in_mg").reshape(bl, s, 2 * d)
    zf3 = _mm(h1f, w_fl, "nn", name="in_fl").reshape(bl, s, LANES)

    lru = (conv_full['w_rnn_conv'], w2['b_rnn_conv'], w2['w_lru_a'], w2['b_lru_a'], w2['w_lru_i'], w2['b_lru_i'], w2['lru_lambda'])
    y_rnn, *rnn_kept = _rnn_fwd(zr, *lru)

    f3 = _fgate_fwd(zf3, bf_pad)
    f_heads = f3[:, :, :nh].transpose(0, 2, 1).reshape(bl * nh, s)
    fcol = jnp.broadcast_to(f_heads[:, :, None], (bl * nh, s, LANES))
    frow = f_heads.reshape(bl * nh, 1, s)
    o3, lse_row = _attn_fwd(q3, kv3, fcol, frow, nh)

    pr3 = _mm(y_rnn.reshape(t, dr), w_proj_rnn_f, "nn", name="proj_rnn").reshape(bl, s, d)
    pa3 = _mm(o3.reshape(t, da), w_proj_attn_f, "nn", name="proj_attn").reshape(bl, s, d)
    merged = _merge_fwd(mg3, pr3, pa3)
    mo3 = _mm(merged.reshape(t, d), w_out_f, "nn", name="mix_out").reshape(bl, s, d)
    x1, h2 = _resid_norm_fwd(x, mo3, gt1, w2['g_norm2'], sh2, sc2)
    h2f = h2.reshape(t, d)
    up3 = _mm(h2f, gathered['w_ffn_up'], "nn", name="ffn_up", b_chunk=up_chunk, tn=up_chunk).reshape(bl, s, 2 * dff)
    act3 = _ffn_act_fwd(up3, conv_full['w_ffn_conv'], w2['b_ffn_conv'])
    yf3 = _mm(act3.reshape(t, dff), w_ffn_down_f, "nn", name="ffn_down").reshape(bl, s, d)

    dx2, dyf, dgt2, dshf, dscf, dg_final, loss_part = _final_fwd_bwd(x1, yf3, gt2, w2['g_final'], shf, scf, loss_target)
    loss = lax.psum(loss_part[0, 0], ("x", "y", "c"))

    dyf_f = dyf.reshape(t, d)
    g_ffn_down = _mm(act3.reshape(t, dff), dyf_f, "tn", name="dw_ffn_down")
    dact3 = _mm(dyf_f, w_ffn_down_f, "nt", name="d_ffn_act").reshape(bl, s, dff)
    dgf, duf, g_ffn_conv, g_b_ffn_conv = _ffn_act_bwd(up3, dact3, conv_full['w_ffn_conv'], w2['b_ffn_conv'])
    dgf_f, duf_f = dgf.reshape(t, dff), duf.reshape(t, dff)
    g_ffn_up = jnp.concatenate([_mm(h2f, dgf_f, "tn", name="dw_ffn_up_gate", out_chunk=up_chunk),
                                _mm(h2f, duf_f, "tn", name="dw_ffn_up_value", out_chunk=up_chunk)], axis=0)
    dh2 = _mm([dgf_f, duf_f], gathered['w_ffn_up'], "nt", name="d_h2", b_chunk=up_chunk).reshape(bl, s, d)
    dmo, dgt1, dx1, dsh2, dsc2, dg_norm2 = _norm_mod_bwd(dh2, x1, dx2, w2['g_norm2'], sc2, "norm2_bwd", gated=(mo3, gt1))

    dmo_f = dmo.reshape(t, d)
    g_out = _mm(merged.reshape(t, d), dmo_f, "tn", name="dw_out")
    dm3 = _mm(dmo_f, w_out_f, "nt", name="d_merged").reshape(bl, s, d)
    dpr, dpa, dmr, dma = _merge_bwd(dm3, mg3, pr3, pa3)
    g_proj_rnn = _mm(y_rnn.reshape(t, dr), dpr.reshape(t, d), "tn", name="dw_proj_rnn")
    g_proj_attn = _mm(o3.reshape(t, da), dpa.reshape(t, d), "tn", name="dw_proj_attn")
    dyr3 = _mm(dpr.reshape(t, d), w_proj_rnn_f, "nt", name="d_y_rnn").reshape(bl, s, dr)
    do3 = _mm(dpa.reshape(t, d), w_proj_attn_f, "nt", name="d_y_attn").reshape(bl, s, da)

    dq3, dk3, dv3, dfk, dfq = _attn_bwd(q3, kv3, do3, o3, lse_row, fcol, frow, nh, scale)
    heads_last = lambda a: jnp.pad(a.reshape(bl, nh, s).transpose(0, 2, 1), ((0, 0), (0, 0), (0, LANES - nh)))
    dzf3, g_bf = _fgate_bwd(heads_last(dfk[:, :, 0]), heads_last(dfq), zf3, bf_pad)

    dxr, dgr, g_rnn_conv, g_b_rnn_conv, g_lru_a, g_b_lru_a, g_lru_i, g_b_lru_i, g_lam = _rnn_bwd(
        zr, rnn_kept, dyr3, lru[0], lru[2], lru[4], lru[6])

    dz = [a.reshape(t, -1) for a in (dxr, dgr, dq3, dk3, dv3, dmr, dma)]
    dzf_f = dzf3.reshape(t, LANES)
    seg_names = ("xr", "gr", "q", "k", "v", "mr", "ma")
    g_seg = [_mm(h1f, a, "tn", name="dw_in_" + n) for n, a in zip(seg_names, dz)]
    g_in_fl = _mm(h1f, dzf_f, "tn", name="dw_in_fl")[:, :nh]
    g_in = _chunks_of_cols(g_seg[:5] + [g_in_fl] + g_seg[5:], N_CHIPS)
    dh1 = _mm(dzf_f, w_fl, "nt", name="d_h1_fl")
    dh1 = _mm(dz[:2], w_rnn, "nt", name="d_h1_rnn", add=dh1)
    dh1 = _mm(dz[2:], w_rest, "nt", name="d_h1", add=dh1).reshape(bl, s, d)
    grad_x, dsh1, dsc1, dg_norm1 = _norm_mod_bwd(dh1, x, dx1, w2['g_norm1'], sc1, "norm1_bwd")

    dmods = [jnp.concatenate([dsh1, dsc1, dgt1, dsh2, dsc2, dgt2], axis=-1).reshape(bl, -1),
             jnp.concatenate([dshf, dscf], axis=-1).reshape(bl, -1)]
    dmods = [g.reshape(nd * bl, -1) for g in _all_gather_devices(dmods, "ag_dmod")]
    grad = {n: _mm(c_act, my_cols(g, n), "tn", name="dw_" + n) for n, g in zip(_ADA, dmods)}
    grad.update({b: _rowsum(g, "d" + b) for b, g in zip(_ADA_BIAS, dmods)})

    rowchunks = lambda g: g.reshape(N_CHIPS, g.shape[0] // N_CHIPS, g.shape[1])
    full = dict(w_in=g_in, w_proj_rnn=rowchunks(g_proj_rnn), w_proj_attn=rowchunks(g_proj_attn),
                w_out=rowchunks(g_out), w_ffn_up=g_ffn_up, w_ffn_down=rowchunks(g_ffn_down))
    small = dict(g_norm1=dg_norm1, w_rnn_conv=g_rnn_conv, b_rnn_conv=g_b_rnn_conv, w_lru_a=g_lru_a,
                 b_lru_a=g_b_lru_a, w_lru_i=g_lru_i, b_lru_i=g_b_lru_i, lru_lambda=g_lam, b_fgate=g_bf[:, :nh],
                 g_norm2=dg_norm2, w_ffn_conv=g_ffn_conv, b_ffn_conv=g_b_ffn_conv, g_final=dg_final)

    small_names = _REPLICATED + _CONV
    n_small = sum(small[n].size for n in small_names)
    rows_q = -(-n_small // (N_CHIPS * FLAT_COLS * 32 * RS_STEPS)) * 32 * RS_STEPS
    small_flat = _flat_pad([small[n] for n in small_names], N_CHIPS * rows_q * FLAT_COLS).reshape(N_CHIPS, rows_q, FLAT_COLS)
    reduced = _reduce_scatter_chips([full[n] for n in names] + [small_flat])
    grad.update(zip(names, reduced[:-1]))
    small_all = _all_gather_chips([reduced[-1]], "ag_small_grads")[0].reshape(-1)
    grad.update(zip(small_names, _split_flat(small_all, [small[n].shape for n in small_names])))
    for n in _CONV:
        n4 = w2[n].shape[1]
        grad[n] = lax.dynamic_slice_in_dim(grad[n], chip * n4, n4, axis=1)

    delta_w, new_m, new_v = {}, {}, {}
    for n in names + _ADA:
        delta_w[n], new_m[n], new_v[n] = _adamw(grad[n], w2[n], m2[n], v2[n], "adamw_" + n)
    small_names = small_names + _ADA_BIAS
    rows_small = -(-sum(w2[n].size for n in small_names) // (FLAT_COLS * SUBLANES)) * SUBLANES
    flat_small = lambda src: _flat_pad([src[n] for n in small_names], rows_small * FLAT_COLS).reshape(rows_small, FLAT_COLS)
    small_out = _adamw(flat_small(grad), flat_small(w2), flat_small(m2), flat_small(v2), "adamw_small")
    for dst, flat in zip((delta_w, new_m, new_v), small_out):
        dst.update(zip(small_names, _split_flat(flat.reshape(-1), [w2[n].shape for n in small_names])))

    out = [loss, grad_x]
    for src in (grad, delta_w, new_m, new_v):
        out += [src[n].reshape(shape_of[n]) for n in _WEIGHTS]
    return tuple(out)
```

```python
import functools
import math

import jax
import jax.numpy as jnp
from jax import lax
from jax.experimental import pallas as pl
from jax.experimental.pallas import tpu as pltpu

F32 = jnp.float32
BF16 = jnp.bfloat16
MESH = pl.DeviceIdType.MESH

RMS_EPS = 1e-6
LRU_C = 8.0
ADAM_LR = 0.001
ADAM_B1 = 0.9
ADAM_B2 = 0.999
ADAM_EPS = 1e-08
ADAM_WD = 0.01
ADAM_STEP = 10

LANES = 128
SUBLANES = 8
N_CHIPS = 4
FLAT_COLS = 1024
SCAN_SEGMENTS = 2 * SUBLANES
VMEM_LIMIT = 48 * 1024 * 1024
NEG_BIG = -1e30


def _cp(n_axes):
    return pltpu.CompilerParams(dimension_semantics=("arbitrary",) * n_axes, vmem_limit_bytes=VMEM_LIMIT)


def _tile(n, target, align):
    if n <= target:
        return n
    t = (target // align) * align
    while t >= align:
        if n % t == 0:
            return t
        t -= align
    return n


def _nice_rows(n, align):
    r = -(-n // align) * align
    while True:
        if r <= 640:
            return r, r
        t = _tile(r, 640, align)
        if 128 <= t <= 640:
            return r, t
        r += align


def _sigmoid(x):
    return 0.5 * jnp.tanh(0.5 * x) + 0.5


def _softplus(x):
    return jnp.maximum(x, 0.0) + jnp.log1p(jnp.exp(-jnp.abs(x)))


def _expm1(x, exp_x):
    small = x * (1.0 + 0.5 * x * (1.0 + (1.0 / 3.0) * x * (1.0 + 0.25 * x)))
    return jnp.where(jnp.abs(x) < 0.05, small, exp_x - 1.0)


_GELU_K = math.sqrt(2.0 / math.pi)
_GELU_C = 0.044715


def _gelu(x):
    t = jnp.tanh(_GELU_K * (x + _GELU_C * x * x * x))
    return 0.5 * x * (1.0 + t)


def _gelu_and_grad(x):
    t = jnp.tanh(_GELU_K * (x + _GELU_C * x * x * x))
    g = 0.5 * x * (1.0 + t)
    dg = 0.5 * (1.0 + t) + 0.5 * x * (1.0 - t * t) * _GELU_K * (1.0 + 3.0 * _GELU_C * x * x)
    return g, dg


def _shift_down(x, k):
    if k == 0:
        return x
    y = pltpu.roll(x, k, 0)
    rows = lax.broadcasted_iota(jnp.int32, (SUBLANES, x.shape[1]), 0)
    return jnp.concatenate([jnp.where(rows >= k, y[:SUBLANES], 0.0), y[SUBLANES:]], axis=0)


def _shift_up(x, k):
    if k == 0:
        return x
    s = x.shape[0]
    y = pltpu.roll(x, s - k, 0)
    rows = lax.broadcasted_iota(jnp.int32, (SUBLANES, x.shape[1]), 0)
    return jnp.concatenate([y[:s - SUBLANES], jnp.where(rows < SUBLANES - k, y[s - SUBLANES:], 0.0)], axis=0)


def _dot(a, b, dims):
    return lax.dot_general(a.astype(BF16), b.astype(BF16), (dims, ((), ())), preferred_element_type=F32)


_NN = ((1,), (0,))
_NT = ((1,), (1,))
_TN = ((0,), (0,))


def _mm(a, b, mode, *, name, out_dtype=F32, scale=None, bias=None, add=None, tm=1024, tn=1024, tk=1024,
        b_chunk=None, out_chunk=None):
    pieces = list(a) if isinstance(a, (list, tuple)) else [a]
    ksize = lambda p: p.shape[0] if mode == "tn" else p.shape[1]
    if b_chunk is None:
        brows, bcols = b.shape
    else:
        brows, bcols = b.shape[1], b.shape[0] * b_chunk
    k = sum(ksize(p) for p in pieces)
    if mode == "nt":
        m, n = pieces[0].shape[0], brows
        assert bcols == k, (bcols, k)
    else:
        m, n = (pieces[0].shape[1] if mode == "tn" else pieces[0].shape[0]), bcols
        assert brows == k, (brows, k)
    tm = _tile(m, tm, LANES)
    ncut = n
    if b_chunk is not None and mode != "nt":
        ncut = b_chunk
    if out_chunk is not None:
        ncut = math.gcd(ncut, out_chunk)
    tn = _tile(ncut, tn, LANES)
    kcut = b_chunk if (b_chunk is not None and mode == "nt") else k
    for p in pieces:
        kcut = math.gcd(kcut, ksize(p))
    tk = _tile(kcut, tk, LANES)
    nk = k // tk
    dims = {"nn": _NN, "nt": _NT, "tn": _TN}[mode]
    counts = [ksize(p) // tk for p in pieces]
    starts = [sum(counts[:i]) for i in range(len(pieces))]
    n_pieces = len(pieces)

    def a_spec(s0, cnt):
        kmap = (lambda kk: kk) if n_pieces == 1 else (lambda kk: jnp.clip(kk - s0, 0, cnt - 1))
        if mode == "tn":
            return pl.BlockSpec((tk, tm), lambda i, j, kk: (kmap(kk), i))
        return pl.BlockSpec((tm, tk), lambda i, j, kk: (i, kmap(kk)))

    if b_chunk is None:
        if mode == "nt":
            b_spec = pl.BlockSpec((tn, tk), lambda i, j, kk: (j, kk))
        else:
            b_spec = pl.BlockSpec((tk, tn), lambda i, j, kk: (kk, j))
    elif mode == "nt":
        per_b = b_chunk // tk
        b_spec = pl.BlockSpec((None, tn, tk), lambda i, j, kk: (kk // per_b, j, kk % per_b))
    else:
        per_b = b_chunk // tn
        b_spec = pl.BlockSpec((None, tk, tn), lambda i, j, kk: (j // per_b, kk, j % per_b))
    if out_chunk is None:
        out_spec = pl.BlockSpec((tm, tn), lambda i, j, kk: (i, j))
        out_shape = jax.ShapeDtypeStruct((m, n), out_dtype)
    else:
        per_o = out_chunk // tn
        out_spec = pl.BlockSpec((None, tm, tn), lambda i, j, kk: (j // per_o, i, j % per_o))
        out_shape = jax.ShapeDtypeStruct((n // out_chunk, m, out_chunk), out_dtype)
    in_specs = [a_spec(s0, cnt) for s0, cnt in zip(starts, counts)] + [b_spec]
    args = pieces + [b]
    if bias is not None:
        in_specs.append(pl.BlockSpec((1, tn), lambda i, j, kk: (0, j)))
        args.append(bias)
    if add is not None:
        in_specs.append(pl.BlockSpec((tm, tn), lambda i, j, kk: (i, j)))
        args.append(add)

    def kern(*refs):
        b_ref = refs[n_pieces]
        o_ref = refs[n_pieces + 1 + (bias is not None) + (add is not None)]

        def finish(r):
            if scale is not None:
                r = r * scale
            pos = n_pieces + 1
            if bias is not None:
                r = r + refs[pos][...]
                pos += 1
            if add is not None:
                r = r + refs[pos][...]
            o_ref[...] = r.astype(out_dtype)

        if nk == 1:
            finish(_dot(refs[0][...], b_ref[...], dims))
            return
        acc = refs[-1]
        kk = pl.program_id(2)

        @pl.when(kk == 0)
        def _():
            acc[...] = jnp.zeros_like(acc)

        if n_pieces == 1:
            acc[...] += _dot(refs[0][...], b_ref[...], dims)
        else:
            for idx in range(n_pieces):
                @pl.when((kk >= starts[idx]) & (kk < starts[idx] + counts[idx]))
                def _(idx=idx):
                    acc[...] += _dot(refs[idx][...], b_ref[...], dims)

        @pl.when(kk == nk - 1)
        def _():
            finish(acc[...])

    return pl.pallas_call(
        kern, name=name,
        grid=(m // tm, n // tn, nk),
        in_specs=in_specs, out_specs=out_spec, out_shape=out_shape,
        scratch_shapes=[pltpu.VMEM((tm, tn), F32)] if nk > 1 else [],
        compiler_params=_cp(3),
    )(*args)


def _silu_pad(c, rows):
    bl, d = c.shape

    def kern(c_ref, o_ref):
        o_ref[...] = jnp.zeros_like(o_ref)
        v = c_ref[...]
        o_ref[0:bl, :] = v * _sigmoid(v)

    return pl.pallas_call(kern, name="silu_pad", out_shape=jax.ShapeDtypeStruct((rows, d), F32))(c)


def _rowsum(x, name):
    r, n = x.shape

    def kern(x_ref, o_ref):
        o_ref[...] = jnp.sum(x_ref[...], axis=0, keepdims=True)

    return pl.pallas_call(kern, name=name, out_shape=jax.ShapeDtypeStruct((1, n), F32))(x)


def _norm_parts(x, g):
    r = lax.rsqrt(jnp.mean(x * x, axis=-1, keepdims=True) + RMS_EPS)
    xh = x * r
    return r, xh, xh * g


def _norm_bwd_parts(dh, xh, r, g, sc):
    n = xh * g
    dn = dh * (1.0 + sc)
    dxh = dn * g
    dx = r * (dxh - xh * jnp.mean(dxh * xh, axis=-1, keepdims=True))
    return dx, dh, dh * n, dn * xh


def _act_specs(ts, d, n):
    return [pl.BlockSpec((1, ts, d), lambda b, t: (b, t, 0)) for _ in range(n)]


def _vec_spec(d):
    return pl.BlockSpec((1, 1, d), lambda b, t: (b, 0, 0))


def _par_spec(d):
    return pl.BlockSpec((1, d), lambda b, t: (0, 0))


def _norm_mod_fwd(x3, g, sh, sc):
    bl, s, d = x3.shape
    ts = _tile(s, 512, SUBLANES)

    def kern(x_ref, g_ref, sh_ref, sc_ref, h_ref):
        _, _, n = _norm_parts(x_ref[0], g_ref[...])
        h_ref[0] = (n * (1.0 + sc_ref[0]) + sh_ref[0]).astype(BF16)

    return pl.pallas_call(
        kern, name="norm_mod_fwd", grid=(bl, s // ts),
        in_specs=_act_specs(ts, d, 1) + [_par_spec(d), _vec_spec(d), _vec_spec(d)],
        out_specs=_act_specs(ts, d, 1)[0],
        out_shape=jax.ShapeDtypeStruct((bl, s, d), BF16),
        compiler_params=_cp(2),
    )(x3, g, sh, sc)


def _resid_norm_fwd(x3, y3, gate, g, sh, sc):
    bl, s, d = x3.shape
    ts = _tile(s, 512, SUBLANES)

    def kern(x_ref, y_ref, gate_ref, g_ref, sh_ref, sc_ref, x1_ref, h_ref):
        x1 = x_ref[0] + gate_ref[0] * y_ref[0]
        x1_ref[0] = x1
        _, _, n = _norm_parts(x1, g_ref[...])
        h_ref[0] = (n * (1.0 + sc_ref[0]) + sh_ref[0]).astype(BF16)

    return pl.pallas_call(
        kern, name="resid_norm_fwd", grid=(bl, s // ts),
        in_specs=_act_specs(ts, d, 2) + [_vec_spec(d), _par_spec(d), _vec_spec(d), _vec_spec(d)],
        out_specs=_act_specs(ts, d, 2),
        out_shape=[jax.ShapeDtypeStruct((bl, s, d), F32), jax.ShapeDtypeStruct((bl, s, d), BF16)],
        compiler_params=_cp(2),
    )(x3, y3, gate, g, sh, sc)


def _norm_mod_bwd(dh3, x3, dres3, g, sc, name, gated=None):
    bl, s, d = x3.shape
    ts = _tile(s, 512, SUBLANES)

    def kern(dh_ref, x_ref, dres_ref, g_ref, sc_ref, *rest):
        dx_ref, dsh_ref, dsc_ref, dg_ref = rest[-4:]
        b, t = pl.program_id(0), pl.program_id(1)
        gv = g_ref[...]
        r, xh, _ = _norm_parts(x_ref[0], gv)
        dx, a, bb, cc = _norm_bwd_parts(dh_ref[0], xh, r, gv, sc_ref[0])
        dx = dres_ref[0] + dx
        dx_ref[0] = dx

        @pl.when(t == 0)
        def _():
            dsh_ref[...] = jnp.zeros_like(dsh_ref)
            dsc_ref[...] = jnp.zeros_like(dsc_ref)

        @pl.when((t == 0) & (b == 0))
        def _():
            dg_ref[...] = jnp.zeros_like(dg_ref)

        dsh_ref[0] += jnp.sum(a, axis=0, keepdims=True)
        dsc_ref[0] += jnp.sum(bb, axis=0, keepdims=True)
        dg_ref[...] += jnp.sum(cc, axis=0, keepdims=True)
        if gated is not None:
            y_ref, gate_ref, dy_ref, dgate_ref = rest[:4]
            dy_ref[0] = (gate_ref[0] * dx).astype(BF16)

            @pl.when(t == 0)
            def _():
                dgate_ref[...] = jnp.zeros_like(dgate_ref)

            dgate_ref[0] += jnp.sum(dx * y_ref[0], axis=0, keepdims=True)

    extra_in = [] if gated is None else _act_specs(ts, d, 1) + [_vec_spec(d)]
    extra_out = [] if gated is None else [_act_specs(ts, d, 1)[0], _vec_spec(d)]
    extra_shape = [] if gated is None else [jax.ShapeDtypeStruct((bl, s, d), BF16), jax.ShapeDtypeStruct((bl, 1, d), F32)]
    return pl.pallas_call(
        kern, name=name, grid=(bl, s // ts),
        in_specs=_act_specs(ts, d, 3) + [_par_spec(d), _vec_spec(d)] + extra_in,
        out_specs=extra_out + [_act_specs(ts, d, 1)[0], _vec_spec(d), _vec_spec(d), _par_spec(d)],
        out_shape=extra_shape + [jax.ShapeDtypeStruct((bl, s, d), F32), jax.ShapeDtypeStruct((bl, 1, d), F32),
                                 jax.ShapeDtypeStruct((bl, 1, d), F32), jax.ShapeDtypeStruct((1, d), F32)],
        compiler_params=_cp(2),
    )(dh3, x3, dres3, g, sc, *(gated or ()))


def _final_fwd_bwd(x1, yf, gate2, g, shf, scf, tgt):
    bl, s, d = x1.shape
    ts = _tile(s, 512, SUBLANES)

    def kern(x1_ref, yf_ref, gate_ref, g_ref, sh_ref, sc_ref, tgt_ref,
             dx_ref, dyf_ref, dgate_ref, dsh_ref, dsc_ref, dg_ref, loss_ref):
        b, t = pl.program_id(0), pl.program_id(1)
        gv, sc, yf, gate = g_ref[...], sc_ref[0], yf_ref[0], gate_ref[0]
        x2 = x1_ref[0] + gate * yf
        r, xh, n = _norm_parts(x2, gv)
        err = n * (1.0 + sc) + sh_ref[0] - tgt_ref[0]
        dx, a, bb, cc = _norm_bwd_parts(err * (1.0 / d), xh, r, gv, sc)
        dx_ref[0] = dx
        dyf_ref[0] = (gate * dx).astype(BF16)

        @pl.when(t == 0)
        def _():
            dsh_ref[...] = jnp.zeros_like(dsh_ref)
            dsc_ref[...] = jnp.zeros_like(dsc_ref)
            dgate_ref[...] = jnp.zeros_like(dgate_ref)

        @pl.when((t == 0) & (b == 0))
        def _():
            dg_ref[...] = jnp.zeros_like(dg_ref)
            loss_ref[...] = jnp.zeros_like(loss_ref)

        dsh_ref[0] += jnp.sum(a, axis=0, keepdims=True)
        dsc_ref[0] += jnp.sum(bb, axis=0, keepdims=True)
        dg_ref[...] += jnp.sum(cc, axis=0, keepdims=True)
        dgate_ref[0] += jnp.sum(dx * yf, axis=0, keepdims=True)
        tok = jnp.mean(err * err, axis=-1, keepdims=True)
        loss_ref[...] += 0.5 * jnp.sum(tok, axis=0, keepdims=True)

    vec = jax.ShapeDtypeStruct((bl, 1, d), F32)
    return pl.pallas_call(
        kern, name="final_fwd_bwd", grid=(bl, s // ts),
        in_specs=_act_specs(ts, d, 2) + [_vec_spec(d), _par_spec(d), _vec_spec(d), _vec_spec(d)] + _act_specs(ts, d, 1),
        out_specs=_act_specs(ts, d, 2) + [_vec_spec(d), _vec_spec(d), _vec_spec(d), _par_spec(d),
                                          pl.BlockSpec((1, 1), lambda b, t: (0, 0))],
        out_shape=[jax.ShapeDtypeStruct((bl, s, d), F32), jax.ShapeDtypeStruct((bl, s, d), BF16), vec, vec, vec,
                   jax.ShapeDtypeStruct((1, d), F32), jax.ShapeDtypeStruct((1, 1), F32)],
        compiler_params=_cp(2),
    )(x1, yf, gate2, g, shf, scf, tgt)


def _rnn_gates(xr, cw, cb, wa, ba, wi, bi, lam):
    kw = cw.shape[0]
    xc = cb
    for k in range(kw):
        xc = xc + _shift_down(xr, kw - 1 - k) * cw[k:k + 1, :]
    r = _sigmoid(_dot(xc, wa, _NN) + ba)
    i = _sigmoid(_dot(xc, wi, _NN) + bi)
    sp = _softplus(-lam)
    log_a = -LRU_C * r * sp
    a = jnp.exp(log_a)
    mult = jnp.sqrt(-_expm1(2.0 * log_a, a * a))
    return xc, r, i, sp, a, mult


def _segment_scan(a_s, u_s, h_s, p_s, reverse):
    s, c = a_s.shape
    seg = s // SCAN_SEGMENTS

    unroll = math.gcd(seg, 8)

    def steps(n, carry):
        h, p = carry
        for j in range(unroll):
            t = n * unroll + j
            t = (seg - 1 - t) if reverse else t
            av = a_s[pl.ds(t, SCAN_SEGMENTS, stride=seg), :]
            uv = u_s[pl.ds(t, SCAN_SEGMENTS, stride=seg), :]
            h = av * h + uv
            p = p * av
            h_s[pl.ds(t, SCAN_SEGMENTS, stride=seg), :] = h
            p_s[pl.ds(t, SCAN_SEGMENTS, stride=seg), :] = p
        return h, p

    lax.fori_loop(0, seg // unroll, steps, (jnp.zeros((SCAN_SEGMENTS, c), F32), jnp.ones((SCAN_SEGMENTS, c), F32)))
    carry = jnp.zeros((1, c), F32)
    order = range(SCAN_SEGMENTS - 1, -1, -1) if reverse else range(SCAN_SEGMENTS)
    for j in order:
        rows = pl.ds(j * seg, seg)
        fixed = h_s[rows, :] + p_s[rows, :] * carry
        h_s[rows, :] = fixed
        carry = fixed[0:1, :] if reverse else fixed[seg - 1:seg, :]


def _rnn_specs(s, rb, nb):
    act = lambda off: pl.BlockSpec((1, s, rb), lambda b, n, off=off: (b, 0, off + n))
    par = pl.BlockSpec((1, rb), lambda b, n: (0, n))
    wsp = pl.BlockSpec((1, rb, rb), lambda b, n: (n, 0, 0))
    return act, par, wsp


def _rnn_fwd(zr3, cw, cb, wa, ba, wi, bi, lam):
    bl, s, two = zr3.shape
    nb, rb, _ = wa.shape
    dr = nb * rb
    kw = cw.shape[0]
    act, par, wsp = _rnn_specs(s, rb, nb)

    def kern(xr_ref, gr_ref, cw_ref, cb_ref, wa_ref, ba_ref, wi_ref, bi_ref, lam_ref,
             y_ref, h_ref, xc_ref, r_ref, i_ref, a_ref, mult_ref, a_s, u_s, h_s, p_s):
        xc, r, i, sp, a, mult = _rnn_gates(xr_ref[0], cw_ref[...], cb_ref[...], wa_ref[0], ba_ref[...],
                                           wi_ref[0], bi_ref[...], lam_ref[...])
        for ref, val in ((xc_ref, xc), (r_ref, r), (i_ref, i), (a_ref, a), (mult_ref, mult)):
            ref[0] = val
        a_s[...] = a
        u_s[...] = mult * (i * xc)
        _segment_scan(a_s, u_s, h_s, p_s, reverse=False)
        h = h_s[...]
        h_ref[0] = h
        y_ref[0] = (_gelu(gr_ref[0]) * h).astype(BF16)

    kept = jax.ShapeDtypeStruct((bl, s, dr), F32)
    return pl.pallas_call(
        kern, name="rnn_fwd", grid=(bl, nb),
        in_specs=[act(0), act(nb), pl.BlockSpec((kw, rb), lambda b, n: (0, n)), par, wsp, par, wsp, par, par],
        out_specs=[act(0)] * 7,
        out_shape=[jax.ShapeDtypeStruct((bl, s, dr), BF16)] + [kept] * 6,
        scratch_shapes=[pltpu.VMEM((s, rb), F32)] * 4,
        compiler_params=_cp(2),
    )(zr3, zr3, cw, cb, wa, ba, wi, bi, lam)


def _rnn_bwd(zr3, kept, dy3, cw, wa, wi, lam):
    bl, s, _ = zr3.shape
    nb, rb, _ = wa.shape
    dr = nb * rb
    kw = cw.shape[0]
    act = lambda off: pl.BlockSpec((1, s, rb), lambda n, b, off=off: (b, 0, off + n))
    par = pl.BlockSpec((1, rb), lambda n, b: (0, n))
    wsp = pl.BlockSpec((1, rb, rb), lambda n, b: (n, 0, 0))
    cws = pl.BlockSpec((kw, rb), lambda n, b: (0, n))

    def kern(xr_ref, gr_ref, h_ref, xc_ref, r_ref, i_ref, a_ref, mult_ref, dy_ref, cw_ref, wa_ref, wi_ref, lam_ref,
             dxr_ref, dgr_ref, dcw_ref, dcb_ref, dwa_ref, dba_ref, dwi_ref, dbi_ref, dlam_ref, a_s, u_s, h_s, p_s):
        b = pl.program_id(1)
        xr, cwv, lamv = xr_ref[0], cw_ref[...], lam_ref[...]
        wav, wiv = wa_ref[0], wi_ref[0]
        xc, r, i, a, mult = xc_ref[0], r_ref[0], i_ref[0], a_ref[0], mult_ref[0]
        sp = _softplus(-lamv)
        h, dy = h_ref[0], dy_ref[0]
        ge, dge = _gelu_and_grad(gr_ref[0])
        dgr_ref[0] = (dy * h * dge).astype(BF16)
        a_s[...] = _shift_up(a, 1)
        u_s[...] = dy * ge
        _segment_scan(a_s, u_s, h_s, p_s, reverse=True)
        g = h_s[...]
        da = g * _shift_down(h, 1)
        ix = i * xc
        dlog_a = da * a + (g * ix) * (-(a * a) / mult)
        di = g * mult * xc
        dpa = (dlog_a * (-LRU_C * sp)) * r * (1.0 - r)
        dpi = di * i * (1.0 - i)
        dxc = g * mult * i + _dot(dpa, wav, _NT) + _dot(dpi, wiv, _NT)
        dxr = jnp.zeros_like(dxc)
        dcw_rows = []
        for k in range(kw):
            dxr = dxr + _shift_up(dxc, kw - 1 - k) * cwv[k:k + 1, :]
            dcw_rows.append(jnp.sum(dxc * _shift_down(xr, kw - 1 - k), axis=0, keepdims=True))
        dxr_ref[0] = dxr.astype(BF16)

        @pl.when(b == 0)
        def _():
            for ref in (dcw_ref, dcb_ref, dwa_ref, dba_ref, dwi_ref, dbi_ref, dlam_ref):
                ref[...] = jnp.zeros_like(ref)

        for k in range(kw):
            dcw_ref[k:k + 1, :] += dcw_rows[k]
        dcb_ref[...] += jnp.sum(dxc, axis=0, keepdims=True)
        dwa_ref[0] += _dot(xc, dpa, _TN)
        dwi_ref[0] += _dot(xc, dpi, _TN)
        dba_ref[...] += jnp.sum(dpa, axis=0, keepdims=True)
        dbi_ref[...] += jnp.sum(dpi, axis=0, keepdims=True)
        dsp = jnp.sum(dlog_a * (-LRU_C * r), axis=0, keepdims=True)
        dlam_ref[...] += dsp * (-_sigmoid(-lamv))

    vec = jax.ShapeDtypeStruct((1, dr), F32)
    wsh = jax.ShapeDtypeStruct((nb, rb, rb), F32)
    return pl.pallas_call(
        kern, name="rnn_bwd", grid=(nb, bl),
        in_specs=[act(0), act(nb)] + [act(0)] * 7 + [cws, wsp, wsp, par],
        out_specs=[act(0), act(0), cws, par, wsp, par, wsp, par, par],
        out_shape=[jax.ShapeDtypeStruct((bl, s, dr), BF16), jax.ShapeDtypeStruct((bl, s, dr), BF16),
                   jax.ShapeDtypeStruct((kw, dr), F32), vec, wsh, vec, wsh, vec, vec],
        scratch_shapes=[pltpu.VMEM((s, rb), F32)] * 4,
        compiler_params=_cp(2),
    )(zr3, zr3, *kept, dy3, cw, wa, wi, lam)


def _tri(n, upper):
    r = lax.broadcasted_iota(jnp.int32, (n, n), 0)
    c = lax.broadcasted_iota(jnp.int32, (n, n), 1)
    return jnp.where((c >= r) if upper else (c <= r), 1.0, 0.0).astype(F32)


def _fgate_fwd(zf3, bf):
    bl, s, w = zf3.shape
    ch = _tile(s, 256, SUBLANES)

    def kern(z_ref, b_ref, f_ref):
        tri = _tri(ch, upper=False)
        carry = jnp.zeros((1, w), F32)
        for j in range(s // ch):
            rows = pl.ds(j * ch, ch)
            lf = -_softplus(-(z_ref[0, rows, :] + b_ref[...]))
            out = jnp.dot(tri, lf, precision=lax.Precision.HIGHEST, preferred_element_type=F32) + carry
            f_ref[0, rows, :] = out
            carry = out[ch - 1:ch, :]

    return pl.pallas_call(
        kern, name="fgate_fwd", grid=(bl,),
        in_specs=[pl.BlockSpec((1, s, w), lambda b: (b, 0, 0)), pl.BlockSpec((1, w), lambda b: (0, 0))],
        out_specs=pl.BlockSpec((1, s, w), lambda b: (b, 0, 0)),
        out_shape=jax.ShapeDtypeStruct((bl, s, w), F32),
        compiler_params=_cp(1),
    )(zf3, bf)


def _fgate_bwd(dfk3, dfq3, zf3, bf):
    bl, s, w = zf3.shape
    ch = _tile(s, 256, SUBLANES)

    def kern(dfk_ref, dfq_ref, z_ref, b_ref, dz_ref, db_ref):
        b = pl.program_id(0)
        tri = _tri(ch, upper=True)
        carry = jnp.zeros((1, w), F32)
        dbsum = jnp.zeros((1, w), F32)
        for j in range(s // ch - 1, -1, -1):
            rows = pl.ds(j * ch, ch)
            df = dfk_ref[0, rows, :] + dfq_ref[0, rows, :]
            dlf = jnp.dot(tri, df, precision=lax.Precision.HIGHEST, preferred_element_type=F32) + carry
            carry = dlf[0:1, :]
            dz = dlf * _sigmoid(-(z_ref[0, rows, :] + b_ref[...]))
            dz_ref[0, rows, :] = dz.astype(BF16)
            dbsum = dbsum + jnp.sum(dz, axis=0, keepdims=True)

        @pl.when(b == 0)
        def _():
            db_ref[...] = jnp.zeros_like(db_ref)

        db_ref[...] += dbsum

    return pl.pallas_call(
        kern, name="fgate_bwd", grid=(bl,),
        in_specs=[pl.BlockSpec((1, s, w), lambda b: (b, 0, 0))] * 3 + [pl.BlockSpec((1, w), lambda b: (0, 0))],
        out_specs=[pl.BlockSpec((1, s, w), lambda b: (b, 0, 0)), pl.BlockSpec((1, w), lambda b: (0, 0))],
        out_shape=[jax.ShapeDtypeStruct((bl, s, w), BF16), jax.ShapeDtypeStruct((1, w), F32)],
        compiler_params=_cp(1),
    )(dfk3, dfq3, zf3, bf)


def _lanes(col, width):
    return col if width == LANES else jnp.concatenate([col] * (width // LANES), axis=1)


def _causal(sc, row0, col0, transposed):
    r = lax.broadcasted_iota(jnp.int32, sc.shape, 0) + row0
    c = lax.broadcasted_iota(jnp.int32, sc.shape, 1) + col0
    return jnp.where((c >= r) if transposed else (r >= c), sc, NEG_BIG)


def _attn_fwd(q3, kv3, fcol, frow, nh):
    bl, s, da = q3.shape
    dh = da // nh
    tq = _tile(s, 512, LANES)
    nq = s // tq

    def kern(iq_tab, ik_tab, q_ref, k_ref, v_ref, fk_ref, fq_ref, o_ref, lse_ref, m_s, l_s, acc):
        iq, ik = iq_tab[pl.program_id(2)], ik_tab[pl.program_id(2)]

        @pl.when(ik == 0)
        def _():
            m_s[...] = jnp.full_like(m_s, NEG_BIG)
            l_s[...] = jnp.zeros_like(l_s)
            acc[...] = jnp.zeros_like(acc)

        def block(masked):
            st = _dot(k_ref[0], q_ref[0], _NT) - _lanes(fk_ref[0], tq) + fq_ref[0]
            if masked:
                st = _causal(st, ik * tq, iq * tq, True)
            m_old = m_s[...]
            m_new = jnp.maximum(m_old, jnp.max(st, axis=0, keepdims=True))
            alpha = jnp.exp(m_old - m_new)
            pt = jnp.exp(st - m_new)
            l_s[...] = alpha * l_s[...] + jnp.sum(pt, axis=0, keepdims=True)
            acc[...] = alpha * acc[...] + _dot(v_ref[0], pt, _TN)
            m_s[...] = m_new

        pl.when(ik < iq)(functools.partial(block, False))

        @pl.when(ik == iq)
        def _():
            block(True)
            l = l_s[...]
            o_ref[0] = (acc[...] / l).T
            lse_ref[0] = m_s[...] + jnp.log(l)

    pairs = [(i, j) for i in range(nq) for j in range(i + 1)]
    iq_tab, ik_tab = (jnp.asarray(col, jnp.int32) for col in zip(*pairs))
    qmap = lambda b, h, p, iqt, ikt: (b, iqt[p], h)
    kmap = lambda off: (lambda b, h, p, iqt, ikt: (b, ikt[p], off + h))
    return pl.pallas_call(
        kern, name="attn_fwd",
        grid_spec=pltpu.PrefetchScalarGridSpec(
            num_scalar_prefetch=2, grid=(bl, nh, len(pairs)),
            in_specs=[pl.BlockSpec((1, tq, dh), qmap), pl.BlockSpec((1, tq, dh), kmap(0)), pl.BlockSpec((1, tq, dh), kmap(nh)),
                      pl.BlockSpec((1, tq, LANES), lambda b, h, p, iqt, ikt: (b * nh + h, ikt[p], 0)),
                      pl.BlockSpec((1, 1, tq), lambda b, h, p, iqt, ikt: (b * nh + h, 0, iqt[p]))],
            out_specs=[pl.BlockSpec((1, tq, dh), qmap),
                       pl.BlockSpec((1, 1, tq), lambda b, h, p, iqt, ikt: (b * nh + h, 0, iqt[p]))],
            scratch_shapes=[pltpu.VMEM((1, tq), F32), pltpu.VMEM((1, tq), F32), pltpu.VMEM((dh, tq), F32)]),
        out_shape=[jax.ShapeDtypeStruct((bl, s, da), F32), jax.ShapeDtypeStruct((bl * nh, 1, s), F32)],
        compiler_params=_cp(3),
    )(iq_tab, ik_tab, q3, kv3, kv3, fcol, frow)


def _attn_bwd(q3, kv3, do3, o3, lse_row, fcol, frow, nh, scale):
    bl, s, da = q3.shape
    dh = da // nh
    tk = _tile(s, 512, LANES)
    nk = s // tk

    pairs = [(j, i) for j in range(nk) for i in range(j, nk)]

    def kern(ik_tab, iq_tab, q_ref, k_ref, v_ref, do_ref, o_ref, lse_ref, fk_ref, fq_ref, dq_ref, dk_ref, dv_ref, dfk_ref,
             dfq_ref, dq_acc, dk_acc, dv_acc, dfq_acc, delta_s):
        step = pl.program_id(2)
        ik, iq = ik_tab[step], iq_tab[step]
        qrow = pl.ds(iq, 1)

        @pl.when(step == 0)
        def _():
            dq_acc[...] = jnp.zeros_like(dq_acc)
            dfq_acc[...] = jnp.zeros_like(dfq_acc)

        @pl.when(ik == 0)
        def _():
            prod = do_ref[0] * o_ref[0]
            rows = lax.dot_general(jnp.ones((SUBLANES, dh), F32), prod, (_NT, ((), ())),
                                   precision=lax.Precision.HIGHEST, preferred_element_type=F32)
            delta_s[qrow, :] = rows[0:1, :]

        @pl.when(iq == ik)
        def _():
            dk_acc[...] = jnp.zeros_like(dk_acc)
            dv_acc[...] = jnp.zeros_like(dv_acc)

        def block(masked):
            q = q_ref[0]
            st = _dot(k_ref[0], q, _NT) - _lanes(fk_ref[0], tk) + fq_ref[0]
            if masked:
                st = _causal(st, ik * tk, iq * tk, True)
            pt = jnp.exp(st - lse_ref[0])
            dv_acc[...] += _dot(pt, do_ref[0], _NN)
            dpt = _dot(v_ref[0], do_ref[0], _NT)
            dst = (pt * (dpt - delta_s[qrow, :])).astype(BF16)
            q_ones = jnp.concatenate([q, jnp.ones_like(q)], axis=1)
            dk_acc[...] += _dot(dst, q_ones, _NN)
            qrows = pl.ds(pl.multiple_of(iq * tk, tk), tk)
            dq_acc[qrows, :] += _dot(dst, k_ref[0], _TN)
            dfq_acc[qrow, :] += jnp.sum(dst.astype(F32), axis=0, keepdims=True)

        pl.when(iq > ik)(functools.partial(block, False))
        pl.when(iq == ik)(functools.partial(block, True))

        @pl.when(iq == nk - 1)
        def _():
            ext = dk_acc[...]
            dk_ref[0] = ext[:, :dh].astype(BF16)
            dfk_ref[0] = -ext[:, dh:]
            dv_ref[0] = dv_acc[...].astype(BF16)

        @pl.when(step == len(pairs) - 1)
        def _():
            dq_ref[0] = (dq_acc[...] * scale).astype(BF16)
            dfq_ref[0] = dfq_acc[...]

    ik_tab, iq_tab = (jnp.asarray(col, jnp.int32) for col in zip(*pairs))
    qmap = lambda b, h, p, ikt, iqt: (b, iqt[p], h)
    omap = lambda b, h, p, ikt, iqt: (b, jnp.where(ikt[p] == 0, iqt[p], 0), h)
    rmap = lambda b, h, p, ikt, iqt: (b * nh + h, 0, iqt[p])
    kmap = lambda off: (lambda b, h, p, ikt, iqt: (b, ikt[p], off + h))
    bmap = lambda b, h, p, ikt, iqt: (b * nh + h, ikt[p], 0)
    return pl.pallas_call(
        kern, name="attn_bwd",
        grid_spec=pltpu.PrefetchScalarGridSpec(
            num_scalar_prefetch=2, grid=(bl, nh, len(pairs)),
            in_specs=[pl.BlockSpec((1, tk, dh), qmap), pl.BlockSpec((1, tk, dh), kmap(0)), pl.BlockSpec((1, tk, dh), kmap(nh)),
                      pl.BlockSpec((1, tk, dh), qmap), pl.BlockSpec((1, tk, dh), omap), pl.BlockSpec((1, 1, tk), rmap),
                      pl.BlockSpec((1, tk, LANES), bmap), pl.BlockSpec((1, 1, tk), rmap)],
            out_specs=[pl.BlockSpec((1, s, dh), lambda b, h, p, ikt, iqt: (b, 0, h)),
                       pl.BlockSpec((1, tk, dh), kmap(0)), pl.BlockSpec((1, tk, dh), kmap(0)),
                       pl.BlockSpec((1, tk, LANES), bmap),
                       pl.BlockSpec((1, nk, tk), lambda b, h, p, ikt, iqt: (b * nh + h, 0, 0))],
            scratch_shapes=[pltpu.VMEM((s, dh), F32), pltpu.VMEM((tk, 2 * dh), F32), pltpu.VMEM((tk, dh), F32),
                            pltpu.VMEM((nk, tk), F32), pltpu.VMEM((nk, tk), F32)]),
        out_shape=[jax.ShapeDtypeStruct((bl, s, da), BF16), jax.ShapeDtypeStruct((bl, s, da), BF16),
                   jax.ShapeDtypeStruct((bl, s, da), BF16), jax.ShapeDtypeStruct((bl * nh, s, LANES), F32),
                   jax.ShapeDtypeStruct((bl * nh, nk, tk), F32)],
        compiler_params=_cp(3),
    )(ik_tab, iq_tab, q3, kv3, kv3, do3, o3, lse_row, fcol, frow)


def _merge_fwd(mg3, pr3, pa3):
    bl, s, d = pr3.shape
    ts = _tile(s, 512, SUBLANES)
    half = lambda j: pl.BlockSpec((1, ts, d), lambda b, t, j=j: (b, t, j))

    def kern(mr_ref, ma_ref, pr_ref, pa_ref, o_ref):
        o_ref[0] = (_sigmoid(mr_ref[0]) * pr_ref[0] + _sigmoid(ma_ref[0]) * pa_ref[0]).astype(BF16)

    return pl.pallas_call(
        kern, name="merge_fwd", grid=(bl, s // ts),
        in_specs=[half(0), half(1)] + _act_specs(ts, d, 2), out_specs=_act_specs(ts, d, 1)[0],
        out_shape=jax.ShapeDtypeStruct((bl, s, d), BF16), compiler_params=_cp(2),
    )(mg3, mg3, pr3, pa3)


def _merge_bwd(dm3, mg3, pr3, pa3):
    bl, s, d = pr3.shape
    ts = _tile(s, 512, SUBLANES)
    half = lambda j: pl.BlockSpec((1, ts, d), lambda b, t, j=j: (b, t, j))

    def kern(dm_ref, mr_ref, ma_ref, pr_ref, pa_ref, dpr_ref, dpa_ref, dmr_ref, dma_ref):
        dm = dm_ref[0]
        gr, ga = _sigmoid(mr_ref[0]), _sigmoid(ma_ref[0])
        dpr_ref[0] = (gr * dm).astype(BF16)
        dpa_ref[0] = (ga * dm).astype(BF16)
        dmr_ref[0] = (dm * pr_ref[0] * gr * (1.0 - gr)).astype(BF16)
        dma_ref[0] = (dm * pa_ref[0] * ga * (1.0 - ga)).astype(BF16)

    return pl.pallas_call(
        kern, name="merge_bwd", grid=(bl, s // ts),
        in_specs=_act_specs(ts, d, 1) + [half(0), half(1)] + _act_specs(ts, d, 2), out_specs=_act_specs(ts, d, 4),
        out_shape=[jax.ShapeDtypeStruct((bl, s, d), BF16)] * 4, compiler_params=_cp(2),
    )(dm3, mg3, mg3, pr3, pa3)


def _ffn_conv(gf, cw, cb):
    kw = cw.shape[0]
    y = cb
    for k in range(kw):
        y = y + _shift_down(gf, kw - 1 - k) * cw[k:k + 1, :]
    return y


def _ffn_act_fwd(up3, cw, cb):
    bl, s, two = up3.shape
    dff = two // 2
    kw = cw.shape[0]
    tc = _tile(dff, 512, LANES)
    nc = dff // tc

    def kern(gf_ref, uf_ref, cw_ref, cb_ref, o_ref):
        o_ref[0] = (_gelu(_ffn_conv(gf_ref[0], cw_ref[...], cb_ref[...])) * uf_ref[0]).astype(BF16)

    act = lambda off: pl.BlockSpec((1, s, tc), lambda b, j, off=off: (b, 0, off + j))
    return pl.pallas_call(
        kern, name="ffn_act_fwd", grid=(bl, nc),
        in_specs=[act(0), act(nc), pl.BlockSpec((kw, tc), lambda b, j: (0, j)), pl.BlockSpec((1, tc), lambda b, j: (0, j))],
        out_specs=act(0), out_shape=jax.ShapeDtypeStruct((bl, s, dff), BF16), compiler_params=_cp(2),
    )(up3, up3, cw, cb)


def _ffn_act_bwd(up3, dact3, cw, cb):
    bl, s, two = up3.shape
    dff = two // 2
    kw = cw.shape[0]
    tc = _tile(dff, 256, LANES)
    nc = dff // tc

    def kern(gf_ref, uf_ref, da_ref, cw_ref, cb_ref, dgf_ref, duf_ref, dcw_ref, dcb_ref):
        b = pl.program_id(1)
        gf, cwv, da = gf_ref[0], cw_ref[...], da_ref[0]
        ge, dge = _gelu_and_grad(_ffn_conv(gf, cwv, cb_ref[...]))
        duf_ref[0] = (da * ge).astype(BF16)
        dgc = da * uf_ref[0] * dge
        dgf = jnp.zeros_like(dgc)
        rows = []
        for k in range(kw):
            dgf = dgf + _shift_up(dgc, kw - 1 - k) * cwv[k:k + 1, :]
            rows.append(jnp.sum(dgc * _shift_down(gf, kw - 1 - k), axis=0, keepdims=True))
        dgf_ref[0] = dgf.astype(BF16)

        @pl.when(b == 0)
        def _():
            dcw_ref[...] = jnp.zeros_like(dcw_ref)
            dcb_ref[...] = jnp.zeros_like(dcb_ref)

        for k in range(kw):
            dcw_ref[k:k + 1, :] += rows[k]
        dcb_ref[...] += jnp.sum(dgc, axis=0, keepdims=True)

    act = lambda off: pl.BlockSpec((1, s, tc), lambda j, b, off=off: (b, 0, off + j))
    cws = pl.BlockSpec((kw, tc), lambda j, b: (0, j))
    cbs = pl.BlockSpec((1, tc), lambda j, b: (0, j))
    return pl.pallas_call(
        kern, name="ffn_act_bwd", grid=(nc, bl),
        in_specs=[act(0), act(nc), act(0), cws, cbs], out_specs=[act(0), act(0), cws, cbs],
        out_shape=[jax.ShapeDtypeStruct((bl, s, dff), BF16), jax.ShapeDtypeStruct((bl, s, dff), BF16),
                   jax.ShapeDtypeStruct((kw, dff), F32), jax.ShapeDtypeStruct((1, dff), F32)],
        compiler_params=_cp(2),
    )(up3, up3, dact3, cw, cb)


_HBM = pl.BlockSpec(memory_space=pltpu.HBM)


def _place():
    x, y, c = lax.axis_index("x"), lax.axis_index("y"), lax.axis_index("c")
    chips = dict(me=2 * x + y, nx=2 * (1 - x) + y, ny=2 * x + (1 - y), diag=2 * (1 - x) + (1 - y))
    peers = dict(nx=(1 - x, y, c), ny=(x, 1 - y, c), sib=(x, y, 1 - c))
    return c, chips, peers


def _remote(src, dst, sems, k, to):
    return pltpu.make_async_remote_copy(src_ref=src, dst_ref=dst, send_sem=sems[0].at[k], recv_sem=sems[1].at[k],
                                        device_id=to, device_id_type=MESH)


RS_STEPS = 2


def _piece(q, idx, n=1):
    start = idx * q
    if not isinstance(start, int):
        start = pl.multiple_of(start, SUBLANES)
    return pl.ds(start, n * q)


def _all_gather_chips(xs, name):
    nt = len(xs)
    per = 9

    def body(*refs):
        x_refs, o_refs = refs[:nt], refs[nt:2 * nt]
        send_sems, recv_sems = refs[2 * nt:]
        c, chip, peer = _place()
        sems = (send_sems, recv_sems)
        me, nx, ny, dg = chip["me"], chip["nx"], chip["ny"], chip["diag"]
        sends = []

        def arrive(k, dst):
            _remote(dst, dst, sems, k, peer["sib"]).wait_recv()

        def pass_on(k, blk, to):
            cp = _remote(blk, blk, sems, k, peer[to])
            cp.start()
            sends.append(cp)

        for t in range(nt):
            q = xs[t].shape[0] // 4
            half = _piece(q, 2 * c, 2)
            for k, to in ((0, "nx"), (1, "ny")):
                cp = _remote(x_refs[t].at[half], o_refs[t].at[me, half], sems, per * t + k, peer[to])
                cp.start()
                sends.append(cp)
            cp = _remote(x_refs[t], o_refs[t].at[me], sems, per * t + 8, peer["sib"])
            cp.start()
            sends.append(cp)
        for t in range(nt):
            q, o, k0 = xs[t].shape[0] // 4, o_refs[t], per * t
            half, sub0, sub1 = _piece(q, 2 * c, 2), _piece(q, 2 * c), _piece(q, 2 * c + 1)
            arrive(k0 + 0, o.at[nx, half])
            pass_on(k0 + 2, o.at[nx, sub0], "ny")
            pass_on(k0 + 4, o.at[nx, half], "sib")
            arrive(k0 + 1, o.at[ny, half])
            pass_on(k0 + 3, o.at[ny, sub1], "nx")
            pass_on(k0 + 5, o.at[ny, half], "sib")
            arrive(k0 + 2, o.at[dg, sub0])
            pass_on(k0 + 6, o.at[dg, sub0], "sib")
            arrive(k0 + 3, o.at[dg, sub1])
            pass_on(k0 + 7, o.at[dg, sub1], "sib")
        for t in range(nt):
            q, o, k0 = xs[t].shape[0] // 4, o_refs[t], per * t
            arrive(k0 + 4, o.at[nx, _piece(q, 2 * (1 - c), 2)])
            arrive(k0 + 5, o.at[ny, _piece(q, 2 * (1 - c), 2)])
            arrive(k0 + 6, o.at[dg, _piece(q, 2 * (1 - c))])
            arrive(k0 + 7, o.at[dg, _piece(q, 2 * (1 - c) + 1)])
            arrive(k0 + 8, o.at[me])
        for cp in sends:
            cp.wait_send()

    return pl.pallas_call(
        body, name=name, in_specs=[_HBM] * nt, out_specs=[_HBM] * nt,
        out_shape=[jax.ShapeDtypeStruct((N_CHIPS,) + x.shape, x.dtype) for x in xs],
        scratch_shapes=[pltpu.SemaphoreType.DMA((per * nt,)), pltpu.SemaphoreType.DMA((per * nt,))],
    )(*xs)


def _all_gather_devices(xs, name):
    nt = len(xs)
    per = 7

    def body(*refs):
        x_refs, o_refs = refs[:nt], refs[nt:2 * nt]
        send_sems, recv_sems, local_sems = refs[2 * nt:]
        x, y, c = lax.axis_index("x"), lax.axis_index("y"), lax.axis_index("c")
        sems = (send_sems, recv_sems)
        sib = (x, y, 1 - c)
        chips = [(1 - x, y), (x, 1 - y), (1 - x, 1 - y)]
        slot = lambda px, py, pc: 4 * px + 2 * py + pc
        me = slot(x, y, c)
        sends, copies = [], []

        def arrive(k, dst):
            _remote(dst, dst, sems, k, sib).wait_recv()

        for t in range(nt):
            cp = pltpu.make_async_copy(x_refs[t], o_refs[t].at[me], local_sems.at[t])
            cp.start()
            copies.append(cp)
            for k, to in enumerate([sib] + [(*chip, c) for chip in chips]):
                cp = _remote(x_refs[t], o_refs[t].at[me], sems, per * t + k, to)
                cp.start()
                sends.append(cp)
        for t in range(nt):
            for j, chip in enumerate(chips):
                blk = o_refs[t].at[slot(*chip, c)]
                arrive(per * t + 1 + j, blk)
                cp = _remote(blk, blk, sems, per * t + 4 + j, sib)
                cp.start()
                sends.append(cp)
        for t in range(nt):
            arrive(per * t, o_refs[t].at[slot(x, y, 1 - c)])
            for j, chip in enumerate(chips):
                arrive(per * t + 4 + j, o_refs[t].at[slot(*chip, 1 - c)])
        for cp in sends:
            cp.wait_send()
        for cp in copies:
            cp.wait()

    return pl.pallas_call(
        body, name=name, in_specs=[_HBM] * nt, out_specs=[_HBM] * nt,
        out_shape=[jax.ShapeDtypeStruct((2 * N_CHIPS,) + a.shape, a.dtype) for a in xs],
        scratch_shapes=[pltpu.SemaphoreType.DMA((per * nt,)), pltpu.SemaphoreType.DMA((per * nt,)),
                        pltpu.SemaphoreType.DMA((nt,))],
    )(*xs)


def _exchange(name, xs, out_shapes, plan):
    nt = len(xs)

    def body(*refs):
        x_refs, o_refs = refs[:nt], refs[nt:2 * nt]
        send_sems, recv_sems = refs[2 * nt:]
        c, chip, peer = _place()
        cps = []
        for t in range(nt):
            for src, dst, to in plan(c, chip, x_refs[t], o_refs[t], xs[t].shape):
                cps.append(_remote(src, dst, (send_sems, recv_sems), len(cps), peer[to]))
        for cp in cps:
            cp.start()
        for cp in cps:
            cp.wait()

    n_copies = nt * len(plan(0, dict(me=0, nx=2, ny=1, diag=3), None, None, xs[0].shape, count_only=True))
    return pl.pallas_call(
        body, name=name, in_specs=[_HBM] * nt, out_specs=[_HBM] * nt,
        out_shape=[jax.ShapeDtypeStruct(s, x.dtype) for s, x in zip(out_shapes, xs)],
        scratch_shapes=[pltpu.SemaphoreType.DMA((n_copies,)), pltpu.SemaphoreType.DMA((n_copies,))],
    )(*xs)


def _plan_sibling(c, chip, g, out, shape, count_only=False):
    if count_only:
        return [None] * N_CHIPS
    q = shape[1] // 4
    return [(g.at[j, _piece(q, 2 * (1 - c), 2)], out.at[j], "sib") for j in range(N_CHIPS)]


def _plan_first(c, chip, p, out, shape, count_only=False):
    if count_only:
        return [None] * 4
    q = shape[1] // 2
    return [(p.at[chip["nx"], _piece(q, 0)], out.at[0], "nx"), (p.at[chip["diag"], _piece(q, 0)], out.at[1], "nx"),
            (p.at[chip["ny"], _piece(q, 1)], out.at[2], "ny"), (p.at[chip["diag"], _piece(q, 1)], out.at[3], "ny")]


def _plan_second(c, chip, p, out, shape, count_only=False):
    if count_only:
        return [None] * 2
    return [(p.at[1], out.at[0], "ny"), (p.at[3], out.at[1], "nx")]


def _rs_last(ps):
    nt = len(ps)

    def body(*refs):
        p_refs, o_refs = refs[:nt], refs[nt:2 * nt]
        send_sems, recv_sems = refs[2 * nt:]
        c, _, peer = _place()
        sems = (send_sems, recv_sems)
        cps = []
        for t in range(nt):
            q = ps[t].shape[0] // 4
            mine = _piece(q, 2 * c, 2)
            cps.append(_remote(p_refs[t].at[mine], o_refs[t].at[mine], sems, t, peer["sib"]))
            cps[-1].start()
        for t in range(nt):
            q = ps[t].shape[0] // 4
            theirs = _piece(q, 2 * (1 - c), 2)
            cps[t].wait_send()
            _remote(p_refs[t].at[theirs], o_refs[t].at[theirs], sems, t, peer["sib"]).wait_recv()

    return pl.pallas_call(
        body, name="rs_last", in_specs=[_HBM] * nt, out_specs=[_HBM] * nt,
        out_shape=[jax.ShapeDtypeStruct(p.shape, F32) for p in ps],
        input_output_aliases={t: t for t in range(nt)},
        scratch_shapes=[pltpu.SemaphoreType.DMA((nt,)), pltpu.SemaphoreType.DMA((nt,))],
    )(*ps)


def _add_stage(name, grid, a_list, b_list, a_map, b_map, tbs, out_shapes, out_map, out_dtype, prefetch=None):
    nt = len(a_list)
    lead = lambda shape: (None,) * (len(shape) - 2)

    def kern(*refs):
        refs = refs[(1 if prefetch is not None else 0):]
        for t in range(nt):
            refs[2 * nt + t][...] = (refs[t][...].astype(F32) + refs[nt + t][...].astype(F32)).astype(out_dtype)

    in_specs = [pl.BlockSpec(lead(a.shape) + (tb, a.shape[-1]), a_map) for a, tb in zip(a_list, tbs)]
    in_specs += [pl.BlockSpec(lead(b.shape) + (tb, b.shape[-1]), b_map) for b, tb in zip(b_list, tbs)]
    out_specs = [pl.BlockSpec(lead(s) + (tb, s[-1]), out_map) for s, tb in zip(out_shapes, tbs)]
    out_shape = [jax.ShapeDtypeStruct(s, out_dtype) for s in out_shapes]
    if prefetch is None:
        return pl.pallas_call(kern, name=name, grid=grid, in_specs=in_specs, out_specs=out_specs, out_shape=out_shape,
                              compiler_params=_cp(len(grid)))(*a_list, *b_list)
    return pl.pallas_call(
        kern, name=name,
        grid_spec=pltpu.PrefetchScalarGridSpec(num_scalar_prefetch=1, grid=grid, in_specs=in_specs, out_specs=out_specs),
        out_shape=out_shape, compiler_params=_cp(len(grid)))(prefetch, *a_list, *b_list)


def _reduce_scatter_chips(gs):
    x, y, c = lax.axis_index("x"), lax.axis_index("y"), lax.axis_index("c")
    me, nx, ny = 2 * x + y, 2 * (1 - x) + y, 2 * x + (1 - y)
    st = RS_STEPS
    unit = 4 * st * 2 * SUBLANES
    rows = [g.shape[1] for g in gs]
    gs = [jnp.pad(g, ((0, 0), (0, -g.shape[1] % unit), (0, 0))) for g in gs]
    qs = [g.shape[1] // 4 for g in gs]
    tbs = [q // st for q in qs]
    cols = [g.shape[2] for g in gs]
    core = jnp.reshape(c, (1,)).astype(jnp.int32)

    got = _exchange("rs_sibling", gs, [(N_CHIPS, 2 * q, cc) for q, cc in zip(qs, cols)], _plan_sibling)
    p0 = _add_stage("rs_add_sibling", (N_CHIPS, 2, st), gs, got,
                    lambda j, h, s, c_ref: (j, (2 * c_ref[0] + h) * st + s, 0), lambda j, h, s, c_ref: (j, h * st + s, 0),
                    tbs, [(N_CHIPS, 2 * q, cc) for q, cc in zip(qs, cols)], lambda j, h, s, c_ref: (j, h * st + s, 0),
                    BF16, prefetch=core)
    got = _exchange("rs_first", p0, [(4, q, cc) for q, cc in zip(qs, cols)], _plan_first)
    p1 = _add_stage("rs_add_first", (4, st), p0, got,
                    lambda k, s, i_ref: (i_ref[k], (k // 2) * st + s, 0), lambda k, s, i_ref: (k, s, 0),
                    tbs, [(4, q, cc) for q, cc in zip(qs, cols)], lambda k, s, i_ref: (k, s, 0),
                    BF16, prefetch=jnp.stack([me, ny, me, nx]).astype(jnp.int32))
    got = _exchange("rs_second", p1, [(2, q, cc) for q, cc in zip(qs, cols)], _plan_second)
    p2 = _add_stage("rs_add_second", (2, st), p1, got, lambda h, s, c_ref: (2 * h, s, 0), lambda h, s, c_ref: (h, s, 0),
                    tbs, [(4 * q, cc) for q, cc in zip(qs, cols)], lambda h, s, c_ref: ((2 * c_ref[0] + h) * st + s, 0),
                    F32, prefetch=core)
    return [out[:r] for out, r in zip(_rs_last(p2), rows)]


def _adamw(g, w, m, v, name):
    rows, cc = g.shape
    tr = _tile(rows, max(SUBLANES, (1 << 18) // cc), SUBLANES)
    k1 = 1.0 - ADAM_B1 ** ADAM_STEP
    k2 = 1.0 - ADAM_B2 ** ADAM_STEP

    def kern(g_ref, w_ref, m_ref, v_ref, d_ref, nm_ref, nv_ref):
        gv = g_ref[...]
        nm = ADAM_B1 * m_ref[...] + (1.0 - ADAM_B1) * gv
        nv = ADAM_B2 * v_ref[...] + (1.0 - ADAM_B2) * (gv * gv)
        nm_ref[...] = nm
        nv_ref[...] = nv
        d_ref[...] = -ADAM_LR * ((nm / k1) / (jnp.sqrt(nv / k2) + ADAM_EPS) + ADAM_WD * w_ref[...])

    spec = pl.BlockSpec((tr, cc), lambda t: (t, 0))
    return pl.pallas_call(
        kern, name=name, grid=(rows // tr,), in_specs=[spec] * 4, out_specs=[spec] * 3,
        out_shape=[jax.ShapeDtypeStruct((rows, cc), F32)] * 3, compiler_params=_cp(1),
    )(g, w, m, v)


def _flat_pad(parts, total):
    flat = jnp.concatenate([p.reshape(-1) for p in parts])
    return jnp.pad(flat, (0, total - flat.shape[0]))


def _split_flat(flat, shapes):
    out, pos = [], 0
    for shp in shapes:
        size = math.prod(shp)
        out.append(flat[pos:pos + size].reshape(shp))
        pos += size
    return out


def _cols_of_chunks(chunks, lo, hi):
    width = chunks.shape[2]
    parts = []
    for j in range(chunks.shape[0]):
        a, b = max(lo, j * width), min(hi, (j + 1) * width)
        if a < b:
            parts.append(chunks[j, :, a - j * width:b - j * width])
    return parts[0] if len(parts) == 1 else jnp.concatenate(parts, axis=1)


def _chunks_of_cols(segments, n_chunks):
    total = sum(s.shape[1] for s in segments)
    width = total // n_chunks
    chunks = []
    for j in range(n_chunks):
        lo, hi, pos, parts = j * width, (j + 1) * width, 0, []
        for s in segments:
            a, b = max(lo, pos), min(hi, pos + s.shape[1])
            if a < b:
                parts.append(s[:, a - pos:b - pos])
            pos += s.shape[1]
        chunks.append(parts[0] if len(parts) == 1 else jnp.concatenate(parts, axis=1))
    return jnp.stack(chunks)


_WEIGHTS = ['w_ada', 'b_ada', 'g_norm1', 'w_in', 'w_rnn_conv', 'b_rnn_conv', 'w_lru_a', 'b_lru_a', 'w_lru_i', 'b_lru_i',
            'lru_lambda', 'b_fgate', 'w_proj_rnn', 'w_proj_attn', 'w_out', 'g_norm2', 'w_ffn_up', 'w_ffn_conv',
            'b_ffn_conv', 'w_ffn_down', 'w_ada_final', 'b_ada_final', 'g_final']
_MATMUL = ['w_in', 'w_proj_rnn', 'w_proj_attn', 'w_out', 'w_ffn_up', 'w_ffn_down']
_ADA = ['w_ada', 'w_ada_final']
_ADA_BIAS = ['b_ada', 'b_ada_final']
_CONV = ['w_rnn_conv', 'w_ffn_conv']
_REPLICATED = [n for n in _WEIGHTS if n not in _MATMUL + _ADA + _ADA_BIAS + _CONV]


def kernel(x, c, w_ada, b_ada, g_norm1, w_in, w_rnn_conv, b_rnn_conv, w_lru_a, b_lru_a, w_lru_i, b_lru_i, lru_lambda, b_fgate, w_proj_rnn, w_proj_attn, w_out, g_norm2, w_ffn_up, w_ffn_conv, b_ffn_conv, w_ffn_down, w_ada_final, b_ada_final, g_final, loss_target, m_w_ada, m_b_ada, m_g_norm1, m_w_in, m_w_rnn_conv, m_b_rnn_conv, m_w_lru_a, m_b_lru_a, m_w_lru_i, m_b_lru_i, m_lru_lambda, m_b_fgate, m_w_proj_rnn, m_w_proj_attn, m_w_out, m_g_norm2, m_w_ffn_up, m_w_ffn_conv, m_b_ffn_conv, m_w_ffn_down, m_w_ada_final, m_b_ada_final, m_g_final, v_w_ada, v_b_ada, v_g_norm1, v_w_in, v_w_rnn_conv, v_b_rnn_conv, v_w_lru_a, v_b_lru_a, v_w_lru_i, v_b_lru_i, v_lru_lambda, v_b_fgate, v_w_proj_rnn, v_w_proj_attn, v_w_out, v_g_norm2, v_w_ffn_up, v_w_ffn_conv, v_b_ffn_conv, v_w_ffn_down, v_w_ada_final, v_b_ada_final, v_g_final):
    args = locals()
    shape_of = {n: args[n].shape for n in _WEIGHTS}

    def view(a):
        if a.ndim >= 3:
            return a[0]
        return a[None, :] if a.ndim == 1 else a

    w2 = {n: view(args[n]) for n in _WEIGHTS}
    m2 = {n: args['m_' + n].reshape(w2[n].shape) for n in _WEIGHTS}
    v2 = {n: args['v_' + n].reshape(w2[n].shape) for n in _WEIGHTS}

    bl, s, d = x.shape
    t = bl * s
    nh = b_fgate.shape[-1]
    nb, rb = w_lru_a.shape[1], w_lru_a.shape[2]
    dr = nb * rb
    da = w2['w_proj_attn'].shape[0] * N_CHIPS
    dh = da // nh
    dff = w2['w_ffn_conv'].shape[1] * N_CHIPS
    scale = dh ** -0.5
    chip = 2 * lax.axis_index("x") + lax.axis_index("y")
    dev = 2 * chip + lax.axis_index("c")

    names = list(_MATMUL)
    n_conv = sum(w2[n].size for n in _CONV)
    rows_conv = -(-n_conv // (FLAT_COLS * 32)) * 32
    conv_local = _flat_pad([w2[n] for n in _CONV], rows_conv * FLAT_COLS).reshape(rows_conv, FLAT_COLS)
    *weights_all, conv_all = _all_gather_chips([w2[n].astype(BF16) for n in names] + [conv_local], "ag_weights")
    gathered = dict(zip(names, weights_all))
    conv_all = conv_all.reshape(N_CHIPS, -1)
    conv_full, pos = {}, 0
    for n in _CONV:
        r, n4 = w2[n].shape
        blocks = conv_all[:, pos:pos + r * n4].reshape(N_CHIPS, r, n4)
        conv_full[n] = jnp.concatenate([blocks[j] for j in range(N_CHIPS)], axis=1)
        pos += r * n4
    rowmajor = lambda n: gathered[n].reshape(-1, gathered[n].shape[2])
    w_proj_rnn_f, w_proj_attn_f, w_out_f, w_ffn_down_f = (rowmajor(n) for n in ('w_proj_rnn', 'w_proj_attn', 'w_out', 'w_ffn_down'))
    up_chunk = w2['w_ffn_up'].shape[1]

    o_q, o_k, o_fl = 2 * dr, 2 * dr + da, 2 * dr + 3 * da
    o_mg = o_fl + nh
    g_in_w = gathered['w_in']
    w_rnn, w_q = _cols_of_chunks(g_in_w, 0, o_q), _cols_of_chunks(g_in_w, o_q, o_k)
    w_kv, w_mg = _cols_of_chunks(g_in_w, o_k, o_fl), _cols_of_chunks(g_in_w, o_mg, o_mg + 2 * d)
    w_fl = jnp.pad(_cols_of_chunks(g_in_w, o_fl, o_mg), ((0, 0), (0, LANES - nh)))
    w_rest = jnp.concatenate([w_q, w_kv, w_mg], axis=1)
    bf_pad = jnp.pad(w2['b_fgate'], ((0, 0), (0, LANES - nh)))

    nd = 2 * N_CHIPS
    c_act = _silu_pad(_all_gather_devices([c], "ag_cond")[0].reshape(nd * bl, d), nd * bl)
    my_cols = lambda a, n: lax.dynamic_slice_in_dim(a, chip * w2[n].shape[1], w2[n].shape[1], axis=1)
    mod_cols = [_mm(c_act, w2[n].astype(BF16), "nn", name=n + "_fwd", bias=my_cols(w2[b], n)) for n, b in zip(_ADA, _ADA_BIAS)]
    my_rows = lambda g: lax.dynamic_slice_in_dim(g, dev * bl, bl, axis=1).transpose(1, 0, 2).reshape(bl, -1)
    mod, modf = (my_rows(g) for g in _all_gather_chips(mod_cols, "ag_mod"))
    sh1, sc1, gt1, sh2, sc2, gt2 = [mod[:, i * d:(i + 1) * d].reshape(bl, 1, d) for i in range(6)]
    shf, scf = modf[:, :d].reshape(bl, 1, d), modf[:, d:].reshape(bl, 1, d)

    h1 = _norm_mod_fwd(x, w2['g_norm1'], sh1, sc1)
    h1f = h1.reshape(t, d)
    zr = _mm(h1f, w_rnn, "nn", name="in_rnn", tn=dr).reshape(bl, s, 2 * dr)
    q3 = _mm(h1f, w_q, "nn", name="in_q", out_dtype=BF16, scale=scale).reshape(bl, s, da)
    kv3 = _mm(h1f, w_kv, "nn", name="in_kv", out_dtype=BF16, tn=2 * da).reshape(bl, s, 2 * da)
    mg3 = _mm(h1f, w_mg, "nn", name="in_mg", tn=2 * d).reshape(bl, s, 2 * d)
    zf3 = _mm(h1f, w_fl, "nn", name="in_fl").reshape(bl, s, LANES)

    lru = (conv_full['w_rnn_conv'], w2['b_rnn_conv'], w2['w_lru_a'], w2['b_lru_a'], w2['w_lru_i'], w2['b_lru_i'], w2['lru_lambda'])
    y_rnn, *rnn_kept = _rnn_fwd(zr, *lru)

    f3 = _fgate_fwd(zf3, bf_pad)
    f_heads = f3[:, :, :nh].transpose(0, 2, 1).reshape(bl * nh, s)
    fcol = jnp.broadcast_to(f_heads[:, :, None], (bl * nh, s, LANES))
    frow = f_heads.reshape(bl * nh, 1, s)
    o3, lse_row = _attn_fwd(q3, kv3, fcol, frow, nh)

    pr3 = _mm(y_rnn.reshape(t, dr), w_proj_rnn_f, "nn", name="proj_rnn").reshape(bl, s, d)
    pa3 = _mm(o3.reshape(t, da), w_proj_attn_f, "nn", name="proj_attn").reshape(bl, s, d)
    merged = _merge_fwd(mg3, pr3, pa3)
    mo3 = _mm(merged.reshape(t, d), w_out_f, "nn", name="mix_out").reshape(bl, s, d)
    x1, h2 = _resid_norm_fwd(x, mo3, gt1, w2['g_norm2'], sh2, sc2)
    h2f = h2.reshape(t, d)
    up3 = _mm(h2f, gathered['w_ffn_up'], "nn", name="ffn_up", b_chunk=up_chunk, tn=up_chunk).reshape(bl, s, 2 * dff)
    act3 = _ffn_act_fwd(up3, conv_full['w_ffn_conv'], w2['b_ffn_conv'])
    yf3 = _mm(act3.reshape(t, dff), w_ffn_down_f, "nn", name="ffn_down").reshape(bl, s, d)

    dx2, dyf, dgt2, dshf, dscf, dg_final, loss_part = _final_fwd_bwd(x1, yf3, gt2, w2['g_final'], shf, scf, loss_target)
    loss = lax.psum(loss_part[0, 0], ("x", "y", "c"))

    dyf_f = dyf.reshape(t, d)
    g_ffn_down = _mm(act3.reshape(t, dff), dyf_f, "tn", name="dw_ffn_down")
    dact3 = _mm(dyf_f, w_ffn_down_f, "nt", name="d_ffn_act", tn=dff // 2).reshape(bl, s, dff)
    dgf, duf, g_ffn_conv, g_b_ffn_conv = _ffn_act_bwd(up3, dact3, conv_full['w_ffn_conv'], w2['b_ffn_conv'])
    dgf_f, duf_f = dgf.reshape(t, dff), duf.reshape(t, dff)
    g_ffn_up = jnp.concatenate([_mm(h2f, dgf_f, "tn", name="dw_ffn_up_gate", out_chunk=up_chunk, tn=up_chunk),
                                _mm(h2f, duf_f, "tn", name="dw_ffn_up_value", out_chunk=up_chunk, tn=up_chunk)], axis=0)
    dh2 = _mm([dgf_f, duf_f], gathered['w_ffn_up'], "nt", name="d_h2", b_chunk=up_chunk).reshape(bl, s, d)
    dmo, dgt1, dx1, dsh2, dsc2, dg_norm2 = _norm_mod_bwd(dh2, x1, dx2, w2['g_norm2'], sc2, "norm2_bwd", gated=(mo3, gt1))

    dmo_f = dmo.reshape(t, d)
    g_out = _mm(merged.reshape(t, d), dmo_f, "tn", name="dw_out")
    dm3 = _mm(dmo_f, w_out_f, "nt", name="d_merged").reshape(bl, s, d)
    dpr, dpa, dmr, dma = _merge_bwd(dm3, mg3, pr3, pa3)
    g_proj_rnn = _mm(y_rnn.reshape(t, dr), dpr.reshape(t, d), "tn", name="dw_proj_rnn")
    g_proj_attn = _mm(o3.reshape(t, da), dpa.reshape(t, d), "tn", name="dw_proj_attn")
    dyr3 = _mm(dpr.reshape(t, d), w_proj_rnn_f, "nt", name="d_y_rnn", tn=dr).reshape(bl, s, dr)
    do3 = _mm(dpa.reshape(t, d), w_proj_attn_f, "nt", name="d_y_attn").reshape(bl, s, da)

    dq3, dk3, dv3, dfk, dfq = _attn_bwd(q3, kv3, do3, o3, lse_row, fcol, frow, nh, scale)
    heads_last = lambda a: jnp.pad(a.reshape(bl, nh, s).transpose(0, 2, 1), ((0, 0), (0, 0), (0, LANES - nh)))
    dzf3, g_bf = _fgate_bwd(heads_last(dfk[:, :, 0]), heads_last(dfq), zf3, bf_pad)

    dxr, dgr, g_rnn_conv, g_b_rnn_conv, g_lru_a, g_b_lru_a, g_lru_i, g_b_lru_i, g_lam = _rnn_bwd(
        zr, rnn_kept, dyr3, lru[0], lru[2], lru[4], lru[6])

    dz = [a.reshape(t, -1) for a in (dxr, dgr, dq3, dk3, dv3, dmr, dma)]
    dzf_f = dzf3.reshape(t, LANES)
    seg_names = ("xr", "gr", "q", "k", "v", "mr", "ma")
    g_seg = [_mm(h1f, a, "tn", name="dw_in_" + n) for n, a in zip(seg_names, dz)]
    g_in_fl = _mm(h1f, dzf_f, "tn", name="dw_in_fl")[:, :nh]
    g_in = _chunks_of_cols(g_seg[:5] + [g_in_fl] + g_seg[5:], N_CHIPS)
    dh1 = _mm(dzf_f, w_fl, "nt", name="d_h1_fl")
    dh1 = _mm(dz[:2], w_rnn, "nt", name="d_h1_rnn", add=dh1)
    dh1 = _mm(dz[2:], w_rest, "nt", name="d_h1", add=dh1).reshape(bl, s, d)
    grad_x, dsh1, dsc1, dg_norm1 = _norm_mod_bwd(dh1, x, dx1, w2['g_norm1'], sc1, "norm1_bwd")

    dmods = [jnp.concatenate([dsh1, dsc1, dgt1, dsh2, dsc2, dgt2], axis=-1).reshape(bl, -1),
             jnp.concatenate([dshf, dscf], axis=-1).reshape(bl, -1)]
    dmods = [g.reshape(nd * bl, -1) for g in _all_gather_devices(dmods, "ag_dmod")]
    grad = {n: _mm(c_act, my_cols(g, n), "tn", name="dw_" + n) for n, g in zip(_ADA, dmods)}
    grad.update({b: _rowsum(g, "d" + b) for b, g in zip(_ADA_BIAS, dmods)})

    rowchunks = lambda g: g.reshape(N_CHIPS, g.shape[0] // N_CHIPS, g.shape[1])
    full = dict(w_in=g_in, w_proj_rnn=rowchunks(g_proj_rnn), w_proj_attn=rowchunks(g_proj_attn),
                w_out=rowchunks(g_out), w_ffn_up=g_ffn_up, w_ffn_down=rowchunks(g_ffn_down))
    small = dict(g_norm1=dg_norm1, w_rnn_conv=g_rnn_conv, b_rnn_conv=g_b_rnn_conv, w_lru_a=g_lru_a,
                 b_lru_a=g_b_lru_a, w_lru_i=g_lru_i, b_lru_i=g_b_lru_i, lru_lambda=g_lam, b_fgate=g_bf[:, :nh],
                 g_norm2=dg_norm2, w_ffn_conv=g_ffn_conv, b_ffn_conv=g_b_ffn_conv, g_final=dg_final)

    small_names = _REPLICATED + _CONV
    n_small = sum(small[n].size for n in small_names)
    rows_q = -(-n_small // (N_CHIPS * FLAT_COLS * 32 * RS_STEPS)) * 32 * RS_STEPS
    small_flat = _flat_pad([small[n] for n in small_names], N_CHIPS * rows_q * FLAT_COLS).reshape(N_CHIPS, rows_q, FLAT_COLS)
    reduced = _reduce_scatter_chips([full[n] for n in names] + [small_flat])
    grad.update(zip(names, reduced[:-1]))
    small_all = _all_gather_chips([reduced[-1]], "ag_small_grads")[0].reshape(-1)
    grad.update(zip(small_names, _split_flat(small_all, [small[n].shape for n in small_names])))
    for n in _CONV:
        n4 = w2[n].shape[1]
        grad[n] = lax.dynamic_slice_in_dim(grad[n], chip * n4, n4, axis=1)

    delta_w, new_m, new_v = {}, {}, {}
    for n in names + _ADA:
        delta_w[n], new_m[n], new_v[n] = _adamw(grad[n], w2[n], m2[n], v2[n], "adamw_" + n)
    small_names = small_names + _ADA_BIAS
    rows_small = -(-sum(w2[n].size for n in small_names) // (FLAT_COLS * SUBLANES)) * SUBLANES
    flat_small = lambda src: _flat_pad([src[n] for n in small_names], rows_small * FLAT_COLS).reshape(rows_small, FLAT_COLS)
    small_out = _adamw(flat_small(grad), flat_small(w2), flat_small(m2), flat_small(v2), "adamw_small")
    for dst, flat in zip((delta_w, new_m, new_v), small_out):
        dst.update(zip(small_names, _split_flat(flat.reshape(-1), [w2[n].shape for n in small_names])))

    out = [loss, grad_x]
    for src in (grad, delta_w, new_m, new_v):
        out += [src[n].reshape(shape_of[n]) for n in _WEIGHTS]
    return tuple(out)
```

```python
import functools
import math

import jax
import jax.numpy as jnp
from jax import lax
from jax.experimental import pallas as pl
from jax.experimental.pallas import tpu as pltpu

F32 = jnp.float32
BF16 = jnp.bfloat16
MESH = pl.DeviceIdType.MESH

RMS_EPS = 1e-6
LRU_C = 8.0
ADAM_LR = 0.001
ADAM_B1 = 0.9
ADAM_B2 = 0.999
ADAM_EPS = 1e-08
ADAM_WD = 0.01
ADAM_STEP = 10

LANES = 128
SUBLANES = 8
N_CHIPS = 4
FLAT_COLS = 1024
SCAN_SEGMENTS = 2 * SUBLANES
VMEM_LIMIT = 48 * 1024 * 1024
NEG_BIG = -1e30


def _cp(n_axes):
    return pltpu.CompilerParams(dimension_semantics=("arbitrary",) * n_axes, vmem_limit_bytes=VMEM_LIMIT)


def _tile(n, target, align):
    if n <= target:
        return n
    t = (target // align) * align
    while t >= align:
        if n % t == 0:
            return t
        t -= align
    return n


def _nice_rows(n, align):
    r = -(-n // align) * align
    while True:
        if r <= 640:
            return r, r
        t = _tile(r, 640, align)
        if 128 <= t <= 640:
            return r, t
        r += align


def _sigmoid(x):
    return 0.5 * jnp.tanh(0.5 * x) + 0.5


def _softplus(x):
    return jnp.maximum(x, 0.0) + jnp.log1p(jnp.exp(-jnp.abs(x)))


def _expm1(x, exp_x):
    small = x * (1.0 + 0.5 * x * (1.0 + (1.0 / 3.0) * x * (1.0 + 0.25 * x)))
    return jnp.where(jnp.abs(x) < 0.05, small, exp_x - 1.0)


_GELU_K = math.sqrt(2.0 / math.pi)
_GELU_C = 0.044715


def _gelu(x):
    t = jnp.tanh(_GELU_K * (x + _GELU_C * x * x * x))
    return 0.5 * x * (1.0 + t)


def _gelu_and_grad(x):
    t = jnp.tanh(_GELU_K * (x + _GELU_C * x * x * x))
    g = 0.5 * x * (1.0 + t)
    dg = 0.5 * (1.0 + t) + 0.5 * x * (1.0 - t * t) * _GELU_K * (1.0 + 3.0 * _GELU_C * x * x)
    return g, dg


def _shift_down(x, k):
    if k == 0:
        return x
    y = pltpu.roll(x, k, 0)
    rows = lax.broadcasted_iota(jnp.int32, (SUBLANES, x.shape[1]), 0)
    return jnp.concatenate([jnp.where(rows >= k, y[:SUBLANES], 0.0), y[SUBLANES:]], axis=0)


def _shift_up(x, k):
    if k == 0:
        return x
    s = x.shape[0]
    y = pltpu.roll(x, s - k, 0)
    rows = lax.broadcasted_iota(jnp.int32, (SUBLANES, x.shape[1]), 0)
    return jnp.concatenate([y[:s - SUBLANES], jnp.where(rows < SUBLANES - k, y[s - SUBLANES:], 0.0)], axis=0)


def _dot(a, b, dims):
    return lax.dot_general(a.astype(BF16), b.astype(BF16), (dims, ((), ())), preferred_element_type=F32)


_NN = ((1,), (0,))
_NT = ((1,), (1,))
_TN = ((0,), (0,))


def _mm(a, b, mode, *, name, out_dtype=F32, scale=None, bias=None, add=None, tm=1024, tn=1024, tk=1024,
        b_chunk=None, out_chunk=None):
    pieces = list(a) if isinstance(a, (list, tuple)) else [a]
    ksize = lambda p: p.shape[0] if mode == "tn" else p.shape[1]
    if b_chunk is None:
        brows, bcols = b.shape
    else:
        brows, bcols = b.shape[1], b.shape[0] * b_chunk
    k = sum(ksize(p) for p in pieces)
    if mode == "nt":
        m, n = pieces[0].shape[0], brows
        assert bcols == k, (bcols, k)
    else:
        m, n = (pieces[0].shape[1] if mode == "tn" else pieces[0].shape[0]), bcols
        assert brows == k, (brows, k)
    tm = _tile(m, tm, LANES)
    ncut = n
    if b_chunk is not None and mode != "nt":
        ncut = b_chunk
    if out_chunk is not None:
        ncut = math.gcd(ncut, out_chunk)
    tn = _tile(ncut, tn, LANES)
    kcut = b_chunk if (b_chunk is not None and mode == "nt") else k
    for p in pieces:
        kcut = math.gcd(kcut, ksize(p))
    tk = _tile(kcut, tk, LANES)
    nk = k // tk
    dims = {"nn": _NN, "nt": _NT, "tn": _TN}[mode]
    counts = [ksize(p) // tk for p in pieces]
    starts = [sum(counts[:i]) for i in range(len(pieces))]
    n_pieces = len(pieces)

    def a_spec(s0, cnt):
        kmap = (lambda kk: kk) if n_pieces == 1 else (lambda kk: jnp.clip(kk - s0, 0, cnt - 1))
        if mode == "tn":
            return pl.BlockSpec((tk, tm), lambda i, j, kk: (kmap(kk), i))
        return pl.BlockSpec((tm, tk), lambda i, j, kk: (i, kmap(kk)))

    if b_chunk is None:
        if mode == "nt":
            b_spec = pl.BlockSpec((tn, tk), lambda i, j, kk: (j, kk))
        else:
            b_spec = pl.BlockSpec((tk, tn), lambda i, j, kk: (kk, j))
    elif mode == "nt":
        per_b = b_chunk // tk
        b_spec = pl.BlockSpec((None, tn, tk), lambda i, j, kk: (kk // per_b, j, kk % per_b))
    else:
        per_b = b_chunk // tn
        b_spec = pl.BlockSpec((None, tk, tn), lambda i, j, kk: (j // per_b, kk, j % per_b))
    if out_chunk is None:
        out_spec = pl.BlockSpec((tm, tn), lambda i, j, kk: (i, j))
        out_shape = jax.ShapeDtypeStruct((m, n), out_dtype)
    else:
        per_o = out_chunk // tn
        out_spec = pl.BlockSpec((None, tm, tn), lambda i, j, kk: (j // per_o, i, j % per_o))
        out_shape = jax.ShapeDtypeStruct((n // out_chunk, m, out_chunk), out_dtype)
    in_specs = [a_spec(s0, cnt) for s0, cnt in zip(starts, counts)] + [b_spec]
    args = pieces + [b]
    if bias is not None:
        in_specs.append(pl.BlockSpec((1, tn), lambda i, j, kk: (0, j)))
        args.append(bias)
    if add is not None:
        in_specs.append(pl.BlockSpec((tm, tn), lambda i, j, kk: (i, j)))
        args.append(add)

    def kern(*refs):
        b_ref = refs[n_pieces]
        o_ref = refs[n_pieces + 1 + (bias is not None) + (add is not None)]

        def finish(r):
            if scale is not None:
                r = r * scale
            pos = n_pieces + 1
            if bias is not None:
                r = r + refs[pos][...]
                pos += 1
            if add is not None:
                r = r + refs[pos][...]
            o_ref[...] = r.astype(out_dtype)

        if nk == 1:
            finish(_dot(refs[0][...], b_ref[...], dims))
            return
        acc = refs[-1]
        kk = pl.program_id(2)

        @pl.when(kk == 0)
        def _():
            acc[...] = jnp.zeros_like(acc)

        if n_pieces == 1:
            acc[...] += _dot(refs[0][...], b_ref[...], dims)
        else:
            for idx in range(n_pieces):
                @pl.when((kk >= starts[idx]) & (kk < starts[idx] + counts[idx]))
                def _(idx=idx):
                    acc[...] += _dot(refs[idx][...], b_ref[...], dims)

        @pl.when(kk == nk - 1)
        def _():
            finish(acc[...])

    return pl.pallas_call(
        kern, name=name,
        grid=(m // tm, n // tn, nk),
        in_specs=in_specs, out_specs=out_spec, out_shape=out_shape,
        scratch_shapes=[pltpu.VMEM((tm, tn), F32)] if nk > 1 else [],
        compiler_params=_cp(3),
    )(*args)


def _silu_pad(c, rows):
    bl, d = c.shape

    def kern(c_ref, o_ref):
        o_ref[...] = jnp.zeros_like(o_ref)
        v = c_ref[...]
        o_ref[0:bl, :] = v * _sigmoid(v)

    return pl.pallas_call(kern, name="silu_pad", out_shape=jax.ShapeDtypeStruct((rows, d), F32))(c)


def _rowsum(x, name):
    r, n = x.shape

    def kern(x_ref, o_ref):
        o_ref[...] = jnp.sum(x_ref[...], axis=0, keepdims=True)

    return pl.pallas_call(kern, name=name, out_shape=jax.ShapeDtypeStruct((1, n), F32))(x)


def _norm_parts(x, g):
    r = lax.rsqrt(jnp.mean(x * x, axis=-1, keepdims=True) + RMS_EPS)
    xh = x * r
    return r, xh, xh * g


def _norm_bwd_parts(dh, xh, r, g, sc):
    n = xh * g
    dn = dh * (1.0 + sc)
    dxh = dn * g
    dx = r * (dxh - xh * jnp.mean(dxh * xh, axis=-1, keepdims=True))
    return dx, dh, dh * n, dn * xh


def _act_specs(ts, d, n):
    return [pl.BlockSpec((1, ts, d), lambda b, t: (b, t, 0)) for _ in range(n)]


def _vec_spec(d):
    return pl.BlockSpec((1, 1, d), lambda b, t: (b, 0, 0))


def _par_spec(d):
    return pl.BlockSpec((1, d), lambda b, t: (0, 0))


def _norm_mod_fwd(x3, g, sh, sc):
    bl, s, d = x3.shape
    ts = _tile(s, 512, SUBLANES)

    def kern(x_ref, g_ref, sh_ref, sc_ref, h_ref):
        _, _, n = _norm_parts(x_ref[0], g_ref[...])
        h_ref[0] = (n * (1.0 + sc_ref[0]) + sh_ref[0]).astype(BF16)

    return pl.pallas_call(
        kern, name="norm_mod_fwd", grid=(bl, s // ts),
        in_specs=_act_specs(ts, d, 1) + [_par_spec(d), _vec_spec(d), _vec_spec(d)],
        out_specs=_act_specs(ts, d, 1)[0],
        out_shape=jax.ShapeDtypeStruct((bl, s, d), BF16),
        compiler_params=_cp(2),
    )(x3, g, sh, sc)


def _resid_norm_fwd(x3, y3, gate, g, sh, sc):
    bl, s, d = x3.shape
    ts = _tile(s, 512, SUBLANES)

    def kern(x_ref, y_ref, gate_ref, g_ref, sh_ref, sc_ref, x1_ref, h_ref):
        x1 = x_ref[0] + gate_ref[0] * y_ref[0]
        x1_ref[0] = x1
        _, _, n = _norm_parts(x1, g_ref[...])
        h_ref[0] = (n * (1.0 + sc_ref[0]) + sh_ref[0]).astype(BF16)

    return pl.pallas_call(
        kern, name="resid_norm_fwd", grid=(bl, s // ts),
        in_specs=_act_specs(ts, d, 2) + [_vec_spec(d), _par_spec(d), _vec_spec(d), _vec_spec(d)],
        out_specs=_act_specs(ts, d, 2),
        out_shape=[jax.ShapeDtypeStruct((bl, s, d), F32), jax.ShapeDtypeStruct((bl, s, d), BF16)],
        compiler_params=_cp(2),
    )(x3, y3, gate, g, sh, sc)


def _norm_mod_bwd(dh3, x3, dres3, g, sc, name, gated=None):
    bl, s, d = x3.shape
    ts = _tile(s, 512, SUBLANES)

    def kern(dh_ref, x_ref, dres_ref, g_ref, sc_ref, *rest):
        dx_ref, dsh_ref, dsc_ref, dg_ref = rest[-4:]
        b, t = pl.program_id(0), pl.program_id(1)
        gv = g_ref[...]
        r, xh, _ = _norm_parts(x_ref[0], gv)
        dx, a, bb, cc = _norm_bwd_parts(dh_ref[0], xh, r, gv, sc_ref[0])
        dx = dres_ref[0] + dx
        dx_ref[0] = dx

        @pl.when(t == 0)
        def _():
            dsh_ref[...] = jnp.zeros_like(dsh_ref)
            dsc_ref[...] = jnp.zeros_like(dsc_ref)

        @pl.when((t == 0) & (b == 0))
        def _():
            dg_ref[...] = jnp.zeros_like(dg_ref)

        dsh_ref[0] += jnp.sum(a, axis=0, keepdims=True)
        dsc_ref[0] += jnp.sum(bb, axis=0, keepdims=True)
        dg_ref[...] += jnp.sum(cc, axis=0, keepdims=True)
        if gated is not None:
            y_ref, gate_ref, dy_ref, dgate_ref = rest[:4]
            dy_ref[0] = (gate_ref[0] * dx).astype(BF16)

            @pl.when(t == 0)
            def _():
                dgate_ref[...] = jnp.zeros_like(dgate_ref)

            dgate_ref[0] += jnp.sum(dx * y_ref[0], axis=0, keepdims=True)

    extra_in = [] if gated is None else _act_specs(ts, d, 1) + [_vec_spec(d)]
    extra_out = [] if gated is None else [_act_specs(ts, d, 1)[0], _vec_spec(d)]
    extra_shape = [] if gated is None else [jax.ShapeDtypeStruct((bl, s, d), BF16), jax.ShapeDtypeStruct((bl, 1, d), F32)]
    return pl.pallas_call(
        kern, name=name, grid=(bl, s // ts),
        in_specs=_act_specs(ts, d, 3) + [_par_spec(d), _vec_spec(d)] + extra_in,
        out_specs=extra_out + [_act_specs(ts, d, 1)[0], _vec_spec(d), _vec_spec(d), _par_spec(d)],
        out_shape=extra_shape + [jax.ShapeDtypeStruct((bl, s, d), F32), jax.ShapeDtypeStruct((bl, 1, d), F32),
                                 jax.ShapeDtypeStruct((bl, 1, d), F32), jax.ShapeDtypeStruct((1, d), F32)],
        compiler_params=_cp(2),
    )(dh3, x3, dres3, g, sc, *(gated or ()))


def _final_fwd_bwd(x1, yf, gate2, g, shf, scf, tgt):
    bl, s, d = x1.shape
    ts = _tile(s, 512, SUBLANES)

    def kern(x1_ref, yf_ref, gate_ref, g_ref, sh_ref, sc_ref, tgt_ref,
             dx_ref, dyf_ref, dgate_ref, dsh_ref, dsc_ref, dg_ref, loss_ref):
        b, t = pl.program_id(0), pl.program_id(1)
        gv, sc, yf, gate = g_ref[...], sc_ref[0], yf_ref[0], gate_ref[0]
        x2 = x1_ref[0] + gate * yf
        r, xh, n = _norm_parts(x2, gv)
        err = n * (1.0 + sc) + sh_ref[0] - tgt_ref[0]
        dx, a, bb, cc = _norm_bwd_parts(err * (1.0 / d), xh, r, gv, sc)
        dx_ref[0] = dx
        dyf_ref[0] = (gate * dx).astype(BF16)

        @pl.when(t == 0)
        def _():
            dsh_ref[...] = jnp.zeros_like(dsh_ref)
            dsc_ref[...] = jnp.zeros_like(dsc_ref)
            dgate_ref[...] = jnp.zeros_like(dgate_ref)

        @pl.when((t == 0) & (b == 0))
        def _():
            dg_ref[...] = jnp.zeros_like(dg_ref)
            loss_ref[...] = jnp.zeros_like(loss_ref)

        dsh_ref[0] += jnp.sum(a, axis=0, keepdims=True)
        dsc_ref[0] += jnp.sum(bb, axis=0, keepdims=True)
        dg_ref[...] += jnp.sum(cc, axis=0, keepdims=True)
        dgate_ref[0] += jnp.sum(dx * yf, axis=0, keepdims=True)
        tok = jnp.mean(err * err, axis=-1, keepdims=True)
        loss_ref[...] += 0.5 * jnp.sum(tok, axis=0, keepdims=True)

    vec = jax.ShapeDtypeStruct((bl, 1, d), F32)
    return pl.pallas_call(
        kern, name="final_fwd_bwd", grid=(bl, s // ts),
        in_specs=_act_specs(ts, d, 2) + [_vec_spec(d), _par_spec(d), _vec_spec(d), _vec_spec(d)] + _act_specs(ts, d, 1),
        out_specs=_act_specs(ts, d, 2) + [_vec_spec(d), _vec_spec(d), _vec_spec(d), _par_spec(d),
                                          pl.BlockSpec((1, 1), lambda b, t: (0, 0))],
        out_shape=[jax.ShapeDtypeStruct((bl, s, d), F32), jax.ShapeDtypeStruct((bl, s, d), BF16), vec, vec, vec,
                   jax.ShapeDtypeStruct((1, d), F32), jax.ShapeDtypeStruct((1, 1), F32)],
        compiler_params=_cp(2),
    )(x1, yf, gate2, g, shf, scf, tgt)


def _rnn_gates(xr, cw, cb, wa, ba, wi, bi, lam):
    kw = cw.shape[0]
    xc = cb
    for k in range(kw):
        xc = xc + _shift_down(xr, kw - 1 - k) * cw[k:k + 1, :]
    r = _sigmoid(_dot(xc, wa, _NN) + ba)
    i = _sigmoid(_dot(xc, wi, _NN) + bi)
    sp = _softplus(-lam)
    log_a = -LRU_C * r * sp
    a = jnp.exp(log_a)
    mult = jnp.sqrt(-_expm1(2.0 * log_a, a * a))
    return xc, r, i, sp, a, mult


def _segment_scan(a_s, u_s, h_s, p_s, reverse):
    s, c = a_s.shape
    seg = s // SCAN_SEGMENTS

    unroll = math.gcd(seg, 8)

    def steps(n, carry):
        h, p = carry
        for j in range(unroll):
            t = n * unroll + j
            t = (seg - 1 - t) if reverse else t
            av = a_s[pl.ds(t, SCAN_SEGMENTS, stride=seg), :]
            uv = u_s[pl.ds(t, SCAN_SEGMENTS, stride=seg), :]
            h = av * h + uv
            p = p * av
            h_s[pl.ds(t, SCAN_SEGMENTS, stride=seg), :] = h
            p_s[pl.ds(t, SCAN_SEGMENTS, stride=seg), :] = p
        return h, p

    lax.fori_loop(0, seg // unroll, steps, (jnp.zeros((SCAN_SEGMENTS, c), F32), jnp.ones((SCAN_SEGMENTS, c), F32)))
    carry = jnp.zeros((1, c), F32)
    order = range(SCAN_SEGMENTS - 1, -1, -1) if reverse else range(SCAN_SEGMENTS)
    for j in order:
        rows = pl.ds(j * seg, seg)
        fixed = h_s[rows, :] + p_s[rows, :] * carry
        h_s[rows, :] = fixed
        carry = fixed[0:1, :] if reverse else fixed[seg - 1:seg, :]


def _rnn_specs(s, rb, nb):
    act = lambda off: pl.BlockSpec((1, s, rb), lambda b, n, off=off: (b, 0, off + n))
    par = pl.BlockSpec((1, rb), lambda b, n: (0, n))
    wsp = pl.BlockSpec((1, rb, rb), lambda b, n: (n, 0, 0))
    return act, par, wsp


def _rnn_fwd(zr3, cw, cb, wa, ba, wi, bi, lam):
    bl, s, two = zr3.shape
    nb, rb, _ = wa.shape
    dr = nb * rb
    kw = cw.shape[0]
    act, par, wsp = _rnn_specs(s, rb, nb)

    def kern(xr_ref, gr_ref, cw_ref, cb_ref, wa_ref, ba_ref, wi_ref, bi_ref, lam_ref,
             y_ref, h_ref, xc_ref, r_ref, i_ref, a_ref, mult_ref, a_s, u_s, h_s, p_s):
        xc, r, i, sp, a, mult = _rnn_gates(xr_ref[0], cw_ref[...], cb_ref[...], wa_ref[0], ba_ref[...],
                                           wi_ref[0], bi_ref[...], lam_ref[...])
        for ref, val in ((xc_ref, xc), (r_ref, r), (i_ref, i), (a_ref, a), (mult_ref, mult)):
            ref[0] = val
        a_s[...] = a
        u_s[...] = mult * (i * xc)
        _segment_scan(a_s, u_s, h_s, p_s, reverse=False)
        h = h_s[...]
        h_ref[0] = h
        y_ref[0] = (_gelu(gr_ref[0]) * h).astype(BF16)

    kept = jax.ShapeDtypeStruct((bl, s, dr), F32)
    return pl.pallas_call(
        kern, name="rnn_fwd", grid=(bl, nb),
        in_specs=[act(0), act(nb), pl.BlockSpec((kw, rb), lambda b, n: (0, n)), par, wsp, par, wsp, par, par],
        out_specs=[act(0)] * 7,
        out_shape=[jax.ShapeDtypeStruct((bl, s, dr), BF16)] + [kept] * 6,
        scratch_shapes=[pltpu.VMEM((s, rb), F32)] * 4,
        compiler_params=_cp(2),
    )(zr3, zr3, cw, cb, wa, ba, wi, bi, lam)


def _rnn_bwd(zr3, kept, dy3, cw, wa, wi, lam):
    bl, s, _ = zr3.shape
    nb, rb, _ = wa.shape
    dr = nb * rb
    kw = cw.shape[0]
    act = lambda off: pl.BlockSpec((1, s, rb), lambda n, b, off=off: (b, 0, off + n))
    par = pl.BlockSpec((1, rb), lambda n, b: (0, n))
    wsp = pl.BlockSpec((1, rb, rb), lambda n, b: (n, 0, 0))
    cws = pl.BlockSpec((kw, rb), lambda n, b: (0, n))

    def kern(xr_ref, gr_ref, h_ref, xc_ref, r_ref, i_ref, a_ref, mult_ref, dy_ref, cw_ref, wa_ref, wi_ref, lam_ref,
             dxr_ref, dgr_ref, dcw_ref, dcb_ref, dwa_ref, dba_ref, dwi_ref, dbi_ref, dlam_ref, a_s, u_s, h_s, p_s):
        b = pl.program_id(1)
        xr, cwv, lamv = xr_ref[0], cw_ref[...], lam_ref[...]
        wav, wiv = wa_ref[0], wi_ref[0]
        xc, r, i, a, mult = xc_ref[0], r_ref[0], i_ref[0], a_ref[0], mult_ref[0]
        sp = _softplus(-lamv)
        h, dy = h_ref[0], dy_ref[0]
        ge, dge = _gelu_and_grad(gr_ref[0])
        dgr_ref[0] = (dy * h * dge).astype(BF16)
        a_s[...] = _shift_up(a, 1)
        u_s[...] = dy * ge
        _segment_scan(a_s, u_s, h_s, p_s, reverse=True)
        g = h_s[...]
        da = g * _shift_down(h, 1)
        ix = i * xc
        dlog_a = da * a + (g * ix) * (-(a * a) / mult)
        di = g * mult * xc
        dpa = (dlog_a * (-LRU_C * sp)) * r * (1.0 - r)
        dpi = di * i * (1.0 - i)
        dxc = g * mult * i + _dot(dpa, wav, _NT) + _dot(dpi, wiv, _NT)
        dxr = jnp.zeros_like(dxc)
        dcw_rows = []
        for k in range(kw):
            dxr = dxr + _shift_up(dxc, kw - 1 - k) * cwv[k:k + 1, :]
            dcw_rows.append(jnp.sum(dxc * _shift_down(xr, kw - 1 - k), axis=0, keepdims=True))
        dxr_ref[0] = dxr.astype(BF16)

        @pl.when(b == 0)
        def _():
            for ref in (dcw_ref, dcb_ref, dwa_ref, dba_ref, dwi_ref, dbi_ref, dlam_ref):
                ref[...] = jnp.zeros_like(ref)

        for k in range(kw):
            dcw_ref[k:k + 1, :] += dcw_rows[k]
        dcb_ref[...] += jnp.sum(dxc, axis=0, keepdims=True)
        dwa_ref[0] += _dot(xc, dpa, _TN)
        dwi_ref[0] += _dot(xc, dpi, _TN)
        dba_ref[...] += jnp.sum(dpa, axis=0, keepdims=True)
        dbi_ref[...] += jnp.sum(dpi, axis=0, keepdims=True)
        dsp = jnp.sum(dlog_a * (-LRU_C * r), axis=0, keepdims=True)
        dlam_ref[...] += dsp * (-_sigmoid(-lamv))

    vec = jax.ShapeDtypeStruct((1, dr), F32)
    wsh = jax.ShapeDtypeStruct((nb, rb, rb), F32)
    return pl.pallas_call(
        kern, name="rnn_bwd", grid=(nb, bl),
        in_specs=[act(0), act(nb)] + [act(0)] * 7 + [cws, wsp, wsp, par],
        out_specs=[act(0), act(0), cws, par, wsp, par, wsp, par, par],
        out_shape=[jax.ShapeDtypeStruct((bl, s, dr), BF16), jax.ShapeDtypeStruct((bl, s, dr), BF16),
                   jax.ShapeDtypeStruct((kw, dr), F32), vec, wsh, vec, wsh, vec, vec],
        scratch_shapes=[pltpu.VMEM((s, rb), F32)] * 4,
        compiler_params=_cp(2),
    )(zr3, zr3, *kept, dy3, cw, wa, wi, lam)


def _tri(n, upper):
    r = lax.broadcasted_iota(jnp.int32, (n, n), 0)
    c = lax.broadcasted_iota(jnp.int32, (n, n), 1)
    return jnp.where((c >= r) if upper else (c <= r), 1.0, 0.0).astype(F32)


def _fgate_fwd(zf3, bf):
    bl, s, w = zf3.shape
    ch = _tile(s, 256, SUBLANES)

    def kern(z_ref, b_ref, f_ref):
        tri = _tri(ch, upper=False)
        carry = jnp.zeros((1, w), F32)
        for j in range(s // ch):
            rows = pl.ds(j * ch, ch)
            lf = -_softplus(-(z_ref[0, rows, :] + b_ref[...]))
            out = jnp.dot(tri, lf, precision=lax.Precision.HIGHEST, preferred_element_type=F32) + carry
            f_ref[0, rows, :] = out
            carry = out[ch - 1:ch, :]

    return pl.pallas_call(
        kern, name="fgate_fwd", grid=(bl,),
        in_specs=[pl.BlockSpec((1, s, w), lambda b: (b, 0, 0)), pl.BlockSpec((1, w), lambda b: (0, 0))],
        out_specs=pl.BlockSpec((1, s, w), lambda b: (b, 0, 0)),
        out_shape=jax.ShapeDtypeStruct((bl, s, w), F32),
        compiler_params=_cp(1),
    )(zf3, bf)


def _fgate_bwd(dfk3, dfq3, zf3, bf):
    bl, s, w = zf3.shape
    ch = _tile(s, 256, SUBLANES)

    def kern(dfk_ref, dfq_ref, z_ref, b_ref, dz_ref, db_ref):
        b = pl.program_id(0)
        tri = _tri(ch, upper=True)
        carry = jnp.zeros((1, w), F32)
        dbsum = jnp.zeros((1, w), F32)
        for j in range(s // ch - 1, -1, -1):
            rows = pl.ds(j * ch, ch)
            df = dfk_ref[0, rows, :] + dfq_ref[0, rows, :]
            dlf = jnp.dot(tri, df, precision=lax.Precision.HIGHEST, preferred_element_type=F32) + carry
            carry = dlf[0:1, :]
            dz = dlf * _sigmoid(-(z_ref[0, rows, :] + b_ref[...]))
            dz_ref[0, rows, :] = dz.astype(BF16)
            dbsum = dbsum + jnp.sum(dz, axis=0, keepdims=True)

        @pl.when(b == 0)
        def _():
            db_ref[...] = jnp.zeros_like(db_ref)

        db_ref[...] += dbsum

    return pl.pallas_call(
        kern, name="fgate_bwd", grid=(bl,),
        in_specs=[pl.BlockSpec((1, s, w), lambda b: (b, 0, 0))] * 3 + [pl.BlockSpec((1, w), lambda b: (0, 0))],
        out_specs=[pl.BlockSpec((1, s, w), lambda b: (b, 0, 0)), pl.BlockSpec((1, w), lambda b: (0, 0))],
        out_shape=[jax.ShapeDtypeStruct((bl, s, w), BF16), jax.ShapeDtypeStruct((1, w), F32)],
        compiler_params=_cp(1),
    )(dfk3, dfq3, zf3, bf)


def _lanes(col, width):
    return col if width == LANES else jnp.concatenate([col] * (width // LANES), axis=1)


def _causal(sc, row0, col0, transposed):
    r = lax.broadcasted_iota(jnp.int32, sc.shape, 0) + row0
    c = lax.broadcasted_iota(jnp.int32, sc.shape, 1) + col0
    return jnp.where((c >= r) if transposed else (r >= c), sc, NEG_BIG)


def _attn_fwd(q3, kv3, fcol, frow, nh):
    bl, s, da = q3.shape
    dh = da // nh
    tq = _tile(s, 512, LANES)
    nq = s // tq

    def kern(iq_tab, ik_tab, q_ref, k_ref, v_ref, fk_ref, fq_ref, o_ref, lse_ref, m_s, l_s, acc):
        iq, ik = iq_tab[pl.program_id(2)], ik_tab[pl.program_id(2)]

        @pl.when(ik == 0)
        def _():
            m_s[...] = jnp.full_like(m_s, NEG_BIG)
            l_s[...] = jnp.zeros_like(l_s)
            acc[...] = jnp.zeros_like(acc)

        def block(masked):
            st = _dot(k_ref[0], q_ref[0], _NT) - _lanes(fk_ref[0], tq) + fq_ref[0]
            if masked:
                st = _causal(st, ik * tq, iq * tq, True)
            m_old = m_s[...]
            m_new = jnp.maximum(m_old, jnp.max(st, axis=0, keepdims=True))
            alpha = jnp.exp(m_old - m_new)
            pt = jnp.exp(st - m_new)
            l_s[...] = alpha * l_s[...] + jnp.sum(pt, axis=0, keepdims=True)
            acc[...] = alpha * acc[...] + _dot(v_ref[0], pt, _TN)
            m_s[...] = m_new

        pl.when(ik < iq)(functools.partial(block, False))

        @pl.when(ik == iq)
        def _():
            block(True)
            l = l_s[...]
            o_ref[0] = (acc[...] / l).T
            lse_ref[0] = m_s[...] + jnp.log(l)

    pairs = [(i, j) for i in range(nq) for j in range(i + 1)]
    iq_tab, ik_tab = (jnp.asarray(col, jnp.int32) for col in zip(*pairs))
    qmap = lambda b, h, p, iqt, ikt: (b, iqt[p], h)
    kmap = lambda off: (lambda b, h, p, iqt, ikt: (b, ikt[p], off + h))
    return pl.pallas_call(
        kern, name="attn_fwd",
        grid_spec=pltpu.PrefetchScalarGridSpec(
            num_scalar_prefetch=2, grid=(bl, nh, len(pairs)),
            in_specs=[pl.BlockSpec((1, tq, dh), qmap), pl.BlockSpec((1, tq, dh), kmap(0)), pl.BlockSpec((1, tq, dh), kmap(nh)),
                      pl.BlockSpec((1, tq, LANES), lambda b, h, p, iqt, ikt: (b * nh + h, ikt[p], 0)),
                      pl.BlockSpec((1, 1, tq), lambda b, h, p, iqt, ikt: (b * nh + h, 0, iqt[p]))],
            out_specs=[pl.BlockSpec((1, tq, dh), qmap),
                       pl.BlockSpec((1, 1, tq), lambda b, h, p, iqt, ikt: (b * nh + h, 0, iqt[p]))],
            scratch_shapes=[pltpu.VMEM((1, tq), F32), pltpu.VMEM((1, tq), F32), pltpu.VMEM((dh, tq), F32)]),
        out_shape=[jax.ShapeDtypeStruct((bl, s, da), F32), jax.ShapeDtypeStruct((bl * nh, 1, s), F32)],
        compiler_params=_cp(3),
    )(iq_tab, ik_tab, q3, kv3, kv3, fcol, frow)


def _attn_bwd(q3, kv3, do3, o3, lse_row, fcol, frow, nh, scale):
    bl, s, da = q3.shape
    dh = da // nh
    tk = _tile(s, 512, LANES)
    nk = s // tk

    pairs = [(j, i) for j in range(nk) for i in range(j, nk)]

    def kern(ik_tab, iq_tab, q_ref, k_ref, v_ref, do_ref, o_ref, lse_ref, fk_ref, fq_ref, dq_ref, dk_ref, dv_ref, dfk_ref,
             dfq_ref, dq_acc, dk_acc, dv_acc, dfq_acc, delta_s):
        step = pl.program_id(2)
        ik, iq = ik_tab[step], iq_tab[step]
        qrow = pl.ds(iq, 1)

        @pl.when(step == 0)
        def _():
            dq_acc[...] = jnp.zeros_like(dq_acc)
            dfq_acc[...] = jnp.zeros_like(dfq_acc)

        @pl.when(ik == 0)
        def _():
            prod = do_ref[0] * o_ref[0]
            rows = lax.dot_general(jnp.ones((SUBLANES, dh), F32), prod, (_NT, ((), ())),
                                   precision=lax.Precision.HIGHEST, preferred_element_type=F32)
            delta_s[qrow, :] = rows[0:1, :]

        @pl.when(iq == ik)
        def _():
            dk_acc[...] = jnp.zeros_like(dk_acc)
            dv_acc[...] = jnp.zeros_like(dv_acc)

        def block(masked):
            q = q_ref[0]
            st = _dot(k_ref[0], q, _NT) - _lanes(fk_ref[0], tk) + fq_ref[0]
            if masked:
                st = _causal(st, ik * tk, iq * tk, True)
            pt = jnp.exp(st - lse_ref[0])
            dv_acc[...] += _dot(pt, do_ref[0], _NN)
            dpt = _dot(v_ref[0], do_ref[0], _NT)
            dst = (pt * (dpt - delta_s[qrow, :])).astype(BF16)
            q_ones = jnp.concatenate([q, jnp.ones_like(q)], axis=1)
            dk_acc[...] += _dot(dst, q_ones, _NN)
            qrows = pl.ds(pl.multiple_of(iq * tk, tk), tk)
            dq_acc[qrows, :] += _dot(dst, k_ref[0], _TN)
            dfq_acc[qrow, :] += jnp.sum(dst.astype(F32), axis=0, keepdims=True)

        pl.when(iq > ik)(functools.partial(block, False))
        pl.when(iq == ik)(functools.partial(block, True))

        @pl.when(iq == nk - 1)
        def _():
            ext = dk_acc[...]
            dk_ref[0] = ext[:, :dh].astype(BF16)
            dfk_ref[0] = -ext[:, dh:]
            dv_ref[0] = dv_acc[...].astype(BF16)

        @pl.when(step == len(pairs) - 1)
        def _():
            dq_ref[0] = (dq_acc[...] * scale).astype(BF16)
            dfq_ref[0] = dfq_acc[...]

    ik_tab, iq_tab = (jnp.asarray(col, jnp.int32) for col in zip(*pairs))
    qmap = lambda b, h, p, ikt, iqt: (b, iqt[p], h)
    omap = lambda b, h, p, ikt, iqt: (b, jnp.where(ikt[p] == 0, iqt[p], 0), h)
    rmap = lambda b, h, p, ikt, iqt: (b * nh + h, 0, iqt[p])
    kmap = lambda off: (lambda b, h, p, ikt, iqt: (b, ikt[p], off + h))
    bmap = lambda b, h, p, ikt, iqt: (b * nh + h, ikt[p], 0)
    return pl.pallas_call(
        kern, name="attn_bwd",
        grid_spec=pltpu.PrefetchScalarGridSpec(
            num_scalar_prefetch=2, grid=(bl, nh, len(pairs)),
            in_specs=[pl.BlockSpec((1, tk, dh), qmap), pl.BlockSpec((1, tk, dh), kmap(0)), pl.BlockSpec((1, tk, dh), kmap(nh)),
                      pl.BlockSpec((1, tk, dh), qmap), pl.BlockSpec((1, tk, dh), omap), pl.BlockSpec((1, 1, tk), rmap),
                      pl.BlockSpec((1, tk, LANES), bmap), pl.BlockSpec((1, 1, tk), rmap)],
            out_specs=[pl.BlockSpec((1, s, dh), lambda b, h, p, ikt, iqt: (b, 0, h)),
                       pl.BlockSpec((1, tk, dh), kmap(0)), pl.BlockSpec((1, tk, dh), kmap(0)),
                       pl.BlockSpec((1, tk, LANES), bmap),
                       pl.BlockSpec((1, nk, tk), lambda b, h, p, ikt, iqt: (b * nh + h, 0, 0))],
            scratch_shapes=[pltpu.VMEM((s, dh), F32), pltpu.VMEM((tk, 2 * dh), F32), pltpu.VMEM((tk, dh), F32),
                            pltpu.VMEM((nk, tk), F32), pltpu.VMEM((nk, tk), F32)]),
        out_shape=[jax.ShapeDtypeStruct((bl, s, da), BF16), jax.ShapeDtypeStruct((bl, s, da), BF16),
                   jax.ShapeDtypeStruct((bl, s, da), BF16), jax.ShapeDtypeStruct((bl * nh, s, LANES), F32),
                   jax.ShapeDtypeStruct((bl * nh, nk, tk), F32)],
        compiler_params=_cp(3),
    )(ik_tab, iq_tab, q3, kv3, kv3, do3, o3, lse_row, fcol, frow)


def _merge_fwd(mg3, pr3, pa3):
    bl, s, d = pr3.shape
    ts = _tile(s, 512, SUBLANES)
    half = lambda j: pl.BlockSpec((1, ts, d), lambda b, t, j=j: (b, t, j))

    def kern(mr_ref, ma_ref, pr_ref, pa_ref, o_ref):
        o_ref[0] = (_sigmoid(mr_ref[0]) * pr_ref[0] + _sigmoid(ma_ref[0]) * pa_ref[0]).astype(BF16)

    return pl.pallas_call(
        kern, name="merge_fwd", grid=(bl, s // ts),
        in_specs=[half(0), half(1)] + _act_specs(ts, d, 2), out_specs=_act_specs(ts, d, 1)[0],
        out_shape=jax.ShapeDtypeStruct((bl, s, d), BF16), compiler_params=_cp(2),
    )(mg3, mg3, pr3, pa3)


def _merge_bwd(dm3, mg3, pr3, pa3):
    bl, s, d = pr3.shape
    ts = _tile(s, 512, SUBLANES)
    half = lambda j: pl.BlockSpec((1, ts, d), lambda b, t, j=j: (b, t, j))

    def kern(dm_ref, mr_ref, ma_ref, pr_ref, pa_ref, dpr_ref, dpa_ref, dmr_ref, dma_ref):
        dm = dm_ref[0]
        gr, ga = _sigmoid(mr_ref[0]), _sigmoid(ma_ref[0])
        dpr_ref[0] = (gr * dm).astype(BF16)
        dpa_ref[0] = (ga * dm).astype(BF16)
        dmr_ref[0] = (dm * pr_ref[0] * gr * (1.0 - gr)).astype(BF16)
        dma_ref[0] = (dm * pa_ref[0] * ga * (1.0 - ga)).astype(BF16)

    return pl.pallas_call(
        kern, name="merge_bwd", grid=(bl, s // ts),
        in_specs=_act_specs(ts, d, 1) + [half(0), half(1)] + _act_specs(ts, d, 2), out_specs=_act_specs(ts, d, 4),
        out_shape=[jax.ShapeDtypeStruct((bl, s, d), BF16)] * 4, compiler_params=_cp(2),
    )(dm3, mg3, mg3, pr3, pa3)


def _ffn_conv(gf, cw, cb):
    kw = cw.shape[0]
    y = cb
    for k in range(kw):
        y = y + _shift_down(gf, kw - 1 - k) * cw[k:k + 1, :]
    return y


def _ffn_act_fwd(up3, cw, cb):
    bl, s, two = up3.shape
    dff = two // 2
    kw = cw.shape[0]
    tc = _tile(dff, 512, LANES)
    nc = dff // tc

    def kern(gf_ref, uf_ref, cw_ref, cb_ref, o_ref):
        o_ref[0] = (_gelu(_ffn_conv(gf_ref[0], cw_ref[...], cb_ref[...])) * uf_ref[0]).astype(BF16)

    act = lambda off: pl.BlockSpec((1, s, tc), lambda b, j, off=off: (b, 0, off + j))
    return pl.pallas_call(
        kern, name="ffn_act_fwd", grid=(bl, nc),
        in_specs=[act(0), act(nc), pl.BlockSpec((kw, tc), lambda b, j: (0, j)), pl.BlockSpec((1, tc), lambda b, j: (0, j))],
        out_specs=act(0), out_shape=jax.ShapeDtypeStruct((bl, s, dff), BF16), compiler_params=_cp(2),
    )(up3, up3, cw, cb)


def _ffn_act_bwd(up3, dact3, cw, cb):
    bl, s, two = up3.shape
    dff = two // 2
    kw = cw.shape[0]
    tc = _tile(dff, 256, LANES)
    nc = dff // tc

    def kern(gf_ref, uf_ref, da_ref, cw_ref, cb_ref, dgf_ref, duf_ref, dcw_ref, dcb_ref):
        b = pl.program_id(1)
        gf, cwv, da = gf_ref[0], cw_ref[...], da_ref[0]
        ge, dge = _gelu_and_grad(_ffn_conv(gf, cwv, cb_ref[...]))
        duf_ref[0] = (da * ge).astype(BF16)
        dgc = da * uf_ref[0] * dge
        dgf = jnp.zeros_like(dgc)
        rows = []
        for k in range(kw):
            dgf = dgf + _shift_up(dgc, kw - 1 - k) * cwv[k:k + 1, :]
            rows.append(jnp.sum(dgc * _shift_down(gf, kw - 1 - k), axis=0, keepdims=True))
        dgf_ref[0] = dgf.astype(BF16)

        @pl.when(b == 0)
        def _():
            dcw_ref[...] = jnp.zeros_like(dcw_ref)
            dcb_ref[...] = jnp.zeros_like(dcb_ref)

        for k in range(kw):
            dcw_ref[k:k + 1, :] += rows[k]
        dcb_ref[...] += jnp.sum(dgc, axis=0, keepdims=True)

    act = lambda off: pl.BlockSpec((1, s, tc), lambda j, b, off=off: (b, 0, off + j))
    cws = pl.BlockSpec((kw, tc), lambda j, b: (0, j))
    cbs = pl.BlockSpec((1, tc), lambda j, b: (0, j))
    return pl.pallas_call(
        kern, name="ffn_act_bwd", grid=(nc, bl),
        in_specs=[act(0), act(nc), act(0), cws, cbs], out_specs=[act(0), act(0), cws, cbs],
        out_shape=[jax.ShapeDtypeStruct((bl, s, dff), BF16), jax.ShapeDtypeStruct((bl, s, dff), BF16),
                   jax.ShapeDtypeStruct((kw, dff), F32), jax.ShapeDtypeStruct((1, dff), F32)],
        compiler_params=_cp(2),
    )(up3, up3, dact3, cw, cb)


_HBM = pl.BlockSpec(memory_space=pltpu.HBM)


def _place():
    x, y, c = lax.axis_index("x"), lax.axis_index("y"), lax.axis_index("c")
    chips = dict(me=2 * x + y, nx=2 * (1 - x) + y, ny=2 * x + (1 - y), diag=2 * (1 - x) + (1 - y))
    peers = dict(nx=(1 - x, y, c), ny=(x, 1 - y, c), sib=(x, y, 1 - c))
    return c, chips, peers


def _remote(src, dst, sems, k, to):
    return pltpu.make_async_remote_copy(src_ref=src, dst_ref=dst, send_sem=sems[0].at[k], recv_sem=sems[1].at[k],
                                        device_id=to, device_id_type=MESH)


RS_STEPS = 2


def _piece(q, idx, n=1):
    start = idx * q
    if not isinstance(start, int):
        start = pl.multiple_of(start, SUBLANES)
    return pl.ds(start, n * q)


def _all_gather_chips(xs, name):
    nt = len(xs)
    per = 9

    def body(*refs):
        x_refs, o_refs = refs[:nt], refs[nt:2 * nt]
        send_sems, recv_sems = refs[2 * nt:]
        c, chip, peer = _place()
        sems = (send_sems, recv_sems)
        me, nx, ny, dg = chip["me"], chip["nx"], chip["ny"], chip["diag"]
        sends = []

        def arrive(k, dst):
            _remote(dst, dst, sems, k, peer["sib"]).wait_recv()

        def pass_on(k, blk, to):
            cp = _remote(blk, blk, sems, k, peer[to])
            cp.start()
            sends.append(cp)

        for t in range(nt):
            q = xs[t].shape[0] // 4
            half = _piece(q, 2 * c, 2)
            for k, to in ((0, "nx"), (1, "ny")):
                cp = _remote(x_refs[t].at[half], o_refs[t].at[me, half], sems, per * t + k, peer[to])
                cp.start()
                sends.append(cp)
            cp = _remote(x_refs[t], o_refs[t].at[me], sems, per * t + 8, peer["sib"])
            cp.start()
            sends.append(cp)
        for t in range(nt):
            q, o, k0 = xs[t].shape[0] // 4, o_refs[t], per * t
            half, sub0, sub1 = _piece(q, 2 * c, 2), _piece(q, 2 * c), _piece(q, 2 * c + 1)
            arrive(k0 + 0, o.at[nx, half])
            pass_on(k0 + 2, o.at[nx, sub0], "ny")
            pass_on(k0 + 4, o.at[nx, half], "sib")
            arrive(k0 + 1, o.at[ny, half])
            pass_on(k0 + 3, o.at[ny, sub1], "nx")
            pass_on(k0 + 5, o.at[ny, half], "sib")
            arrive(k0 + 2, o.at[dg, sub0])
            pass_on(k0 + 6, o.at[dg, sub0], "sib")
            arrive(k0 + 3, o.at[dg, sub1])
            pass_on(k0 + 7, o.at[dg, sub1], "sib")
        for t in range(nt):
            q, o, k0 = xs[t].shape[0] // 4, o_refs[t], per * t
            arrive(k0 + 4, o.at[nx, _piece(q, 2 * (1 - c), 2)])
            arrive(k0 + 5, o.at[ny, _piece(q, 2 * (1 - c), 2)])
            arrive(k0 + 6, o.at[dg, _piece(q, 2 * (1 - c))])
            arrive(k0 + 7, o.at[dg, _piece(q, 2 * (1 - c) + 1)])
            arrive(k0 + 8, o.at[me])
        for cp in sends:
            cp.wait_send()

    return pl.pallas_call(
        body, name=name, in_specs=[_HBM] * nt, out_specs=[_HBM] * nt,
        out_shape=[jax.ShapeDtypeStruct((N_CHIPS,) + x.shape, x.dtype) for x in xs],
        scratch_shapes=[pltpu.SemaphoreType.DMA((per * nt,)), pltpu.SemaphoreType.DMA((per * nt,))],
    )(*xs)


def _all_gather_devices(xs, name):
    nt = len(xs)
    per = 7

    def body(*refs):
        x_refs, o_refs = refs[:nt], refs[nt:2 * nt]
        send_sems, recv_sems, local_sems = refs[2 * nt:]
        x, y, c = lax.axis_index("x"), lax.axis_index("y"), lax.axis_index("c")
        sems = (send_sems, recv_sems)
        sib = (x, y, 1 - c)
        chips = [(1 - x, y), (x, 1 - y), (1 - x, 1 - y)]
        slot = lambda px, py, pc: 4 * px + 2 * py + pc
        me = slot(x, y, c)
        sends, copies = [], []

        def arrive(k, dst):
            _remote(dst, dst, sems, k, sib).wait_recv()

        for t in range(nt):
            cp = pltpu.make_async_copy(x_refs[t], o_refs[t].at[me], local_sems.at[t])
            cp.start()
            copies.append(cp)
            for k, to in enumerate([sib] + [(*chip, c) for chip in chips]):
                cp = _remote(x_refs[t], o_refs[t].at[me], sems, per * t + k, to)
                cp.start()
                sends.append(cp)
        for t in range(nt):
            for j, chip in enumerate(chips):
                blk = o_refs[t].at[slot(*chip, c)]
                arrive(per * t + 1 + j, blk)
                cp = _remote(blk, blk, sems, per * t + 4 + j, sib)
                cp.start()
                sends.append(cp)
        for t in range(nt):
            arrive(per * t, o_refs[t].at[slot(x, y, 1 - c)])
            for j, chip in enumerate(chips):
                arrive(per * t + 4 + j, o_refs[t].at[slot(*chip, 1 - c)])
        for cp in sends:
            cp.wait_send()
        for cp in copies:
            cp.wait()

    return pl.pallas_call(
        body, name=name, in_specs=[_HBM] * nt, out_specs=[_HBM] * nt,
        out_shape=[jax.ShapeDtypeStruct((2 * N_CHIPS,) + a.shape, a.dtype) for a in xs],
        scratch_shapes=[pltpu.SemaphoreType.DMA((per * nt,)), pltpu.SemaphoreType.DMA((per * nt,)),
                        pltpu.SemaphoreType.DMA((nt,))],
    )(*xs)


def _exchange(name, xs, out_shapes, plan):
    nt = len(xs)

    def body(*refs):
        x_refs, o_refs = refs[:nt], refs[nt:2 * nt]
        send_sems, recv_sems = refs[2 * nt:]
        c, chip, peer = _place()
        cps = []
        for t in range(nt):
            for src, dst, to in plan(c, chip, x_refs[t], o_refs[t], xs[t].shape):
                cps.append(_remote(src, dst, (send_sems, recv_sems), len(cps), peer[to]))
        for cp in cps:
            cp.start()
        for cp in cps:
            cp.wait()

    n_copies = nt * len(plan(0, dict(me=0, nx=2, ny=1, diag=3), None, None, xs[0].shape, count_only=True))
    return pl.pallas_call(
        body, name=name, in_specs=[_HBM] * nt, out_specs=[_HBM] * nt,
        out_shape=[jax.ShapeDtypeStruct(s, x.dtype) for s, x in zip(out_shapes, xs)],
        scratch_shapes=[pltpu.SemaphoreType.DMA((n_copies,)), pltpu.SemaphoreType.DMA((n_copies,))],
    )(*xs)


def _plan_sibling(c, chip, g, out, shape, count_only=False):
    if count_only:
        return [None] * N_CHIPS
    q = shape[1] // 4
    return [(g.at[j, _piece(q, 2 * (1 - c), 2)], out.at[j], "sib") for j in range(N_CHIPS)]


def _plan_first(c, chip, p, out, shape, count_only=False):
    if count_only:
        return [None] * 4
    q = shape[1] // 2
    return [(p.at[chip["nx"], _piece(q, 0)], out.at[0], "nx"), (p.at[chip["diag"], _piece(q, 0)], out.at[1], "nx"),
            (p.at[chip["ny"], _piece(q, 1)], out.at[2], "ny"), (p.at[chip["diag"], _piece(q, 1)], out.at[3], "ny")]


def _plan_second(c, chip, p, out, shape, count_only=False):
    if count_only:
        return [None] * 2
    return [(p.at[1], out.at[0], "ny"), (p.at[3], out.at[1], "nx")]


def _rs_last(ps):
    nt = len(ps)

    def body(*refs):
        p_refs, o_refs = refs[:nt], refs[nt:2 * nt]
        send_sems, recv_sems = refs[2 * nt:]
        c, _, peer = _place()
        sems = (send_sems, recv_sems)
        cps = []
        for t in range(nt):
            q = ps[t].shape[0] // 4
            mine = _piece(q, 2 * c, 2)
            cps.append(_remote(p_refs[t].at[mine], o_refs[t].at[mine], sems, t, peer["sib"]))
            cps[-1].start()
        for t in range(nt):
            q = ps[t].shape[0] // 4
            theirs = _piece(q, 2 * (1 - c), 2)
            cps[t].wait_send()
            _remote(p_refs[t].at[theirs], o_refs[t].at[theirs], sems, t, peer["sib"]).wait_recv()

    return pl.pallas_call(
        body, name="rs_last", in_specs=[_HBM] * nt, out_specs=[_HBM] * nt,
        out_shape=[jax.ShapeDtypeStruct(p.shape, F32) for p in ps],
        input_output_aliases={t: t for t in range(nt)},
        scratch_shapes=[pltpu.SemaphoreType.DMA((nt,)), pltpu.SemaphoreType.DMA((nt,))],
    )(*ps)


def _add_stage(name, grid, a_list, b_list, a_map, b_map, tbs, out_shapes, out_map, out_dtype, prefetch=None):
    nt = len(a_list)
    lead = lambda shape: (None,) * (len(shape) - 2)

    def kern(*refs):
        refs = refs[(1 if prefetch is not None else 0):]
        for t in range(nt):
            refs[2 * nt + t][...] = (refs[t][...].astype(F32) + refs[nt + t][...].astype(F32)).astype(out_dtype)

    in_specs = [pl.BlockSpec(lead(a.shape) + (tb, a.shape[-1]), a_map) for a, tb in zip(a_list, tbs)]
    in_specs += [pl.BlockSpec(lead(b.shape) + (tb, b.shape[-1]), b_map) for b, tb in zip(b_list, tbs)]
    out_specs = [pl.BlockSpec(lead(s) + (tb, s[-1]), out_map) for s, tb in zip(out_shapes, tbs)]
    out_shape = [jax.ShapeDtypeStruct(s, out_dtype) for s in out_shapes]
    if prefetch is None:
        return pl.pallas_call(kern, name=name, grid=grid, in_specs=in_specs, out_specs=out_specs, out_shape=out_shape,
                              compiler_params=_cp(len(grid)))(*a_list, *b_list)
    return pl.pallas_call(
        kern, name=name,
        grid_spec=pltpu.PrefetchScalarGridSpec(num_scalar_prefetch=1, grid=grid, in_specs=in_specs, out_specs=out_specs),
        out_shape=out_shape, compiler_params=_cp(len(grid)))(prefetch, *a_list, *b_list)


def _reduce_scatter_chips(gs):
    x, y, c = lax.axis_index("x"), lax.axis_index("y"), lax.axis_index("c")
    me, nx, ny = 2 * x + y, 2 * (1 - x) + y, 2 * x + (1 - y)
    st = RS_STEPS
    unit = 4 * st * 2 * SUBLANES
    rows = [g.shape[1] for g in gs]
    gs = [jnp.pad(g, ((0, 0), (0, -g.shape[1] % unit), (0, 0))) for g in gs]
    qs = [g.shape[1] // 4 for g in gs]
    tbs = [q // st for q in qs]
    cols = [g.shape[2] for g in gs]
    core = jnp.reshape(c, (1,)).astype(jnp.int32)

    got = _exchange("rs_sibling", gs, [(N_CHIPS, 2 * q, cc) for q, cc in zip(qs, cols)], _plan_sibling)
    p0 = _add_stage("rs_add_sibling", (N_CHIPS, 2, st), gs, got,
                    lambda j, h, s, c_ref: (j, (2 * c_ref[0] + h) * st + s, 0), lambda j, h, s, c_ref: (j, h * st + s, 0),
                    tbs, [(N_CHIPS, 2 * q, cc) for q, cc in zip(qs, cols)], lambda j, h, s, c_ref: (j, h * st + s, 0),
                    BF16, prefetch=core)
    got = _exchange("rs_first", p0, [(4, q, cc) for q, cc in zip(qs, cols)], _plan_first)
    p1 = _add_stage("rs_add_first", (4, st), p0, got,
                    lambda k, s, i_ref: (i_ref[k], (k // 2) * st + s, 0), lambda k, s, i_ref: (k, s, 0),
                    tbs, [(4, q, cc) for q, cc in zip(qs, cols)], lambda k, s, i_ref: (k, s, 0),
                    BF16, prefetch=jnp.stack([me, ny, me, nx]).astype(jnp.int32))
    got = _exchange("rs_second", p1, [(2, q, cc) for q, cc in zip(qs, cols)], _plan_second)
    p2 = _add_stage("rs_add_second", (2, st), p1, got, lambda h, s, c_ref: (2 * h, s, 0), lambda h, s, c_ref: (h, s, 0),
                    tbs, [(4 * q, cc) for q, cc in zip(qs, cols)], lambda h, s, c_ref: ((2 * c_ref[0] + h) * st + s, 0),
                    F32, prefetch=core)
    return [out[:r] for out, r in zip(_rs_last(p2), rows)]


def _adamw(g, w, m, v, name):
    rows, cc = g.shape
    tr = _tile(rows, max(SUBLANES, (1 << 18) // cc), SUBLANES)
    k1 = 1.0 - ADAM_B1 ** ADAM_STEP
    k2 = 1.0 - ADAM_B2 ** ADAM_STEP

    def kern(g_ref, w_ref, m_ref, v_ref, d_ref, nm_ref, nv_ref):
        gv = g_ref[...]
        nm = ADAM_B1 * m_ref[...] + (1.0 - ADAM_B1) * gv
        nv = ADAM_B2 * v_ref[...] + (1.0 - ADAM_B2) * (gv * gv)
        nm_ref[...] = nm
        nv_ref[...] = nv
        d_ref[...] = -ADAM_LR * ((nm / k1) / (jnp.sqrt(nv / k2) + ADAM_EPS) + ADAM_WD * w_ref[...])

    spec = pl.BlockSpec((tr, cc), lambda t: (t, 0))
    return pl.pallas_call(
        kern, name=name, grid=(rows // tr,), in_specs=[spec] * 4, out_specs=[spec] * 3,
        out_shape=[jax.ShapeDtypeStruct((rows, cc), F32)] * 3, compiler_params=_cp(1),
    )(g, w, m, v)


def _flat_pad(parts, total):
    flat = jnp.concatenate([p.reshape(-1) for p in parts])
    return jnp.pad(flat, (0, total - flat.shape[0]))


def _split_flat(flat, shapes):
    out, pos = [], 0
    for shp in shapes:
        size = math.prod(shp)
        out.append(flat[pos:pos + size].reshape(shp))
        pos += size
    return out


def _cols_of_chunks(chunks, lo, hi):
    width = chunks.shape[2]
    parts = []
    for j in range(chunks.shape[0]):
        a, b = max(lo, j * width), min(hi, (j + 1) * width)
        if a < b:
            parts.append(chunks[j, :, a - j * width:b - j * width])
    return parts[0] if len(parts) == 1 else jnp.concatenate(parts, axis=1)


def _chunks_of_cols(segments, n_chunks):
    total = sum(s.shape[1] for s in segments)
    width = total // n_chunks
    chunks = []
    for j in range(n_chunks):
        lo, hi, pos, parts = j * width, (j + 1) * width, 0, []
        for s in segments:
            a, b = max(lo, pos), min(hi, pos + s.shape[1])
            if a < b:
                parts.append(s[:, a - pos:b - pos])
            pos += s.shape[1]
        chunks.append(parts[0] if len(parts) == 1 else jnp.concatenate(parts, axis=1))
    return jnp.stack(chunks)


_WEIGHTS = ['w_ada', 'b_ada', 'g_norm1', 'w_in', 'w_rnn_conv', 'b_rnn_conv', 'w_lru_a', 'b_lru_a', 'w_lru_i', 'b_lru_i',
            'lru_lambda', 'b_fgate', 'w_proj_rnn', 'w_proj_attn', 'w_out', 'g_norm2', 'w_ffn_up', 'w_ffn_conv',
            'b_ffn_conv', 'w_ffn_down', 'w_ada_final', 'b_ada_final', 'g_final']
_MATMUL = ['w_in', 'w_proj_rnn', 'w_proj_attn', 'w_out', 'w_ffn_up', 'w_ffn_down']
_ADA = ['w_ada', 'w_ada_final']
_ADA_BIAS = ['b_ada', 'b_ada_final']
_CONV = ['w_rnn_conv', 'w_ffn_conv']
_REPLICATED = [n for n in _WEIGHTS if n not in _MATMUL + _ADA + _ADA_BIAS + _CONV]


def kernel(x, c, w_ada, b_ada, g_norm1, w_in, w_rnn_conv, b_rnn_conv, w_lru_a, b_lru_a, w_lru_i, b_lru_i, lru_lambda, b_fgate, w_proj_rnn, w_proj_attn, w_out, g_norm2, w_ffn_up, w_ffn_conv, b_ffn_conv, w_ffn_down, w_ada_final, b_ada_final, g_final, loss_target, m_w_ada, m_b_ada, m_g_norm1, m_w_in, m_w_rnn_conv, m_b_rnn_conv, m_w_lru_a, m_b_lru_a, m_w_lru_i, m_b_lru_i, m_lru_lambda, m_b_fgate, m_w_proj_rnn, m_w_proj_attn, m_w_out, m_g_norm2, m_w_ffn_up, m_w_ffn_conv, m_b_ffn_conv, m_w_ffn_down, m_w_ada_final, m_b_ada_final, m_g_final, v_w_ada, v_b_ada, v_g_norm1, v_w_in, v_w_rnn_conv, v_b_rnn_conv, v_w_lru_a, v_b_lru_a, v_w_lru_i, v_b_lru_i, v_lru_lambda, v_b_fgate, v_w_proj_rnn, v_w_proj_attn, v_w_out, v_g_norm2, v_w_ffn_up, v_w_ffn_conv, v_b_ffn_conv, v_w_ffn_down, v_w_ada_final, v_b_ada_final, v_g_final):
    args = locals()
    shape_of = {n: args[n].shape for n in _WEIGHTS}

    def view(a):
        if a.ndim >= 3:
            return a[0]
        return a[None, :] if a.ndim == 1 else a

    w2 = {n: view(args[n]) for n in _WEIGHTS}
    m2 = {n: args['m_' + n].reshape(w2[n].shape) for n in _WEIGHTS}
    v2 = {n: args['v_' + n].reshape(w2[n].shape) for n in _WEIGHTS}

    bl, s, d = x.shape
    t = bl * s
    nh = b_fgate.shape[-1]
    nb, rb = w_lru_a.shape[1], w_lru_a.shape[2]
    dr = nb * rb
    da = w2['w_proj_attn'].shape[0] * N_CHIPS
    dh = da // nh
    dff = w2['w_ffn_conv'].shape[1] * N_CHIPS
    scale = dh ** -0.5
    chip = 2 * lax.axis_index("x") + lax.axis_index("y")
    dev = 2 * chip + lax.axis_index("c")

    names = list(_MATMUL)
    n_conv = sum(w2[n].size for n in _CONV)
    rows_conv = -(-n_conv // (FLAT_COLS * 32)) * 32
    conv_local = _flat_pad([w2[n] for n in _CONV], rows_conv * FLAT_COLS).reshape(rows_conv, FLAT_COLS)
    *weights_all, conv_all = _all_gather_chips([w2[n].astype(BF16) for n in names] + [conv_local], "ag_weights")
    gathered = dict(zip(names, weights_all))
    conv_all = conv_all.reshape(N_CHIPS, -1)
    conv_full, pos = {}, 0
    for n in _CONV:
        r, n4 = w2[n].shape
        blocks = conv_all[:, pos:pos + r * n4].reshape(N_CHIPS, r, n4)
        conv_full[n] = jnp.concatenate([blocks[j] for j in range(N_CHIPS)], axis=1)
        pos += r * n4
    rowmajor = lambda n: gathered[n].reshape(-1, gathered[n].shape[2])
    w_proj_rnn_f, w_proj_attn_f, w_out_f, w_ffn_down_f = (rowmajor(n) for n in ('w_proj_rnn', 'w_proj_attn', 'w_out', 'w_ffn_down'))
    up_chunk = w2['w_ffn_up'].shape[1]

    o_q, o_k, o_fl = 2 * dr, 2 * dr + da, 2 * dr + 3 * da
    o_mg = o_fl + nh
    g_in_w = gathered['w_in']
    w_rnn, w_q = _cols_of_chunks(g_in_w, 0, o_q), _cols_of_chunks(g_in_w, o_q, o_k)
    w_kv, w_mg = _cols_of_chunks(g_in_w, o_k, o_fl), _cols_of_chunks(g_in_w, o_mg, o_mg + 2 * d)
    w_fl = jnp.pad(_cols_of_chunks(g_in_w, o_fl, o_mg), ((0, 0), (0, LANES - nh)))
    w_rest = jnp.concatenate([w_q, w_kv, w_mg], axis=1)
    bf_pad = jnp.pad(w2['b_fgate'], ((0, 0), (0, LANES - nh)))

    nd = 2 * N_CHIPS
    c_act = _silu_pad(_all_gather_devices([c], "ag_cond")[0].reshape(nd * bl, d), nd * bl)
    my_cols = lambda a, n: lax.dynamic_slice_in_dim(a, chip * w2[n].shape[1], w2[n].shape[1], axis=1)
    mod_cols = [_mm(c_act, w2[n].astype(BF16), "nn", name=n + "_fwd", bias=my_cols(w2[b], n)) for n, b in zip(_ADA, _ADA_BIAS)]
    my_rows = lambda g: lax.dynamic_slice_in_dim(g, dev * bl, bl, axis=1).transpose(1, 0, 2).reshape(bl, -1)
    mod, modf = (my_rows(g) for g in _all_gather_chips(mod_cols, "ag_mod"))
    sh1, sc1, gt1, sh2, sc2, gt2 = [mod[:, i * d:(i + 1) * d].reshape(bl, 1, d) for i in range(6)]
    shf, scf = modf[:, :d].reshape(bl, 1, d), modf[:, d:].reshape(bl, 1, d)

    h1 = _norm_mod_fwd(x, w2['g_norm1'], sh1, sc1)
    h1f = h1.reshape(t, d)
    zr = _mm(h1f, w_rnn, "nn", name="in_rnn", tn=dr).reshape(bl, s, 2 * dr)
    q3 = _mm(h1f, w_q, "nn", name="in_q", out_dtype=BF16, scale=scale).reshape(bl, s, da)
    kv3 = _mm(h1f, w_kv, "nn", name="in_kv", out_dtype=BF16, tn=2 * da).reshape(bl, s, 2 * da)
    mg3 = _mm(h1f, w_mg, "nn", name="in_mg", tn=2 * d).reshape(bl, s, 2 * d)
    zf3 = _mm(h1f, w_fl, "nn", name="in_fl").reshape(bl, s, LANES)

    lru = (conv_full['w_rnn_conv'], w2['b_rnn_conv'], w2['w_lru_a'], w2['b_lru_a'], w2['w_lru_i'], w2['b_lru_i'], w2['lru_lambda'])
    y_rnn, *rnn_kept = _rnn_fwd(zr, *lru)

    f3 = _fgate_fwd(zf3, bf_pad)
    f_heads = f3[:, :, :nh].transpose(0, 2, 1).reshape(bl * nh, s)
    fcol = jnp.broadcast_to(f_heads[:, :, None], (bl * nh, s, LANES))
    frow = f_heads.reshape(bl * nh, 1, s)
    o3, lse_row = _attn_fwd(q3, kv3, fcol, frow, nh)

    pr3 = _mm(y_rnn.reshape(t, dr), w_proj_rnn_f, "nn", name="proj_rnn").reshape(bl, s, d)
    pa3 = _mm(o3.reshape(t, da), w_proj_attn_f, "nn", name="proj_attn").reshape(bl, s, d)
    merged = _merge_fwd(mg3, pr3, pa3)
    mo3 = _mm(merged.reshape(t, d), w_out_f, "nn", name="mix_out").reshape(bl, s, d)
    x1, h2 = _resid_norm_fwd(x, mo3, gt1, w2['g_norm2'], sh2, sc2)
    h2f = h2.reshape(t, d)
    up3 = _mm(h2f, gathered['w_ffn_up'], "nn", name="ffn_up", b_chunk=up_chunk, tn=up_chunk).reshape(bl, s, 2 * dff)
    act3 = _ffn_act_fwd(up3, conv_full['w_ffn_conv'], w2['b_ffn_conv'])
    yf3 = _mm(act3.reshape(t, dff), w_ffn_down_f, "nn", name="ffn_down").reshape(bl, s, d)

    dx2, dyf, dgt2, dshf, dscf, dg_final, loss_part = _final_fwd_bwd(x1, yf3, gt2, w2['g_final'], shf, scf, loss_target)
    loss = lax.psum(loss_part[0, 0], ("x", "y", "c"))

    dyf_f = dyf.reshape(t, d)
    g_ffn_down = _mm(act3.reshape(t, dff), dyf_f, "tn", name="dw_ffn_down")
    dact3 = _mm(dyf_f, w_ffn_down_f, "nt", name="d_ffn_act", tn=dff // 2).reshape(bl, s, dff)
    dgf, duf, g_ffn_conv, g_b_ffn_conv = _ffn_act_bwd(up3, dact3, conv_full['w_ffn_conv'], w2['b_ffn_conv'])
    dgf_f, duf_f = dgf.reshape(t, dff), duf.reshape(t, dff)
    g_ffn_up = jnp.concatenate([_mm(h2f, dgf_f, "tn", name="dw_ffn_up_gate", out_chunk=up_chunk, tn=up_chunk),
                                _mm(h2f, duf_f, "tn", name="dw_ffn_up_value", out_chunk=up_chunk, tn=up_chunk)], axis=0)
    dh2 = _mm([dgf_f, duf_f], gathered['w_ffn_up'], "nt", name="d_h2", b_chunk=up_chunk).reshape(bl, s, d)
    dmo, dgt1, dx1, dsh2, dsc2, dg_norm2 = _norm_mod_bwd(dh2, x1, dx2, w2['g_norm2'], sc2, "norm2_bwd", gated=(mo3, gt1))

    dmo_f = dmo.reshape(t, d)
    g_out = _mm(merged.reshape(t, d), dmo_f, "tn", name="dw_out")
    dm3 = _mm(dmo_f, w_out_f, "nt", name="d_merged").reshape(bl, s, d)
    dpr, dpa, dmr, dma = _merge_bwd(dm3, mg3, pr3, pa3)
    g_proj_rnn = _mm(y_rnn.reshape(t, dr), dpr.reshape(t, d), "tn", name="dw_proj_rnn", tm=dr)
    g_proj_attn = _mm(o3.reshape(t, da), dpa.reshape(t, d), "tn", name="dw_proj_attn")
    dyr3 = _mm(dpr.reshape(t, d), w_proj_rnn_f, "nt", name="d_y_rnn", tn=dr).reshape(bl, s, dr)
    do3 = _mm(dpa.reshape(t, d), w_proj_attn_f, "nt", name="d_y_attn").reshape(bl, s, da)

    dq3, dk3, dv3, dfk, dfq = _attn_bwd(q3, kv3, do3, o3, lse_row, fcol, frow, nh, scale)
    heads_last = lambda a: jnp.pad(a.reshape(bl, nh, s).transpose(0, 2, 1), ((0, 0), (0, 0), (0, LANES - nh)))
    dzf3, g_bf = _fgate_bwd(heads_last(dfk[:, :, 0]), heads_last(dfq), zf3, bf_pad)

    dxr, dgr, g_rnn_conv, g_b_rnn_conv, g_lru_a, g_b_lru_a, g_lru_i, g_b_lru_i, g_lam = _rnn_bwd(
        zr, rnn_kept, dyr3, lru[0], lru[2], lru[4], lru[6])

    dz = [a.reshape(t, -1) for a in (dxr, dgr, dq3, dk3, dv3, dmr, dma)]
    dzf_f = dzf3.reshape(t, LANES)
    seg_names = ("xr", "gr", "q", "k", "v", "mr", "ma")
    g_seg = [_mm(h1f, a, "tn", name="dw_in_" + n, tn=dr) for n, a in zip(seg_names, dz)]
    g_in_fl = _mm(h1f, dzf_f, "tn", name="dw_in_fl")[:, :nh]
    g_in = _chunks_of_cols(g_seg[:5] + [g_in_fl] + g_seg[5:], N_CHIPS)
    dh1 = _mm(dzf_f, w_fl, "nt", name="d_h1_fl")
    dh1 = _mm(dz[:2], w_rnn, "nt", name="d_h1_rnn", add=dh1)
    dh1 = _mm(dz[2:], w_rest, "nt", name="d_h1", add=dh1).reshape(bl, s, d)
    grad_x, dsh1, dsc1, dg_norm1 = _norm_mod_bwd(dh1, x, dx1, w2['g_norm1'], sc1, "norm1_bwd")

    dmods = [jnp.concatenate([dsh1, dsc1, dgt1, dsh2, dsc2, dgt2], axis=-1).reshape(bl, -1),
             jnp.concatenate([dshf, dscf], axis=-1).reshape(bl, -1)]
    dmods = [g.reshape(nd * bl, -1) for g in _all_gather_devices(dmods, "ag_dmod")]
    grad = {n: _mm(c_act, my_cols(g, n), "tn", name="dw_" + n) for n, g in zip(_ADA, dmods)}
    grad.update({b: _rowsum(g, "d" + b) for b, g in zip(_ADA_BIAS, dmods)})

    rowchunks = lambda g: g.reshape(N_CHIPS, g.shape[0] // N_CHIPS, g.shape[1])
    full = dict(w_in=g_in, w_proj_rnn=rowchunks(g_proj_rnn), w_proj_attn=rowchunks(g_proj_attn),
                w_out=rowchunks(g_out), w_ffn_up=g_ffn_up, w_ffn_down=rowchunks(g_ffn_down))
    small = dict(g_norm1=dg_norm1, w_rnn_conv=g_rnn_conv, b_rnn_conv=g_b_rnn_conv, w_lru_a=g_lru_a,
                 b_lru_a=g_b_lru_a, w_lru_i=g_lru_i, b_lru_i=g_b_lru_i, lru_lambda=g_lam, b_fgate=g_bf[:, :nh],
                 g_norm2=dg_norm2, w_ffn_conv=g_ffn_conv, b_ffn_conv=g_b_ffn_conv, g_final=dg_final)

    small_names = _REPLICATED + _CONV
    n_small = sum(small[n].size for n in small_names)
    rows_q = -(-n_small // (N_CHIPS * FLAT_COLS * 32 * RS_STEPS)) * 32 * RS_STEPS
    small_flat = _flat_pad([small[n] for n in small_names], N_CHIPS * rows_q * FLAT_COLS).reshape(N_CHIPS, rows_q, FLAT_COLS)
    reduced = _reduce_scatter_chips([full[n] for n in names] + [small_flat])
    grad.update(zip(names, reduced[:-1]))
    small_all = _all_gather_chips([reduced[-1]], "ag_small_grads")[0].reshape(-1)
    grad.update(zip(small_names, _split_flat(small_all, [small[n].shape for n in small_names])))
    for n in _CONV:
        n4 = w2[n].shape[1]
        grad[n] = lax.dynamic_slice_in_dim(grad[n], chip * n4, n4, axis=1)

    delta_w, new_m, new_v = {}, {}, {}
    for n in names + _ADA:
        delta_w[n], new_m[n], new_v[n] = _adamw(grad[n], w2[n], m2[n], v2[n], "adamw_" + n)
    small_names = small_names + _ADA_BIAS
    rows_small = -(-sum(w2[n].size for n in small_names) // (FLAT_COLS * SUBLANES)) * SUBLANES
    flat_small = lambda src: _flat_pad([src[n] for n in small_names], rows_small * FLAT_COLS).reshape(rows_small, FLAT_COLS)
    small_out = _adamw(flat_small(grad), flat_small(w2), flat_small(m2), flat_small(v2), "adamw_small")
    for dst, flat in zip((delta_w, new_m, new_v), small_out):
        dst.update(zip(small_names, _split_flat(flat.reshape(-1), [w2[n].shape for n in small_names])))

    out = [loss, grad_x]
    for src in (grad, delta_w, new_m, new_v):
        out += [src[n].reshape(shape_of[n]) for n in _WEIGHTS]
    return tuple(out)
```

```python
import functools
import math

import jax
import jax.numpy as jnp
from jax import lax
from jax.experimental import pallas as pl
from jax.experimental.pallas import tpu as pltpu

F32 = jnp.float32
BF16 = jnp.bfloat16
MESH = pl.DeviceIdType.MESH

RMS_EPS = 1e-6
LRU_C = 8.0
ADAM_LR = 0.001
ADAM_B1 = 0.9
ADAM_B2 = 0.999
ADAM_EPS = 1e-08
ADAM_WD = 0.01
ADAM_STEP = 10

LANES = 128
SUBLANES = 8
N_CHIPS = 4
FLAT_COLS = 1024
SCAN_SEGMENTS = 2 * SUBLANES
VMEM_LIMIT = 48 * 1024 * 1024
NEG_BIG = -1e30


def _cp(n_axes):
    return pltpu.CompilerParams(dimension_semantics=("arbitrary",) * n_axes, vmem_limit_bytes=VMEM_LIMIT)


def _tile(n, target, align):
    if n <= target:
        return n
    t = (target // align) * align
    while t >= align:
        if n % t == 0:
            return t
        t -= align
    return n


def _nice_rows(n, align):
    r = -(-n // align) * align
    while True:
        if r <= 640:
            return r, r
        t = _tile(r, 640, align)
        if 128 <= t <= 640:
            return r, t
        r += align


def _sigmoid(x):
    return 0.5 * jnp.tanh(0.5 * x) + 0.5


def _softplus(x):
    return jnp.maximum(x, 0.0) + jnp.log1p(jnp.exp(-jnp.abs(x)))


def _expm1(x, exp_x):
    small = x * (1.0 + 0.5 * x * (1.0 + (1.0 / 3.0) * x * (1.0 + 0.25 * x)))
    return jnp.where(jnp.abs(x) < 0.05, small, exp_x - 1.0)


_GELU_K = math.sqrt(2.0 / math.pi)
_GELU_C = 0.044715


def _gelu(x):
    t = jnp.tanh(_GELU_K * (x + _GELU_C * x * x * x))
    return 0.5 * x * (1.0 + t)


def _gelu_and_grad(x):
    t = jnp.tanh(_GELU_K * (x + _GELU_C * x * x * x))
    g = 0.5 * x * (1.0 + t)
    dg = 0.5 * (1.0 + t) + 0.5 * x * (1.0 - t * t) * _GELU_K * (1.0 + 3.0 * _GELU_C * x * x)
    return g, dg


def _shift_down(x, k):
    if k == 0:
        return x
    y = pltpu.roll(x, k, 0)
    rows = lax.broadcasted_iota(jnp.int32, (SUBLANES, x.shape[1]), 0)
    return jnp.concatenate([jnp.where(rows >= k, y[:SUBLANES], 0.0), y[SUBLANES:]], axis=0)


def _shift_up(x, k):
    if k == 0:
        return x
    s = x.shape[0]
    y = pltpu.roll(x, s - k, 0)
    rows = lax.broadcasted_iota(jnp.int32, (SUBLANES, x.shape[1]), 0)
    return jnp.concatenate([y[:s - SUBLANES], jnp.where(rows < SUBLANES - k, y[s - SUBLANES:], 0.0)], axis=0)


def _dot(a, b, dims):
    return lax.dot_general(a.astype(BF16), b.astype(BF16), (dims, ((), ())), preferred_element_type=F32)


_NN = ((1,), (0,))
_NT = ((1,), (1,))
_TN = ((0,), (0,))


def _mm(a, b, mode, *, name, out_dtype=F32, scale=None, bias=None, add=None, tm=1024, tn=1024, tk=1024,
        b_chunk=None, out_chunk=None):
    pieces = list(a) if isinstance(a, (list, tuple)) else [a]
    ksize = lambda p: p.shape[0] if mode == "tn" else p.shape[1]
    if b_chunk is None:
        brows, bcols = b.shape
    else:
        brows, bcols = b.shape[1], b.shape[0] * b_chunk
    k = sum(ksize(p) for p in pieces)
    if mode == "nt":
        m, n = pieces[0].shape[0], brows
        assert bcols == k, (bcols, k)
    else:
        m, n = (pieces[0].shape[1] if mode == "tn" else pieces[0].shape[0]), bcols
        assert brows == k, (brows, k)
    tm = _tile(m, tm, LANES)
    ncut = n
    if b_chunk is not None and mode != "nt":
        ncut = b_chunk
    if out_chunk is not None:
        ncut = math.gcd(ncut, out_chunk)
    tn = _tile(ncut, tn, LANES)
    kcut = b_chunk if (b_chunk is not None and mode == "nt") else k
    for p in pieces:
        kcut = math.gcd(kcut, ksize(p))
    tk = _tile(kcut, tk, LANES)
    nk = k // tk
    dims = {"nn": _NN, "nt": _NT, "tn": _TN}[mode]
    counts = [ksize(p) // tk for p in pieces]
    starts = [sum(counts[:i]) for i in range(len(pieces))]
    n_pieces = len(pieces)

    def a_spec(s0, cnt):
        kmap = (lambda kk: kk) if n_pieces == 1 else (lambda kk: jnp.clip(kk - s0, 0, cnt - 1))
        if mode == "tn":
            return pl.BlockSpec((tk, tm), lambda i, j, kk: (kmap(kk), i))
        return pl.BlockSpec((tm, tk), lambda i, j, kk: (i, kmap(kk)))

    if b_chunk is None:
        if mode == "nt":
            b_spec = pl.BlockSpec((tn, tk), lambda i, j, kk: (j, kk))
        else:
            b_spec = pl.BlockSpec((tk, tn), lambda i, j, kk: (kk, j))
    elif mode == "nt":
        per_b = b_chunk // tk
        b_spec = pl.BlockSpec((None, tn, tk), lambda i, j, kk: (kk // per_b, j, kk % per_b))
    else:
        per_b = b_chunk // tn
        b_spec = pl.BlockSpec((None, tk, tn), lambda i, j, kk: (j // per_b, kk, j % per_b))
    if out_chunk is None:
        out_spec = pl.BlockSpec((tm, tn), lambda i, j, kk: (i, j))
        out_shape = jax.ShapeDtypeStruct((m, n), out_dtype)
    else:
        per_o = out_chunk // tn
        out_spec = pl.BlockSpec((None, tm, tn), lambda i, j, kk: (j // per_o, i, j % per_o))
        out_shape = jax.ShapeDtypeStruct((n // out_chunk, m, out_chunk), out_dtype)
    in_specs = [a_spec(s0, cnt) for s0, cnt in zip(starts, counts)] + [b_spec]
    args = pieces + [b]
    if bias is not None:
        in_specs.append(pl.BlockSpec((1, tn), lambda i, j, kk: (0, j)))
        args.append(bias)
    if add is not None:
        in_specs.append(pl.BlockSpec((tm, tn), lambda i, j, kk: (i, j)))
        args.append(add)

    def kern(*refs):
        b_ref = refs[n_pieces]
        o_ref = refs[n_pieces + 1 + (bias is not None) + (add is not None)]

        def finish(r):
            if scale is not None:
                r = r * scale
            pos = n_pieces + 1
            if bias is not None:
                r = r + refs[pos][...]
                pos += 1
            if add is not None:
                r = r + refs[pos][...]
            o_ref[...] = r.astype(out_dtype)

        if nk == 1:
            finish(_dot(refs[0][...], b_ref[...], dims))
            return
        acc = refs[-1]
        kk = pl.program_id(2)

        @pl.when(kk == 0)
        def _():
            acc[...] = jnp.zeros_like(acc)

        if n_pieces == 1:
            acc[...] += _dot(refs[0][...], b_ref[...], dims)
        else:
            for idx in range(n_pieces):
                @pl.when((kk >= starts[idx]) & (kk < starts[idx] + counts[idx]))
                def _(idx=idx):
                    acc[...] += _dot(refs[idx][...], b_ref[...], dims)

        @pl.when(kk == nk - 1)
        def _():
            finish(acc[...])

    return pl.pallas_call(
        kern, name=name,
        grid=(m // tm, n // tn, nk),
        in_specs=in_specs, out_specs=out_spec, out_shape=out_shape,
        scratch_shapes=[pltpu.VMEM((tm, tn), F32)] if nk > 1 else [],
        compiler_params=_cp(3),
    )(*args)


def _silu_pad(c, rows):
    bl, d = c.shape

    def kern(c_ref, o_ref):
        o_ref[...] = jnp.zeros_like(o_ref)
        v = c_ref[...]
        o_ref[0:bl, :] = v * _sigmoid(v)

    return pl.pallas_call(kern, name="silu_pad", out_shape=jax.ShapeDtypeStruct((rows, d), F32))(c)


def _rowsum(x, name):
    r, n = x.shape

    def kern(x_ref, o_ref):
        o_ref[...] = jnp.sum(x_ref[...], axis=0, keepdims=True)

    return pl.pallas_call(kern, name=name, out_shape=jax.ShapeDtypeStruct((1, n), F32))(x)


def _norm_parts(x, g):
    r = lax.rsqrt(jnp.mean(x * x, axis=-1, keepdims=True) + RMS_EPS)
    xh = x * r
    return r, xh, xh * g


def _norm_bwd_parts(dh, xh, r, g, sc):
    n = xh * g
    dn = dh * (1.0 + sc)
    dxh = dn * g
    dx = r * (dxh - xh * jnp.mean(dxh * xh, axis=-1, keepdims=True))
    return dx, dh, dh * n, dn * xh


def _act_specs(ts, d, n):
    return [pl.BlockSpec((1, ts, d), lambda b, t: (b, t, 0)) for _ in range(n)]


def _vec_spec(d):
    return pl.BlockSpec((1, 1, d), lambda b, t: (b, 0, 0))


def _par_spec(d):
    return pl.BlockSpec((1, d), lambda b, t: (0, 0))


def _norm_mod_fwd(x3, g, sh, sc):
    bl, s, d = x3.shape
    ts = _tile(s, 512, SUBLANES)

    def kern(x_ref, g_ref, sh_ref, sc_ref, h_ref):
        _, _, n = _norm_parts(x_ref[0], g_ref[...])
        h_ref[0] = (n * (1.0 + sc_ref[0]) + sh_ref[0]).astype(BF16)

    return pl.pallas_call(
        kern, name="norm_mod_fwd", grid=(bl, s // ts),
        in_specs=_act_specs(ts, d, 1) + [_par_spec(d), _vec_spec(d), _vec_spec(d)],
        out_specs=_act_specs(ts, d, 1)[0],
        out_shape=jax.ShapeDtypeStruct((bl, s, d), BF16),
        compiler_params=_cp(2),
    )(x3, g, sh, sc)


def _resid_norm_fwd(x3, y3, gate, g, sh, sc):
    bl, s, d = x3.shape
    ts = _tile(s, 512, SUBLANES)

    def kern(x_ref, y_ref, gate_ref, g_ref, sh_ref, sc_ref, x1_ref, h_ref):
        x1 = x_ref[0] + gate_ref[0] * y_ref[0]
        x1_ref[0] = x1
        _, _, n = _norm_parts(x1, g_ref[...])
        h_ref[0] = (n * (1.0 + sc_ref[0]) + sh_ref[0]).astype(BF16)

    return pl.pallas_call(
        kern, name="resid_norm_fwd", grid=(bl, s // ts),
        in_specs=_act_specs(ts, d, 2) + [_vec_spec(d), _par_spec(d), _vec_spec(d), _vec_spec(d)],
        out_specs=_act_specs(ts, d, 2),
        out_shape=[jax.ShapeDtypeStruct((bl, s, d), F32), jax.ShapeDtypeStruct((bl, s, d), BF16)],
        compiler_params=_cp(2),
    )(x3, y3, gate, g, sh, sc)


def _norm_mod_bwd(dh3, x3, dres3, g, sc, name, gated=None):
    bl, s, d = x3.shape
    ts = _tile(s, 512, SUBLANES)

    def kern(dh_ref, x_ref, dres_ref, g_ref, sc_ref, *rest):
        dx_ref, dsh_ref, dsc_ref, dg_ref = rest[-4:]
        b, t = pl.program_id(0), pl.program_id(1)
        gv = g_ref[...]
        r, xh, _ = _norm_parts(x_ref[0], gv)
        dx, a, bb, cc = _norm_bwd_parts(dh_ref[0], xh, r, gv, sc_ref[0])
        dx = dres_ref[0] + dx
        dx_ref[0] = dx

        @pl.when(t == 0)
        def _():
            dsh_ref[...] = jnp.zeros_like(dsh_ref)
            dsc_ref[...] = jnp.zeros_like(dsc_ref)

        @pl.when((t == 0) & (b == 0))
        def _():
            dg_ref[...] = jnp.zeros_like(dg_ref)

        dsh_ref[0] += jnp.sum(a, axis=0, keepdims=True)
        dsc_ref[0] += jnp.sum(bb, axis=0, keepdims=True)
        dg_ref[...] += jnp.sum(cc, axis=0, keepdims=True)
        if gated is not None:
            y_ref, gate_ref, dy_ref, dgate_ref = rest[:4]
            dy_ref[0] = (gate_ref[0] * dx).astype(BF16)

            @pl.when(t == 0)
            def _():
                dgate_ref[...] = jnp.zeros_like(dgate_ref)

            dgate_ref[0] += jnp.sum(dx * y_ref[0], axis=0, keepdims=True)

    extra_in = [] if gated is None else _act_specs(ts, d, 1) + [_vec_spec(d)]
    extra_out = [] if gated is None else [_act_specs(ts, d, 1)[0], _vec_spec(d)]
    extra_shape = [] if gated is None else [jax.ShapeDtypeStruct((bl, s, d), BF16), jax.ShapeDtypeStruct((bl, 1, d), F32)]
    return pl.pallas_call(
        kern, name=name, grid=(bl, s // ts),
        in_specs=_act_specs(ts, d, 3) + [_par_spec(d), _vec_spec(d)] + extra_in,
        out_specs=extra_out + [_act_specs(ts, d, 1)[0], _vec_spec(d), _vec_spec(d), _par_spec(d)],
        out_shape=extra_shape + [jax.ShapeDtypeStruct((bl, s, d), F32), jax.ShapeDtypeStruct((bl, 1, d), F32),
                                 jax.ShapeDtypeStruct((bl, 1, d), F32), jax.ShapeDtypeStruct((1, d), F32)],
        compiler_params=_cp(2),
    )(dh3, x3, dres3, g, sc, *(gated or ()))


def _final_fwd_bwd(x1, yf, gate2, g, shf, scf, tgt):
    bl, s, d = x1.shape
    ts = _tile(s, 512, SUBLANES)

    def kern(x1_ref, yf_ref, gate_ref, g_ref, sh_ref, sc_ref, tgt_ref,
             dx_ref, dyf_ref, dgate_ref, dsh_ref, dsc_ref, dg_ref, loss_ref):
        b, t = pl.program_id(0), pl.program_id(1)
        gv, sc, yf, gate = g_ref[...], sc_ref[0], yf_ref[0], gate_ref[0]
        x2 = x1_ref[0] + gate * yf
        r, xh, n = _norm_parts(x2, gv)
        err = n * (1.0 + sc) + sh_ref[0] - tgt_ref[0]
        dx, a, bb, cc = _norm_bwd_parts(err * (1.0 / d), xh, r, gv, sc)
        dx_ref[0] = dx
        dyf_ref[0] = (gate * dx).astype(BF16)

        @pl.when(t == 0)
        def _():
            dsh_ref[...] = jnp.zeros_like(dsh_ref)
            dsc_ref[...] = jnp.zeros_like(dsc_ref)
            dgate_ref[...] = jnp.zeros_like(dgate_ref)

        @pl.when((t == 0) & (b == 0))
        def _():
            dg_ref[...] = jnp.zeros_like(dg_ref)
            loss_ref[...] = jnp.zeros_like(loss_ref)

        dsh_ref[0] += jnp.sum(a, axis=0, keepdims=True)
        dsc_ref[0] += jnp.sum(bb, axis=0, keepdims=True)
        dg_ref[...] += jnp.sum(cc, axis=0, keepdims=True)
        dgate_ref[0] += jnp.sum(dx * yf, axis=0, keepdims=True)
        tok = jnp.mean(err * err, axis=-1, keepdims=True)
        loss_ref[...] += 0.5 * jnp.sum(tok, axis=0, keepdims=True)

    vec = jax.ShapeDtypeStruct((bl, 1, d), F32)
    return pl.pallas_call(
        kern, name="final_fwd_bwd", grid=(bl, s // ts),
        in_specs=_act_specs(ts, d, 2) + [_vec_spec(d), _par_spec(d), _vec_spec(d), _vec_spec(d)] + _act_specs(ts, d, 1),
        out_specs=_act_specs(ts, d, 2) + [_vec_spec(d), _vec_spec(d), _vec_spec(d), _par_spec(d),
                                          pl.BlockSpec((1, 1), lambda b, t: (0, 0))],
        out_shape=[jax.ShapeDtypeStruct((bl, s, d), F32), jax.ShapeDtypeStruct((bl, s, d), BF16), vec, vec, vec,
                   jax.ShapeDtypeStruct((1, d), F32), jax.ShapeDtypeStruct((1, 1), F32)],
        compiler_params=_cp(2),
    )(x1, yf, gate2, g, shf, scf, tgt)


def _rnn_gates(xr, cw, cb, wa, ba, wi, bi, lam):
    kw = cw.shape[0]
    xc = cb
    for k in range(kw):
        xc = xc + _shift_down(xr, kw - 1 - k) * cw[k:k + 1, :]
    r = _sigmoid(_dot(xc, wa, _NN) + ba)
    i = _sigmoid(_dot(xc, wi, _NN) + bi)
    sp = _softplus(-lam)
    log_a = -LRU_C * r * sp
    a = jnp.exp(log_a)
    mult = jnp.sqrt(-_expm1(2.0 * log_a, a * a))
    return xc, r, i, sp, a, mult


def _segment_scan(a_s, u_s, h_s, p_s, reverse):
    s, c = a_s.shape
    seg = s // SCAN_SEGMENTS

    unroll = math.gcd(seg, 8)

    def steps(n, carry):
        h, p = carry
        for j in range(unroll):
            t = n * unroll + j
            t = (seg - 1 - t) if reverse else t
            av = a_s[pl.ds(t, SCAN_SEGMENTS, stride=seg), :]
            uv = u_s[pl.ds(t, SCAN_SEGMENTS, stride=seg), :]
            h = av * h + uv
            p = p * av
            h_s[pl.ds(t, SCAN_SEGMENTS, stride=seg), :] = h
            p_s[pl.ds(t, SCAN_SEGMENTS, stride=seg), :] = p
        return h, p

    lax.fori_loop(0, seg // unroll, steps, (jnp.zeros((SCAN_SEGMENTS, c), F32), jnp.ones((SCAN_SEGMENTS, c), F32)))
    carry = jnp.zeros((1, c), F32)
    order = range(SCAN_SEGMENTS - 1, -1, -1) if reverse else range(SCAN_SEGMENTS)
    for j in order:
        rows = pl.ds(j * seg, seg)
        fixed = h_s[rows, :] + p_s[rows, :] * carry
        h_s[rows, :] = fixed
        carry = fixed[0:1, :] if reverse else fixed[seg - 1:seg, :]


def _rnn_specs(s, rb, nb):
    act = lambda off: pl.BlockSpec((1, s, rb), lambda b, n, off=off: (b, 0, off + n))
    par = pl.BlockSpec((1, rb), lambda b, n: (0, n))
    wsp = pl.BlockSpec((1, rb, rb), lambda b, n: (n, 0, 0))
    return act, par, wsp


def _rnn_fwd(zr3, cw, cb, wa, ba, wi, bi, lam):
    bl, s, two = zr3.shape
    nb, rb, _ = wa.shape
    dr = nb * rb
    kw = cw.shape[0]
    act, par, wsp = _rnn_specs(s, rb, nb)

    def kern(xr_ref, gr_ref, cw_ref, cb_ref, wa_ref, ba_ref, wi_ref, bi_ref, lam_ref,
             y_ref, h_ref, xc_ref, r_ref, i_ref, a_ref, mult_ref, a_s, u_s, h_s, p_s):
        xc, r, i, sp, a, mult = _rnn_gates(xr_ref[0], cw_ref[...], cb_ref[...], wa_ref[0], ba_ref[...],
                                           wi_ref[0], bi_ref[...], lam_ref[...])
        for ref, val in ((xc_ref, xc), (r_ref, r), (i_ref, i), (a_ref, a), (mult_ref, mult)):
            ref[0] = val
        a_s[...] = a
        u_s[...] = mult * (i * xc)
        _segment_scan(a_s, u_s, h_s, p_s, reverse=False)
        h = h_s[...]
        h_ref[0] = h
        y_ref[0] = (_gelu(gr_ref[0]) * h).astype(BF16)

    kept = jax.ShapeDtypeStruct((bl, s, dr), F32)
    return pl.pallas_call(
        kern, name="rnn_fwd", grid=(bl, nb),
        in_specs=[act(0), act(nb), pl.BlockSpec((kw, rb), lambda b, n: (0, n)), par, wsp, par, wsp, par, par],
        out_specs=[act(0)] * 7,
        out_shape=[jax.ShapeDtypeStruct((bl, s, dr), BF16)] + [kept] * 6,
        scratch_shapes=[pltpu.VMEM((s, rb), F32)] * 4,
        compiler_params=_cp(2),
    )(zr3, zr3, cw, cb, wa, ba, wi, bi, lam)


def _rnn_bwd(zr3, kept, dy3, cw, wa, wi, lam):
    bl, s, _ = zr3.shape
    nb, rb, _ = wa.shape
    dr = nb * rb
    kw = cw.shape[0]
    act = lambda off: pl.BlockSpec((1, s, rb), lambda n, b, off=off: (b, 0, off + n))
    par = pl.BlockSpec((1, rb), lambda n, b: (0, n))
    wsp = pl.BlockSpec((1, rb, rb), lambda n, b: (n, 0, 0))
    cws = pl.BlockSpec((kw, rb), lambda n, b: (0, n))

    def kern(xr_ref, gr_ref, h_ref, xc_ref, r_ref, i_ref, a_ref, mult_ref, dy_ref, cw_ref, wa_ref, wi_ref, lam_ref,
             dxr_ref, dgr_ref, dcw_ref, dcb_ref, dwa_ref, dba_ref, dwi_ref, dbi_ref, dlam_ref, a_s, u_s, h_s, p_s):
        b = pl.program_id(1)
        xr, cwv, lamv = xr_ref[0], cw_ref[...], lam_ref[...]
        wav, wiv = wa_ref[0], wi_ref[0]
        xc, r, i, a, mult = xc_ref[0], r_ref[0], i_ref[0], a_ref[0], mult_ref[0]
        sp = _softplus(-lamv)
        h, dy = h_ref[0], dy_ref[0]
        ge, dge = _gelu_and_grad(gr_ref[0])
        dgr_ref[0] = (dy * h * dge).astype(BF16)
        a_s[...] = _shift_up(a, 1)
        u_s[...] = dy * ge
        _segment_scan(a_s, u_s, h_s, p_s, reverse=True)
        g = h_s[...]
        da = g * _shift_down(h, 1)
        ix = i * xc
        dlog_a = da * a + (g * ix) * (-(a * a) / mult)
        di = g * mult * xc
        dpa = (dlog_a * (-LRU_C * sp)) * r * (1.0 - r)
        dpi = di * i * (1.0 - i)
        dxc = g * mult * i + _dot(dpa, wav, _NT) + _dot(dpi, wiv, _NT)
        dxr = jnp.zeros_like(dxc)
        dcw_rows = []
        for k in range(kw):
            dxr = dxr + _shift_up(dxc, kw - 1 - k) * cwv[k:k + 1, :]
            dcw_rows.append(jnp.sum(dxc * _shift_down(xr, kw - 1 - k), axis=0, keepdims=True))
        dxr_ref[0] = dxr.astype(BF16)

        @pl.when(b == 0)
        def _():
            for ref in (dcw_ref, dcb_ref, dwa_ref, dba_ref, dwi_ref, dbi_ref, dlam_ref):
                ref[...] = jnp.zeros_like(ref)

        for k in range(kw):
            dcw_ref[k:k + 1, :] += dcw_rows[k]
        dcb_ref[...] += jnp.sum(dxc, axis=0, keepdims=True)
        dwa_ref[0] += _dot(xc, dpa, _TN)
        dwi_ref[0] += _dot(xc, dpi, _TN)
        dba_ref[...] += jnp.sum(dpa, axis=0, keepdims=True)
        dbi_ref[...] += jnp.sum(dpi, axis=0, keepdims=True)
        dsp = jnp.sum(dlog_a * (-LRU_C * r), axis=0, keepdims=True)
        dlam_ref[...] += dsp * (-_sigmoid(-lamv))

    vec = jax.ShapeDtypeStruct((1, dr), F32)
    wsh = jax.ShapeDtypeStruct((nb, rb, rb), F32)
    return pl.pallas_call(
        kern, name="rnn_bwd", grid=(nb, bl),
        in_specs=[act(0), act(nb)] + [act(0)] * 7 + [cws, wsp, wsp, par],
        out_specs=[act(0), act(0), cws, par, wsp, par, wsp, par, par],
        out_shape=[jax.ShapeDtypeStruct((bl, s, dr), BF16), jax.ShapeDtypeStruct((bl, s, dr), BF16),
                   jax.ShapeDtypeStruct((kw, dr), F32), vec, wsh, vec, wsh, vec, vec],
        scratch_shapes=[pltpu.VMEM((s, rb), F32)] * 4,
        compiler_params=_cp(2),
    )(zr3, zr3, *kept, dy3, cw, wa, wi, lam)


def _tri(n, upper):
    r = lax.broadcasted_iota(jnp.int32, (n, n), 0)
    c = lax.broadcasted_iota(jnp.int32, (n, n), 1)
    return jnp.where((c >= r) if upper else (c <= r), 1.0, 0.0).astype(F32)


def _fgate_fwd(zf3, bf):
    bl, s, w = zf3.shape
    ch = _tile(s, 256, SUBLANES)

    def kern(z_ref, b_ref, f_ref):
        tri = _tri(ch, upper=False)
        carry = jnp.zeros((1, w), F32)
        for j in range(s // ch):
            rows = pl.ds(j * ch, ch)
            lf = -_softplus(-(z_ref[0, rows, :] + b_ref[...]))
            out = jnp.dot(tri, lf, precision=lax.Precision.HIGHEST, preferred_element_type=F32) + carry
            f_ref[0, rows, :] = out
            carry = out[ch - 1:ch, :]

    return pl.pallas_call(
        kern, name="fgate_fwd", grid=(bl,),
        in_specs=[pl.BlockSpec((1, s, w), lambda b: (b, 0, 0)), pl.BlockSpec((1, w), lambda b: (0, 0))],
        out_specs=pl.BlockSpec((1, s, w), lambda b: (b, 0, 0)),
        out_shape=jax.ShapeDtypeStruct((bl, s, w), F32),
        compiler_params=_cp(1),
    )(zf3, bf)


def _fgate_bwd(dfk3, dfq3, zf3, bf):
    bl, s, w = zf3.shape
    ch = _tile(s, 256, SUBLANES)

    def kern(dfk_ref, dfq_ref, z_ref, b_ref, dz_ref, db_ref):
        b = pl.program_id(0)
        tri = _tri(ch, upper=True)
        carry = jnp.zeros((1, w), F32)
        dbsum = jnp.zeros((1, w), F32)
        for j in range(s // ch - 1, -1, -1):
            rows = pl.ds(j * ch, ch)
            df = dfk_ref[0, rows, :] + dfq_ref[0, rows, :]
            dlf = jnp.dot(tri, df, precision=lax.Precision.HIGHEST, preferred_element_type=F32) + carry
            carry = dlf[0:1, :]
            dz = dlf * _sigmoid(-(z_ref[0, rows, :] + b_ref[...]))
            dz_ref[0, rows, :] = dz.astype(BF16)
            dbsum = dbsum + jnp.sum(dz, axis=0, keepdims=True)

        @pl.when(b == 0)
        def _():
            db_ref[...] = jnp.zeros_like(db_ref)

        db_ref[...] += dbsum

    return pl.pallas_call(
        kern, name="fgate_bwd", grid=(bl,),
        in_specs=[pl.BlockSpec((1, s, w), lambda b: (b, 0, 0))] * 3 + [pl.BlockSpec((1, w), lambda b: (0, 0))],
        out_specs=[pl.BlockSpec((1, s, w), lambda b: (b, 0, 0)), pl.BlockSpec((1, w), lambda b: (0, 0))],
        out_shape=[jax.ShapeDtypeStruct((bl, s, w), BF16), jax.ShapeDtypeStruct((1, w), F32)],
        compiler_params=_cp(1),
    )(dfk3, dfq3, zf3, bf)


def _lanes(col, width):
    return col if width == LANES else jnp.concatenate([col] * (width // LANES), axis=1)


def _causal(sc, row0, col0, transposed):
    r = lax.broadcasted_iota(jnp.int32, sc.shape, 0) + row0
    c = lax.broadcasted_iota(jnp.int32, sc.shape, 1) + col0
    return jnp.where((c >= r) if transposed else (r >= c), sc, NEG_BIG)


def _attn_fwd(q3, kv3, fcol, frow, nh):
    bl, s, da = q3.shape
    dh = da // nh
    tq = _tile(s, 512, LANES)
    nq = s // tq

    def kern(iq_tab, ik_tab, q_ref, k_ref, v_ref, fk_ref, fq_ref, o_ref, lse_ref, m_s, l_s, acc):
        iq, ik = iq_tab[pl.program_id(2)], ik_tab[pl.program_id(2)]

        @pl.when(ik == 0)
        def _():
            m_s[...] = jnp.full_like(m_s, NEG_BIG)
            l_s[...] = jnp.zeros_like(l_s)
            acc[...] = jnp.zeros_like(acc)

        def block(masked):
            st = _dot(k_ref[0], q_ref[0], _NT) - _lanes(fk_ref[0], tq) + fq_ref[0]
            if masked:
                st = _causal(st, ik * tq, iq * tq, True)
            m_old = m_s[...]
            m_new = jnp.maximum(m_old, jnp.max(st, axis=0, keepdims=True))
            alpha = jnp.exp(m_old - m_new)
            pt = jnp.exp(st - m_new)
            l_s[...] = alpha * l_s[...] + jnp.sum(pt, axis=0, keepdims=True)
            acc[...] = alpha * acc[...] + _dot(v_ref[0], pt, _TN)
            m_s[...] = m_new

        pl.when(ik < iq)(functools.partial(block, False))

        @pl.when(ik == iq)
        def _():
            block(True)
            l = l_s[...]
            o_ref[0] = (acc[...] / l).T
            lse_ref[0] = m_s[...] + jnp.log(l)

    pairs = [(i, j) for i in range(nq) for j in range(i + 1)]
    iq_tab, ik_tab = (jnp.asarray(col, jnp.int32) for col in zip(*pairs))
    qmap = lambda b, h, p, iqt, ikt: (b, iqt[p], h)
    kmap = lambda off: (lambda b, h, p, iqt, ikt: (b, ikt[p], off + h))
    return pl.pallas_call(
        kern, name="attn_fwd",
        grid_spec=pltpu.PrefetchScalarGridSpec(
            num_scalar_prefetch=2, grid=(bl, nh, len(pairs)),
            in_specs=[pl.BlockSpec((1, tq, dh), qmap), pl.BlockSpec((1, tq, dh), kmap(0)), pl.BlockSpec((1, tq, dh), kmap(nh)),
                      pl.BlockSpec((1, tq, LANES), lambda b, h, p, iqt, ikt: (b * nh + h, ikt[p], 0)),
                      pl.BlockSpec((1, 1, tq), lambda b, h, p, iqt, ikt: (b * nh + h, 0, iqt[p]))],
            out_specs=[pl.BlockSpec((1, tq, dh), qmap),
                       pl.BlockSpec((1, 1, tq), lambda b, h, p, iqt, ikt: (b * nh + h, 0, iqt[p]))],
            scratch_shapes=[pltpu.VMEM((1, tq), F32), pltpu.VMEM((1, tq), F32), pltpu.VMEM((dh, tq), F32)]),
        out_shape=[jax.ShapeDtypeStruct((bl, s, da), F32), jax.ShapeDtypeStruct((bl * nh, 1, s), F32)],
        compiler_params=_cp(3),
    )(iq_tab, ik_tab, q3, kv3, kv3, fcol, frow)


def _attn_bwd(q3, kv3, do3, o3, lse_row, fcol, frow, nh, scale):
    bl, s, da = q3.shape
    dh = da // nh
    tk = _tile(s, 512, LANES)
    nk = s // tk

    pairs = [(j, i) for j in range(nk) for i in range(j, nk)]

    def kern(ik_tab, iq_tab, q_ref, k_ref, v_ref, do_ref, o_ref, lse_ref, fk_ref, fq_ref, dq_ref, dk_ref, dv_ref, dfk_ref,
             dfq_ref, dq_acc, dk_acc, dv_acc, dfq_acc, delta_s):
        step = pl.program_id(2)
        ik, iq = ik_tab[step], iq_tab[step]
        qrow = pl.ds(iq, 1)

        @pl.when(step == 0)
        def _():
            dq_acc[...] = jnp.zeros_like(dq_acc)
            dfq_acc[...] = jnp.zeros_like(dfq_acc)

        @pl.when(ik == 0)
        def _():
            prod = do_ref[0] * o_ref[0]
            rows = lax.dot_general(jnp.ones((SUBLANES, dh), F32), prod, (_NT, ((), ())),
                                   precision=lax.Precision.HIGHEST, preferred_element_type=F32)
            delta_s[qrow, :] = rows[0:1, :]

        @pl.when(iq == ik)
        def _():
            dk_acc[...] = jnp.zeros_like(dk_acc)
            dv_acc[...] = jnp.zeros_like(dv_acc)

        def block(masked):
            q = q_ref[0]
            st = _dot(k_ref[0], q, _NT) - _lanes(fk_ref[0], tk) + fq_ref[0]
            if masked:
                st = _causal(st, ik * tk, iq * tk, True)
            pt = jnp.exp(st - lse_ref[0])
            dv_acc[...] += _dot(pt, do_ref[0], _NN)
            dpt = _dot(v_ref[0], do_ref[0], _NT)
            dst = (pt * (dpt - delta_s[qrow, :])).astype(BF16)
            q_ones = jnp.concatenate([q, jnp.ones_like(q)], axis=1)
            dk_acc[...] += _dot(dst, q_ones, _NN)
            qrows = pl.ds(pl.multiple_of(iq * tk, tk), tk)
            dq_acc[qrows, :] += _dot(dst, k_ref[0], _TN)
            dfq_acc[qrow, :] += jnp.sum(dst.astype(F32), axis=0, keepdims=True)

        pl.when(iq > ik)(functools.partial(block, False))
        pl.when(iq == ik)(functools.partial(block, True))

        @pl.when(iq == nk - 1)
        def _():
            ext = dk_acc[...]
            dk_ref[0] = ext[:, :dh].astype(BF16)
            dfk_ref[0] = -ext[:, dh:]
            dv_ref[0] = dv_acc[...].astype(BF16)

        @pl.when(step == len(pairs) - 1)
        def _():
            dq_ref[0] = (dq_acc[...] * scale).astype(BF16)
            dfq_ref[0] = dfq_acc[...]

    ik_tab, iq_tab = (jnp.asarray(col, jnp.int32) for col in zip(*pairs))
    qmap = lambda b, h, p, ikt, iqt: (b, iqt[p], h)
    omap = lambda b, h, p, ikt, iqt: (b, jnp.where(ikt[p] == 0, iqt[p], 0), h)
    rmap = lambda b, h, p, ikt, iqt: (b * nh + h, 0, iqt[p])
    kmap = lambda off: (lambda b, h, p, ikt, iqt: (b, ikt[p], off + h))
    bmap = lambda b, h, p, ikt, iqt: (b * nh + h, ikt[p], 0)
    return pl.pallas_call(
        kern, name="attn_bwd",
        grid_spec=pltpu.PrefetchScalarGridSpec(
            num_scalar_prefetch=2, grid=(bl, nh, len(pairs)),
            in_specs=[pl.BlockSpec((1, tk, dh), qmap), pl.BlockSpec((1, tk, dh), kmap(0)), pl.BlockSpec((1, tk, dh), kmap(nh)),
                      pl.BlockSpec((1, tk, dh), qmap), pl.BlockSpec((1, tk, dh), omap), pl.BlockSpec((1, 1, tk), rmap),
                      pl.BlockSpec((1, tk, LANES), bmap), pl.BlockSpec((1, 1, tk), rmap)],
            out_specs=[pl.BlockSpec((1, s, dh), lambda b, h, p, ikt, iqt: (b, 0, h)),
                       pl.BlockSpec((1, tk, dh), kmap(0)), pl.BlockSpec((1, tk, dh), kmap(0)),
                       pl.BlockSpec((1, tk, LANES), bmap),
                       pl.BlockSpec((1, nk, tk), lambda b, h, p, ikt, iqt: (b * nh + h, 0, 0))],
            scratch_shapes=[pltpu.VMEM((s, dh), F32), pltpu.VMEM((tk, 2 * dh), F32), pltpu.VMEM((tk, dh), F32),
                            pltpu.VMEM((nk, tk), F32), pltpu.VMEM((nk, tk), F32)]),
        out_shape=[jax.ShapeDtypeStruct((bl, s, da), BF16), jax.ShapeDtypeStruct((bl, s, da), BF16),
                   jax.ShapeDtypeStruct((bl, s, da), BF16), jax.ShapeDtypeStruct((bl * nh, s, LANES), F32),
                   jax.ShapeDtypeStruct((bl * nh, nk, tk), F32)],
        compiler_params=_cp(3),
    )(ik_tab, iq_tab, q3, kv3, kv3, do3, o3, lse_row, fcol, frow)


def _merge_fwd(mg3, pr3, pa3):
    bl, s, d = pr3.shape
    ts = _tile(s, 512, SUBLANES)
    half = lambda j: pl.BlockSpec((1, ts, d), lambda b, t, j=j: (b, t, j))

    def kern(mr_ref, ma_ref, pr_ref, pa_ref, o_ref):
        o_ref[0] = (_sigmoid(mr_ref[0]) * pr_ref[0] + _sigmoid(ma_ref[0]) * pa_ref[0]).astype(BF16)

    return pl.pallas_call(
        kern, name="merge_fwd", grid=(bl, s // ts),
        in_specs=[half(0), half(1)] + _act_specs(ts, d, 2), out_specs=_act_specs(ts, d, 1)[0],
        out_shape=jax.ShapeDtypeStruct((bl, s, d), BF16), compiler_params=_cp(2),
    )(mg3, mg3, pr3, pa3)


def _merge_bwd(dm3, mg3, pr3, pa3):
    bl, s, d = pr3.shape
    ts = _tile(s, 512, SUBLANES)
    half = lambda j: pl.BlockSpec((1, ts, d), lambda b, t, j=j: (b, t, j))

    def kern(dm_ref, mr_ref, ma_ref, pr_ref, pa_ref, dpr_ref, dpa_ref, dmr_ref, dma_ref):
        dm = dm_ref[0]
        gr, ga = _sigmoid(mr_ref[0]), _sigmoid(ma_ref[0])
        dpr_ref[0] = (gr * dm).astype(BF16)
        dpa_ref[0] = (ga * dm).astype(BF16)
        dmr_ref[0] = (dm * pr_ref[0] * gr * (1.0 - gr)).astype(BF16)
        dma_ref[0] = (dm * pa_ref[0] * ga * (1.0 - ga)).astype(BF16)

    return pl.pallas_call(
        kern, name="merge_bwd", grid=(bl, s // ts),
        in_specs=_act_specs(ts, d, 1) + [half(0), half(1)] + _act_specs(ts, d, 2), out_specs=_act_specs(ts, d, 4),
        out_shape=[jax.ShapeDtypeStruct((bl, s, d), BF16)] * 4, compiler_params=_cp(2),
    )(dm3, mg3, mg3, pr3, pa3)


def _ffn_conv(gf, cw, cb):
    kw = cw.shape[0]
    y = cb
    for k in range(kw):
        y = y + _shift_down(gf, kw - 1 - k) * cw[k:k + 1, :]
    return y


def _ffn_act_fwd(up3, cw, cb):
    bl, s, two = up3.shape
    dff = two // 2
    kw = cw.shape[0]
    tc = _tile(dff, 512, LANES)
    nc = dff // tc

    def kern(gf_ref, uf_ref, cw_ref, cb_ref, o_ref):
        o_ref[0] = (_gelu(_ffn_conv(gf_ref[0], cw_ref[...], cb_ref[...])) * uf_ref[0]).astype(BF16)

    act = lambda off: pl.BlockSpec((1, s, tc), lambda b, j, off=off: (b, 0, off + j))
    return pl.pallas_call(
        kern, name="ffn_act_fwd", grid=(bl, nc),
        in_specs=[act(0), act(nc), pl.BlockSpec((kw, tc), lambda b, j: (0, j)), pl.BlockSpec((1, tc), lambda b, j: (0, j))],
        out_specs=act(0), out_shape=jax.ShapeDtypeStruct((bl, s, dff), BF16), compiler_params=_cp(2),
    )(up3, up3, cw, cb)


def _ffn_act_bwd(up3, dact3, cw, cb):
    bl, s, two = up3.shape
    dff = two // 2
    kw = cw.shape[0]
    tc = _tile(dff, 256, LANES)
    nc = dff // tc

    def kern(gf_ref, uf_ref, da_ref, cw_ref, cb_ref, dgf_ref, duf_ref, dcw_ref, dcb_ref):
        b = pl.program_id(1)
        gf, cwv, da = gf_ref[0], cw_ref[...], da_ref[0]
        ge, dge = _gelu_and_grad(_ffn_conv(gf, cwv, cb_ref[...]))
        duf_ref[0] = (da * ge).astype(BF16)
        dgc = da * uf_ref[0] * dge
        dgf = jnp.zeros_like(dgc)
        rows = []
        for k in range(kw):
            dgf = dgf + _shift_up(dgc, kw - 1 - k) * cwv[k:k + 1, :]
            rows.append(jnp.sum(dgc * _shift_down(gf, kw - 1 - k), axis=0, keepdims=True))
        dgf_ref[0] = dgf.astype(BF16)

        @pl.when(b == 0)
        def _():
            dcw_ref[...] = jnp.zeros_like(dcw_ref)
            dcb_ref[...] = jnp.zeros_like(dcb_ref)

        for k in range(kw):
            dcw_ref[k:k + 1, :] += rows[k]
        dcb_ref[...] += jnp.sum(dgc, axis=0, keepdims=True)

    act = lambda off: pl.BlockSpec((1, s, tc), lambda j, b, off=off: (b, 0, off + j))
    cws = pl.BlockSpec((kw, tc), lambda j, b: (0, j))
    cbs = pl.BlockSpec((1, tc), lambda j, b: (0, j))
    return pl.pallas_call(
        kern, name="ffn_act_bwd", grid=(nc, bl),
        in_specs=[act(0), act(nc), act(0), cws, cbs], out_specs=[act(0), act(0), cws, cbs],
        out_shape=[jax.ShapeDtypeStruct((bl, s, dff), BF16), jax.ShapeDtypeStruct((bl, s, dff), BF16),
                   jax.ShapeDtypeStruct((kw, dff), F32), jax.ShapeDtypeStruct((1, dff), F32)],
        compiler_params=_cp(2),
    )(up3, up3, dact3, cw, cb)


_HBM = pl.BlockSpec(memory_space=pltpu.HBM)


def _place():
    x, y, c = lax.axis_index("x"), lax.axis_index("y"), lax.axis_index("c")
    chips = dict(me=2 * x + y, nx=2 * (1 - x) + y, ny=2 * x + (1 - y), diag=2 * (1 - x) + (1 - y))
    peers = dict(nx=(1 - x, y, c), ny=(x, 1 - y, c), sib=(x, y, 1 - c))
    return c, chips, peers


def _remote(src, dst, sems, k, to):
    return pltpu.make_async_remote_copy(src_ref=src, dst_ref=dst, send_sem=sems[0].at[k], recv_sem=sems[1].at[k],
                                        device_id=to, device_id_type=MESH)


RS_STEPS = 2


def _piece(q, idx, n=1):
    start = idx * q
    if not isinstance(start, int):
        start = pl.multiple_of(start, SUBLANES)
    return pl.ds(start, n * q)


def _all_gather_chips(xs, name):
    nt = len(xs)
    per = 9

    def body(*refs):
        x_refs, o_refs = refs[:nt], refs[nt:2 * nt]
        send_sems, recv_sems = refs[2 * nt:]
        c, chip, peer = _place()
        sems = (send_sems, recv_sems)
        me, nx, ny, dg = chip["me"], chip["nx"], chip["ny"], chip["diag"]
        sends = []

        def arrive(k, dst):
            _remote(dst, dst, sems, k, peer["sib"]).wait_recv()

        def pass_on(k, blk, to):
            cp = _remote(blk, blk, sems, k, peer[to])
            cp.start()
            sends.append(cp)

        for t in range(nt):
            q = xs[t].shape[0] // 4
            half = _piece(q, 2 * c, 2)
            for k, to in ((0, "nx"), (1, "ny")):
                cp = _remote(x_refs[t].at[half], o_refs[t].at[me, half], sems, per * t + k, peer[to])
                cp.start()
                sends.append(cp)
            cp = _remote(x_refs[t], o_refs[t].at[me], sems, per * t + 8, peer["sib"])
            cp.start()
            sends.append(cp)
        for t in range(nt):
            q, o, k0 = xs[t].shape[0] // 4, o_refs[t], per * t
            half, sub0, sub1 = _piece(q, 2 * c, 2), _piece(q, 2 * c), _piece(q, 2 * c + 1)
            arrive(k0 + 0, o.at[nx, half])
            pass_on(k0 + 2, o.at[nx, sub0], "ny")
            pass_on(k0 + 4, o.at[nx, half], "sib")
            arrive(k0 + 1, o.at[ny, half])
            pass_on(k0 + 3, o.at[ny, sub1], "nx")
            pass_on(k0 + 5, o.at[ny, half], "sib")
            arrive(k0 + 2, o.at[dg, sub0])
            pass_on(k0 + 6, o.at[dg, sub0], "sib")
            arrive(k0 + 3, o.at[dg, sub1])
            pass_on(k0 + 7, o.at[dg, sub1], "sib")
        for t in range(nt):
            q, o, k0 = xs[t].shape[0] // 4, o_refs[t], per * t
            arrive(k0 + 4, o.at[nx, _piece(q, 2 * (1 - c), 2)])
            arrive(k0 + 5, o.at[ny, _piece(q, 2 * (1 - c), 2)])
            arrive(k0 + 6, o.at[dg, _piece(q, 2 * (1 - c))])
            arrive(k0 + 7, o.at[dg, _piece(q, 2 * (1 - c) + 1)])
            arrive(k0 + 8, o.at[me])
        for cp in sends:
            cp.wait_send()

    return pl.pallas_call(
        body, name=name, in_specs=[_HBM] * nt, out_specs=[_HBM] * nt,
        out_shape=[jax.ShapeDtypeStruct((N_CHIPS,) + x.shape, x.dtype) for x in xs],
        scratch_shapes=[pltpu.SemaphoreType.DMA((per * nt,)), pltpu.SemaphoreType.DMA((per * nt,))],
    )(*xs)


def _all_gather_devices(xs, name):
    nt = len(xs)
    per = 7

    def body(*refs):
        x_refs, o_refs = refs[:nt], refs[nt:2 * nt]
        send_sems, recv_sems, local_sems = refs[2 * nt:]
        x, y, c = lax.axis_index("x"), lax.axis_index("y"), lax.axis_index("c")
        sems = (send_sems, recv_sems)
        sib = (x, y, 1 - c)
        chips = [(1 - x, y), (x, 1 - y), (1 - x, 1 - y)]
        slot = lambda px, py, pc: 4 * px + 2 * py + pc
        me = slot(x, y, c)
        sends, copies = [], []

        def arrive(k, dst):
            _remote(dst, dst, sems, k, sib).wait_recv()

        for t in range(nt):
            cp = pltpu.make_async_copy(x_refs[t], o_refs[t].at[me], local_sems.at[t])
            cp.start()
            copies.append(cp)
            for k, to in enumerate([sib] + [(*chip, c) for chip in chips]):
                cp = _remote(x_refs[t], o_refs[t].at[me], sems, per * t + k, to)
                cp.start()
                sends.append(cp)
        for t in range(nt):
            for j, chip in enumerate(chips):
                blk = o_refs[t].at[slot(*chip, c)]
                arrive(per * t + 1 + j, blk)
                cp = _remote(blk, blk, sems, per * t + 4 + j, sib)
                cp.start()
                sends.append(cp)
        for t in range(nt):
            arrive(per * t, o_refs[t].at[slot(x, y, 1 - c)])
            for j, chip in enumerate(chips):
                arrive(per * t + 4 + j, o_refs[t].at[slot(*chip, 1 - c)])
        for cp in sends:
            cp.wait_send()
        for cp in copies:
            cp.wait()

    return pl.pallas_call(
        body, name=name, in_specs=[_HBM] * nt, out_specs=[_HBM] * nt,
        out_shape=[jax.ShapeDtypeStruct((2 * N_CHIPS,) + a.shape, a.dtype) for a in xs],
        scratch_shapes=[pltpu.SemaphoreType.DMA((per * nt,)), pltpu.SemaphoreType.DMA((per * nt,)),
                        pltpu.SemaphoreType.DMA((nt,))],
    )(*xs)


def _exchange(name, xs, out_shapes, plan):
    nt = len(xs)

    def body(*refs):
        x_refs, o_refs = refs[:nt], refs[nt:2 * nt]
        send_sems, recv_sems = refs[2 * nt:]
        c, chip, peer = _place()
        cps = []
        for t in range(nt):
            for src, dst, to in plan(c, chip, x_refs[t], o_refs[t], xs[t].shape):
                cps.append(_remote(src, dst, (send_sems, recv_sems), len(cps), peer[to]))
        for cp in cps:
            cp.start()
        for cp in cps:
            cp.wait()

    n_copies = nt * len(plan(0, dict(me=0, nx=2, ny=1, diag=3), None, None, xs[0].shape, count_only=True))
    return pl.pallas_call(
        body, name=name, in_specs=[_HBM] * nt, out_specs=[_HBM] * nt,
        out_shape=[jax.ShapeDtypeStruct(s, x.dtype) for s, x in zip(out_shapes, xs)],
        scratch_shapes=[pltpu.SemaphoreType.DMA((n_copies,)), pltpu.SemaphoreType.DMA((n_copies,))],
    )(*xs)


def _plan_sibling(c, chip, g, out, shape, count_only=False):
    if count_only:
        return [None] * N_CHIPS
    q = shape[1] // 4
    return [(g.at[j, _piece(q, 2 * (1 - c), 2)], out.at[j], "sib") for j in range(N_CHIPS)]


def _plan_first(c, chip, p, out, shape, count_only=False):
    if count_only:
        return [None] * 4
    q = shape[1] // 2
    return [(p.at[chip["nx"], _piece(q, 0)], out.at[0], "nx"), (p.at[chip["diag"], _piece(q, 0)], out.at[1], "nx"),
            (p.at[chip["ny"], _piece(q, 1)], out.at[2], "ny"), (p.at[chip["diag"], _piece(q, 1)], out.at[3], "ny")]


def _plan_second(c, chip, p, out, shape, count_only=False):
    if count_only:
        return [None] * 2
    return [(p.at[1], out.at[0], "ny"), (p.at[3], out.at[1], "nx")]


def _rs_last(ps):
    nt = len(ps)

    def body(*refs):
        p_refs, o_refs = refs[:nt], refs[nt:2 * nt]
        send_sems, recv_sems = refs[2 * nt:]
        c, _, peer = _place()
        sems = (send_sems, recv_sems)
        cps = []
        for t in range(nt):
            q = ps[t].shape[0] // 4
            mine = _piece(q, 2 * c, 2)
            cps.append(_remote(p_refs[t].at[mine], o_refs[t].at[mine], sems, t, peer["sib"]))
            cps[-1].start()
        for t in range(nt):
            q = ps[t].shape[0] // 4
            theirs = _piece(q, 2 * (1 - c), 2)
            cps[t].wait_send()
            _remote(p_refs[t].at[theirs], o_refs[t].at[theirs], sems, t, peer["sib"]).wait_recv()

    return pl.pallas_call(
        body, name="rs_last", in_specs=[_HBM] * nt, out_specs=[_HBM] * nt,
        out_shape=[jax.ShapeDtypeStruct(p.shape, F32) for p in ps],
        input_output_aliases={t: t for t in range(nt)},
        scratch_shapes=[pltpu.SemaphoreType.DMA((nt,)), pltpu.SemaphoreType.DMA((nt,))],
    )(*ps)


def _add_stage(name, grid, a_list, b_list, a_map, b_map, tbs, out_shapes, out_map, out_dtype, prefetch=None):
    nt = len(a_list)
    lead = lambda shape: (None,) * (len(shape) - 2)

    def kern(*refs):
        refs = refs[(1 if prefetch is not None else 0):]
        for t in range(nt):
            refs[2 * nt + t][...] = (refs[t][...].astype(F32) + refs[nt + t][...].astype(F32)).astype(out_dtype)

    in_specs = [pl.BlockSpec(lead(a.shape) + (tb, a.shape[-1]), a_map) for a, tb in zip(a_list, tbs)]
    in_specs += [pl.BlockSpec(lead(b.shape) + (tb, b.shape[-1]), b_map) for b, tb in zip(b_list, tbs)]
    out_specs = [pl.BlockSpec(lead(s) + (tb, s[-1]), out_map) for s, tb in zip(out_shapes, tbs)]
    out_shape = [jax.ShapeDtypeStruct(s, out_dtype) for s in out_shapes]
    if prefetch is None:
        return pl.pallas_call(kern, name=name, grid=grid, in_specs=in_specs, out_specs=out_specs, out_shape=out_shape,
                              compiler_params=_cp(len(grid)))(*a_list, *b_list)
    return pl.pallas_call(
        kern, name=name,
        grid_spec=pltpu.PrefetchScalarGridSpec(num_scalar_prefetch=1, grid=grid, in_specs=in_specs, out_specs=out_specs),
        out_shape=out_shape, compiler_params=_cp(len(grid)))(prefetch, *a_list, *b_list)


def _reduce_scatter_chips(gs):
    x, y, c = lax.axis_index("x"), lax.axis_index("y"), lax.axis_index("c")
    me, nx, ny = 2 * x + y, 2 * (1 - x) + y, 2 * x + (1 - y)
    st = RS_STEPS
    unit = 4 * st * 2 * SUBLANES
    rows = [g.shape[1] for g in gs]
    gs = [jnp.pad(g, ((0, 0), (0, -g.shape[1] % unit), (0, 0))) for g in gs]
    qs = [g.shape[1] // 4 for g in gs]
    tbs = [q // st for q in qs]
    cols = [g.shape[2] for g in gs]
    core = jnp.reshape(c, (1,)).astype(jnp.int32)

    got = _exchange("rs_sibling", gs, [(N_CHIPS, 2 * q, cc) for q, cc in zip(qs, cols)], _plan_sibling)
    p0 = _add_stage("rs_add_sibling", (N_CHIPS, 2, st), gs, got,
                    lambda j, h, s, c_ref: (j, (2 * c_ref[0] + h) * st + s, 0), lambda j, h, s, c_ref: (j, h * st + s, 0),
                    tbs, [(N_CHIPS, 2 * q, cc) for q, cc in zip(qs, cols)], lambda j, h, s, c_ref: (j, h * st + s, 0),
                    BF16, prefetch=core)
    got = _exchange("rs_first", p0, [(4, q, cc) for q, cc in zip(qs, cols)], _plan_first)
    p1 = _add_stage("rs_add_first", (4, st), p0, got,
                    lambda k, s, i_ref: (i_ref[k], (k // 2) * st + s, 0), lambda k, s, i_ref: (k, s, 0),
                    tbs, [(4, q, cc) for q, cc in zip(qs, cols)], lambda k, s, i_ref: (k, s, 0),
                    BF16, prefetch=jnp.stack([me, ny, me, nx]).astype(jnp.int32))
    got = _exchange("rs_second", p1, [(2, q, cc) for q, cc in zip(qs, cols)], _plan_second)
    p2 = _add_stage("rs_add_second", (2, st), p1, got, lambda h, s, c_ref: (2 * h, s, 0), lambda h, s, c_ref: (h, s, 0),
                    tbs, [(4 * q, cc) for q, cc in zip(qs, cols)], lambda h, s, c_ref: ((2 * c_ref[0] + h) * st + s, 0),
                    F32, prefetch=core)
    return [out[:r] for out, r in zip(_rs_last(p2), rows)]


def _adamw(g, w, m, v, name):
    rows, cc = g.shape
    tr = _tile(rows, max(SUBLANES, (1 << 18) // cc), SUBLANES)
    k1 = 1.0 - ADAM_B1 ** ADAM_STEP
    k2 = 1.0 - ADAM_B2 ** ADAM_STEP

    def kern(g_ref, w_ref, m_ref, v_ref, d_ref, nm_ref, nv_ref):
        gv = g_ref[...]
        nm = ADAM_B1 * m_ref[...] + (1.0 - ADAM_B1) * gv
        nv = ADAM_B2 * v_ref[...] + (1.0 - ADAM_B2) * (gv * gv)
        nm_ref[...] = nm
        nv_ref[...] = nv
        d_ref[...] = -ADAM_LR * ((nm / k1) / (jnp.sqrt(nv / k2) + ADAM_EPS) + ADAM_WD * w_ref[...])

    spec = pl.BlockSpec((tr, cc), lambda t: (t, 0))
    return pl.pallas_call(
        kern, name=name, grid=(rows // tr,), in_specs=[spec] * 4, out_specs=[spec] * 3,
        out_shape=[jax.ShapeDtypeStruct((rows, cc), F32)] * 3, compiler_params=_cp(1),
    )(g, w, m, v)


def _flat_pad(parts, total):
    flat = jnp.concatenate([p.reshape(-1) for p in parts])
    return jnp.pad(flat, (0, total - flat.shape[0]))


def _split_flat(flat, shapes):
    out, pos = [], 0
    for shp in shapes:
        size = math.prod(shp)
        out.append(flat[pos:pos + size].reshape(shp))
        pos += size
    return out


def _cols_of_chunks(chunks, lo, hi):
    width = chunks.shape[2]
    parts = []
    for j in range(chunks.shape[0]):
        a, b = max(lo, j * width), min(hi, (j + 1) * width)
        if a < b:
            parts.append(chunks[j, :, a - j * width:b - j * width])
    return parts[0] if len(parts) == 1 else jnp.concatenate(parts, axis=1)


def _chunks_of_cols(segments, n_chunks):
    total = sum(s.shape[1] for s in segments)
    width = total // n_chunks
    chunks = []
    for j in range(n_chunks):
        lo, hi, pos, parts = j * width, (j + 1) * width, 0, []
        for s in segments:
            a, b = max(lo, pos), min(hi, pos + s.shape[1])
            if a < b:
                parts.append(s[:, a - pos:b - pos])
            pos += s.shape[1]
        chunks.append(parts[0] if len(parts) == 1 else jnp.concatenate(parts, axis=1))
    return jnp.stack(chunks)


_WEIGHTS = ['w_ada', 'b_ada', 'g_norm1', 'w_in', 'w_rnn_conv', 'b_rnn_conv', 'w_lru_a', 'b_lru_a', 'w_lru_i', 'b_lru_i',
            'lru_lambda', 'b_fgate', 'w_proj_rnn', 'w_proj_attn', 'w_out', 'g_norm2', 'w_ffn_up', 'w_ffn_conv',
            'b_ffn_conv', 'w_ffn_down', 'w_ada_final', 'b_ada_final', 'g_final']
_MATMUL = ['w_in', 'w_proj_rnn', 'w_proj_attn', 'w_out', 'w_ffn_up', 'w_ffn_down']
_ADA = ['w_ada', 'w_ada_final']
_ADA_BIAS = ['b_ada', 'b_ada_final']
_CONV = ['w_rnn_conv', 'w_ffn_conv']
_REPLICATED = [n for n in _WEIGHTS if n not in _MATMUL + _ADA + _ADA_BIAS + _CONV]


def kernel(x, c, w_ada, b_ada, g_norm1, w_in, w_rnn_conv, b_rnn_conv, w_lru_a, b_lru_a, w_lru_i, b_lru_i, lru_lambda, b_fgate, w_proj_rnn, w_proj_attn, w_out, g_norm2, w_ffn_up, w_ffn_conv, b_ffn_conv, w_ffn_down, w_ada_final, b_ada_final, g_final, loss_target, m_w_ada, m_b_ada, m_g_norm1, m_w_in, m_w_rnn_conv, m_b_rnn_conv, m_w_lru_a, m_b_lru_a, m_w_lru_i, m_b_lru_i, m_lru_lambda, m_b_fgate, m_w_proj_rnn, m_w_proj_attn, m_w_out, m_g_norm2, m_w_ffn_up, m_w_ffn_conv, m_b_ffn_conv, m_w_ffn_down, m_w_ada_final, m_b_ada_final, m_g_final, v_w_ada, v_b_ada, v_g_norm1, v_w_in, v_w_rnn_conv, v_b_rnn_conv, v_w_lru_a, v_b_lru_a, v_w_lru_i, v_b_lru_i, v_lru_lambda, v_b_fgate, v_w_proj_rnn, v_w_proj_attn, v_w_out, v_g_norm2, v_w_ffn_up, v_w_ffn_conv, v_b_ffn_conv, v_w_ffn_down, v_w_ada_final, v_b_ada_final, v_g_final):
    args = locals()
    shape_of = {n: args[n].shape for n in _WEIGHTS}

    def view(a):
        if a.ndim >= 3:
            return a[0]
        return a[None, :] if a.ndim == 1 else a

    w2 = {n: view(args[n]) for n in _WEIGHTS}
    m2 = {n: args['m_' + n].reshape(w2[n].shape) for n in _WEIGHTS}
    v2 = {n: args['v_' + n].reshape(w2[n].shape) for n in _WEIGHTS}

    bl, s, d = x.shape
    t = bl * s
    nh = b_fgate.shape[-1]
    nb, rb = w_lru_a.shape[1], w_lru_a.shape[2]
    dr = nb * rb
    da = w2['w_proj_attn'].shape[0] * N_CHIPS
    dh = da // nh
    dff = w2['w_ffn_conv'].shape[1] * N_CHIPS
    scale = dh ** -0.5
    chip = 2 * lax.axis_index("x") + lax.axis_index("y")
    dev = 2 * chip + lax.axis_index("c")

    names = list(_MATMUL)
    n_conv = sum(w2[n].size for n in _CONV)
    rows_conv = -(-n_conv // (FLAT_COLS * 32)) * 32
    conv_local = _flat_pad([w2[n] for n in _CONV], rows_conv * FLAT_COLS).reshape(rows_conv, FLAT_COLS)
    *weights_all, conv_all = _all_gather_chips([w2[n].astype(BF16) for n in names] + [conv_local], "ag_weights")
    gathered = dict(zip(names, weights_all))
    conv_all = conv_all.reshape(N_CHIPS, -1)
    conv_full, pos = {}, 0
    for n in _CONV:
        r, n4 = w2[n].shape
        blocks = conv_all[:, pos:pos + r * n4].reshape(N_CHIPS, r, n4)
        conv_full[n] = jnp.concatenate([blocks[j] for j in range(N_CHIPS)], axis=1)
        pos += r * n4
    rowmajor = lambda n: gathered[n].reshape(-1, gathered[n].shape[2])
    w_proj_rnn_f, w_proj_attn_f, w_out_f, w_ffn_down_f = (rowmajor(n) for n in ('w_proj_rnn', 'w_proj_attn', 'w_out', 'w_ffn_down'))
    up_chunk = w2['w_ffn_up'].shape[1]

    o_q, o_k, o_fl = 2 * dr, 2 * dr + da, 2 * dr + 3 * da
    o_mg = o_fl + nh
    g_in_w = gathered['w_in']
    w_rnn, w_q = _cols_of_chunks(g_in_w, 0, o_q), _cols_of_chunks(g_in_w, o_q, o_k)
    w_kv, w_mg = _cols_of_chunks(g_in_w, o_k, o_fl), _cols_of_chunks(g_in_w, o_mg, o_mg + 2 * d)
    w_fl = jnp.pad(_cols_of_chunks(g_in_w, o_fl, o_mg), ((0, 0), (0, LANES - nh)))
    w_rest = jnp.concatenate([w_q, w_kv, w_mg], axis=1)
    bf_pad = jnp.pad(w2['b_fgate'], ((0, 0), (0, LANES - nh)))

    nd = 2 * N_CHIPS
    c_act = _silu_pad(_all_gather_devices([c], "ag_cond")[0].reshape(nd * bl, d), nd * bl)
    my_cols = lambda a, n: lax.dynamic_slice_in_dim(a, chip * w2[n].shape[1], w2[n].shape[1], axis=1)
    mod_cols = [_mm(c_act, w2[n].astype(BF16), "nn", name=n + "_fwd", bias=my_cols(w2[b], n)) for n, b in zip(_ADA, _ADA_BIAS)]
    my_rows = lambda g: lax.dynamic_slice_in_dim(g, dev * bl, bl, axis=1).transpose(1, 0, 2).reshape(bl, -1)
    mod, modf = (my_rows(g) for g in _all_gather_chips(mod_cols, "ag_mod"))
    sh1, sc1, gt1, sh2, sc2, gt2 = [mod[:, i * d:(i + 1) * d].reshape(bl, 1, d) for i in range(6)]
    shf, scf = modf[:, :d].reshape(bl, 1, d), modf[:, d:].reshape(bl, 1, d)

    h1 = _norm_mod_fwd(x, w2['g_norm1'], sh1, sc1)
    h1f = h1.reshape(t, d)
    zr = _mm(h1f, w_rnn, "nn", name="in_rnn", tn=dr).reshape(bl, s, 2 * dr)
    q3 = _mm(h1f, w_q, "nn", name="in_q", out_dtype=BF16, scale=scale).reshape(bl, s, da)
    kv3 = _mm(h1f, w_kv, "nn", name="in_kv", out_dtype=BF16, tn=2 * da).reshape(bl, s, 2 * da)
    mg3 = _mm(h1f, w_mg, "nn", name="in_mg", tn=2 * d).reshape(bl, s, 2 * d)
    zf3 = _mm(h1f, w_fl, "nn", name="in_fl").reshape(bl, s, LANES)

    lru = (conv_full['w_rnn_conv'], w2['b_rnn_conv'], w2['w_lru_a'], w2['b_lru_a'], w2['w_lru_i'], w2['b_lru_i'], w2['lru_lambda'])
    y_rnn, *rnn_kept = _rnn_fwd(zr, *lru)

    f3 = _fgate_fwd(zf3, bf_pad)
    f_heads = f3[:, :, :nh].transpose(0, 2, 1).reshape(bl * nh, s)
    fcol = jnp.broadcast_to(f_heads[:, :, None], (bl * nh, s, LANES))
    frow = f_heads.reshape(bl * nh, 1, s)
    o3, lse_row = _attn_fwd(q3, kv3, fcol, frow, nh)

    pr3 = _mm(y_rnn.reshape(t, dr), w_proj_rnn_f, "nn", name="proj_rnn").reshape(bl, s, d)
    pa3 = _mm(o3.reshape(t, da), w_proj_attn_f, "nn", name="proj_attn").reshape(bl, s, d)
    merged = _merge_fwd(mg3, pr3, pa3)
    mo3 = _mm(merged.reshape(t, d), w_out_f, "nn", name="mix_out").reshape(bl, s, d)
    x1, h2 = _resid_norm_fwd(x, mo3, gt1, w2['g_norm2'], sh2, sc2)
    h2f = h2.reshape(t, d)
    up3 = _mm(h2f, gathered['w_ffn_up'], "nn", name="ffn_up", b_chunk=up_chunk, tn=up_chunk).reshape(bl, s, 2 * dff)
    act3 = _ffn_act_fwd(up3, conv_full['w_ffn_conv'], w2['b_ffn_conv'])
    yf3 = _mm(act3.reshape(t, dff), w_ffn_down_f, "nn", name="ffn_down", tm=2048).reshape(bl, s, d)

    dx2, dyf, dgt2, dshf, dscf, dg_final, loss_part = _final_fwd_bwd(x1, yf3, gt2, w2['g_final'], shf, scf, loss_target)
    loss = lax.psum(loss_part[0, 0], ("x", "y", "c"))

    dyf_f = dyf.reshape(t, d)
    g_ffn_down = _mm(act3.reshape(t, dff), dyf_f, "tn", name="dw_ffn_down")
    dact3 = _mm(dyf_f, w_ffn_down_f, "nt", name="d_ffn_act", tn=dff // 2).reshape(bl, s, dff)
    dgf, duf, g_ffn_conv, g_b_ffn_conv = _ffn_act_bwd(up3, dact3, conv_full['w_ffn_conv'], w2['b_ffn_conv'])
    dgf_f, duf_f = dgf.reshape(t, dff), duf.reshape(t, dff)
    g_ffn_up = jnp.concatenate([_mm(h2f, dgf_f, "tn", name="dw_ffn_up_gate", out_chunk=up_chunk, tn=up_chunk),
                                _mm(h2f, duf_f, "tn", name="dw_ffn_up_value", out_chunk=up_chunk, tn=up_chunk)], axis=0)
    dh2 = _mm([dgf_f, duf_f], gathered['w_ffn_up'], "nt", name="d_h2", b_chunk=up_chunk).reshape(bl, s, d)
    dmo, dgt1, dx1, dsh2, dsc2, dg_norm2 = _norm_mod_bwd(dh2, x1, dx2, w2['g_norm2'], sc2, "norm2_bwd", gated=(mo3, gt1))

    dmo_f = dmo.reshape(t, d)
    g_out = _mm(merged.reshape(t, d), dmo_f, "tn", name="dw_out")
    dm3 = _mm(dmo_f, w_out_f, "nt", name="d_merged").reshape(bl, s, d)
    dpr, dpa, dmr, dma = _merge_bwd(dm3, mg3, pr3, pa3)
    g_proj_rnn = _mm(y_rnn.reshape(t, dr), dpr.reshape(t, d), "tn", name="dw_proj_rnn", tm=dr)
    g_proj_attn = _mm(o3.reshape(t, da), dpa.reshape(t, d), "tn", name="dw_proj_attn")
    dyr3 = _mm(dpr.reshape(t, d), w_proj_rnn_f, "nt", name="d_y_rnn", tn=dr).reshape(bl, s, dr)
    do3 = _mm(dpa.reshape(t, d), w_proj_attn_f, "nt", name="d_y_attn").reshape(bl, s, da)

    dq3, dk3, dv3, dfk, dfq = _attn_bwd(q3, kv3, do3, o3, lse_row, fcol, frow, nh, scale)
    heads_last = lambda a: jnp.pad(a.reshape(bl, nh, s).transpose(0, 2, 1), ((0, 0), (0, 0), (0, LANES - nh)))
    dzf3, g_bf = _fgate_bwd(heads_last(dfk[:, :, 0]), heads_last(dfq), zf3, bf_pad)

    dxr, dgr, g_rnn_conv, g_b_rnn_conv, g_lru_a, g_b_lru_a, g_lru_i, g_b_lru_i, g_lam = _rnn_bwd(
        zr, rnn_kept, dyr3, lru[0], lru[2], lru[4], lru[6])

    dz = [a.reshape(t, -1) for a in (dxr, dgr, dq3, dk3, dv3, dmr, dma)]
    dzf_f = dzf3.reshape(t, LANES)
    seg_names = ("xr", "gr", "q", "k", "v", "mr", "ma")
    g_seg = [_mm(h1f, a, "tn", name="dw_in_" + n, tn=dr) for n, a in zip(seg_names, dz)]
    g_in_fl = _mm(h1f, dzf_f, "tn", name="dw_in_fl")[:, :nh]
    g_in = _chunks_of_cols(g_seg[:5] + [g_in_fl] + g_seg[5:], N_CHIPS)
    dh1 = _mm(dzf_f, w_fl, "nt", name="d_h1_fl")
    dh1 = _mm(dz[:2], w_rnn, "nt", name="d_h1_rnn", add=dh1)
    dh1 = _mm(dz[2:], w_rest, "nt", name="d_h1", add=dh1).reshape(bl, s, d)
    grad_x, dsh1, dsc1, dg_norm1 = _norm_mod_bwd(dh1, x, dx1, w2['g_norm1'], sc1, "norm1_bwd")

    dmods = [jnp.concatenate([dsh1, dsc1, dgt1, dsh2, dsc2, dgt2], axis=-1).reshape(bl, -1),
             jnp.concatenate([dshf, dscf], axis=-1).reshape(bl, -1)]
    dmods = [g.reshape(nd * bl, -1) for g in _all_gather_devices(dmods, "ag_dmod")]
    grad = {n: _mm(c_act, my_cols(g, n), "tn", name="dw_" + n) for n, g in zip(_ADA, dmods)}
    grad.update({b: _rowsum(g, "d" + b) for b, g in zip(_ADA_BIAS, dmods)})

    rowchunks = lambda g: g.reshape(N_CHIPS, g.shape[0] // N_CHIPS, g.shape[1])
    full = dict(w_in=g_in, w_proj_rnn=rowchunks(g_proj_rnn), w_proj_attn=rowchunks(g_proj_attn),
                w_out=rowchunks(g_out), w_ffn_up=g_ffn_up, w_ffn_down=rowchunks(g_ffn_down))
    small = dict(g_norm1=dg_norm1, w_rnn_conv=g_rnn_conv, b_rnn_conv=g_b_rnn_conv, w_lru_a=g_lru_a,
                 b_lru_a=g_b_lru_a, w_lru_i=g_lru_i, b_lru_i=g_b_lru_i, lru_lambda=g_lam, b_fgate=g_bf[:, :nh],
                 g_norm2=dg_norm2, w_ffn_conv=g_ffn_conv, b_ffn_conv=g_b_ffn_conv, g_final=dg_final)

    small_names = _REPLICATED + _CONV
    n_small = sum(small[n].size for n in small_names)
    rows_q = -(-n_small // (N_CHIPS * FLAT_COLS * 32 * RS_STEPS)) * 32 * RS_STEPS
    small_flat = _flat_pad([small[n] for n in small_names], N_CHIPS * rows_q * FLAT_COLS).reshape(N_CHIPS, rows_q, FLAT_COLS)
    reduced = _reduce_scatter_chips([full[n] for n in names] + [small_flat])
    grad.update(zip(names, reduced[:-1]))
    small_all = _all_gather_chips([reduced[-1]], "ag_small_grads")[0].reshape(-1)
    grad.update(zip(small_names, _split_flat(small_all, [small[n].shape for n in small_names])))
    for n in _CONV:
        n4 = w2[n].shape[1]
        grad[n] = lax.dynamic_slice_in_dim(grad[n], chip * n4, n4, axis=1)

    delta_w, new_m, new_v = {}, {}, {}
    for n in names + _ADA:
        delta_w[n], new_m[n], new_v[n] = _adamw(grad[n], w2[n], m2[n], v2[n], "adamw_" + n)
    small_names = small_names + _ADA_BIAS
    rows_small = -(-sum(w2[n].size for n in small_names) // (FLAT_COLS * SUBLANES)) * SUBLANES
    flat_small = lambda src: _flat_pad([src[n] for n in small_names], rows_small * FLAT_COLS).reshape(rows_small, FLAT_COLS)
    small_out = _adamw(flat_small(grad), flat_small(w2), flat_small(m2), flat_small(v2), "adamw_small")
    for dst, flat in zip((delta_w, new_m, new_v), small_out):
        dst.update(zip(small_names, _split_flat(flat.reshape(-1), [w2[n].shape for n in small_names])))

    out = [loss, grad_x]
    for src in (grad, delta_w, new_m, new_v):
        out += [src[n].reshape(shape_of[n]) for n in _WEIGHTS]
    return tuple(out)
```
